```python
import jax, jax.numpy as jnp
from jax import lax
import numpy as np

D_MODEL = 1024
BATCH = 16
SEQ = 2048
DEPTH = 1

CHUNK = 64
D_MIX = D_MODEL
ATTN_WIDTH = D_MIX // 2
HGRN_WIDTH = D_MIX - ATTN_WIDTH
ATTN_HEAD_DIM = 64
ATTN_HEADS = ATTN_WIDTH // ATTN_HEAD_DIM
HGRN_HEAD_DIM = 128
HGRN_HEADS = HGRN_WIDTH // HGRN_HEAD_DIM
LEFT_CHUNKS = 8
BAND = (LEFT_CHUNKS + 1) * CHUNK
REL_CLIP = 128
N_REL = 2 * REL_CLIP + 1
D_FF = 2816
RMS_EPS = 1e-6
PROJ_SIZES = (ATTN_WIDTH, ATTN_WIDTH, ATTN_WIDTH, HGRN_WIDTH, HGRN_WIDTH, HGRN_WIDTH, HGRN_WIDTH)
PROJ_COLS = sum(PROJ_SIZES)
PROJ_SPLITS = tuple(int(v) for v in np.cumsum(PROJ_SIZES)[:-1])

kernel_name = "hybrid_chunk_attn_hgrn2_macaron"


def rms_norm(x, g):
    xf = x.astype(jnp.float32)
    y = xf * lax.rsqrt(jnp.mean(xf * xf, axis=-1, keepdims=True) + RMS_EPS)
    return (y * g.astype(jnp.float32)).astype(x.dtype)


def swiglu_ffn(h, w_gate, w_up, w_down):
    return (jax.nn.silu(h @ w_gate) * (h @ w_up)) @ w_down


def chunked_band_attention(q, k, v, rel_bias):
    B, S, H, Dh = q.shape
    n_chunks = S // CHUNK
    pad = LEFT_CHUNKS * CHUNK
    k_pad = jnp.pad(k, ((0, 0), (pad, 0), (0, 0), (0, 0)))
    v_pad = jnp.pad(v, ((0, 0), (pad, 0), (0, 0), (0, 0)))
    t_pos = jnp.arange(CHUNK)[:, None] + pad
    s_pos = jnp.arange(BAND)[None, :]
    rel_idx = jnp.clip(t_pos - s_pos, -REL_CLIP, REL_CLIP) + REL_CLIP
    bias = rel_bias.astype(jnp.float32)[:, rel_idx]
    scale = Dh ** -0.5
    q_chunks = q.reshape(B, n_chunks, CHUNK, H, Dh).transpose(1, 0, 2, 3, 4)

    def one_chunk(args):
        c, qc = args
        start = c * CHUNK
        kb = lax.dynamic_slice_in_dim(k_pad, start, BAND, axis=1)
        vb = lax.dynamic_slice_in_dim(v_pad, start, BAND, axis=1)
        scores = jnp.einsum('bqhd,bkhd->bhqk', qc, kb,
                            preferred_element_type=jnp.float32) * scale + bias
        key_pos = start - pad + jnp.arange(BAND)
        scores = jnp.where((key_pos >= 0)[None, None, None, :], scores, -jnp.inf)
        p = jax.nn.softmax(scores, axis=-1)
        return jnp.einsum('bhqk,bkhd->bqhd', p.astype(vb.dtype), vb)

    out = lax.map(one_chunk, (jnp.arange(n_chunks), q_chunks))
    return out.transpose(1, 0, 2, 3, 4).reshape(B, S, H * Dh)


def hgrn2_chunkwise(q, k, v, log_f):
    B, S, H, Dk = q.shape
    Dv = v.shape[-1]
    n_chunks = S // CHUNK

    def to_chunks(a):
        return a.reshape(B, n_chunks, CHUNK, H, a.shape[-1]).transpose(1, 0, 3, 2, 4)

    qc, kc, vc, gc = to_chunks(q), to_chunks(k), to_chunks(v), to_chunks(log_f)
    causal = jnp.tril(jnp.ones((CHUNK, CHUNK), dtype=bool))[None, None, :, :, None]

    def step(state, inp):
        qi, ki, vi, gi = inp
        b = jnp.cumsum(gi, axis=2)
        diff = b[:, :, :, None, :] - b[:, :, None, :, :]
        decay = jnp.exp(jnp.where(causal, diff, -jnp.inf))
        scores = jnp.einsum('bhtk,bhsk,bhtsk->bhts', qi, ki, decay)
        o = (jnp.einsum('bhts,bhsv->bhtv', scores, vi)
             + jnp.einsum('bhtk,bhkv->bhtv', qi * jnp.exp(b), state))
        b_last = b[:, :, -1:, :]
        new_state = (state * jnp.exp(b_last[:, :, 0, :])[..., None]
                     + jnp.einsum('bhsk,bhsv->bhkv', ki * jnp.exp(b_last - b), vi))
        return new_state, o

    s0 = jnp.zeros((B, H, Dk, Dv), jnp.float32)
    _, o = lax.scan(step, s0, (qc, kc, vc, gc))
    return o.transpose(1, 0, 3, 2, 4).reshape(B, S, H, Dv)


def _fwd_setup_inputs(seed: int = 0) -> dict:
    key = jax.random.key(seed)
    ks = jax.random.split(key, 20)
    f32 = jnp.float32

    def w(k, shape, fan_in):
        return jax.random.normal(k, shape, f32) * (fan_in ** -0.5)

    def gain(k, shape):
        return 1.0 + 0.05 * jax.random.normal(k, shape, f32)

    return {
        "x": jax.random.normal(ks[0], (BATCH, SEQ, D_MODEL), f32),
        "ffn1_norm_g": gain(ks[1], (DEPTH, D_MODEL)),
        "ffn1_w_gate": w(ks[2], (DEPTH, D_MODEL, D_FF), D_MODEL),
        "ffn1_w_up": w(ks[3], (DEPTH, D_MODEL, D_FF), D_MODEL),
        "ffn1_w_down": w(ks[4], (DEPTH, D_FF, D_MODEL), D_FF),
        "mix_norm_g": gain(ks[5], (DEPTH, D_MODEL)),
        "w_in": w(ks[6], (DEPTH, D_MODEL, PROJ_COLS), D_MODEL),
        "attn_q_norm_g": gain(ks[7], (DEPTH, ATTN_HEAD_DIM)),
        "attn_k_norm_g": gain(ks[8], (DEPTH, ATTN_HEAD_DIM)),
        "attn_rel_bias": 0.1 * jax.random.normal(ks[9], (DEPTH, ATTN_HEADS, N_REL), f32),
        "hgrn_lower_bounds": 0.1 * jax.random.normal(ks[10], (DEPTH + 1, HGRN_WIDTH), f32),
        "hgrn_out_norm_g": gain(ks[11], (DEPTH, HGRN_HEAD_DIM)),
        "w_out": w(ks[12], (DEPTH, D_MIX, D_MODEL), D_MIX),
        "ffn2_norm_g": gain(ks[13], (DEPTH, D_MODEL)),
        "ffn2_w_gate": w(ks[14], (DEPTH, D_MODEL, D_FF), D_MODEL),
        "ffn2_w_up": w(ks[15], (DEPTH, D_MODEL, D_FF), D_MODEL),
        "ffn2_w_down": w(ks[16], (DEPTH, D_FF, D_MODEL), D_FF),
    }


def _fwd_reference(x, ffn1_norm_g, ffn1_w_gate, ffn1_w_up, ffn1_w_down, mix_norm_g, w_in,
              attn_q_norm_g, attn_k_norm_g, attn_rel_bias, hgrn_lower_bounds, hgrn_out_norm_g,
              w_out, ffn2_norm_g, ffn2_w_gate, ffn2_w_up, ffn2_w_down):
    B, S, _ = x.shape
    lb_all = jnp.cumsum(jax.nn.softmax(hgrn_lower_bounds.astype(jnp.float32), axis=0), axis=0)

    for l in range(DEPTH):
        h = rms_norm(x, ffn1_norm_g[l])
        x = x + 0.5 * swiglu_ffn(h, ffn1_w_gate[l], ffn1_w_up[l], ffn1_w_down[l])

        h = rms_norm(x, mix_norm_g[l])
        proj = h @ w_in[l]
        aq, ak, av, hq, hf, hi, hg = jnp.split(proj, PROJ_SPLITS, axis=-1)

        aq = rms_norm(aq.reshape(B, S, ATTN_HEADS, ATTN_HEAD_DIM), attn_q_norm_g[l])
        ak = rms_norm(ak.reshape(B, S, ATTN_HEADS, ATTN_HEAD_DIM), attn_k_norm_g[l])
        av = av.reshape(B, S, ATTN_HEADS, ATTN_HEAD_DIM)
        attn_out = chunked_band_attention(aq, ak, av, attn_rel_bias[l])

        lb = lb_all[l]
        f = lb + (1.0 - lb) * jax.nn.sigmoid(hf.astype(jnp.float32))
        shp = (B, S, HGRN_HEADS, HGRN_HEAD_DIM)
        rq = jax.nn.silu(hq.astype(jnp.float32)).reshape(shp)
        rk = (1.0 - f).reshape(shp)
        rv = hi.astype(jnp.float32).reshape(shp)
        ro = hgrn2_chunkwise(rq, rk, rv, jnp.log(f).reshape(shp))
        ro = rms_norm(ro, hgrn_out_norm_g[l]) * jax.nn.silu(hg.astype(jnp.float32).reshape(shp))
        hgrn_out = ro.reshape(B, S, HGRN_WIDTH).astype(x.dtype)

        x = x + jnp.concatenate([attn_out, hgrn_out], axis=-1) @ w_out[l]

        h = rms_norm(x, ffn2_norm_g[l])
        x = x + 0.5 * swiglu_ffn(h, ffn2_w_gate[l], ffn2_w_up[l], ffn2_w_down[l])
    return x


import jax as _jax
import jax.numpy as _jnp

TWIN_FORMAT = 'train_step'
FWD_PARAMS = ['x', 'ffn1_norm_g', 'ffn1_w_gate', 'ffn1_w_up', 'ffn1_w_down', 'mix_norm_g', 'w_in', 'attn_q_norm_g', 'attn_k_norm_g', 'attn_rel_bias', 'hgrn_lower_bounds', 'hgrn_out_norm_g', 'w_out', 'ffn2_norm_g', 'ffn2_w_gate', 'ffn2_w_up', 'ffn2_w_down']
TWIN_WEIGHTS = ['ffn1_norm_g', 'ffn1_w_gate', 'ffn1_w_up', 'ffn1_w_down', 'mix_norm_g', 'w_in', 'attn_q_norm_g', 'attn_k_norm_g', 'attn_rel_bias', 'hgrn_lower_bounds', 'hgrn_out_norm_g', 'w_out', 'ffn2_norm_g', 'ffn2_w_gate', 'ffn2_w_up', 'ffn2_w_down']
TWIN_DIFF_INPUT = 'x'
TWIN_INPUTS = ['x', 'ffn1_norm_g', 'ffn1_w_gate', 'ffn1_w_up', 'ffn1_w_down', 'mix_norm_g', 'w_in', 'attn_q_norm_g', 'attn_k_norm_g', 'attn_rel_bias', 'hgrn_lower_bounds', 'hgrn_out_norm_g', 'w_out', 'ffn2_norm_g', 'ffn2_w_gate', 'ffn2_w_up', 'ffn2_w_down', 'loss_target', 'm_ffn1_norm_g', 'm_ffn1_w_gate', 'm_ffn1_w_up', 'm_ffn1_w_down', 'm_mix_norm_g', 'm_w_in', 'm_attn_q_norm_g', 'm_attn_k_norm_g', 'm_attn_rel_bias', 'm_hgrn_lower_bounds', 'm_hgrn_out_norm_g', 'm_w_out', 'm_ffn2_norm_g', 'm_ffn2_w_gate', 'm_ffn2_w_up', 'm_ffn2_w_down', 'v_ffn1_norm_g', 'v_ffn1_w_gate', 'v_ffn1_w_up', 'v_ffn1_w_down', 'v_mix_norm_g', 'v_w_in', 'v_attn_q_norm_g', 'v_attn_k_norm_g', 'v_attn_rel_bias', 'v_hgrn_lower_bounds', 'v_hgrn_out_norm_g', 'v_w_out', 'v_ffn2_norm_g', 'v_ffn2_w_gate', 'v_ffn2_w_up', 'v_ffn2_w_down']
TWIN_OUTPUTS = ['loss', 'grad_x', 'grad_ffn1_norm_g', 'grad_ffn1_w_gate', 'grad_ffn1_w_up', 'grad_ffn1_w_down', 'grad_mix_norm_g', 'grad_w_in', 'grad_attn_q_norm_g', 'grad_attn_k_norm_g', 'grad_attn_rel_bias', 'grad_hgrn_lower_bounds', 'grad_hgrn_out_norm_g', 'grad_w_out', 'grad_ffn2_norm_g', 'grad_ffn2_w_gate', 'grad_ffn2_w_up', 'grad_ffn2_w_down', 'delta_ffn1_norm_g', 'delta_ffn1_w_gate', 'delta_ffn1_w_up', 'delta_ffn1_w_down', 'delta_mix_norm_g', 'delta_w_in', 'delta_attn_q_norm_g', 'delta_attn_k_norm_g', 'delta_attn_rel_bias', 'delta_hgrn_lower_bounds', 'delta_hgrn_out_norm_g', 'delta_w_out', 'delta_ffn2_norm_g', 'delta_ffn2_w_gate', 'delta_ffn2_w_up', 'delta_ffn2_w_down', 'new_m_ffn1_norm_g', 'new_m_ffn1_w_gate', 'new_m_ffn1_w_up', 'new_m_ffn1_w_down', 'new_m_mix_norm_g', 'new_m_w_in', 'new_m_attn_q_norm_g', 'new_m_attn_k_norm_g', 'new_m_attn_rel_bias', 'new_m_hgrn_lower_bounds', 'new_m_hgrn_out_norm_g', 'new_m_w_out', 'new_m_ffn2_norm_g', 'new_m_ffn2_w_gate', 'new_m_ffn2_w_up', 'new_m_ffn2_w_down', 'new_v_ffn1_norm_g', 'new_v_ffn1_w_gate', 'new_v_ffn1_w_up', 'new_v_ffn1_w_down', 'new_v_mix_norm_g', 'new_v_w_in', 'new_v_attn_q_norm_g', 'new_v_attn_k_norm_g', 'new_v_attn_rel_bias', 'new_v_hgrn_lower_bounds', 'new_v_hgrn_out_norm_g', 'new_v_w_out', 'new_v_ffn2_norm_g', 'new_v_ffn2_w_gate', 'new_v_ffn2_w_up', 'new_v_ffn2_w_down']
TWIN_LEAF_KINDS = {'loss': 'loss', 'grad_x': 'grad_x', 'grad_ffn1_norm_g': 'grad_w', 'grad_ffn1_w_gate': 'grad_w', 'grad_ffn1_w_up': 'grad_w', 'grad_ffn1_w_down': 'grad_w', 'grad_mix_norm_g': 'grad_w', 'grad_w_in': 'grad_w', 'grad_attn_q_norm_g': 'grad_w', 'grad_attn_k_norm_g': 'grad_w', 'grad_attn_rel_bias': 'grad_w', 'grad_hgrn_lower_bounds': 'grad_w', 'grad_hgrn_out_norm_g': 'grad_w', 'grad_w_out': 'grad_w', 'grad_ffn2_norm_g': 'grad_w', 'grad_ffn2_w_gate': 'grad_w', 'grad_ffn2_w_up': 'grad_w', 'grad_ffn2_w_down': 'grad_w', 'delta_ffn1_norm_g': 'delta_w', 'delta_ffn1_w_gate': 'delta_w', 'delta_ffn1_w_up': 'delta_w', 'delta_ffn1_w_down': 'delta_w', 'delta_mix_norm_g': 'delta_w', 'delta_w_in': 'delta_w', 'delta_attn_q_norm_g': 'delta_w', 'delta_attn_k_norm_g': 'delta_w', 'delta_attn_rel_bias': 'delta_w', 'delta_hgrn_lower_bounds': 'delta_w', 'delta_hgrn_out_norm_g': 'delta_w', 'delta_w_out': 'delta_w', 'delta_ffn2_norm_g': 'delta_w', 'delta_ffn2_w_gate': 'delta_w', 'delta_ffn2_w_up': 'delta_w', 'delta_ffn2_w_down': 'delta_w', 'new_m_ffn1_norm_g': 'new_m', 'new_m_ffn1_w_gate': 'new_m', 'new_m_ffn1_w_up': 'new_m', 'new_m_ffn1_w_down': 'new_m', 'new_m_mix_norm_g': 'new_m', 'new_m_w_in': 'new_m', 'new_m_attn_q_norm_g': 'new_m', 'new_m_attn_k_norm_g': 'new_m', 'new_m_attn_rel_bias': 'new_m', 'new_m_hgrn_lower_bounds': 'new_m', 'new_m_hgrn_out_norm_g': 'new_m', 'new_m_w_out': 'new_m', 'new_m_ffn2_norm_g': 'new_m', 'new_m_ffn2_w_gate': 'new_m', 'new_m_ffn2_w_up': 'new_m', 'new_m_ffn2_w_down': 'new_m', 'new_v_ffn1_norm_g': 'new_v', 'new_v_ffn1_w_gate': 'new_v', 'new_v_ffn1_w_up': 'new_v', 'new_v_ffn1_w_down': 'new_v', 'new_v_mix_norm_g': 'new_v', 'new_v_w_in': 'new_v', 'new_v_attn_q_norm_g': 'new_v', 'new_v_attn_k_norm_g': 'new_v', 'new_v_attn_rel_bias': 'new_v', 'new_v_hgrn_lower_bounds': 'new_v', 'new_v_hgrn_out_norm_g': 'new_v', 'new_v_w_out': 'new_v', 'new_v_ffn2_norm_g': 'new_v', 'new_v_ffn2_w_gate': 'new_v', 'new_v_ffn2_w_up': 'new_v', 'new_v_ffn2_w_down': 'new_v'}


def _forward(args):
    return _fwd_reference(*[args[k] for k in FWD_PARAMS])


def _output_shape():
    out = _jax.eval_shape(lambda: _forward(_fwd_setup_inputs(0)))
    return out.shape, out.dtype

N_MICROBATCH = 1
ADAM_LR = 0.001
ADAM_B1 = 0.9
ADAM_B2 = 0.999
ADAM_EPS = 1e-08
ADAM_WD = 0.01
ADAM_STEP = 10
PER_EXAMPLE_BATCH_AXIS = {'x': 0, 'loss_target': 0}
SHARED_INPUTS = []
_WEIGHT_DTYPES = {'ffn1_norm_g': _jnp.float32, 'ffn1_w_gate': _jnp.float32, 'ffn1_w_up': _jnp.float32, 'ffn1_w_down': _jnp.float32, 'mix_norm_g': _jnp.float32, 'w_in': _jnp.float32, 'attn_q_norm_g': _jnp.float32, 'attn_k_norm_g': _jnp.float32, 'attn_rel_bias': _jnp.float32, 'hgrn_lower_bounds': _jnp.float32, 'hgrn_out_norm_g': _jnp.float32, 'w_out': _jnp.float32, 'ffn2_norm_g': _jnp.float32, 'ffn2_w_gate': _jnp.float32, 'ffn2_w_up': _jnp.float32, 'ffn2_w_down': _jnp.float32}
MOMENT_SCALE = {'ffn1_norm_g': 6.165473e+00, 'ffn1_w_gate': 7.628701e-02, 'ffn1_w_up': 8.152428e-02, 'ffn1_w_down': 1.333428e-01, 'mix_norm_g': 6.549493e+00, 'w_in': 1.631518e-01, 'attn_q_norm_g': 8.031659e-01, 'attn_k_norm_g': 8.036360e-01, 'attn_rel_bias': 2.050538e-02, 'hgrn_lower_bounds': 2.260931e-02, 'hgrn_out_norm_g': 4.530772e+01, 'w_out': 2.014389e-01, 'ffn2_norm_g': 6.216447e+00, 'ffn2_w_gate': 5.782382e-02, 'ffn2_w_up': 6.802872e-02, 'ffn2_w_down': 1.102180e-01}


def _to_microbatches(a, axis):
    t = _jnp.moveaxis(a, axis, 0)
    t = t.reshape((N_MICROBATCH, t.shape[0] // N_MICROBATCH) + t.shape[1:])
    return _jnp.moveaxis(t, 1, axis + 1)


def setup_inputs(seed: int = 0) -> dict:
    inp = _fwd_setup_inputs(seed)
    key = _jax.random.fold_in(_jax.random.key(seed), 7919)
    shape, _ = _output_shape()
    out = dict(inp)
    out["loss_target"] = _jax.random.normal(_jax.random.fold_in(key, 0), shape, _jnp.float32)
    for i, name in enumerate(TWIN_WEIGHTS):
        w = inp[name].astype(_jnp.float32)
        if MOMENT_SCALE is None:
            s = _jnp.sqrt(_jnp.mean(_jnp.square(w)) + 1e-30)
        else:
            s = MOMENT_SCALE[name]
        km, kv = _jax.random.split(_jax.random.fold_in(key, i + 1))
        out[name] = w
        out["m_" + name] = s * _jax.random.normal(km, w.shape, _jnp.float32)
        out["v_" + name] = (s * s) * _jax.random.uniform(kv, w.shape, _jnp.float32, 0.5, 1.5)
    if N_MICROBATCH > 1:
        for name, axis in PER_EXAMPLE_BATCH_AXIS.items():
            out[name] = _to_microbatches(out[name], axis)
    return {'x': out['x'], 'ffn1_norm_g': out['ffn1_norm_g'], 'ffn1_w_gate': out['ffn1_w_gate'], 'ffn1_w_up': out['ffn1_w_up'], 'ffn1_w_down': out['ffn1_w_down'], 'mix_norm_g': out['mix_norm_g'], 'w_in': out['w_in'], 'attn_q_norm_g': out['attn_q_norm_g'], 'attn_k_norm_g': out['attn_k_norm_g'], 'attn_rel_bias': out['attn_rel_bias'], 'hgrn_lower_bounds': out['hgrn_lower_bounds'], 'hgrn_out_norm_g': out['hgrn_out_norm_g'], 'w_out': out['w_out'], 'ffn2_norm_g': out['ffn2_norm_g'], 'ffn2_w_gate': out['ffn2_w_gate'], 'ffn2_w_up': out['ffn2_w_up'], 'ffn2_w_down': out['ffn2_w_down'], 'loss_target': out['loss_target'], 'm_ffn1_norm_g': out['m_ffn1_norm_g'], 'm_ffn1_w_gate': out['m_ffn1_w_gate'], 'm_ffn1_w_up': out['m_ffn1_w_up'], 'm_ffn1_w_down': out['m_ffn1_w_down'], 'm_mix_norm_g': out['m_mix_norm_g'], 'm_w_in': out['m_w_in'], 'm_attn_q_norm_g': out['m_attn_q_norm_g'], 'm_attn_k_norm_g': out['m_attn_k_norm_g'], 'm_attn_rel_bias': out['m_attn_rel_bias'], 'm_hgrn_lower_bounds': out['m_hgrn_lower_bounds'], 'm_hgrn_out_norm_g': out['m_hgrn_out_norm_g'], 'm_w_out': out['m_w_out'], 'm_ffn2_norm_g': out['m_ffn2_norm_g'], 'm_ffn2_w_gate': out['m_ffn2_w_gate'], 'm_ffn2_w_up': out['m_ffn2_w_up'], 'm_ffn2_w_down': out['m_ffn2_w_down'], 'v_ffn1_norm_g': out['v_ffn1_norm_g'], 'v_ffn1_w_gate': out['v_ffn1_w_gate'], 'v_ffn1_w_up': out['v_ffn1_w_up'], 'v_ffn1_w_down': out['v_ffn1_w_down'], 'v_mix_norm_g': out['v_mix_norm_g'], 'v_w_in': out['v_w_in'], 'v_attn_q_norm_g': out['v_attn_q_norm_g'], 'v_attn_k_norm_g': out['v_attn_k_norm_g'], 'v_attn_rel_bias': out['v_attn_rel_bias'], 'v_hgrn_lower_bounds': out['v_hgrn_lower_bounds'], 'v_hgrn_out_norm_g': out['v_hgrn_out_norm_g'], 'v_w_out': out['v_w_out'], 'v_ffn2_norm_g': out['v_ffn2_norm_g'], 'v_ffn2_w_gate': out['v_ffn2_w_gate'], 'v_ffn2_w_up': out['v_ffn2_w_up'], 'v_ffn2_w_down': out['v_ffn2_w_down']}


def _loss(weights, diff, rest, loss_target):
    with _jax.named_scope("forward"):
        args = {**rest, TWIN_DIFF_INPUT: diff, **{k: w.astype(_WEIGHT_DTYPES[k]) for k, w in weights.items()}}
        y = _forward(args)
    with _jax.named_scope("loss_head"):
        err = _jnp.square(y.astype(_jnp.float32) - loss_target)
        return 0.5 * _jnp.sum(_jnp.mean(err, axis=-1)) if err.ndim else 0.5 * err


def _adamw(w, g, m, v):
    m = ADAM_B1 * m + (1.0 - ADAM_B1) * g
    v = ADAM_B2 * v + (1.0 - ADAM_B2) * _jnp.square(g)
    m_hat = m / (1.0 - ADAM_B1 ** ADAM_STEP)
    v_hat = v / (1.0 - ADAM_B2 ** ADAM_STEP)
    delta = -ADAM_LR * (m_hat / (_jnp.sqrt(v_hat) + ADAM_EPS) + ADAM_WD * w)
    return delta, m, v


def reference(x, ffn1_norm_g, ffn1_w_gate, ffn1_w_up, ffn1_w_down, mix_norm_g, w_in, attn_q_norm_g, attn_k_norm_g, attn_rel_bias, hgrn_lower_bounds, hgrn_out_norm_g, w_out, ffn2_norm_g, ffn2_w_gate, ffn2_w_up, ffn2_w_down, loss_target, m_ffn1_norm_g, m_ffn1_w_gate, m_ffn1_w_up, m_ffn1_w_down, m_mix_norm_g, m_w_in, m_attn_q_norm_g, m_attn_k_norm_g, m_attn_rel_bias, m_hgrn_lower_bounds, m_hgrn_out_norm_g, m_w_out, m_ffn2_norm_g, m_ffn2_w_gate, m_ffn2_w_up, m_ffn2_w_down, v_ffn1_norm_g, v_ffn1_w_gate, v_ffn1_w_up, v_ffn1_w_down, v_mix_norm_g, v_w_in, v_attn_q_norm_g, v_attn_k_norm_g, v_attn_rel_bias, v_hgrn_lower_bounds, v_hgrn_out_norm_g, v_w_out, v_ffn2_norm_g, v_ffn2_w_gate, v_ffn2_w_up, v_ffn2_w_down):
    given = dict(x=x, ffn1_norm_g=ffn1_norm_g, ffn1_w_gate=ffn1_w_gate, ffn1_w_up=ffn1_w_up, ffn1_w_down=ffn1_w_down, mix_norm_g=mix_norm_g, w_in=w_in, attn_q_norm_g=attn_q_norm_g, attn_k_norm_g=attn_k_norm_g, attn_rel_bias=attn_rel_bias, hgrn_lower_bounds=hgrn_lower_bounds, hgrn_out_norm_g=hgrn_out_norm_g, w_out=w_out, ffn2_norm_g=ffn2_norm_g, ffn2_w_gate=ffn2_w_gate, ffn2_w_up=ffn2_w_up, ffn2_w_down=ffn2_w_down, loss_target=loss_target, m_ffn1_norm_g=m_ffn1_norm_g, m_ffn1_w_gate=m_ffn1_w_gate, m_ffn1_w_up=m_ffn1_w_up, m_ffn1_w_down=m_ffn1_w_down, m_mix_norm_g=m_mix_norm_g, m_w_in=m_w_in, m_attn_q_norm_g=m_attn_q_norm_g, m_attn_k_norm_g=m_attn_k_norm_g, m_attn_rel_bias=m_attn_rel_bias, m_hgrn_lower_bounds=m_hgrn_lower_bounds, m_hgrn_out_norm_g=m_hgrn_out_norm_g, m_w_out=m_w_out, m_ffn2_norm_g=m_ffn2_norm_g, m_ffn2_w_gate=m_ffn2_w_gate, m_ffn2_w_up=m_ffn2_w_up, m_ffn2_w_down=m_ffn2_w_down, v_ffn1_norm_g=v_ffn1_norm_g, v_ffn1_w_gate=v_ffn1_w_gate, v_ffn1_w_up=v_ffn1_w_up, v_ffn1_w_down=v_ffn1_w_down, v_mix_norm_g=v_mix_norm_g, v_w_in=v_w_in, v_attn_q_norm_g=v_attn_q_norm_g, v_attn_k_norm_g=v_attn_k_norm_g, v_attn_rel_bias=v_attn_rel_bias, v_hgrn_lower_bounds=v_hgrn_lower_bounds, v_hgrn_out_norm_g=v_hgrn_out_norm_g, v_w_out=v_w_out, v_ffn2_norm_g=v_ffn2_norm_g, v_ffn2_w_gate=v_ffn2_w_gate, v_ffn2_w_up=v_ffn2_w_up, v_ffn2_w_down=v_ffn2_w_down)
    weights = {n: given[n] for n in TWIN_WEIGHTS}
    shared = {n: given[n] for n in SHARED_INPUTS}
    per_example = {n: given[n] for n in ['x']}
    grad_fn = _jax.value_and_grad(_loss, argnums=(0, 1))

    def one_microbatch(ex, loss_target):
        ex = dict(ex)
        diff = ex.pop(TWIN_DIFF_INPUT)
        return grad_fn(weights, diff, {**shared, **ex}, loss_target)

    if N_MICROBATCH == 1:
        loss, (grad_w, grad_x) = one_microbatch(per_example, given["loss_target"])
    else:
        def body(carry, xs):
            loss_sum, grad_sum = carry
            l_k, (gw_k, gx_k) = one_microbatch(xs[0], xs[1])
            with _jax.named_scope("update"):
                return (loss_sum + l_k, _jax.tree.map(_jnp.add, grad_sum, gw_k)), gx_k

        init = (_jnp.zeros((), _jnp.float32), _jax.tree.map(_jnp.zeros_like, weights))
        (loss, grad_w), grad_x = _jax.lax.scan(body, init, (per_example, given["loss_target"]))
    with _jax.named_scope("update"):
        delta_w, new_m, new_v = {}, {}, {}
        for n in TWIN_WEIGHTS:
            delta_w[n], new_m[n], new_v[n] = _adamw(weights[n], grad_w[n], given["m_" + n], given["v_" + n])
    return (loss, grad_x, *[grad_w[n] for n in TWIN_WEIGHTS], *[delta_w[n] for n in TWIN_WEIGHTS],
            *[new_m[n] for n in TWIN_WEIGHTS], *[new_v[n] for n in TWIN_WEIGHTS])
```

```python
import functools

import jax
import jax.numpy as jnp
from jax import lax
from jax.experimental import pallas as pl
from jax.experimental.pallas import tpu as pltpu

F32 = jnp.float32
BF16 = jnp.bfloat16

RMS_EPS = 1e-6
CHUNK = 64
LEFT_CHUNKS = 8
BAND = (LEFT_CHUNKS + 2) * CHUNK
KPAD = BAND - CHUNK
REL_CLIP = 128
N_REL = 2 * REL_CLIP + 1
N_REL_PAD = 384
ATTN_HEADS = 8
ATTN_HEAD_DIM = 64
ATTN_WIDTH = ATTN_HEADS * ATTN_HEAD_DIM
HGRN_HEADS = 4
HGRN_HEAD_DIM = 128
SUB = 16
N_SUB = CHUNK // SUB
N_DEV = 8

ADAM_LR = 0.001
ADAM_B1 = 0.9
ADAM_B2 = 0.999
ADAM_EPS = 1e-08
ADAM_WD = 0.01
ADAM_STEP = 10

VMEM_LIMIT = 56 * 1024 * 1024

NT = (((1,), (1,)), ((), ()))
NN = (((1,), (0,)), ((), ()))


def _params(*sem):
    return pltpu.CompilerParams(dimension_semantics=sem, vmem_limit_bytes=VMEM_LIMIT)


def _sigmoid(v):
    return 1.0 / (1.0 + jnp.exp(-v))


def _dot(a, b, dims=NN):
    return lax.dot_general(a.astype(BF16), b.astype(BF16), dims, preferred_element_type=F32)


def _dot_exact01(m01, v):
    m = m01.astype(BF16)
    hi = v.astype(BF16)
    r1 = v - hi.astype(F32)
    mid = r1.astype(BF16)
    lo = (r1 - mid.astype(F32)).astype(BF16)
    out = lax.dot_general(m, hi, NN, preferred_element_type=F32)
    out = out + lax.dot_general(m, mid, NN, preferred_element_type=F32)
    return out + lax.dot_general(m, lo, NN, preferred_element_type=F32)


def _dot_exact01_r(v, m01):
    m = m01.astype(BF16)
    hi = v.astype(BF16)
    r1 = v - hi.astype(F32)
    mid = r1.astype(BF16)
    lo = (r1 - mid.astype(F32)).astype(BF16)
    out = lax.dot_general(hi, m, NN, preferred_element_type=F32)
    out = out + lax.dot_general(mid, m, NN, preferred_element_type=F32)
    return out + lax.dot_general(lo, m, NN, preferred_element_type=F32)


def _tn(a, b):
    ap = jnp.concatenate([a, jnp.zeros_like(a)], axis=0)
    bp = jnp.concatenate([b, jnp.zeros_like(b)], axis=0)
    return _dot(ap.T, bp)


def _row_tile(t):
    for tm in (512, 256, 128, 64, 32, 16, 8):
        if t % tm == 0:
            return tm
    raise ValueError(t)


def _rms_fwd(x, g, name):
    t, d = x.shape
    tm = _row_tile(t)

    def body(x_ref, g_ref, h_ref):
        xv = x_ref[...]
        r = lax.rsqrt(jnp.mean(xv * xv, axis=-1, keepdims=True) + RMS_EPS)
        h_ref[...] = (xv * r * g_ref[...]).astype(BF16)

    return pl.pallas_call(
        body, name=name, grid=(t // tm,),
        in_specs=[pl.BlockSpec((tm, d), lambda i: (i, 0)), pl.BlockSpec((1, d), lambda i: (0, 0))],
        out_specs=pl.BlockSpec((tm, d), lambda i: (i, 0)),
        out_shape=jax.ShapeDtypeStruct((t, d), BF16),
        compiler_params=_params("parallel"),
    )(x, g)


def _rms_bwd(x, g, dh, dres, name):
    t, d = x.shape
    tm = _row_tile(t)

    def body(x_ref, g_ref, dh_ref, dres_ref, dx_ref, dx16_ref, dg_ref):
        xv = x_ref[...]
        r = lax.rsqrt(jnp.mean(xv * xv, axis=-1, keepdims=True) + RMS_EPS)
        xhat = xv * r
        dhv = dh_ref[...]
        gd = dhv * g_ref[...]
        dx = dres_ref[...] + r * (gd - xhat * jnp.mean(gd * xhat, axis=-1, keepdims=True))
        dx_ref[...] = dx
        dx16_ref[...] = dx.astype(BF16)
        part = jnp.sum(dhv * xhat, axis=0, keepdims=True)

        @pl.when(pl.program_id(0) == 0)
        def _():
            dg_ref[...] = part

        @pl.when(pl.program_id(0) > 0)
        def _():
            dg_ref[...] += part

    row = pl.BlockSpec((tm, d), lambda i: (i, 0))
    vec = pl.BlockSpec((1, d), lambda i: (0, 0))
    return pl.pallas_call(
        body, name=name, grid=(t // tm,),
        in_specs=[row, vec, row, row], out_specs=[row, row, vec],
        out_shape=[jax.ShapeDtypeStruct((t, d), F32), jax.ShapeDtypeStruct((t, d), BF16),
                   jax.ShapeDtypeStruct((1, d), F32)],
        compiler_params=_params("arbitrary"),
    )(x, g, dh, dres)


def _mm(a, b, *, ta=False, tb=False, tm, tn, out_dtype=F32, add=None, scale=1.0, name):
    m, k = (a.shape[1], a.shape[0]) if ta else a.shape
    n = b.shape[0] if tb else b.shape[1]
    tm, tn = min(tm, m), min(tn, n)
    assert m % tm == 0 and n % tn == 0, (m, n, tm, tn)
    dims = (((0 if ta else 1,), (1 if tb else 0,)), ((), ()))

    def body(*refs):
        a_ref, b_ref = refs[0], refs[1]
        o_ref = refs[-1]
        r = lax.dot_general(a_ref[...].astype(BF16), b_ref[...].astype(BF16), dims, preferred_element_type=F32)
        if scale != 1.0:
            r = r * scale
        if add is not None:
            r = r + refs[2][...]
        o_ref[...] = r.astype(out_dtype)

    a_spec = pl.BlockSpec((k, tm), lambda i, j: (0, i)) if ta else pl.BlockSpec((tm, k), lambda i, j: (i, 0))
    b_spec = pl.BlockSpec((tn, k), lambda i, j: (j, 0)) if tb else pl.BlockSpec((k, tn), lambda i, j: (0, j))
    o_spec = pl.BlockSpec((tm, tn), lambda i, j: (i, j))
    ins, specs = [a, b], [a_spec, b_spec]
    if add is not None:
        ins.append(add)
        specs.append(o_spec)
    return pl.pallas_call(
        body, name=name, grid=(m // tm, n // tn), in_specs=specs, out_specs=o_spec,
        out_shape=jax.ShapeDtypeStruct((m, n), out_dtype),
        compiler_params=_params("parallel", "parallel"),
    )(*ins)


def _ffn_tile(f):
    for tf in (1408, 512, 256, 128):
        if f % tf == 0:
            return tf
    raise ValueError(f)


def _ffn_fwd(h, x, wg, wu, wd, name):
    t, d = x.shape
    f = wg.shape[1]
    tm, tf = _row_tile(t), _ffn_tile(f)
    nf = f // tf

    def body(h_ref, x_ref, wg_ref, wu_ref, wd_ref, y_ref, g_ref, u_ref, acc_ref):
        j = pl.program_id(1)
        hv = h_ref[...]
        gv = lax.dot_general(hv, wg_ref[...], NN, preferred_element_type=F32)
        uv = lax.dot_general(hv, wu_ref[...], NN, preferred_element_type=F32)
        av = gv * _sigmoid(gv) * uv
        g_ref[...] = gv.astype(BF16)
        u_ref[...] = uv.astype(BF16)
        part = lax.dot_general(av.astype(BF16), wd_ref[...], NN, preferred_element_type=F32)

        @pl.when(j == 0)
        def _():
            acc_ref[...] = part

        @pl.when(j > 0)
        def _():
            acc_ref[...] += part

        @pl.when(j == nf - 1)
        def _():
            y_ref[...] = x_ref[...] + 0.5 * acc_ref[...]

    row = pl.BlockSpec((tm, d), lambda i, j: (i, 0))
    hid = pl.BlockSpec((tm, tf), lambda i, j: (i, j))
    return pl.pallas_call(
        body, name=name, grid=(t // tm, nf),
        in_specs=[row, row, pl.BlockSpec((d, tf), lambda i, j: (0, j)), pl.BlockSpec((d, tf), lambda i, j: (0, j)),
                  pl.BlockSpec((tf, d), lambda i, j: (j, 0))],
        out_specs=[row, hid, hid],
        out_shape=[jax.ShapeDtypeStruct((t, d), F32), jax.ShapeDtypeStruct((t, f), BF16),
                   jax.ShapeDtypeStruct((t, f), BF16)],
        scratch_shapes=[pltpu.VMEM((tm, d), F32)],
        compiler_params=_params("parallel", "arbitrary"),
    )(h, x, wg, wu, wd)


def _ffn_bwd_mid(dy, wd, g, u, name):
    t, d = dy.shape
    f = wd.shape[0]
    tm, tf = _row_tile(t), _ffn_tile(f)

    def body(dy_ref, wd_ref, g_ref, u_ref, dg_ref, du_ref, a_ref):
        da = 0.5 * lax.dot_general(dy_ref[...].astype(BF16), wd_ref[...], NT, preferred_element_type=F32)
        gv = g_ref[...].astype(F32)
        uv = u_ref[...].astype(F32)
        s = _sigmoid(gv)
        silu = gv * s
        dg_ref[...] = (da * uv * (s * (1.0 + gv * (1.0 - s)))).astype(BF16)
        du_ref[...] = (da * silu).astype(BF16)
        a_ref[...] = (silu * uv).astype(BF16)

    hid = pl.BlockSpec((tm, tf), lambda i, j: (i, j))
    return pl.pallas_call(
        body, name=name, grid=(t // tm, f // tf),
        in_specs=[pl.BlockSpec((tm, d), lambda i, j: (i, 0)), pl.BlockSpec((tf, d), lambda i, j: (j, 0)), hid, hid],
        out_specs=[hid, hid, hid],
        out_shape=[jax.ShapeDtypeStruct((t, f), BF16)] * 3,
        compiler_params=_params("parallel", "parallel"),
    )(dy, wd, g, u)


def _loss(y, tgt, name):
    t, d = y.shape
    tm = _row_tile(t)

    def body(y_ref, t_ref, dy_ref, dy16_ref, sq_ref):
        e = y_ref[...] - t_ref[...]
        dy = e * (1.0 / d)
        dy_ref[...] = dy
        dy16_ref[...] = dy.astype(BF16)
        part = jnp.sum(e * e, axis=0, keepdims=True)

        @pl.when(pl.program_id(0) == 0)
        def _():
            sq_ref[...] = part

        @pl.when(pl.program_id(0) > 0)
        def _():
            sq_ref[...] += part

    row = pl.BlockSpec((tm, d), lambda i: (i, 0))
    vec = pl.BlockSpec((1, d), lambda i: (0, 0))
    return pl.pallas_call(
        body, name=name, grid=(t // tm,), in_specs=[row, row], out_specs=[row, row, vec],
        out_shape=[jax.ShapeDtypeStruct((t, d), F32), jax.ShapeDtypeStruct((t, d), BF16),
                   jax.ShapeDtypeStruct((1, d), F32)],
        compiler_params=_params("arbitrary"),
    )(y, tgt)


def _rel_index(t, s_band):
    return jnp.clip(t + KPAD - s_band, -REL_CLIP, REL_CLIP) + REL_CLIP


def _bias_expand(rel_bias_pad):
    nh = rel_bias_pad.shape[0]

    def body(rb_ref, out_ref):
        rb = rb_ref[...]
        i_io = lax.broadcasted_iota(jnp.int32, (N_REL_PAD, BAND), 0)
        s_io = lax.broadcasted_iota(jnp.int32, (N_REL_PAD, BAND), 1)

        def row(t, carry):
            onehot = (i_io == _rel_index(t, s_io)).astype(F32)
            out_ref[t] = _dot_exact01_r(rb, onehot)
            return carry

        lax.fori_loop(0, CHUNK, row, 0)

    return pl.pallas_call(
        body, name="bias_expand", out_shape=jax.ShapeDtypeStruct((CHUNK, nh, BAND), F32),
        compiler_params=pltpu.CompilerParams(vmem_limit_bytes=VMEM_LIMIT),
    )(rel_bias_pad)


def _bias_fold(dbias):
    ng, nh = dbias.shape[0], dbias.shape[2]

    def body(db_ref, out_ref):
        s_io = lax.broadcasted_iota(jnp.int32, (BAND, N_REL_PAD), 0)
        i_io = lax.broadcasted_iota(jnp.int32, (BAND, N_REL_PAD), 1)

        def row(t, acc):
            onehot = (i_io == _rel_index(t, s_io)).astype(F32)
            d = db_ref[0, t]
            for gi in range(1, ng):
                d = d + db_ref[gi, t]
            return acc + _dot_exact01_r(d, onehot)

        out_ref[...] = lax.fori_loop(0, CHUNK, row, jnp.zeros((nh, N_REL_PAD), F32))

    return pl.pallas_call(
        body, name="bias_fold", out_shape=jax.ShapeDtypeStruct((nh, N_REL_PAD), F32),
        compiler_params=pltpu.CompilerParams(vmem_limit_bytes=VMEM_LIMIT),
    )(dbias)


def _attn_scores(qs_scr, k_scr, bias_h, c):
    r0 = pl.multiple_of(c * CHUNK, CHUNK)
    qc = qs_scr[pl.ds(r0, CHUNK), :]
    kb = k_scr[pl.ds(r0, BAND), :]
    s = lax.dot_general(qc, kb, NT, preferred_element_type=F32) + bias_h
    col = lax.broadcasted_iota(jnp.int32, (CHUNK, BAND), 1)
    first = jnp.maximum(CHUNK, (LEFT_CHUNKS + 1 - c) * CHUNK)
    s = jnp.where(col >= first, s, -jnp.inf)
    e = jnp.exp(s - jnp.max(s, axis=-1, keepdims=True))
    p = e * (1.0 / jnp.sum(e, axis=-1, keepdims=True))
    return p, qc, kb, r0


def _qk_norm(v, g):
    r = lax.rsqrt(jnp.mean(v * v, axis=-1, keepdims=True) + RMS_EPS)
    return v * r, r


def _attn_fwd(proj, bias, gq, gk, nb, seq):
    nc = seq // CHUNK
    scale = ATTN_HEAD_DIM ** -0.5

    def body(q_ref, k_ref, v_ref, bias_ref, gq_ref, gk_ref, o_ref, qs_scr, k_scr, v_scr):
        for hh in range(2):
            sl = slice(hh * ATTN_HEAD_DIM, (hh + 1) * ATTN_HEAD_DIM)
            qhat, _ = _qk_norm(q_ref[:, sl], None)
            khat, _ = _qk_norm(k_ref[:, sl], None)
            qs_scr[...] = (qhat * gq_ref[...] * scale).astype(BF16)
            k_scr[0:KPAD, :] = jnp.zeros((KPAD, ATTN_HEAD_DIM), BF16)
            v_scr[0:KPAD, :] = jnp.zeros((KPAD, ATTN_HEAD_DIM), BF16)
            k_scr[KPAD:, :] = (khat * gk_ref[...]).astype(BF16)
            v_scr[KPAD:, :] = v_ref[:, sl].astype(BF16)
            bias_h = bias_ref[hh]

            def chunk(c, carry):
                p, _, _, r0 = _attn_scores(qs_scr, k_scr, bias_h, c)
                vb = v_scr[pl.ds(r0, BAND), :]
                o_ref[pl.ds(r0, CHUNK), sl] = lax.dot_general(p.astype(BF16), vb, NN, preferred_element_type=F32)
                return carry

            lax.fori_loop(0, nc, chunk, 0)

    def col(off):
        return pl.BlockSpec((seq, 128), lambda b, hp: (b, off + hp))

    vec = pl.BlockSpec((1, ATTN_HEAD_DIM), lambda b, hp: (0, 0))
    return pl.pallas_call(
        body, name="attn_fwd", grid=(nb, ATTN_HEADS // 2),
        in_specs=[col(0), col(4), col(8), pl.BlockSpec((2, CHUNK, BAND), lambda b, hp: (hp, 0, 0)), vec, vec],
        out_specs=pl.BlockSpec((seq, 128), lambda b, hp: (b, hp)),
        out_shape=jax.ShapeDtypeStruct((nb * seq, ATTN_WIDTH), F32),
        scratch_shapes=[pltpu.VMEM((seq, ATTN_HEAD_DIM), BF16), pltpu.VMEM((seq + KPAD, ATTN_HEAD_DIM), BF16),
                        pltpu.VMEM((seq + KPAD, ATTN_HEAD_DIM), BF16)],
        compiler_params=_params("parallel", "parallel"),
    )(proj, proj, proj, bias, gq, gk)


def _attn_bwd(proj, out, dout, bias, gq, gk, nb, seq):
    nc = seq // CHUNK
    scale = ATTN_HEAD_DIM ** -0.5
    hd = ATTN_HEAD_DIM

    def body(q_ref, k_ref, v_ref, o_ref, do_ref, bias_ref, gq_ref, gk_ref,
             dq_ref, dk_ref, dv_ref, dbias_ref, dgq_ref, dgk_ref,
             qs_scr, k_scr, v_scr, dqn_scr, dk_scr, dv_scr, db_scr):
        for hh in range(2):
            sl = slice(hh * hd, (hh + 1) * hd)
            qhat, rq = _qk_norm(q_ref[:, sl], None)
            khat, rk = _qk_norm(k_ref[:, sl], None)
            qs_scr[...] = (qhat * gq_ref[...] * scale).astype(BF16)
            k_scr[0:KPAD, :] = jnp.zeros((KPAD, hd), BF16)
            v_scr[0:KPAD, :] = jnp.zeros((KPAD, hd), BF16)
            k_scr[KPAD:, :] = (khat * gk_ref[...]).astype(BF16)
            v_scr[KPAD:, :] = v_ref[:, sl].astype(BF16)
            dk_scr[...] = jnp.zeros_like(dk_scr)
            dv_scr[...] = jnp.zeros_like(dv_scr)
            db_scr[...] = jnp.zeros_like(db_scr)
            bias_h = bias_ref[hh]

            def chunk(c, carry):
                p, qc, kb, r0 = _attn_scores(qs_scr, k_scr, bias_h, c)
                vb = v_scr[pl.ds(r0, BAND), :]
                do_c = do_ref[pl.ds(r0, CHUNK), sl]
                o_c = o_ref[pl.ds(r0, CHUNK), sl]
                drow = jnp.sum(do_c * o_c, axis=-1, keepdims=True)
                dp = lax.dot_general(do_c.astype(BF16), vb, NT, preferred_element_type=F32)
                ds = p * (dp - drow)
                db_scr[...] += ds
                dqn_scr[pl.ds(r0, CHUNK), :] = scale * lax.dot_general(ds.astype(BF16), kb, NN,
                                                                       preferred_element_type=F32)
                dk_scr[pl.ds(r0, BAND), :] += _tn(ds, qc.astype(F32))
                dv_scr[pl.ds(r0, BAND), :] += _tn(p, do_c)
                return carry

            lax.fori_loop(0, nc, chunk, 0)

            def norm_bwd(dn, hat, r, g_ref):
                gd = dn * g_ref[...]
                return r * (gd - hat * jnp.mean(gd * hat, axis=-1, keepdims=True)), jnp.sum(dn * hat, axis=0,
                                                                                             keepdims=True)

            dq, dgq = norm_bwd(dqn_scr[...], qhat, rq, gq_ref)
            dk, dgk = norm_bwd(dk_scr[KPAD:, :], khat, rk, gk_ref)
            dq_ref[:, sl] = dq.astype(BF16)
            dk_ref[:, sl] = dk.astype(BF16)
            dv_ref[:, sl] = dv_scr[KPAD:, :].astype(BF16)
            dbias_ref[0, hh] = db_scr[...]
            dgq_ref[0, hh:hh + 1, :] = dgq
            dgk_ref[0, hh:hh + 1, :] = dgk

    def col(off):
        return pl.BlockSpec((seq, 128), lambda b, hp: (b, off + hp))

    vec = pl.BlockSpec((1, hd), lambda b, hp: (0, 0))
    gvec = pl.BlockSpec((1, 2, hd), lambda b, hp: (b * (ATTN_HEADS // 2) + hp, 0, 0))
    t = nb * seq
    return pl.pallas_call(
        body, name="attn_bwd", grid=(nb, ATTN_HEADS // 2),
        in_specs=[col(0), col(4), col(8), col(0), col(0),
                  pl.BlockSpec((2, CHUNK, BAND), lambda b, hp: (hp, 0, 0)), vec, vec],
        out_specs=[col(0), col(0), col(0), pl.BlockSpec((1, 2, CHUNK, BAND), lambda b, hp: (b, hp, 0, 0)),
                   gvec, gvec],
        out_shape=[jax.ShapeDtypeStruct((t, ATTN_WIDTH), BF16)] * 3
        + [jax.ShapeDtypeStruct((nb, ATTN_HEADS, CHUNK, BAND), F32)]
        + [jax.ShapeDtypeStruct((nb * ATTN_HEADS // 2, 2, hd), F32)] * 2,
        scratch_shapes=[pltpu.VMEM((seq, hd), BF16), pltpu.VMEM((seq + KPAD, hd), BF16),
                        pltpu.VMEM((seq + KPAD, hd), BF16), pltpu.VMEM((seq, hd), F32),
                        pltpu.VMEM((seq + KPAD, hd), F32), pltpu.VMEM((seq + KPAD, hd), F32),
                        pltpu.VMEM((CHUNK, BAND), F32)],
        compiler_params=_params("parallel", "parallel"),
    )(proj, proj, proj, out, dout, bias, gq, gk)


def _tri(lower):
    r = lax.broadcasted_iota(jnp.int32, (CHUNK, CHUNK), 0)
    c = lax.broadcasted_iota(jnp.int32, (CHUNK, CHUNK), 1)
    return (r >= c) if lower else (r <= c)


def _hgrn_gates(hq, hf, lb):
    sq = _sigmoid(hq)
    sf = _sigmoid(hf)
    return hq * sq, sq, sf, lb + (1.0 - lb) * sf


def _hgrn_offdiag(q_s, k_s, b_s):
    row = lax.broadcasted_iota(jnp.int32, (CHUNK, HGRN_HEAD_DIM), 0)
    bv, qv, kv = b_s[...], q_s[...], k_s[...]
    eqs, eks = [], []
    for i in range(1, N_SUB):
        r = b_s[pl.ds(SUB * i - 1, 1), :]
        in_i = (row >= SUB * i) & (row < SUB * (i + 1))
        eqs.append(jnp.exp(jnp.where(in_i, bv - r, -jnp.inf)))
        eks.append(jnp.exp(jnp.where(row < SUB * i, r - bv, -jnp.inf)))
    eq = jnp.concatenate(eqs, axis=1)
    ek = jnp.concatenate(eks, axis=1)
    qt = jnp.concatenate([qv] * (N_SUB - 1), axis=1) * eq
    kt = jnp.concatenate([kv] * (N_SUB - 1), axis=1) * ek
    return qt, kt, eq, ek


def _hgrn_diag_e(b_s, i, s):
    t_io = lax.broadcasted_iota(jnp.int32, (SUB, HGRN_HEAD_DIM), 0)
    bi = b_s[pl.ds(SUB * i, SUB), :]
    return jnp.exp(jnp.where(t_io >= s, bi - b_s[pl.ds(SUB * i + s, 1), :], -jnp.inf)), t_io


def _hgrn_intra(q_s, k_s, b_s, a_s, qt, kt):
    ktp = jnp.concatenate([kt, jnp.zeros_like(kt)], axis=0)
    a_s[...] = _dot(qt, ktp, NT)
    col = lax.broadcasted_iota(jnp.int32, (SUB, HGRN_HEAD_DIM), 1)
    for i in range(N_SUB):
        qi = q_s[pl.ds(SUB * i, SUB), :]
        ai = jnp.zeros((SUB, HGRN_HEAD_DIM), F32)
        for s in range(SUB):
            e, _ = _hgrn_diag_e(b_s, i, s)
            a_col = jnp.sum(qi * k_s[pl.ds(SUB * i + s, 1), :] * e, axis=-1, keepdims=True)
            ai = ai + jnp.where(col == SUB * i + s, a_col, 0.0)
        a_s[pl.ds(SUB * i, SUB), :] += ai


def _hgrn_fwd(proj, lb, go, nb, seq):
    nc = seq // CHUNK
    hd = HGRN_HEAD_DIM

    def body(hq_ref, hf_ref, hi_ref, hg_ref, lb_ref, go_ref, y_ref, o_ref, st_ref, st, q_s, k_s, b_s, a_s):
        st[...] = jnp.zeros_like(st)
        lower = _tri(True)

        def chunk(c, carry):
            r0 = pl.multiple_of(c * CHUNK, CHUNK)
            rows = pl.ds(r0, CHUNK)
            q, _, _, f = _hgrn_gates(hq_ref[rows, :], hf_ref[rows, :], lb_ref[...])
            v = hi_ref[rows, :]
            b = _dot_exact01(lower, jnp.log(f))
            q_s[...] = q
            k_s[...] = 1.0 - f
            b_s[...] = b
            st_ref[0, c] = st[...]
            qt, kt, _, _ = _hgrn_offdiag(q_s, k_s, b_s)
            _hgrn_intra(q_s, k_s, b_s, a_s, qt, kt)
            vp = jnp.concatenate([v, jnp.zeros_like(v)], axis=0)
            o = _dot(a_s[...], vp) + _dot(q * jnp.exp(b), st[...], NT)
            bl = b_s[pl.ds(CHUNK - 1, 1), :]
            st[...] = st[...] * jnp.exp(bl) + _tn(v, (1.0 - f) * jnp.exp(bl - b))
            o_ref[rows, :] = o
            n = o * lax.rsqrt(jnp.mean(o * o, axis=-1, keepdims=True) + RMS_EPS) * go_ref[...]
            hg = hg_ref[rows, :]
            y_ref[rows, :] = n * hg * _sigmoid(hg)
            return carry

        lax.fori_loop(0, nc, chunk, 0)

    def col(off):
        return pl.BlockSpec((seq, hd), lambda b, h: (b, off + h))

    out = pl.BlockSpec((seq, hd), lambda b, h: (b, h))
    t = nb * seq
    return pl.pallas_call(
        body, name="hgrn_fwd", grid=(nb, HGRN_HEADS),
        in_specs=[col(12), col(16), col(20), col(24), pl.BlockSpec((1, hd), lambda b, h: (0, h)),
                  pl.BlockSpec((1, hd), lambda b, h: (0, 0))],
        out_specs=[out, out, pl.BlockSpec((1, nc, hd, hd), lambda b, h: (b * HGRN_HEADS + h, 0, 0, 0))],
        out_shape=[jax.ShapeDtypeStruct((t, HGRN_HEADS * hd), F32)] * 2
        + [jax.ShapeDtypeStruct((nb * HGRN_HEADS, nc, hd, hd), F32)],
        scratch_shapes=[pltpu.VMEM((hd, hd), F32)] + [pltpu.VMEM((CHUNK, hd), F32)] * 4,
        compiler_params=_params("parallel", "parallel"),
    )(proj, proj, proj, proj, lb, go)


def _hgrn_bwd(proj, lb, go, o_pre, states, dout, nb, seq):
    nc = seq // CHUNK
    hd = HGRN_HEAD_DIM

    def body(hq_ref, hf_ref, hi_ref, hg_ref, lb_ref, go_ref, o_ref, st_ref, dy_ref,
             dhq_ref, dhf_ref, dhi_ref, dhg_ref, dlb_ref, dgo_ref,
             dst, q_s, k_s, b_s, a_s, da_s, dqi_s, dki_s, dlb_acc, dgo_acc):
        dst[...] = jnp.zeros_like(dst)
        dlb_acc[...] = jnp.zeros_like(dlb_acc)
        dgo_acc[...] = jnp.zeros_like(dgo_acc)
        lower, upper = _tri(True), _tri(False)
        lbv, gov = lb_ref[...], go_ref[...]
        row = lax.broadcasted_iota(jnp.int32, (CHUNK, hd), 0)

        def chunk(it, carry):
            c = nc - 1 - it
            r0 = pl.multiple_of(c * CHUNK, CHUNK)
            rows = pl.ds(r0, CHUNK)
            hq, hf, v, hg = hq_ref[rows, :], hf_ref[rows, :], hi_ref[rows, :], hg_ref[rows, :]
            q, sq, sf, f = _hgrn_gates(hq, hf, lbv)
            kk = 1.0 - f
            b = _dot_exact01(lower, jnp.log(f))
            q_s[...] = q
            k_s[...] = kk
            b_s[...] = b
            bl = b_s[pl.ds(CHUNK - 1, 1), :]
            ebl = jnp.exp(bl)
            ekd = jnp.exp(bl - b)
            kd = kk * ekd
            eb = jnp.exp(b)
            qb = q * eb
            st0 = st_ref[0, c]
            dst1 = dst[...]

            o = o_ref[rows, :]
            dy = dy_ref[rows, :]
            sg = _sigmoid(hg)
            rstd = lax.rsqrt(jnp.mean(o * o, axis=-1, keepdims=True) + RMS_EPS)
            ohat = o * rstd
            dn = dy * hg * sg
            dhg_ref[rows, :] = (dy * ohat * gov * (sg * (1.0 + hg * (1.0 - sg)))).astype(BF16)
            dgo_acc[...] += jnp.sum(dn * ohat, axis=0, keepdims=True)
            gdn = dn * gov
            do = rstd * (gdn - ohat * jnp.mean(gdn * ohat, axis=-1, keepdims=True))

            qt, kt, eq, ek = _hgrn_offdiag(q_s, k_s, b_s)
            _hgrn_intra(q_s, k_s, b_s, a_s, qt, kt)
            da = _dot(do, v, NT)
            dat = _dot(v, do, NT)
            da_s[...] = da
            dqo = _dot(da, kt) * eq
            dko = _dot(dat, qt) * ek
            dqi_s[...] = dqo[:, 0:hd] + dqo[:, hd:2 * hd] + dqo[:, 2 * hd:3 * hd]
            dki_s[...] = dko[:, 0:hd] + dko[:, hd:2 * hd] + dko[:, 2 * hd:3 * hd]
            col = lax.broadcasted_iota(jnp.int32, (SUB, CHUNK), 1)
            for i in range(N_SUB):
                qi = q_s[pl.ds(SUB * i, SUB), :]
                dai = da_s[pl.ds(SUB * i, SUB), :]
                dqd = jnp.zeros((SUB, hd), F32)
                dkd_ = jnp.zeros((SUB, hd), F32)
                for s in range(SUB):
                    e, t_io = _hgrn_diag_e(b_s, i, s)
                    dacol = jnp.sum(jnp.where(col == SUB * i + s, dai, 0.0), axis=-1, keepdims=True)
                    w = dacol * e
                    dqd = dqd + w * k_s[pl.ds(SUB * i + s, 1), :]
                    dkd_ = dkd_ + jnp.where(t_io == s, jnp.sum(w * qi, axis=0, keepdims=True), 0.0)
                dqi_s[pl.ds(SUB * i, SUB), :] += dqd
                dki_s[pl.ds(SUB * i, SUB), :] += dkd_
            dqi, dki = dqi_s[...], dki_s[...]

            dv = _tn(a_s[...], do)[0:CHUNK, :] + _dot(kd, dst1, NT)
            dqb = _dot(do, st0)
            dkd = _dot(v, dst1)
            t2 = dkd * kd
            dq = dqb * eb + dqi
            dk = dkd * ekd + dki
            dbl = jnp.sum(t2, axis=0, keepdims=True) + ebl * jnp.sum(st0 * dst1, axis=0, keepdims=True)
            db = dqb * qb - t2 + q * dqi - kk * dki + jnp.where(row == CHUNK - 1, dbl, 0.0)
            dg = _dot_exact01(upper, db)
            dst[...] = dst1 * ebl + _tn(do, qb)

            df = dg / f - dk
            dhf_ref[rows, :] = (df * (1.0 - lbv) * sf * (1.0 - sf)).astype(BF16)
            dlb_acc[...] += jnp.sum(df * (1.0 - sf), axis=0, keepdims=True)
            dhq_ref[rows, :] = (dq * (sq * (1.0 + hq * (1.0 - sq)))).astype(BF16)
            dhi_ref[rows, :] = dv.astype(BF16)
            return carry

        lax.fori_loop(0, nc, chunk, 0)
        dlb_ref[0] = dlb_acc[...]
        dgo_ref[0] = dgo_acc[...]

    def col(off):
        return pl.BlockSpec((seq, hd), lambda b, h: (b, off + h))

    out = pl.BlockSpec((seq, hd), lambda b, h: (b, h))
    part = pl.BlockSpec((1, 1, hd), lambda b, h: (b * HGRN_HEADS + h, 0, 0))
    t = nb * seq
    return pl.pallas_call(
        body, name="hgrn_bwd", grid=(nb, HGRN_HEADS),
        in_specs=[col(12), col(16), col(20), col(24), pl.BlockSpec((1, hd), lambda b, h: (0, h)),
                  pl.BlockSpec((1, hd), lambda b, h: (0, 0)), out,
                  pl.BlockSpec((1, nc, hd, hd), lambda b, h: (b * HGRN_HEADS + h, 0, 0, 0)), col(4)],
        out_specs=[out, out, out, out, part, part],
        out_shape=[jax.ShapeDtypeStruct((t, HGRN_HEADS * hd), BF16)] * 4
        + [jax.ShapeDtypeStruct((nb * HGRN_HEADS, 1, hd), F32)] * 2,
        scratch_shapes=[pltpu.VMEM((hd, hd), F32)] + [pltpu.VMEM((CHUNK, hd), F32)] * 4
        + [pltpu.VMEM((CHUNK, CHUNK), F32)] + [pltpu.VMEM((CHUNK, hd), F32)] * 2 + [pltpu.VMEM((1, hd), F32)] * 2,
        compiler_params=_params("parallel", "parallel"),
    )(proj, proj, proj, proj, lb, go, o_pre, states, dout)


def _lb_fwd(lower_bounds):
    def body(x_ref, o_ref):
        xv = x_ref[...]
        e = jnp.exp(xv - jnp.max(xv, axis=0, keepdims=True))
        o_ref[...] = e[0:1, :] / jnp.sum(e, axis=0, keepdims=True)

    return pl.pallas_call(body, name="lb_fwd",
                          out_shape=jax.ShapeDtypeStruct((1, lower_bounds.shape[1]), F32))(lower_bounds)


def _lb_bwd(lower_bounds, dlb_parts):
    ng = dlb_parts.shape[0]

    def body(x_ref, d_ref, o_ref):
        xv = x_ref[...]
        e = jnp.exp(xv - jnp.max(xv, axis=0, keepdims=True))
        p = e / jnp.sum(e, axis=0, keepdims=True)
        dlb = d_ref[0]
        for gi in range(1, ng):
            dlb = dlb + d_ref[gi]
        first = lax.broadcasted_iota(jnp.int32, xv.shape, 0) == 0
        o_ref[...] = p * (jnp.where(first, dlb, 0.0) - p[0:1, :] * dlb)

    return pl.pallas_call(body, name="lb_bwd",
                          out_shape=jax.ShapeDtypeStruct(lower_bounds.shape, F32))(lower_bounds, dlb_parts)


def _ffn_bwd(x, g, h, gate, up, dy, dy16, wg, wu, wd, tag):
    dgate, dup, act = _ffn_bwd_mid(dy16, wd, gate, up, tag + "_bwd_mid")
    dwd = _mm(act, dy16, ta=True, tm=1408, tn=512, scale=0.5, name=tag + "_dwd")
    dwg = _mm(h, dgate, ta=True, tm=512, tn=1408, name=tag + "_dwg")
    dwu = _mm(h, dup, ta=True, tm=512, tn=1408, name=tag + "_dwu")
    dh = _mm(dgate, wg, tb=True, tm=512, tn=1024, name=tag + "_dh_gate")
    dh = _mm(dup, wu, tb=True, tm=512, tn=1024, add=dh, name=tag + "_dh_up")
    dx, dx16, dgain = _rms_bwd(x, g, dh, dy, tag + "_norm_bwd")
    return dx, dx16, dgain, dwg, dwu, dwd


def _local_step(x, tgt, sp, w, nb, seq):
    d = x.shape[1]
    h1 = _rms_fwd(x, sp["ffn1_norm_g"], "ffn1_norm")
    x1, gate1, up1 = _ffn_fwd(h1, x, w["ffn1_w_gate"], w["ffn1_w_up"], w["ffn1_w_down"], "ffn1_fwd")
    h2 = _rms_fwd(x1, sp["mix_norm_g"], "mix_norm")
    proj = _mm(h2, w["w_in"], tm=512, tn=512, name="in_proj")
    rb_pad = jnp.pad(sp["attn_rel_bias"], ((0, 0), (0, N_REL_PAD - N_REL)))
    bias = jnp.transpose(_bias_expand(rb_pad), (1, 0, 2))
    lb = _lb_fwd(sp["hgrn_lower_bounds"])
    attn = _attn_fwd(proj, bias, sp["attn_q_norm_g"], sp["attn_k_norm_g"], nb, seq)
    hy, ho, hstate = _hgrn_fwd(proj, lb, sp["hgrn_out_norm_g"], nb, seq)
    mix = jnp.concatenate([attn, hy], axis=1)
    x2 = _mm(mix, w["w_out"], tm=512, tn=1024, add=x1, name="out_proj")
    h3 = _rms_fwd(x2, sp["ffn2_norm_g"], "ffn2_norm")
    x3, gate2, up2 = _ffn_fwd(h3, x2, w["ffn2_w_gate"], w["ffn2_w_up"], w["ffn2_w_down"], "ffn2_fwd")
    dx3, dx3_16, sq = _loss(x3, tgt, "loss")
    loss = 0.5 * jnp.sum(sq) / d

    dx2, dx2_16, dg3, dwg2, dwu2, dwd2 = _ffn_bwd(x2, sp["ffn2_norm_g"], h3, gate2, up2, dx3, dx3_16,
                                                  w["ffn2_w_gate"], w["ffn2_w_up"], w["ffn2_w_down"], "ffn2")
    dmix = _mm(dx2_16, w["w_out"], tb=True, tm=512, tn=1024, name="out_proj_dx")
    dwout = _mm(mix, dx2_16, ta=True, tm=512, tn=1024, name="out_proj_dw")
    dq, dk, dv, dbias, dgq, dgk = _attn_bwd(proj, attn, dmix, bias, sp["attn_q_norm_g"], sp["attn_k_norm_g"],
                                            nb, seq)
    dhq, dhf, dhi, dhg, dlb, dgo = _hgrn_bwd(proj, lb, sp["hgrn_out_norm_g"], ho, hstate, dmix, nb, seq)
    dproj = jnp.concatenate([dq, dk, dv, dhq, dhf, dhi, dhg], axis=1)
    dwin = _mm(h2, dproj, ta=True, tm=512, tn=512, name="in_proj_dw")
    dh2 = _mm(dproj, w["w_in"], tb=True, tm=512, tn=1024, name="in_proj_dx")
    dx1, dx1_16, dgm = _rms_bwd(x1, sp["mix_norm_g"], dh2, dx2, "mix_norm_bwd")
    dx0, _, dg1, dwg1, dwu1, dwd1 = _ffn_bwd(x, sp["ffn1_norm_g"], h1, gate1, up1, dx1, dx1_16,
                                             w["ffn1_w_gate"], w["ffn1_w_up"], w["ffn1_w_down"], "ffn1")

    small = {
        "ffn1_norm_g": dg1, "mix_norm_g": dgm, "ffn2_norm_g": dg3,
        "attn_q_norm_g": jnp.sum(dgq, axis=(0, 1))[None, :], "attn_k_norm_g": jnp.sum(dgk, axis=(0, 1))[None, :],
        "attn_rel_bias": _bias_fold(jnp.transpose(dbias, (0, 2, 1, 3)))[:, :N_REL],
        "hgrn_lower_bounds": _lb_bwd(sp["hgrn_lower_bounds"], dlb.reshape(nb, 1, HGRN_HEADS * HGRN_HEAD_DIM)),
        "hgrn_out_norm_g": jnp.sum(dgo, axis=(0, 1))[None, :],
    }
    big = {"ffn1_w_gate": dwg1, "ffn1_w_up": dwu1, "ffn1_w_down": dwd1, "w_in": dwin, "w_out": dwout,
           "ffn2_w_gate": dwg2, "ffn2_w_up": dwu2, "ffn2_w_down": dwd2}
    return loss, dx0, small, big


MESH = pl.DeviceIdType.MESH
ANY = pl.BlockSpec(memory_space=pl.ANY)


def _coords():
    return lax.axis_index("x"), lax.axis_index("y"), lax.axis_index("c")


def _other_chips(x, y):
    return [(1 - x, y), (x, 1 - y), (1 - x, 1 - y)]


def _all_gather(shards):
    n = len(shards)

    def body(*refs):
        ins, outs = refs[:n], refs[n:2 * n]
        send_sems, recv_sems, local_sems = refs[2 * n:]
        x, y, c = _coords()
        me, sibling = (x, y, c), (x, y, 1 - c)
        chips = _other_chips(x, y)

        def copy(i, k, block, to, src=None):
            bx, by, bc = block
            dst = outs[i].at[4 * bx + 2 * by + bc]
            return pltpu.make_async_remote_copy(
                src_ref=dst if src is None else src, dst_ref=dst, send_sem=send_sems.at[i, k],
                recv_sem=recv_sems.at[i, k], device_id=to, device_id_type=MESH)

        mine = [pltpu.make_async_copy(ins[i], outs[i].at[4 * x + 2 * y + c], local_sems.at[i]) for i in range(n)]
        for cp in mine:
            cp.start()
        sent = []
        for i in range(n):
            sent.append(copy(i, 0, me, sibling, src=ins[i]))
            sent += [copy(i, 1 + j, me, (*chip, c), src=ins[i]) for j, chip in enumerate(chips)]
        for cp in sent:
            cp.start()
        for i in range(n):
            for j, chip in enumerate(chips):
                copy(i, 1 + j, (*chip, c), me).wait_recv()
                fwd = copy(i, 4 + j, (*chip, c), sibling)
                fwd.start()
                sent.append(fwd)
        for i in range(n):
            copy(i, 0, sibling, me).wait_recv()
            for j, chip in enumerate(chips):
                copy(i, 4 + j, (*chip, 1 - c), me).wait_recv()
        for cp in sent:
            cp.wait_send()
        for cp in mine:
            cp.wait()

    return pl.pallas_call(
        body, name="weights_all_gather",
        out_shape=[jax.ShapeDtypeStruct((N_DEV,) + s.shape, s.dtype) for s in shards],
        in_specs=[ANY] * n, out_specs=[ANY] * n,
        scratch_shapes=[pltpu.SemaphoreType.DMA((n, 7)), pltpu.SemaphoreType.DMA((n, 7)),
                        pltpu.SemaphoreType.DMA((n,))],
    )(*shards)


def _pair_exchange(grads):
    n = len(grads)

    def body(*refs):
        ins, outs = refs[:n], refs[n:2 * n]
        send_sems, recv_sems = refs[2 * n:]
        x, y, c = _coords()
        cps = []
        for i in range(n):
            for k in range(4):
                cps.append(pltpu.make_async_remote_copy(
                    src_ref=ins[i].at[2 * k + 1 - c], dst_ref=outs[i].at[k], send_sem=send_sems.at[i, k],
                    recv_sem=recv_sems.at[i, k], device_id=(x, y, 1 - c), device_id_type=MESH))
        for cp in cps:
            cp.start()
        for cp in cps:
            cp.wait()

    return pl.pallas_call(
        body, name="grads_pair_exchange",
        out_shape=[jax.ShapeDtypeStruct((4,) + g.shape[1:], g.dtype) for g in grads],
        in_specs=[ANY] * n, out_specs=[ANY] * n,
        scratch_shapes=[pltpu.SemaphoreType.DMA((n, 4)), pltpu.SemaphoreType.DMA((n, 4))],
    )(*grads)


def _pair_add(grad, recv, core, name):
    _, r, cdim = grad.shape

    def body(c_ref, g_ref, r_ref, o_ref):
        o_ref[...] = (g_ref[...] + r_ref[...]).astype(BF16)

    blk = (1, r, cdim)
    return pl.pallas_call(
        body, name=name,
        grid_spec=pltpu.PrefetchScalarGridSpec(
            num_scalar_prefetch=1, grid=(4,),
            in_specs=[pl.BlockSpec(blk, lambda k, c_ref: (2 * k + c_ref[0], 0, 0)),
                      pl.BlockSpec(blk, lambda k, c_ref: (k, 0, 0))],
            out_specs=pl.BlockSpec(blk, lambda k, c_ref: (k, 0, 0))),
        out_shape=jax.ShapeDtypeStruct((4, r, cdim), BF16),
        compiler_params=_params("arbitrary"),
    )(core, grad, recv)


def _chip_exchange(parts):
    n = len(parts)

    def body(*refs):
        ins, outs = refs[:n], refs[n:2 * n]
        send_sems, recv_sems, local_sems = refs[2 * n:]
        x, y, c = _coords()
        chips = _other_chips(x, y)
        mine = [pltpu.make_async_copy(ins[i].at[2 * x + y], outs[i].at[2 * x + y], local_sems.at[i])
                for i in range(n)]
        for cp in mine:
            cp.start()
        sent = []
        for i in range(n):
            for j, (px, py) in enumerate(chips):
                sent.append(pltpu.make_async_remote_copy(
                    src_ref=ins[i].at[2 * px + py], dst_ref=outs[i].at[2 * x + y], send_sem=send_sems.at[i, j],
                    recv_sem=recv_sems.at[i, j], device_id=(px, py, c), device_id_type=MESH))
        for cp in sent:
            cp.start()
        for i in range(n):
            for j, (px, py) in enumerate(chips):
                landed = outs[i].at[2 * px + py]
                pltpu.make_async_remote_copy(
                    src_ref=landed, dst_ref=landed, send_sem=send_sems.at[i, j], recv_sem=recv_sems.at[i, j],
                    device_id=(px, py, c), device_id_type=MESH).wait_recv()
        for cp in sent:
            cp.wait_send()
        for cp in mine:
            cp.wait()

    return pl.pallas_call(
        body, name="grads_chip_exchange",
        out_shape=[jax.ShapeDtypeStruct(p.shape, p.dtype) for p in parts],
        in_specs=[ANY] * n, out_specs=[ANY] * n,
        scratch_shapes=[pltpu.SemaphoreType.DMA((n, 3)), pltpu.SemaphoreType.DMA((n, 3)),
                        pltpu.SemaphoreType.DMA((n,))],
    )(*parts)


def _all_reduce_small(v):
    r = v.shape[0]

    def body(v_ref, o_ref, buf, send_sems, recv_sems):
        x, y, c = _coords()
        me = 4 * x + 2 * y + c
        buf[me] = v_ref[...]
        cps = []
        for k in range(1, N_DEV):
            px = 1 - x if k & 4 else x
            py = 1 - y if k & 2 else y
            pc = 1 - c if k & 1 else c
            cps.append((pltpu.make_async_remote_copy(
                src_ref=v_ref, dst_ref=buf.at[me], send_sem=send_sems.at[k - 1], recv_sem=recv_sems.at[k - 1],
                device_id=(px, py, pc), device_id_type=MESH), 4 * px + 2 * py + pc))
        for cp, _ in cps:
            cp.start()
        for k, (cp, peer) in enumerate(cps):
            pltpu.make_async_remote_copy(
                src_ref=v_ref, dst_ref=buf.at[peer], send_sem=send_sems.at[k], recv_sem=recv_sems.at[k],
                device_id=(x, y, c), device_id_type=MESH).wait_recv()
        for cp, _ in cps:
            cp.wait_send()
        acc = buf[0]
        for j in range(1, N_DEV):
            acc = acc + buf[j]
        o_ref[...] = acc

    return pl.pallas_call(
        body, name="small_all_reduce", out_shape=jax.ShapeDtypeStruct(v.shape, F32),
        in_specs=[pl.BlockSpec(memory_space=pltpu.VMEM)], out_specs=pl.BlockSpec(memory_space=pltpu.VMEM),
        scratch_shapes=[pltpu.VMEM((N_DEV, r, 128), F32), pltpu.SemaphoreType.DMA((N_DEV - 1,)),
                        pltpu.SemaphoreType.DMA((N_DEV - 1,))],
    )(v)


def _adamw(w, m, v, g, name):
    r, cdim = w.shape
    parts = g.ndim == 3
    tr = r // 4 if r % 32 == 0 else r

    def body(w_ref, m_ref, v_ref, g_ref, go_ref, d_ref, mo_ref, vo_ref):
        if parts:
            gv = g_ref[0].astype(F32)
            for k in range(1, 4):
                gv = gv + g_ref[k].astype(F32)
        else:
            gv = g_ref[...]
        m2 = ADAM_B1 * m_ref[...] + (1.0 - ADAM_B1) * gv
        v2 = ADAM_B2 * v_ref[...] + (1.0 - ADAM_B2) * (gv * gv)
        m_hat = m2 / (1.0 - ADAM_B1 ** ADAM_STEP)
        v_hat = v2 / (1.0 - ADAM_B2 ** ADAM_STEP)
        go_ref[...] = gv
        d_ref[...] = -ADAM_LR * (m_hat / (jnp.sqrt(v_hat) + ADAM_EPS) + ADAM_WD * w_ref[...])
        mo_ref[...] = m2
        vo_ref[...] = v2

    row = pl.BlockSpec((tr, cdim), lambda i: (i, 0))
    g_spec = pl.BlockSpec((4, tr, cdim), lambda i: (0, i, 0)) if parts else row
    return pl.pallas_call(
        body, name=name, grid=(r // tr,), in_specs=[row, row, row, g_spec], out_specs=[row] * 4,
        out_shape=[jax.ShapeDtypeStruct((r, cdim), F32)] * 4,
        compiler_params=_params("parallel"),
    )(w, m, v, g)


WEIGHTS = ["ffn1_norm_g", "ffn1_w_gate", "ffn1_w_up", "ffn1_w_down", "mix_norm_g", "w_in", "attn_q_norm_g",
           "attn_k_norm_g", "attn_rel_bias", "hgrn_lower_bounds", "hgrn_out_norm_g", "w_out", "ffn2_norm_g",
           "ffn2_w_gate", "ffn2_w_up", "ffn2_w_down"]
COL_SHARDED = ("ffn1_w_gate", "ffn1_w_up", "w_in", "ffn2_w_gate", "ffn2_w_up")
ROW_SHARDED = ("ffn1_w_down", "w_out", "ffn2_w_down")
BIG = [n for n in WEIGHTS if n in COL_SHARDED or n in ROW_SHARDED]
SMALL = [n for n in WEIGHTS if n not in BIG]
PACK_ROWS = 8


def _pack_small(vals, loss=None):
    parts = []
    for n in SMALL:
        a = vals[n]
        if n == "attn_rel_bias":
            a = jnp.pad(a.reshape(ATTN_HEADS, N_REL), ((0, 0), (0, N_REL_PAD - N_REL)))
        flat = a.reshape(-1)
        size = -(-flat.shape[0] // (PACK_ROWS * 128)) * PACK_ROWS * 128
        parts.append(jnp.pad(flat, (0, size - flat.shape[0])).reshape(-1, 128))
    tail = jnp.zeros((PACK_ROWS, 128), F32)
    if loss is not None:
        tail = tail.at[0, 0].set(loss)
    return jnp.concatenate(parts + [tail], axis=0)


def _unpack_small(packed, shapes):
    out, row = {}, 0
    for n in SMALL:
        shape = shapes[n]
        if n == "attn_rel_bias":
            rows = ATTN_HEADS * N_REL_PAD // 128
            out[n] = packed[row:row + rows].reshape(ATTN_HEADS, N_REL_PAD)[:, :N_REL].reshape(shape)
        else:
            size = 1
            for s in shape:
                size *= s
            rows = -(-size // (PACK_ROWS * 128)) * PACK_ROWS
            out[n] = packed[row:row + rows].reshape(-1)[:size].reshape(shape)
        row += rows
    return out, packed[row, 0]


def kernel(x, ffn1_norm_g, ffn1_w_gate, ffn1_w_up, ffn1_w_down, mix_norm_g, w_in, attn_q_norm_g, attn_k_norm_g, attn_rel_bias, hgrn_lower_bounds, hgrn_out_norm_g, w_out, ffn2_norm_g, ffn2_w_gate, ffn2_w_up, ffn2_w_down, loss_target, m_ffn1_norm_g, m_ffn1_w_gate, m_ffn1_w_up, m_ffn1_w_down, m_mix_norm_g, m_w_in, m_attn_q_norm_g, m_attn_k_norm_g, m_attn_rel_bias, m_hgrn_lower_bounds, m_hgrn_out_norm_g, m_w_out, m_ffn2_norm_g, m_ffn2_w_gate, m_ffn2_w_up, m_ffn2_w_down, v_ffn1_norm_g, v_ffn1_w_gate, v_ffn1_w_up, v_ffn1_w_down, v_mix_norm_g, v_w_in, v_attn_q_norm_g, v_attn_k_norm_g, v_attn_rel_bias, v_hgrn_lower_bounds, v_hgrn_out_norm_g, v_w_out, v_ffn2_norm_g, v_ffn2_w_gate, v_ffn2_w_up, v_ffn2_w_down):
    wts = dict(zip(WEIGHTS, (ffn1_norm_g, ffn1_w_gate, ffn1_w_up, ffn1_w_down, mix_norm_g, w_in, attn_q_norm_g,
                             attn_k_norm_g, attn_rel_bias, hgrn_lower_bounds, hgrn_out_norm_g, w_out, ffn2_norm_g,
                             ffn2_w_gate, ffn2_w_up, ffn2_w_down)))
    mom = dict(zip(WEIGHTS, (m_ffn1_norm_g, m_ffn1_w_gate, m_ffn1_w_up, m_ffn1_w_down, m_mix_norm_g, m_w_in,
                             m_attn_q_norm_g, m_attn_k_norm_g, m_attn_rel_bias, m_hgrn_lower_bounds,
                             m_hgrn_out_norm_g, m_w_out, m_ffn2_norm_g, m_ffn2_w_gate, m_ffn2_w_up, m_ffn2_w_down)))
    var = dict(zip(WEIGHTS, (v_ffn1_norm_g, v_ffn1_w_gate, v_ffn1_w_up, v_ffn1_w_down, v_mix_norm_g, v_w_in,
                             v_attn_q_norm_g, v_attn_k_norm_g, v_attn_rel_bias, v_hgrn_lower_bounds,
                             v_hgrn_out_norm_g, v_w_out, v_ffn2_norm_g, v_ffn2_w_gate, v_ffn2_w_up, v_ffn2_w_down)))
    nb, seq, d = x.shape
    shapes = {n: wts[n].shape for n in WEIGHTS}

    gathered = _all_gather([wts[n][0].astype(BF16) for n in BIG])
    full = {}
    for n, g in zip(BIG, gathered):
        if n in COL_SHARDED:
            full[n] = jnp.transpose(g, (1, 0, 2)).reshape(g.shape[1], N_DEV * g.shape[2])
        else:
            full[n] = g.reshape(N_DEV * g.shape[1], g.shape[2])

    sp = {n: wts[n] for n in SMALL}
    sp["attn_rel_bias"] = wts["attn_rel_bias"][0]
    loss, dx, dsmall, dbig = _local_step(x.reshape(nb * seq, d), loss_target.reshape(nb * seq, d), sp, full, nb, seq)

    small_sum = _all_reduce_small(_pack_small(dsmall, loss))
    gsmall, loss_total = _unpack_small(small_sum, shapes)

    sliced = []
    for n in BIG:
        g = dbig[n]
        r, c = shapes[n][1], shapes[n][2]
        if n in COL_SHARDED:
            sliced.append(jnp.transpose(g.reshape(r, N_DEV, c), (1, 0, 2)))
        else:
            sliced.append(g.reshape(N_DEV, r, c))
    from_sibling = _pair_exchange(sliced)
    core = lax.axis_index("c").astype(jnp.int32).reshape(1)
    partials = [_pair_add(g, r, core, n + "_pair_add") for n, g, r in zip(BIG, sliced, from_sibling)]
    reduced = dict(zip(BIG, _chip_exchange(partials)))

    grads, deltas, new_m, new_v = {}, {}, {}, {}
    for n in BIG:
        out = _adamw(wts[n][0], mom[n][0], var[n][0], reduced[n], n + "_adamw")
        grads[n], deltas[n], new_m[n], new_v[n] = (o.reshape(shapes[n]) for o in out)
    packed = _adamw(_pack_small(wts), _pack_small(mom), _pack_small(var), small_sum, "small_adamw")
    for dst, p in zip((deltas, new_m, new_v), packed[1:]):
        dst.update(_unpack_small(p, shapes)[0])
    grads.update(gsmall)

    return (loss_total, dx.reshape(nb, seq, d), *[grads[n] for n in WEIGHTS], *[deltas[n] for n in WEIGHTS],
            *[new_m[n] for n in WEIGHTS], *[new_v[n] for n in WEIGHTS])
```

```python
import functools

import jax
import jax.numpy as jnp
from jax import lax
from jax.experimental import pallas as pl
from jax.experimental.pallas import tpu as pltpu

F32 = jnp.float32
BF16 = jnp.bfloat16

RMS_EPS = 1e-6
CHUNK = 64
LEFT_CHUNKS = 8
BAND = (LEFT_CHUNKS + 2) * CHUNK
KPAD = BAND - CHUNK
REL_CLIP = 128
N_REL = 2 * REL_CLIP + 1
N_REL_PAD = 384
ATTN_HEADS = 8
ATTN_HEAD_DIM = 64
ATTN_WIDTH = ATTN_HEADS * ATTN_HEAD_DIM
ATTN_UNROLL = 4
HGRN_HEADS = 4
HGRN_HEAD_DIM = 128
SUB = 16
N_SUB = CHUNK // SUB
N_DEV = 8

ADAM_LR = 0.001
ADAM_B1 = 0.9
ADAM_B2 = 0.999
ADAM_EPS = 1e-08
ADAM_WD = 0.01
ADAM_STEP = 10

VMEM_LIMIT = 56 * 1024 * 1024

NT = (((1,), (1,)), ((), ()))
NN = (((1,), (0,)), ((), ()))


def _params(*sem):
    return pltpu.CompilerParams(dimension_semantics=sem, vmem_limit_bytes=VMEM_LIMIT)


def _sigmoid(v):
    return 1.0 / (1.0 + jnp.exp(-v))


def _dot(a, b, dims=NN):
    return lax.dot_general(a.astype(BF16), b.astype(BF16), dims, preferred_element_type=F32)


def _dot_exact01(m01, v):
    m = m01.astype(BF16)
    hi = v.astype(BF16)
    r1 = v - hi.astype(F32)
    mid = r1.astype(BF16)
    lo = (r1 - mid.astype(F32)).astype(BF16)
    out = lax.dot_general(m, hi, NN, preferred_element_type=F32)
    out = out + lax.dot_general(m, mid, NN, preferred_element_type=F32)
    return out + lax.dot_general(m, lo, NN, preferred_element_type=F32)


def _dot_exact01_r(v, m01):
    m = m01.astype(BF16)
    hi = v.astype(BF16)
    r1 = v - hi.astype(F32)
    mid = r1.astype(BF16)
    lo = (r1 - mid.astype(F32)).astype(BF16)
    out = lax.dot_general(hi, m, NN, preferred_element_type=F32)
    out = out + lax.dot_general(mid, m, NN, preferred_element_type=F32)
    return out + lax.dot_general(lo, m, NN, preferred_element_type=F32)


def _tn(a, b):
    ap = jnp.concatenate([a, jnp.zeros_like(a)], axis=0)
    bp = jnp.concatenate([b, jnp.zeros_like(b)], axis=0)
    return _dot(ap.T, bp)


def _row_tile(t):
    for tm in (512, 256, 128, 64, 32, 16, 8):
        if t % tm == 0:
            return tm
    raise ValueError(t)


def _rms_fwd(x, g, name):
    t, d = x.shape
    tm = _row_tile(t)

    def body(x_ref, g_ref, h_ref):
        xv = x_ref[...]
        r = lax.rsqrt(jnp.mean(xv * xv, axis=-1, keepdims=True) + RMS_EPS)
        h_ref[...] = (xv * r * g_ref[...]).astype(BF16)

    return pl.pallas_call(
        body, name=name, grid=(t // tm,),
        in_specs=[pl.BlockSpec((tm, d), lambda i: (i, 0)), pl.BlockSpec((1, d), lambda i: (0, 0))],
        out_specs=pl.BlockSpec((tm, d), lambda i: (i, 0)),
        out_shape=jax.ShapeDtypeStruct((t, d), BF16),
        compiler_params=_params("parallel"),
    )(x, g)


def _rms_bwd(x, g, dh, dres, name):
    t, d = x.shape
    tm = _row_tile(t)

    def body(x_ref, g_ref, dh_ref, dres_ref, dx_ref, dx16_ref, dg_ref):
        xv = x_ref[...]
        r = lax.rsqrt(jnp.mean(xv * xv, axis=-1, keepdims=True) + RMS_EPS)
        xhat = xv * r
        dhv = dh_ref[...]
        gd = dhv * g_ref[...]
        dx = dres_ref[...] + r * (gd - xhat * jnp.mean(gd * xhat, axis=-1, keepdims=True))
        dx_ref[...] = dx
        dx16_ref[...] = dx.astype(BF16)
        part = jnp.sum(dhv * xhat, axis=0, keepdims=True)

        @pl.when(pl.program_id(0) == 0)
        def _():
            dg_ref[...] = part

        @pl.when(pl.program_id(0) > 0)
        def _():
            dg_ref[...] += part

    row = pl.BlockSpec((tm, d), lambda i: (i, 0))
    vec = pl.BlockSpec((1, d), lambda i: (0, 0))
    return pl.pallas_call(
        body, name=name, grid=(t // tm,),
        in_specs=[row, vec, row, row], out_specs=[row, row, vec],
        out_shape=[jax.ShapeDtypeStruct((t, d), F32), jax.ShapeDtypeStruct((t, d), BF16),
                   jax.ShapeDtypeStruct((1, d), F32)],
        compiler_params=_params("arbitrary"),
    )(x, g, dh, dres)


def _mm(a, b, *, ta=False, tb=False, tm, tn, out_dtype=F32, add=None, scale=1.0, name):
    m, k = (a.shape[1], a.shape[0]) if ta else a.shape
    n = b.shape[0] if tb else b.shape[1]
    tm, tn = min(tm, m), min(tn, n)
    assert m % tm == 0 and n % tn == 0, (m, n, tm, tn)
    dims = (((0 if ta else 1,), (1 if tb else 0,)), ((), ()))

    def body(*refs):
        a_ref, b_ref = refs[0], refs[1]
        o_ref = refs[-1]
        r = lax.dot_general(a_ref[...].astype(BF16), b_ref[...].astype(BF16), dims, preferred_element_type=F32)
        if scale != 1.0:
            r = r * scale
        if add is not None:
            r = r + refs[2][...]
        o_ref[...] = r.astype(out_dtype)

    a_spec = pl.BlockSpec((k, tm), lambda i, j: (0, i)) if ta else pl.BlockSpec((tm, k), lambda i, j: (i, 0))
    b_spec = pl.BlockSpec((tn, k), lambda i, j: (j, 0)) if tb else pl.BlockSpec((k, tn), lambda i, j: (0, j))
    o_spec = pl.BlockSpec((tm, tn), lambda i, j: (i, j))
    ins, specs = [a, b], [a_spec, b_spec]
    if add is not None:
        ins.append(add)
        specs.append(o_spec)
    return pl.pallas_call(
        body, name=name, grid=(m // tm, n // tn), in_specs=specs, out_specs=o_spec,
        out_shape=jax.ShapeDtypeStruct((m, n), out_dtype),
        compiler_params=_params("parallel", "parallel"),
    )(*ins)


def _ffn_tile(f):
    for tf in (1408, 512, 256, 128):
        if f % tf == 0:
            return tf
    raise ValueError(f)


def _ffn_fwd(h, x, wg, wu, wd, name):
    t, d = x.shape
    f = wg.shape[1]
    tm, tf = _row_tile(t), _ffn_tile(f)
    nf = f // tf

    def body(h_ref, x_ref, wg_ref, wu_ref, wd_ref, y_ref, g_ref, u_ref, acc_ref):
        j = pl.program_id(1)
        hv = h_ref[...]
        gv = lax.dot_general(hv, wg_ref[...], NN, preferred_element_type=F32)
        uv = lax.dot_general(hv, wu_ref[...], NN, preferred_element_type=F32)
        av = gv * _sigmoid(gv) * uv
        g_ref[...] = gv.astype(BF16)
        u_ref[...] = uv.astype(BF16)
        part = lax.dot_general(av.astype(BF16), wd_ref[...], NN, preferred_element_type=F32)

        @pl.when(j == 0)
        def _():
            acc_ref[...] = part

        @pl.when(j > 0)
        def _():
            acc_ref[...] += part

        @pl.when(j == nf - 1)
        def _():
            y_ref[...] = x_ref[...] + 0.5 * acc_ref[...]

    row = pl.BlockSpec((tm, d), lambda i, j: (i, 0))
    hid = pl.BlockSpec((tm, tf), lambda i, j: (i, j))
    return pl.pallas_call(
        body, name=name, grid=(t // tm, nf),
        in_specs=[row, row, pl.BlockSpec((d, tf), lambda i, j: (0, j)), pl.BlockSpec((d, tf), lambda i, j: (0, j)),
                  pl.BlockSpec((tf, d), lambda i, j: (j, 0))],
        out_specs=[row, hid, hid],
        out_shape=[jax.ShapeDtypeStruct((t, d), F32), jax.ShapeDtypeStruct((t, f), BF16),
                   jax.ShapeDtypeStruct((t, f), BF16)],
        scratch_shapes=[pltpu.VMEM((tm, d), F32)],
        compiler_params=_params("parallel", "arbitrary"),
    )(h, x, wg, wu, wd)


def _ffn_bwd_mid(dy, wd, g, u, name):
    t, d = dy.shape
    f = wd.shape[0]
    tm, tf = _row_tile(t), _ffn_tile(f)

    def body(dy_ref, wd_ref, g_ref, u_ref, dg_ref, du_ref, a_ref):
        da = 0.5 * lax.dot_general(dy_ref[...].astype(BF16), wd_ref[...], NT, preferred_element_type=F32)
        gv = g_ref[...].astype(F32)
        uv = u_ref[...].astype(F32)
        s = _sigmoid(gv)
        silu = gv * s
        dg_ref[...] = (da * uv * (s * (1.0 + gv * (1.0 - s)))).astype(BF16)
        du_ref[...] = (da * silu).astype(BF16)
        a_ref[...] = (silu * uv).astype(BF16)

    hid = pl.BlockSpec((tm, tf), lambda i, j: (i, j))
    return pl.pallas_call(
        body, name=name, grid=(t // tm, f // tf),
        in_specs=[pl.BlockSpec((tm, d), lambda i, j: (i, 0)), pl.BlockSpec((tf, d), lambda i, j: (j, 0)), hid, hid],
        out_specs=[hid, hid, hid],
        out_shape=[jax.ShapeDtypeStruct((t, f), BF16)] * 3,
        compiler_params=_params("parallel", "parallel"),
    )(dy, wd, g, u)


def _loss(y, tgt, name):
    t, d = y.shape
    tm = _row_tile(t)

    def body(y_ref, t_ref, dy_ref, dy16_ref, sq_ref):
        e = y_ref[...] - t_ref[...]
        dy = e * (1.0 / d)
        dy_ref[...] = dy
        dy16_ref[...] = dy.astype(BF16)
        part = jnp.sum(e * e, axis=0, keepdims=True)

        @pl.when(pl.program_id(0) == 0)
        def _():
            sq_ref[...] = part

        @pl.when(pl.program_id(0) > 0)
        def _():
            sq_ref[...] += part

    row = pl.BlockSpec((tm, d), lambda i: (i, 0))
    vec = pl.BlockSpec((1, d), lambda i: (0, 0))
    return pl.pallas_call(
        body, name=name, grid=(t // tm,), in_specs=[row, row], out_specs=[row, row, vec],
        out_shape=[jax.ShapeDtypeStruct((t, d), F32), jax.ShapeDtypeStruct((t, d), BF16),
                   jax.ShapeDtypeStruct((1, d), F32)],
        compiler_params=_params("arbitrary"),
    )(y, tgt)


def _rel_index(t, s_band):
    return jnp.clip(t + KPAD - s_band, -REL_CLIP, REL_CLIP) + REL_CLIP


def _bias_expand(rel_bias_pad):
    nh = rel_bias_pad.shape[0]

    def body(rb_ref, out_ref):
        rb = rb_ref[...]
        i_io = lax.broadcasted_iota(jnp.int32, (N_REL_PAD, BAND), 0)
        s_io = lax.broadcasted_iota(jnp.int32, (N_REL_PAD, BAND), 1)

        def row(t, carry):
            onehot = (i_io == _rel_index(t, s_io)).astype(F32)
            out_ref[t] = _dot_exact01_r(rb, onehot)
            return carry

        lax.fori_loop(0, CHUNK, row, 0)

    return pl.pallas_call(
        body, name="bias_expand", out_shape=jax.ShapeDtypeStruct((CHUNK, nh, BAND), F32),
        compiler_params=pltpu.CompilerParams(vmem_limit_bytes=VMEM_LIMIT),
    )(rel_bias_pad)


def _bias_fold(dbias):
    ng, nh = dbias.shape[0], dbias.shape[2]

    def body(db_ref, out_ref):
        s_io = lax.broadcasted_iota(jnp.int32, (BAND, N_REL_PAD), 0)
        i_io = lax.broadcasted_iota(jnp.int32, (BAND, N_REL_PAD), 1)

        def row(t, acc):
            onehot = (i_io == _rel_index(t, s_io)).astype(F32)
            d = db_ref[0, t]
            for gi in range(1, ng):
                d = d + db_ref[gi, t]
            return acc + _dot_exact01_r(d, onehot)

        out_ref[...] = lax.fori_loop(0, CHUNK, row, jnp.zeros((nh, N_REL_PAD), F32))

    return pl.pallas_call(
        body, name="bias_fold", out_shape=jax.ShapeDtypeStruct((nh, N_REL_PAD), F32),
        compiler_params=pltpu.CompilerParams(vmem_limit_bytes=VMEM_LIMIT),
    )(dbias)


def _left_half(shape):
    return lax.broadcasted_iota(jnp.int32, shape, len(shape) - 1) < ATTN_HEAD_DIM


def _stack_heads(v):
    left = _left_half(v.shape)
    zero = jnp.zeros_like(v)
    return jnp.concatenate([jnp.where(left, v, zero), jnp.where(left, zero, v)], axis=0)


def _unstack_heads(v):
    return jnp.where(_left_half((CHUNK, 128)), v[0:CHUNK, :], v[CHUNK:2 * CHUNK, :])


def _half_mean(v):
    r = lax.broadcasted_iota(jnp.int32, (128, 128), 0) < ATTN_HEAD_DIM
    c = lax.broadcasted_iota(jnp.int32, (128, 128), 1) < ATTN_HEAD_DIM
    return _dot_exact01_r(v, r == c) * (1.0 / ATTN_HEAD_DIM)


def _attn_prepare(q_ref, k_ref, v_ref, gq_ref, gk_ref, qs_scr, k_scr, v_scr):
    q, k = q_ref[...], k_ref[...]
    rq = lax.rsqrt(_half_mean(q * q) + RMS_EPS)
    rk = lax.rsqrt(_half_mean(k * k) + RMS_EPS)
    qhat, khat = q * rq, k * rk
    qs_scr[...] = (qhat * gq_ref[...] * ATTN_HEAD_DIM ** -0.5).astype(BF16)
    k_scr[0:KPAD, :] = jnp.zeros((KPAD, 128), BF16)
    v_scr[0:KPAD, :] = jnp.zeros((KPAD, 128), BF16)
    k_scr[KPAD:, :] = (khat * gk_ref[...]).astype(BF16)
    v_scr[KPAD:, :] = v_ref[...].astype(BF16)
    return qhat, rq, khat, rk


def _attn_scores(qs_scr, k_scr, bias_ref, c):
    r0 = pl.multiple_of(c * CHUNK, CHUNK)
    qst = _stack_heads(qs_scr[pl.ds(r0, CHUNK), :])
    kb = k_scr[pl.ds(r0, BAND), :]
    s = lax.dot_general(qst, kb, NT, preferred_element_type=F32) + bias_ref[...]
    col = lax.broadcasted_iota(jnp.int32, (2 * CHUNK, BAND), 1)
    first = jnp.maximum(CHUNK, (LEFT_CHUNKS + 1 - c) * CHUNK)
    s = jnp.where(col >= first, s, -jnp.inf)
    e = jnp.exp(s - jnp.max(s, axis=-1, keepdims=True))
    return e, 1.0 / jnp.sum(e, axis=-1, keepdims=True), qst, kb, r0


def _attn_fwd(proj, bias, gq, gk, nb, seq):
    nc = seq // CHUNK

    def body(q_ref, k_ref, v_ref, bias_ref, gq_ref, gk_ref, o_ref, qs_scr, k_scr, v_scr):
        _attn_prepare(q_ref, k_ref, v_ref, gq_ref, gk_ref, qs_scr, k_scr, v_scr)

        def chunk(c, carry):
            e, inv, _, _, r0 = _attn_scores(qs_scr, k_scr, bias_ref, c)
            vb = v_scr[pl.ds(r0, BAND), :]
            o_ref[pl.ds(r0, CHUNK), :] = _unstack_heads(
                lax.dot_general(e.astype(BF16), vb, NN, preferred_element_type=F32) * inv)
            return carry

        lax.fori_loop(0, nc, chunk, 0, unroll=ATTN_UNROLL)

    def col(off):
        return pl.BlockSpec((seq, 128), lambda b, hp: (b, off + hp))

    vec = pl.BlockSpec((1, 128), lambda b, hp: (0, 0))
    return pl.pallas_call(
        body, name="attn_fwd", grid=(nb, ATTN_HEADS // 2),
        in_specs=[col(0), col(4), col(8), pl.BlockSpec((2 * CHUNK, BAND), lambda b, hp: (hp, 0)), vec, vec],
        out_specs=pl.BlockSpec((seq, 128), lambda b, hp: (b, hp)),
        out_shape=jax.ShapeDtypeStruct((nb * seq, ATTN_WIDTH), F32),
        scratch_shapes=[pltpu.VMEM((seq, 128), BF16), pltpu.VMEM((seq + KPAD, 128), BF16),
                        pltpu.VMEM((seq + KPAD, 128), BF16)],
        compiler_params=_params("parallel", "parallel"),
    )(proj, proj, proj, bias, gq, gk)


def _attn_bwd(proj, out, dout, bias, gq, gk, nb, seq):
    nc = seq // CHUNK
    scale = ATTN_HEAD_DIM ** -0.5

    def body(q_ref, k_ref, v_ref, o_ref, do_ref, bias_ref, gq_ref, gk_ref,
             dq_ref, dk_ref, dv_ref, dbias_ref, dgq_ref, dgk_ref,
             qs_scr, k_scr, v_scr, dqn_scr, dk_scr, dv_scr, db_scr):
        qhat, rq, khat, rk = _attn_prepare(q_ref, k_ref, v_ref, gq_ref, gk_ref, qs_scr, k_scr, v_scr)
        dk_scr[...] = jnp.zeros_like(dk_scr)
        dv_scr[...] = jnp.zeros_like(dv_scr)
        db_scr[...] = jnp.zeros_like(db_scr)

        def chunk(c, carry):
            e, inv, qst, kb, r0 = _attn_scores(qs_scr, k_scr, bias_ref, c)
            p = e * inv
            vb = v_scr[pl.ds(r0, BAND), :]
            do_c = do_ref[pl.ds(r0, CHUNK), :]
            dost = _stack_heads(do_c)
            drow = jnp.sum(dost * _stack_heads(o_ref[pl.ds(r0, CHUNK), :]), axis=-1, keepdims=True)
            dp = lax.dot_general(dost.astype(BF16), vb, NT, preferred_element_type=F32)
            ds = p * (dp - drow)
            db_scr[...] += ds
            dqn_scr[pl.ds(r0, CHUNK), :] = scale * _unstack_heads(
                lax.dot_general(ds.astype(BF16), kb, NN, preferred_element_type=F32))
            dk_scr[pl.ds(r0, BAND), :] += lax.dot_general(ds.T.astype(BF16), qst, NN, preferred_element_type=F32)
            dv_scr[pl.ds(r0, BAND), :] += _dot(p.T, dost)
            return carry

        lax.fori_loop(0, nc, chunk, 0, unroll=ATTN_UNROLL)

        def norm_bwd(dn, hat, r, g_ref):
            gd = dn * g_ref[...]
            return r * (gd - hat * _half_mean(gd * hat)), jnp.sum(dn * hat, axis=0, keepdims=True)

        dq, dgq = norm_bwd(dqn_scr[...], qhat, rq, gq_ref)
        dk, dgk = norm_bwd(dk_scr[KPAD:, :], khat, rk, gk_ref)
        dq_ref[...] = dq.astype(BF16)
        dk_ref[...] = dk.astype(BF16)
        dv_ref[...] = dv_scr[KPAD:, :].astype(BF16)
        dbias_ref[0] = db_scr[...]
        dgq_ref[0] = dgq
        dgk_ref[0] = dgk

    def col(off):
        return pl.BlockSpec((seq, 128), lambda b, hp: (b, off + hp))

    vec = pl.BlockSpec((1, 128), lambda b, hp: (0, 0))
    gvec = pl.BlockSpec((1, 1, 128), lambda b, hp: (b * (ATTN_HEADS // 2) + hp, 0, 0))
    t = nb * seq
    return pl.pallas_call(
        body, name="attn_bwd", grid=(nb, ATTN_HEADS // 2),
        in_specs=[col(0), col(4), col(8), col(0), col(0),
                  pl.BlockSpec((2 * CHUNK, BAND), lambda b, hp: (hp, 0)), vec, vec],
        out_specs=[col(0), col(0), col(0), pl.BlockSpec((1, 2 * CHUNK, BAND), lambda b, hp: (b, hp, 0)),
                   gvec, gvec],
        out_shape=[jax.ShapeDtypeStruct((t, ATTN_WIDTH), BF16)] * 3
        + [jax.ShapeDtypeStruct((nb, ATTN_HEADS * CHUNK, BAND), F32)]
        + [jax.ShapeDtypeStruct((nb * ATTN_HEADS // 2, 1, 128), F32)] * 2,
        scratch_shapes=[pltpu.VMEM((seq, 128), BF16), pltpu.VMEM((seq + KPAD, 128), BF16),
                        pltpu.VMEM((seq + KPAD, 128), BF16), pltpu.VMEM((seq, 128), F32),
                        pltpu.VMEM((seq + KPAD, 128), F32), pltpu.VMEM((seq + KPAD, 128), F32),
                        pltpu.VMEM((2 * CHUNK, BAND), F32)],
        compiler_params=_params("parallel", "parallel"),
    )(proj, proj, proj, out, dout, bias, gq, gk)


def _tri(lower):
    r = lax.broadcasted_iota(jnp.int32, (CHUNK, CHUNK), 0)
    c = lax.broadcasted_iota(jnp.int32, (CHUNK, CHUNK), 1)
    return (r >= c) if lower else (r <= c)


def _hgrn_gates(hq, hf, lb):
    sq = _sigmoid(hq)
    sf = _sigmoid(hf)
    return hq * sq, sq, sf, lb + (1.0 - lb) * sf


def _hgrn_offdiag(q_s, k_s, b_s):
    row = lax.broadcasted_iota(jnp.int32, (CHUNK, HGRN_HEAD_DIM), 0)
    bv, qv, kv = b_s[...], q_s[...], k_s[...]
    eqs, eks = [], []
    for i in range(1, N_SUB):
        r = b_s[pl.ds(SUB * i - 1, 1), :]
        in_i = (row >= SUB * i) & (row < SUB * (i + 1))
        eqs.append(jnp.exp(jnp.where(in_i, bv - r, -jnp.inf)))
        eks.append(jnp.exp(jnp.where(row < SUB * i, r - bv, -jnp.inf)))
    eq = jnp.concatenate(eqs, axis=1)
    ek = jnp.concatenate(eks, axis=1)
    qt = jnp.concatenate([qv] * (N_SUB - 1), axis=1) * eq
    kt = jnp.concatenate([kv] * (N_SUB - 1), axis=1) * ek
    return qt, kt, eq, ek


def _hgrn_diag_e(b_s, i, s):
    t_io = lax.broadcasted_iota(jnp.int32, (SUB, HGRN_HEAD_DIM), 0)
    bi = b_s[pl.ds(SUB * i, SUB), :]
    return jnp.exp(jnp.where(t_io >= s, bi - b_s[pl.ds(SUB * i + s, 1), :], -jnp.inf)), t_io


def _hgrn_intra(q_s, k_s, b_s, a_s, qt, kt):
    ktp = jnp.concatenate([kt, jnp.zeros_like(kt)], axis=0)
    a_s[...] = _dot(qt, ktp, NT)
    col = lax.broadcasted_iota(jnp.int32, (SUB, HGRN_HEAD_DIM), 1)
    for i in range(N_SUB):
        qi = q_s[pl.ds(SUB * i, SUB), :]
        ai = jnp.zeros((SUB, HGRN_HEAD_DIM), F32)
        for s in range(SUB):
            e, _ = _hgrn_diag_e(b_s, i, s)
            a_col = jnp.sum(qi * k_s[pl.ds(SUB * i + s, 1), :] * e, axis=-1, keepdims=True)
            ai = ai + jnp.where(col == SUB * i + s, a_col, 0.0)
        a_s[pl.ds(SUB * i, SUB), :] += ai


def _hgrn_fwd(proj, lb, go, nb, seq):
    nc = seq // CHUNK
    hd = HGRN_HEAD_DIM

    def body(hq_ref, hf_ref, hi_ref, hg_ref, lb_ref, go_ref, y_ref, o_ref, st_ref, st, q_s, k_s, b_s, a_s):
        st[...] = jnp.zeros_like(st)
        lower = _tri(True)

        def chunk(c, carry):
            r0 = pl.multiple_of(c * CHUNK, CHUNK)
            rows = pl.ds(r0, CHUNK)
            q, _, _, f = _hgrn_gates(hq_ref[rows, :], hf_ref[rows, :], lb_ref[...])
            v = hi_ref[rows, :]
            b = _dot_exact01(lower, jnp.log(f))
            q_s[...] = q
            k_s[...] = 1.0 - f
            b_s[...] = b
            st_ref[0, c] = st[...]
            qt, kt, _, _ = _hgrn_offdiag(q_s, k_s, b_s)
            _hgrn_intra(q_s, k_s, b_s, a_s, qt, kt)
            vp = jnp.concatenate([v, jnp.zeros_like(v)], axis=0)
            o = _dot(a_s[...], vp) + _dot(q * jnp.exp(b), st[...], NT)
            bl = b_s[pl.ds(CHUNK - 1, 1), :]
            st[...] = st[...] * jnp.exp(bl) + _tn(v, (1.0 - f) * jnp.exp(bl - b))
            o_ref[rows, :] = o
            n = o * lax.rsqrt(jnp.mean(o * o, axis=-1, keepdims=True) + RMS_EPS) * go_ref[...]
            hg = hg_ref[rows, :]
            y_ref[rows, :] = n * hg * _sigmoid(hg)
            return carry

        lax.fori_loop(0, nc, chunk, 0)

    def col(off):
        return pl.BlockSpec((seq, hd), lambda b, h: (b, off + h))

    out = pl.BlockSpec((seq, hd), lambda b, h: (b, h))
    t = nb * seq
    return pl.pallas_call(
        body, name="hgrn_fwd", grid=(nb, HGRN_HEADS),
        in_specs=[col(12), col(16), col(20), col(24), pl.BlockSpec((1, hd), lambda b, h: (0, h)),
                  pl.BlockSpec((1, hd), lambda b, h: (0, 0))],
        out_specs=[out, out, pl.BlockSpec((1, nc, hd, hd), lambda b, h: (b * HGRN_HEADS + h, 0, 0, 0))],
        out_shape=[jax.ShapeDtypeStruct((t, HGRN_HEADS * hd), F32)] * 2
        + [jax.ShapeDtypeStruct((nb * HGRN_HEADS, nc, hd, hd), F32)],
        scratch_shapes=[pltpu.VMEM((hd, hd), F32)] + [pltpu.VMEM((CHUNK, hd), F32)] * 4,
        compiler_params=_params("parallel", "parallel"),
    )(proj, proj, proj, proj, lb, go)


def _hgrn_bwd(proj, lb, go, o_pre, states, dout, nb, seq):
    nc = seq // CHUNK
    hd = HGRN_HEAD_DIM

    def body(hq_ref, hf_ref, hi_ref, hg_ref, lb_ref, go_ref, o_ref, st_ref, dy_ref,
             dhq_ref, dhf_ref, dhi_ref, dhg_ref, dlb_ref, dgo_ref,
             dst, q_s, k_s, b_s, a_s, da_s, dqi_s, dki_s, dlb_acc, dgo_acc):
        dst[...] = jnp.zeros_like(dst)
        dlb_acc[...] = jnp.zeros_like(dlb_acc)
        dgo_acc[...] = jnp.zeros_like(dgo_acc)
        lower, upper = _tri(True), _tri(False)
        lbv, gov = lb_ref[...], go_ref[...]
        row = lax.broadcasted_iota(jnp.int32, (CHUNK, hd), 0)

        def chunk(it, carry):
            c = nc - 1 - it
            r0 = pl.multiple_of(c * CHUNK, CHUNK)
            rows = pl.ds(r0, CHUNK)
            hq, hf, v, hg = hq_ref[rows, :], hf_ref[rows, :], hi_ref[rows, :], hg_ref[rows, :]
            q, sq, sf, f = _hgrn_gates(hq, hf, lbv)
            kk = 1.0 - f
            b = _dot_exact01(lower, jnp.log(f))
            q_s[...] = q
            k_s[...] = kk
            b_s[...] = b
            bl = b_s[pl.ds(CHUNK - 1, 1), :]
            ebl = jnp.exp(bl)
            ekd = jnp.exp(bl - b)
            kd = kk * ekd
            eb = jnp.exp(b)
            qb = q * eb
            st0 = st_ref[0, c]
            dst1 = dst[...]

            o = o_ref[rows, :]
            dy = dy_ref[rows, :]
            sg = _sigmoid(hg)
            rstd = lax.rsqrt(jnp.mean(o * o, axis=-1, keepdims=True) + RMS_EPS)
            ohat = o * rstd
            dn = dy * hg * sg
            dhg_ref[rows, :] = (dy * ohat * gov * (sg * (1.0 + hg * (1.0 - sg)))).astype(BF16)
            dgo_acc[...] += jnp.sum(dn * ohat, axis=0, keepdims=True)
            gdn = dn * gov
            do = rstd * (gdn - ohat * jnp.mean(gdn * ohat, axis=-1, keepdims=True))

            qt, kt, eq, ek = _hgrn_offdiag(q_s, k_s, b_s)
            _hgrn_intra(q_s, k_s, b_s, a_s, qt, kt)
            da = _dot(do, v, NT)
            dat = _dot(v, do, NT)
            da_s[...] = da
            dqo = _dot(da, kt) * eq
            dko = _dot(dat, qt) * ek
            dqi_s[...] = dqo[:, 0:hd] + dqo[:, hd:2 * hd] + dqo[:, 2 * hd:3 * hd]
            dki_s[...] = dko[:, 0:hd] + dko[:, hd:2 * hd] + dko[:, 2 * hd:3 * hd]
            col = lax.broadcasted_iota(jnp.int32, (SUB, CHUNK), 1)
            for i in range(N_SUB):
                qi = q_s[pl.ds(SUB * i, SUB), :]
                dai = da_s[pl.ds(SUB * i, SUB), :]
                dqd = jnp.zeros((SUB, hd), F32)
                dkd_ = jnp.zeros((SUB, hd), F32)
                for s in range(SUB):
                    e, t_io = _hgrn_diag_e(b_s, i, s)
                    dacol = jnp.sum(jnp.where(col == SUB * i + s, dai, 0.0), axis=-1, keepdims=True)
                    w = dacol * e
                    dqd = dqd + w * k_s[pl.ds(SUB * i + s, 1), :]
                    dkd_ = dkd_ + jnp.where(t_io == s, jnp.sum(w * qi, axis=0, keepdims=True), 0.0)
                dqi_s[pl.ds(SUB * i, SUB), :] += dqd
                dki_s[pl.ds(SUB * i, SUB), :] += dkd_
            dqi, dki = dqi_s[...], dki_s[...]

            dv = _tn(a_s[...], do)[0:CHUNK, :] + _dot(kd, dst1, NT)
            dqb = _dot(do, st0)
            dkd = _dot(v, dst1)
            t2 = dkd * kd
            dq = dqb * eb + dqi
            dk = dkd * ekd + dki
            dbl = jnp.sum(t2, axis=0, keepdims=True) + ebl * jnp.sum(st0 * dst1, axis=0, keepdims=True)
            db = dqb * qb - t2 + q * dqi - kk * dki + jnp.where(row == CHUNK - 1, dbl, 0.0)
            dg = _dot_exact01(upper, db)
            dst[...] = dst1 * ebl + _tn(do, qb)

            df = dg / f - dk
            dhf_ref[rows, :] = (df * (1.0 - lbv) * sf * (1.0 - sf)).astype(BF16)
            dlb_acc[...] += jnp.sum(df * (1.0 - sf), axis=0, keepdims=True)
            dhq_ref[rows, :] = (dq * (sq * (1.0 + hq * (1.0 - sq)))).astype(BF16)
            dhi_ref[rows, :] = dv.astype(BF16)
            return carry

        lax.fori_loop(0, nc, chunk, 0)
        dlb_ref[0] = dlb_acc[...]
        dgo_ref[0] = dgo_acc[...]

    def col(off):
        return pl.BlockSpec((seq, hd), lambda b, h: (b, off + h))

    out = pl.BlockSpec((seq, hd), lambda b, h: (b, h))
    part = pl.BlockSpec((1, 1, hd), lambda b, h: (b * HGRN_HEADS + h, 0, 0))
    t = nb * seq
    return pl.pallas_call(
        body, name="hgrn_bwd", grid=(nb, HGRN_HEADS),
        in_specs=[col(12), col(16), col(20), col(24), pl.BlockSpec((1, hd), lambda b, h: (0, h)),
                  pl.BlockSpec((1, hd), lambda b, h: (0, 0)), out,
                  pl.BlockSpec((1, nc, hd, hd), lambda b, h: (b * HGRN_HEADS + h, 0, 0, 0)), col(4)],
        out_specs=[out, out, out, out, part, part],
        out_shape=[jax.ShapeDtypeStruct((t, HGRN_HEADS * hd), BF16)] * 4
        + [jax.ShapeDtypeStruct((nb * HGRN_HEADS, 1, hd), F32)] * 2,
        scratch_shapes=[pltpu.VMEM((hd, hd), F32)] + [pltpu.VMEM((CHUNK, hd), F32)] * 4
        + [pltpu.VMEM((CHUNK, CHUNK), F32)] + [pltpu.VMEM((CHUNK, hd), F32)] * 2 + [pltpu.VMEM((1, hd), F32)] * 2,
        compiler_params=_params("parallel", "parallel"),
    )(proj, proj, proj, proj, lb, go, o_pre, states, dout)


def _lb_fwd(lower_bounds):
    def body(x_ref, o_ref):
        xv = x_ref[...]
        e = jnp.exp(xv - jnp.max(xv, axis=0, keepdims=True))
        o_ref[...] = e[0:1, :] / jnp.sum(e, axis=0, keepdims=True)

    return pl.pallas_call(body, name="lb_fwd",
                          out_shape=jax.ShapeDtypeStruct((1, lower_bounds.shape[1]), F32))(lower_bounds)


def _lb_bwd(lower_bounds, dlb_parts):
    ng = dlb_parts.shape[0]

    def body(x_ref, d_ref, o_ref):
        xv = x_ref[...]
        e = jnp.exp(xv - jnp.max(xv, axis=0, keepdims=True))
        p = e / jnp.sum(e, axis=0, keepdims=True)
        dlb = d_ref[0]
        for gi in range(1, ng):
            dlb = dlb + d_ref[gi]
        first = lax.broadcasted_iota(jnp.int32, xv.shape, 0) == 0
        o_ref[...] = p * (jnp.where(first, dlb, 0.0) - p[0:1, :] * dlb)

    return pl.pallas_call(body, name="lb_bwd",
                          out_shape=jax.ShapeDtypeStruct(lower_bounds.shape, F32))(lower_bounds, dlb_parts)


def _ffn_bwd(x, g, h, gate, up, dy, dy16, wg, wu, wd, tag):
    dgate, dup, act = _ffn_bwd_mid(dy16, wd, gate, up, tag + "_bwd_mid")
    dwd = _mm(act, dy16, ta=True, tm=1408, tn=512, scale=0.5, name=tag + "_dwd")
    dwg = _mm(h, dgate, ta=True, tm=512, tn=1408, name=tag + "_dwg")
    dwu = _mm(h, dup, ta=True, tm=512, tn=1408, name=tag + "_dwu")
    dh = _mm(dgate, wg, tb=True, tm=512, tn=1024, name=tag + "_dh_gate")
    dh = _mm(dup, wu, tb=True, tm=512, tn=1024, add=dh, name=tag + "_dh_up")
    dx, dx16, dgain = _rms_bwd(x, g, dh, dy, tag + "_norm_bwd")
    return dx, dx16, dgain, dwg, dwu, dwd


def _local_step(x, tgt, sp, w, nb, seq):
    d = x.shape[1]
    h1 = _rms_fwd(x, sp["ffn1_norm_g"], "ffn1_norm")
    x1, gate1, up1 = _ffn_fwd(h1, x, w["ffn1_w_gate"], w["ffn1_w_up"], w["ffn1_w_down"], "ffn1_fwd")
    h2 = _rms_fwd(x1, sp["mix_norm_g"], "mix_norm")
    proj = _mm(h2, w["w_in"], tm=512, tn=512, name="in_proj")
    rb_pad = jnp.pad(sp["attn_rel_bias"], ((0, 0), (0, N_REL_PAD - N_REL)))
    bias = jnp.transpose(_bias_expand(rb_pad), (1, 0, 2)).reshape(ATTN_HEADS * CHUNK, BAND)
    gq2 = jnp.concatenate([sp["attn_q_norm_g"]] * 2, axis=1)
    gk2 = jnp.concatenate([sp["attn_k_norm_g"]] * 2, axis=1)
    lb = _lb_fwd(sp["hgrn_lower_bounds"])
    attn = _attn_fwd(proj, bias, gq2, gk2, nb, seq)
    hy, ho, hstate = _hgrn_fwd(proj, lb, sp["hgrn_out_norm_g"], nb, seq)
    mix = jnp.concatenate([attn, hy], axis=1)
    x2 = _mm(mix, w["w_out"], tm=512, tn=1024, add=x1, name="out_proj")
    h3 = _rms_fwd(x2, sp["ffn2_norm_g"], "ffn2_norm")
    x3, gate2, up2 = _ffn_fwd(h3, x2, w["ffn2_w_gate"], w["ffn2_w_up"], w["ffn2_w_down"], "ffn2_fwd")
    dx3, dx3_16, sq = _loss(x3, tgt, "loss")
    loss = 0.5 * jnp.sum(sq) / d

    dx2, dx2_16, dg3, dwg2, dwu2, dwd2 = _ffn_bwd(x2, sp["ffn2_norm_g"], h3, gate2, up2, dx3, dx3_16,
                                                  w["ffn2_w_gate"], w["ffn2_w_up"], w["ffn2_w_down"], "ffn2")
    dmix = _mm(dx2_16, w["w_out"], tb=True, tm=512, tn=1024, name="out_proj_dx")
    dwout = _mm(mix, dx2_16, ta=True, tm=512, tn=1024, name="out_proj_dw")
    dq, dk, dv, dbias, dgq, dgk = _attn_bwd(proj, attn, dmix, bias, gq2, gk2, nb, seq)
    dbias = jnp.transpose(dbias.reshape(nb, ATTN_HEADS, CHUNK, BAND), (0, 2, 1, 3))
    dgq = jnp.sum(dgq, axis=(0, 1)).reshape(2, ATTN_HEAD_DIM).sum(axis=0, keepdims=True)
    dgk = jnp.sum(dgk, axis=(0, 1)).reshape(2, ATTN_HEAD_DIM).sum(axis=0, keepdims=True)
    dhq, dhf, dhi, dhg, dlb, dgo = _hgrn_bwd(proj, lb, sp["hgrn_out_norm_g"], ho, hstate, dmix, nb, seq)
    dproj = jnp.concatenate([dq, dk, dv, dhq, dhf, dhi, dhg], axis=1)
    dwin = _mm(h2, dproj, ta=True, tm=512, tn=512, name="in_proj_dw")
    dh2 = _mm(dproj, w["w_in"], tb=True, tm=512, tn=1024, name="in_proj_dx")
    dx1, dx1_16, dgm = _rms_bwd(x1, sp["mix_norm_g"], dh2, dx2, "mix_norm_bwd")
    dx0, _, dg1, dwg1, dwu1, dwd1 = _ffn_bwd(x, sp["ffn1_norm_g"], h1, gate1, up1, dx1, dx1_16,
                                             w["ffn1_w_gate"], w["ffn1_w_up"], w["ffn1_w_down"], "ffn1")

    small = {
        "ffn1_norm_g": dg1, "mix_norm_g": dgm, "ffn2_norm_g": dg3,
        "attn_q_norm_g": dgq, "attn_k_norm_g": dgk,
        "attn_rel_bias": _bias_fold(dbias)[:, :N_REL],
        "hgrn_lower_bounds": _lb_bwd(sp["hgrn_lower_bounds"], dlb.reshape(nb, 1, HGRN_HEADS * HGRN_HEAD_DIM)),
        "hgrn_out_norm_g": jnp.sum(dgo, axis=(0, 1))[None, :],
    }
    big = {"ffn1_w_gate": dwg1, "ffn1_w_up": dwu1, "ffn1_w_down": dwd1, "w_in": dwin, "w_out": dwout,
           "ffn2_w_gate": dwg2, "ffn2_w_up": dwu2, "ffn2_w_down": dwd2}
    return loss, dx0, small, big


MESH = pl.DeviceIdType.MESH
ANY = pl.BlockSpec(memory_space=pl.ANY)


def _coords():
    return lax.axis_index("x"), lax.axis_index("y"), lax.axis_index("c")


def _other_chips(x, y):
    return [(1 - x, y), (x, 1 - y), (1 - x, 1 - y)]


def _all_gather(shards):
    n = len(shards)

    def body(*refs):
        ins, outs = refs[:n], refs[n:2 * n]
        send_sems, recv_sems, local_sems = refs[2 * n:]
        x, y, c = _coords()
        me, sibling = (x, y, c), (x, y, 1 - c)
        chips = _other_chips(x, y)

        def copy(i, k, block, to, src=None):
            bx, by, bc = block
            dst = outs[i].at[4 * bx + 2 * by + bc]
            return pltpu.make_async_remote_copy(
                src_ref=dst if src is None else src, dst_ref=dst, send_sem=send_sems.at[i, k],
                recv_sem=recv_sems.at[i, k], device_id=to, device_id_type=MESH)

        mine = [pltpu.make_async_copy(ins[i], outs[i].at[4 * x + 2 * y + c], local_sems.at[i]) for i in range(n)]
        for cp in mine:
            cp.start()
        sent = []
        for i in range(n):
            sent.append(copy(i, 0, me, sibling, src=ins[i]))
            sent += [copy(i, 1 + j, me, (*chip, c), src=ins[i]) for j, chip in enumerate(chips)]
        for cp in sent:
            cp.start()
        for i in range(n):
            for j, chip in enumerate(chips):
                copy(i, 1 + j, (*chip, c), me).wait_recv()
                fwd = copy(i, 4 + j, (*chip, c), sibling)
                fwd.start()
                sent.append(fwd)
        for i in range(n):
            copy(i, 0, sibling, me).wait_recv()
            for j, chip in enumerate(chips):
                copy(i, 4 + j, (*chip, 1 - c), me).wait_recv()
        for cp in sent:
            cp.wait_send()
        for cp in mine:
            cp.wait()

    return pl.pallas_call(
        body, name="weights_all_gather",
        out_shape=[jax.ShapeDtypeStruct((N_DEV,) + s.shape, s.dtype) for s in shards],
        in_specs=[ANY] * n, out_specs=[ANY] * n,
        scratch_shapes=[pltpu.SemaphoreType.DMA((n, 7)), pltpu.SemaphoreType.DMA((n, 7)),
                        pltpu.SemaphoreType.DMA((n,))],
    )(*shards)


def _pair_exchange(grads):
    n = len(grads)

    def body(*refs):
        ins, outs = refs[:n], refs[n:2 * n]
        send_sems, recv_sems = refs[2 * n:]
        x, y, c = _coords()
        cps = []
        for i in range(n):
            for k in range(4):
                cps.append(pltpu.make_async_remote_copy(
                    src_ref=ins[i].at[2 * k + 1 - c], dst_ref=outs[i].at[k], send_sem=send_sems.at[i, k],
                    recv_sem=recv_sems.at[i, k], device_id=(x, y, 1 - c), device_id_type=MESH))
        for cp in cps:
            cp.start()
        for cp in cps:
            cp.wait()

    return pl.pallas_call(
        body, name="grads_pair_exchange",
        out_shape=[jax.ShapeDtypeStruct((4,) + g.shape[1:], g.dtype) for g in grads],
        in_specs=[ANY] * n, out_specs=[ANY] * n,
        scratch_shapes=[pltpu.SemaphoreType.DMA((n, 4)), pltpu.SemaphoreType.DMA((n, 4))],
    )(*grads)


def _pair_add(grad, recv, core, name):
    _, r, cdim = grad.shape

    def body(c_ref, g_ref, r_ref, o_ref):
        o_ref[...] = (g_ref[...] + r_ref[...]).astype(BF16)

    blk = (1, r, cdim)
    return pl.pallas_call(
        body, name=name,
        grid_spec=pltpu.PrefetchScalarGridSpec(
            num_scalar_prefetch=1, grid=(4,),
            in_specs=[pl.BlockSpec(blk, lambda k, c_ref: (2 * k + c_ref[0], 0, 0)),
                      pl.BlockSpec(blk, lambda k, c_ref: (k, 0, 0))],
            out_specs=pl.BlockSpec(blk, lambda k, c_ref: (k, 0, 0))),
        out_shape=jax.ShapeDtypeStruct((4, r, cdim), BF16),
        compiler_params=_params("arbitrary"),
    )(core, grad, recv)


def _chip_exchange(parts):
    n = len(parts)

    def body(*refs):
        ins, outs = refs[:n], refs[n:2 * n]
        send_sems, recv_sems, local_sems = refs[2 * n:]
        x, y, c = _coords()
        chips = _other_chips(x, y)
        mine = [pltpu.make_async_copy(ins[i].at[2 * x + y], outs[i].at[2 * x + y], local_sems.at[i])
                for i in range(n)]
        for cp in mine:
            cp.start()
        sent = []
        for i in range(n):
            for j, (px, py) in enumerate(chips):
                sent.append(pltpu.make_async_remote_copy(
                    src_ref=ins[i].at[2 * px + py], dst_ref=outs[i].at[2 * x + y], send_sem=send_sems.at[i, j],
                    recv_sem=recv_sems.at[i, j], device_id=(px, py, c), device_id_type=MESH))
        for cp in sent:
            cp.start()
        for i in range(n):
            for j, (px, py) in enumerate(chips):
                landed = outs[i].at[2 * px + py]
                pltpu.make_async_remote_copy(
                    src_ref=landed, dst_ref=landed, send_sem=send_sems.at[i, j], recv_sem=recv_sems.at[i, j],
                    device_id=(px, py, c), device_id_type=MESH).wait_recv()
        for cp in sent:
            cp.wait_send()
        for cp in mine:
            cp.wait()

    return pl.pallas_call(
        body, name="grads_chip_exchange",
        out_shape=[jax.ShapeDtypeStruct(p.shape, p.dtype) for p in parts],
        in_specs=[ANY] * n, out_specs=[ANY] * n,
        scratch_shapes=[pltpu.SemaphoreType.DMA((n, 3)), pltpu.SemaphoreType.DMA((n, 3)),
                        pltpu.SemaphoreType.DMA((n,))],
    )(*parts)


def _all_reduce_small(v):
    r = v.shape[0]

    def body(v_ref, o_ref, buf, send_sems, recv_sems):
        x, y, c = _coords()
        me = 4 * x + 2 * y + c
        buf[me] = v_ref[...]
        cps = []
        for k in range(1, N_DEV):
            px = 1 - x if k & 4 else x
            py = 1 - y if k & 2 else y
            pc = 1 - c if k & 1 else c
            cps.append((pltpu.make_async_remote_copy(
                src_ref=v_ref, dst_ref=buf.at[me], send_sem=send_sems.at[k - 1], recv_sem=recv_sems.at[k - 1],
                device_id=(px, py, pc), device_id_type=MESH), 4 * px + 2 * py + pc))
        for cp, _ in cps:
            cp.start()
        for k, (cp, peer) in enumerate(cps):
            pltpu.make_async_remote_copy(
                src_ref=v_ref, dst_ref=buf.at[peer], send_sem=send_sems.at[k], recv_sem=recv_sems.at[k],
                device_id=(x, y, c), device_id_type=MESH).wait_recv()
        for cp, _ in cps:
            cp.wait_send()
        acc = buf[0]
        for j in range(1, N_DEV):
            acc = acc + buf[j]
        o_ref[...] = acc

    return pl.pallas_call(
        body, name="small_all_reduce", out_shape=jax.ShapeDtypeStruct(v.shape, F32),
        in_specs=[pl.BlockSpec(memory_space=pltpu.VMEM)], out_specs=pl.BlockSpec(memory_space=pltpu.VMEM),
        scratch_shapes=[pltpu.VMEM((N_DEV, r, 128), F32), pltpu.SemaphoreType.DMA((N_DEV - 1,)),
                        pltpu.SemaphoreType.DMA((N_DEV - 1,))],
    )(v)


def _adamw(w, m, v, g, name):
    r, cdim = w.shape
    parts = g.ndim == 3
    tr = r // 4 if r % 32 == 0 else r

    def body(w_ref, m_ref, v_ref, g_ref, go_ref, d_ref, mo_ref, vo_ref):
        if parts:
            gv = g_ref[0].astype(F32)
            for k in range(1, 4):
                gv = gv + g_ref[k].astype(F32)
        else:
            gv = g_ref[...]
        m2 = ADAM_B1 * m_ref[...] + (1.0 - ADAM_B1) * gv
        v2 = ADAM_B2 * v_ref[...] + (1.0 - ADAM_B2) * (gv * gv)
        m_hat = m2 / (1.0 - ADAM_B1 ** ADAM_STEP)
        v_hat = v2 / (1.0 - ADAM_B2 ** ADAM_STEP)
        go_ref[...] = gv
        d_ref[...] = -ADAM_LR * (m_hat / (jnp.sqrt(v_hat) + ADAM_EPS) + ADAM_WD * w_ref[...])
        mo_ref[...] = m2
        vo_ref[...] = v2

    row = pl.BlockSpec((tr, cdim), lambda i: (i, 0))
    g_spec = pl.BlockSpec((4, tr, cdim), lambda i: (0, i, 0)) if parts else row
    return pl.pallas_call(
        body, name=name, grid=(r // tr,), in_specs=[row, row, row, g_spec], out_specs=[row] * 4,
        out_shape=[jax.ShapeDtypeStruct((r, cdim), F32)] * 4,
        compiler_params=_params("parallel"),
    )(w, m, v, g)


WEIGHTS = ["ffn1_norm_g", "ffn1_w_gate", "ffn1_w_up", "ffn1_w_down", "mix_norm_g", "w_in", "attn_q_norm_g",
           "attn_k_norm_g", "attn_rel_bias", "hgrn_lower_bounds", "hgrn_out_norm_g", "w_out", "ffn2_norm_g",
           "ffn2_w_gate", "ffn2_w_up", "ffn2_w_down"]
COL_SHARDED = ("ffn1_w_gate", "ffn1_w_up", "w_in", "ffn2_w_gate", "ffn2_w_up")
ROW_SHARDED = ("ffn1_w_down", "w_out", "ffn2_w_down")
BIG = [n for n in WEIGHTS if n in COL_SHARDED or n in ROW_SHARDED]
SMALL = [n for n in WEIGHTS if n not in BIG]
PACK_ROWS = 8


def _pack_small(vals, loss=None):
    parts = []
    for n in SMALL:
        a = vals[n]
        if n == "attn_rel_bias":
            a = jnp.pad(a.reshape(ATTN_HEADS, N_REL), ((0, 0), (0, N_REL_PAD - N_REL)))
        flat = a.reshape(-1)
        size = -(-flat.shape[0] // (PACK_ROWS * 128)) * PACK_ROWS * 128
        parts.append(jnp.pad(flat, (0, size - flat.shape[0])).reshape(-1, 128))
    tail = jnp.zeros((PACK_ROWS, 128), F32)
    if loss is not None:
        tail = tail.at[0, 0].set(loss)
    return jnp.concatenate(parts + [tail], axis=0)


def _unpack_small(packed, shapes):
    out, row = {}, 0
    for n in SMALL:
        shape = shapes[n]
        if n == "attn_rel_bias":
            rows = ATTN_HEADS * N_REL_PAD // 128
            out[n] = packed[row:row + rows].reshape(ATTN_HEADS, N_REL_PAD)[:, :N_REL].reshape(shape)
        else:
            size = 1
            for s in shape:
                size *= s
            rows = -(-size // (PACK_ROWS * 128)) * PACK_ROWS
            out[n] = packed[row:row + rows].reshape(-1)[:size].reshape(shape)
        row += rows
    return out, packed[row, 0]


def kernel(x, ffn1_norm_g, ffn1_w_gate, ffn1_w_up, ffn1_w_down, mix_norm_g, w_in, attn_q_norm_g, attn_k_norm_g, attn_rel_bias, hgrn_lower_bounds, hgrn_out_norm_g, w_out, ffn2_norm_g, ffn2_w_gate, ffn2_w_up, ffn2_w_down, loss_target, m_ffn1_norm_g, m_ffn1_w_gate, m_ffn1_w_up, m_ffn1_w_down, m_mix_norm_g, m_w_in, m_attn_q_norm_g, m_attn_k_norm_g, m_attn_rel_bias, m_hgrn_lower_bounds, m_hgrn_out_norm_g, m_w_out, m_ffn2_norm_g, m_ffn2_w_gate, m_ffn2_w_up, m_ffn2_w_down, v_ffn1_norm_g, v_ffn1_w_gate, v_ffn1_w_up, v_ffn1_w_down, v_mix_norm_g, v_w_in, v_attn_q_norm_g, v_attn_k_norm_g, v_attn_rel_bias, v_hgrn_lower_bounds, v_hgrn_out_norm_g, v_w_out, v_ffn2_norm_g, v_ffn2_w_gate, v_ffn2_w_up, v_ffn2_w_down):
    wts = dict(zip(WEIGHTS, (ffn1_norm_g, ffn1_w_gate, ffn1_w_up, ffn1_w_down, mix_norm_g, w_in, attn_q_norm_g,
                             attn_k_norm_g, attn_rel_bias, hgrn_lower_bounds, hgrn_out_norm_g, w_out, ffn2_norm_g,
                             ffn2_w_gate, ffn2_w_up, ffn2_w_down)))
    mom = dict(zip(WEIGHTS, (m_ffn1_norm_g, m_ffn1_w_gate, m_ffn1_w_up, m_ffn1_w_down, m_mix_norm_g, m_w_in,
                             m_attn_q_norm_g, m_attn_k_norm_g, m_attn_rel_bias, m_hgrn_lower_bounds,
                             m_hgrn_out_norm_g, m_w_out, m_ffn2_norm_g, m_ffn2_w_gate, m_ffn2_w_up, m_ffn2_w_down)))
    var = dict(zip(WEIGHTS, (v_ffn1_norm_g, v_ffn1_w_gate, v_ffn1_w_up, v_ffn1_w_down, v_mix_norm_g, v_w_in,
                             v_attn_q_norm_g, v_attn_k_norm_g, v_attn_rel_bias, v_hgrn_lower_bounds,
                             v_hgrn_out_norm_g, v_w_out, v_ffn2_norm_g, v_ffn2_w_gate, v_ffn2_w_up, v_ffn2_w_down)))
    nb, seq, d = x.shape
    shapes = {n: wts[n].shape for n in WEIGHTS}

    gathered = _all_gather([wts[n][0].astype(BF16) for n in BIG])
    full = {}
    for n, g in zip(BIG, gathered):
        if n in COL_SHARDED:
            full[n] = jnp.transpose(g, (1, 0, 2)).reshape(g.shape[1], N_DEV * g.shape[2])
        else:
            full[n] = g.reshape(N_DEV * g.shape[1], g.shape[2])

    sp = {n: wts[n] for n in SMALL}
    sp["attn_rel_bias"] = wts["attn_rel_bias"][0]
    loss, dx, dsmall, dbig = _local_step(x.reshape(nb * seq, d), loss_target.reshape(nb * seq, d), sp, full, nb, seq)

    small_sum = _all_reduce_small(_pack_small(dsmall, loss))
    gsmall, loss_total = _unpack_small(small_sum, shapes)

    sliced = []
    for n in BIG:
        g = dbig[n]
        r, c = shapes[n][1], shapes[n][2]
        if n in COL_SHARDED:
            sliced.append(jnp.transpose(g.reshape(r, N_DEV, c), (1, 0, 2)))
        else:
            sliced.append(g.reshape(N_DEV, r, c))
    from_sibling = _pair_exchange(sliced)
    core = lax.axis_index("c").astype(jnp.int32).reshape(1)
    partials = [_pair_add(g, r, core, n + "_pair_add") for n, g, r in zip(BIG, sliced, from_sibling)]
    reduced = dict(zip(BIG, _chip_exchange(partials)))

    grads, deltas, new_m, new_v = {}, {}, {}, {}
    for n in BIG:
        out = _adamw(wts[n][0], mom[n][0], var[n][0], reduced[n], n + "_adamw")
        grads[n], deltas[n], new_m[n], new_v[n] = (o.reshape(shapes[n]) for o in out)
    packed = _adamw(_pack_small(wts), _pack_small(mom), _pack_small(var), small_sum, "small_adamw")
    for dst, p in zip((deltas, new_m, new_v), packed[1:]):
        dst.update(_unpack_small(p, shapes)[0])
    grads.update(gsmall)

    return (loss_total, dx.reshape(nb, seq, d), *[grads[n] for n in WEIGHTS], *[deltas[n] for n in WEIGHTS],
            *[new_m[n] for n in WEIGHTS], *[new_v[n] for n in WEIGHTS])
```

```python
import functools

import jax
import jax.numpy as jnp
from jax import lax
from jax.experimental import pallas as pl
from jax.experimental.pallas import tpu as pltpu

F32 = jnp.float32
BF16 = jnp.bfloat16

RMS_EPS = 1e-6
CHUNK = 64
LEFT_CHUNKS = 8
BAND = (LEFT_CHUNKS + 2) * CHUNK
KPAD = BAND - CHUNK
REL_CLIP = 128
N_REL = 2 * REL_CLIP + 1
N_REL_PAD = 384
ATTN_HEADS = 8
ATTN_HEAD_DIM = 64
ATTN_WIDTH = ATTN_HEADS * ATTN_HEAD_DIM
ATTN_UNROLL = 4
HGRN_HEADS = 4
HGRN_HEAD_DIM = 128
SUB = 16
N_SUB = CHUNK // SUB
N_DEV = 8

ADAM_LR = 0.001
ADAM_B1 = 0.9
ADAM_B2 = 0.999
ADAM_EPS = 1e-08
ADAM_WD = 0.01
ADAM_STEP = 10

VMEM_LIMIT = 56 * 1024 * 1024

NT = (((1,), (1,)), ((), ()))
NN = (((1,), (0,)), ((), ()))


def _params(*sem):
    return pltpu.CompilerParams(dimension_semantics=sem, vmem_limit_bytes=VMEM_LIMIT)


def _sigmoid(v):
    return 1.0 / (1.0 + jnp.exp(-v))


def _dot(a, b, dims=NN):
    return lax.dot_general(a.astype(BF16), b.astype(BF16), dims, preferred_element_type=F32)


def _dot_exact01(m01, v):
    m = m01.astype(BF16)
    hi = v.astype(BF16)
    r1 = v - hi.astype(F32)
    mid = r1.astype(BF16)
    lo = (r1 - mid.astype(F32)).astype(BF16)
    out = lax.dot_general(m, hi, NN, preferred_element_type=F32)
    out = out + lax.dot_general(m, mid, NN, preferred_element_type=F32)
    return out + lax.dot_general(m, lo, NN, preferred_element_type=F32)


def _dot_exact01_r(v, m01):
    m = m01.astype(BF16)
    hi = v.astype(BF16)
    r1 = v - hi.astype(F32)
    mid = r1.astype(BF16)
    lo = (r1 - mid.astype(F32)).astype(BF16)
    out = lax.dot_general(hi, m, NN, preferred_element_type=F32)
    out = out + lax.dot_general(mid, m, NN, preferred_element_type=F32)
    return out + lax.dot_general(lo, m, NN, preferred_element_type=F32)


def _tn(a, b):
    ap = jnp.concatenate([a, jnp.zeros_like(a)], axis=0)
    bp = jnp.concatenate([b, jnp.zeros_like(b)], axis=0)
    return _dot(ap.T, bp)


def _row_tile(t):
    for tm in (512, 256, 128, 64, 32, 16, 8):
        if t % tm == 0:
            return tm
    raise ValueError(t)


class _Side:
    def __init__(self, ins, out_shape, sems, start, finish):
        self.ins, self.out_shape, self.sems, self.start, self.finish = ins, out_shape, sems, start, finish


_ACTIVE = [None]


def _pallas(body, *, name, grid, in_specs, out_specs, out_shape, scratch_shapes=(), sem, args):
    sched = _ACTIVE[0]
    side = sched.side_for(name) if sched is not None else None
    if side is None:
        return pl.pallas_call(
            body, name=name, grid=grid, in_specs=list(in_specs), out_specs=list(out_specs),
            out_shape=list(out_shape), scratch_shapes=list(scratch_shapes), compiler_params=_params(*sem))(*args)
    cuts = [len(in_specs), len(side.ins), len(out_shape), len(side.out_shape), len(scratch_shapes)]

    def with_side(*refs):
        groups, at = [], 0
        for n in cuts:
            groups.append(refs[at:at + n])
            at += n
        ins, side_ins, outs, side_outs, scratch = groups
        side_sems = refs[at:]
        first = pl.program_id(0) == 0
        last = pl.program_id(0) == grid[0] - 1
        for a in range(1, len(grid)):
            first = jnp.logical_and(first, pl.program_id(a) == 0)
            last = jnp.logical_and(last, pl.program_id(a) == grid[a] - 1)

        @pl.when(first)
        def _():
            side.start(side_ins, side_outs, side_sems)

        body(*ins, *outs, *scratch)

        @pl.when(last)
        def _():
            side.finish(side_ins, side_outs, side_sems)

    hbm = pl.BlockSpec(memory_space=pl.ANY)
    res = pl.pallas_call(
        with_side, name=name, grid=grid, in_specs=list(in_specs) + [hbm] * len(side.ins),
        out_specs=list(out_specs) + [hbm] * len(side.out_shape), out_shape=list(out_shape) + list(side.out_shape),
        scratch_shapes=list(scratch_shapes) + list(side.sems),
        compiler_params=_params(*(["arbitrary"] * len(grid))))(*args, *side.ins)
    sched.done(name, res[len(out_shape):])
    return res[:len(out_shape)]


def _run_side(side, name):
    n_in, n_out = len(side.ins), len(side.out_shape)

    def body(*refs):
        ins, outs, sems = refs[:n_in], refs[n_in:n_in + n_out], refs[n_in + n_out:]
        side.start(ins, outs, sems)
        side.finish(ins, outs, sems)

    hbm = pl.BlockSpec(memory_space=pl.ANY)
    return pl.pallas_call(body, name=name, in_specs=[hbm] * n_in, out_specs=[hbm] * n_out,
                          out_shape=list(side.out_shape), scratch_shapes=list(side.sems))(*side.ins)


def _rms_fwd(x, g, name):
    t, d = x.shape
    tm = _row_tile(t)

    def body(x_ref, g_ref, h_ref):
        xv = x_ref[...]
        r = lax.rsqrt(jnp.mean(xv * xv, axis=-1, keepdims=True) + RMS_EPS)
        h_ref[...] = (xv * r * g_ref[...]).astype(BF16)

    return pl.pallas_call(
        body, name=name, grid=(t // tm,),
        in_specs=[pl.BlockSpec((tm, d), lambda i: (i, 0)), pl.BlockSpec((1, d), lambda i: (0, 0))],
        out_specs=pl.BlockSpec((tm, d), lambda i: (i, 0)),
        out_shape=jax.ShapeDtypeStruct((t, d), BF16),
        compiler_params=_params("parallel"),
    )(x, g)


def _rms_bwd(x, g, dh, dres, name):
    t, d = x.shape
    tm = _row_tile(t)

    def body(x_ref, g_ref, dh_ref, dres_ref, dx_ref, dx16_ref, dg_ref):
        xv = x_ref[...]
        r = lax.rsqrt(jnp.mean(xv * xv, axis=-1, keepdims=True) + RMS_EPS)
        xhat = xv * r
        dhv = dh_ref[...]
        gd = dhv * g_ref[...]
        dx = dres_ref[...] + r * (gd - xhat * jnp.mean(gd * xhat, axis=-1, keepdims=True))
        dx_ref[...] = dx
        dx16_ref[...] = dx.astype(BF16)
        part = jnp.sum(dhv * xhat, axis=0, keepdims=True)

        @pl.when(pl.program_id(0) == 0)
        def _():
            dg_ref[...] = part

        @pl.when(pl.program_id(0) > 0)
        def _():
            dg_ref[...] += part

    row = pl.BlockSpec((tm, d), lambda i: (i, 0))
    vec = pl.BlockSpec((1, d), lambda i: (0, 0))
    return pl.pallas_call(
        body, name=name, grid=(t // tm,),
        in_specs=[row, vec, row, row], out_specs=[row, row, vec],
        out_shape=[jax.ShapeDtypeStruct((t, d), F32), jax.ShapeDtypeStruct((t, d), BF16),
                   jax.ShapeDtypeStruct((1, d), F32)],
        compiler_params=_params("arbitrary"),
    )(x, g, dh, dres)


def _mm(a, b, *, ta=False, tb=False, tm, tn, out_dtype=F32, add=None, scale=1.0, name):
    m, k = (a.shape[1], a.shape[0]) if ta else a.shape
    n = b.shape[0] if tb else b.shape[1]
    tm, tn = min(tm, m), min(tn, n)
    assert m % tm == 0 and n % tn == 0, (m, n, tm, tn)
    dims = (((0 if ta else 1,), (1 if tb else 0,)), ((), ()))

    def body(*refs):
        a_ref, b_ref = refs[0], refs[1]
        o_ref = refs[-1]
        r = lax.dot_general(a_ref[...].astype(BF16), b_ref[...].astype(BF16), dims, preferred_element_type=F32)
        if scale != 1.0:
            r = r * scale
        if add is not None:
            r = r + refs[2][...]
        o_ref[...] = r.astype(out_dtype)

    a_spec = pl.BlockSpec((k, tm), lambda i, j: (0, i)) if ta else pl.BlockSpec((tm, k), lambda i, j: (i, 0))
    b_spec = pl.BlockSpec((tn, k), lambda i, j: (j, 0)) if tb else pl.BlockSpec((k, tn), lambda i, j: (0, j))
    o_spec = pl.BlockSpec((tm, tn), lambda i, j: (i, j))
    ins, specs = [a, b], [a_spec, b_spec]
    if add is not None:
        ins.append(add)
        specs.append(o_spec)
    return _pallas(
        body, name=name, grid=(m // tm, n // tn), in_specs=specs, out_specs=[o_spec],
        out_shape=[jax.ShapeDtypeStruct((m, n), out_dtype)], sem=("parallel", "parallel"), args=ins)[0]


def _ffn_tile(f):
    for tf in (1408, 512, 256, 128):
        if f % tf == 0:
            return tf
    raise ValueError(f)


def _ffn_fwd(h, x, wg, wu, wd, name):
    t, d = x.shape
    f = wg.shape[1]
    tm, tf = _row_tile(t), _ffn_tile(f)
    nf = f // tf

    def body(h_ref, x_ref, wg_ref, wu_ref, wd_ref, y_ref, g_ref, u_ref, acc_ref):
        j = pl.program_id(1)
        hv = h_ref[...]
        gv = lax.dot_general(hv, wg_ref[...], NN, preferred_element_type=F32)
        uv = lax.dot_general(hv, wu_ref[...], NN, preferred_element_type=F32)
        av = gv * _sigmoid(gv) * uv
        g_ref[...] = gv.astype(BF16)
        u_ref[...] = uv.astype(BF16)
        part = lax.dot_general(av.astype(BF16), wd_ref[...], NN, preferred_element_type=F32)

        @pl.when(j == 0)
        def _():
            acc_ref[...] = part

        @pl.when(j > 0)
        def _():
            acc_ref[...] += part

        @pl.when(j == nf - 1)
        def _():
            y_ref[...] = x_ref[...] + 0.5 * acc_ref[...]

    row = pl.BlockSpec((tm, d), lambda i, j: (i, 0))
    hid = pl.BlockSpec((tm, tf), lambda i, j: (i, j))
    return _pallas(
        body, name=name, grid=(t // tm, nf),
        in_specs=[row, row, pl.BlockSpec((d, tf), lambda i, j: (0, j)), pl.BlockSpec((d, tf), lambda i, j: (0, j)),
                  pl.BlockSpec((tf, d), lambda i, j: (j, 0))],
        out_specs=[row, hid, hid],
        out_shape=[jax.ShapeDtypeStruct((t, d), F32), jax.ShapeDtypeStruct((t, f), BF16),
                   jax.ShapeDtypeStruct((t, f), BF16)],
        scratch_shapes=[pltpu.VMEM((tm, d), F32)], sem=("parallel", "arbitrary"), args=(h, x, wg, wu, wd))


def _ffn_bwd_mid(dy, wd, g, u, name):
    t, d = dy.shape
    f = wd.shape[0]
    tm, tf = _row_tile(t), _ffn_tile(f)

    def body(dy_ref, wd_ref, g_ref, u_ref, dg_ref, du_ref, a_ref):
        da = 0.5 * lax.dot_general(dy_ref[...].astype(BF16), wd_ref[...], NT, preferred_element_type=F32)
        gv = g_ref[...].astype(F32)
        uv = u_ref[...].astype(F32)
        s = _sigmoid(gv)
        silu = gv * s
        dg_ref[...] = (da * uv * (s * (1.0 + gv * (1.0 - s)))).astype(BF16)
        du_ref[...] = (da * silu).astype(BF16)
        a_ref[...] = (silu * uv).astype(BF16)

    hid = pl.BlockSpec((tm, tf), lambda i, j: (i, j))
    return _pallas(
        body, name=name, grid=(t // tm, f // tf),
        in_specs=[pl.BlockSpec((tm, d), lambda i, j: (i, 0)), pl.BlockSpec((tf, d), lambda i, j: (j, 0)), hid, hid],
        out_specs=[hid, hid, hid], out_shape=[jax.ShapeDtypeStruct((t, f), BF16)] * 3,
        sem=("parallel", "parallel"), args=(dy, wd, g, u))


def _loss(y, tgt, name):
    t, d = y.shape
    tm = _row_tile(t)

    def body(y_ref, t_ref, dy_ref, dy16_ref, sq_ref):
        e = y_ref[...] - t_ref[...]
        dy = e * (1.0 / d)
        dy_ref[...] = dy
        dy16_ref[...] = dy.astype(BF16)
        part = jnp.sum(e * e, axis=0, keepdims=True)

        @pl.when(pl.program_id(0) == 0)
        def _():
            sq_ref[...] = part

        @pl.when(pl.program_id(0) > 0)
        def _():
            sq_ref[...] += part

    row = pl.BlockSpec((tm, d), lambda i: (i, 0))
    vec = pl.BlockSpec((1, d), lambda i: (0, 0))
    return pl.pallas_call(
        body, name=name, grid=(t // tm,), in_specs=[row, row], out_specs=[row, row, vec],
        out_shape=[jax.ShapeDtypeStruct((t, d), F32), jax.ShapeDtypeStruct((t, d), BF16),
                   jax.ShapeDtypeStruct((1, d), F32)],
        compiler_params=_params("arbitrary"),
    )(y, tgt)


def _rel_index(t, s_band):
    return jnp.clip(t + KPAD - s_band, -REL_CLIP, REL_CLIP) + REL_CLIP


def _bias_expand(rel_bias_pad):
    nh = rel_bias_pad.shape[0]

    def body(rb_ref, out_ref):
        rb = rb_ref[...]
        i_io = lax.broadcasted_iota(jnp.int32, (N_REL_PAD, BAND), 0)
        s_io = lax.broadcasted_iota(jnp.int32, (N_REL_PAD, BAND), 1)

        def row(t, carry):
            onehot = (i_io == _rel_index(t, s_io)).astype(F32)
            out_ref[t] = _dot_exact01_r(rb, onehot)
            return carry

        lax.fori_loop(0, CHUNK, row, 0)

    return pl.pallas_call(
        body, name="bias_expand", out_shape=jax.ShapeDtypeStruct((CHUNK, nh, BAND), F32),
        compiler_params=pltpu.CompilerParams(vmem_limit_bytes=VMEM_LIMIT),
    )(rel_bias_pad)


def _bias_fold(dbias):
    ng, nh = dbias.shape[0], dbias.shape[2]

    def body(db_ref, out_ref):
        s_io = lax.broadcasted_iota(jnp.int32, (BAND, N_REL_PAD), 0)
        i_io = lax.broadcasted_iota(jnp.int32, (BAND, N_REL_PAD), 1)

        def row(t, acc):
            onehot = (i_io == _rel_index(t, s_io)).astype(F32)
            d = db_ref[0, t]
            for gi in range(1, ng):
                d = d + db_ref[gi, t]
            return acc + _dot_exact01_r(d, onehot)

        out_ref[...] = lax.fori_loop(0, CHUNK, row, jnp.zeros((nh, N_REL_PAD), F32))

    return pl.pallas_call(
        body, name="bias_fold", out_shape=jax.ShapeDtypeStruct((nh, N_REL_PAD), F32),
        compiler_params=pltpu.CompilerParams(vmem_limit_bytes=VMEM_LIMIT),
    )(dbias)


def _left_half(shape):
    return lax.broadcasted_iota(jnp.int32, shape, len(shape) - 1) < ATTN_HEAD_DIM


def _stack_heads(v):
    left = _left_half(v.shape)
    zero = jnp.zeros_like(v)
    return jnp.concatenate([jnp.where(left, v, zero), jnp.where(left, zero, v)], axis=0)


def _unstack_heads(v):
    return jnp.where(_left_half((CHUNK, 128)), v[0:CHUNK, :], v[CHUNK:2 * CHUNK, :])


def _half_mean(v):
    r = lax.broadcasted_iota(jnp.int32, (128, 128), 0) < ATTN_HEAD_DIM
    c = lax.broadcasted_iota(jnp.int32, (128, 128), 1) < ATTN_HEAD_DIM
    return _dot_exact01_r(v, r == c) * (1.0 / ATTN_HEAD_DIM)


def _attn_prepare(q_ref, k_ref, v_ref, gq_ref, gk_ref, qs_scr, k_scr, v_scr):
    q, k = q_ref[...], k_ref[...]
    rq = lax.rsqrt(_half_mean(q * q) + RMS_EPS)
    rk = lax.rsqrt(_half_mean(k * k) + RMS_EPS)
    qhat, khat = q * rq, k * rk
    qs_scr[...] = (qhat * gq_ref[...] * ATTN_HEAD_DIM ** -0.5).astype(BF16)
    k_scr[0:KPAD, :] = jnp.zeros((KPAD, 128), BF16)
    v_scr[0:KPAD, :] = jnp.zeros((KPAD, 128), BF16)
    k_scr[KPAD:, :] = (khat * gk_ref[...]).astype(BF16)
    v_scr[KPAD:, :] = v_ref[...].astype(BF16)
    return qhat, rq, khat, rk


def _attn_scores(qs_scr, k_scr, bias_ref, c):
    r0 = pl.multiple_of(c * CHUNK, CHUNK)
    qst = _stack_heads(qs_scr[pl.ds(r0, CHUNK), :])
    kb = k_scr[pl.ds(r0, BAND), :]
    s = lax.dot_general(qst, kb, NT, preferred_element_type=F32) + bias_ref[...]
    col = lax.broadcasted_iota(jnp.int32, (2 * CHUNK, BAND), 1)
    first = jnp.maximum(CHUNK, (LEFT_CHUNKS + 1 - c) * CHUNK)
    s = jnp.where(col >= first, s, -jnp.inf)
    e = jnp.exp(s - jnp.max(s, axis=-1, keepdims=True))
    return e, 1.0 / jnp.sum(e, axis=-1, keepdims=True), qst, kb, r0


def _attn_fwd(proj, bias, gq, gk, nb, seq):
    nc = seq // CHUNK

    def body(q_ref, k_ref, v_ref, bias_ref, gq_ref, gk_ref, o_ref, qs_scr, k_scr, v_scr):
        _attn_prepare(q_ref, k_ref, v_ref, gq_ref, gk_ref, qs_scr, k_scr, v_scr)

        def chunk(c, carry):
            e, inv, _, _, r0 = _attn_scores(qs_scr, k_scr, bias_ref, c)
            vb = v_scr[pl.ds(r0, BAND), :]
            o_ref[pl.ds(r0, CHUNK), :] = _unstack_heads(
                lax.dot_general(e.astype(BF16), vb, NN, preferred_element_type=F32) * inv)
            return carry

        lax.fori_loop(0, nc, chunk, 0, unroll=ATTN_UNROLL)

    def col(off):
        return pl.BlockSpec((seq, 128), lambda b, hp: (b, off + hp))

    vec = pl.BlockSpec((1, 128), lambda b, hp: (0, 0))
    return _pallas(
        body, name="attn_fwd", grid=(nb, ATTN_HEADS // 2),
        in_specs=[col(0), col(4), col(8), pl.BlockSpec((2 * CHUNK, BAND), lambda b, hp: (hp, 0)), vec, vec],
        out_specs=[pl.BlockSpec((seq, 128), lambda b, hp: (b, hp))],
        out_shape=[jax.ShapeDtypeStruct((nb * seq, ATTN_WIDTH), F32)],
        scratch_shapes=[pltpu.VMEM((seq, 128), BF16), pltpu.VMEM((seq + KPAD, 128), BF16),
                        pltpu.VMEM((seq + KPAD, 128), BF16)],
        sem=("parallel", "parallel"), args=(proj, proj, proj, bias, gq, gk))[0]


def _attn_bwd(proj, out, dout, bias, gq, gk, nb, seq):
    nc = seq // CHUNK
    scale = ATTN_HEAD_DIM ** -0.5

    def body(q_ref, k_ref, v_ref, o_ref, do_ref, bias_ref, gq_ref, gk_ref,
             dq_ref, dk_ref, dv_ref, dbias_ref, dgq_ref, dgk_ref,
             qs_scr, k_scr, v_scr, dqn_scr, dk_scr, dv_scr, db_scr):
        qhat, rq, khat, rk = _attn_prepare(q_ref, k_ref, v_ref, gq_ref, gk_ref, qs_scr, k_scr, v_scr)
        dk_scr[...] = jnp.zeros_like(dk_scr)
        dv_scr[...] = jnp.zeros_like(dv_scr)
        db_scr[...] = jnp.zeros_like(db_scr)

        def chunk(c, carry):
            e, inv, qst, kb, r0 = _attn_scores(qs_scr, k_scr, bias_ref, c)
            p = e * inv
            vb = v_scr[pl.ds(r0, BAND), :]
            do_c = do_ref[pl.ds(r0, CHUNK), :]
            dost = _stack_heads(do_c)
            drow = jnp.sum(dost * _stack_heads(o_ref[pl.ds(r0, CHUNK), :]), axis=-1, keepdims=True)
            dp = lax.dot_general(dost.astype(BF16), vb, NT, preferred_element_type=F32)
            ds = p * (dp - drow)
            db_scr[...] += ds
            dqn_scr[pl.ds(r0, CHUNK), :] = scale * _unstack_heads(
                lax.dot_general(ds.astype(BF16), kb, NN, preferred_element_type=F32))
            dk_scr[pl.ds(r0, BAND), :] += lax.dot_general(ds.T.astype(BF16), qst, NN, preferred_element_type=F32)
            dv_scr[pl.ds(r0, BAND), :] += _dot(p.T, dost)
            return carry

        lax.fori_loop(0, nc, chunk, 0, unroll=ATTN_UNROLL)

        def norm_bwd(dn, hat, r, g_ref):
            gd = dn * g_ref[...]
            return r * (gd - hat * _half_mean(gd * hat)), jnp.sum(dn * hat, axis=0, keepdims=True)

        dq, dgq = norm_bwd(dqn_scr[...], qhat, rq, gq_ref)
        dk, dgk = norm_bwd(dk_scr[KPAD:, :], khat, rk, gk_ref)
        dq_ref[...] = dq.astype(BF16)
        dk_ref[...] = dk.astype(BF16)
        dv_ref[...] = dv_scr[KPAD:, :].astype(BF16)
        dbias_ref[0] = db_scr[...]
        dgq_ref[0] = dgq
        dgk_ref[0] = dgk

    def col(off):
        return pl.BlockSpec((seq, 128), lambda b, hp: (b, off + hp))

    vec = pl.BlockSpec((1, 128), lambda b, hp: (0, 0))
    gvec = pl.BlockSpec((1, 1, 128), lambda b, hp: (b * (ATTN_HEADS // 2) + hp, 0, 0))
    t = nb * seq
    return _pallas(
        body, name="attn_bwd", grid=(nb, ATTN_HEADS // 2),
        in_specs=[col(0), col(4), col(8), col(0), col(0),
                  pl.BlockSpec((2 * CHUNK, BAND), lambda b, hp: (hp, 0)), vec, vec],
        out_specs=[col(0), col(0), col(0), pl.BlockSpec((1, 2 * CHUNK, BAND), lambda b, hp: (b, hp, 0)),
                   gvec, gvec],
        out_shape=[jax.ShapeDtypeStruct((t, ATTN_WIDTH), BF16)] * 3
        + [jax.ShapeDtypeStruct((nb, ATTN_HEADS * CHUNK, BAND), F32)]
        + [jax.ShapeDtypeStruct((nb * ATTN_HEADS // 2, 1, 128), F32)] * 2,
        scratch_shapes=[pltpu.VMEM((seq, 128), BF16), pltpu.VMEM((seq + KPAD, 128), BF16),
                        pltpu.VMEM((seq + KPAD, 128), BF16), pltpu.VMEM((seq, 128), F32),
                        pltpu.VMEM((seq + KPAD, 128), F32), pltpu.VMEM((seq + KPAD, 128), F32),
                        pltpu.VMEM((2 * CHUNK, BAND), F32)],
        sem=("parallel", "parallel"), args=(proj, proj, proj, out, dout, bias, gq, gk))


def _tri(lower):
    r = lax.broadcasted_iota(jnp.int32, (CHUNK, CHUNK), 0)
    c = lax.broadcasted_iota(jnp.int32, (CHUNK, CHUNK), 1)
    return (r >= c) if lower else (r <= c)


def _hgrn_gates(hq, hf, lb):
    sq = _sigmoid(hq)
    sf = _sigmoid(hf)
    return hq * sq, sq, sf, lb + (1.0 - lb) * sf


def _hgrn_offdiag(q_s, k_s, b_s):
    row = lax.broadcasted_iota(jnp.int32, (CHUNK, HGRN_HEAD_DIM), 0)
    bv, qv, kv = b_s[...], q_s[...], k_s[...]
    eqs, eks = [], []
    for i in range(1, N_SUB):
        r = b_s[pl.ds(SUB * i - 1, 1), :]
        in_i = (row >= SUB * i) & (row < SUB * (i + 1))
        eqs.append(jnp.exp(jnp.where(in_i, bv - r, -jnp.inf)))
        eks.append(jnp.exp(jnp.where(row < SUB * i, r - bv, -jnp.inf)))
    eq = jnp.concatenate(eqs, axis=1)
    ek = jnp.concatenate(eks, axis=1)
    qt = jnp.concatenate([qv] * (N_SUB - 1), axis=1) * eq
    kt = jnp.concatenate([kv] * (N_SUB - 1), axis=1) * ek
    return qt, kt, eq, ek


def _hgrn_diag_e(b_s, i, s):
    t_io = lax.broadcasted_iota(jnp.int32, (SUB, HGRN_HEAD_DIM), 0)
    bi = b_s[pl.ds(SUB * i, SUB), :]
    return jnp.exp(jnp.where(t_io >= s, bi - b_s[pl.ds(SUB * i + s, 1), :], -jnp.inf)), t_io


def _hgrn_intra(q_s, k_s, b_s, a_s, qt, kt):
    ktp = jnp.concatenate([kt, jnp.zeros_like(kt)], axis=0)
    a_s[...] = _dot(qt, ktp, NT)
    col = lax.broadcasted_iota(jnp.int32, (SUB, HGRN_HEAD_DIM), 1)
    for i in range(N_SUB):
        qi = q_s[pl.ds(SUB * i, SUB), :]
        ai = jnp.zeros((SUB, HGRN_HEAD_DIM), F32)
        for s in range(SUB):
            e, _ = _hgrn_diag_e(b_s, i, s)
            a_col = jnp.sum(qi * k_s[pl.ds(SUB * i + s, 1), :] * e, axis=-1, keepdims=True)
            ai = ai + jnp.where(col == SUB * i + s, a_col, 0.0)
        a_s[pl.ds(SUB * i, SUB), :] += ai


def _hgrn_fwd(proj, lb, go, nb, seq):
    nc = seq // CHUNK
    hd = HGRN_HEAD_DIM

    def body(hq_ref, hf_ref, hi_ref, hg_ref, lb_ref, go_ref, y_ref, o_ref, st_ref, st, q_s, k_s, b_s, a_s):
        st[...] = jnp.zeros_like(st)
        lower = _tri(True)

        def chunk(c, carry):
            r0 = pl.multiple_of(c * CHUNK, CHUNK)
            rows = pl.ds(r0, CHUNK)
            q, _, _, f = _hgrn_gates(hq_ref[rows, :], hf_ref[rows, :], lb_ref[...])
            v = hi_ref[rows, :]
            b = _dot_exact01(lower, jnp.log(f))
            q_s[...] = q
            k_s[...] = 1.0 - f
            b_s[...] = b
            st_ref[0, c] = st[...]
            qt, kt, _, _ = _hgrn_offdiag(q_s, k_s, b_s)
            _hgrn_intra(q_s, k_s, b_s, a_s, qt, kt)
            vp = jnp.concatenate([v, jnp.zeros_like(v)], axis=0)
            o = _dot(a_s[...], vp) + _dot(q * jnp.exp(b), st[...], NT)
            bl = b_s[pl.ds(CHUNK - 1, 1), :]
            st[...] = st[...] * jnp.exp(bl) + _tn(v, (1.0 - f) * jnp.exp(bl - b))
            o_ref[rows, :] = o
            n = o * lax.rsqrt(jnp.mean(o * o, axis=-1, keepdims=True) + RMS_EPS) * go_ref[...]
            hg = hg_ref[rows, :]
            y_ref[rows, :] = n * hg * _sigmoid(hg)
            return carry

        lax.fori_loop(0, nc, chunk, 0)

    def col(off):
        return pl.BlockSpec((seq, hd), lambda b, h: (b, off + h))

    out = pl.BlockSpec((seq, hd), lambda b, h: (b, h))
    t = nb * seq
    return pl.pallas_call(
        body, name="hgrn_fwd", grid=(nb, HGRN_HEADS),
        in_specs=[col(12), col(16), col(20), col(24), pl.BlockSpec((1, hd), lambda b, h: (0, h)),
                  pl.BlockSpec((1, hd), lambda b, h: (0, 0))],
        out_specs=[out, out, pl.BlockSpec((1, nc, hd, hd), lambda b, h: (b * HGRN_HEADS + h, 0, 0, 0))],
        out_shape=[jax.ShapeDtypeStruct((t, HGRN_HEADS * hd), F32)] * 2
        + [jax.ShapeDtypeStruct((nb * HGRN_HEADS, nc, hd, hd), F32)],
        scratch_shapes=[pltpu.VMEM((hd, hd), F32)] + [pltpu.VMEM((CHUNK, hd), F32)] * 4,
        compiler_params=_params("parallel", "parallel"),
    )(proj, proj, proj, proj, lb, go)


def _hgrn_bwd(proj, lb, go, o_pre, states, dout, nb, seq):
    nc = seq // CHUNK
    hd = HGRN_HEAD_DIM

    def body(hq_ref, hf_ref, hi_ref, hg_ref, lb_ref, go_ref, o_ref, st_ref, dy_ref,
             dhq_ref, dhf_ref, dhi_ref, dhg_ref, dlb_ref, dgo_ref,
             dst, q_s, k_s, b_s, a_s, da_s, dqi_s, dki_s, dlb_acc, dgo_acc):
        dst[...] = jnp.zeros_like(dst)
        dlb_acc[...] = jnp.zeros_like(dlb_acc)
        dgo_acc[...] = jnp.zeros_like(dgo_acc)
        lower, upper = _tri(True), _tri(False)
        lbv, gov = lb_ref[...], go_ref[...]
        row = lax.broadcasted_iota(jnp.int32, (CHUNK, hd), 0)

        def chunk(it, carry):
            c = nc - 1 - it
            r0 = pl.multiple_of(c * CHUNK, CHUNK)
            rows = pl.ds(r0, CHUNK)
            hq, hf, v, hg = hq_ref[rows, :], hf_ref[rows, :], hi_ref[rows, :], hg_ref[rows, :]
            q, sq, sf, f = _hgrn_gates(hq, hf, lbv)
            kk = 1.0 - f
            b = _dot_exact01(lower, jnp.log(f))
            q_s[...] = q
            k_s[...] = kk
            b_s[...] = b
            bl = b_s[pl.ds(CHUNK - 1, 1), :]
            ebl = jnp.exp(bl)
            ekd = jnp.exp(bl - b)
            kd = kk * ekd
            eb = jnp.exp(b)
            qb = q * eb
            st0 = st_ref[0, c]
            dst1 = dst[...]

            o = o_ref[rows, :]
            dy = dy_ref[rows, :]
            sg = _sigmoid(hg)
            rstd = lax.rsqrt(jnp.mean(o * o, axis=-1, keepdims=True) + RMS_EPS)
            ohat = o * rstd
            dn = dy * hg * sg
            dhg_ref[rows, :] = (dy * ohat * gov * (sg * (1.0 + hg * (1.0 - sg)))).astype(BF16)
            dgo_acc[...] += jnp.sum(dn * ohat, axis=0, keepdims=True)
            gdn = dn * gov
            do = rstd * (gdn - ohat * jnp.mean(gdn * ohat, axis=-1, keepdims=True))

            qt, kt, eq, ek = _hgrn_offdiag(q_s, k_s, b_s)
            _hgrn_intra(q_s, k_s, b_s, a_s, qt, kt)
            da = _dot(do, v, NT)
            dat = _dot(v, do, NT)
            da_s[...] = da
            dqo = _dot(da, kt) * eq
            dko = _dot(dat, qt) * ek
            dqi_s[...] = dqo[:, 0:hd] + dqo[:, hd:2 * hd] + dqo[:, 2 * hd:3 * hd]
            dki_s[...] = dko[:, 0:hd] + dko[:, hd:2 * hd] + dko[:, 2 * hd:3 * hd]
            col = lax.broadcasted_iota(jnp.int32, (SUB, CHUNK), 1)
            for i in range(N_SUB):
                qi = q_s[pl.ds(SUB * i, SUB), :]
                dai = da_s[pl.ds(SUB * i, SUB), :]
                dqd = jnp.zeros((SUB, hd), F32)
                dkd_ = jnp.zeros((SUB, hd), F32)
                for s in range(SUB):
                    e, t_io = _hgrn_diag_e(b_s, i, s)
                    dacol = jnp.sum(jnp.where(col == SUB * i + s, dai, 0.0), axis=-1, keepdims=True)
                    w = dacol * e
                    dqd = dqd + w * k_s[pl.ds(SUB * i + s, 1), :]
                    dkd_ = dkd_ + jnp.where(t_io == s, jnp.sum(w * qi, axis=0, keepdims=True), 0.0)
                dqi_s[pl.ds(SUB * i, SUB), :] += dqd
                dki_s[pl.ds(SUB * i, SUB), :] += dkd_
            dqi, dki = dqi_s[...], dki_s[...]

            dv = _tn(a_s[...], do)[0:CHUNK, :] + _dot(kd, dst1, NT)
            dqb = _dot(do, st0)
            dkd = _dot(v, dst1)
            t2 = dkd * kd
            dq = dqb * eb + dqi
            dk = dkd * ekd + dki
            dbl = jnp.sum(t2, axis=0, keepdims=True) + ebl * jnp.sum(st0 * dst1, axis=0, keepdims=True)
            db = dqb * qb - t2 + q * dqi - kk * dki + jnp.where(row == CHUNK - 1, dbl, 0.0)
            dg = _dot_exact01(upper, db)
            dst[...] = dst1 * ebl + _tn(do, qb)

            df = dg / f - dk
            dhf_ref[rows, :] = (df * (1.0 - lbv) * sf * (1.0 - sf)).astype(BF16)
            dlb_acc[...] += jnp.sum(df * (1.0 - sf), axis=0, keepdims=True)
            dhq_ref[rows, :] = (dq * (sq * (1.0 + hq * (1.0 - sq)))).astype(BF16)
            dhi_ref[rows, :] = dv.astype(BF16)
            return carry

        lax.fori_loop(0, nc, chunk, 0)
        dlb_ref[0] = dlb_acc[...]
        dgo_ref[0] = dgo_acc[...]

    def col(off):
        return pl.BlockSpec((seq, hd), lambda b, h: (b, off + h))

    out = pl.BlockSpec((seq, hd), lambda b, h: (b, h))
    part = pl.BlockSpec((1, 1, hd), lambda b, h: (b * HGRN_HEADS + h, 0, 0))
    t = nb * seq
    return pl.pallas_call(
        body, name="hgrn_bwd", grid=(nb, HGRN_HEADS),
        in_specs=[col(12), col(16), col(20), col(24), pl.BlockSpec((1, hd), lambda b, h: (0, h)),
                  pl.BlockSpec((1, hd), lambda b, h: (0, 0)), out,
                  pl.BlockSpec((1, nc, hd, hd), lambda b, h: (b * HGRN_HEADS + h, 0, 0, 0)), col(4)],
        out_specs=[out, out, out, out, part, part],
        out_shape=[jax.ShapeDtypeStruct((t, HGRN_HEADS * hd), BF16)] * 4
        + [jax.ShapeDtypeStruct((nb * HGRN_HEADS, 1, hd), F32)] * 2,
        scratch_shapes=[pltpu.VMEM((hd, hd), F32)] + [pltpu.VMEM((CHUNK, hd), F32)] * 4
        + [pltpu.VMEM((CHUNK, CHUNK), F32)] + [pltpu.VMEM((CHUNK, hd), F32)] * 2 + [pltpu.VMEM((1, hd), F32)] * 2,
        compiler_params=_params("parallel", "parallel"),
    )(proj, proj, proj, proj, lb, go, o_pre, states, dout)


def _lb_fwd(lower_bounds):
    def body(x_ref, o_ref):
        xv = x_ref[...]
        e = jnp.exp(xv - jnp.max(xv, axis=0, keepdims=True))
        o_ref[...] = e[0:1, :] / jnp.sum(e, axis=0, keepdims=True)

    return pl.pallas_call(body, name="lb_fwd",
                          out_shape=jax.ShapeDtypeStruct((1, lower_bounds.shape[1]), F32))(lower_bounds)


def _lb_bwd(lower_bounds, dlb_parts):
    ng = dlb_parts.shape[0]

    def body(x_ref, d_ref, o_ref):
        xv = x_ref[...]
        e = jnp.exp(xv - jnp.max(xv, axis=0, keepdims=True))
        p = e / jnp.sum(e, axis=0, keepdims=True)
        dlb = d_ref[0]
        for gi in range(1, ng):
            dlb = dlb + d_ref[gi]
        first = lax.broadcasted_iota(jnp.int32, xv.shape, 0) == 0
        o_ref[...] = p * (jnp.where(first, dlb, 0.0) - p[0:1, :] * dlb)

    return pl.pallas_call(body, name="lb_bwd",
                          out_shape=jax.ShapeDtypeStruct(lower_bounds.shape, F32))(lower_bounds, dlb_parts)


def _ffn_bwd(x, g, h, gate, up, dy, dy16, w, put, tag):
    wg, wu, wd = w[tag + "_w_gate"], w[tag + "_w_up"], w[tag + "_w_down"]
    dgate, dup, act = _ffn_bwd_mid(dy16, wd, gate, up, tag + "_bwd_mid")
    put(tag + "_w_down", _mm(act, dy16, ta=True, tm=1408, tn=512, scale=0.5, name=tag + "_dwd"))
    put(tag + "_w_gate", _mm(h, dgate, ta=True, tm=512, tn=1408, name=tag + "_dwg"))
    put(tag + "_w_up", _mm(h, dup, ta=True, tm=512, tn=1408, name=tag + "_dwu"))
    dh = _mm(dgate, wg, tb=True, tm=512, tn=1024, name=tag + "_dh_gate")
    dh = _mm(dup, wu, tb=True, tm=512, tn=1024, add=dh, name=tag + "_dh_up")
    return _rms_bwd(x, g, dh, dy, tag + "_norm_bwd")


def _local_step(x, tgt, sp, w, put, nb, seq):
    d = x.shape[1]
    h1 = _rms_fwd(x, sp["ffn1_norm_g"], "ffn1_norm")
    x1, gate1, up1 = _ffn_fwd(h1, x, w["ffn1_w_gate"], w["ffn1_w_up"], w["ffn1_w_down"], "ffn1_fwd")
    h2 = _rms_fwd(x1, sp["mix_norm_g"], "mix_norm")
    proj = _mm(h2, w["w_in"], tm=512, tn=512, name="in_proj")
    rb_pad = jnp.pad(sp["attn_rel_bias"], ((0, 0), (0, N_REL_PAD - N_REL)))
    bias = jnp.transpose(_bias_expand(rb_pad), (1, 0, 2)).reshape(ATTN_HEADS * CHUNK, BAND)
    gq2 = jnp.concatenate([sp["attn_q_norm_g"]] * 2, axis=1)
    gk2 = jnp.concatenate([sp["attn_k_norm_g"]] * 2, axis=1)
    lb = _lb_fwd(sp["hgrn_lower_bounds"])
    attn = _attn_fwd(proj, bias, gq2, gk2, nb, seq)
    hy, ho, hstate = _hgrn_fwd(proj, lb, sp["hgrn_out_norm_g"], nb, seq)
    mix = jnp.concatenate([attn, hy], axis=1)
    x2 = _mm(mix, w["w_out"], tm=512, tn=1024, add=x1, name="out_proj")
    h3 = _rms_fwd(x2, sp["ffn2_norm_g"], "ffn2_norm")
    x3, gate2, up2 = _ffn_fwd(h3, x2, w["ffn2_w_gate"], w["ffn2_w_up"], w["ffn2_w_down"], "ffn2_fwd")
    dx3, dx3_16, sq = _loss(x3, tgt, "loss")
    loss = 0.5 * jnp.sum(sq) / d

    dx2, dx2_16, dg3 = _ffn_bwd(x2, sp["ffn2_norm_g"], h3, gate2, up2, dx3, dx3_16, w, put, "ffn2")
    dmix = _mm(dx2_16, w["w_out"], tb=True, tm=512, tn=1024, name="out_proj_dx")
    put("w_out", _mm(mix, dx2_16, ta=True, tm=512, tn=1024, name="out_proj_dw"))
    dq, dk, dv, dbias, dgq, dgk = _attn_bwd(proj, attn, dmix, bias, gq2, gk2, nb, seq)
    dbias = jnp.transpose(dbias.reshape(nb, ATTN_HEADS, CHUNK, BAND), (0, 2, 1, 3))
    dgq = jnp.sum(dgq, axis=(0, 1)).reshape(2, ATTN_HEAD_DIM).sum(axis=0, keepdims=True)
    dgk = jnp.sum(dgk, axis=(0, 1)).reshape(2, ATTN_HEAD_DIM).sum(axis=0, keepdims=True)
    dhq, dhf, dhi, dhg, dlb, dgo = _hgrn_bwd(proj, lb, sp["hgrn_out_norm_g"], ho, hstate, dmix, nb, seq)
    dproj = jnp.concatenate([dq, dk, dv, dhq, dhf, dhi, dhg], axis=1)
    put("w_in", _mm(h2, dproj, ta=True, tm=512, tn=512, name="in_proj_dw"))
    dh2 = _mm(dproj, w["w_in"], tb=True, tm=512, tn=1024, name="in_proj_dx")
    dx1, dx1_16, dgm = _rms_bwd(x1, sp["mix_norm_g"], dh2, dx2, "mix_norm_bwd")
    dx0, _, dg1 = _ffn_bwd(x, sp["ffn1_norm_g"], h1, gate1, up1, dx1, dx1_16, w, put, "ffn1")

    small = {
        "ffn1_norm_g": dg1, "mix_norm_g": dgm, "ffn2_norm_g": dg3,
        "attn_q_norm_g": dgq, "attn_k_norm_g": dgk,
        "attn_rel_bias": _bias_fold(dbias)[:, :N_REL],
        "hgrn_lower_bounds": _lb_bwd(sp["hgrn_lower_bounds"], dlb.reshape(nb, 1, HGRN_HEADS * HGRN_HEAD_DIM)),
        "hgrn_out_norm_g": jnp.sum(dgo, axis=(0, 1))[None, :],
    }
    return loss, dx0, small


MESH = pl.DeviceIdType.MESH
ANY = pl.BlockSpec(memory_space=pl.ANY)


def _coords():
    return lax.axis_index("x"), lax.axis_index("y"), lax.axis_index("c")


def _other_chips(x, y):
    return [(1 - x, y), (x, 1 - y), (1 - x, 1 - y)]


def _gather_side(shards):
    n = len(shards)

    def copies(ins, outs, sems):
        send_sems, recv_sems, local_sems = sems
        x, y, c = _coords()
        me, sibling = (x, y, c), (x, y, 1 - c)
        chips = _other_chips(x, y)

        def copy(i, k, block, to, src=None):
            bx, by, bc = block
            dst = outs[i].at[4 * bx + 2 * by + bc]
            return pltpu.make_async_remote_copy(
                src_ref=dst if src is None else src, dst_ref=dst, send_sem=send_sems.at[i, k],
                recv_sem=recv_sems.at[i, k], device_id=to, device_id_type=MESH)

        mine = [pltpu.make_async_copy(ins[i], outs[i].at[4 * x + 2 * y + c], local_sems.at[i]) for i in range(n)]
        own = []
        for i in range(n):
            own.append(copy(i, 0, me, sibling, src=ins[i]))
            own += [copy(i, 1 + j, me, (*chip, c), src=ins[i]) for j, chip in enumerate(chips)]
        return copy, mine, own, me, sibling, chips, c

    def start(ins, outs, sems):
        _, mine, own, *_ = copies(ins, outs, sems)
        for cp in mine + own:
            cp.start()

    def finish(ins, outs, sems):
        copy, mine, own, me, sibling, chips, c = copies(ins, outs, sems)
        passed = []
        for i in range(n):
            for j, chip in enumerate(chips):
                copy(i, 1 + j, (*chip, c), me).wait_recv()
                passed.append(copy(i, 4 + j, (*chip, c), sibling))
                passed[-1].start()
        for i in range(n):
            copy(i, 0, sibling, me).wait_recv()
            for j, chip in enumerate(chips):
                copy(i, 4 + j, (*chip, 1 - c), me).wait_recv()
        for cp in own + passed:
            cp.wait_send()
        for cp in mine:
            cp.wait()

    return _Side(list(shards), [jax.ShapeDtypeStruct((N_DEV,) + s.shape, s.dtype) for s in shards],
                 [pltpu.SemaphoreType.DMA((n, 7)), pltpu.SemaphoreType.DMA((n, 7)), pltpu.SemaphoreType.DMA((n,))],
                 start, finish)


def _pair_side(grads):
    n = len(grads)

    def copies(ins, outs, sems):
        send_sems, recv_sems = sems
        x, y, c = _coords()
        return [pltpu.make_async_remote_copy(
            src_ref=ins[i].at[2 * k + 1 - c], dst_ref=outs[i].at[k], send_sem=send_sems.at[i, k],
            recv_sem=recv_sems.at[i, k], device_id=(x, y, 1 - c), device_id_type=MESH)
            for i in range(n) for k in range(4)]

    def start(ins, outs, sems):
        for cp in copies(ins, outs, sems):
            cp.start()

    def finish(ins, outs, sems):
        for cp in copies(ins, outs, sems):
            cp.wait()

    return _Side(list(grads), [jax.ShapeDtypeStruct((4,) + g.shape[1:], g.dtype) for g in grads],
                 [pltpu.SemaphoreType.DMA((n, 4)), pltpu.SemaphoreType.DMA((n, 4))], start, finish)


def _pair_add(grad, recv, core, name):
    _, r, cdim = grad.shape

    def body(c_ref, g_ref, r_ref, o_ref):
        o_ref[...] = (g_ref[...] + r_ref[...]).astype(BF16)

    blk = (1, r, cdim)
    return pl.pallas_call(
        body, name=name,
        grid_spec=pltpu.PrefetchScalarGridSpec(
            num_scalar_prefetch=1, grid=(4,),
            in_specs=[pl.BlockSpec(blk, lambda k, c_ref: (2 * k + c_ref[0], 0, 0)),
                      pl.BlockSpec(blk, lambda k, c_ref: (k, 0, 0))],
            out_specs=pl.BlockSpec(blk, lambda k, c_ref: (k, 0, 0))),
        out_shape=jax.ShapeDtypeStruct((4, r, cdim), BF16),
        compiler_params=_params("arbitrary"),
    )(core, grad, recv)


def _chip_side(parts):
    n = len(parts)

    def copies(ins, outs, sems):
        send_sems, recv_sems, local_sems = sems
        x, y, c = _coords()
        chips = _other_chips(x, y)
        mine = [pltpu.make_async_copy(ins[i].at[2 * x + y], outs[i].at[2 * x + y], local_sems.at[i])
                for i in range(n)]
        sent = [pltpu.make_async_remote_copy(
            src_ref=ins[i].at[2 * px + py], dst_ref=outs[i].at[2 * x + y], send_sem=send_sems.at[i, j],
            recv_sem=recv_sems.at[i, j], device_id=(px, py, c), device_id_type=MESH)
            for i in range(n) for j, (px, py) in enumerate(chips)]
        return mine, sent, chips, c

    def start(ins, outs, sems):
        mine, sent, _, _ = copies(ins, outs, sems)
        for cp in mine + sent:
            cp.start()

    def finish(ins, outs, sems):
        mine, sent, chips, c = copies(ins, outs, sems)
        send_sems, recv_sems, _ = sems
        for i in range(n):
            for j, (px, py) in enumerate(chips):
                landed = outs[i].at[2 * px + py]
                pltpu.make_async_remote_copy(
                    src_ref=landed, dst_ref=landed, send_sem=send_sems.at[i, j], recv_sem=recv_sems.at[i, j],
                    device_id=(px, py, c), device_id_type=MESH).wait_recv()
        for cp in sent:
            cp.wait_send()
        for cp in mine:
            cp.wait()

    return _Side(list(parts), [jax.ShapeDtypeStruct(p.shape, p.dtype) for p in parts],
                 [pltpu.SemaphoreType.DMA((n, 3)), pltpu.SemaphoreType.DMA((n, 3)), pltpu.SemaphoreType.DMA((n,))],
                 start, finish)


def _all_reduce_small(v):
    r = v.shape[0]

    def body(v_ref, o_ref, buf, send_sems, recv_sems):
        x, y, c = _coords()
        me = 4 * x + 2 * y + c
        buf[me] = v_ref[...]
        cps = []
        for k in range(1, N_DEV):
            px = 1 - x if k & 4 else x
            py = 1 - y if k & 2 else y
            pc = 1 - c if k & 1 else c
            cps.append((pltpu.make_async_remote_copy(
                src_ref=v_ref, dst_ref=buf.at[me], send_sem=send_sems.at[k - 1], recv_sem=recv_sems.at[k - 1],
                device_id=(px, py, pc), device_id_type=MESH), 4 * px + 2 * py + pc))
        for cp, _ in cps:
            cp.start()
        for k, (cp, peer) in enumerate(cps):
            pltpu.make_async_remote_copy(
                src_ref=v_ref, dst_ref=buf.at[peer], send_sem=send_sems.at[k], recv_sem=recv_sems.at[k],
                device_id=(x, y, c), device_id_type=MESH).wait_recv()
        for cp, _ in cps:
            cp.wait_send()
        acc = buf[0]
        for j in range(1, N_DEV):
            acc = acc + buf[j]
        o_ref[...] = acc

    return pl.pallas_call(
        body, name="small_all_reduce", out_shape=jax.ShapeDtypeStruct(v.shape, F32),
        in_specs=[pl.BlockSpec(memory_space=pltpu.VMEM)], out_specs=pl.BlockSpec(memory_space=pltpu.VMEM),
        scratch_shapes=[pltpu.VMEM((N_DEV, r, 128), F32), pltpu.SemaphoreType.DMA((N_DEV - 1,)),
                        pltpu.SemaphoreType.DMA((N_DEV - 1,))],
    )(v)


def _adamw(w, m, v, g, name):
    r, cdim = w.shape
    parts = g.ndim == 3
    tr = r // 4 if r % 32 == 0 else r

    def body(w_ref, m_ref, v_ref, g_ref, go_ref, d_ref, mo_ref, vo_ref):
        if parts:
            gv = g_ref[0].astype(F32)
            for k in range(1, 4):
                gv = gv + g_ref[k].astype(F32)
        else:
            gv = g_ref[...]
        m2 = ADAM_B1 * m_ref[...] + (1.0 - ADAM_B1) * gv
        v2 = ADAM_B2 * v_ref[...] + (1.0 - ADAM_B2) * (gv * gv)
        m_hat = m2 / (1.0 - ADAM_B1 ** ADAM_STEP)
        v_hat = v2 / (1.0 - ADAM_B2 ** ADAM_STEP)
        go_ref[...] = gv
        d_ref[...] = -ADAM_LR * (m_hat / (jnp.sqrt(v_hat) + ADAM_EPS) + ADAM_WD * w_ref[...])
        mo_ref[...] = m2
        vo_ref[...] = v2

    row = pl.BlockSpec((tr, cdim), lambda i: (i, 0))
    g_spec = pl.BlockSpec((4, tr, cdim), lambda i: (0, i, 0)) if parts else row
    return pl.pallas_call(
        body, name=name, grid=(r // tr,), in_specs=[row, row, row, g_spec], out_specs=[row] * 4,
        out_shape=[jax.ShapeDtypeStruct((r, cdim), F32)] * 4,
        compiler_params=_params("parallel"),
    )(w, m, v, g)


WEIGHTS = ["ffn1_norm_g", "ffn1_w_gate", "ffn1_w_up", "ffn1_w_down", "mix_norm_g", "w_in", "attn_q_norm_g",
           "attn_k_norm_g", "attn_rel_bias", "hgrn_lower_bounds", "hgrn_out_norm_g", "w_out", "ffn2_norm_g",
           "ffn2_w_gate", "ffn2_w_up", "ffn2_w_down"]
COL_SHARDED = ("ffn1_w_gate", "ffn1_w_up", "w_in", "ffn2_w_gate", "ffn2_w_up")
ROW_SHARDED = ("ffn1_w_down", "w_out", "ffn2_w_down")
BIG = [n for n in WEIGHTS if n in COL_SHARDED or n in ROW_SHARDED]
SMALL = [n for n in WEIGHTS if n not in BIG]
PACK_ROWS = 8
FFN1 = ["ffn1_w_down", "ffn1_w_gate", "ffn1_w_up"]
FFN2 = ["ffn2_w_down", "ffn2_w_gate", "ffn2_w_up"]
MIXER = ["w_out", "w_in"]

PLAN = {
    "ffn1_fwd": ("gather", MIXER),
    "attn_fwd": ("gather", FFN2),
    "ffn2_dh_gate": ("pair", FFN2),
    "attn_bwd": ("chip", FFN2),
    "in_proj_dx": ("pair", MIXER),
    "ffn1_bwd_mid": ("chip", MIXER),
    "ffn1_dh_gate": ("pair", FFN1),
    "ffn1_dh_up": ("chip", FFN1),
}


class _Schedule:
    def __init__(self, shards):
        self.shards = shards
        self.weights = {}
        self.sliced = {}
        self.partials = {}
        self.reduced = {}

    def put(self, name, grad):
        r, c = self.shards[name].shape
        if name in COL_SHARDED:
            self.sliced[name] = jnp.transpose(grad.reshape(r, N_DEV, c), (1, 0, 2))
        else:
            self.sliced[name] = grad.reshape(N_DEV, r, c)

    def side_for(self, call):
        if call not in PLAN:
            return None
        kind, names = PLAN[call]
        if kind == "gather":
            return _gather_side([self.shards[n] for n in names])
        if kind == "pair":
            return _pair_side([self.sliced[n] for n in names])
        return _chip_side([self.partials[n] for n in names])

    def done(self, call, outs):
        self.file(*PLAN[call], outs)

    def file(self, kind, names, outs):
        for n, o in zip(names, outs):
            if kind == "gather":
                if n in COL_SHARDED:
                    self.weights[n] = jnp.transpose(o, (1, 0, 2)).reshape(o.shape[1], N_DEV * o.shape[2])
                else:
                    self.weights[n] = o.reshape(N_DEV * o.shape[1], o.shape[2])
            elif kind == "pair":
                core = lax.axis_index("c").astype(jnp.int32).reshape(1)
                self.partials[n] = _pair_add(self.sliced[n], o, core, n + "_pair_add")
            else:
                self.reduced[n] = o

    def gather_now(self, names, call):
        self.file("gather", names, _run_side(_gather_side([self.shards[n] for n in names]), call))


def _pack_small(vals, loss=None):
    parts = []
    for n in SMALL:
        a = vals[n]
        if n == "attn_rel_bias":
            a = jnp.pad(a.reshape(ATTN_HEADS, N_REL), ((0, 0), (0, N_REL_PAD - N_REL)))
        flat = a.reshape(-1)
        size = -(-flat.shape[0] // (PACK_ROWS * 128)) * PACK_ROWS * 128
        parts.append(jnp.pad(flat, (0, size - flat.shape[0])).reshape(-1, 128))
    tail = jnp.zeros((PACK_ROWS, 128), F32)
    if loss is not None:
        tail = tail.at[0, 0].set(loss)
    return jnp.concatenate(parts + [tail], axis=0)


def _unpack_small(packed, shapes):
    out, row = {}, 0
    for n in SMALL:
        shape = shapes[n]
        if n == "attn_rel_bias":
            rows = ATTN_HEADS * N_REL_PAD // 128
            out[n] = packed[row:row + rows].reshape(ATTN_HEADS, N_REL_PAD)[:, :N_REL].reshape(shape)
        else:
            size = 1
            for s in shape:
                size *= s
            rows = -(-size // (PACK_ROWS * 128)) * PACK_ROWS
            out[n] = packed[row:row + rows].reshape(-1)[:size].reshape(shape)
        row += rows
    return out, packed[row, 0]


def kernel(x, ffn1_norm_g, ffn1_w_gate, ffn1_w_up, ffn1_w_down, mix_norm_g, w_in, attn_q_norm_g, attn_k_norm_g, attn_rel_bias, hgrn_lower_bounds, hgrn_out_norm_g, w_out, ffn2_norm_g, ffn2_w_gate, ffn2_w_up, ffn2_w_down, loss_target, m_ffn1_norm_g, m_ffn1_w_gate, m_ffn1_w_up, m_ffn1_w_down, m_mix_norm_g, m_w_in, m_attn_q_norm_g, m_attn_k_norm_g, m_attn_rel_bias, m_hgrn_lower_bounds, m_hgrn_out_norm_g, m_w_out, m_ffn2_norm_g, m_ffn2_w_gate, m_ffn2_w_up, m_ffn2_w_down, v_ffn1_norm_g, v_ffn1_w_gate, v_ffn1_w_up, v_ffn1_w_down, v_mix_norm_g, v_w_in, v_attn_q_norm_g, v_attn_k_norm_g, v_attn_rel_bias, v_hgrn_lower_bounds, v_hgrn_out_norm_g, v_w_out, v_ffn2_norm_g, v_ffn2_w_gate, v_ffn2_w_up, v_ffn2_w_down):
    wts = dict(zip(WEIGHTS, (ffn1_norm_g, ffn1_w_gate, ffn1_w_up, ffn1_w_down, mix_norm_g, w_in, attn_q_norm_g,
                             attn_k_norm_g, attn_rel_bias, hgrn_lower_bounds, hgrn_out_norm_g, w_out, ffn2_norm_g,
                             ffn2_w_gate, ffn2_w_up, ffn2_w_down)))
    mom = dict(zip(WEIGHTS, (m_ffn1_norm_g, m_ffn1_w_gate, m_ffn1_w_up, m_ffn1_w_down, m_mix_norm_g, m_w_in,
                             m_attn_q_norm_g, m_attn_k_norm_g, m_attn_rel_bias, m_hgrn_lower_bounds,
                             m_hgrn_out_norm_g, m_w_out, m_ffn2_norm_g, m_ffn2_w_gate, m_ffn2_w_up, m_ffn2_w_down)))
    var = dict(zip(WEIGHTS, (v_ffn1_norm_g, v_ffn1_w_gate, v_ffn1_w_up, v_ffn1_w_down, v_mix_norm_g, v_w_in,
                             v_attn_q_norm_g, v_attn_k_norm_g, v_attn_rel_bias, v_hgrn_lower_bounds,
                             v_hgrn_out_norm_g, v_w_out, v_ffn2_norm_g, v_ffn2_w_gate, v_ffn2_w_up, v_ffn2_w_down)))
    nb, seq, d = x.shape
    shapes = {n: wts[n].shape for n in WEIGHTS}

    sched = _Schedule({n: wts[n][0].astype(BF16) for n in BIG})
    sched.gather_now(FFN1, "ffn1_weights_all_gather")
    sp = {n: wts[n] for n in SMALL}
    sp["attn_rel_bias"] = wts["attn_rel_bias"][0]
    _ACTIVE[0] = sched
    try:
        loss, dx, dsmall = _local_step(x.reshape(nb * seq, d), loss_target.reshape(nb * seq, d), sp,
                                       sched.weights, sched.put, nb, seq)
    finally:
        _ACTIVE[0] = None
    reduced = sched.reduced

    small_sum = _all_reduce_small(_pack_small(dsmall, loss))
    gsmall, loss_total = _unpack_small(small_sum, shapes)

    grads, deltas, new_m, new_v = {}, {}, {}, {}
    for n in BIG:
        out = _adamw(wts[n][0], mom[n][0], var[n][0], reduced[n], n + "_adamw")
        grads[n], deltas[n], new_m[n], new_v[n] = (o.reshape(shapes[n]) for o in out)
    packed = _adamw(_pack_small(wts), _pack_small(mom), _pack_small(var), small_sum, "small_adamw")
    for dst, p in zip((deltas, new_m, new_v), packed[1:]):
        dst.update(_unpack_small(p, shapes)[0])
    grads.update(gsmall)

    return (loss_total, dx.reshape(nb, seq, d), *[grads[n] for n in WEIGHTS], *[deltas[n] for n in WEIGHTS],
            *[new_m[n] for n in WEIGHTS], *[new_v[n] for n in WEIGHTS])
```

```python
import functools

import jax
import jax.numpy as jnp
from jax import lax
from jax.experimental import pallas as pl
from jax.experimental.pallas import tpu as pltpu

F32 = jnp.float32
BF16 = jnp.bfloat16

RMS_EPS = 1e-6
CHUNK = 64
LEFT_CHUNKS = 8
BAND = (LEFT_CHUNKS + 2) * CHUNK
KPAD = BAND - CHUNK
REL_CLIP = 128
N_REL = 2 * REL_CLIP + 1
N_REL_PAD = 384
ATTN_HEADS = 8
ATTN_HEAD_DIM = 64
ATTN_WIDTH = ATTN_HEADS * ATTN_HEAD_DIM
ATTN_UNROLL = 4
HGRN_HEADS = 4
HGRN_HEAD_DIM = 128
HGRN_PER_STEP = 2
SUB = 16
N_SUB = CHUNK // SUB
N_DEV = 8

ADAM_LR = 0.001
ADAM_B1 = 0.9
ADAM_B2 = 0.999
ADAM_EPS = 1e-08
ADAM_WD = 0.01
ADAM_STEP = 10

VMEM_LIMIT = 56 * 1024 * 1024

NT = (((1,), (1,)), ((), ()))
NN = (((1,), (0,)), ((), ()))


def _params(*sem):
    return pltpu.CompilerParams(dimension_semantics=sem, vmem_limit_bytes=VMEM_LIMIT)


def _sigmoid(v):
    return 0.5 * jnp.tanh(0.5 * v) + 0.5


def _dot(a, b, dims=NN):
    return lax.dot_general(a.astype(BF16), b.astype(BF16), dims, preferred_element_type=F32)


def _dot_exact01(m01, v):
    m = m01.astype(BF16)
    hi = v.astype(BF16)
    r1 = v - hi.astype(F32)
    mid = r1.astype(BF16)
    lo = (r1 - mid.astype(F32)).astype(BF16)
    out = lax.dot_general(m, hi, NN, preferred_element_type=F32)
    out = out + lax.dot_general(m, mid, NN, preferred_element_type=F32)
    return out + lax.dot_general(m, lo, NN, preferred_element_type=F32)


def _dot_exact01_r(v, m01):
    m = m01.astype(BF16)
    hi = v.astype(BF16)
    r1 = v - hi.astype(F32)
    mid = r1.astype(BF16)
    lo = (r1 - mid.astype(F32)).astype(BF16)
    out = lax.dot_general(hi, m, NN, preferred_element_type=F32)
    out = out + lax.dot_general(mid, m, NN, preferred_element_type=F32)
    return out + lax.dot_general(lo, m, NN, preferred_element_type=F32)


def _tn(a, b):
    ap = jnp.concatenate([a, jnp.zeros_like(a)], axis=0)
    bp = jnp.concatenate([b, jnp.zeros_like(b)], axis=0)
    return _dot(ap.T, bp)


def _row_tile(t):
    for tm in (512, 256, 128, 64, 32, 16, 8):
        if t % tm == 0:
            return tm
    raise ValueError(t)


class _Side:
    def __init__(self, ins, out_shape, sems, start, finish):
        self.ins, self.out_shape, self.sems, self.start, self.finish = ins, out_shape, sems, start, finish


_ACTIVE = [None]


def _pallas(body, *, name, grid, in_specs, out_specs, out_shape, scratch_shapes=(), sem, args):
    sched = _ACTIVE[0]
    side = sched.side_for(name) if sched is not None else None
    if side is None:
        return pl.pallas_call(
            body, name=name, grid=grid, in_specs=list(in_specs), out_specs=list(out_specs),
            out_shape=list(out_shape), scratch_shapes=list(scratch_shapes), compiler_params=_params(*sem))(*args)
    cuts = [len(in_specs), len(side.ins), len(out_shape), len(side.out_shape), len(scratch_shapes)]

    def with_side(*refs):
        groups, at = [], 0
        for n in cuts:
            groups.append(refs[at:at + n])
            at += n
        ins, side_ins, outs, side_outs, scratch = groups
        side_sems = refs[at:]
        first = pl.program_id(0) == 0
        last = pl.program_id(0) == grid[0] - 1
        for a in range(1, len(grid)):
            first = jnp.logical_and(first, pl.program_id(a) == 0)
            last = jnp.logical_and(last, pl.program_id(a) == grid[a] - 1)

        @pl.when(first)
        def _():
            side.start(side_ins, side_outs, side_sems)

        body(*ins, *outs, *scratch)

        @pl.when(last)
        def _():
            side.finish(side_ins, side_outs, side_sems)

    hbm = pl.BlockSpec(memory_space=pl.ANY)
    res = pl.pallas_call(
        with_side, name=name, grid=grid, in_specs=list(in_specs) + [hbm] * len(side.ins),
        out_specs=list(out_specs) + [hbm] * len(side.out_shape), out_shape=list(out_shape) + list(side.out_shape),
        scratch_shapes=list(scratch_shapes) + list(side.sems),
        compiler_params=_params(*(["arbitrary"] * len(grid))))(*args, *side.ins)
    sched.done(name, res[len(out_shape):])
    return res[:len(out_shape)]


def _run_side(side, name):
    n_in, n_out = len(side.ins), len(side.out_shape)

    def body(*refs):
        ins, outs, sems = refs[:n_in], refs[n_in:n_in + n_out], refs[n_in + n_out:]
        side.start(ins, outs, sems)
        side.finish(ins, outs, sems)

    hbm = pl.BlockSpec(memory_space=pl.ANY)
    return pl.pallas_call(body, name=name, in_specs=[hbm] * n_in, out_specs=[hbm] * n_out,
                          out_shape=list(side.out_shape), scratch_shapes=list(side.sems))(*side.ins)


def _rms_fwd(x, g, name):
    t, d = x.shape
    tm = _row_tile(t)

    def body(x_ref, g_ref, h_ref):
        xv = x_ref[...]
        r = lax.rsqrt(jnp.mean(xv * xv, axis=-1, keepdims=True) + RMS_EPS)
        h_ref[...] = (xv * r * g_ref[...]).astype(BF16)

    return pl.pallas_call(
        body, name=name, grid=(t // tm,),
        in_specs=[pl.BlockSpec((tm, d), lambda i: (i, 0)), pl.BlockSpec((1, d), lambda i: (0, 0))],
        out_specs=pl.BlockSpec((tm, d), lambda i: (i, 0)),
        out_shape=jax.ShapeDtypeStruct((t, d), BF16),
        compiler_params=_params("parallel"),
    )(x, g)


def _accumulate(ref, part, step):
    @pl.when(step == 0)
    def _():
        ref[...] = part

    @pl.when(step > 0)
    def _():
        ref[...] += part


def _mm(a, b, *, ta=False, tb=False, tm, tn, out_dtype=F32, add=None, scale=1.0, norm_g=None, norm_bwd=None, name):
    m, k = (a.shape[1], a.shape[0]) if ta else a.shape
    n = b.shape[0] if tb else b.shape[1]
    tm, tn = min(tm, m), min(tn, n)
    assert m % tm == 0 and n % tn == 0, (m, n, tm, tn)
    assert (norm_g is None and norm_bwd is None) or tn == n
    dims = (((0 if ta else 1,), (1 if tb else 0,)), ((), ()))
    n_in = 2 + (add is not None) + (norm_g is not None) + (3 if norm_bwd is not None else 0)

    def body(*refs):
        ins, outs = list(refs[2:n_in]), refs[n_in:]
        r = lax.dot_general(refs[0][...].astype(BF16), refs[1][...].astype(BF16), dims, preferred_element_type=F32)
        if scale != 1.0:
            r = r * scale
        if add is not None:
            r = r + ins.pop(0)[...]
        if norm_bwd is not None:
            xv, gv, dres = (ref[...] for ref in ins)
            rs = lax.rsqrt(jnp.mean(xv * xv, axis=-1, keepdims=True) + RMS_EPS)
            xhat = xv * rs
            gd = r * gv
            dx = dres + rs * (gd - xhat * jnp.mean(gd * xhat, axis=-1, keepdims=True))
            outs[0][...] = dx
            outs[1][...] = dx.astype(BF16)
            _accumulate(outs[2], jnp.sum(r * xhat, axis=0, keepdims=True), pl.program_id(0))
            return
        outs[0][...] = r.astype(out_dtype)
        if norm_g is not None:
            rs = lax.rsqrt(jnp.mean(r * r, axis=-1, keepdims=True) + RMS_EPS)
            outs[1][...] = (r * rs * ins.pop(0)[...]).astype(BF16)

    a_spec = pl.BlockSpec((k, tm), lambda i, j: (0, i)) if ta else pl.BlockSpec((tm, k), lambda i, j: (i, 0))
    b_spec = pl.BlockSpec((tn, k), lambda i, j: (j, 0)) if tb else pl.BlockSpec((k, tn), lambda i, j: (0, j))
    o_spec = pl.BlockSpec((tm, tn), lambda i, j: (i, j))
    vec = pl.BlockSpec((1, tn), lambda i, j: (0, j))
    args, specs = [a, b], [a_spec, b_spec]
    if add is not None:
        args.append(add)
        specs.append(o_spec)
    out_specs, out_shape = [o_spec], [jax.ShapeDtypeStruct((m, n), out_dtype)]
    if norm_g is not None:
        args.append(norm_g)
        specs.append(vec)
        out_specs.append(o_spec)
        out_shape.append(jax.ShapeDtypeStruct((m, n), BF16))
    if norm_bwd is not None:
        args += list(norm_bwd)
        specs += [o_spec, vec, o_spec]
        out_specs = [o_spec, o_spec, vec]
        out_shape = [jax.ShapeDtypeStruct((m, n), F32), jax.ShapeDtypeStruct((m, n), BF16),
                     jax.ShapeDtypeStruct((1, n), F32)]
    res = _pallas(body, name=name, grid=(m // tm, n // tn), in_specs=specs, out_specs=out_specs, out_shape=out_shape,
                  sem=("arbitrary", "arbitrary") if norm_bwd is not None else ("parallel", "parallel"), args=args)
    return res[0] if len(res) == 1 else res


def _ffn_tile(f):
    for tf in (1408, 512, 256, 128):
        if f % tf == 0:
            return tf
    raise ValueError(f)


def _ffn_fwd(h, x, wg, wu, wd, name, next_g=None, tgt=None):
    t, d = x.shape
    f = wg.shape[1]
    tm, tf = _row_tile(t), _ffn_tile(f)
    nf = f // tf
    assert (next_g is None) != (tgt is None)

    def body(h_ref, x_ref, wg_ref, wu_ref, wd_ref, tail_ref, g_ref, u_ref, o0_ref, o1_ref, *rest):
        acc_ref = rest[-1]
        j = pl.program_id(1)
        hv = h_ref[...]
        gv = lax.dot_general(hv, wg_ref[...], NN, preferred_element_type=F32)
        uv = lax.dot_general(hv, wu_ref[...], NN, preferred_element_type=F32)
        av = gv * _sigmoid(gv) * uv
        g_ref[...] = gv.astype(BF16)
        u_ref[...] = uv.astype(BF16)
        _accumulate(acc_ref, lax.dot_general(av.astype(BF16), wd_ref[...], NN, preferred_element_type=F32), j)

        @pl.when(j == nf - 1)
        def _():
            y = x_ref[...] + 0.5 * acc_ref[...]
            if tgt is None:
                o0_ref[...] = y
                rs = lax.rsqrt(jnp.mean(y * y, axis=-1, keepdims=True) + RMS_EPS)
                o1_ref[...] = (y * rs * tail_ref[...]).astype(BF16)
            else:
                e = y - tail_ref[...]
                dy = e * (1.0 / d)
                o0_ref[...] = dy
                o1_ref[...] = dy.astype(BF16)
                _accumulate(rest[0], jnp.sum(e * e, axis=0, keepdims=True), pl.program_id(0))

    row = pl.BlockSpec((tm, d), lambda i, j: (i, 0))
    hid = pl.BlockSpec((tm, tf), lambda i, j: (i, j))
    vec = pl.BlockSpec((1, d), lambda i, j: (0, 0))
    out_specs = [hid, hid, row, row] + ([vec] if tgt is not None else [])
    out_shape = [jax.ShapeDtypeStruct((t, f), BF16)] * 2 + [jax.ShapeDtypeStruct((t, d), F32),
                                                            jax.ShapeDtypeStruct((t, d), BF16)]
    if tgt is not None:
        out_shape.append(jax.ShapeDtypeStruct((1, d), F32))
    return _pallas(
        body, name=name, grid=(t // tm, nf),
        in_specs=[row, row, pl.BlockSpec((d, tf), lambda i, j: (0, j)), pl.BlockSpec((d, tf), lambda i, j: (0, j)),
                  pl.BlockSpec((tf, d), lambda i, j: (j, 0)), vec if tgt is None else row],
        out_specs=out_specs, out_shape=out_shape, scratch_shapes=[pltpu.VMEM((tm, d), F32)],
        sem=("parallel" if tgt is None else "arbitrary", "arbitrary"),
        args=(h, x, wg, wu, wd, next_g if tgt is None else tgt))


def _ffn_bwd_mid(dy, wd, g, u, name):
    t, d = dy.shape
    f = wd.shape[0]
    tm, tf = _row_tile(t), _ffn_tile(f)

    def body(dy_ref, wd_ref, g_ref, u_ref, dg_ref, du_ref, a_ref):
        da = 0.5 * lax.dot_general(dy_ref[...].astype(BF16), wd_ref[...], NT, preferred_element_type=F32)
        gv = g_ref[...].astype(F32)
        uv = u_ref[...].astype(F32)
        s = _sigmoid(gv)
        silu = gv * s
        dg_ref[...] = (da * uv * (s * (1.0 + gv * (1.0 - s)))).astype(BF16)
        du_ref[...] = (da * silu).astype(BF16)
        a_ref[...] = (silu * uv).astype(BF16)

    hid = pl.BlockSpec((tm, tf), lambda i, j: (i, j))
    return _pallas(
        body, name=name, grid=(t // tm, f // tf),
        in_specs=[pl.BlockSpec((tm, d), lambda i, j: (i, 0)), pl.BlockSpec((tf, d), lambda i, j: (j, 0)), hid, hid],
        out_specs=[hid, hid, hid], out_shape=[jax.ShapeDtypeStruct((t, f), BF16)] * 3,
        sem=("parallel", "parallel"), args=(dy, wd, g, u))


def _rel_index(t, s_band):
    return jnp.clip(t + KPAD - s_band, -REL_CLIP, REL_CLIP) + REL_CLIP


def _bias_expand(rel_bias_pad):
    nh = rel_bias_pad.shape[0]

    def body(rb_ref, out_ref):
        rb = rb_ref[...]
        i_io = lax.broadcasted_iota(jnp.int32, (N_REL_PAD, BAND), 0)
        s_io = lax.broadcasted_iota(jnp.int32, (N_REL_PAD, BAND), 1)

        def row(t, carry):
            onehot = (i_io == _rel_index(t, s_io)).astype(F32)
            out_ref[t] = _dot_exact01_r(rb, onehot)
            return carry

        lax.fori_loop(0, CHUNK, row, 0)

    return pl.pallas_call(
        body, name="bias_expand", out_shape=jax.ShapeDtypeStruct((CHUNK, nh, BAND), F32),
        compiler_params=pltpu.CompilerParams(vmem_limit_bytes=VMEM_LIMIT),
    )(rel_bias_pad)


def _bias_fold(dbias):
    ng, nh = dbias.shape[0], dbias.shape[2]

    def body(db_ref, out_ref):
        s_io = lax.broadcasted_iota(jnp.int32, (BAND, N_REL_PAD), 0)
        i_io = lax.broadcasted_iota(jnp.int32, (BAND, N_REL_PAD), 1)

        def row(t, acc):
            onehot = (i_io == _rel_index(t, s_io)).astype(F32)
            d = db_ref[0, t]
            for gi in range(1, ng):
                d = d + db_ref[gi, t]
            return acc + _dot_exact01_r(d, onehot)

        out_ref[...] = lax.fori_loop(0, CHUNK, row, jnp.zeros((nh, N_REL_PAD), F32))

    return pl.pallas_call(
        body, name="bias_fold", out_shape=jax.ShapeDtypeStruct((nh, N_REL_PAD), F32),
        compiler_params=pltpu.CompilerParams(vmem_limit_bytes=VMEM_LIMIT),
    )(dbias)


def _left_half(shape):
    return lax.broadcasted_iota(jnp.int32, shape, len(shape) - 1) < ATTN_HEAD_DIM


def _stack_heads(v):
    left = _left_half(v.shape)
    zero = jnp.zeros_like(v)
    return jnp.concatenate([jnp.where(left, v, zero), jnp.where(left, zero, v)], axis=0)


def _unstack_heads(v):
    return jnp.where(_left_half((CHUNK, 128)), v[0:CHUNK, :], v[CHUNK:2 * CHUNK, :])


def _half_mean(v):
    r = lax.broadcasted_iota(jnp.int32, (128, 128), 0) < ATTN_HEAD_DIM
    c = lax.broadcasted_iota(jnp.int32, (128, 128), 1) < ATTN_HEAD_DIM
    return _dot_exact01_r(v, r == c) * (1.0 / ATTN_HEAD_DIM)


def _attn_prepare(q_ref, k_ref, v_ref, gq_ref, gk_ref, qs_scr, k_scr, v_scr):
    q, k = q_ref[...], k_ref[...]
    rq = lax.rsqrt(_half_mean(q * q) + RMS_EPS)
    rk = lax.rsqrt(_half_mean(k * k) + RMS_EPS)
    qhat, khat = q * rq, k * rk
    qs_scr[...] = (qhat * gq_ref[...] * ATTN_HEAD_DIM ** -0.5).astype(BF16)
    k_scr[0:KPAD, :] = jnp.zeros((KPAD, 128), BF16)
    v_scr[0:KPAD, :] = jnp.zeros((KPAD, 128), BF16)
    k_scr[KPAD:, :] = (khat * gk_ref[...]).astype(BF16)
    v_scr[KPAD:, :] = v_ref[...].astype(BF16)
    return qhat, rq, khat, rk


def _attn_scores(qs_scr, k_scr, bias_ref, c):
    r0 = pl.multiple_of(c * CHUNK, CHUNK)
    qst = _stack_heads(qs_scr[pl.ds(r0, CHUNK), :])
    kb = k_scr[pl.ds(r0, BAND), :]
    s = lax.dot_general(qst, kb, NT, preferred_element_type=F32) + bias_ref[...]
    col = lax.broadcasted_iota(jnp.int32, (2 * CHUNK, BAND), 1)
    first = jnp.maximum(CHUNK, (LEFT_CHUNKS + 1 - c) * CHUNK)
    s = jnp.where(col >= first, s, -jnp.inf)
    e = jnp.exp(s - jnp.max(s, axis=-1, keepdims=True))
    return e, 1.0 / jnp.sum(e, axis=-1, keepdims=True), qst, kb, r0


def _attn_fwd(proj, bias, gq, gk, nb, seq):
    nc = seq // CHUNK

    def body(q_ref, k_ref, v_ref, bias_ref, gq_ref, gk_ref, o_ref, qs_scr, k_scr, v_scr):
        _attn_prepare(q_ref, k_ref, v_ref, gq_ref, gk_ref, qs_scr, k_scr, v_scr)

        def chunk(c, carry):
            e, inv, _, _, r0 = _attn_scores(qs_scr, k_scr, bias_ref, c)
            vb = v_scr[pl.ds(r0, BAND), :]
            o_ref[pl.ds(r0, CHUNK), :] = _unstack_heads(
                lax.dot_general(e.astype(BF16), vb, NN, preferred_element_type=F32) * inv)
            return carry

        lax.fori_loop(0, nc, chunk, 0, unroll=ATTN_UNROLL)

    def col(off):
        return pl.BlockSpec((seq, 128), lambda b, hp: (b, off + hp))

    vec = pl.BlockSpec((1, 128), lambda b, hp: (0, 0))
    return _pallas(
        body, name="attn_fwd", grid=(nb, ATTN_HEADS // 2),
        in_specs=[col(0), col(4), col(8), pl.BlockSpec((2 * CHUNK, BAND), lambda b, hp: (hp, 0)), vec, vec],
        out_specs=[pl.BlockSpec((seq, 128), lambda b, hp: (b, hp))],
        out_shape=[jax.ShapeDtypeStruct((nb * seq, ATTN_WIDTH), F32)],
        scratch_shapes=[pltpu.VMEM((seq, 128), BF16), pltpu.VMEM((seq + KPAD, 128), BF16),
                        pltpu.VMEM((seq + KPAD, 128), BF16)],
        sem=("parallel", "parallel"), args=(proj, proj, proj, bias, gq, gk))[0]


def _attn_bwd(proj, out, dout, bias, gq, gk, nb, seq):
    nc = seq // CHUNK
    scale = ATTN_HEAD_DIM ** -0.5

    def body(q_ref, k_ref, v_ref, o_ref, do_ref, bias_ref, gq_ref, gk_ref,
             dq_ref, dk_ref, dv_ref, dbias_ref, dgq_ref, dgk_ref,
             qs_scr, k_scr, v_scr, dqn_scr, dk_scr, dv_scr, db_scr):
        qhat, rq, khat, rk = _attn_prepare(q_ref, k_ref, v_ref, gq_ref, gk_ref, qs_scr, k_scr, v_scr)
        dk_scr[...] = jnp.zeros_like(dk_scr)
        dv_scr[...] = jnp.zeros_like(dv_scr)
        db_scr[...] = jnp.zeros_like(db_scr)

        def chunk(c, carry):
            e, inv, qst, kb, r0 = _attn_scores(qs_scr, k_scr, bias_ref, c)
            p = e * inv
            vb = v_scr[pl.ds(r0, BAND), :]
            do_c = do_ref[pl.ds(r0, CHUNK), :]
            dost = _stack_heads(do_c)
            drow = jnp.sum(dost * _stack_heads(o_ref[pl.ds(r0, CHUNK), :]), axis=-1, keepdims=True)
            dp = lax.dot_general(dost.astype(BF16), vb, NT, preferred_element_type=F32)
            ds = p * (dp - drow)
            db_scr[...] += ds
            dqn_scr[pl.ds(r0, CHUNK), :] = scale * _unstack_heads(
                lax.dot_general(ds.astype(BF16), kb, NN, preferred_element_type=F32))
            dk_scr[pl.ds(r0, BAND), :] += lax.dot_general(ds.T.astype(BF16), qst, NN, preferred_element_type=F32)
            dv_scr[pl.ds(r0, BAND), :] += _dot(p.T, dost)
            return carry

        lax.fori_loop(0, nc, chunk, 0, unroll=ATTN_UNROLL)

        def norm_bwd(dn, hat, r, g_ref):
            gd = dn * g_ref[...]
            return r * (gd - hat * _half_mean(gd * hat)), jnp.sum(dn * hat, axis=0, keepdims=True)

        dq, dgq = norm_bwd(dqn_scr[...], qhat, rq, gq_ref)
        dk, dgk = norm_bwd(dk_scr[KPAD:, :], khat, rk, gk_ref)
        dq_ref[...] = dq.astype(BF16)
        dk_ref[...] = dk.astype(BF16)
        dv_ref[...] = dv_scr[KPAD:, :].astype(BF16)
        dbias_ref[0] = db_scr[...]
        dgq_ref[0] = dgq
        dgk_ref[0] = dgk

    def col(off):
        return pl.BlockSpec((seq, 128), lambda b, hp: (b, off + hp))

    vec = pl.BlockSpec((1, 128), lambda b, hp: (0, 0))
    gvec = pl.BlockSpec((1, 1, 128), lambda b, hp: (b * (ATTN_HEADS // 2) + hp, 0, 0))
    t = nb * seq
    return _pallas(
        body, name="attn_bwd", grid=(nb, ATTN_HEADS // 2),
        in_specs=[col(0), col(4), col(8), col(0), col(0),
                  pl.BlockSpec((2 * CHUNK, BAND), lambda b, hp: (hp, 0)), vec, vec],
        out_specs=[col(0), col(0), col(0), pl.BlockSpec((1, 2 * CHUNK, BAND), lambda b, hp: (b, hp, 0)),
                   gvec, gvec],
        out_shape=[jax.ShapeDtypeStruct((t, ATTN_WIDTH), BF16)] * 3
        + [jax.ShapeDtypeStruct((nb, ATTN_HEADS * CHUNK, BAND), F32)]
        + [jax.ShapeDtypeStruct((nb * ATTN_HEADS // 2, 1, 128), F32)] * 2,
        scratch_shapes=[pltpu.VMEM((seq, 128), BF16), pltpu.VMEM((seq + KPAD, 128), BF16),
                        pltpu.VMEM((seq + KPAD, 128), BF16), pltpu.VMEM((seq, 128), F32),
                        pltpu.VMEM((seq + KPAD, 128), F32), pltpu.VMEM((seq + KPAD, 128), F32),
                        pltpu.VMEM((2 * CHUNK, BAND), F32)],
        sem=("parallel", "parallel"), args=(proj, proj, proj, out, dout, bias, gq, gk))


def _tri(lower):
    r = lax.broadcasted_iota(jnp.int32, (CHUNK, CHUNK), 0)
    c = lax.broadcasted_iota(jnp.int32, (CHUNK, CHUNK), 1)
    return (r >= c) if lower else (r <= c)


def _hgrn_gates(hq, hf, lb):
    sq = _sigmoid(hq)
    sf = _sigmoid(hf)
    return hq * sq, sq, sf, lb + (1.0 - lb) * sf


def _hgrn_offdiag(q_s, k_s, b_s):
    row = lax.broadcasted_iota(jnp.int32, (CHUNK, HGRN_HEAD_DIM), 0)
    bv, qv, kv = b_s[...], q_s[...], k_s[...]
    eqs, eks = [], []
    for i in range(1, N_SUB):
        r = b_s[pl.ds(SUB * i - 1, 1), :]
        in_i = (row >= SUB * i) & (row < SUB * (i + 1))
        eqs.append(jnp.exp(jnp.where(in_i, bv - r, -jnp.inf)))
        eks.append(jnp.exp(jnp.where(row < SUB * i, r - bv, -jnp.inf)))
    eq = jnp.concatenate(eqs, axis=1)
    ek = jnp.concatenate(eks, axis=1)
    qt = jnp.concatenate([qv] * (N_SUB - 1), axis=1) * eq
    kt = jnp.concatenate([kv] * (N_SUB - 1), axis=1) * ek
    return qt, kt, eq, ek


def _hgrn_diag_e(b_s, i, s):
    t_io = lax.broadcasted_iota(jnp.int32, (SUB, HGRN_HEAD_DIM), 0)
    bi = b_s[pl.ds(SUB * i, SUB), :]
    return jnp.exp(jnp.where(t_io >= s, bi - b_s[pl.ds(SUB * i + s, 1), :], -jnp.inf)), t_io


def _hgrn_intra(q_s, k_s, b_s, a_s, qt, kt):
    ktp = jnp.concatenate([kt, jnp.zeros_like(kt)], axis=0)
    a_s[...] = _dot(qt, ktp, NT)
    col = lax.broadcasted_iota(jnp.int32, (SUB, HGRN_HEAD_DIM), 1)
    for i in range(N_SUB):
        qi = q_s[pl.ds(SUB * i, SUB), :]
        ai = jnp.zeros((SUB, HGRN_HEAD_DIM), F32)
        for s in range(SUB):
            e, _ = _hgrn_diag_e(b_s, i, s)
            a_col = jnp.sum(qi * k_s[pl.ds(SUB * i + s, 1), :] * e, axis=-1, keepdims=True)
            ai = ai + jnp.where(col == SUB * i + s, a_col, 0.0)
        a_s[pl.ds(SUB * i, SUB), :] += ai


def _hgrn_fwd(proj, lb, go, nb, seq):
    nc = seq // CHUNK
    hd = HGRN_HEAD_DIM

    def body(hq_ref, hf_ref, hi_ref, hg_ref, lb_ref, go_ref, y_ref, o_ref, st_ref, a_ref,
             st_all, q_all, k_all, b_all, a_all):
        st_all[...] = jnp.zeros_like(st_all)
        lower = _tri(True)

        def head_chunk(hh, c, rows):
            ln = slice(hd * hh, hd * (hh + 1))
            st, q_s, k_s, b_s, a_s = st_all.at[hh], q_all.at[hh], k_all.at[hh], b_all.at[hh], a_all.at[hh]
            q, _, _, f = _hgrn_gates(hq_ref[rows, ln], hf_ref[rows, ln], lb_ref[:, ln])
            v = hi_ref[rows, ln]
            b = _dot_exact01(lower, jnp.log(f))
            q_s[...] = q
            k_s[...] = 1.0 - f
            b_s[...] = b
            st_ref[hh, c] = st[...]
            qt, kt, _, _ = _hgrn_offdiag(q_s, k_s, b_s)
            _hgrn_intra(q_s, k_s, b_s, a_s, qt, kt)
            a16 = a_s[...].astype(BF16)
            a_ref[hh, c] = a16
            vp = jnp.concatenate([v, jnp.zeros_like(v)], axis=0)
            o = _dot(a16, vp) + _dot(q * jnp.exp(b), st[...], NT)
            bl = b_s[pl.ds(CHUNK - 1, 1), :]
            st[...] = st[...] * jnp.exp(bl) + _tn(v, (1.0 - f) * jnp.exp(bl - b))
            o_ref[rows, ln] = o
            n = o * lax.rsqrt(jnp.mean(o * o, axis=-1, keepdims=True) + RMS_EPS) * go_ref[...]
            hg = hg_ref[rows, ln]
            y_ref[rows, ln] = n * hg * _sigmoid(hg)

        def chunk(c, carry):
            rows = pl.ds(pl.multiple_of(c * CHUNK, CHUNK), CHUNK)
            for hh in range(HGRN_PER_STEP):
                head_chunk(hh, c, rows)
            return carry

        lax.fori_loop(0, nc, chunk, 0)

    hp, wide = HGRN_PER_STEP, HGRN_PER_STEP * hd

    def col(off):
        return pl.BlockSpec((seq, wide), lambda b, h: (b, off // hp + h))

    out = pl.BlockSpec((seq, wide), lambda b, h: (b, h))
    t = nb * seq
    return pl.pallas_call(
        body, name="hgrn_fwd", grid=(nb, HGRN_HEADS // hp),
        in_specs=[col(12), col(16), col(20), col(24), pl.BlockSpec((1, wide), lambda b, h: (0, h)),
                  pl.BlockSpec((1, hd), lambda b, h: (0, 0))],
        out_specs=[out, out, pl.BlockSpec((hp, nc, hd, hd), lambda b, h: (b * (HGRN_HEADS // hp) + h, 0, 0, 0)),
                   pl.BlockSpec((hp, nc, CHUNK, hd), lambda b, h: (b * (HGRN_HEADS // hp) + h, 0, 0, 0))],
        out_shape=[jax.ShapeDtypeStruct((t, HGRN_HEADS * hd), F32)] * 2
        + [jax.ShapeDtypeStruct((nb * HGRN_HEADS, nc, hd, hd), F32),
           jax.ShapeDtypeStruct((nb * HGRN_HEADS, nc, CHUNK, hd), BF16)],
        scratch_shapes=[pltpu.VMEM((hp, hd, hd), F32)] + [pltpu.VMEM((hp, CHUNK, hd), F32)] * 4,
        compiler_params=_params("parallel", "parallel"),
    )(proj, proj, proj, proj, lb, go)


def _hgrn_bwd(proj, lb, go, o_pre, states, scores, dout, nb, seq):
    nc = seq // CHUNK
    hd = HGRN_HEAD_DIM

    def body(hq_ref, hf_ref, hi_ref, hg_ref, lb_ref, go_ref, o_ref, st_ref, a_ref, dy_ref,
             dhq_ref, dhf_ref, dhi_ref, dhg_ref, dlb_ref, dgo_ref,
             dst_all, q_all, k_all, b_all, da_all, dqi_all, dki_all, dlb_all, dgo_all):
        dst_all[...] = jnp.zeros_like(dst_all)
        dlb_all[...] = jnp.zeros_like(dlb_all)
        dgo_all[...] = jnp.zeros_like(dgo_all)
        lower, upper = _tri(True), _tri(False)
        gov = go_ref[...]
        row = lax.broadcasted_iota(jnp.int32, (CHUNK, hd), 0)

        def head_chunk(hh, c, rows):
            ln = slice(hd * hh, hd * (hh + 1))
            dst, q_s, k_s, b_s = dst_all.at[hh], q_all.at[hh], k_all.at[hh], b_all.at[hh]
            da_s, dqi_s, dki_s = da_all.at[hh], dqi_all.at[hh], dki_all.at[hh]
            dlb_acc, dgo_acc = dlb_all.at[hh], dgo_all.at[hh]
            lbv = lb_ref[:, ln]
            hq, hf, v, hg = hq_ref[rows, ln], hf_ref[rows, ln], hi_ref[rows, ln], hg_ref[rows, ln]
            q, sq, sf, f = _hgrn_gates(hq, hf, lbv)
            kk = 1.0 - f
            b = _dot_exact01(lower, jnp.log(f))
            q_s[...] = q
            k_s[...] = kk
            b_s[...] = b
            bl = b_s[pl.ds(CHUNK - 1, 1), :]
            ebl = jnp.exp(bl)
            ekd = jnp.exp(bl - b)
            kd = kk * ekd
            eb = jnp.exp(b)
            qb = q * eb
            st0 = st_ref[hh, c]
            dst1 = dst[...]

            o = o_ref[rows, ln]
            dy = dy_ref[rows, ln]
            sg = _sigmoid(hg)
            rstd = lax.rsqrt(jnp.mean(o * o, axis=-1, keepdims=True) + RMS_EPS)
            ohat = o * rstd
            dn = dy * hg * sg
            dhg_ref[rows, ln] = (dy * ohat * gov * (sg * (1.0 + hg * (1.0 - sg)))).astype(BF16)
            dgo_acc[...] += jnp.sum(dn * ohat, axis=0, keepdims=True)
            gdn = dn * gov
            do = rstd * (gdn - ohat * jnp.mean(gdn * ohat, axis=-1, keepdims=True))

            qt, kt, eq, ek = _hgrn_offdiag(q_s, k_s, b_s)
            da = _dot(do, v, NT)
            dat = _dot(v, do, NT)
            da_s[...] = da
            dqo = _dot(da, kt) * eq
            dko = _dot(dat, qt) * ek
            dqi_s[...] = dqo[:, 0:hd] + dqo[:, hd:2 * hd] + dqo[:, 2 * hd:3 * hd]
            dki_s[...] = dko[:, 0:hd] + dko[:, hd:2 * hd] + dko[:, 2 * hd:3 * hd]
            col = lax.broadcasted_iota(jnp.int32, (SUB, CHUNK), 1)
            for i in range(N_SUB):
                qi = q_s[pl.ds(SUB * i, SUB), :]
                dai = da_s[pl.ds(SUB * i, SUB), :]
                dqd = jnp.zeros((SUB, hd), F32)
                dkd_ = jnp.zeros((SUB, hd), F32)
                for s in range(SUB):
                    e, t_io = _hgrn_diag_e(b_s, i, s)
                    dacol = jnp.sum(jnp.where(col == SUB * i + s, dai, 0.0), axis=-1, keepdims=True)
                    w = dacol * e
                    dqd = dqd + w * k_s[pl.ds(SUB * i + s, 1), :]
                    dkd_ = dkd_ + jnp.where(t_io == s, jnp.sum(w * qi, axis=0, keepdims=True), 0.0)
                dqi_s[pl.ds(SUB * i, SUB), :] += dqd
                dki_s[pl.ds(SUB * i, SUB), :] += dkd_
            dqi, dki = dqi_s[...], dki_s[...]

            dv = _tn(a_ref[hh, c].astype(F32), do)[0:CHUNK, :] + _dot(kd, dst1, NT)
            dqb = _dot(do, st0)
            dkd = _dot(v, dst1)
            t2 = dkd * kd
            dq = dqb * eb + dqi
            dk = dkd * ekd + dki
            dbl = jnp.sum(t2, axis=0, keepdims=True) + ebl * jnp.sum(st0 * dst1, axis=0, keepdims=True)
            db = dqb * qb - t2 + q * dqi - kk * dki + jnp.where(row == CHUNK - 1, dbl, 0.0)
            dg = _dot_exact01(upper, db)
            dst[...] = dst1 * ebl + _tn(do, qb)

            df = dg / f - dk
            dhf_ref[rows, ln] = (df * (1.0 - lbv) * sf * (1.0 - sf)).astype(BF16)
            dlb_acc[...] += jnp.sum(df * (1.0 - sf), axis=0, keepdims=True)
            dhq_ref[rows, ln] = (dq * (sq * (1.0 + hq * (1.0 - sq)))).astype(BF16)
            dhi_ref[rows, ln] = dv.astype(BF16)

        def chunk(it, carry):
            c = nc - 1 - it
            rows = pl.ds(pl.multiple_of(c * CHUNK, CHUNK), CHUNK)
            for hh in range(HGRN_PER_STEP):
                head_chunk(hh, c, rows)
            return carry

        lax.fori_loop(0, nc, chunk, 0)
        dlb_ref[...] = dlb_all[...]
        dgo_ref[...] = dgo_all[...]

    hp, wide = HGRN_PER_STEP, HGRN_PER_STEP * hd

    def col(off):
        return pl.BlockSpec((seq, wide), lambda b, h: (b, off // hp + h))

    out = pl.BlockSpec((seq, wide), lambda b, h: (b, h))
    part = pl.BlockSpec((hp, 1, hd), lambda b, h: (b * (HGRN_HEADS // hp) + h, 0, 0))
    t = nb * seq
    return pl.pallas_call(
        body, name="hgrn_bwd", grid=(nb, HGRN_HEADS // hp),
        in_specs=[col(12), col(16), col(20), col(24), pl.BlockSpec((1, wide), lambda b, h: (0, h)),
                  pl.BlockSpec((1, hd), lambda b, h: (0, 0)), out,
                  pl.BlockSpec((hp, nc, hd, hd), lambda b, h: (b * (HGRN_HEADS // hp) + h, 0, 0, 0)),
                  pl.BlockSpec((hp, nc, CHUNK, hd), lambda b, h: (b * (HGRN_HEADS // hp) + h, 0, 0, 0)), col(4)],
        out_specs=[out, out, out, out, part, part],
        out_shape=[jax.ShapeDtypeStruct((t, HGRN_HEADS * hd), BF16)] * 4
        + [jax.ShapeDtypeStruct((nb * HGRN_HEADS, 1, hd), F32)] * 2,
        scratch_shapes=[pltpu.VMEM((hp, hd, hd), F32)] + [pltpu.VMEM((hp, CHUNK, hd), F32)] * 3
        + [pltpu.VMEM((hp, CHUNK, CHUNK), F32)] + [pltpu.VMEM((hp, CHUNK, hd), F32)] * 2
        + [pltpu.VMEM((hp, 1, hd), F32)] * 2,
        compiler_params=_params("parallel", "parallel"),
    )(proj, proj, proj, proj, lb, go, o_pre, states, scores, dout)


def _lb_fwd(lower_bounds):
    def body(x_ref, o_ref):
        xv = x_ref[...]
        e = jnp.exp(xv - jnp.max(xv, axis=0, keepdims=True))
        o_ref[...] = e[0:1, :] / jnp.sum(e, axis=0, keepdims=True)

    return pl.pallas_call(body, name="lb_fwd",
                          out_shape=jax.ShapeDtypeStruct((1, lower_bounds.shape[1]), F32))(lower_bounds)


def _lb_bwd(lower_bounds, dlb_parts):
    ng = dlb_parts.shape[0]

    def body(x_ref, d_ref, o_ref):
        xv = x_ref[...]
        e = jnp.exp(xv - jnp.max(xv, axis=0, keepdims=True))
        p = e / jnp.sum(e, axis=0, keepdims=True)
        dlb = d_ref[0]
        for gi in range(1, ng):
            dlb = dlb + d_ref[gi]
        first = lax.broadcasted_iota(jnp.int32, xv.shape, 0) == 0
        o_ref[...] = p * (jnp.where(first, dlb, 0.0) - p[0:1, :] * dlb)

    return pl.pallas_call(body, name="lb_bwd",
                          out_shape=jax.ShapeDtypeStruct(lower_bounds.shape, F32))(lower_bounds, dlb_parts)


def _ffn_bwd(x, g, h, gate, up, dy, dy16, w, put, tag):
    wg, wu, wd = w[tag + "_w_gate"], w[tag + "_w_up"], w[tag + "_w_down"]
    dgate, dup, act = _ffn_bwd_mid(dy16, wd, gate, up, tag + "_bwd_mid")
    put(tag + "_w_down", _mm(act, dy16, ta=True, tm=1408, tn=512, scale=0.5, name=tag + "_dwd"))
    put(tag + "_w_gate", _mm(h, dgate, ta=True, tm=512, tn=1408, name=tag + "_dwg"))
    put(tag + "_w_up", _mm(h, dup, ta=True, tm=512, tn=1408, name=tag + "_dwu"))
    dh = _mm(dgate, wg, tb=True, tm=512, tn=1024, name=tag + "_dh_gate")
    return _mm(dup, wu, tb=True, tm=512, tn=1024, add=dh, norm_bwd=(x, g, dy), name=tag + "_dh_up")


def _local_step(x, tgt, sp, w, put, nb, seq):
    d = x.shape[1]
    h1 = _rms_fwd(x, sp["ffn1_norm_g"], "ffn1_norm")
    gate1, up1, x1, h2 = _ffn_fwd(h1, x, w["ffn1_w_gate"], w["ffn1_w_up"], w["ffn1_w_down"], "ffn1_fwd",
                                  next_g=sp["mix_norm_g"])
    proj = _mm(h2, w["w_in"], tm=512, tn=512, name="in_proj")
    rb_pad = jnp.pad(sp["attn_rel_bias"], ((0, 0), (0, N_REL_PAD - N_REL)))
    bias = jnp.transpose(_bias_expand(rb_pad), (1, 0, 2)).reshape(ATTN_HEADS * CHUNK, BAND)
    gq2 = jnp.concatenate([sp["attn_q_norm_g"]] * 2, axis=1)
    gk2 = jnp.concatenate([sp["attn_k_norm_g"]] * 2, axis=1)
    lb = _lb_fwd(sp["hgrn_lower_bounds"])
    attn = _attn_fwd(proj, bias, gq2, gk2, nb, seq)
    hy, ho, hstate, hscore = _hgrn_fwd(proj, lb, sp["hgrn_out_norm_g"], nb, seq)
    mix = jnp.concatenate([attn, hy], axis=1)
    x2, h3 = _mm(mix, w["w_out"], tm=512, tn=1024, add=x1, norm_g=sp["ffn2_norm_g"], name="out_proj")
    gate2, up2, dx3, dx3_16, sq = _ffn_fwd(h3, x2, w["ffn2_w_gate"], w["ffn2_w_up"], w["ffn2_w_down"], "ffn2_fwd",
                                           tgt=tgt)
    loss = 0.5 * jnp.sum(sq) / d

    dx2, dx2_16, dg3 = _ffn_bwd(x2, sp["ffn2_norm_g"], h3, gate2, up2, dx3, dx3_16, w, put, "ffn2")
    dmix = _mm(dx2_16, w["w_out"], tb=True, tm=512, tn=1024, name="out_proj_dx")
    put("w_out", _mm(mix, dx2_16, ta=True, tm=512, tn=1024, name="out_proj_dw"))
    dq, dk, dv, dbias, dgq, dgk = _attn_bwd(proj, attn, dmix, bias, gq2, gk2, nb, seq)
    dbias = jnp.transpose(dbias.reshape(nb, ATTN_HEADS, CHUNK, BAND), (0, 2, 1, 3))
    dgq = jnp.sum(dgq, axis=(0, 1)).reshape(2, ATTN_HEAD_DIM).sum(axis=0, keepdims=True)
    dgk = jnp.sum(dgk, axis=(0, 1)).reshape(2, ATTN_HEAD_DIM).sum(axis=0, keepdims=True)
    dhq, dhf, dhi, dhg, dlb, dgo = _hgrn_bwd(proj, lb, sp["hgrn_out_norm_g"], ho, hstate, hscore, dmix, nb, seq)
    dproj = jnp.concatenate([dq, dk, dv, dhq, dhf, dhi, dhg], axis=1)
    put("w_in", _mm(h2, dproj, ta=True, tm=512, tn=512, name="in_proj_dw"))
    dx1, dx1_16, dgm = _mm(dproj, w["w_in"], tb=True, tm=512, tn=1024, norm_bwd=(x1, sp["mix_norm_g"], dx2),
                           name="in_proj_dx")
    dx0, _, dg1 = _ffn_bwd(x, sp["ffn1_norm_g"], h1, gate1, up1, dx1, dx1_16, w, put, "ffn1")

    small = {
        "ffn1_norm_g": dg1, "mix_norm_g": dgm, "ffn2_norm_g": dg3,
        "attn_q_norm_g": dgq, "attn_k_norm_g": dgk,
        "attn_rel_bias": _bias_fold(dbias)[:, :N_REL],
        "hgrn_lower_bounds": _lb_bwd(sp["hgrn_lower_bounds"], dlb.reshape(nb, 1, HGRN_HEADS * HGRN_HEAD_DIM)),
        "hgrn_out_norm_g": jnp.sum(dgo, axis=(0, 1))[None, :],
    }
    return loss, dx0, small


MESH = pl.DeviceIdType.MESH
ANY = pl.BlockSpec(memory_space=pl.ANY)


def _coords():
    return lax.axis_index("x"), lax.axis_index("y"), lax.axis_index("c")


def _other_chips(x, y):
    return [(1 - x, y), (x, 1 - y), (1 - x, 1 - y)]


def _gather_side(shards):
    n = len(shards)

    def copies(ins, outs, sems):
        send_sems, recv_sems, local_sems = sems
        x, y, c = _coords()
        me, sibling = (x, y, c), (x, y, 1 - c)
        chips = _other_chips(x, y)

        def copy(i, k, block, to, src=None):
            bx, by, bc = block
            dst = outs[i].at[4 * bx + 2 * by + bc]
            return pltpu.make_async_remote_copy(
                src_ref=dst if src is None else src, dst_ref=dst, send_sem=send_sems.at[i, k],
                recv_sem=recv_sems.at[i, k], device_id=to, device_id_type=MESH)

        mine = [pltpu.make_async_copy(ins[i], outs[i].at[4 * x + 2 * y + c], local_sems.at[i]) for i in range(n)]
        own = []
        for i in range(n):
            own.append(copy(i, 0, me, sibling, src=ins[i]))
            own += [copy(i, 1 + j, me, (*chip, c), src=ins[i]) for j, chip in enumerate(chips)]
        return copy, mine, own, me, sibling, chips, c

    def start(ins, outs, sems):
        _, mine, own, *_ = copies(ins, outs, sems)
        for cp in mine + own:
            cp.start()

    def finish(ins, outs, sems):
        copy, mine, own, me, sibling, chips, c = copies(ins, outs, sems)
        passed = []
        for i in range(n):
            for j, chip in enumerate(chips):
                copy(i, 1 + j, (*chip, c), me).wait_recv()
                passed.append(copy(i, 4 + j, (*chip, c), sibling))
                passed[-1].start()
        for i in range(n):
            copy(i, 0, sibling, me).wait_recv()
            for j, chip in enumerate(chips):
                copy(i, 4 + j, (*chip, 1 - c), me).wait_recv()
        for cp in own + passed:
            cp.wait_send()
        for cp in mine:
            cp.wait()

    return _Side(list(shards), [jax.ShapeDtypeStruct((N_DEV,) + s.shape, s.dtype) for s in shards],
                 [pltpu.SemaphoreType.DMA((n, 7)), pltpu.SemaphoreType.DMA((n, 7)), pltpu.SemaphoreType.DMA((n,))],
                 start, finish)


def _pair_side(grads):
    n = len(grads)

    def copies(ins, outs, sems):
        send_sems, recv_sems = sems
        x, y, c = _coords()
        return [pltpu.make_async_remote_copy(
            src_ref=ins[i].at[2 * k + 1 - c], dst_ref=outs[i].at[k], send_sem=send_sems.at[i, k],
            recv_sem=recv_sems.at[i, k], device_id=(x, y, 1 - c), device_id_type=MESH)
            for i in range(n) for k in range(4)]

    def start(ins, outs, sems):
        for cp in copies(ins, outs, sems):
            cp.start()

    def finish(ins, outs, sems):
        for cp in copies(ins, outs, sems):
            cp.wait()

    return _Side(list(grads), [jax.ShapeDtypeStruct((4,) + g.shape[1:], g.dtype) for g in grads],
                 [pltpu.SemaphoreType.DMA((n, 4)), pltpu.SemaphoreType.DMA((n, 4))], start, finish)


def _pair_add(grad, recv, core, name):
    _, r, cdim = grad.shape

    def body(c_ref, g_ref, r_ref, o_ref):
        o_ref[...] = (g_ref[...] + r_ref[...]).astype(BF16)

    blk = (1, r, cdim)
    return pl.pallas_call(
        body, name=name,
        grid_spec=pltpu.PrefetchScalarGridSpec(
            num_scalar_prefetch=1, grid=(4,),
            in_specs=[pl.BlockSpec(blk, lambda k, c_ref: (2 * k + c_ref[0], 0, 0)),
                      pl.BlockSpec(blk, lambda k, c_ref: (k, 0, 0))],
            out_specs=pl.BlockSpec(blk, lambda k, c_ref: (k, 0, 0))),
        out_shape=jax.ShapeDtypeStruct((4, r, cdim), BF16),
        compiler_params=_params("arbitrary"),
    )(core, grad, recv)


def _chip_side(parts):
    n = len(parts)

    def copies(ins, outs, sems):
        send_sems, recv_sems, local_sems = sems
        x, y, c = _coords()
        chips = _other_chips(x, y)
        mine = [pltpu.make_async_copy(ins[i].at[2 * x + y], outs[i].at[2 * x + y], local_sems.at[i])
                for i in range(n)]
        sent = [pltpu.make_async_remote_copy(
            src_ref=ins[i].at[2 * px + py], dst_ref=outs[i].at[2 * x + y], send_sem=send_sems.at[i, j],
            recv_sem=recv_sems.at[i, j], device_id=(px, py, c), device_id_type=MESH)
            for i in range(n) for j, (px, py) in enumerate(chips)]
        return mine, sent, chips, c

    def start(ins, outs, sems):
        mine, sent, _, _ = copies(ins, outs, sems)
        for cp in mine + sent:
            cp.start()

    def finish(ins, outs, sems):
        mine, sent, chips, c = copies(ins, outs, sems)
        send_sems, recv_sems, _ = sems
        for i in range(n):
            for j, (px, py) in enumerate(chips):
                landed = outs[i].at[2 * px + py]
                pltpu.make_async_remote_copy(
                    src_ref=landed, dst_ref=landed, send_sem=send_sems.at[i, j], recv_sem=recv_sems.at[i, j],
                    device_id=(px, py, c), device_id_type=MESH).wait_recv()
        for cp in sent:
            cp.wait_send()
        for cp in mine:
            cp.wait()

    return _Side(list(parts), [jax.ShapeDtypeStruct(p.shape, p.dtype) for p in parts],
                 [pltpu.SemaphoreType.DMA((n, 3)), pltpu.SemaphoreType.DMA((n, 3)), pltpu.SemaphoreType.DMA((n,))],
                 start, finish)


def _all_reduce_small(v):
    r = v.shape[0]

    def body(v_ref, o_ref, buf, send_sems, recv_sems):
        x, y, c = _coords()
        me = 4 * x + 2 * y + c
        buf[me] = v_ref[...]
        cps = []
        for k in range(1, N_DEV):
            px = 1 - x if k & 4 else x
            py = 1 - y if k & 2 else y
            pc = 1 - c if k & 1 else c
            cps.append((pltpu.make_async_remote_copy(
                src_ref=v_ref, dst_ref=buf.at[me], send_sem=send_sems.at[k - 1], recv_sem=recv_sems.at[k - 1],
                device_id=(px, py, pc), device_id_type=MESH), 4 * px + 2 * py + pc))
        for cp, _ in cps:
            cp.start()
        for k, (cp, peer) in enumerate(cps):
            pltpu.make_async_remote_copy(
                src_ref=v_ref, dst_ref=buf.at[peer], send_sem=send_sems.at[k], recv_sem=recv_sems.at[k],
                device_id=(x, y, c), device_id_type=MESH).wait_recv()
        for cp, _ in cps:
            cp.wait_send()
        acc = buf[0]
        for j in range(1, N_DEV):
            acc = acc + buf[j]
        o_ref[...] = acc

    return pl.pallas_call(
        body, name="small_all_reduce", out_shape=jax.ShapeDtypeStruct(v.shape, F32),
        in_specs=[pl.BlockSpec(memory_space=pltpu.VMEM)], out_specs=pl.BlockSpec(memory_space=pltpu.VMEM),
        scratch_shapes=[pltpu.VMEM((N_DEV, r, 128), F32), pltpu.SemaphoreType.DMA((N_DEV - 1,)),
                        pltpu.SemaphoreType.DMA((N_DEV - 1,))],
    )(v)


def _adamw(w, m, v, g, name):
    r, cdim = w.shape
    parts = g.ndim == 3
    tr = r // 4 if r % 32 == 0 else r

    def body(w_ref, m_ref, v_ref, g_ref, go_ref, d_ref, mo_ref, vo_ref):
        if parts:
            gv = g_ref[0].astype(F32)
            for k in range(1, 4):
                gv = gv + g_ref[k].astype(F32)
        else:
            gv = g_ref[...]
        m2 = ADAM_B1 * m_ref[...] + (1.0 - ADAM_B1) * gv
        v2 = ADAM_B2 * v_ref[...] + (1.0 - ADAM_B2) * (gv * gv)
        m_hat = m2 / (1.0 - ADAM_B1 ** ADAM_STEP)
        v_hat = v2 / (1.0 - ADAM_B2 ** ADAM_STEP)
        go_ref[...] = gv
        d_ref[...] = -ADAM_LR * (m_hat / (jnp.sqrt(v_hat) + ADAM_EPS) + ADAM_WD * w_ref[...])
        mo_ref[...] = m2
        vo_ref[...] = v2

    row = pl.BlockSpec((tr, cdim), lambda i: (i, 0))
    g_spec = pl.BlockSpec((4, tr, cdim), lambda i: (0, i, 0)) if parts else row
    return pl.pallas_call(
        body, name=name, grid=(r // tr,), in_specs=[row, row, row, g_spec], out_specs=[row] * 4,
        out_shape=[jax.ShapeDtypeStruct((r, cdim), F32)] * 4,
        compiler_params=_params("parallel"),
    )(w, m, v, g)


WEIGHTS = ["ffn1_norm_g", "ffn1_w_gate", "ffn1_w_up", "ffn1_w_down", "mix_norm_g", "w_in", "attn_q_norm_g",
           "attn_k_norm_g", "attn_rel_bias", "hgrn_lower_bounds", "hgrn_out_norm_g", "w_out", "ffn2_norm_g",
           "ffn2_w_gate", "ffn2_w_up", "ffn2_w_down"]
COL_SHARDED = ("ffn1_w_gate", "ffn1_w_up", "w_in", "ffn2_w_gate", "ffn2_w_up")
ROW_SHARDED = ("ffn1_w_down", "w_out", "ffn2_w_down")
BIG = [n for n in WEIGHTS if n in COL_SHARDED or n in ROW_SHARDED]
SMALL = [n for n in WEIGHTS if n not in BIG]
PACK_ROWS = 8
FFN1 = ["ffn1_w_down", "ffn1_w_gate", "ffn1_w_up"]
FFN2 = ["ffn2_w_down", "ffn2_w_gate", "ffn2_w_up"]
MIXER = ["w_out", "w_in"]

PLAN = {
    "ffn1_fwd": ("gather", MIXER),
    "attn_fwd": ("gather", FFN2),
    "ffn2_dh_gate": ("pair", FFN2),
    "attn_bwd": ("chip", FFN2),
    "in_proj_dx": ("pair", MIXER),
    "ffn1_bwd_mid": ("chip", MIXER),
    "ffn1_dh_gate": ("pair", FFN1),
    "ffn1_dh_up": ("chip", FFN1),
}


class _Schedule:
    def __init__(self, shards):
        self.shards = shards
        self.weights = {}
        self.sliced = {}
        self.partials = {}
        self.reduced = {}

    def put(self, name, grad):
        r, c = self.shards[name].shape
        if name in COL_SHARDED:
            self.sliced[name] = jnp.transpose(grad.reshape(r, N_DEV, c), (1, 0, 2))
        else:
            self.sliced[name] = grad.reshape(N_DEV, r, c)

    def side_for(self, call):
        if call not in PLAN:
            return None
        kind, names = PLAN[call]
        if kind == "gather":
            return _gather_side([self.shards[n] for n in names])
        if kind == "pair":
            return _pair_side([self.sliced[n] for n in names])
        return _chip_side([self.partials[n] for n in names])

    def done(self, call, outs):
        self.file(*PLAN[call], outs)

    def file(self, kind, names, outs):
        for n, o in zip(names, outs):
            if kind == "gather":
                if n in COL_SHARDED:
                    self.weights[n] = jnp.transpose(o, (1, 0, 2)).reshape(o.shape[1], N_DEV * o.shape[2])
                else:
                    self.weights[n] = o.reshape(N_DEV * o.shape[1], o.shape[2])
            elif kind == "pair":
                core = lax.axis_index("c").astype(jnp.int32).reshape(1)
                self.partials[n] = _pair_add(self.sliced[n], o, core, n + "_pair_add")
            else:
                self.reduced[n] = o

    def gather_now(self, names, call):
        self.file("gather", names, _run_side(_gather_side([self.shards[n] for n in names]), call))


def _pack_small(vals, loss=None):
    parts = []
    for n in SMALL:
        a = vals[n]
        if n == "attn_rel_bias":
            a = jnp.pad(a.reshape(ATTN_HEADS, N_REL), ((0, 0), (0, N_REL_PAD - N_REL)))
        flat = a.reshape(-1)
        size = -(-flat.shape[0] // (PACK_ROWS * 128)) * PACK_ROWS * 128
        parts.append(jnp.pad(flat, (0, size - flat.shape[0])).reshape(-1, 128))
    tail = jnp.zeros((PACK_ROWS, 128), F32)
    if loss is not None:
        tail = tail.at[0, 0].set(loss)
    return jnp.concatenate(parts + [tail], axis=0)


def _unpack_small(packed, shapes):
    out, row = {}, 0
    for n in SMALL:
        shape = shapes[n]
        if n == "attn_rel_bias":
            rows = ATTN_HEADS * N_REL_PAD // 128
            out[n] = packed[row:row + rows].reshape(ATTN_HEADS, N_REL_PAD)[:, :N_REL].reshape(shape)
        else:
            size = 1
            for s in shape:
                size *= s
            rows = -(-size // (PACK_ROWS * 128)) * PACK_ROWS
            out[n] = packed[row:row + rows].reshape(-1)[:size].reshape(shape)
        row += rows
    return out, packed[row, 0]


def kernel(x, ffn1_norm_g, ffn1_w_gate, ffn1_w_up, ffn1_w_down, mix_norm_g, w_in, attn_q_norm_g, attn_k_norm_g, attn_rel_bias, hgrn_lower_bounds, hgrn_out_norm_g, w_out, ffn2_norm_g, ffn2_w_gate, ffn2_w_up, ffn2_w_down, loss_target, m_ffn1_norm_g, m_ffn1_w_gate, m_ffn1_w_up, m_ffn1_w_down, m_mix_norm_g, m_w_in, m_attn_q_norm_g, m_attn_k_norm_g, m_attn_rel_bias, m_hgrn_lower_bounds, m_hgrn_out_norm_g, m_w_out, m_ffn2_norm_g, m_ffn2_w_gate, m_ffn2_w_up, m_ffn2_w_down, v_ffn1_norm_g, v_ffn1_w_gate, v_ffn1_w_up, v_ffn1_w_down, v_mix_norm_g, v_w_in, v_attn_q_norm_g, v_attn_k_norm_g, v_attn_rel_bias, v_hgrn_lower_bounds, v_hgrn_out_norm_g, v_w_out, v_ffn2_norm_g, v_ffn2_w_gate, v_ffn2_w_up, v_ffn2_w_down):
    wts = dict(zip(WEIGHTS, (ffn1_norm_g, ffn1_w_gate, ffn1_w_up, ffn1_w_down, mix_norm_g, w_in, attn_q_norm_g,
                             attn_k_norm_g, attn_rel_bias, hgrn_lower_bounds, hgrn_out_norm_g, w_out, ffn2_norm_g,
                             ffn2_w_gate, ffn2_w_up, ffn2_w_down)))
    mom = dict(zip(WEIGHTS, (m_ffn1_norm_g, m_ffn1_w_gate, m_ffn1_w_up, m_ffn1_w_down, m_mix_norm_g, m_w_in,
                             m_attn_q_norm_g, m_attn_k_norm_g, m_attn_rel_bias, m_hgrn_lower_bounds,
                             m_hgrn_out_norm_g, m_w_out, m_ffn2_norm_g, m_ffn2_w_gate, m_ffn2_w_up, m_ffn2_w_down)))
    var = dict(zip(WEIGHTS, (v_ffn1_norm_g, v_ffn1_w_gate, v_ffn1_w_up, v_ffn1_w_down, v_mix_norm_g, v_w_in,
                             v_attn_q_norm_g, v_attn_k_norm_g, v_attn_rel_bias, v_hgrn_lower_bounds,
                             v_hgrn_out_norm_g, v_w_out, v_ffn2_norm_g, v_ffn2_w_gate, v_ffn2_w_up, v_ffn2_w_down)))
    nb, seq, d = x.shape
    shapes = {n: wts[n].shape for n in WEIGHTS}

    sched = _Schedule({n: wts[n][0].astype(BF16) for n in BIG})
    sched.gather_now(FFN1, "ffn1_weights_all_gather")
    sp = {n: wts[n] for n in SMALL}
    sp["attn_rel_bias"] = wts["attn_rel_bias"][0]
    _ACTIVE[0] = sched
    try:
        loss, dx, dsmall = _local_step(x.reshape(nb * seq, d), loss_target.reshape(nb * seq, d), sp,
                                       sched.weights, sched.put, nb, seq)
    finally:
        _ACTIVE[0] = None
    reduced = sched.reduced

    small_sum = _all_reduce_small(_pack_small(dsmall, loss))
    gsmall, loss_total = _unpack_small(small_sum, shapes)

    grads, deltas, new_m, new_v = {}, {}, {}, {}
    for n in BIG:
        out = _adamw(wts[n][0], mom[n][0], var[n][0], reduced[n], n + "_adamw")
        grads[n], deltas[n], new_m[n], new_v[n] = (o.reshape(shapes[n]) for o in out)
    packed = _adamw(_pack_small(wts), _pack_small(mom), _pack_small(var), small_sum, "small_adamw")
    for dst, p in zip((deltas, new_m, new_v), packed[1:]):
        dst.update(_unpack_small(p, shapes)[0])
    grads.update(gsmall)

    return (loss_total, dx.reshape(nb, seq, d), *[grads[n] for n in WEIGHTS], *[deltas[n] for n in WEIGHTS],
            *[new_m[n] for n in WEIGHTS], *[new_v[n] for n in WEIGHTS])
```

```python
import functools

import jax
import jax.numpy as jnp
from jax import lax
from jax.experimental import pallas as pl
from jax.experimental.pallas import tpu as pltpu

F32 = jnp.float32
BF16 = jnp.bfloat16

RMS_EPS = 1e-6
CHUNK = 64
LEFT_CHUNKS = 8
BAND = (LEFT_CHUNKS + 2) * CHUNK
KPAD = BAND - CHUNK
REL_CLIP = 128
N_REL = 2 * REL_CLIP + 1
N_REL_PAD = 384
ATTN_HEADS = 8
ATTN_HEAD_DIM = 64
ATTN_WIDTH = ATTN_HEADS * ATTN_HEAD_DIM
ATTN_UNROLL = 4
HGRN_HEADS = 4
HGRN_HEAD_DIM = 128
HGRN_PER_STEP = 2
SUB = 16
N_SUB = CHUNK // SUB
N_DEV = 8

ADAM_LR = 0.001
ADAM_B1 = 0.9
ADAM_B2 = 0.999
ADAM_EPS = 1e-08
ADAM_WD = 0.01
ADAM_STEP = 10

VMEM_LIMIT = 56 * 1024 * 1024

NT = (((1,), (1,)), ((), ()))
NN = (((1,), (0,)), ((), ()))


def _params(*sem):
    return pltpu.CompilerParams(dimension_semantics=sem, vmem_limit_bytes=VMEM_LIMIT)


def _sigmoid(v):
    return 0.5 * jnp.tanh(0.5 * v) + 0.5


def _dot(a, b, dims=NN):
    return lax.dot_general(a.astype(BF16), b.astype(BF16), dims, preferred_element_type=F32)


def _dot_exact01(m01, v):
    m = m01.astype(BF16)
    hi = v.astype(BF16)
    r1 = v - hi.astype(F32)
    mid = r1.astype(BF16)
    lo = (r1 - mid.astype(F32)).astype(BF16)
    out = lax.dot_general(m, hi, NN, preferred_element_type=F32)
    out = out + lax.dot_general(m, mid, NN, preferred_element_type=F32)
    return out + lax.dot_general(m, lo, NN, preferred_element_type=F32)


def _dot_exact01_r(v, m01):
    m = m01.astype(BF16)
    hi = v.astype(BF16)
    r1 = v - hi.astype(F32)
    mid = r1.astype(BF16)
    lo = (r1 - mid.astype(F32)).astype(BF16)
    out = lax.dot_general(hi, m, NN, preferred_element_type=F32)
    out = out + lax.dot_general(mid, m, NN, preferred_element_type=F32)
    return out + lax.dot_general(lo, m, NN, preferred_element_type=F32)


def _tn(a, b):
    ap = jnp.concatenate([a, jnp.zeros_like(a)], axis=0)
    bp = jnp.concatenate([b, jnp.zeros_like(b)], axis=0)
    return _dot(ap.T, bp)


def _row_tile(t):
    for tm in (512, 256, 128, 64, 32, 16, 8):
        if t % tm == 0:
            return tm
    raise ValueError(t)


class _Side:
    def __init__(self, ins, out_shape, sems, start, finish):
        self.ins, self.out_shape, self.sems, self.start, self.finish = ins, out_shape, sems, start, finish


_ACTIVE = [None]


def _pallas(body, *, name, grid, in_specs, out_specs, out_shape, scratch_shapes=(), sem, args):
    sched = _ACTIVE[0]
    side = sched.side_for(name) if sched is not None else None
    if side is None:
        return pl.pallas_call(
            body, name=name, grid=grid, in_specs=list(in_specs), out_specs=list(out_specs),
            out_shape=list(out_shape), scratch_shapes=list(scratch_shapes), compiler_params=_params(*sem))(*args)
    cuts = [len(in_specs), len(side.ins), len(out_shape), len(side.out_shape), len(scratch_shapes)]

    def with_side(*refs):
        groups, at = [], 0
        for n in cuts:
            groups.append(refs[at:at + n])
            at += n
        ins, side_ins, outs, side_outs, scratch = groups
        side_sems = refs[at:]
        first = pl.program_id(0) == 0
        last = pl.program_id(0) == grid[0] - 1
        for a in range(1, len(grid)):
            first = jnp.logical_and(first, pl.program_id(a) == 0)
            last = jnp.logical_and(last, pl.program_id(a) == grid[a] - 1)

        @pl.when(first)
        def _():
            side.start(side_ins, side_outs, side_sems)

        body(*ins, *outs, *scratch)

        @pl.when(last)
        def _():
            side.finish(side_ins, side_outs, side_sems)

    hbm = pl.BlockSpec(memory_space=pl.ANY)
    res = pl.pallas_call(
        with_side, name=name, grid=grid, in_specs=list(in_specs) + [hbm] * len(side.ins),
        out_specs=list(out_specs) + [hbm] * len(side.out_shape), out_shape=list(out_shape) + list(side.out_shape),
        scratch_shapes=list(scratch_shapes) + list(side.sems),
        compiler_params=_params(*(["arbitrary"] * len(grid))))(*args, *side.ins)
    sched.done(name, res[len(out_shape):])
    return res[:len(out_shape)]


def _rms_fwd(x, g, name):
    t, d = x.shape
    tm = _row_tile(t)

    def body(x_ref, g_ref, h_ref):
        xv = x_ref[...]
        r = lax.rsqrt(jnp.mean(xv * xv, axis=-1, keepdims=True) + RMS_EPS)
        h_ref[...] = (xv * r * g_ref[...]).astype(BF16)

    return _pallas(
        body, name=name, grid=(t // tm,),
        in_specs=[pl.BlockSpec((tm, d), lambda i: (i, 0)), pl.BlockSpec((1, d), lambda i: (0, 0))],
        out_specs=[pl.BlockSpec((tm, d), lambda i: (i, 0))], out_shape=[jax.ShapeDtypeStruct((t, d), BF16)],
        sem=("parallel",), args=(x, g))[0]


def _accumulate(ref, part, step):
    @pl.when(step == 0)
    def _():
        ref[...] = part

    @pl.when(step > 0)
    def _():
        ref[...] += part


def _mm(a, b, *, ta=False, tb=False, tm, tn, out_dtype=F32, add=None, scale=1.0, norm_g=None, norm_bwd=None, name):
    m, k = (a.shape[1], a.shape[0]) if ta else a.shape
    n = b.shape[0] if tb else b.shape[1]
    tm, tn = min(tm, m), min(tn, n)
    assert m % tm == 0 and n % tn == 0, (m, n, tm, tn)
    assert (norm_g is None and norm_bwd is None) or tn == n
    dims = (((0 if ta else 1,), (1 if tb else 0,)), ((), ()))
    n_in = 2 + (add is not None) + (norm_g is not None) + (3 if norm_bwd is not None else 0)

    def body(*refs):
        ins, outs = list(refs[2:n_in]), refs[n_in:]
        r = lax.dot_general(refs[0][...].astype(BF16), refs[1][...].astype(BF16), dims, preferred_element_type=F32)
        if scale != 1.0:
            r = r * scale
        if add is not None:
            r = r + ins.pop(0)[...]
        if norm_bwd is not None:
            xv, gv, dres = (ref[...] for ref in ins)
            rs = lax.rsqrt(jnp.mean(xv * xv, axis=-1, keepdims=True) + RMS_EPS)
            xhat = xv * rs
            gd = r * gv
            dx = dres + rs * (gd - xhat * jnp.mean(gd * xhat, axis=-1, keepdims=True))
            outs[0][...] = dx
            outs[1][...] = dx.astype(BF16)
            _accumulate(outs[2], jnp.sum(r * xhat, axis=0, keepdims=True), pl.program_id(0))
            return
        outs[0][...] = r.astype(out_dtype)
        if norm_g is not None:
            rs = lax.rsqrt(jnp.mean(r * r, axis=-1, keepdims=True) + RMS_EPS)
            outs[1][...] = (r * rs * ins.pop(0)[...]).astype(BF16)

    a_spec = pl.BlockSpec((k, tm), lambda i, j: (0, i)) if ta else pl.BlockSpec((tm, k), lambda i, j: (i, 0))
    b_spec = pl.BlockSpec((tn, k), lambda i, j: (j, 0)) if tb else pl.BlockSpec((k, tn), lambda i, j: (0, j))
    o_spec = pl.BlockSpec((tm, tn), lambda i, j: (i, j))
    vec = pl.BlockSpec((1, tn), lambda i, j: (0, j))
    args, specs = [a, b], [a_spec, b_spec]
    if add is not None:
        args.append(add)
        specs.append(o_spec)
    out_specs, out_shape = [o_spec], [jax.ShapeDtypeStruct((m, n), out_dtype)]
    if norm_g is not None:
        args.append(norm_g)
        specs.append(vec)
        out_specs.append(o_spec)
        out_shape.append(jax.ShapeDtypeStruct((m, n), BF16))
    if norm_bwd is not None:
        args += list(norm_bwd)
        specs += [o_spec, vec, o_spec]
        out_specs = [o_spec, o_spec, vec]
        out_shape = [jax.ShapeDtypeStruct((m, n), F32), jax.ShapeDtypeStruct((m, n), BF16),
                     jax.ShapeDtypeStruct((1, n), F32)]
    res = _pallas(body, name=name, grid=(m // tm, n // tn), in_specs=specs, out_specs=out_specs, out_shape=out_shape,
                  sem=("arbitrary", "arbitrary") if norm_bwd is not None else ("parallel", "parallel"), args=args)
    return res[0] if len(res) == 1 else res


def _ffn_tile(f):
    for tf in (1408, 512, 256, 128):
        if f % tf == 0:
            return tf
    raise ValueError(f)


def _ffn_fwd(h, x, wg, wu, wd, name, next_g=None, tgt=None):
    t, d = x.shape
    f = wg.shape[1]
    tm, tf = _row_tile(t), _ffn_tile(f)
    nf = f // tf
    assert (next_g is None) != (tgt is None)

    def body(h_ref, x_ref, wg_ref, wu_ref, wd_ref, tail_ref, g_ref, u_ref, o0_ref, o1_ref, *rest):
        acc_ref = rest[-1]
        j = pl.program_id(1)
        hv = h_ref[...]
        gv = lax.dot_general(hv, wg_ref[...], NN, preferred_element_type=F32)
        uv = lax.dot_general(hv, wu_ref[...], NN, preferred_element_type=F32)
        av = gv * _sigmoid(gv) * uv
        g_ref[...] = gv.astype(BF16)
        u_ref[...] = uv.astype(BF16)
        _accumulate(acc_ref, lax.dot_general(av.astype(BF16), wd_ref[...], NN, preferred_element_type=F32), j)

        @pl.when(j == nf - 1)
        def _():
            y = x_ref[...] + 0.5 * acc_ref[...]
            if tgt is None:
                o0_ref[...] = y
                rs = lax.rsqrt(jnp.mean(y * y, axis=-1, keepdims=True) + RMS_EPS)
                o1_ref[...] = (y * rs * tail_ref[...]).astype(BF16)
            else:
                e = y - tail_ref[...]
                dy = e * (1.0 / d)
                o0_ref[...] = dy
                o1_ref[...] = dy.astype(BF16)
                _accumulate(rest[0], jnp.sum(e * e, axis=0, keepdims=True), pl.program_id(0))

    row = pl.BlockSpec((tm, d), lambda i, j: (i, 0))
    hid = pl.BlockSpec((tm, tf), lambda i, j: (i, j))
    vec = pl.BlockSpec((1, d), lambda i, j: (0, 0))
    out_specs = [hid, hid, row, row] + ([vec] if tgt is not None else [])
    out_shape = [jax.ShapeDtypeStruct((t, f), BF16)] * 2 + [jax.ShapeDtypeStruct((t, d), F32),
                                                            jax.ShapeDtypeStruct((t, d), BF16)]
    if tgt is not None:
        out_shape.append(jax.ShapeDtypeStruct((1, d), F32))
    return _pallas(
        body, name=name, grid=(t // tm, nf),
        in_specs=[row, row, pl.BlockSpec((d, tf), lambda i, j: (0, j)), pl.BlockSpec((d, tf), lambda i, j: (0, j)),
                  pl.BlockSpec((tf, d), lambda i, j: (j, 0)), vec if tgt is None else row],
        out_specs=out_specs, out_shape=out_shape, scratch_shapes=[pltpu.VMEM((tm, d), F32)],
        sem=("parallel" if tgt is None else "arbitrary", "arbitrary"),
        args=(h, x, wg, wu, wd, next_g if tgt is None else tgt))


def _ffn_bwd_mid(dy, wd, g, u, name):
    t, d = dy.shape
    f = wd.shape[0]
    tm, tf = _row_tile(t), _ffn_tile(f)

    def body(dy_ref, wd_ref, g_ref, u_ref, dg_ref, du_ref, a_ref):
        da = 0.5 * lax.dot_general(dy_ref[...].astype(BF16), wd_ref[...], NT, preferred_element_type=F32)
        gv = g_ref[...].astype(F32)
        uv = u_ref[...].astype(F32)
        s = _sigmoid(gv)
        silu = gv * s
        dg_ref[...] = (da * uv * (s * (1.0 + gv * (1.0 - s)))).astype(BF16)
        du_ref[...] = (da * silu).astype(BF16)
        a_ref[...] = (silu * uv).astype(BF16)

    hid = pl.BlockSpec((tm, tf), lambda i, j: (i, j))
    return _pallas(
        body, name=name, grid=(t // tm, f // tf),
        in_specs=[pl.BlockSpec((tm, d), lambda i, j: (i, 0)), pl.BlockSpec((tf, d), lambda i, j: (j, 0)), hid, hid],
        out_specs=[hid, hid, hid], out_shape=[jax.ShapeDtypeStruct((t, f), BF16)] * 3,
        sem=("parallel", "parallel"), args=(dy, wd, g, u))


def _rel_index(t, s_band):
    return jnp.clip(t + KPAD - s_band, -REL_CLIP, REL_CLIP) + REL_CLIP


def _bias_expand(rel_bias_pad):
    nh = rel_bias_pad.shape[0]

    def body(rb_ref, out_ref):
        rb = rb_ref[...]
        i_io = lax.broadcasted_iota(jnp.int32, (N_REL_PAD, BAND), 0)
        s_io = lax.broadcasted_iota(jnp.int32, (N_REL_PAD, BAND), 1)

        def row(t, carry):
            onehot = (i_io == _rel_index(t, s_io)).astype(F32)
            out_ref[t] = _dot_exact01_r(rb, onehot)
            return carry

        lax.fori_loop(0, CHUNK, row, 0)

    return _pallas(
        body, name="bias_expand", grid=(1,), in_specs=[pl.BlockSpec(rel_bias_pad.shape, lambda i: (0, 0))],
        out_specs=[pl.BlockSpec((CHUNK, nh, BAND), lambda i: (0, 0, 0))],
        out_shape=[jax.ShapeDtypeStruct((CHUNK, nh, BAND), F32)], sem=("arbitrary",), args=(rel_bias_pad,))[0]


def _bias_fold(dbias):
    ng, nh = dbias.shape[0], dbias.shape[2]

    def body(db_ref, out_ref):
        s_io = lax.broadcasted_iota(jnp.int32, (BAND, N_REL_PAD), 0)
        i_io = lax.broadcasted_iota(jnp.int32, (BAND, N_REL_PAD), 1)

        def row(t, acc):
            onehot = (i_io == _rel_index(t, s_io)).astype(F32)
            d = db_ref[0, t]
            for gi in range(1, ng):
                d = d + db_ref[gi, t]
            return acc + _dot_exact01_r(d, onehot)

        out_ref[...] = lax.fori_loop(0, CHUNK, row, jnp.zeros((nh, N_REL_PAD), F32))

    return _pallas(
        body, name="bias_fold", grid=(1,), in_specs=[pl.BlockSpec(dbias.shape, lambda i: (0, 0, 0, 0))],
        out_specs=[pl.BlockSpec((nh, N_REL_PAD), lambda i: (0, 0))],
        out_shape=[jax.ShapeDtypeStruct((nh, N_REL_PAD), F32)], sem=("arbitrary",), args=(dbias,))[0]


def _left_half(shape):
    return lax.broadcasted_iota(jnp.int32, shape, len(shape) - 1) < ATTN_HEAD_DIM


def _stack_heads(v):
    left = _left_half(v.shape)
    zero = jnp.zeros_like(v)
    return jnp.concatenate([jnp.where(left, v, zero), jnp.where(left, zero, v)], axis=0)


def _unstack_heads(v):
    return jnp.where(_left_half((CHUNK, 128)), v[0:CHUNK, :], v[CHUNK:2 * CHUNK, :])


def _half_mean(v):
    r = lax.broadcasted_iota(jnp.int32, (128, 128), 0) < ATTN_HEAD_DIM
    c = lax.broadcasted_iota(jnp.int32, (128, 128), 1) < ATTN_HEAD_DIM
    return _dot_exact01_r(v, r == c) * (1.0 / ATTN_HEAD_DIM)


def _attn_prepare(q_ref, k_ref, v_ref, gq_ref, gk_ref, qs_scr, k_scr, v_scr):
    q, k = q_ref[...], k_ref[...]
    rq = lax.rsqrt(_half_mean(q * q) + RMS_EPS)
    rk = lax.rsqrt(_half_mean(k * k) + RMS_EPS)
    qhat, khat = q * rq, k * rk
    qs_scr[...] = (qhat * gq_ref[...] * ATTN_HEAD_DIM ** -0.5).astype(BF16)
    k_scr[0:KPAD, :] = jnp.zeros((KPAD, 128), BF16)
    v_scr[0:KPAD, :] = jnp.zeros((KPAD, 128), BF16)
    k_scr[KPAD:, :] = (khat * gk_ref[...]).astype(BF16)
    v_scr[KPAD:, :] = v_ref[...].astype(BF16)
    return qhat, rq, khat, rk


def _attn_scores(qs_scr, k_scr, bias_ref, c):
    r0 = pl.multiple_of(c * CHUNK, CHUNK)
    qst = _stack_heads(qs_scr[pl.ds(r0, CHUNK), :])
    kb = k_scr[pl.ds(r0, BAND), :]
    s = lax.dot_general(qst, kb, NT, preferred_element_type=F32) + bias_ref[...]
    col = lax.broadcasted_iota(jnp.int32, (2 * CHUNK, BAND), 1)
    first = jnp.maximum(CHUNK, (LEFT_CHUNKS + 1 - c) * CHUNK)
    s = jnp.where(col >= first, s, -jnp.inf)
    e = jnp.exp(s - jnp.max(s, axis=-1, keepdims=True))
    return e, 1.0 / jnp.sum(e, axis=-1, keepdims=True), qst, kb, r0


def _attn_fwd(proj, bias, gq, gk, nb, seq):
    nc = seq // CHUNK

    def body(q_ref, k_ref, v_ref, bias_ref, gq_ref, gk_ref, o_ref, qs_scr, k_scr, v_scr):
        _attn_prepare(q_ref, k_ref, v_ref, gq_ref, gk_ref, qs_scr, k_scr, v_scr)

        def chunk(c, carry):
            e, inv, _, _, r0 = _attn_scores(qs_scr, k_scr, bias_ref, c)
            vb = v_scr[pl.ds(r0, BAND), :]
            o_ref[pl.ds(r0, CHUNK), :] = _unstack_heads(
                lax.dot_general(e.astype(BF16), vb, NN, preferred_element_type=F32) * inv)
            return carry

        lax.fori_loop(0, nc, chunk, 0, unroll=ATTN_UNROLL)

    def col(off):
        return pl.BlockSpec((seq, 128), lambda b, hp: (b, off + hp))

    vec = pl.BlockSpec((1, 128), lambda b, hp: (0, 0))
    return _pallas(
        body, name="attn_fwd", grid=(nb, ATTN_HEADS // 2),
        in_specs=[col(0), col(4), col(8), pl.BlockSpec((2 * CHUNK, BAND), lambda b, hp: (hp, 0)), vec, vec],
        out_specs=[pl.BlockSpec((seq, 128), lambda b, hp: (b, hp))],
        out_shape=[jax.ShapeDtypeStruct((nb * seq, ATTN_WIDTH), F32)],
        scratch_shapes=[pltpu.VMEM((seq, 128), BF16), pltpu.VMEM((seq + KPAD, 128), BF16),
                        pltpu.VMEM((seq + KPAD, 128), BF16)],
        sem=("parallel", "parallel"), args=(proj, proj, proj, bias, gq, gk))[0]


def _attn_bwd(proj, out, dout, bias, gq, gk, nb, seq):
    nc = seq // CHUNK
    scale = ATTN_HEAD_DIM ** -0.5

    def body(q_ref, k_ref, v_ref, o_ref, do_ref, bias_ref, gq_ref, gk_ref,
             dq_ref, dk_ref, dv_ref, dbias_ref, dgq_ref, dgk_ref,
             qs_scr, k_scr, v_scr, dqn_scr, dk_scr, dv_scr, db_scr):
        qhat, rq, khat, rk = _attn_prepare(q_ref, k_ref, v_ref, gq_ref, gk_ref, qs_scr, k_scr, v_scr)
        dk_scr[...] = jnp.zeros_like(dk_scr)
        dv_scr[...] = jnp.zeros_like(dv_scr)
        db_scr[...] = jnp.zeros_like(db_scr)

        def chunk(c, carry):
            e, inv, qst, kb, r0 = _attn_scores(qs_scr, k_scr, bias_ref, c)
            p = e * inv
            vb = v_scr[pl.ds(r0, BAND), :]
            do_c = do_ref[pl.ds(r0, CHUNK), :]
            dost = _stack_heads(do_c)
            drow = jnp.sum(dost * _stack_heads(o_ref[pl.ds(r0, CHUNK), :]), axis=-1, keepdims=True)
            dp = lax.dot_general(dost.astype(BF16), vb, NT, preferred_element_type=F32)
            ds = p * (dp - drow)
            db_scr[...] += ds
            dqn_scr[pl.ds(r0, CHUNK), :] = scale * _unstack_heads(
                lax.dot_general(ds.astype(BF16), kb, NN, preferred_element_type=F32))
            dk_scr[pl.ds(r0, BAND), :] += lax.dot_general(ds.T.astype(BF16), qst, NN, preferred_element_type=F32)
            dv_scr[pl.ds(r0, BAND), :] += _dot(p.T, dost)
            return carry

        lax.fori_loop(0, nc, chunk, 0, unroll=ATTN_UNROLL)

        def norm_bwd(dn, hat, r, g_ref):
            gd = dn * g_ref[...]
            return r * (gd - hat * _half_mean(gd * hat)), jnp.sum(dn * hat, axis=0, keepdims=True)

        dq, dgq = norm_bwd(dqn_scr[...], qhat, rq, gq_ref)
        dk, dgk = norm_bwd(dk_scr[KPAD:, :], khat, rk, gk_ref)
        dq_ref[...] = dq.astype(BF16)
        dk_ref[...] = dk.astype(BF16)
        dv_ref[...] = dv_scr[KPAD:, :].astype(BF16)
        dbias_ref[0] = db_scr[...]
        dgq_ref[0] = dgq
        dgk_ref[0] = dgk

    def col(off):
        return pl.BlockSpec((seq, 128), lambda b, hp: (b, off + hp))

    vec = pl.BlockSpec((1, 128), lambda b, hp: (0, 0))
    gvec = pl.BlockSpec((1, 1, 128), lambda b, hp: (b * (ATTN_HEADS // 2) + hp, 0, 0))
    t = nb * seq
    return _pallas(
        body, name="attn_bwd", grid=(nb, ATTN_HEADS // 2),
        in_specs=[col(0), col(4), col(8), col(0), col(0),
                  pl.BlockSpec((2 * CHUNK, BAND), lambda b, hp: (hp, 0)), vec, vec],
        out_specs=[col(0), col(0), col(0), pl.BlockSpec((1, 2 * CHUNK, BAND), lambda b, hp: (b, hp, 0)),
                   gvec, gvec],
        out_shape=[jax.ShapeDtypeStruct((t, ATTN_WIDTH), BF16)] * 3
        + [jax.ShapeDtypeStruct((nb, ATTN_HEADS * CHUNK, BAND), F32)]
        + [jax.ShapeDtypeStruct((nb * ATTN_HEADS // 2, 1, 128), F32)] * 2,
        scratch_shapes=[pltpu.VMEM((seq, 128), BF16), pltpu.VMEM((seq + KPAD, 128), BF16),
                        pltpu.VMEM((seq + KPAD, 128), BF16), pltpu.VMEM((seq, 128), F32),
                        pltpu.VMEM((seq + KPAD, 128), F32), pltpu.VMEM((seq + KPAD, 128), F32),
                        pltpu.VMEM((2 * CHUNK, BAND), F32)],
        sem=("parallel", "parallel"), args=(proj, proj, proj, out, dout, bias, gq, gk))


def _tri(lower):
    r = lax.broadcasted_iota(jnp.int32, (CHUNK, CHUNK), 0)
    c = lax.broadcasted_iota(jnp.int32, (CHUNK, CHUNK), 1)
    return (r >= c) if lower else (r <= c)


def _hgrn_gates(hq, hf, lb):
    sq = _sigmoid(hq)
    sf = _sigmoid(hf)
    return hq * sq, sq, sf, lb + (1.0 - lb) * sf


def _hgrn_offdiag(q_s, k_s, b_s):
    row = lax.broadcasted_iota(jnp.int32, (CHUNK, HGRN_HEAD_DIM), 0)
    bv, qv, kv = b_s[...], q_s[...], k_s[...]
    eqs, eks = [], []
    for i in range(1, N_SUB):
        r = b_s[pl.ds(SUB * i - 1, 1), :]
        in_i = (row >= SUB * i) & (row < SUB * (i + 1))
        eqs.append(jnp.exp(jnp.where(in_i, bv - r, -jnp.inf)))
        eks.append(jnp.exp(jnp.where(row < SUB * i, r - bv, -jnp.inf)))
    eq = jnp.concatenate(eqs, axis=1)
    ek = jnp.concatenate(eks, axis=1)
    qt = jnp.concatenate([qv] * (N_SUB - 1), axis=1) * eq
    kt = jnp.concatenate([kv] * (N_SUB - 1), axis=1) * ek
    return qt, kt, eq, ek


def _hgrn_diag_e(b_s, i, s):
    t_io = lax.broadcasted_iota(jnp.int32, (SUB, HGRN_HEAD_DIM), 0)
    bi = b_s[pl.ds(SUB * i, SUB), :]
    return jnp.exp(jnp.where(t_io >= s, bi - b_s[pl.ds(SUB * i + s, 1), :], -jnp.inf)), t_io


def _hgrn_intra(q_s, k_s, b_s, a_s, qt, kt):
    ktp = jnp.concatenate([kt, jnp.zeros_like(kt)], axis=0)
    a_s[...] = _dot(qt, ktp, NT)
    col = lax.broadcasted_iota(jnp.int32, (SUB, HGRN_HEAD_DIM), 1)
    for i in range(N_SUB):
        qi = q_s[pl.ds(SUB * i, SUB), :]
        ai = jnp.zeros((SUB, HGRN_HEAD_DIM), F32)
        for s in range(SUB):
            e, _ = _hgrn_diag_e(b_s, i, s)
            a_col = jnp.sum(qi * k_s[pl.ds(SUB * i + s, 1), :] * e, axis=-1, keepdims=True)
            ai = ai + jnp.where(col == SUB * i + s, a_col, 0.0)
        a_s[pl.ds(SUB * i, SUB), :] += ai


def _hgrn_fwd(proj, lb, go, nb, seq):
    nc = seq // CHUNK
    hd = HGRN_HEAD_DIM

    def body(hq_ref, hf_ref, hi_ref, hg_ref, lb_ref, go_ref, y_ref, o_ref, st_ref, a_ref,
             st_all, q_all, k_all, b_all, a_all):
        st_all[...] = jnp.zeros_like(st_all)
        lower = _tri(True)

        def head_chunk(hh, c, rows):
            ln = slice(hd * hh, hd * (hh + 1))
            st, q_s, k_s, b_s, a_s = st_all.at[hh], q_all.at[hh], k_all.at[hh], b_all.at[hh], a_all.at[hh]
            q, _, _, f = _hgrn_gates(hq_ref[rows, ln], hf_ref[rows, ln], lb_ref[:, ln])
            v = hi_ref[rows, ln]
            b = _dot_exact01(lower, jnp.log(f))
            q_s[...] = q
            k_s[...] = 1.0 - f
            b_s[...] = b
            st_ref[hh, c] = st[...]
            qt, kt, _, _ = _hgrn_offdiag(q_s, k_s, b_s)
            _hgrn_intra(q_s, k_s, b_s, a_s, qt, kt)
            a16 = a_s[...].astype(BF16)
            a_ref[hh, c] = a16
            vp = jnp.concatenate([v, jnp.zeros_like(v)], axis=0)
            o = _dot(a16, vp) + _dot(q * jnp.exp(b), st[...], NT)
            bl = b_s[pl.ds(CHUNK - 1, 1), :]
            st[...] = st[...] * jnp.exp(bl) + _tn(v, (1.0 - f) * jnp.exp(bl - b))
            o_ref[rows, ln] = o
            n = o * lax.rsqrt(jnp.mean(o * o, axis=-1, keepdims=True) + RMS_EPS) * go_ref[...]
            hg = hg_ref[rows, ln]
            y_ref[rows, ln] = n * hg * _sigmoid(hg)

        def chunk(c, carry):
            rows = pl.ds(pl.multiple_of(c * CHUNK, CHUNK), CHUNK)
            for hh in range(HGRN_PER_STEP):
                head_chunk(hh, c, rows)
            return carry

        lax.fori_loop(0, nc, chunk, 0)

    hp, wide = HGRN_PER_STEP, HGRN_PER_STEP * hd

    def col(off):
        return pl.BlockSpec((seq, wide), lambda b, h: (b, off // hp + h))

    out = pl.BlockSpec((seq, wide), lambda b, h: (b, h))
    t = nb * seq
    return pl.pallas_call(
        body, name="hgrn_fwd", grid=(nb, HGRN_HEADS // hp),
        in_specs=[col(12), col(16), col(20), col(24), pl.BlockSpec((1, wide), lambda b, h: (0, h)),
                  pl.BlockSpec((1, hd), lambda b, h: (0, 0))],
        out_specs=[out, out, pl.BlockSpec((hp, nc, hd, hd), lambda b, h: (b * (HGRN_HEADS // hp) + h, 0, 0, 0)),
                   pl.BlockSpec((hp, nc, CHUNK, hd), lambda b, h: (b * (HGRN_HEADS // hp) + h, 0, 0, 0))],
        out_shape=[jax.ShapeDtypeStruct((t, HGRN_HEADS * hd), F32)] * 2
        + [jax.ShapeDtypeStruct((nb * HGRN_HEADS, nc, hd, hd), F32),
           jax.ShapeDtypeStruct((nb * HGRN_HEADS, nc, CHUNK, hd), BF16)],
        scratch_shapes=[pltpu.VMEM((hp, hd, hd), F32)] + [pltpu.VMEM((hp, CHUNK, hd), F32)] * 4,
        compiler_params=_params("parallel", "parallel"),
    )(proj, proj, proj, proj, lb, go)


def _hgrn_bwd(proj, lb, go, o_pre, states, scores, dout, nb, seq):
    nc = seq // CHUNK
    hd = HGRN_HEAD_DIM

    def body(hq_ref, hf_ref, hi_ref, hg_ref, lb_ref, go_ref, o_ref, st_ref, a_ref, dy_ref,
             dhq_ref, dhf_ref, dhi_ref, dhg_ref, dlb_ref, dgo_ref,
             dst_all, q_all, k_all, b_all, da_all, dqi_all, dki_all, dlb_all, dgo_all):
        dst_all[...] = jnp.zeros_like(dst_all)
        dlb_all[...] = jnp.zeros_like(dlb_all)
        dgo_all[...] = jnp.zeros_like(dgo_all)
        lower, upper = _tri(True), _tri(False)
        gov = go_ref[...]
        row = lax.broadcasted_iota(jnp.int32, (CHUNK, hd), 0)

        def head_chunk(hh, c, rows):
            ln = slice(hd * hh, hd * (hh + 1))
            dst, q_s, k_s, b_s = dst_all.at[hh], q_all.at[hh], k_all.at[hh], b_all.at[hh]
            da_s, dqi_s, dki_s = da_all.at[hh], dqi_all.at[hh], dki_all.at[hh]
            dlb_acc, dgo_acc = dlb_all.at[hh], dgo_all.at[hh]
            lbv = lb_ref[:, ln]
            hq, hf, v, hg = hq_ref[rows, ln], hf_ref[rows, ln], hi_ref[rows, ln], hg_ref[rows, ln]
            q, sq, sf, f = _hgrn_gates(hq, hf, lbv)
            kk = 1.0 - f
            b = _dot_exact01(lower, jnp.log(f))
            q_s[...] = q
            k_s[...] = kk
            b_s[...] = b
            bl = b_s[pl.ds(CHUNK - 1, 1), :]
            ebl = jnp.exp(bl)
            ekd = jnp.exp(bl - b)
            kd = kk * ekd
            eb = jnp.exp(b)
            qb = q * eb
            st0 = st_ref[hh, c]
            dst1 = dst[...]

            o = o_ref[rows, ln]
            dy = dy_ref[rows, ln]
            sg = _sigmoid(hg)
            rstd = lax.rsqrt(jnp.mean(o * o, axis=-1, keepdims=True) + RMS_EPS)
            ohat = o * rstd
            dn = dy * hg * sg
            dhg_ref[rows, ln] = (dy * ohat * gov * (sg * (1.0 + hg * (1.0 - sg)))).astype(BF16)
            dgo_acc[...] += jnp.sum(dn * ohat, axis=0, keepdims=True)
            gdn = dn * gov
            do = rstd * (gdn - ohat * jnp.mean(gdn * ohat, axis=-1, keepdims=True))

            qt, kt, eq, ek = _hgrn_offdiag(q_s, k_s, b_s)
            da = _dot(do, v, NT)
            dat = _dot(v, do, NT)
            da_s[...] = da
            dqo = _dot(da, kt) * eq
            dko = _dot(dat, qt) * ek
            dqi_s[...] = dqo[:, 0:hd] + dqo[:, hd:2 * hd] + dqo[:, 2 * hd:3 * hd]
            dki_s[...] = dko[:, 0:hd] + dko[:, hd:2 * hd] + dko[:, 2 * hd:3 * hd]
            col = lax.broadcasted_iota(jnp.int32, (SUB, CHUNK), 1)
            for i in range(N_SUB):
                qi = q_s[pl.ds(SUB * i, SUB), :]
                dai = da_s[pl.ds(SUB * i, SUB), :]
                dqd = jnp.zeros((SUB, hd), F32)
                dkd_ = jnp.zeros((SUB, hd), F32)
                for s in range(SUB):
                    e, t_io = _hgrn_diag_e(b_s, i, s)
                    dacol = jnp.sum(jnp.where(col == SUB * i + s, dai, 0.0), axis=-1, keepdims=True)
                    w = dacol * e
                    dqd = dqd + w * k_s[pl.ds(SUB * i + s, 1), :]
                    dkd_ = dkd_ + jnp.where(t_io == s, jnp.sum(w * qi, axis=0, keepdims=True), 0.0)
                dqi_s[pl.ds(SUB * i, SUB), :] += dqd
                dki_s[pl.ds(SUB * i, SUB), :] += dkd_
            dqi, dki = dqi_s[...], dki_s[...]

            dv = _tn(a_ref[hh, c].astype(F32), do)[0:CHUNK, :] + _dot(kd, dst1, NT)
            dqb = _dot(do, st0)
            dkd = _dot(v, dst1)
            t2 = dkd * kd
            dq = dqb * eb + dqi
            dk = dkd * ekd + dki
            dbl = jnp.sum(t2, axis=0, keepdims=True) + ebl * jnp.sum(st0 * dst1, axis=0, keepdims=True)
            db = dqb * qb - t2 + q * dqi - kk * dki + jnp.where(row == CHUNK - 1, dbl, 0.0)
            dg = _dot_exact01(upper, db)
            dst[...] = dst1 * ebl + _tn(do, qb)

            df = dg / f - dk
            dhf_ref[rows, ln] = (df * (1.0 - lbv) * sf * (1.0 - sf)).astype(BF16)
            dlb_acc[...] += jnp.sum(df * (1.0 - sf), axis=0, keepdims=True)
            dhq_ref[rows, ln] = (dq * (sq * (1.0 + hq * (1.0 - sq)))).astype(BF16)
            dhi_ref[rows, ln] = dv.astype(BF16)

        def chunk(it, carry):
            c = nc - 1 - it
            rows = pl.ds(pl.multiple_of(c * CHUNK, CHUNK), CHUNK)
            for hh in range(HGRN_PER_STEP):
                head_chunk(hh, c, rows)
            return carry

        lax.fori_loop(0, nc, chunk, 0)
        dlb_ref[...] = dlb_all[...]
        dgo_ref[...] = dgo_all[...]

    hp, wide = HGRN_PER_STEP, HGRN_PER_STEP * hd

    def col(off):
        return pl.BlockSpec((seq, wide), lambda b, h: (b, off // hp + h))

    out = pl.BlockSpec((seq, wide), lambda b, h: (b, h))
    part = pl.BlockSpec((hp, 1, hd), lambda b, h: (b * (HGRN_HEADS // hp) + h, 0, 0))
    t = nb * seq
    return pl.pallas_call(
        body, name="hgrn_bwd", grid=(nb, HGRN_HEADS // hp),
        in_specs=[col(12), col(16), col(20), col(24), pl.BlockSpec((1, wide), lambda b, h: (0, h)),
                  pl.BlockSpec((1, hd), lambda b, h: (0, 0)), out,
                  pl.BlockSpec((hp, nc, hd, hd), lambda b, h: (b * (HGRN_HEADS // hp) + h, 0, 0, 0)),
                  pl.BlockSpec((hp, nc, CHUNK, hd), lambda b, h: (b * (HGRN_HEADS // hp) + h, 0, 0, 0)), col(4)],
        out_specs=[out, out, out, out, part, part],
        out_shape=[jax.ShapeDtypeStruct((t, HGRN_HEADS * hd), BF16)] * 4
        + [jax.ShapeDtypeStruct((nb * HGRN_HEADS, 1, hd), F32)] * 2,
        scratch_shapes=[pltpu.VMEM((hp, hd, hd), F32)] + [pltpu.VMEM((hp, CHUNK, hd), F32)] * 3
        + [pltpu.VMEM((hp, CHUNK, CHUNK), F32)] + [pltpu.VMEM((hp, CHUNK, hd), F32)] * 2
        + [pltpu.VMEM((hp, 1, hd), F32)] * 2,
        compiler_params=_params("parallel", "parallel"),
    )(proj, proj, proj, proj, lb, go, o_pre, states, scores, dout)


def _lb_fwd(lower_bounds):
    def body(x_ref, o_ref):
        xv = x_ref[...]
        e = jnp.exp(xv - jnp.max(xv, axis=0, keepdims=True))
        o_ref[...] = e[0:1, :] / jnp.sum(e, axis=0, keepdims=True)

    return pl.pallas_call(body, name="lb_fwd",
                          out_shape=jax.ShapeDtypeStruct((1, lower_bounds.shape[1]), F32))(lower_bounds)


def _lb_bwd(lower_bounds, dlb_parts):
    ng = dlb_parts.shape[0]

    def body(x_ref, d_ref, o_ref):
        xv = x_ref[...]
        e = jnp.exp(xv - jnp.max(xv, axis=0, keepdims=True))
        p = e / jnp.sum(e, axis=0, keepdims=True)
        dlb = d_ref[0]
        for gi in range(1, ng):
            dlb = dlb + d_ref[gi]
        first = lax.broadcasted_iota(jnp.int32, xv.shape, 0) == 0
        o_ref[...] = p * (jnp.where(first, dlb, 0.0) - p[0:1, :] * dlb)

    return pl.pallas_call(body, name="lb_bwd",
                          out_shape=jax.ShapeDtypeStruct(lower_bounds.shape, F32))(lower_bounds, dlb_parts)


def _ffn_bwd(x, g, h, gate, up, dy, dy16, w, put, tag):
    wg, wu, wd = w[tag + "_w_gate"], w[tag + "_w_up"], w[tag + "_w_down"]
    dgate, dup, act = _ffn_bwd_mid(dy16, wd, gate, up, tag + "_bwd_mid")
    put(tag + "_w_down", _mm(act, dy16, ta=True, tm=1408, tn=512, scale=0.5, name=tag + "_dwd"))
    put(tag + "_w_gate", _mm(h, dgate, ta=True, tm=512, tn=1408, name=tag + "_dwg"))
    put(tag + "_w_up", _mm(h, dup, ta=True, tm=512, tn=1408, name=tag + "_dwu"))
    dh = _mm(dgate, wg, tb=True, tm=512, tn=1024, name=tag + "_dh_gate")
    return _mm(dup, wu, tb=True, tm=512, tn=1024, add=dh, norm_bwd=(x, g, dy), name=tag + "_dh_up")


def _local_step(x, tgt, sp, w, put, nb, seq):
    d = x.shape[1]
    h1 = _rms_fwd(x, sp["ffn1_norm_g"], "ffn1_norm")
    rb_pad = jnp.pad(sp["attn_rel_bias"], ((0, 0), (0, N_REL_PAD - N_REL)))
    bias = jnp.transpose(_bias_expand(rb_pad), (1, 0, 2)).reshape(ATTN_HEADS * CHUNK, BAND)
    gq2 = jnp.concatenate([sp["attn_q_norm_g"]] * 2, axis=1)
    gk2 = jnp.concatenate([sp["attn_k_norm_g"]] * 2, axis=1)
    lb = _lb_fwd(sp["hgrn_lower_bounds"])
    gate1, up1, x1, h2 = _ffn_fwd(h1, x, w["ffn1_w_gate"], w["ffn1_w_up"], w["ffn1_w_down"], "ffn1_fwd",
                                  next_g=sp["mix_norm_g"])
    proj = _mm(h2, w["w_in"], tm=256, tn=w["w_in"].shape[1], name="in_proj")
    attn = _attn_fwd(proj, bias, gq2, gk2, nb, seq)
    hy, ho, hstate, hscore = _hgrn_fwd(proj, lb, sp["hgrn_out_norm_g"], nb, seq)
    mix = jnp.concatenate([attn, hy], axis=1)
    x2, h3 = _mm(mix, w["w_out"], tm=512, tn=1024, add=x1, norm_g=sp["ffn2_norm_g"], name="out_proj")
    gate2, up2, dx3, dx3_16, sq = _ffn_fwd(h3, x2, w["ffn2_w_gate"], w["ffn2_w_up"], w["ffn2_w_down"], "ffn2_fwd",
                                           tgt=tgt)
    loss = 0.5 * jnp.sum(sq) / d

    dx2, dx2_16, dg3 = _ffn_bwd(x2, sp["ffn2_norm_g"], h3, gate2, up2, dx3, dx3_16, w, put, "ffn2")
    dmix = _mm(dx2_16, w["w_out"], tb=True, tm=512, tn=1024, name="out_proj_dx")
    put("w_out", _mm(mix, dx2_16, ta=True, tm=512, tn=1024, name="out_proj_dw"))
    dq, dk, dv, dbias, dgq, dgk = _attn_bwd(proj, attn, dmix, bias, gq2, gk2, nb, seq)
    dbias = jnp.transpose(dbias.reshape(nb, ATTN_HEADS, CHUNK, BAND), (0, 2, 1, 3))
    dgq = jnp.sum(dgq, axis=(0, 1)).reshape(2, ATTN_HEAD_DIM).sum(axis=0, keepdims=True)
    dgk = jnp.sum(dgk, axis=(0, 1)).reshape(2, ATTN_HEAD_DIM).sum(axis=0, keepdims=True)
    dhq, dhf, dhi, dhg, dlb, dgo = _hgrn_bwd(proj, lb, sp["hgrn_out_norm_g"], ho, hstate, hscore, dmix, nb, seq)
    dproj = jnp.concatenate([dq, dk, dv, dhq, dhf, dhi, dhg], axis=1)
    put("w_in", _mm(h2, dproj, ta=True, tm=512, tn=512, name="in_proj_dw"))
    dx1, dx1_16, dgm = _mm(dproj, w["w_in"], tb=True, tm=512, tn=1024, norm_bwd=(x1, sp["mix_norm_g"], dx2),
                           name="in_proj_dx")
    dx0, _, dg1 = _ffn_bwd(x, sp["ffn1_norm_g"], h1, gate1, up1, dx1, dx1_16, w, put, "ffn1")

    small = {
        "ffn1_norm_g": dg1, "mix_norm_g": dgm, "ffn2_norm_g": dg3,
        "attn_q_norm_g": dgq, "attn_k_norm_g": dgk,
        "attn_rel_bias": _bias_fold(dbias)[:, :N_REL],
        "hgrn_lower_bounds": _lb_bwd(sp["hgrn_lower_bounds"], dlb.reshape(nb, 1, HGRN_HEADS * HGRN_HEAD_DIM)),
        "hgrn_out_norm_g": jnp.sum(dgo, axis=(0, 1))[None, :],
    }
    return loss, dx0, small


MESH = pl.DeviceIdType.MESH
ANY = pl.BlockSpec(memory_space=pl.ANY)


def _coords():
    return lax.axis_index("x"), lax.axis_index("y"), lax.axis_index("c")


def _other_chips(x, y):
    return [(1 - x, y), (x, 1 - y), (1 - x, 1 - y)]


def _gather_side(shards):
    n = len(shards)

    def copies(ins, outs, sems):
        send_sems, recv_sems, local_sems = sems
        x, y, c = _coords()
        me, sibling = (x, y, c), (x, y, 1 - c)
        chips = _other_chips(x, y)

        def copy(i, k, block, to, src=None):
            bx, by, bc = block
            dst = outs[i].at[4 * bx + 2 * by + bc]
            return pltpu.make_async_remote_copy(
                src_ref=dst if src is None else src, dst_ref=dst, send_sem=send_sems.at[i, k],
                recv_sem=recv_sems.at[i, k], device_id=to, device_id_type=MESH)

        mine = [pltpu.make_async_copy(ins[i], outs[i].at[4 * x + 2 * y + c], local_sems.at[i]) for i in range(n)]
        own = []
        for i in range(n):
            own.append(copy(i, 0, me, sibling, src=ins[i]))
            own += [copy(i, 1 + j, me, (*chip, c), src=ins[i]) for j, chip in enumerate(chips)]
        return copy, mine, own, me, sibling, chips, c

    def start(ins, outs, sems):
        _, mine, own, *_ = copies(ins, outs, sems)
        for cp in mine + own:
            cp.start()

    def finish(ins, outs, sems):
        copy, mine, own, me, sibling, chips, c = copies(ins, outs, sems)
        passed = []
        for i in range(n):
            for j, chip in enumerate(chips):
                copy(i, 1 + j, (*chip, c), me).wait_recv()
                passed.append(copy(i, 4 + j, (*chip, c), sibling))
                passed[-1].start()
        for i in range(n):
            copy(i, 0, sibling, me).wait_recv()
            for j, chip in enumerate(chips):
                copy(i, 4 + j, (*chip, 1 - c), me).wait_recv()
        for cp in own + passed:
            cp.wait_send()
        for cp in mine:
            cp.wait()

    return _Side(list(shards), [jax.ShapeDtypeStruct((N_DEV,) + s.shape, s.dtype) for s in shards],
                 [pltpu.SemaphoreType.DMA((n, 7)), pltpu.SemaphoreType.DMA((n, 7)), pltpu.SemaphoreType.DMA((n,))],
                 start, finish)


def _pair_side(grads):
    n = len(grads)

    def copies(ins, outs, sems):
        send_sems, recv_sems = sems
        x, y, c = _coords()
        return [pltpu.make_async_remote_copy(
            src_ref=ins[i].at[2 * k + 1 - c], dst_ref=outs[i].at[k], send_sem=send_sems.at[i, k],
            recv_sem=recv_sems.at[i, k], device_id=(x, y, 1 - c), device_id_type=MESH)
            for i in range(n) for k in range(4)]

    def start(ins, outs, sems):
        for cp in copies(ins, outs, sems):
            cp.start()

    def finish(ins, outs, sems):
        for cp in copies(ins, outs, sems):
            cp.wait()

    return _Side(list(grads), [jax.ShapeDtypeStruct((4,) + g.shape[1:], g.dtype) for g in grads],
                 [pltpu.SemaphoreType.DMA((n, 4)), pltpu.SemaphoreType.DMA((n, 4))], start, finish)


def _pair_add(grad, recv, core, name):
    _, r, cdim = grad.shape

    def body(c_ref, g_ref, r_ref, o_ref):
        o_ref[...] = (g_ref[...] + r_ref[...]).astype(BF16)

    blk = (1, r, cdim)
    return pl.pallas_call(
        body, name=name,
        grid_spec=pltpu.PrefetchScalarGridSpec(
            num_scalar_prefetch=1, grid=(4,),
            in_specs=[pl.BlockSpec(blk, lambda k, c_ref: (2 * k + c_ref[0], 0, 0)),
                      pl.BlockSpec(blk, lambda k, c_ref: (k, 0, 0))],
            out_specs=pl.BlockSpec(blk, lambda k, c_ref: (k, 0, 0))),
        out_shape=jax.ShapeDtypeStruct((4, r, cdim), BF16),
        compiler_params=_params("arbitrary"),
    )(core, grad, recv)


def _chip_side(parts):
    n = len(parts)

    def copies(ins, outs, sems):
        send_sems, recv_sems, local_sems = sems
        x, y, c = _coords()
        chips = _other_chips(x, y)
        mine = [pltpu.make_async_copy(ins[i].at[2 * x + y], outs[i].at[2 * x + y], local_sems.at[i])
                for i in range(n)]
        sent = [pltpu.make_async_remote_copy(
            src_ref=ins[i].at[2 * px + py], dst_ref=outs[i].at[2 * x + y], send_sem=send_sems.at[i, j],
            recv_sem=recv_sems.at[i, j], device_id=(px, py, c), device_id_type=MESH)
            for i in range(n) for j, (px, py) in enumerate(chips)]
        return mine, sent, chips, c

    def start(ins, outs, sems):
        mine, sent, _, _ = copies(ins, outs, sems)
        for cp in mine + sent:
            cp.start()

    def finish(ins, outs, sems):
        mine, sent, chips, c = copies(ins, outs, sems)
        send_sems, recv_sems, _ = sems
        for i in range(n):
            for j, (px, py) in enumerate(chips):
                landed = outs[i].at[2 * px + py]
                pltpu.make_async_remote_copy(
                    src_ref=landed, dst_ref=landed, send_sem=send_sems.at[i, j], recv_sem=recv_sems.at[i, j],
                    device_id=(px, py, c), device_id_type=MESH).wait_recv()
        for cp in sent:
            cp.wait_send()
        for cp in mine:
            cp.wait()

    return _Side(list(parts), [jax.ShapeDtypeStruct(p.shape, p.dtype) for p in parts],
                 [pltpu.SemaphoreType.DMA((n, 3)), pltpu.SemaphoreType.DMA((n, 3)), pltpu.SemaphoreType.DMA((n,))],
                 start, finish)


def _all_reduce_small(v):
    r = v.shape[0]

    def body(v_ref, o_ref, buf, send_sems, recv_sems):
        x, y, c = _coords()
        me = 4 * x + 2 * y + c
        buf[me] = v_ref[...]
        cps = []
        for k in range(1, N_DEV):
            px = 1 - x if k & 4 else x
            py = 1 - y if k & 2 else y
            pc = 1 - c if k & 1 else c
            cps.append((pltpu.make_async_remote_copy(
                src_ref=v_ref, dst_ref=buf.at[me], send_sem=send_sems.at[k - 1], recv_sem=recv_sems.at[k - 1],
                device_id=(px, py, pc), device_id_type=MESH), 4 * px + 2 * py + pc))
        for cp, _ in cps:
            cp.start()
        for k, (cp, peer) in enumerate(cps):
            pltpu.make_async_remote_copy(
                src_ref=v_ref, dst_ref=buf.at[peer], send_sem=send_sems.at[k], recv_sem=recv_sems.at[k],
                device_id=(x, y, c), device_id_type=MESH).wait_recv()
        for cp, _ in cps:
            cp.wait_send()
        acc = buf[0]
        for j in range(1, N_DEV):
            acc = acc + buf[j]
        o_ref[...] = acc

    return pl.pallas_call(
        body, name="small_all_reduce", out_shape=jax.ShapeDtypeStruct(v.shape, F32),
        in_specs=[pl.BlockSpec(memory_space=pltpu.VMEM)], out_specs=pl.BlockSpec(memory_space=pltpu.VMEM),
        scratch_shapes=[pltpu.VMEM((N_DEV, r, 128), F32), pltpu.SemaphoreType.DMA((N_DEV - 1,)),
                        pltpu.SemaphoreType.DMA((N_DEV - 1,))],
    )(v)


def _adamw(w, m, v, g, name):
    parts = w.ndim == 3
    r, cdim = w.shape[-2:]
    tr = r // 4 if r % 32 == 0 else r

    def body(w_ref, m_ref, v_ref, g_ref, go_ref, d_ref, mo_ref, vo_ref):
        if parts:
            gv = g_ref[0].astype(F32)
            for k in range(1, 4):
                gv = gv + g_ref[k].astype(F32)
            gv = gv[None]
        else:
            gv = g_ref[...]
        m2 = ADAM_B1 * m_ref[...] + (1.0 - ADAM_B1) * gv
        v2 = ADAM_B2 * v_ref[...] + (1.0 - ADAM_B2) * (gv * gv)
        m_hat = m2 / (1.0 - ADAM_B1 ** ADAM_STEP)
        v_hat = v2 / (1.0 - ADAM_B2 ** ADAM_STEP)
        go_ref[...] = gv
        d_ref[...] = -ADAM_LR * (m_hat / (jnp.sqrt(v_hat) + ADAM_EPS) + ADAM_WD * w_ref[...])
        mo_ref[...] = m2
        vo_ref[...] = v2

    if parts:
        row = pl.BlockSpec((1, tr, cdim), lambda i: (0, i, 0))
        g_spec = pl.BlockSpec((4, tr, cdim), lambda i: (0, i, 0))
    else:
        row = g_spec = pl.BlockSpec((tr, cdim), lambda i: (i, 0))
    return pl.pallas_call(
        body, name=name, grid=(r // tr,), in_specs=[row, row, row, g_spec], out_specs=[row] * 4,
        out_shape=[jax.ShapeDtypeStruct(w.shape, F32)] * 4,
        compiler_params=_params("parallel"),
    )(w, m, v, g)


WEIGHTS = ["ffn1_norm_g", "ffn1_w_gate", "ffn1_w_up", "ffn1_w_down", "mix_norm_g", "w_in", "attn_q_norm_g",
           "attn_k_norm_g", "attn_rel_bias", "hgrn_lower_bounds", "hgrn_out_norm_g", "w_out", "ffn2_norm_g",
           "ffn2_w_gate", "ffn2_w_up", "ffn2_w_down"]
COL_SHARDED = ("ffn1_w_gate", "ffn1_w_up", "w_in", "ffn2_w_gate", "ffn2_w_up")
ROW_SHARDED = ("ffn1_w_down", "w_out", "ffn2_w_down")
BIG = [n for n in WEIGHTS if n in COL_SHARDED or n in ROW_SHARDED]
SMALL = [n for n in WEIGHTS if n not in BIG]
PACK_ROWS = 8
FFN2 = ["ffn2_w_down", "ffn2_w_gate", "ffn2_w_up"]
MIXER = ["w_out", "w_in"]

PLAN = {
    "ffn1_norm": [("gather", ["ffn1_w_down"])],
    "bias_expand": [("gather", ["ffn1_w_gate", "ffn1_w_up"])],
    "ffn1_fwd": [("gather", MIXER)],
    "attn_fwd": [("gather", FFN2)],
    "ffn2_dh_gate": [("pair", FFN2)],
    "attn_bwd": [("chip", FFN2)],
    "in_proj_dx": [("pair", MIXER)],
    "ffn1_bwd_mid": [("chip", MIXER)],
    "ffn1_dwg": [("pair", ["ffn1_w_down"])],
    "ffn1_dwu": [("chip", ["ffn1_w_down"]), ("pair", ["ffn1_w_gate"])],
    "ffn1_dh_gate": [("chip", ["ffn1_w_gate"]), ("pair", ["ffn1_w_up"])],
    "bias_fold": [("chip", ["ffn1_w_up"])],
}


def _join_sides(sides):
    def split(refs, counts):
        out, at = [], 0
        for n in counts:
            out.append(refs[at:at + n])
            at += n
        return out

    n_in, n_out, n_sem = ([len(getattr(s, f)) for s in sides] for f in ("ins", "out_shape", "sems"))

    def run(which):
        def go(ins, outs, sems):
            for s, i, o, m in zip(sides, split(ins, n_in), split(outs, n_out), split(sems, n_sem)):
                getattr(s, which)(i, o, m)
        return go

    return _Side([a for s in sides for a in s.ins], [a for s in sides for a in s.out_shape],
                 [a for s in sides for a in s.sems], run("start"), run("finish"))


class _Schedule:
    def __init__(self, shards):
        self.shards = shards
        self.weights = {}
        self.sliced = {}
        self.partials = {}
        self.reduced = {}

    def put(self, name, grad):
        r, c = self.shards[name].shape
        if name in COL_SHARDED:
            self.sliced[name] = jnp.transpose(grad.reshape(r, N_DEV, c), (1, 0, 2))
        else:
            self.sliced[name] = grad.reshape(N_DEV, r, c)

    def side_for(self, call):
        if call not in PLAN:
            return None
        sides = []
        for kind, names in PLAN[call]:
            if kind == "gather":
                sides.append(_gather_side([self.shards[n] for n in names]))
            elif kind == "pair":
                sides.append(_pair_side([self.sliced[n] for n in names]))
            else:
                sides.append(_chip_side([self.partials[n] for n in names]))
        return _join_sides(sides)

    def done(self, call, outs):
        at = 0
        for kind, names in PLAN[call]:
            self.file(kind, names, outs[at:at + len(names)])
            at += len(names)

    def file(self, kind, names, outs):
        for n, o in zip(names, outs):
            if kind == "gather":
                if n in COL_SHARDED:
                    self.weights[n] = jnp.transpose(o, (1, 0, 2)).reshape(o.shape[1], N_DEV * o.shape[2])
                else:
                    self.weights[n] = o.reshape(N_DEV * o.shape[1], o.shape[2])
            elif kind == "pair":
                core = lax.axis_index("c").astype(jnp.int32).reshape(1)
                self.partials[n] = _pair_add(self.sliced[n], o, core, n + "_pair_add")
            else:
                self.reduced[n] = o


def _pack_small(vals, loss=None):
    parts = []
    for n in SMALL:
        a = vals[n]
        if n == "attn_rel_bias":
            a = jnp.pad(a.reshape(ATTN_HEADS, N_REL), ((0, 0), (0, N_REL_PAD - N_REL)))
        flat = a.reshape(-1)
        size = -(-flat.shape[0] // (PACK_ROWS * 128)) * PACK_ROWS * 128
        parts.append(jnp.pad(flat, (0, size - flat.shape[0])).reshape(-1, 128))
    tail = jnp.zeros((PACK_ROWS, 128), F32)
    if loss is not None:
        tail = tail.at[0, 0].set(loss)
    return jnp.concatenate(parts + [tail], axis=0)


def _unpack_small(packed, shapes):
    out, row = {}, 0
    for n in SMALL:
        shape = shapes[n]
        if n == "attn_rel_bias":
            rows = ATTN_HEADS * N_REL_PAD // 128
            out[n] = packed[row:row + rows].reshape(ATTN_HEADS, N_REL_PAD)[:, :N_REL].reshape(shape)
        else:
            size = 1
            for s in shape:
                size *= s
            rows = -(-size // (PACK_ROWS * 128)) * PACK_ROWS
            out[n] = packed[row:row + rows].reshape(-1)[:size].reshape(shape)
        row += rows
    return out, packed[row, 0]


def kernel(x, ffn1_norm_g, ffn1_w_gate, ffn1_w_up, ffn1_w_down, mix_norm_g, w_in, attn_q_norm_g, attn_k_norm_g, attn_rel_bias, hgrn_lower_bounds, hgrn_out_norm_g, w_out, ffn2_norm_g, ffn2_w_gate, ffn2_w_up, ffn2_w_down, loss_target, m_ffn1_norm_g, m_ffn1_w_gate, m_ffn1_w_up, m_ffn1_w_down, m_mix_norm_g, m_w_in, m_attn_q_norm_g, m_attn_k_norm_g, m_attn_rel_bias, m_hgrn_lower_bounds, m_hgrn_out_norm_g, m_w_out, m_ffn2_norm_g, m_ffn2_w_gate, m_ffn2_w_up, m_ffn2_w_down, v_ffn1_norm_g, v_ffn1_w_gate, v_ffn1_w_up, v_ffn1_w_down, v_mix_norm_g, v_w_in, v_attn_q_norm_g, v_attn_k_norm_g, v_attn_rel_bias, v_hgrn_lower_bounds, v_hgrn_out_norm_g, v_w_out, v_ffn2_norm_g, v_ffn2_w_gate, v_ffn2_w_up, v_ffn2_w_down):
    wts = dict(zip(WEIGHTS, (ffn1_norm_g, ffn1_w_gate, ffn1_w_up, ffn1_w_down, mix_norm_g, w_in, attn_q_norm_g,
                             attn_k_norm_g, attn_rel_bias, hgrn_lower_bounds, hgrn_out_norm_g, w_out, ffn2_norm_g,
                             ffn2_w_gate, ffn2_w_up, ffn2_w_down)))
    mom = dict(zip(WEIGHTS, (m_ffn1_norm_g, m_ffn1_w_gate, m_ffn1_w_up, m_ffn1_w_down, m_mix_norm_g, m_w_in,
                             m_attn_q_norm_g, m_attn_k_norm_g, m_attn_rel_bias, m_hgrn_lower_bounds,
                             m_hgrn_out_norm_g, m_w_out, m_ffn2_norm_g, m_ffn2_w_gate, m_ffn2_w_up, m_ffn2_w_down)))
    var = dict(zip(WEIGHTS, (v_ffn1_norm_g, v_ffn1_w_gate, v_ffn1_w_up, v_ffn1_w_down, v_mix_norm_g, v_w_in,
                             v_attn_q_norm_g, v_attn_k_norm_g, v_attn_rel_bias, v_hgrn_lower_bounds,
                             v_hgrn_out_norm_g, v_w_out, v_ffn2_norm_g, v_ffn2_w_gate, v_ffn2_w_up, v_ffn2_w_down)))
    nb, seq, d = x.shape
    shapes = {n: wts[n].shape for n in WEIGHTS}

    sched = _Schedule({n: wts[n][0].astype(BF16) for n in BIG})
    sp ={n: wts[n] for n in SMALL}
    sp["attn_rel_bias"] = wts["attn_rel_bias"][0]
    _ACTIVE[0] = sched
    try:
        loss, dx, dsmall = _local_step(x.reshape(nb * seq, d), loss_target.reshape(nb * seq, d), sp,
                                       sched.weights, sched.put, nb, seq)
    finally:
        _ACTIVE[0] = None
    reduced = sched.reduced

    small_sum = _all_reduce_small(_pack_small(dsmall, loss))
    gsmall, loss_total = _unpack_small(small_sum, shapes)

    grads, deltas, new_m, new_v = {}, {}, {}, {}
    for n in BIG:
        grads[n], deltas[n], new_m[n], new_v[n] = _adamw(wts[n], mom[n], var[n], reduced[n], n + "_adamw")
    packed = _adamw(_pack_small(wts), _pack_small(mom), _pack_small(var), small_sum, "small_adamw")
    for dst, p in zip((deltas, new_m, new_v), packed[1:]):
        dst.update(_unpack_small(p, shapes)[0])
    grads.update(gsmall)

    return (loss_total, dx.reshape(nb, seq, d), *[grads[n] for n in WEIGHTS], *[deltas[n] for n in WEIGHTS],
            *[new_m[n] for n in WEIGHTS], *[new_v[n] for n in WEIGHTS])
```

```python
import functools

import jax
import jax.numpy as jnp
from jax import lax
from jax.experimental import pallas as pl
from jax.experimental.pallas import tpu as pltpu

F32 = jnp.float32
BF16 = jnp.bfloat16

RMS_EPS = 1e-6
CHUNK = 64
LEFT_CHUNKS = 8
BAND = (LEFT_CHUNKS + 2) * CHUNK
KPAD = BAND - CHUNK
REL_CLIP = 128
N_REL = 2 * REL_CLIP + 1
N_REL_PAD = 384
ATTN_HEADS = 8
ATTN_HEAD_DIM = 64
ATTN_WIDTH = ATTN_HEADS * ATTN_HEAD_DIM
ATTN_UNROLL = 4
HGRN_HEADS = 4
HGRN_HEAD_DIM = 128
HGRN_PER_STEP = 2
SUB = 16
N_SUB = CHUNK // SUB
N_DEV = 8

ADAM_LR = 0.001
ADAM_B1 = 0.9
ADAM_B2 = 0.999
ADAM_EPS = 1e-08
ADAM_WD = 0.01
ADAM_STEP = 10

VMEM_LIMIT = 56 * 1024 * 1024

NT = (((1,), (1,)), ((), ()))
NN = (((1,), (0,)), ((), ()))


def _params(*sem):
    return pltpu.CompilerParams(dimension_semantics=sem, vmem_limit_bytes=VMEM_LIMIT)


def _sigmoid(v):
    return 0.5 * jnp.tanh(0.5 * v) + 0.5


def _dot(a, b, dims=NN):
    return lax.dot_general(a.astype(BF16), b.astype(BF16), dims, preferred_element_type=F32)


def _dot_exact01(m01, v):
    m = m01.astype(BF16)
    hi = v.astype(BF16)
    r1 = v - hi.astype(F32)
    mid = r1.astype(BF16)
    lo = (r1 - mid.astype(F32)).astype(BF16)
    out = lax.dot_general(m, hi, NN, preferred_element_type=F32)
    out = out + lax.dot_general(m, mid, NN, preferred_element_type=F32)
    return out + lax.dot_general(m, lo, NN, preferred_element_type=F32)


def _dot_exact01_r(v, m01):
    m = m01.astype(BF16)
    hi = v.astype(BF16)
    r1 = v - hi.astype(F32)
    mid = r1.astype(BF16)
    lo = (r1 - mid.astype(F32)).astype(BF16)
    out = lax.dot_general(hi, m, NN, preferred_element_type=F32)
    out = out + lax.dot_general(mid, m, NN, preferred_element_type=F32)
    return out + lax.dot_general(lo, m, NN, preferred_element_type=F32)


def _tn(a, b):
    ap = jnp.concatenate([a, jnp.zeros_like(a)], axis=0)
    bp = jnp.concatenate([b, jnp.zeros_like(b)], axis=0)
    return _dot(ap.T, bp)


def _row_tile(t):
    for tm in (512, 256, 128, 64, 32, 16, 8):
        if t % tm == 0:
            return tm
    raise ValueError(t)


class _Side:
    def __init__(self, ins, out_shape, sems, start, finish):
        self.ins, self.out_shape, self.sems, self.start, self.finish = ins, out_shape, sems, start, finish


_ACTIVE = [None]


def _pallas(body, *, name, grid, in_specs, out_specs, out_shape, scratch_shapes=(), sem, args):
    sched = _ACTIVE[0]
    side = sched.side_for(name) if sched is not None else None
    if side is None:
        return pl.pallas_call(
            body, name=name, grid=grid, in_specs=list(in_specs), out_specs=list(out_specs),
            out_shape=list(out_shape), scratch_shapes=list(scratch_shapes), compiler_params=_params(*sem))(*args)
    cuts = [len(in_specs), len(side.ins), len(out_shape), len(side.out_shape), len(scratch_shapes)]

    def with_side(*refs):
        groups, at = [], 0
        for n in cuts:
            groups.append(refs[at:at + n])
            at += n
        ins, side_ins, outs, side_outs, scratch = groups
        side_sems = refs[at:]
        first = pl.program_id(0) == 0
        last = pl.program_id(0) == grid[0] - 1
        for a in range(1, len(grid)):
            first = jnp.logical_and(first, pl.program_id(a) == 0)
            last = jnp.logical_and(last, pl.program_id(a) == grid[a] - 1)

        @pl.when(first)
        def _():
            side.start(side_ins, side_outs, side_sems)

        body(*ins, *outs, *scratch)

        @pl.when(last)
        def _():
            side.finish(side_ins, side_outs, side_sems)

    hbm = pl.BlockSpec(memory_space=pl.ANY)
    res = pl.pallas_call(
        with_side, name=name, grid=grid, in_specs=list(in_specs) + [hbm] * len(side.ins),
        out_specs=list(out_specs) + [hbm] * len(side.out_shape), out_shape=list(out_shape) + list(side.out_shape),
        scratch_shapes=list(scratch_shapes) + list(side.sems),
        compiler_params=_params(*(["arbitrary"] * len(grid))))(*args, *side.ins)
    sched.done(name, res[len(out_shape):])
    return res[:len(out_shape)]


def _rms_fwd(x, g, name):
    t, d = x.shape
    tm = _row_tile(t)

    def body(x_ref, g_ref, h_ref):
        xv = x_ref[...]
        r = lax.rsqrt(jnp.mean(xv * xv, axis=-1, keepdims=True) + RMS_EPS)
        h_ref[...] = (xv * r * g_ref[...]).astype(BF16)

    return _pallas(
        body, name=name, grid=(t // tm,),
        in_specs=[pl.BlockSpec((tm, d), lambda i: (i, 0)), pl.BlockSpec((1, d), lambda i: (0, 0))],
        out_specs=[pl.BlockSpec((tm, d), lambda i: (i, 0))], out_shape=[jax.ShapeDtypeStruct((t, d), BF16)],
        sem=("parallel",), args=(x, g))[0]


def _accumulate(ref, part, step):
    @pl.when(step == 0)
    def _():
        ref[...] = part

    @pl.when(step > 0)
    def _():
        ref[...] += part


def _mm(a, b, *, ta=False, tb=False, tm, tn, out_dtype=F32, add=None, scale=1.0, norm_g=None, norm_bwd=None, name):
    m, k = (a.shape[1], a.shape[0]) if ta else a.shape
    n = b.shape[0] if tb else b.shape[1]
    tm, tn = min(tm, m), min(tn, n)
    assert m % tm == 0 and n % tn == 0, (m, n, tm, tn)
    assert (norm_g is None and norm_bwd is None) or tn == n
    dims = (((0 if ta else 1,), (1 if tb else 0,)), ((), ()))
    n_in = 2 + (add is not None) + (norm_g is not None) + (3 if norm_bwd is not None else 0)

    def body(*refs):
        ins, outs = list(refs[2:n_in]), refs[n_in:]
        r = lax.dot_general(refs[0][...].astype(BF16), refs[1][...].astype(BF16), dims, preferred_element_type=F32)
        if scale != 1.0:
            r = r * scale
        if add is not None:
            r = r + ins.pop(0)[...]
        if norm_bwd is not None:
            xv, gv, dres = (ref[...] for ref in ins)
            rs = lax.rsqrt(jnp.mean(xv * xv, axis=-1, keepdims=True) + RMS_EPS)
            xhat = xv * rs
            gd = r * gv
            dx = dres + rs * (gd - xhat * jnp.mean(gd * xhat, axis=-1, keepdims=True))
            outs[0][...] = dx
            outs[1][...] = dx.astype(BF16)
            _accumulate(outs[2], jnp.sum(r * xhat, axis=0, keepdims=True), pl.program_id(0))
            return
        outs[0][...] = r.astype(out_dtype)
        if norm_g is not None:
            rs = lax.rsqrt(jnp.mean(r * r, axis=-1, keepdims=True) + RMS_EPS)
            outs[1][...] = (r * rs * ins.pop(0)[...]).astype(BF16)

    a_spec = pl.BlockSpec((k, tm), lambda i, j: (0, i)) if ta else pl.BlockSpec((tm, k), lambda i, j: (i, 0))
    b_spec = pl.BlockSpec((tn, k), lambda i, j: (j, 0)) if tb else pl.BlockSpec((k, tn), lambda i, j: (0, j))
    o_spec = pl.BlockSpec((tm, tn), lambda i, j: (i, j))
    vec = pl.BlockSpec((1, tn), lambda i, j: (0, j))
    args, specs = [a, b], [a_spec, b_spec]
    if add is not None:
        args.append(add)
        specs.append(o_spec)
    out_specs, out_shape = [o_spec], [jax.ShapeDtypeStruct((m, n), out_dtype)]
    if norm_g is not None:
        args.append(norm_g)
        specs.append(vec)
        out_specs.append(o_spec)
        out_shape.append(jax.ShapeDtypeStruct((m, n), BF16))
    if norm_bwd is not None:
        args += list(norm_bwd)
        specs += [o_spec, vec, o_spec]
        out_specs = [o_spec, o_spec, vec]
        out_shape = [jax.ShapeDtypeStruct((m, n), F32), jax.ShapeDtypeStruct((m, n), BF16),
                     jax.ShapeDtypeStruct((1, n), F32)]
    res = _pallas(body, name=name, grid=(m // tm, n // tn), in_specs=specs, out_specs=out_specs, out_shape=out_shape,
                  sem=("arbitrary", "arbitrary") if norm_bwd is not None else ("parallel", "parallel"), args=args)
    return res[0] if len(res) == 1 else res


def _ffn_tile(f):
    for tf in (1408, 512, 256, 128):
        if f % tf == 0:
            return tf
    raise ValueError(f)


def _ffn_fwd(h, x, wg, wu, wd, name, next_g=None, tgt=None):
    t, d = x.shape
    f = wg.shape[0]
    tm, tf = _row_tile(t), _ffn_tile(f)
    nf = f // tf
    assert (next_g is None) != (tgt is None)

    def body(h_ref, x_ref, wg_ref, wu_ref, wd_ref, tail_ref, g_ref, u_ref, o0_ref, o1_ref, *rest):
        acc_ref = rest[-1]
        j = pl.program_id(1)
        hv = h_ref[...]
        gv = lax.dot_general(hv, wg_ref[...], NT, preferred_element_type=F32)
        uv = lax.dot_general(hv, wu_ref[...], NT, preferred_element_type=F32)
        av = gv * _sigmoid(gv) * uv
        g_ref[...] = gv.astype(BF16)
        u_ref[...] = uv.astype(BF16)
        _accumulate(acc_ref, lax.dot_general(av.astype(BF16), wd_ref[...], NN, preferred_element_type=F32), j)

        @pl.when(j == nf - 1)
        def _():
            y = x_ref[...] + 0.5 * acc_ref[...]
            if tgt is None:
                o0_ref[...] = y
                rs = lax.rsqrt(jnp.mean(y * y, axis=-1, keepdims=True) + RMS_EPS)
                o1_ref[...] = (y * rs * tail_ref[...]).astype(BF16)
            else:
                e = y - tail_ref[...]
                dy = e * (1.0 / d)
                o0_ref[...] = dy
                o1_ref[...] = dy.astype(BF16)
                _accumulate(rest[0], jnp.sum(e * e, axis=0, keepdims=True), pl.program_id(0))

    row = pl.BlockSpec((tm, d), lambda i, j: (i, 0))
    hid = pl.BlockSpec((tm, tf), lambda i, j: (i, j))
    vec = pl.BlockSpec((1, d), lambda i, j: (0, 0))
    out_specs = [hid, hid, row, row] + ([vec] if tgt is not None else [])
    out_shape = [jax.ShapeDtypeStruct((t, f), BF16)] * 2 + [jax.ShapeDtypeStruct((t, d), F32),
                                                            jax.ShapeDtypeStruct((t, d), BF16)]
    if tgt is not None:
        out_shape.append(jax.ShapeDtypeStruct((1, d), F32))
    return _pallas(
        body, name=name, grid=(t // tm, nf),
        in_specs=[row, row] + [pl.BlockSpec((tf, d), lambda i, j: (j, 0))] * 3 + [vec if tgt is None else row],
        out_specs=out_specs, out_shape=out_shape, scratch_shapes=[pltpu.VMEM((tm, d), F32)],
        sem=("parallel" if tgt is None else "arbitrary", "arbitrary"),
        args=(h, x, wg, wu, wd, next_g if tgt is None else tgt))


def _ffn_bwd_mid(dy, wd, g, u, name):
    t, d = dy.shape
    f = wd.shape[0]
    tm, tf = _row_tile(t), _ffn_tile(f)

    def body(dy_ref, wd_ref, g_ref, u_ref, dg_ref, du_ref, a_ref):
        da = 0.5 * lax.dot_general(dy_ref[...].astype(BF16), wd_ref[...], NT, preferred_element_type=F32)
        gv = g_ref[...].astype(F32)
        uv = u_ref[...].astype(F32)
        s = _sigmoid(gv)
        silu = gv * s
        dg_ref[...] = (da * uv * (s * (1.0 + gv * (1.0 - s)))).astype(BF16)
        du_ref[...] = (da * silu).astype(BF16)
        a_ref[...] = (silu * uv).astype(BF16)

    hid = pl.BlockSpec((tm, tf), lambda i, j: (i, j))
    return _pallas(
        body, name=name, grid=(t // tm, f // tf),
        in_specs=[pl.BlockSpec((tm, d), lambda i, j: (i, 0)), pl.BlockSpec((tf, d), lambda i, j: (j, 0)), hid, hid],
        out_specs=[hid, hid, hid], out_shape=[jax.ShapeDtypeStruct((t, f), BF16)] * 3,
        sem=("parallel", "parallel"), args=(dy, wd, g, u))


def _rel_index(t, s_band):
    return jnp.clip(t + KPAD - s_band, -REL_CLIP, REL_CLIP) + REL_CLIP


def _bias_expand(rel_bias_pad):
    nh = rel_bias_pad.shape[0]

    def body(rb_ref, out_ref):
        rb = rb_ref[...]
        i_io = lax.broadcasted_iota(jnp.int32, (N_REL_PAD, BAND), 0)
        s_io = lax.broadcasted_iota(jnp.int32, (N_REL_PAD, BAND), 1)

        def row(t, carry):
            onehot = (i_io == _rel_index(t, s_io)).astype(F32)
            out_ref[t] = _dot_exact01_r(rb, onehot)
            return carry

        lax.fori_loop(0, CHUNK, row, 0)

    return _pallas(
        body, name="bias_expand", grid=(1,), in_specs=[pl.BlockSpec(rel_bias_pad.shape, lambda i: (0, 0))],
        out_specs=[pl.BlockSpec((CHUNK, nh, BAND), lambda i: (0, 0, 0))],
        out_shape=[jax.ShapeDtypeStruct((CHUNK, nh, BAND), F32)], sem=("arbitrary",), args=(rel_bias_pad,))[0]


def _bias_fold(dbias):
    ng, nh = dbias.shape[0], dbias.shape[2]

    def body(db_ref, out_ref):
        s_io = lax.broadcasted_iota(jnp.int32, (BAND, N_REL_PAD), 0)
        i_io = lax.broadcasted_iota(jnp.int32, (BAND, N_REL_PAD), 1)

        def row(t, acc):
            onehot = (i_io == _rel_index(t, s_io)).astype(F32)
            d = db_ref[0, t]
            for gi in range(1, ng):
                d = d + db_ref[gi, t]
            return acc + _dot_exact01_r(d, onehot)

        out_ref[...] = lax.fori_loop(0, CHUNK, row, jnp.zeros((nh, N_REL_PAD), F32))

    return _pallas(
        body, name="bias_fold", grid=(1,), in_specs=[pl.BlockSpec(dbias.shape, lambda i: (0, 0, 0, 0))],
        out_specs=[pl.BlockSpec((nh, N_REL_PAD), lambda i: (0, 0))],
        out_shape=[jax.ShapeDtypeStruct((nh, N_REL_PAD), F32)], sem=("arbitrary",), args=(dbias,))[0]


def _left_half(shape):
    return lax.broadcasted_iota(jnp.int32, shape, len(shape) - 1) < ATTN_HEAD_DIM


def _stack_heads(v):
    left = _left_half(v.shape)
    zero = jnp.zeros_like(v)
    return jnp.concatenate([jnp.where(left, v, zero), jnp.where(left, zero, v)], axis=0)


def _unstack_heads(v):
    return jnp.where(_left_half((CHUNK, 128)), v[0:CHUNK, :], v[CHUNK:2 * CHUNK, :])


def _half_mean(v):
    r = lax.broadcasted_iota(jnp.int32, (128, 128), 0) < ATTN_HEAD_DIM
    c = lax.broadcasted_iota(jnp.int32, (128, 128), 1) < ATTN_HEAD_DIM
    return _dot_exact01_r(v, r == c) * (1.0 / ATTN_HEAD_DIM)


def _attn_prepare(q_ref, k_ref, v_ref, gq_ref, gk_ref, qs_scr, k_scr, v_scr):
    q, k = q_ref[...], k_ref[...]
    rq = lax.rsqrt(_half_mean(q * q) + RMS_EPS)
    rk = lax.rsqrt(_half_mean(k * k) + RMS_EPS)
    qhat, khat = q * rq, k * rk
    qs_scr[...] = (qhat * gq_ref[...] * ATTN_HEAD_DIM ** -0.5).astype(BF16)
    k_scr[0:KPAD, :] = jnp.zeros((KPAD, 128), BF16)
    v_scr[0:KPAD, :] = jnp.zeros((KPAD, 128), BF16)
    k_scr[KPAD:, :] = (khat * gk_ref[...]).astype(BF16)
    v_scr[KPAD:, :] = v_ref[...].astype(BF16)
    return qhat, rq, khat, rk


def _attn_scores(qs_scr, k_scr, bias_ref, c):
    r0 = pl.multiple_of(c * CHUNK, CHUNK)
    qst = _stack_heads(qs_scr[pl.ds(r0, CHUNK), :])
    kb = k_scr[pl.ds(r0, BAND), :]
    s = lax.dot_general(qst, kb, NT, preferred_element_type=F32) + bias_ref[...]
    col = lax.broadcasted_iota(jnp.int32, (2 * CHUNK, BAND), 1)
    first = jnp.maximum(CHUNK, (LEFT_CHUNKS + 1 - c) * CHUNK)
    s = jnp.where(col >= first, s, -jnp.inf)
    e = jnp.exp(s - jnp.max(s, axis=-1, keepdims=True))
    return e, 1.0 / jnp.sum(e, axis=-1, keepdims=True), qst, kb, r0


def _attn_fwd(proj, bias, gq, gk, nb, seq):
    nc = seq // CHUNK

    def body(q_ref, k_ref, v_ref, bias_ref, gq_ref, gk_ref, o_ref, qs_scr, k_scr, v_scr):
        _attn_prepare(q_ref, k_ref, v_ref, gq_ref, gk_ref, qs_scr, k_scr, v_scr)

        def chunk(c, carry):
            e, inv, _, _, r0 = _attn_scores(qs_scr, k_scr, bias_ref, c)
            vb = v_scr[pl.ds(r0, BAND), :]
            o_ref[pl.ds(r0, CHUNK), :] = _unstack_heads(
                lax.dot_general(e.astype(BF16), vb, NN, preferred_element_type=F32) * inv)
            return carry

        lax.fori_loop(0, nc, chunk, 0, unroll=ATTN_UNROLL)

    def col(off):
        return pl.BlockSpec((seq, 128), lambda b, hp: (b, off + hp))

    vec = pl.BlockSpec((1, 128), lambda b, hp: (0, 0))
    return _pallas(
        body, name="attn_fwd", grid=(nb, ATTN_HEADS // 2),
        in_specs=[col(0), col(4), col(8), pl.BlockSpec((2 * CHUNK, BAND), lambda b, hp: (hp, 0)), vec, vec],
        out_specs=[pl.BlockSpec((seq, 128), lambda b, hp: (b, hp))],
        out_shape=[jax.ShapeDtypeStruct((nb * seq, ATTN_WIDTH), F32)],
        scratch_shapes=[pltpu.VMEM((seq, 128), BF16), pltpu.VMEM((seq + KPAD, 128), BF16),
                        pltpu.VMEM((seq + KPAD, 128), BF16)],
        sem=("parallel", "parallel"), args=(proj, proj, proj, bias, gq, gk))[0]


def _attn_bwd(proj, out, dout, bias, gq, gk, nb, seq):
    nc = seq // CHUNK
    scale = ATTN_HEAD_DIM ** -0.5

    def body(q_ref, k_ref, v_ref, o_ref, do_ref, bias_ref, gq_ref, gk_ref,
             dq_ref, dk_ref, dv_ref, dbias_ref, dgq_ref, dgk_ref,
             qs_scr, k_scr, v_scr, dqn_scr, dk_scr, dv_scr, db_scr):
        qhat, rq, khat, rk = _attn_prepare(q_ref, k_ref, v_ref, gq_ref, gk_ref, qs_scr, k_scr, v_scr)
        dk_scr[...] = jnp.zeros_like(dk_scr)
        dv_scr[...] = jnp.zeros_like(dv_scr)
        db_scr[...] = jnp.zeros_like(db_scr)

        def chunk(c, carry):
            e, inv, qst, kb, r0 = _attn_scores(qs_scr, k_scr, bias_ref, c)
            p = e * inv
            vb = v_scr[pl.ds(r0, BAND), :]
            do_c = do_ref[pl.ds(r0, CHUNK), :]
            dost = _stack_heads(do_c)
            drow = jnp.sum(dost * _stack_heads(o_ref[pl.ds(r0, CHUNK), :]), axis=-1, keepdims=True)
            dp = lax.dot_general(dost.astype(BF16), vb, NT, preferred_element_type=F32)
            ds = p * (dp - drow)
            db_scr[...] += ds
            dqn_scr[pl.ds(r0, CHUNK), :] = scale * _unstack_heads(
                lax.dot_general(ds.astype(BF16), kb, NN, preferred_element_type=F32))
            dk_scr[pl.ds(r0, BAND), :] += lax.dot_general(ds.T.astype(BF16), qst, NN, preferred_element_type=F32)
            dv_scr[pl.ds(r0, BAND), :] += _dot(p.T, dost)
            return carry

        lax.fori_loop(0, nc, chunk, 0, unroll=ATTN_UNROLL)

        def norm_bwd(dn, hat, r, g_ref):
            gd = dn * g_ref[...]
            return r * (gd - hat * _half_mean(gd * hat)), jnp.sum(dn * hat, axis=0, keepdims=True)

        dq, dgq = norm_bwd(dqn_scr[...], qhat, rq, gq_ref)
        dk, dgk = norm_bwd(dk_scr[KPAD:, :], khat, rk, gk_ref)
        dq_ref[...] = dq.astype(BF16)
        dk_ref[...] = dk.astype(BF16)
        dv_ref[...] = dv_scr[KPAD:, :].astype(BF16)
        dbias_ref[0] = db_scr[...]
        dgq_ref[0] = dgq
        dgk_ref[0] = dgk

    def col(off):
        return pl.BlockSpec((seq, 128), lambda b, hp: (b, off + hp))

    vec = pl.BlockSpec((1, 128), lambda b, hp: (0, 0))
    gvec = pl.BlockSpec((1, 1, 128), lambda b, hp: (b * (ATTN_HEADS // 2) + hp, 0, 0))
    t = nb * seq
    return _pallas(
        body, name="attn_bwd", grid=(nb, ATTN_HEADS // 2),
        in_specs=[col(0), col(4), col(8), col(0), col(0),
                  pl.BlockSpec((2 * CHUNK, BAND), lambda b, hp: (hp, 0)), vec, vec],
        out_specs=[col(0), col(0), col(0), pl.BlockSpec((1, 2 * CHUNK, BAND), lambda b, hp: (b, hp, 0)),
                   gvec, gvec],
        out_shape=[jax.ShapeDtypeStruct((t, ATTN_WIDTH), BF16)] * 3
        + [jax.ShapeDtypeStruct((nb, ATTN_HEADS * CHUNK, BAND), F32)]
        + [jax.ShapeDtypeStruct((nb * ATTN_HEADS // 2, 1, 128), F32)] * 2,
        scratch_shapes=[pltpu.VMEM((seq, 128), BF16), pltpu.VMEM((seq + KPAD, 128), BF16),
                        pltpu.VMEM((seq + KPAD, 128), BF16), pltpu.VMEM((seq, 128), F32),
                        pltpu.VMEM((seq + KPAD, 128), F32), pltpu.VMEM((seq + KPAD, 128), F32),
                        pltpu.VMEM((2 * CHUNK, BAND), F32)],
        sem=("parallel", "parallel"), args=(proj, proj, proj, out, dout, bias, gq, gk))


def _tri(lower):
    r = lax.broadcasted_iota(jnp.int32, (CHUNK, CHUNK), 0)
    c = lax.broadcasted_iota(jnp.int32, (CHUNK, CHUNK), 1)
    return (r >= c) if lower else (r <= c)


def _hgrn_gates(hq, hf, lb):
    sq = _sigmoid(hq)
    sf = _sigmoid(hf)
    return hq * sq, sq, sf, lb + (1.0 - lb) * sf


def _hgrn_offdiag(q_s, k_s, b_s):
    row = lax.broadcasted_iota(jnp.int32, (CHUNK, HGRN_HEAD_DIM), 0)
    bv, qv, kv = b_s[...], q_s[...], k_s[...]
    eqs, eks = [], []
    for i in range(1, N_SUB):
        r = b_s[pl.ds(SUB * i - 1, 1), :]
        in_i = (row >= SUB * i) & (row < SUB * (i + 1))
        eqs.append(jnp.exp(jnp.where(in_i, bv - r, -jnp.inf)))
        eks.append(jnp.exp(jnp.where(row < SUB * i, r - bv, -jnp.inf)))
    eq = jnp.concatenate(eqs, axis=1)
    ek = jnp.concatenate(eks, axis=1)
    qt = jnp.concatenate([qv] * (N_SUB - 1), axis=1) * eq
    kt = jnp.concatenate([kv] * (N_SUB - 1), axis=1) * ek
    return qt, kt, eq, ek


def _hgrn_diag_e(b_s, i, s):
    t_io = lax.broadcasted_iota(jnp.int32, (SUB, HGRN_HEAD_DIM), 0)
    bi = b_s[pl.ds(SUB * i, SUB), :]
    return jnp.exp(jnp.where(t_io >= s, bi - b_s[pl.ds(SUB * i + s, 1), :], -jnp.inf)), t_io


def _hgrn_intra(q_s, k_s, b_s, a_s, qt, kt):
    ktp = jnp.concatenate([kt, jnp.zeros_like(kt)], axis=0)
    a_s[...] = _dot(qt, ktp, NT)
    col = lax.broadcasted_iota(jnp.int32, (SUB, HGRN_HEAD_DIM), 1)
    for i in range(N_SUB):
        qi = q_s[pl.ds(SUB * i, SUB), :]
        ai = jnp.zeros((SUB, HGRN_HEAD_DIM), F32)
        for s in range(SUB):
            e, _ = _hgrn_diag_e(b_s, i, s)
            a_col = jnp.sum(qi * k_s[pl.ds(SUB * i + s, 1), :] * e, axis=-1, keepdims=True)
            ai = ai + jnp.where(col == SUB * i + s, a_col, 0.0)
        a_s[pl.ds(SUB * i, SUB), :] += ai


def _hgrn_fwd(proj, lb, go, nb, seq):
    nc = seq // CHUNK
    hd = HGRN_HEAD_DIM

    def body(hq_ref, hf_ref, hi_ref, hg_ref, lb_ref, go_ref, y_ref, o_ref, st_ref, a_ref,
             st_all, q_all, k_all, b_all, a_all):
        st_all[...] = jnp.zeros_like(st_all)
        lower = _tri(True)

        def head_chunk(hh, c, rows):
            ln = slice(hd * hh, hd * (hh + 1))
            st, q_s, k_s, b_s, a_s = st_all.at[hh], q_all.at[hh], k_all.at[hh], b_all.at[hh], a_all.at[hh]
            q, _, _, f = _hgrn_gates(hq_ref[rows, ln], hf_ref[rows, ln], lb_ref[:, ln])
            v = hi_ref[rows, ln]
            b = _dot_exact01(lower, jnp.log(f))
            q_s[...] = q
            k_s[...] = 1.0 - f
            b_s[...] = b
            st_ref[hh, c] = st[...]
            qt, kt, _, _ = _hgrn_offdiag(q_s, k_s, b_s)
            _hgrn_intra(q_s, k_s, b_s, a_s, qt, kt)
            a16 = a_s[...].astype(BF16)
            a_ref[hh, c] = a16
            vp = jnp.concatenate([v, jnp.zeros_like(v)], axis=0)
            o = _dot(a16, vp) + _dot(q * jnp.exp(b), st[...], NT)
            bl = b_s[pl.ds(CHUNK - 1, 1), :]
            st[...] = st[...] * jnp.exp(bl) + _tn(v, (1.0 - f) * jnp.exp(bl - b))
            o_ref[rows, ln] = o
            n = o * lax.rsqrt(jnp.mean(o * o, axis=-1, keepdims=True) + RMS_EPS) * go_ref[...]
            hg = hg_ref[rows, ln]
            y_ref[rows, ln] = n * hg * _sigmoid(hg)

        def chunk(c, carry):
            rows = pl.ds(pl.multiple_of(c * CHUNK, CHUNK), CHUNK)
            for hh in range(HGRN_PER_STEP):
                head_chunk(hh, c, rows)
            return carry

        lax.fori_loop(0, nc, chunk, 0)

    hp, wide = HGRN_PER_STEP, HGRN_PER_STEP * hd

    def col(off):
        return pl.BlockSpec((seq, wide), lambda b, h: (b, off // hp + h))

    out = pl.BlockSpec((seq, wide), lambda b, h: (b, h))
    t = nb * seq
    return pl.pallas_call(
        body, name="hgrn_fwd", grid=(nb, HGRN_HEADS // hp),
        in_specs=[col(12), col(16), col(20), col(24), pl.BlockSpec((1, wide), lambda b, h: (0, h)),
                  pl.BlockSpec((1, hd), lambda b, h: (0, 0))],
        out_specs=[out, out, pl.BlockSpec((hp, nc, hd, hd), lambda b, h: (b * (HGRN_HEADS // hp) + h, 0, 0, 0)),
                   pl.BlockSpec((hp, nc, CHUNK, hd), lambda b, h: (b * (HGRN_HEADS // hp) + h, 0, 0, 0))],
        out_shape=[jax.ShapeDtypeStruct((t, HGRN_HEADS * hd), F32)] * 2
        + [jax.ShapeDtypeStruct((nb * HGRN_HEADS, nc, hd, hd), F32),
           jax.ShapeDtypeStruct((nb * HGRN_HEADS, nc, CHUNK, hd), BF16)],
        scratch_shapes=[pltpu.VMEM((hp, hd, hd), F32)] + [pltpu.VMEM((hp, CHUNK, hd), F32)] * 4,
        compiler_params=_params("parallel", "parallel"),
    )(proj, proj, proj, proj, lb, go)


def _hgrn_bwd(proj, lb, go, o_pre, states, scores, dout, nb, seq):
    nc = seq // CHUNK
    hd = HGRN_HEAD_DIM

    def body(hq_ref, hf_ref, hi_ref, hg_ref, lb_ref, go_ref, o_ref, st_ref, a_ref, dy_ref,
             dhq_ref, dhf_ref, dhi_ref, dhg_ref, dlb_ref, dgo_ref,
             dst_all, q_all, k_all, b_all, da_all, dqi_all, dki_all, dlb_all, dgo_all):
        dst_all[...] = jnp.zeros_like(dst_all)
        dlb_all[...] = jnp.zeros_like(dlb_all)
        dgo_all[...] = jnp.zeros_like(dgo_all)
        lower, upper = _tri(True), _tri(False)
        gov = go_ref[...]
        row = lax.broadcasted_iota(jnp.int32, (CHUNK, hd), 0)

        def head_chunk(hh, c, rows):
            ln = slice(hd * hh, hd * (hh + 1))
            dst, q_s, k_s, b_s = dst_all.at[hh], q_all.at[hh], k_all.at[hh], b_all.at[hh]
            da_s, dqi_s, dki_s = da_all.at[hh], dqi_all.at[hh], dki_all.at[hh]
            dlb_acc, dgo_acc = dlb_all.at[hh], dgo_all.at[hh]
            lbv = lb_ref[:, ln]
            hq, hf, v, hg = hq_ref[rows, ln], hf_ref[rows, ln], hi_ref[rows, ln], hg_ref[rows, ln]
            q, sq, sf, f = _hgrn_gates(hq, hf, lbv)
            kk = 1.0 - f
            b = _dot_exact01(lower, jnp.log(f))
            q_s[...] = q
            k_s[...] = kk
            b_s[...] = b
            bl = b_s[pl.ds(CHUNK - 1, 1), :]
            ebl = jnp.exp(bl)
            ekd = jnp.exp(bl - b)
            kd = kk * ekd
            eb = jnp.exp(b)
            qb = q * eb
            st0 = st_ref[hh, c]
            dst1 = dst[...]

            o = o_ref[rows, ln]
            dy = dy_ref[rows, ln]
            sg = _sigmoid(hg)
            rstd = lax.rsqrt(jnp.mean(o * o, axis=-1, keepdims=True) + RMS_EPS)
            ohat = o * rstd
            dn = dy * hg * sg
            dhg_ref[rows, ln] = (dy * ohat * gov * (sg * (1.0 + hg * (1.0 - sg)))).astype(BF16)
            dgo_acc[...] += jnp.sum(dn * ohat, axis=0, keepdims=True)
            gdn = dn * gov
            do = rstd * (gdn - ohat * jnp.mean(gdn * ohat, axis=-1, keepdims=True))

            qt, kt, eq, ek = _hgrn_offdiag(q_s, k_s, b_s)
            da = _dot(do, v, NT)
            dat = _dot(v, do, NT)
            da_s[...] = da
            dqo = _dot(da, kt) * eq
            dko = _dot(dat, qt) * ek
            dqi_s[...] = dqo[:, 0:hd] + dqo[:, hd:2 * hd] + dqo[:, 2 * hd:3 * hd]
            dki_s[...] = dko[:, 0:hd] + dko[:, hd:2 * hd] + dko[:, 2 * hd:3 * hd]
            col = lax.broadcasted_iota(jnp.int32, (SUB, CHUNK), 1)
            for i in range(N_SUB):
                qi = q_s[pl.ds(SUB * i, SUB), :]
                dai = da_s[pl.ds(SUB * i, SUB), :]
                dqd = jnp.zeros((SUB, hd), F32)
                dkd_ = jnp.zeros((SUB, hd), F32)
                for s in range(SUB):
                    e, t_io = _hgrn_diag_e(b_s, i, s)
                    dacol = jnp.sum(jnp.where(col == SUB * i + s, dai, 0.0), axis=-1, keepdims=True)
                    w = dacol * e
                    dqd = dqd + w * k_s[pl.ds(SUB * i + s, 1), :]
                    dkd_ = dkd_ + jnp.where(t_io == s, jnp.sum(w * qi, axis=0, keepdims=True), 0.0)
                dqi_s[pl.ds(SUB * i, SUB), :] += dqd
                dki_s[pl.ds(SUB * i, SUB), :] += dkd_
            dqi, dki = dqi_s[...], dki_s[...]

            dv = _tn(a_ref[hh, c].astype(F32), do)[0:CHUNK, :] + _dot(kd, dst1, NT)
            dqb = _dot(do, st0)
            dkd = _dot(v, dst1)
            t2 = dkd * kd
            dq = dqb * eb + dqi
            dk = dkd * ekd + dki
            dbl = jnp.sum(t2, axis=0, keepdims=True) + ebl * jnp.sum(st0 * dst1, axis=0, keepdims=True)
            db = dqb * qb - t2 + q * dqi - kk * dki + jnp.where(row == CHUNK - 1, dbl, 0.0)
            dg = _dot_exact01(upper, db)
            dst[...] = dst1 * ebl + _tn(do, qb)

            df = dg / f - dk
            dhf_ref[rows, ln] = (df * (1.0 - lbv) * sf * (1.0 - sf)).astype(BF16)
            dlb_acc[...] += jnp.sum(df * (1.0 - sf), axis=0, keepdims=True)
            dhq_ref[rows, ln] = (dq * (sq * (1.0 + hq * (1.0 - sq)))).astype(BF16)
            dhi_ref[rows, ln] = dv.astype(BF16)

        def chunk(it, carry):
            c = nc - 1 - it
            rows = pl.ds(pl.multiple_of(c * CHUNK, CHUNK), CHUNK)
            for hh in range(HGRN_PER_STEP):
                head_chunk(hh, c, rows)
            return carry

        lax.fori_loop(0, nc, chunk, 0)
        dlb_ref[...] = dlb_all[...]
        dgo_ref[...] = dgo_all[...]

    hp, wide = HGRN_PER_STEP, HGRN_PER_STEP * hd

    def col(off):
        return pl.BlockSpec((seq, wide), lambda b, h: (b, off // hp + h))

    out = pl.BlockSpec((seq, wide), lambda b, h: (b, h))
    part = pl.BlockSpec((hp, 1, hd), lambda b, h: (b * (HGRN_HEADS // hp) + h, 0, 0))
    t = nb * seq
    return pl.pallas_call(
        body, name="hgrn_bwd", grid=(nb, HGRN_HEADS // hp),
        in_specs=[col(12), col(16), col(20), col(24), pl.BlockSpec((1, wide), lambda b, h: (0, h)),
                  pl.BlockSpec((1, hd), lambda b, h: (0, 0)), out,
                  pl.BlockSpec((hp, nc, hd, hd), lambda b, h: (b * (HGRN_HEADS // hp) + h, 0, 0, 0)),
                  pl.BlockSpec((hp, nc, CHUNK, hd), lambda b, h: (b * (HGRN_HEADS // hp) + h, 0, 0, 0)), col(4)],
        out_specs=[out, out, out, out, part, part],
        out_shape=[jax.ShapeDtypeStruct((t, HGRN_HEADS * hd), BF16)] * 4
        + [jax.ShapeDtypeStruct((nb * HGRN_HEADS, 1, hd), F32)] * 2,
        scratch_shapes=[pltpu.VMEM((hp, hd, hd), F32)] + [pltpu.VMEM((hp, CHUNK, hd), F32)] * 3
        + [pltpu.VMEM((hp, CHUNK, CHUNK), F32)] + [pltpu.VMEM((hp, CHUNK, hd), F32)] * 2
        + [pltpu.VMEM((hp, 1, hd), F32)] * 2,
        compiler_params=_params("parallel", "parallel"),
    )(proj, proj, proj, proj, lb, go, o_pre, states, scores, dout)


def _lb_fwd(lower_bounds):
    def body(x_ref, o_ref):
        xv = x_ref[...]
        e = jnp.exp(xv - jnp.max(xv, axis=0, keepdims=True))
        o_ref[...] = e[0:1, :] / jnp.sum(e, axis=0, keepdims=True)

    return pl.pallas_call(body, name="lb_fwd",
                          out_shape=jax.ShapeDtypeStruct((1, lower_bounds.shape[1]), F32))(lower_bounds)


def _lb_bwd(lower_bounds, dlb_parts):
    ng = dlb_parts.shape[0]

    def body(x_ref, d_ref, o_ref):
        xv = x_ref[...]
        e = jnp.exp(xv - jnp.max(xv, axis=0, keepdims=True))
        p = e / jnp.sum(e, axis=0, keepdims=True)
        dlb = d_ref[0]
        for gi in range(1, ng):
            dlb = dlb + d_ref[gi]
        first = lax.broadcasted_iota(jnp.int32, xv.shape, 0) == 0
        o_ref[...] = p * (jnp.where(first, dlb, 0.0) - p[0:1, :] * dlb)

    return pl.pallas_call(body, name="lb_bwd",
                          out_shape=jax.ShapeDtypeStruct(lower_bounds.shape, F32))(lower_bounds, dlb_parts)


def _ffn_bwd(x, g, h, gate, up, dy, dy16, w, put, tag):
    wg, wu, wd = w[tag + "_w_gate"], w[tag + "_w_up"], w[tag + "_w_down"]
    dgate, dup, act = _ffn_bwd_mid(dy16, wd, gate, up, tag + "_bwd_mid")
    put(tag + "_w_down", _mm(act, dy16, ta=True, tm=1408, tn=512, scale=0.5, name=tag + "_dwd"))
    put(tag + "_w_gate", _mm(dgate, h, ta=True, tm=1408, tn=512, name=tag + "_dwg"))
    put(tag + "_w_up", _mm(dup, h, ta=True, tm=1408, tn=512, name=tag + "_dwu"))
    dh = _mm(dgate, wg, tm=512, tn=1024, name=tag + "_dh_gate")
    return _mm(dup, wu, tm=512, tn=1024, add=dh, norm_bwd=(x, g, dy), name=tag + "_dh_up")


def _local_step(x, tgt, sp, w, put, nb, seq):
    d = x.shape[1]
    h1 = _rms_fwd(x, sp["ffn1_norm_g"], "ffn1_norm")
    rb_pad = jnp.pad(sp["attn_rel_bias"], ((0, 0), (0, N_REL_PAD - N_REL)))
    bias = jnp.transpose(_bias_expand(rb_pad), (1, 0, 2)).reshape(ATTN_HEADS * CHUNK, BAND)
    gq2 = jnp.concatenate([sp["attn_q_norm_g"]] * 2, axis=1)
    gk2 = jnp.concatenate([sp["attn_k_norm_g"]] * 2, axis=1)
    lb = _lb_fwd(sp["hgrn_lower_bounds"])
    gate1, up1, x1, h2 = _ffn_fwd(h1, x, w["ffn1_w_gate"], w["ffn1_w_up"], w["ffn1_w_down"], "ffn1_fwd",
                                  next_g=sp["mix_norm_g"])
    proj = _mm(h2, w["w_in"], tb=True, tm=256, tn=w["w_in"].shape[0], name="in_proj")
    attn = _attn_fwd(proj, bias, gq2, gk2, nb, seq)
    hy, ho, hstate, hscore = _hgrn_fwd(proj, lb, sp["hgrn_out_norm_g"], nb, seq)
    mix = jnp.concatenate([attn, hy], axis=1)
    x2, h3 = _mm(mix, w["w_out"], tm=512, tn=1024, add=x1, norm_g=sp["ffn2_norm_g"], name="out_proj")
    gate2, up2, dx3, dx3_16, sq = _ffn_fwd(h3, x2, w["ffn2_w_gate"], w["ffn2_w_up"], w["ffn2_w_down"], "ffn2_fwd",
                                           tgt=tgt)
    loss = 0.5 * jnp.sum(sq) / d

    dx2, dx2_16, dg3 = _ffn_bwd(x2, sp["ffn2_norm_g"], h3, gate2, up2, dx3, dx3_16, w, put, "ffn2")
    dmix = _mm(dx2_16, w["w_out"], tb=True, tm=512, tn=1024, name="out_proj_dx")
    put("w_out", _mm(mix, dx2_16, ta=True, tm=512, tn=1024, name="out_proj_dw"))
    dq, dk, dv, dbias, dgq, dgk = _attn_bwd(proj, attn, dmix, bias, gq2, gk2, nb, seq)
    dbias = jnp.transpose(dbias.reshape(nb, ATTN_HEADS, CHUNK, BAND), (0, 2, 1, 3))
    dgq = jnp.sum(dgq, axis=(0, 1)).reshape(2, ATTN_HEAD_DIM).sum(axis=0, keepdims=True)
    dgk = jnp.sum(dgk, axis=(0, 1)).reshape(2, ATTN_HEAD_DIM).sum(axis=0, keepdims=True)
    dhq, dhf, dhi, dhg, dlb, dgo = _hgrn_bwd(proj, lb, sp["hgrn_out_norm_g"], ho, hstate, hscore, dmix, nb, seq)
    dproj = jnp.concatenate([dq, dk, dv, dhq, dhf, dhi, dhg], axis=1)
    put("w_in", _mm(dproj, h2, ta=True, tm=512, tn=1024, name="in_proj_dw"))
    dx1, dx1_16, dgm = _mm(dproj, w["w_in"], tm=512, tn=1024, norm_bwd=(x1, sp["mix_norm_g"], dx2),
                           name="in_proj_dx")
    dx0, _, dg1 = _ffn_bwd(x, sp["ffn1_norm_g"], h1, gate1, up1, dx1, dx1_16, w, put, "ffn1")

    small = {
        "ffn1_norm_g": dg1, "mix_norm_g": dgm, "ffn2_norm_g": dg3,
        "attn_q_norm_g": dgq, "attn_k_norm_g": dgk,
        "attn_rel_bias": _bias_fold(dbias)[:, :N_REL],
        "hgrn_lower_bounds": _lb_bwd(sp["hgrn_lower_bounds"], dlb.reshape(nb, 1, HGRN_HEADS * HGRN_HEAD_DIM)),
        "hgrn_out_norm_g": jnp.sum(dgo, axis=(0, 1))[None, :],
    }
    return loss, dx0, small


MESH = pl.DeviceIdType.MESH
ANY = pl.BlockSpec(memory_space=pl.ANY)


def _coords():
    return lax.axis_index("x"), lax.axis_index("y"), lax.axis_index("c")


def _other_chips(x, y):
    return [(1 - x, y), (x, 1 - y), (1 - x, 1 - y)]


def _gather_side(shards):
    n = len(shards)

    def copies(ins, outs, sems):
        send_sems, recv_sems, local_sems = sems
        x, y, c = _coords()
        me, sibling = (x, y, c), (x, y, 1 - c)
        chips = _other_chips(x, y)

        def copy(i, k, block, to, src=None):
            bx, by, bc = block
            dst = outs[i].at[4 * bx + 2 * by + bc]
            return pltpu.make_async_remote_copy(
                src_ref=dst if src is None else src, dst_ref=dst, send_sem=send_sems.at[i, k],
                recv_sem=recv_sems.at[i, k], device_id=to, device_id_type=MESH)

        mine = [pltpu.make_async_copy(ins[i], outs[i].at[4 * x + 2 * y + c], local_sems.at[i]) for i in range(n)]
        own = []
        for i in range(n):
            own.append(copy(i, 0, me, sibling, src=ins[i]))
            own += [copy(i, 1 + j, me, (*chip, c), src=ins[i]) for j, chip in enumerate(chips)]
        return copy, mine, own, me, sibling, chips, c

    def start(ins, outs, sems):
        _, mine, own, *_ = copies(ins, outs, sems)
        for cp in mine + own:
            cp.start()

    def finish(ins, outs, sems):
        copy, mine, own, me, sibling, chips, c = copies(ins, outs, sems)
        passed = []
        for i in range(n):
            for j, chip in enumerate(chips):
                copy(i, 1 + j, (*chip, c), me).wait_recv()
                passed.append(copy(i, 4 + j, (*chip, c), sibling))
                passed[-1].start()
        for i in range(n):
            copy(i, 0, sibling, me).wait_recv()
            for j, chip in enumerate(chips):
                copy(i, 4 + j, (*chip, 1 - c), me).wait_recv()
        for cp in own + passed:
            cp.wait_send()
        for cp in mine:
            cp.wait()

    return _Side(list(shards), [jax.ShapeDtypeStruct((N_DEV,) + s.shape, s.dtype) for s in shards],
                 [pltpu.SemaphoreType.DMA((n, 7)), pltpu.SemaphoreType.DMA((n, 7)), pltpu.SemaphoreType.DMA((n,))],
                 start, finish)


def _pair_side(grads):
    n = len(grads)

    def copies(ins, outs, sems):
        send_sems, recv_sems = sems
        x, y, c = _coords()
        return [pltpu.make_async_remote_copy(
            src_ref=ins[i].at[2 * k + 1 - c], dst_ref=outs[i].at[k], send_sem=send_sems.at[i, k],
            recv_sem=recv_sems.at[i, k], device_id=(x, y, 1 - c), device_id_type=MESH)
            for i in range(n) for k in range(4)]

    def start(ins, outs, sems):
        for cp in copies(ins, outs, sems):
            cp.start()

    def finish(ins, outs, sems):
        for cp in copies(ins, outs, sems):
            cp.wait()

    return _Side(list(grads), [jax.ShapeDtypeStruct((4,) + g.shape[1:], g.dtype) for g in grads],
                 [pltpu.SemaphoreType.DMA((n, 4)), pltpu.SemaphoreType.DMA((n, 4))], start, finish)


def _pair_add(grad, recv, core, name):
    _, r, cdim = grad.shape

    def body(c_ref, g_ref, r_ref, o_ref):
        o_ref[...] = (g_ref[...] + r_ref[...]).astype(BF16)

    blk = (1, r, cdim)
    return pl.pallas_call(
        body, name=name,
        grid_spec=pltpu.PrefetchScalarGridSpec(
            num_scalar_prefetch=1, grid=(4,),
            in_specs=[pl.BlockSpec(blk, lambda k, c_ref: (2 * k + c_ref[0], 0, 0)),
                      pl.BlockSpec(blk, lambda k, c_ref: (k, 0, 0))],
            out_specs=pl.BlockSpec(blk, lambda k, c_ref: (k, 0, 0))),
        out_shape=jax.ShapeDtypeStruct((4, r, cdim), BF16),
        compiler_params=_params("arbitrary"),
    )(core, grad, recv)


def _chip_side(parts):
    n = len(parts)

    def copies(ins, outs, sems):
        send_sems, recv_sems, local_sems = sems
        x, y, c = _coords()
        chips = _other_chips(x, y)
        mine = [pltpu.make_async_copy(ins[i].at[2 * x + y], outs[i].at[2 * x + y], local_sems.at[i])
                for i in range(n)]
        sent = [pltpu.make_async_remote_copy(
            src_ref=ins[i].at[2 * px + py], dst_ref=outs[i].at[2 * x + y], send_sem=send_sems.at[i, j],
            recv_sem=recv_sems.at[i, j], device_id=(px, py, c), device_id_type=MESH)
            for i in range(n) for j, (px, py) in enumerate(chips)]
        return mine, sent, chips, c

    def start(ins, outs, sems):
        mine, sent, _, _ = copies(ins, outs, sems)
        for cp in mine + sent:
            cp.start()

    def finish(ins, outs, sems):
        mine, sent, chips, c = copies(ins, outs, sems)
        send_sems, recv_sems, _ = sems
        for i in range(n):
            for j, (px, py) in enumerate(chips):
                landed = outs[i].at[2 * px + py]
                pltpu.make_async_remote_copy(
                    src_ref=landed, dst_ref=landed, send_sem=send_sems.at[i, j], recv_sem=recv_sems.at[i, j],
                    device_id=(px, py, c), device_id_type=MESH).wait_recv()
        for cp in sent:
            cp.wait_send()
        for cp in mine:
            cp.wait()

    return _Side(list(parts), [jax.ShapeDtypeStruct(p.shape, p.dtype) for p in parts],
                 [pltpu.SemaphoreType.DMA((n, 3)), pltpu.SemaphoreType.DMA((n, 3)), pltpu.SemaphoreType.DMA((n,))],
                 start, finish)


def _all_reduce_small(v):
    r = v.shape[0]

    def body(v_ref, o_ref, buf, send_sems, recv_sems):
        x, y, c = _coords()
        me = 4 * x + 2 * y + c
        buf[me] = v_ref[...]
        cps = []
        for k in range(1, N_DEV):
            px = 1 - x if k & 4 else x
            py = 1 - y if k & 2 else y
            pc = 1 - c if k & 1 else c
            cps.append((pltpu.make_async_remote_copy(
                src_ref=v_ref, dst_ref=buf.at[me], send_sem=send_sems.at[k - 1], recv_sem=recv_sems.at[k - 1],
                device_id=(px, py, pc), device_id_type=MESH), 4 * px + 2 * py + pc))
        for cp, _ in cps:
            cp.start()
        for k, (cp, peer) in enumerate(cps):
            pltpu.make_async_remote_copy(
                src_ref=v_ref, dst_ref=buf.at[peer], send_sem=send_sems.at[k], recv_sem=recv_sems.at[k],
                device_id=(x, y, c), device_id_type=MESH).wait_recv()
        for cp, _ in cps:
            cp.wait_send()
        acc = buf[0]
        for j in range(1, N_DEV):
            acc = acc + buf[j]
        o_ref[...] = acc

    return pl.pallas_call(
        body, name="small_all_reduce", out_shape=jax.ShapeDtypeStruct(v.shape, F32),
        in_specs=[pl.BlockSpec(memory_space=pltpu.VMEM)], out_specs=pl.BlockSpec(memory_space=pltpu.VMEM),
        scratch_shapes=[pltpu.VMEM((N_DEV, r, 128), F32), pltpu.SemaphoreType.DMA((N_DEV - 1,)),
                        pltpu.SemaphoreType.DMA((N_DEV - 1,))],
    )(v)


def _adamw(w, m, v, g, name):
    parts = w.ndim == 3
    r, cdim = w.shape[-2:]
    tr = r // 4 if r % 32 == 0 else r

    def body(w_ref, m_ref, v_ref, g_ref, go_ref, d_ref, mo_ref, vo_ref):
        if parts:
            gv = g_ref[0].astype(F32)
            for k in range(1, 4):
                gv = gv + g_ref[k].astype(F32)
            gv = gv[None]
        else:
            gv = g_ref[...]
        m2 = ADAM_B1 * m_ref[...] + (1.0 - ADAM_B1) * gv
        v2 = ADAM_B2 * v_ref[...] + (1.0 - ADAM_B2) * (gv * gv)
        m_hat = m2 / (1.0 - ADAM_B1 ** ADAM_STEP)
        v_hat = v2 / (1.0 - ADAM_B2 ** ADAM_STEP)
        go_ref[...] = gv
        d_ref[...] = -ADAM_LR * (m_hat / (jnp.sqrt(v_hat) + ADAM_EPS) + ADAM_WD * w_ref[...])
        mo_ref[...] = m2
        vo_ref[...] = v2

    if parts:
        row = pl.BlockSpec((1, tr, cdim), lambda i: (0, i, 0))
        g_spec = pl.BlockSpec((4, tr, cdim), lambda i: (0, i, 0))
    else:
        row = g_spec = pl.BlockSpec((tr, cdim), lambda i: (i, 0))
    return pl.pallas_call(
        body, name=name, grid=(r // tr,), in_specs=[row, row, row, g_spec], out_specs=[row] * 4,
        out_shape=[jax.ShapeDtypeStruct(w.shape, F32)] * 4,
        compiler_params=_params("parallel"),
    )(w, m, v, g)


WEIGHTS = ["ffn1_norm_g", "ffn1_w_gate", "ffn1_w_up", "ffn1_w_down", "mix_norm_g", "w_in", "attn_q_norm_g",
           "attn_k_norm_g", "attn_rel_bias", "hgrn_lower_bounds", "hgrn_out_norm_g", "w_out", "ffn2_norm_g",
           "ffn2_w_gate", "ffn2_w_up", "ffn2_w_down"]
COL_SHARDED = ("ffn1_w_gate", "ffn1_w_up", "w_in", "ffn2_w_gate", "ffn2_w_up")
ROW_SHARDED = ("ffn1_w_down", "w_out", "ffn2_w_down")
BIG = [n for n in WEIGHTS if n in COL_SHARDED or n in ROW_SHARDED]
SMALL = [n for n in WEIGHTS if n not in BIG]
PACK_ROWS = 8
FFN2 = ["ffn2_w_down", "ffn2_w_gate", "ffn2_w_up"]
MIXER = ["w_out", "w_in"]

PLAN = {
    "ffn1_norm": [("gather", ["ffn1_w_down"])],
    "bias_expand": [("gather", ["ffn1_w_gate", "ffn1_w_up"])],
    "ffn1_fwd": [("gather", MIXER)],
    "attn_fwd": [("gather", FFN2)],
    "ffn2_dh_gate": [("pair", FFN2)],
    "attn_bwd": [("chip", FFN2)],
    "in_proj_dx": [("pair", MIXER)],
    "ffn1_bwd_mid": [("chip", MIXER)],
    "ffn1_dwg": [("pair", ["ffn1_w_down"])],
    "ffn1_dwu": [("chip", ["ffn1_w_down"]), ("pair", ["ffn1_w_gate"])],
    "ffn1_dh_gate": [("chip", ["ffn1_w_gate"]), ("pair", ["ffn1_w_up"])],
    "bias_fold": [("chip", ["ffn1_w_up"])],
}


def _join_sides(sides):
    def split(refs, counts):
        out, at = [], 0
        for n in counts:
            out.append(refs[at:at + n])
            at += n
        return out

    n_in, n_out, n_sem = ([len(getattr(s, f)) for s in sides] for f in ("ins", "out_shape", "sems"))

    def run(which):
        def go(ins, outs, sems):
            for s, i, o, m in zip(sides, split(ins, n_in), split(outs, n_out), split(sems, n_sem)):
                getattr(s, which)(i, o, m)
        return go

    return _Side([a for s in sides for a in s.ins], [a for s in sides for a in s.out_shape],
                 [a for s in sides for a in s.sems], run("start"), run("finish"))


class _Schedule:
    def __init__(self, shards):
        self.shards = shards
        self.weights = {}
        self.sliced = {}
        self.partials = {}
        self.reduced = {}

    def put(self, name, grad):
        self.sliced[name] = grad.reshape((N_DEV,) + self.shards[name].shape)

    def side_for(self, call):
        if call not in PLAN:
            return None
        sides = []
        for kind, names in PLAN[call]:
            if kind == "gather":
                sides.append(_gather_side([self.shards[n] for n in names]))
            elif kind == "pair":
                sides.append(_pair_side([self.sliced[n] for n in names]))
            else:
                sides.append(_chip_side([self.partials[n] for n in names]))
        return _join_sides(sides)

    def done(self, call, outs):
        at = 0
        for kind, names in PLAN[call]:
            self.file(kind, names, outs[at:at + len(names)])
            at += len(names)

    def file(self, kind, names, outs):
        for n, o in zip(names, outs):
            if kind == "gather":
                self.weights[n] = o.reshape(N_DEV * o.shape[1], o.shape[2])
            elif kind == "pair":
                core = lax.axis_index("c").astype(jnp.int32).reshape(1)
                self.partials[n] = _pair_add(self.sliced[n], o, core, n + "_pair_add")
            else:
                self.reduced[n] = o


def _pack_small(vals, loss=None):
    parts = []
    for n in SMALL:
        a = vals[n]
        if n == "attn_rel_bias":
            a = jnp.pad(a.reshape(ATTN_HEADS, N_REL), ((0, 0), (0, N_REL_PAD - N_REL)))
        flat = a.reshape(-1)
        size = -(-flat.shape[0] // (PACK_ROWS * 128)) * PACK_ROWS * 128
        parts.append(jnp.pad(flat, (0, size - flat.shape[0])).reshape(-1, 128))
    tail = jnp.zeros((PACK_ROWS, 128), F32)
    if loss is not None:
        tail = tail.at[0, 0].set(loss)
    return jnp.concatenate(parts + [tail], axis=0)


def _unpack_small(packed, shapes):
    out, row = {}, 0
    for n in SMALL:
        shape = shapes[n]
        if n == "attn_rel_bias":
            rows = ATTN_HEADS * N_REL_PAD // 128
            out[n] = packed[row:row + rows].reshape(ATTN_HEADS, N_REL_PAD)[:, :N_REL].reshape(shape)
        else:
            size = 1
            for s in shape:
                size *= s
            rows = -(-size // (PACK_ROWS * 128)) * PACK_ROWS
            out[n] = packed[row:row + rows].reshape(-1)[:size].reshape(shape)
        row += rows
    return out, packed[row, 0]


def kernel(x, ffn1_norm_g, ffn1_w_gate, ffn1_w_up, ffn1_w_down, mix_norm_g, w_in, attn_q_norm_g, attn_k_norm_g, attn_rel_bias, hgrn_lower_bounds, hgrn_out_norm_g, w_out, ffn2_norm_g, ffn2_w_gate, ffn2_w_up, ffn2_w_down, loss_target, m_ffn1_norm_g, m_ffn1_w_gate, m_ffn1_w_up, m_ffn1_w_down, m_mix_norm_g, m_w_in, m_attn_q_norm_g, m_attn_k_norm_g, m_attn_rel_bias, m_hgrn_lower_bounds, m_hgrn_out_norm_g, m_w_out, m_ffn2_norm_g, m_ffn2_w_gate, m_ffn2_w_up, m_ffn2_w_down, v_ffn1_norm_g, v_ffn1_w_gate, v_ffn1_w_up, v_ffn1_w_down, v_mix_norm_g, v_w_in, v_attn_q_norm_g, v_attn_k_norm_g, v_attn_rel_bias, v_hgrn_lower_bounds, v_hgrn_out_norm_g, v_w_out, v_ffn2_norm_g, v_ffn2_w_gate, v_ffn2_w_up, v_ffn2_w_down):
    wts = dict(zip(WEIGHTS, (ffn1_norm_g, ffn1_w_gate, ffn1_w_up, ffn1_w_down, mix_norm_g, w_in, attn_q_norm_g,
                             attn_k_norm_g, attn_rel_bias, hgrn_lower_bounds, hgrn_out_norm_g, w_out, ffn2_norm_g,
                             ffn2_w_gate, ffn2_w_up, ffn2_w_down)))
    mom = dict(zip(WEIGHTS, (m_ffn1_norm_g, m_ffn1_w_gate, m_ffn1_w_up, m_ffn1_w_down, m_mix_norm_g, m_w_in,
                             m_attn_q_norm_g, m_attn_k_norm_g, m_attn_rel_bias, m_hgrn_lower_bounds,
                             m_hgrn_out_norm_g, m_w_out, m_ffn2_norm_g, m_ffn2_w_gate, m_ffn2_w_up, m_ffn2_w_down)))
    var = dict(zip(WEIGHTS, (v_ffn1_norm_g, v_ffn1_w_gate, v_ffn1_w_up, v_ffn1_w_down, v_mix_norm_g, v_w_in,
                             v_attn_q_norm_g, v_attn_k_norm_g, v_attn_rel_bias, v_hgrn_lower_bounds,
                             v_hgrn_out_norm_g, v_w_out, v_ffn2_norm_g, v_ffn2_w_gate, v_ffn2_w_up, v_ffn2_w_down)))
    nb, seq, d = x.shape
    shapes = {n: wts[n].shape for n in WEIGHTS}

    def rows_first(a, n):
        return jnp.swapaxes(a, 1, 2) if n in COL_SHARDED else a

    sched = _Schedule({n: rows_first(wts[n], n)[0].astype(BF16) for n in BIG})
    sp = {n: wts[n] for n in SMALL}
    sp["attn_rel_bias"] = wts["attn_rel_bias"][0]
    _ACTIVE[0] = sched
    try:
        loss, dx, dsmall = _local_step(x.reshape(nb * seq, d), loss_target.reshape(nb * seq, d), sp,
                                       sched.weights, sched.put, nb, seq)
    finally:
        _ACTIVE[0] = None
    reduced = sched.reduced

    small_sum = _all_reduce_small(_pack_small(dsmall, loss))
    gsmall, loss_total = _unpack_small(small_sum, shapes)

    grads, deltas, new_m, new_v = {}, {}, {}, {}
    for n in BIG:
        out = _adamw(rows_first(wts[n], n), rows_first(mom[n], n), rows_first(var[n], n), reduced[n], n + "_adamw")
        grads[n], deltas[n], new_m[n], new_v[n] = (rows_first(o, n) for o in out)
    packed = _adamw(_pack_small(wts), _pack_small(mom), _pack_small(var), small_sum, "small_adamw")
    for dst, p in zip((deltas, new_m, new_v), packed[1:]):
        dst.update(_unpack_small(p, shapes)[0])
    grads.update(gsmall)

    return (loss_total, dx.reshape(nb, seq, d), *[grads[n] for n in WEIGHTS], *[deltas[n] for n in WEIGHTS],
            *[new_m[n] for n in WEIGHTS], *[new_v[n] for n in WEIGHTS])
```

```python
import functools

import jax
import jax.numpy as jnp
from jax import lax
from jax.experimental import pallas as pl
from jax.experimental.pallas import tpu as pltpu

F32 = jnp.float32
BF16 = jnp.bfloat16

RMS_EPS = 1e-6
CHUNK = 64
LEFT_CHUNKS = 8
BAND = (LEFT_CHUNKS + 2) * CHUNK
KPAD = BAND - CHUNK
REL_CLIP = 128
N_REL = 2 * REL_CLIP + 1
N_REL_PAD = 384
ATTN_HEADS = 8
ATTN_HEAD_DIM = 64
ATTN_WIDTH = ATTN_HEADS * ATTN_HEAD_DIM
ATTN_FWD_ROWS = 32
ATTN_UNROLL = 4
HGRN_HEADS = 4
HGRN_HEAD_DIM = 128
HGRN_ROWS = 512
SUB = 16
N_SUB = CHUNK // SUB
N_DEV = 8

ADAM_LR = 0.001
ADAM_B1 = 0.9
ADAM_B2 = 0.999
ADAM_EPS = 1e-08
ADAM_WD = 0.01
ADAM_STEP = 10

VMEM_LIMIT = 56 * 1024 * 1024
NT = (((1,), (1,)), ((), ()))
NN = (((1,), (0,)), ((), ()))


def _params(*sem):
    return pltpu.CompilerParams(dimension_semantics=sem, vmem_limit_bytes=VMEM_LIMIT)


def _sigmoid(v):
    return 0.5 * jnp.tanh(0.5 * v) + 0.5


def _dot(a, b, dims=NN):
    return lax.dot_general(a.astype(BF16), b.astype(BF16), dims, preferred_element_type=F32)


def _dot_exact01(m01, v):
    m = m01.astype(BF16)
    hi = v.astype(BF16)
    r1 = v - hi.astype(F32)
    mid = r1.astype(BF16)
    lo = (r1 - mid.astype(F32)).astype(BF16)
    out = lax.dot_general(m, hi, NN, preferred_element_type=F32)
    out = out + lax.dot_general(m, mid, NN, preferred_element_type=F32)
    return out + lax.dot_general(m, lo, NN, preferred_element_type=F32)


def _dot_exact01_r(v, m01):
    m = m01.astype(BF16)
    hi = v.astype(BF16)
    r1 = v - hi.astype(F32)
    mid = r1.astype(BF16)
    lo = (r1 - mid.astype(F32)).astype(BF16)
    out = lax.dot_general(hi, m, NN, preferred_element_type=F32)
    out = out + lax.dot_general(mid, m, NN, preferred_element_type=F32)
    return out + lax.dot_general(lo, m, NN, preferred_element_type=F32)


def _tn(a, b):
    ap = jnp.concatenate([a, jnp.zeros_like(a)], axis=0)
    bp = jnp.concatenate([b, jnp.zeros_like(b)], axis=0)
    return _dot(ap.T, bp)


def _row_tile(t):
    for tm in (512, 256, 128, 64, 32, 16, 8):
        if t % tm == 0:
            return tm
    raise ValueError(t)


class _Side:
    def __init__(self, ins, out_shape, sems, start, finish):
        self.ins, self.out_shape, self.sems, self.start, self.finish = ins, out_shape, sems, start, finish


_ACTIVE = [None]


def _pallas(body, *, name, grid, in_specs, out_specs, out_shape, scratch_shapes=(), sem, args):
    sched = _ACTIVE[0]
    side = sched.side_for(name) if sched is not None else None
    if side is None:
        return pl.pallas_call(
            body, name=name, grid=grid, in_specs=list(in_specs), out_specs=list(out_specs),
            out_shape=list(out_shape), scratch_shapes=list(scratch_shapes), compiler_params=_params(*sem))(*args)
    cuts = [len(in_specs), len(side.ins), len(out_shape), len(side.out_shape), len(scratch_shapes)]

    def with_side(*refs):
        groups, at = [], 0
        for n in cuts:
            groups.append(refs[at:at + n])
            at += n
        ins, side_ins, outs, side_outs, scratch = groups
        side_sems = refs[at:]
        first = pl.program_id(0) == 0
        last = pl.program_id(0) == grid[0] - 1
        for a in range(1, len(grid)):
            first = jnp.logical_and(first, pl.program_id(a) == 0)
            last = jnp.logical_and(last, pl.program_id(a) == grid[a] - 1)

        @pl.when(first)
        def _():
            side.start(side_ins, side_outs, side_sems)

        body(*ins, *outs, *scratch)

        @pl.when(last)
        def _():
            side.finish(side_ins, side_outs, side_sems)

    hbm = pl.BlockSpec(memory_space=pl.ANY)
    res = pl.pallas_call(
        with_side, name=name, grid=grid, in_specs=list(in_specs) + [hbm] * len(side.ins),
        out_specs=list(out_specs) + [hbm] * len(side.out_shape), out_shape=list(out_shape) + list(side.out_shape),
        scratch_shapes=list(scratch_shapes) + list(side.sems),
        compiler_params=_params(*(["arbitrary"] * len(grid))))(*args, *side.ins)
    sched.done(name, res[len(out_shape):])
    return res[:len(out_shape)]


def _rms_fwd(x, g, name):
    t, d = x.shape
    tm = _row_tile(t)

    def body(x_ref, g_ref, h_ref):
        xv = x_ref[...]
        r = lax.rsqrt(jnp.mean(xv * xv, axis=-1, keepdims=True) + RMS_EPS)
        h_ref[...] = (xv * r * g_ref[...]).astype(BF16)

    return _pallas(
        body, name=name, grid=(t // tm,),
        in_specs=[pl.BlockSpec((tm, d), lambda i: (i, 0)), pl.BlockSpec((1, d), lambda i: (0, 0))],
        out_specs=[pl.BlockSpec((tm, d), lambda i: (i, 0))], out_shape=[jax.ShapeDtypeStruct((t, d), BF16)],
        sem=("parallel",), args=(x, g))[0]


def _accumulate(ref, part, step):
    @pl.when(step == 0)
    def _():
        ref[...] = part

    @pl.when(step > 0)
    def _():
        ref[...] += part


def _mm(a, b, *, ta=False, tb=False, tm, tn, out_dtype=F32, add=None, scale=1.0, norm_g=None, norm_bwd=None, name):
    m, k = (a.shape[1], a.shape[0]) if ta else a.shape
    n = b.shape[0] if tb else b.shape[1]
    tm, tn = min(tm, m), min(tn, n)
    assert m % tm == 0 and n % tn == 0, (m, n, tm, tn)
    assert (norm_g is None and norm_bwd is None) or tn == n
    dims = (((0 if ta else 1,), (1 if tb else 0,)), ((), ()))
    n_in = 2 + (add is not None) + (norm_g is not None) + (3 if norm_bwd is not None else 0)

    def body(*refs):
        ins, outs = list(refs[2:n_in]), refs[n_in:]
        r = lax.dot_general(refs[0][...].astype(BF16), refs[1][...].astype(BF16), dims, preferred_element_type=F32)
        if scale != 1.0:
            r = r * scale
        if add is not None:
            r = r + ins.pop(0)[...]
        if norm_bwd is not None:
            xv, gv, dres = (ref[...] for ref in ins)
            rs = lax.rsqrt(jnp.mean(xv * xv, axis=-1, keepdims=True) + RMS_EPS)
            xhat = xv * rs
            gd = r * gv
            dx = dres + rs * (gd - xhat * jnp.mean(gd * xhat, axis=-1, keepdims=True))
            outs[0][...] = dx
            outs[1][...] = dx.astype(BF16)
            _accumulate(outs[2], jnp.sum(r * xhat, axis=0, keepdims=True), pl.program_id(0))
            return
        outs[0][...] = r.astype(out_dtype)
        if norm_g is not None:
            rs = lax.rsqrt(jnp.mean(r * r, axis=-1, keepdims=True) + RMS_EPS)
            outs[1][...] = (r * rs * ins.pop(0)[...]).astype(BF16)

    a_spec = pl.BlockSpec((k, tm), lambda i, j: (0, i)) if ta else pl.BlockSpec((tm, k), lambda i, j: (i, 0))
    b_spec = pl.BlockSpec((tn, k), lambda i, j: (j, 0)) if tb else pl.BlockSpec((k, tn), lambda i, j: (0, j))
    o_spec = pl.BlockSpec((tm, tn), lambda i, j: (i, j))
    vec = pl.BlockSpec((1, tn), lambda i, j: (0, j))
    args, specs = [a, b], [a_spec, b_spec]
    if add is not None:
        args.append(add)
        specs.append(o_spec)
    out_specs, out_shape = [o_spec], [jax.ShapeDtypeStruct((m, n), out_dtype)]
    if norm_g is not None:
        args.append(norm_g)
        specs.append(vec)
        out_specs.append(o_spec)
        out_shape.append(jax.ShapeDtypeStruct((m, n), BF16))
    if norm_bwd is not None:
        args += list(norm_bwd)
        specs += [o_spec, vec, o_spec]
        out_specs = [o_spec, o_spec, vec]
        out_shape = [jax.ShapeDtypeStruct((m, n), F32), jax.ShapeDtypeStruct((m, n), BF16),
                     jax.ShapeDtypeStruct((1, n), F32)]
    res = _pallas(body, name=name, grid=(m // tm, n // tn), in_specs=specs, out_specs=out_specs, out_shape=out_shape,
                  sem=("arbitrary", "arbitrary") if norm_bwd is not None else ("parallel", "parallel"), args=args)
    return res[0] if len(res) == 1 else res


def _ffn_tile(f):
    for tf in (1408, 512, 256, 128):
        if f % tf == 0:
            return tf
    raise ValueError(f)


def _ffn_fwd(h, x, wg, wu, wd, name, next_g=None, tgt=None):
    t, d = x.shape
    f = wg.shape[0]
    tm, tf = _row_tile(t), _ffn_tile(f)
    nf = f // tf
    assert (next_g is None) != (tgt is None)

    def body(h_ref, x_ref, wg_ref, wu_ref, wd_ref, tail_ref, g_ref, u_ref, o0_ref, o1_ref, *rest):
        acc_ref = rest[-1]
        j = pl.program_id(1)
        hv = h_ref[...]
        gv = lax.dot_general(hv, wg_ref[...], NT, preferred_element_type=F32)
        uv = lax.dot_general(hv, wu_ref[...], NT, preferred_element_type=F32)
        av = gv * _sigmoid(gv) * uv
        g_ref[...] = gv.astype(BF16)
        u_ref[...] = uv.astype(BF16)
        _accumulate(acc_ref, lax.dot_general(av.astype(BF16), wd_ref[...], NN, preferred_element_type=F32), j)

        @pl.when(j == nf - 1)
        def _():
            y = x_ref[...] + 0.5 * acc_ref[...]
            if tgt is None:
                o0_ref[...] = y
                rs = lax.rsqrt(jnp.mean(y * y, axis=-1, keepdims=True) + RMS_EPS)
                o1_ref[...] = (y * rs * tail_ref[...]).astype(BF16)
            else:
                e = y - tail_ref[...]
                dy = e * (1.0 / d)
                o0_ref[...] = dy
                o1_ref[...] = dy.astype(BF16)
                _accumulate(rest[0], jnp.sum(e * e, axis=0, keepdims=True), pl.program_id(0))

    row = pl.BlockSpec((tm, d), lambda i, j: (i, 0))
    hid = pl.BlockSpec((tm, tf), lambda i, j: (i, j))
    vec = pl.BlockSpec((1, d), lambda i, j: (0, 0))
    out_specs = [hid, hid, row, row] + ([vec] if tgt is not None else [])
    out_shape = [jax.ShapeDtypeStruct((t, f), BF16)] * 2 + [jax.ShapeDtypeStruct((t, d), F32),
                                                            jax.ShapeDtypeStruct((t, d), BF16)]
    if tgt is not None:
        out_shape.append(jax.ShapeDtypeStruct((1, d), F32))
    return _pallas(
        body, name=name, grid=(t // tm, nf),
        in_specs=[row, row] + [pl.BlockSpec((tf, d), lambda i, j: (j, 0))] * 3 + [vec if tgt is None else row],
        out_specs=out_specs, out_shape=out_shape, scratch_shapes=[pltpu.VMEM((tm, d), F32)],
        sem=("parallel" if tgt is None else "arbitrary", "arbitrary"),
        args=(h, x, wg, wu, wd, next_g if tgt is None else tgt))


def _ffn_bwd_mid(dy, wd, g, u, name):
    t, d = dy.shape
    f = wd.shape[0]
    tm, tf = _row_tile(t), _ffn_tile(f)

    def body(dy_ref, wd_ref, g_ref, u_ref, dg_ref, du_ref, a_ref):
        da = 0.5 * lax.dot_general(dy_ref[...].astype(BF16), wd_ref[...], NT, preferred_element_type=F32)
        gv = g_ref[...].astype(F32)
        uv = u_ref[...].astype(F32)
        s = _sigmoid(gv)
        silu = gv * s
        dg_ref[...] = (da * uv * (s * (1.0 + gv * (1.0 - s)))).astype(BF16)
        du_ref[...] = (da * silu).astype(BF16)
        a_ref[...] = (silu * uv).astype(BF16)

    hid = pl.BlockSpec((tm, tf), lambda i, j: (i, j))
    return _pallas(
        body, name=name, grid=(t // tm, f // tf),
        in_specs=[pl.BlockSpec((tm, d), lambda i, j: (i, 0)), pl.BlockSpec((tf, d), lambda i, j: (j, 0)), hid, hid],
        out_specs=[hid, hid, hid], out_shape=[jax.ShapeDtypeStruct((t, f), BF16)] * 3,
        sem=("parallel", "parallel"), args=(dy, wd, g, u))


def _rel_index(t, s_band):
    return jnp.clip(t + KPAD - s_band, -REL_CLIP, REL_CLIP) + REL_CLIP


def _bias_expand(rel_bias_pad):
    nh = rel_bias_pad.shape[0]

    def body(rb_ref, out_ref):
        rb = rb_ref[...]
        i_io = lax.broadcasted_iota(jnp.int32, (N_REL_PAD, BAND), 0)
        s_io = lax.broadcasted_iota(jnp.int32, (N_REL_PAD, BAND), 1)

        def row(t, carry):
            onehot = (i_io == _rel_index(t, s_io)).astype(F32)
            out_ref[t] = _dot_exact01_r(rb, onehot)
            return carry

        lax.fori_loop(0, CHUNK, row, 0)

    return _pallas(
        body, name="bias_expand", grid=(1,), in_specs=[pl.BlockSpec(rel_bias_pad.shape, lambda i: (0, 0))],
        out_specs=[pl.BlockSpec((CHUNK, nh, BAND), lambda i: (0, 0, 0))],
        out_shape=[jax.ShapeDtypeStruct((CHUNK, nh, BAND), F32)], sem=("arbitrary",), args=(rel_bias_pad,))[0]


def _bias_fold(dbias):
    ng, nh = dbias.shape[0], dbias.shape[2]

    def body(db_ref, out_ref):
        s_io = lax.broadcasted_iota(jnp.int32, (BAND, N_REL_PAD), 0)
        i_io = lax.broadcasted_iota(jnp.int32, (BAND, N_REL_PAD), 1)

        def row(t, acc):
            onehot = (i_io == _rel_index(t, s_io)).astype(F32)
            d = db_ref[0, t]
            for gi in range(1, ng):
                d = d + db_ref[gi, t]
            return acc + _dot_exact01_r(d, onehot)

        out_ref[...] = lax.fori_loop(0, CHUNK, row, jnp.zeros((nh, N_REL_PAD), F32))

    return _pallas(
        body, name="bias_fold", grid=(1,), in_specs=[pl.BlockSpec(dbias.shape, lambda i: (0, 0, 0, 0))],
        out_specs=[pl.BlockSpec((nh, N_REL_PAD), lambda i: (0, 0))],
        out_shape=[jax.ShapeDtypeStruct((nh, N_REL_PAD), F32)], sem=("arbitrary",), args=(dbias,))[0]


def _left_half(shape):
    return lax.broadcasted_iota(jnp.int32, shape, len(shape) - 1) < ATTN_HEAD_DIM


def _stack_heads(v):
    left = _left_half(v.shape)
    zero = jnp.zeros_like(v)
    return jnp.concatenate([jnp.where(left, v, zero), jnp.where(left, zero, v)], axis=0)


def _unstack_heads(v):
    return jnp.where(_left_half((CHUNK, 128)), v[0:CHUNK, :], v[CHUNK:2 * CHUNK, :])


def _half_mean(v):
    r = lax.broadcasted_iota(jnp.int32, (128, 128), 0) < ATTN_HEAD_DIM
    c = lax.broadcasted_iota(jnp.int32, (128, 128), 1) < ATTN_HEAD_DIM
    return _dot_exact01_r(v, r == c) * (1.0 / ATTN_HEAD_DIM)


def _attn_prepare(q_ref, k_ref, v_ref, gq_ref, gk_ref, qs_scr, k_scr, v_scr):
    q, k = q_ref[...], k_ref[...]
    rq = lax.rsqrt(_half_mean(q * q) + RMS_EPS)
    rk = lax.rsqrt(_half_mean(k * k) + RMS_EPS)
    qhat, khat = q * rq, k * rk
    qs_scr[...] = (qhat * gq_ref[...] * ATTN_HEAD_DIM ** -0.5).astype(BF16)
    k_scr[0:KPAD, :] = jnp.zeros((KPAD, 128), BF16)
    v_scr[0:KPAD, :] = jnp.zeros((KPAD, 128), BF16)
    k_scr[KPAD:, :] = (khat * gk_ref[...]).astype(BF16)
    v_scr[KPAD:, :] = v_ref[...].astype(BF16)
    return qhat, rq, khat, rk


def _attn_scores(qs_scr, k_scr, bias_ref, c, rows):
    r0 = pl.multiple_of(c * CHUNK, CHUNK)
    qst = _stack_heads(qs_scr[pl.ds(r0, CHUNK), :])
    kb = k_scr[pl.ds(r0, BAND), :]
    s_all = lax.dot_general(qst, kb, NT, preferred_element_type=F32)
    col = lax.broadcasted_iota(jnp.int32, (rows, BAND), 1)
    first = jnp.maximum(CHUNK, (LEFT_CHUNKS + 1 - c) * CHUNK)
    es, invs = [], []
    for lo in range(0, 2 * CHUNK, rows):
        s = jnp.where(col >= first, s_all[lo:lo + rows] + bias_ref[lo:lo + rows, :], -jnp.inf)
        e = jnp.exp(s - jnp.max(s, axis=-1, keepdims=True))
        es.append(e)
        invs.append(1.0 / jnp.sum(e, axis=-1, keepdims=True))
    return jnp.concatenate(es, axis=0), jnp.concatenate(invs, axis=0), qst, kb, r0


def _attn_fwd(proj, bias, gq, gk, nb, seq):
    nc = seq // CHUNK

    def body(q_ref, k_ref, v_ref, bias_ref, gq_ref, gk_ref, o_ref, qs_scr, k_scr, v_scr):
        _attn_prepare(q_ref, k_ref, v_ref, gq_ref, gk_ref, qs_scr, k_scr, v_scr)

        def chunk(c, carry):
            e, inv, _, _, r0 = _attn_scores(qs_scr, k_scr, bias_ref, c, ATTN_FWD_ROWS)
            vb = v_scr[pl.ds(r0, BAND), :]
            o_ref[pl.ds(r0, CHUNK), :] = _unstack_heads(
                lax.dot_general(e.astype(BF16), vb, NN, preferred_element_type=F32) * inv)
            return carry

        lax.fori_loop(0, nc, chunk, 0, unroll=ATTN_UNROLL)

    def col(off):
        return pl.BlockSpec((seq, 128), lambda b, hp: (b, off + hp))

    vec = pl.BlockSpec((1, 128), lambda b, hp: (0, 0))
    return _pallas(
        body, name="attn_fwd", grid=(nb, ATTN_HEADS // 2),
        in_specs=[col(0), col(4), col(8), pl.BlockSpec((2 * CHUNK, BAND), lambda b, hp: (hp, 0)), vec, vec],
        out_specs=[pl.BlockSpec((seq, 128), lambda b, hp: (b, hp))],
        out_shape=[jax.ShapeDtypeStruct((nb * seq, ATTN_WIDTH), F32)],
        scratch_shapes=[pltpu.VMEM((seq, 128), BF16), pltpu.VMEM((seq + KPAD, 128), BF16),
                        pltpu.VMEM((seq + KPAD, 128), BF16)],
        sem=("parallel", "parallel"), args=(proj, proj, proj, bias, gq, gk))[0]


def _attn_bwd(proj, out, dout, bias, gq, gk, nb, seq):
    nc = seq // CHUNK
    scale = ATTN_HEAD_DIM ** -0.5

    def body(q_ref, k_ref, v_ref, o_ref, do_ref, bias_ref, gq_ref, gk_ref,
             dq_ref, dk_ref, dv_ref, dbias_ref, dgq_ref, dgk_ref,
             qs_scr, k_scr, v_scr, dqn_scr, dk_scr, dv_scr, db_scr):
        qhat, rq, khat, rk = _attn_prepare(q_ref, k_ref, v_ref, gq_ref, gk_ref, qs_scr, k_scr, v_scr)
        dk_scr[...] = jnp.zeros_like(dk_scr)
        dv_scr[...] = jnp.zeros_like(dv_scr)
        db_scr[...] = jnp.zeros_like(db_scr)

        def chunk(c, carry):
            e, inv, qst, kb, r0 = _attn_scores(qs_scr, k_scr, bias_ref, c, 2 * CHUNK)
            p = e * inv
            vb = v_scr[pl.ds(r0, BAND), :]
            do_c = do_ref[pl.ds(r0, CHUNK), :]
            dost = _stack_heads(do_c)
            drow = jnp.sum(dost * _stack_heads(o_ref[pl.ds(r0, CHUNK), :]), axis=-1, keepdims=True)
            dp = lax.dot_general(dost.astype(BF16), vb, NT, preferred_element_type=F32)
            ds = p * (dp - drow)
            db_scr[...] += ds
            dqn_scr[pl.ds(r0, CHUNK), :] = scale * _unstack_heads(
                lax.dot_general(ds.astype(BF16), kb, NN, preferred_element_type=F32))
            dk_scr[pl.ds(r0, BAND), :] += lax.dot_general(ds.T.astype(BF16), qst, NN, preferred_element_type=F32)
            dv_scr[pl.ds(r0, BAND), :] += _dot(p.T, dost)
            return carry

        lax.fori_loop(0, nc, chunk, 0, unroll=ATTN_UNROLL)

        def norm_bwd(dn, hat, r, g_ref):
            gd = dn * g_ref[...]
            return r * (gd - hat * _half_mean(gd * hat)), jnp.sum(dn * hat, axis=0, keepdims=True)

        dq, dgq = norm_bwd(dqn_scr[...], qhat, rq, gq_ref)
        dk, dgk = norm_bwd(dk_scr[KPAD:, :], khat, rk, gk_ref)
        dq_ref[...] = dq.astype(BF16)
        dk_ref[...] = dk.astype(BF16)
        dv_ref[...] = dv_scr[KPAD:, :].astype(BF16)
        dbias_ref[0] = db_scr[...]
        dgq_ref[0] = dgq
        dgk_ref[0] = dgk

    def col(off):
        return pl.BlockSpec((seq, 128), lambda b, hp: (b, off + hp))

    vec = pl.BlockSpec((1, 128), lambda b, hp: (0, 0))
    gvec = pl.BlockSpec((1, 1, 128), lambda b, hp: (b * (ATTN_HEADS // 2) + hp, 0, 0))
    t = nb * seq
    return _pallas(
        body, name="attn_bwd", grid=(nb, ATTN_HEADS // 2),
        in_specs=[col(0), col(4), col(8), col(0), col(0),
                  pl.BlockSpec((2 * CHUNK, BAND), lambda b, hp: (hp, 0)), vec, vec],
        out_specs=[col(0), col(0), col(0), pl.BlockSpec((1, 2 * CHUNK, BAND), lambda b, hp: (b, hp, 0)),
                   gvec, gvec],
        out_shape=[jax.ShapeDtypeStruct((t, ATTN_WIDTH), BF16)] * 3
        + [jax.ShapeDtypeStruct((nb, ATTN_HEADS * CHUNK, BAND), F32)]
        + [jax.ShapeDtypeStruct((nb * ATTN_HEADS // 2, 1, 128), F32)] * 2,
        scratch_shapes=[pltpu.VMEM((seq, 128), BF16), pltpu.VMEM((seq + KPAD, 128), BF16),
                        pltpu.VMEM((seq + KPAD, 128), BF16), pltpu.VMEM((seq, 128), F32),
                        pltpu.VMEM((seq + KPAD, 128), F32), pltpu.VMEM((seq + KPAD, 128), F32),
                        pltpu.VMEM((2 * CHUNK, BAND), F32)],
        sem=("parallel", "parallel"), args=(proj, proj, proj, out, dout, bias, gq, gk))


def _tri(lower):
    r = lax.broadcasted_iota(jnp.int32, (CHUNK, CHUNK), 0)
    c = lax.broadcasted_iota(jnp.int32, (CHUNK, CHUNK), 1)
    return (r >= c) if lower else (r <= c)


def _hgrn_gates(hq, hf, lb):
    sq = _sigmoid(hq)
    sf = _sigmoid(hf)
    return hq * sq, sq, sf, lb + (1.0 - lb) * sf


def _hgrn_offdiag(q_s, k_s, b_s):
    row = lax.broadcasted_iota(jnp.int32, (CHUNK, HGRN_HEAD_DIM), 0)
    bv, qv, kv = b_s[...], q_s[...], k_s[...]
    eqs, eks = [], []
    for i in range(1, N_SUB):
        r = b_s[pl.ds(SUB * i - 1, 1), :]
        in_i = (row >= SUB * i) & (row < SUB * (i + 1))
        eqs.append(jnp.exp(jnp.where(in_i, bv - r, -jnp.inf)))
        eks.append(jnp.exp(jnp.where(row < SUB * i, r - bv, -jnp.inf)))
    eq = jnp.concatenate(eqs, axis=1)
    ek = jnp.concatenate(eks, axis=1)
    qt = jnp.concatenate([qv] * (N_SUB - 1), axis=1) * eq
    kt = jnp.concatenate([kv] * (N_SUB - 1), axis=1) * ek
    return qt, kt, eq, ek


def _hgrn_diag_e(b_s, i, s):
    t_io = lax.broadcasted_iota(jnp.int32, (SUB, HGRN_HEAD_DIM), 0)
    bi = b_s[pl.ds(SUB * i, SUB), :]
    return jnp.exp(jnp.where(t_io >= s, bi - b_s[pl.ds(SUB * i + s, 1), :], -jnp.inf)), t_io


def _hgrn_intra(q_s, k_s, b_s, a_s, qt, kt):
    ktp = jnp.concatenate([kt, jnp.zeros_like(kt)], axis=0)
    a_s[...] = _dot(qt, ktp, NT)
    col = lax.broadcasted_iota(jnp.int32, (SUB, HGRN_HEAD_DIM), 1)
    for i in range(N_SUB):
        qi = q_s[pl.ds(SUB * i, SUB), :]
        ai = jnp.zeros((SUB, HGRN_HEAD_DIM), F32)
        for s in range(SUB):
            e, _ = _hgrn_diag_e(b_s, i, s)
            a_col = jnp.sum(qi * k_s[pl.ds(SUB * i + s, 1), :] * e, axis=-1, keepdims=True)
            ai = ai + jnp.where(col == SUB * i + s, a_col, 0.0)
        a_s[pl.ds(SUB * i, SUB), :] += ai


def _hgrn_fwd(proj, lb, go, nb, seq):
    nc = seq // CHUNK
    hd = HGRN_HEAD_DIM
    rows_blk = min(HGRN_ROWS, seq)
    nblk, nck = seq // rows_blk, rows_blk // CHUNK

    def body(hq_ref, hf_ref, hi_ref, hg_ref, lb_ref, go_ref, y_ref, o_ref, st_ref, a_ref,
             st_all, q_all, k_all, b_all, a_all):
        @pl.when(pl.program_id(1) == 0)
        def _():
            st_all[...] = jnp.zeros_like(st_all)

        lower = _tri(True)

        def head_chunk(hh, c, rows):
            ln = slice(hd * hh, hd * (hh + 1))
            st, q_s, k_s, b_s, a_s = st_all.at[hh], q_all.at[hh], k_all.at[hh], b_all.at[hh], a_all.at[hh]
            q, _, _, f = _hgrn_gates(hq_ref[rows, ln], hf_ref[rows, ln], lb_ref[:, ln])
            v = hi_ref[rows, ln]
            b = _dot_exact01(lower, jnp.log(f))
            q_s[...] = q
            k_s[...] = 1.0 - f
            b_s[...] = b
            st_ref[hh, c] = st[...]
            qt, kt, _, _ = _hgrn_offdiag(q_s, k_s, b_s)
            _hgrn_intra(q_s, k_s, b_s, a_s, qt, kt)
            a16 = a_s[...].astype(BF16)
            a_ref[hh, c] = a16
            vp = jnp.concatenate([v, jnp.zeros_like(v)], axis=0)
            o = _dot(a16, vp) + _dot(q * jnp.exp(b), st[...], NT)
            bl = b_s[pl.ds(CHUNK - 1, 1), :]
            st[...] = st[...] * jnp.exp(bl) + _tn(v, (1.0 - f) * jnp.exp(bl - b))
            o_ref[rows, ln] = o
            n = o * lax.rsqrt(jnp.mean(o * o, axis=-1, keepdims=True) + RMS_EPS) * go_ref[...]
            hg = hg_ref[rows, ln]
            y_ref[rows, ln] = n * hg * _sigmoid(hg)

        def chunk(c, carry):
            rows = pl.ds(pl.multiple_of(c * CHUNK, CHUNK), CHUNK)
            for hh in range(HGRN_HEADS):
                head_chunk(hh, c, rows)
            return carry

        lax.fori_loop(0, nck, chunk, 0)

    hp, wide = HGRN_HEADS, HGRN_HEADS * hd

    def col(off):
        return pl.BlockSpec((rows_blk, wide), lambda b, s: (b * nblk + s, off // hp))

    out = pl.BlockSpec((rows_blk, wide), lambda b, s: (b * nblk + s, 0))
    t = nb * seq
    return pl.pallas_call(
        body, name="hgrn_fwd", grid=(nb, nblk),
        in_specs=[col(12), col(16), col(20), col(24), pl.BlockSpec((1, wide), lambda b, s: (0, 0)),
                  pl.BlockSpec((1, hd), lambda b, s: (0, 0))],
        out_specs=[out, out, pl.BlockSpec((hp, nck, hd, hd), lambda b, s: (b, s, 0, 0)),
                   pl.BlockSpec((hp, nck, CHUNK, hd), lambda b, s: (b, s, 0, 0))],
        out_shape=[jax.ShapeDtypeStruct((t, wide), F32)] * 2
        + [jax.ShapeDtypeStruct((nb * hp, nc, hd, hd), F32), jax.ShapeDtypeStruct((nb * hp, nc, CHUNK, hd), BF16)],
        scratch_shapes=[pltpu.VMEM((hp, hd, hd), F32)] + [pltpu.VMEM((hp, CHUNK, hd), F32)] * 4,
        compiler_params=_params("parallel", "arbitrary"),
    )(proj, proj, proj, proj, lb, go)


def _hgrn_bwd(proj, lb, go, o_pre, states, scores, dout, nb, seq):
    nc = seq // CHUNK
    hd = HGRN_HEAD_DIM
    rows_blk = min(HGRN_ROWS, seq)
    nblk, nck = seq // rows_blk, rows_blk // CHUNK

    def body(hq_ref, hf_ref, hi_ref, hg_ref, lb_ref, go_ref, o_ref, st_ref, a_ref, dy_ref,
             dhq_ref, dhf_ref, dhi_ref, dhg_ref, dlb_ref, dgo_ref,
             dst_all, q_all, k_all, b_all, da_all, dqi_all, dki_all, dlb_all, dgo_all):
        @pl.when(pl.program_id(1) == 0)
        def _():
            dst_all[...] = jnp.zeros_like(dst_all)
            dlb_all[...] = jnp.zeros_like(dlb_all)
            dgo_all[...] = jnp.zeros_like(dgo_all)

        lower, upper = _tri(True), _tri(False)
        gov = go_ref[...]
        row = lax.broadcasted_iota(jnp.int32, (CHUNK, hd), 0)

        def head_chunk(hh, c, rows):
            ln = slice(hd * hh, hd * (hh + 1))
            dst, q_s, k_s, b_s = dst_all.at[hh], q_all.at[hh], k_all.at[hh], b_all.at[hh]
            da_s, dqi_s, dki_s = da_all.at[hh], dqi_all.at[hh], dki_all.at[hh]
            dlb_acc, dgo_acc = dlb_all.at[hh], dgo_all.at[hh]
            lbv = lb_ref[:, ln]
            hq, hf, v, hg = hq_ref[rows, ln], hf_ref[rows, ln], hi_ref[rows, ln], hg_ref[rows, ln]
            q, sq, sf, f = _hgrn_gates(hq, hf, lbv)
            kk = 1.0 - f
            b = _dot_exact01(lower, jnp.log(f))
            q_s[...] = q
            k_s[...] = kk
            b_s[...] = b
            bl = b_s[pl.ds(CHUNK - 1, 1), :]
            ebl = jnp.exp(bl)
            ekd = jnp.exp(bl - b)
            kd = kk * ekd
            eb = jnp.exp(b)
            qb = q * eb
            st0 = st_ref[hh, c]
            dst1 = dst[...]

            o = o_ref[rows, ln]
            dy = dy_ref[rows, ln]
            sg = _sigmoid(hg)
            rstd = lax.rsqrt(jnp.mean(o * o, axis=-1, keepdims=True) + RMS_EPS)
            ohat = o * rstd
            dn = dy * hg * sg
            dhg_ref[rows, ln] = (dy * ohat * gov * (sg * (1.0 + hg * (1.0 - sg)))).astype(BF16)
            dgo_acc[...] += jnp.sum(dn * ohat, axis=0, keepdims=True)
            gdn = dn * gov
            do = rstd * (gdn - ohat * jnp.mean(gdn * ohat, axis=-1, keepdims=True))

            qt, kt, eq, ek = _hgrn_offdiag(q_s, k_s, b_s)
            da = _dot(do, v, NT)
            dat = _dot(v, do, NT)
            da_s[...] = da
            dqo = _dot(da, kt) * eq
            dko = _dot(dat, qt) * ek
            dqi_s[...] = sum(dqo[:, j * hd:(j + 1) * hd] for j in range(N_SUB - 1))
            dki_s[...] = sum(dko[:, j * hd:(j + 1) * hd] for j in range(N_SUB - 1))
            col = lax.broadcasted_iota(jnp.int32, (SUB, CHUNK), 1)
            for i in range(N_SUB):
                qi = q_s[pl.ds(SUB * i, SUB), :]
                dai = da_s[pl.ds(SUB * i, SUB), :]
                dqd = jnp.zeros((SUB, hd), F32)
                dkd_ = jnp.zeros((SUB, hd), F32)
                for s in range(SUB):
                    e, t_io = _hgrn_diag_e(b_s, i, s)
                    dacol = jnp.sum(jnp.where(col == SUB * i + s, dai, 0.0), axis=-1, keepdims=True)
                    w = dacol * e
                    dqd = dqd + w * k_s[pl.ds(SUB * i + s, 1), :]
                    dkd_ = dkd_ + jnp.where(t_io == s, jnp.sum(w * qi, axis=0, keepdims=True), 0.0)
                dqi_s[pl.ds(SUB * i, SUB), :] += dqd
                dki_s[pl.ds(SUB * i, SUB), :] += dkd_
            dqi, dki = dqi_s[...], dki_s[...]

            dv = _tn(a_ref[hh, c].astype(F32), do)[0:CHUNK, :] + _dot(kd, dst1, NT)
            dqb = _dot(do, st0)
            dkd = _dot(v, dst1)
            t2 = dkd * kd
            dq = dqb * eb + dqi
            dk = dkd * ekd + dki
            dbl = jnp.sum(t2, axis=0, keepdims=True) + ebl * jnp.sum(st0 * dst1, axis=0, keepdims=True)
            db = dqb * qb - t2 + q * dqi - kk * dki + jnp.where(row == CHUNK - 1, dbl, 0.0)
            dg = _dot_exact01(upper, db)
            dst[...] = dst1 * ebl + _tn(do, qb)

            df = dg / f - dk
            dhf_ref[rows, ln] = (df * (1.0 - lbv) * sf * (1.0 - sf)).astype(BF16)
            dlb_acc[...] += jnp.sum(df * (1.0 - sf), axis=0, keepdims=True)
            dhq_ref[rows, ln] = (dq * (sq * (1.0 + hq * (1.0 - sq)))).astype(BF16)
            dhi_ref[rows, ln] = dv.astype(BF16)

        def chunk(it, carry):
            c = nck - 1 - it
            rows = pl.ds(pl.multiple_of(c * CHUNK, CHUNK), CHUNK)
            for hh in range(HGRN_HEADS):
                head_chunk(hh, c, rows)
            return carry

        lax.fori_loop(0, nck, chunk, 0)

        @pl.when(pl.program_id(1) == nblk - 1)
        def _():
            dlb_ref[...] = dlb_all[...]
            dgo_ref[...] = dgo_all[...]

    hp, wide = HGRN_HEADS, HGRN_HEADS * hd

    def col(off):
        return pl.BlockSpec((rows_blk, wide), lambda b, s: (b * nblk + nblk - 1 - s, off // hp))

    out = pl.BlockSpec((rows_blk, wide), lambda b, s: (b * nblk + nblk - 1 - s, 0))
    part = pl.BlockSpec((hp, 1, hd), lambda b, s: (b, 0, 0))
    t = nb * seq
    return pl.pallas_call(
        body, name="hgrn_bwd", grid=(nb, nblk),
        in_specs=[col(12), col(16), col(20), col(24), pl.BlockSpec((1, wide), lambda b, s: (0, 0)),
                  pl.BlockSpec((1, hd), lambda b, s: (0, 0)), out,
                  pl.BlockSpec((hp, nck, hd, hd), lambda b, s: (b, nblk - 1 - s, 0, 0)),
                  pl.BlockSpec((hp, nck, CHUNK, hd), lambda b, s: (b, nblk - 1 - s, 0, 0)), col(4)],
        out_specs=[out, out, out, out, part, part],
        out_shape=[jax.ShapeDtypeStruct((t, wide), BF16)] * 4 + [jax.ShapeDtypeStruct((nb * hp, 1, hd), F32)] * 2,
        scratch_shapes=[pltpu.VMEM((hp, hd, hd), F32)] + [pltpu.VMEM((hp, CHUNK, hd), F32)] * 3
        + [pltpu.VMEM((hp, CHUNK, CHUNK), F32)] + [pltpu.VMEM((hp, CHUNK, hd), F32)] * 2
        + [pltpu.VMEM((hp, 1, hd), F32)] * 2,
        compiler_params=_params("parallel", "arbitrary"),
    )(proj, proj, proj, proj, lb, go, o_pre, states, scores, dout)


def _lb_fwd(lower_bounds):
    def body(x_ref, o_ref):
        xv = x_ref[...]
        e = jnp.exp(xv - jnp.max(xv, axis=0, keepdims=True))
        o_ref[...] = e[0:1, :] / jnp.sum(e, axis=0, keepdims=True)

    return pl.pallas_call(body, name="lb_fwd",
                          out_shape=jax.ShapeDtypeStruct((1, lower_bounds.shape[1]), F32))(lower_bounds)


def _lb_bwd(lower_bounds, dlb_parts):
    ng = dlb_parts.shape[0]

    def body(x_ref, d_ref, o_ref):
        xv = x_ref[...]
        e = jnp.exp(xv - jnp.max(xv, axis=0, keepdims=True))
        p = e / jnp.sum(e, axis=0, keepdims=True)
        dlb = d_ref[0]
        for gi in range(1, ng):
            dlb = dlb + d_ref[gi]
        first = lax.broadcasted_iota(jnp.int32, xv.shape, 0) == 0
        o_ref[...] = p * (jnp.where(first, dlb, 0.0) - p[0:1, :] * dlb)

    return pl.pallas_call(body, name="lb_bwd",
                          out_shape=jax.ShapeDtypeStruct(lower_bounds.shape, F32))(lower_bounds, dlb_parts)


def _ffn_bwd(x, g, h, gate, up, dy, dy16, w, put, tag):
    wg, wu, wd = w[tag + "_w_gate"], w[tag + "_w_up"], w[tag + "_w_down"]
    dgate, dup, act = _ffn_bwd_mid(dy16, wd, gate, up, tag + "_bwd_mid")
    put(tag + "_w_down", _mm(act, dy16, ta=True, tm=1408, tn=512, scale=0.5, name=tag + "_dwd"))
    put(tag + "_w_gate", _mm(dgate, h, ta=True, tm=1408, tn=512, name=tag + "_dwg"))
    put(tag + "_w_up", _mm(dup, h, ta=True, tm=1408, tn=512, name=tag + "_dwu"))
    dh = _mm(dgate, wg, tm=512, tn=1024, name=tag + "_dh_gate")
    return _mm(dup, wu, tm=512, tn=1024, add=dh, norm_bwd=(x, g, dy), name=tag + "_dh_up")


def _local_step(x, tgt, sp, w, put, nb, seq):
    d = x.shape[1]
    h1 = _rms_fwd(x, sp["ffn1_norm_g"], "ffn1_norm")
    rb_pad = jnp.pad(sp["attn_rel_bias"], ((0, 0), (0, N_REL_PAD - N_REL)))
    bias = jnp.transpose(_bias_expand(rb_pad), (1, 0, 2)).reshape(ATTN_HEADS * CHUNK, BAND)
    gq2 = jnp.concatenate([sp["attn_q_norm_g"]] * 2, axis=1)
    gk2 = jnp.concatenate([sp["attn_k_norm_g"]] * 2, axis=1)
    lb = _lb_fwd(sp["hgrn_lower_bounds"])
    gate1, up1, x1, h2 = _ffn_fwd(h1, x, w["ffn1_w_gate"], w["ffn1_w_up"], w["ffn1_w_down"], "ffn1_fwd",
                                  next_g=sp["mix_norm_g"])
    proj = _mm(h2, w["w_in"], tb=True, tm=256, tn=w["w_in"].shape[0], name="in_proj")
    attn = _attn_fwd(proj, bias, gq2, gk2, nb, seq)
    hy, ho, hstate, hscore = _hgrn_fwd(proj, lb, sp["hgrn_out_norm_g"], nb, seq)
    mix = jnp.concatenate([attn, hy], axis=1)
    x2, h3 = _mm(mix, w["w_out"], tm=512, tn=1024, add=x1, norm_g=sp["ffn2_norm_g"], name="out_proj")
    gate2, up2, dx3, dx3_16, sq = _ffn_fwd(h3, x2, w["ffn2_w_gate"], w["ffn2_w_up"], w["ffn2_w_down"], "ffn2_fwd",
                                           tgt=tgt)
    loss = 0.5 * jnp.sum(sq) / d

    dx2, dx2_16, dg3 = _ffn_bwd(x2, sp["ffn2_norm_g"], h3, gate2, up2, dx3, dx3_16, w, put, "ffn2")
    dmix = _mm(dx2_16, w["w_out"], tb=True, tm=512, tn=1024, name="out_proj_dx")
    put("w_out", _mm(mix, dx2_16, ta=True, tm=512, tn=1024, name="out_proj_dw"))
    dq, dk, dv, dbias, dgq, dgk = _attn_bwd(proj, attn, dmix, bias, gq2, gk2, nb, seq)
    dbias = jnp.transpose(dbias.reshape(nb, ATTN_HEADS, CHUNK, BAND), (0, 2, 1, 3))
    dgq = jnp.sum(dgq, axis=(0, 1)).reshape(2, ATTN_HEAD_DIM).sum(axis=0, keepdims=True)
    dgk = jnp.sum(dgk, axis=(0, 1)).reshape(2, ATTN_HEAD_DIM).sum(axis=0, keepdims=True)
    dhq, dhf, dhi, dhg, dlb, dgo = _hgrn_bwd(proj, lb, sp["hgrn_out_norm_g"], ho, hstate, hscore, dmix, nb, seq)
    dproj = jnp.concatenate([dq, dk, dv, dhq, dhf, dhi, dhg], axis=1)
    put("w_in", _mm(dproj, h2, ta=True, tm=512, tn=1024, name="in_proj_dw"))
    dx1, dx1_16, dgm = _mm(dproj, w["w_in"], tm=512, tn=1024, norm_bwd=(x1, sp["mix_norm_g"], dx2),
                           name="in_proj_dx")
    dx0, _, dg1 = _ffn_bwd(x, sp["ffn1_norm_g"], h1, gate1, up1, dx1, dx1_16, w, put, "ffn1")

    small = {
        "ffn1_norm_g": dg1, "mix_norm_g": dgm, "ffn2_norm_g": dg3,
        "attn_q_norm_g": dgq, "attn_k_norm_g": dgk,
        "attn_rel_bias": _bias_fold(dbias)[:, :N_REL],
        "hgrn_lower_bounds": _lb_bwd(sp["hgrn_lower_bounds"], dlb.reshape(nb, 1, HGRN_HEADS * HGRN_HEAD_DIM)),
        "hgrn_out_norm_g": jnp.sum(dgo, axis=(0, 1))[None, :],
    }
    return loss, dx0, small


MESH = pl.DeviceIdType.MESH
ANY = pl.BlockSpec(memory_space=pl.ANY)


def _coords():
    return lax.axis_index("x"), lax.axis_index("y"), lax.axis_index("c")


def _other_chips(x, y):
    return [(1 - x, y), (x, 1 - y), (1 - x, 1 - y)]


def _gather_side(shards):
    n = len(shards)

    def copies(ins, outs, sems):
        send_sems, recv_sems, local_sems = sems
        x, y, c = _coords()
        me, sibling = (x, y, c), (x, y, 1 - c)
        chips = _other_chips(x, y)

        def copy(i, k, block, to, src=None):
            bx, by, bc = block
            dst = outs[i].at[4 * bx + 2 * by + bc]
            return pltpu.make_async_remote_copy(
                src_ref=dst if src is None else src, dst_ref=dst, send_sem=send_sems.at[i, k],
                recv_sem=recv_sems.at[i, k], device_id=to, device_id_type=MESH)

        mine = [pltpu.make_async_copy(ins[i], outs[i].at[4 * x + 2 * y + c], local_sems.at[i]) for i in range(n)]
        own = []
        for i in range(n):
            own.append(copy(i, 0, me, sibling, src=ins[i]))
            own += [copy(i, 1 + j, me, (*chip, c), src=ins[i]) for j, chip in enumerate(chips)]
        return copy, mine, own, me, sibling, chips, c

    def start(ins, outs, sems):
        _, mine, own, *_ = copies(ins, outs, sems)
        for cp in mine + own:
            cp.start()

    def finish(ins, outs, sems):
        copy, mine, own, me, sibling, chips, c = copies(ins, outs, sems)
        passed = []
        for i in range(n):
            for j, chip in enumerate(chips):
                copy(i, 1 + j, (*chip, c), me).wait_recv()
                passed.append(copy(i, 4 + j, (*chip, c), sibling))
                passed[-1].start()
        for i in range(n):
            copy(i, 0, sibling, me).wait_recv()
            for j, chip in enumerate(chips):
                copy(i, 4 + j, (*chip, 1 - c), me).wait_recv()
        for cp in own + passed:
            cp.wait_send()
        for cp in mine:
            cp.wait()

    return _Side(list(shards), [jax.ShapeDtypeStruct((N_DEV,) + s.shape, s.dtype) for s in shards],
                 [pltpu.SemaphoreType.DMA((n, 7)), pltpu.SemaphoreType.DMA((n, 7)), pltpu.SemaphoreType.DMA((n,))],
                 start, finish)


def _pair_side(grads):
    n = len(grads)

    def copies(ins, outs, sems):
        send_sems, recv_sems = sems
        x, y, c = _coords()
        return [pltpu.make_async_remote_copy(
            src_ref=ins[i].at[2 * k + 1 - c], dst_ref=outs[i].at[k], send_sem=send_sems.at[i, k],
            recv_sem=recv_sems.at[i, k], device_id=(x, y, 1 - c), device_id_type=MESH)
            for i in range(n) for k in range(4)]

    def start(ins, outs, sems):
        for cp in copies(ins, outs, sems):
            cp.start()

    def finish(ins, outs, sems):
        for cp in copies(ins, outs, sems):
            cp.wait()

    return _Side(list(grads), [jax.ShapeDtypeStruct((4,) + g.shape[1:], g.dtype) for g in grads],
                 [pltpu.SemaphoreType.DMA((n, 4)), pltpu.SemaphoreType.DMA((n, 4))], start, finish)


def _pair_add(grad, recv, core, name):
    _, r, cdim = grad.shape

    def body(c_ref, g_ref, r_ref, o_ref):
        o_ref[...] = (g_ref[...] + r_ref[...]).astype(BF16)

    blk = (1, r, cdim)
    return pl.pallas_call(
        body, name=name,
        grid_spec=pltpu.PrefetchScalarGridSpec(
            num_scalar_prefetch=1, grid=(4,),
            in_specs=[pl.BlockSpec(blk, lambda k, c_ref: (2 * k + c_ref[0], 0, 0)),
                      pl.BlockSpec(blk, lambda k, c_ref: (k, 0, 0))],
            out_specs=pl.BlockSpec(blk, lambda k, c_ref: (k, 0, 0))),
        out_shape=jax.ShapeDtypeStruct((4, r, cdim), BF16),
        compiler_params=_params("arbitrary"),
    )(core, grad, recv)


def _chip_side(parts):
    n = len(parts)

    def copies(ins, outs, sems):
        send_sems, recv_sems, local_sems = sems
        x, y, c = _coords()
        chips = _other_chips(x, y)
        mine = [pltpu.make_async_copy(ins[i].at[2 * x + y], outs[i].at[2 * x + y], local_sems.at[i])
                for i in range(n)]
        sent = [pltpu.make_async_remote_copy(
            src_ref=ins[i].at[2 * px + py], dst_ref=outs[i].at[2 * x + y], send_sem=send_sems.at[i, j],
            recv_sem=recv_sems.at[i, j], device_id=(px, py, c), device_id_type=MESH)
            for i in range(n) for j, (px, py) in enumerate(chips)]
        return mine, sent, chips, c

    def start(ins, outs, sems):
        mine, sent, _, _ = copies(ins, outs, sems)
        for cp in mine + sent:
            cp.start()

    def finish(ins, outs, sems):
        mine, sent, chips, c = copies(ins, outs, sems)
        send_sems, recv_sems, _ = sems
        for i in range(n):
            for j, (px, py) in enumerate(chips):
                landed = outs[i].at[2 * px + py]
                pltpu.make_async_remote_copy(
                    src_ref=landed, dst_ref=landed, send_sem=send_sems.at[i, j], recv_sem=recv_sems.at[i, j],
                    device_id=(px, py, c), device_id_type=MESH).wait_recv()
        for cp in sent:
            cp.wait_send()
        for cp in mine:
            cp.wait()

    return _Side(list(parts), [jax.ShapeDtypeStruct(p.shape, p.dtype) for p in parts],
                 [pltpu.SemaphoreType.DMA((n, 3)), pltpu.SemaphoreType.DMA((n, 3)), pltpu.SemaphoreType.DMA((n,))],
                 start, finish)


def _all_reduce_small(v):
    r = v.shape[0]

    def body(v_ref, o_ref, buf, send_sems, recv_sems):
        x, y, c = _coords()
        me = 4 * x + 2 * y + c
        buf[me] = v_ref[...]
        cps = []
        for k in range(1, N_DEV):
            px = 1 - x if k & 4 else x
            py = 1 - y if k & 2 else y
            pc = 1 - c if k & 1 else c
            cps.append((pltpu.make_async_remote_copy(
                src_ref=v_ref, dst_ref=buf.at[me], send_sem=send_sems.at[k - 1], recv_sem=recv_sems.at[k - 1],
                device_id=(px, py, pc), device_id_type=MESH), 4 * px + 2 * py + pc))
        for cp, _ in cps:
            cp.start()
        for k, (cp, peer) in enumerate(cps):
            pltpu.make_async_remote_copy(
                src_ref=v_ref, dst_ref=buf.at[peer], send_sem=send_sems.at[k], recv_sem=recv_sems.at[k],
                device_id=(x, y, c), device_id_type=MESH).wait_recv()
        for cp, _ in cps:
            cp.wait_send()
        acc = buf[0]
        for j in range(1, N_DEV):
            acc = acc + buf[j]
        o_ref[...] = acc

    return pl.pallas_call(
        body, name="small_all_reduce", out_shape=jax.ShapeDtypeStruct(v.shape, F32),
        in_specs=[pl.BlockSpec(memory_space=pltpu.VMEM)], out_specs=pl.BlockSpec(memory_space=pltpu.VMEM),
        scratch_shapes=[pltpu.VMEM((N_DEV, r, 128), F32), pltpu.SemaphoreType.DMA((N_DEV - 1,)),
                        pltpu.SemaphoreType.DMA((N_DEV - 1,))],
    )(v)


def _adamw(w, m, v, g, name):
    parts = w.ndim == 3
    r, cdim = w.shape[-2:]
    tr = r // 4 if r % 32 == 0 else r

    def body(w_ref, m_ref, v_ref, g_ref, go_ref, d_ref, mo_ref, vo_ref):
        if parts:
            gv = g_ref[0].astype(F32)
            for k in range(1, 4):
                gv = gv + g_ref[k].astype(F32)
            gv = gv[None]
        else:
            gv = g_ref[...]
        m2 = ADAM_B1 * m_ref[...] + (1.0 - ADAM_B1) * gv
        v2 = ADAM_B2 * v_ref[...] + (1.0 - ADAM_B2) * (gv * gv)
        m_hat = m2 / (1.0 - ADAM_B1 ** ADAM_STEP)
        v_hat = v2 / (1.0 - ADAM_B2 ** ADAM_STEP)
        go_ref[...] = gv
        d_ref[...] = -ADAM_LR * (m_hat / (jnp.sqrt(v_hat) + ADAM_EPS) + ADAM_WD * w_ref[...])
        mo_ref[...] = m2
        vo_ref[...] = v2

    if parts:
        row = pl.BlockSpec((1, tr, cdim), lambda i: (0, i, 0))
        g_spec = pl.BlockSpec((4, tr, cdim), lambda i: (0, i, 0))
    else:
        row = g_spec = pl.BlockSpec((tr, cdim), lambda i: (i, 0))
    return pl.pallas_call(
        body, name=name, grid=(r // tr,), in_specs=[row, row, row, g_spec], out_specs=[row] * 4,
        out_shape=[jax.ShapeDtypeStruct(w.shape, F32)] * 4,
        compiler_params=_params("parallel"),
    )(w, m, v, g)


WEIGHTS = ["ffn1_norm_g", "ffn1_w_gate", "ffn1_w_up", "ffn1_w_down", "mix_norm_g", "w_in", "attn_q_norm_g",
           "attn_k_norm_g", "attn_rel_bias", "hgrn_lower_bounds", "hgrn_out_norm_g", "w_out", "ffn2_norm_g",
           "ffn2_w_gate", "ffn2_w_up", "ffn2_w_down"]
COL_SHARDED = ("ffn1_w_gate", "ffn1_w_up", "w_in", "ffn2_w_gate", "ffn2_w_up")
ROW_SHARDED = ("ffn1_w_down", "w_out", "ffn2_w_down")
BIG = [n for n in WEIGHTS if n in COL_SHARDED or n in ROW_SHARDED]
SMALL = [n for n in WEIGHTS if n not in BIG]
PACK_ROWS = 8
FFN2 = ["ffn2_w_down", "ffn2_w_gate", "ffn2_w_up"]
MIXER = ["w_out", "w_in"]

PLAN = {
    "ffn1_norm": [("gather", ["ffn1_w_down"])],
    "bias_expand": [("gather", ["ffn1_w_gate", "ffn1_w_up"])],
    "ffn1_fwd": [("gather", MIXER)],
    "attn_fwd": [("gather", FFN2)],
    "ffn2_dh_gate": [("pair", FFN2)],
    "attn_bwd": [("chip", FFN2)],
    "in_proj_dx": [("pair", MIXER)],
    "ffn1_bwd_mid": [("chip", MIXER)],
    "ffn1_dwg": [("pair", ["ffn1_w_down"])],
    "ffn1_dwu": [("chip", ["ffn1_w_down"]), ("pair", ["ffn1_w_gate"])],
    "ffn1_dh_gate": [("chip", ["ffn1_w_gate"]), ("pair", ["ffn1_w_up"])],
    "bias_fold": [("chip", ["ffn1_w_up"])],
}


def _join_sides(sides):
    def split(refs, counts):
        out, at = [], 0
        for n in counts:
            out.append(refs[at:at + n])
            at += n
        return out

    n_in, n_out, n_sem = ([len(getattr(s, f)) for s in sides] for f in ("ins", "out_shape", "sems"))

    def run(which):
        def go(ins, outs, sems):
            for s, i, o, m in zip(sides, split(ins, n_in), split(outs, n_out), split(sems, n_sem)):
                getattr(s, which)(i, o, m)
        return go

    return _Side([a for s in sides for a in s.ins], [a for s in sides for a in s.out_shape],
                 [a for s in sides for a in s.sems], run("start"), run("finish"))


class _Schedule:
    def __init__(self, shards):
        self.shards = shards
        self.weights = {}
        self.sliced = {}
        self.partials = {}
        self.reduced = {}

    def put(self, name, grad):
        self.sliced[name] = grad.reshape((N_DEV,) + self.shards[name].shape)

    def side_for(self, call):
        if call not in PLAN:
            return None
        sides = []
        for kind, names in PLAN[call]:
            if kind == "gather":
                sides.append(_gather_side([self.shards[n] for n in names]))
            elif kind == "pair":
                sides.append(_pair_side([self.sliced[n] for n in names]))
            else:
                sides.append(_chip_side([self.partials[n] for n in names]))
        return _join_sides(sides)

    def done(self, call, outs):
        at = 0
        for kind, names in PLAN[call]:
            self.file(kind, names, outs[at:at + len(names)])
            at += len(names)

    def file(self, kind, names, outs):
        for n, o in zip(names, outs):
            if kind == "gather":
                self.weights[n] = o.reshape(N_DEV * o.shape[1], o.shape[2])
            elif kind == "pair":
                core = lax.axis_index("c").astype(jnp.int32).reshape(1)
                self.partials[n] = _pair_add(self.sliced[n], o, core, n + "_pair_add")
            else:
                self.reduced[n] = o


def _pack_small(vals, loss=None):
    parts = []
    for n in SMALL:
        a = vals[n]
        if n == "attn_rel_bias":
            a = jnp.pad(a.reshape(ATTN_HEADS, N_REL), ((0, 0), (0, N_REL_PAD - N_REL)))
        flat = a.reshape(-1)
        size = -(-flat.shape[0] // (PACK_ROWS * 128)) * PACK_ROWS * 128
        parts.append(jnp.pad(flat, (0, size - flat.shape[0])).reshape(-1, 128))
    tail = jnp.zeros((PACK_ROWS, 128), F32)
    if loss is not None:
        tail = tail.at[0, 0].set(loss)
    return jnp.concatenate(parts + [tail], axis=0)


def _unpack_small(packed, shapes):
    out, row = {}, 0
    for n in SMALL:
        shape = shapes[n]
        if n == "attn_rel_bias":
            rows = ATTN_HEADS * N_REL_PAD // 128
            out[n] = packed[row:row + rows].reshape(ATTN_HEADS, N_REL_PAD)[:, :N_REL].reshape(shape)
        else:
            size = 1
            for s in shape:
                size *= s
            rows = -(-size // (PACK_ROWS * 128)) * PACK_ROWS
            out[n] = packed[row:row + rows].reshape(-1)[:size].reshape(shape)
        row += rows
    return out, packed[row, 0]


def kernel(x, ffn1_norm_g, ffn1_w_gate, ffn1_w_up, ffn1_w_down, mix_norm_g, w_in, attn_q_norm_g, attn_k_norm_g, attn_rel_bias, hgrn_lower_bounds, hgrn_out_norm_g, w_out, ffn2_norm_g, ffn2_w_gate, ffn2_w_up, ffn2_w_down, loss_target, m_ffn1_norm_g, m_ffn1_w_gate, m_ffn1_w_up, m_ffn1_w_down, m_mix_norm_g, m_w_in, m_attn_q_norm_g, m_attn_k_norm_g, m_attn_rel_bias, m_hgrn_lower_bounds, m_hgrn_out_norm_g, m_w_out, m_ffn2_norm_g, m_ffn2_w_gate, m_ffn2_w_up, m_ffn2_w_down, v_ffn1_norm_g, v_ffn1_w_gate, v_ffn1_w_up, v_ffn1_w_down, v_mix_norm_g, v_w_in, v_attn_q_norm_g, v_attn_k_norm_g, v_attn_rel_bias, v_hgrn_lower_bounds, v_hgrn_out_norm_g, v_w_out, v_ffn2_norm_g, v_ffn2_w_gate, v_ffn2_w_up, v_ffn2_w_down):
    wts = dict(zip(WEIGHTS, (ffn1_norm_g, ffn1_w_gate, ffn1_w_up, ffn1_w_down, mix_norm_g, w_in, attn_q_norm_g,
                             attn_k_norm_g, attn_rel_bias, hgrn_lower_bounds, hgrn_out_norm_g, w_out, ffn2_norm_g,
                             ffn2_w_gate, ffn2_w_up, ffn2_w_down)))
    mom = dict(zip(WEIGHTS, (m_ffn1_norm_g, m_ffn1_w_gate, m_ffn1_w_up, m_ffn1_w_down, m_mix_norm_g, m_w_in,
                             m_attn_q_norm_g, m_attn_k_norm_g, m_attn_rel_bias, m_hgrn_lower_bounds,
                             m_hgrn_out_norm_g, m_w_out, m_ffn2_norm_g, m_ffn2_w_gate, m_ffn2_w_up, m_ffn2_w_down)))
    var = dict(zip(WEIGHTS, (v_ffn1_norm_g, v_ffn1_w_gate, v_ffn1_w_up, v_ffn1_w_down, v_mix_norm_g, v_w_in,
                             v_attn_q_norm_g, v_attn_k_norm_g, v_attn_rel_bias, v_hgrn_lower_bounds,
                             v_hgrn_out_norm_g, v_w_out, v_ffn2_norm_g, v_ffn2_w_gate, v_ffn2_w_up, v_ffn2_w_down)))
    nb, seq, d = x.shape
    shapes = {n: wts[n].shape for n in WEIGHTS}

    def rows_first(a, n):
        return jnp.swapaxes(a, 1, 2) if n in COL_SHARDED else a

    sched = _Schedule({n: rows_first(wts[n], n)[0].astype(BF16) for n in BIG})
    sp = {n: wts[n] for n in SMALL}
    sp["attn_rel_bias"] = wts["attn_rel_bias"][0]
    _ACTIVE[0] = sched
    try:
        loss, dx, dsmall = _local_step(x.reshape(nb * seq, d), loss_target.reshape(nb * seq, d), sp,
                                       sched.weights, sched.put, nb, seq)
    finally:
        _ACTIVE[0] = None
    reduced = sched.reduced

    small_sum = _all_reduce_small(_pack_small(dsmall, loss))
    gsmall, loss_total = _unpack_small(small_sum, shapes)

    grads, deltas, new_m, new_v = {}, {}, {}, {}
    for n in BIG:
        out = _adamw(rows_first(wts[n], n), rows_first(mom[n], n), rows_first(var[n], n), reduced[n], n + "_adamw")
        grads[n], deltas[n], new_m[n], new_v[n] = (rows_first(o, n) for o in out)
    packed = _adamw(_pack_small(wts), _pack_small(mom), _pack_small(var), small_sum, "small_adamw")
    for dst, p in zip((deltas, new_m, new_v), packed[1:]):
        dst.update(_unpack_small(p, shapes)[0])
    grads.update(gsmall)

    return (loss_total, dx.reshape(nb, seq, d), *[grads[n] for n in WEIGHTS], *[deltas[n] for n in WEIGHTS],
            *[new_m[n] for n in WEIGHTS], *[new_v[n] for n in WEIGHTS])
```

```python
import functools

import jax
import jax.numpy as jnp
from jax import lax
from jax.experimental import pallas as pl
from jax.experimental.pallas import tpu as pltpu

F32 = jnp.float32
BF16 = jnp.bfloat16

RMS_EPS = 1e-6
CHUNK = 64
LEFT_CHUNKS = 8
BAND = (LEFT_CHUNKS + 2) * CHUNK
KPAD = BAND - CHUNK
REL_CLIP = 128
N_REL = 2 * REL_CLIP + 1
N_REL_PAD = 384
ATTN_HEADS = 8
ATTN_HEAD_DIM = 64
ATTN_WIDTH = ATTN_HEADS * ATTN_HEAD_DIM
ATTN_UNROLL = 8
HGRN_HEADS = 4
HGRN_HEAD_DIM = 128
HGRN_ROWS = 512
SUB = 16
N_SUB = CHUNK // SUB
N_DEV = 8

ADAM_LR = 0.001
ADAM_B1 = 0.9
ADAM_B2 = 0.999
ADAM_EPS = 1e-08
ADAM_WD = 0.01
ADAM_STEP = 10

VMEM_LIMIT = 56 * 1024 * 1024
NT = (((1,), (1,)), ((), ()))
NN = (((1,), (0,)), ((), ()))


def _params(*sem):
    return pltpu.CompilerParams(dimension_semantics=sem, vmem_limit_bytes=VMEM_LIMIT)


def _sigmoid(v):
    return 0.5 * jnp.tanh(0.5 * v) + 0.5


def _dot(a, b, dims=NN):
    return lax.dot_general(a.astype(BF16), b.astype(BF16), dims, preferred_element_type=F32)


def _dot_exact01(m01, v):
    m = m01.astype(BF16)
    hi = v.astype(BF16)
    r1 = v - hi.astype(F32)
    mid = r1.astype(BF16)
    lo = (r1 - mid.astype(F32)).astype(BF16)
    out = lax.dot_general(m, hi, NN, preferred_element_type=F32)
    out = out + lax.dot_general(m, mid, NN, preferred_element_type=F32)
    return out + lax.dot_general(m, lo, NN, preferred_element_type=F32)


def _dot_exact01_r(v, m01):
    m = m01.astype(BF16)
    hi = v.astype(BF16)
    r1 = v - hi.astype(F32)
    mid = r1.astype(BF16)
    lo = (r1 - mid.astype(F32)).astype(BF16)
    out = lax.dot_general(hi, m, NN, preferred_element_type=F32)
    out = out + lax.dot_general(mid, m, NN, preferred_element_type=F32)
    return out + lax.dot_general(lo, m, NN, preferred_element_type=F32)


def _row_sums_on_lanes(v):
    ones = jnp.ones((8, v.shape[1]), BF16)
    hi = v.astype(BF16)
    r1 = v - hi.astype(F32)
    mid = r1.astype(BF16)
    lo = (r1 - mid.astype(F32)).astype(BF16)
    out = lax.dot_general(ones, hi, NT, preferred_element_type=F32)
    out = out + lax.dot_general(ones, mid, NT, preferred_element_type=F32)
    return (out + lax.dot_general(ones, lo, NT, preferred_element_type=F32))[0:1, :]


def _tn(a, b):
    ap = jnp.concatenate([a, jnp.zeros_like(a)], axis=0)
    bp = jnp.concatenate([b, jnp.zeros_like(b)], axis=0)
    return _dot(ap.T, bp)


def _row_tile(t):
    for tm in (512, 256, 128, 64, 32, 16, 8):
        if t % tm == 0:
            return tm
    raise ValueError(t)


class _Side:
    def __init__(self, ins, out_shape, sems, start, finish):
        self.ins, self.out_shape, self.sems, self.start, self.finish = ins, out_shape, sems, start, finish


_ACTIVE = [None]


def _pallas(body, *, name, grid, in_specs, out_specs, out_shape, scratch_shapes=(), sem, args):
    sched = _ACTIVE[0]
    side = sched.side_for(name) if sched is not None else None
    if side is None:
        return pl.pallas_call(
            body, name=name, grid=grid, in_specs=list(in_specs), out_specs=list(out_specs),
            out_shape=list(out_shape), scratch_shapes=list(scratch_shapes), compiler_params=_params(*sem))(*args)
    cuts = [len(in_specs), len(side.ins), len(out_shape), len(side.out_shape), len(scratch_shapes)]

    def with_side(*refs):
        groups, at = [], 0
        for n in cuts:
            groups.append(refs[at:at + n])
            at += n
        ins, side_ins, outs, side_outs, scratch = groups
        side_sems = refs[at:]
        first = pl.program_id(0) == 0
        last = pl.program_id(0) == grid[0] - 1
        for a in range(1, len(grid)):
            first = jnp.logical_and(first, pl.program_id(a) == 0)
            last = jnp.logical_and(last, pl.program_id(a) == grid[a] - 1)

        @pl.when(first)
        def _():
            side.start(side_ins, side_outs, side_sems)

        body(*ins, *outs, *scratch)

        @pl.when(last)
        def _():
            side.finish(side_ins, side_outs, side_sems)

    hbm = pl.BlockSpec(memory_space=pl.ANY)
    res = pl.pallas_call(
        with_side, name=name, grid=grid, in_specs=list(in_specs) + [hbm] * len(side.ins),
        out_specs=list(out_specs) + [hbm] * len(side.out_shape), out_shape=list(out_shape) + list(side.out_shape),
        scratch_shapes=list(scratch_shapes) + list(side.sems),
        compiler_params=_params(*(["arbitrary"] * len(grid))))(*args, *side.ins)
    sched.done(name, res[len(out_shape):])
    return res[:len(out_shape)]


def _rms_fwd(x, g, name):
    t, d = x.shape
    tm = _row_tile(t)

    def body(x_ref, g_ref, h_ref):
        xv = x_ref[...]
        r = lax.rsqrt(jnp.mean(xv * xv, axis=-1, keepdims=True) + RMS_EPS)
        h_ref[...] = (xv * r * g_ref[...]).astype(BF16)

    return _pallas(
        body, name=name, grid=(t // tm,),
        in_specs=[pl.BlockSpec((tm, d), lambda i: (i, 0)), pl.BlockSpec((1, d), lambda i: (0, 0))],
        out_specs=[pl.BlockSpec((tm, d), lambda i: (i, 0))], out_shape=[jax.ShapeDtypeStruct((t, d), BF16)],
        sem=("parallel",), args=(x, g))[0]


def _accumulate(ref, part, step):
    @pl.when(step == 0)
    def _():
        ref[...] = part

    @pl.when(step > 0)
    def _():
        ref[...] += part


def _mm(a, b, *, ta=False, tb=False, tm, tn, out_dtype=F32, add=None, scale=1.0, norm_g=None, norm_bwd=None, name):
    m, k = (a.shape[1], a.shape[0]) if ta else a.shape
    n = b.shape[0] if tb else b.shape[1]
    tm, tn = min(tm, m), min(tn, n)
    assert m % tm == 0 and n % tn == 0, (m, n, tm, tn)
    assert (norm_g is None and norm_bwd is None) or tn == n
    dims = (((0 if ta else 1,), (1 if tb else 0,)), ((), ()))
    n_in = 2 + (add is not None) + (norm_g is not None) + (3 if norm_bwd is not None else 0)

    def body(*refs):
        ins, outs = list(refs[2:n_in]), refs[n_in:]
        r = lax.dot_general(refs[0][...].astype(BF16), refs[1][...].astype(BF16), dims, preferred_element_type=F32)
        if scale != 1.0:
            r = r * scale
        if add is not None:
            r = r + ins.pop(0)[...]
        if norm_bwd is not None:
            xv, gv, dres = (ref[...] for ref in ins)
            rs = lax.rsqrt(jnp.mean(xv * xv, axis=-1, keepdims=True) + RMS_EPS)
            xhat = xv * rs
            gd = r * gv
            dx = dres + rs * (gd - xhat * jnp.mean(gd * xhat, axis=-1, keepdims=True))
            outs[0][...] = dx
            outs[1][...] = dx.astype(BF16)
            _accumulate(outs[2], jnp.sum(r * xhat, axis=0, keepdims=True), pl.program_id(0))
            return
        outs[0][...] = r.astype(out_dtype)
        if norm_g is not None:
            rs = lax.rsqrt(jnp.mean(r * r, axis=-1, keepdims=True) + RMS_EPS)
            outs[1][...] = (r * rs * ins.pop(0)[...]).astype(BF16)

    a_spec = pl.BlockSpec((k, tm), lambda i, j: (0, i)) if ta else pl.BlockSpec((tm, k), lambda i, j: (i, 0))
    b_spec = pl.BlockSpec((tn, k), lambda i, j: (j, 0)) if tb else pl.BlockSpec((k, tn), lambda i, j: (0, j))
    o_spec = pl.BlockSpec((tm, tn), lambda i, j: (i, j))
    vec = pl.BlockSpec((1, tn), lambda i, j: (0, j))
    args, specs = [a, b], [a_spec, b_spec]
    if add is not None:
        args.append(add)
        specs.append(o_spec)
    out_specs, out_shape = [o_spec], [jax.ShapeDtypeStruct((m, n), out_dtype)]
    if norm_g is not None:
        args.append(norm_g)
        specs.append(vec)
        out_specs.append(o_spec)
        out_shape.append(jax.ShapeDtypeStruct((m, n), BF16))
    if norm_bwd is not None:
        args += list(norm_bwd)
        specs += [o_spec, vec, o_spec]
        out_specs = [o_spec, o_spec, vec]
        out_shape = [jax.ShapeDtypeStruct((m, n), F32), jax.ShapeDtypeStruct((m, n), BF16),
                     jax.ShapeDtypeStruct((1, n), F32)]
    res = _pallas(body, name=name, grid=(m // tm, n // tn), in_specs=specs, out_specs=out_specs, out_shape=out_shape,
                  sem=("arbitrary", "arbitrary") if norm_bwd is not None else ("parallel", "parallel"), args=args)
    return res[0] if len(res) == 1 else res


def _ffn_tile(f):
    for tf in (1408, 512, 256, 128):
        if f % tf == 0:
            return tf
    raise ValueError(f)


def _ffn_fwd(h, x, wg, wu, wd, name, next_g=None, tgt=None):
    t, d = x.shape
    f = wg.shape[0]
    tm, tf = _row_tile(t), _ffn_tile(f)
    nf = f // tf
    assert (next_g is None) != (tgt is None)

    def body(h_ref, x_ref, wg_ref, wu_ref, wd_ref, tail_ref, g_ref, u_ref, o0_ref, o1_ref, *rest):
        acc_ref = rest[-1]
        j = pl.program_id(1)
        hv = h_ref[...]
        gv = lax.dot_general(hv, wg_ref[...], NT, preferred_element_type=F32)
        uv = lax.dot_general(hv, wu_ref[...], NT, preferred_element_type=F32)
        av = gv * _sigmoid(gv) * uv
        g_ref[...] = gv.astype(BF16)
        u_ref[...] = uv.astype(BF16)
        _accumulate(acc_ref, lax.dot_general(av.astype(BF16), wd_ref[...], NN, preferred_element_type=F32), j)

        @pl.when(j == nf - 1)
        def _():
            y = x_ref[...] + 0.5 * acc_ref[...]
            if tgt is None:
                o0_ref[...] = y
                rs = lax.rsqrt(jnp.mean(y * y, axis=-1, keepdims=True) + RMS_EPS)
                o1_ref[...] = (y * rs * tail_ref[...]).astype(BF16)
            else:
                e = y - tail_ref[...]
                dy = e * (1.0 / d)
                o0_ref[...] = dy
                o1_ref[...] = dy.astype(BF16)
                _accumulate(rest[0], jnp.sum(e * e, axis=0, keepdims=True), pl.program_id(0))

    row = pl.BlockSpec((tm, d), lambda i, j: (i, 0))
    hid = pl.BlockSpec((tm, tf), lambda i, j: (i, j))
    vec = pl.BlockSpec((1, d), lambda i, j: (0, 0))
    out_specs = [hid, hid, row, row] + ([vec] if tgt is not None else [])
    out_shape = [jax.ShapeDtypeStruct((t, f), BF16)] * 2 + [jax.ShapeDtypeStruct((t, d), F32),
                                                            jax.ShapeDtypeStruct((t, d), BF16)]
    if tgt is not None:
        out_shape.append(jax.ShapeDtypeStruct((1, d), F32))
    return _pallas(
        body, name=name, grid=(t // tm, nf),
        in_specs=[row, row] + [pl.BlockSpec((tf, d), lambda i, j: (j, 0))] * 3 + [vec if tgt is None else row],
        out_specs=out_specs, out_shape=out_shape, scratch_shapes=[pltpu.VMEM((tm, d), F32)],
        sem=("parallel" if tgt is None else "arbitrary", "arbitrary"),
        args=(h, x, wg, wu, wd, next_g if tgt is None else tgt))


def _ffn_bwd_mid(dy, wd, g, u, name):
    t, d = dy.shape
    f = wd.shape[0]
    tm, tf = _row_tile(t), _ffn_tile(f)

    def body(dy_ref, wd_ref, g_ref, u_ref, dg_ref, du_ref, a_ref):
        da = 0.5 * lax.dot_general(dy_ref[...].astype(BF16), wd_ref[...], NT, preferred_element_type=F32)
        gv = g_ref[...].astype(F32)
        uv = u_ref[...].astype(F32)
        s = _sigmoid(gv)
        silu = gv * s
        dg_ref[...] = (da * uv * (s * (1.0 + gv * (1.0 - s)))).astype(BF16)
        du_ref[...] = (da * silu).astype(BF16)
        a_ref[...] = (silu * uv).astype(BF16)

    hid = pl.BlockSpec((tm, tf), lambda i, j: (i, j))
    return _pallas(
        body, name=name, grid=(t // tm, f // tf),
        in_specs=[pl.BlockSpec((tm, d), lambda i, j: (i, 0)), pl.BlockSpec((tf, d), lambda i, j: (j, 0)), hid, hid],
        out_specs=[hid, hid, hid], out_shape=[jax.ShapeDtypeStruct((t, f), BF16)] * 3,
        sem=("parallel", "parallel"), args=(dy, wd, g, u))


def _rel_index(t, s_band):
    return jnp.clip(t + KPAD - s_band, -REL_CLIP, REL_CLIP) + REL_CLIP


def _bias_expand(rel_bias_pad):
    nh = rel_bias_pad.shape[0]

    def body(rb_ref, out_ref):
        rb = rb_ref[...]
        i_io = lax.broadcasted_iota(jnp.int32, (N_REL_PAD, BAND), 0)
        s_io = lax.broadcasted_iota(jnp.int32, (N_REL_PAD, BAND), 1)

        def row(t, carry):
            onehot = (i_io == _rel_index(t, s_io)).astype(F32)
            out_ref[t] = _dot_exact01_r(rb, onehot)
            return carry

        lax.fori_loop(0, CHUNK, row, 0)

    return _pallas(
        body, name="bias_expand", grid=(1,), in_specs=[pl.BlockSpec(rel_bias_pad.shape, lambda i: (0, 0))],
        out_specs=[pl.BlockSpec((CHUNK, nh, BAND), lambda i: (0, 0, 0))],
        out_shape=[jax.ShapeDtypeStruct((CHUNK, nh, BAND), F32)], sem=("arbitrary",), args=(rel_bias_pad,))[0]


def _bias_fold(dbias):
    ng, nh = dbias.shape[0], dbias.shape[2]

    def body(db_ref, out_ref):
        s_io = lax.broadcasted_iota(jnp.int32, (BAND, N_REL_PAD), 0)
        i_io = lax.broadcasted_iota(jnp.int32, (BAND, N_REL_PAD), 1)

        def row(t, acc):
            onehot = (i_io == _rel_index(t, s_io)).astype(F32)
            d = db_ref[0, t]
            for gi in range(1, ng):
                d = d + db_ref[gi, t]
            return acc + _dot_exact01_r(d, onehot)

        out_ref[...] = lax.fori_loop(0, CHUNK, row, jnp.zeros((nh, N_REL_PAD), F32))

    return _pallas(
        body, name="bias_fold", grid=(1,), in_specs=[pl.BlockSpec(dbias.shape, lambda i: (0, 0, 0, 0))],
        out_specs=[pl.BlockSpec((nh, N_REL_PAD), lambda i: (0, 0))],
        out_shape=[jax.ShapeDtypeStruct((nh, N_REL_PAD), F32)], sem=("arbitrary",), args=(dbias,))[0]


def _left_half(shape):
    return lax.broadcasted_iota(jnp.int32, shape, len(shape) - 1) < ATTN_HEAD_DIM


def _stack_heads(v):
    left = _left_half(v.shape)
    zero = jnp.zeros_like(v)
    return jnp.concatenate([jnp.where(left, v, zero), jnp.where(left, zero, v)], axis=0)


def _unstack_heads(v):
    return jnp.where(_left_half((CHUNK, 128)), v[0:CHUNK, :], v[CHUNK:2 * CHUNK, :])


def _half_mean(v):
    r = lax.broadcasted_iota(jnp.int32, (128, 128), 0) < ATTN_HEAD_DIM
    c = lax.broadcasted_iota(jnp.int32, (128, 128), 1) < ATTN_HEAD_DIM
    return _dot_exact01_r(v, r == c) * (1.0 / ATTN_HEAD_DIM)


def _attn_prepare(q_ref, k_ref, v_ref, gq_ref, gk_ref, qs_scr, k_scr, v_scr):
    q, k = q_ref[...], k_ref[...]
    rq = lax.rsqrt(_half_mean(q * q) + RMS_EPS)
    rk = lax.rsqrt(_half_mean(k * k) + RMS_EPS)
    qhat, khat = q * rq, k * rk
    qs_scr[...] = (qhat * gq_ref[...] * ATTN_HEAD_DIM ** -0.5).astype(BF16)
    k_scr[0:KPAD, :] = jnp.zeros((KPAD, 128), BF16)
    v_scr[0:KPAD, :] = jnp.zeros((KPAD, 128), BF16)
    k_scr[KPAD:, :] = (khat * gk_ref[...]).astype(BF16)
    v_scr[KPAD:, :] = v_ref[...].astype(BF16)
    return qhat, rq, khat, rk


def _first_key(c):
    return jnp.maximum(CHUNK, (LEFT_CHUNKS + 1 - c) * CHUNK)


def _attn_fwd(proj, bias, gq, gk, nb, seq):
    nc = seq // CHUNK

    def body(q_ref, k_ref, v_ref, bias_ref, gq_ref, gk_ref, o_ref, qs_scr, k_scr, v_scr):
        _attn_prepare(q_ref, k_ref, v_ref, gq_ref, gk_ref, qs_scr, k_scr, v_scr)

        def chunk(c, carry):
            r0 = pl.multiple_of(c * CHUNK, CHUNK)
            qst = _stack_heads(qs_scr[pl.ds(r0, CHUNK), :])
            s = lax.dot_general(qst, k_scr[pl.ds(r0, BAND), :], NT, preferred_element_type=F32) + bias_ref[...]
            col = lax.broadcasted_iota(jnp.int32, (2 * CHUNK, BAND), 1)
            s = jnp.where(col >= _first_key(c), s, -jnp.inf)
            e = jnp.exp(s - jnp.max(s, axis=-1, keepdims=True))
            inv = 1.0 / jnp.sum(e, axis=-1, keepdims=True)
            o_ref[pl.ds(r0, CHUNK), :] = _unstack_heads(
                lax.dot_general(e.astype(BF16), v_scr[pl.ds(r0, BAND), :], NN, preferred_element_type=F32) * inv)
            return carry

        lax.fori_loop(0, nc, chunk, 0, unroll=min(ATTN_UNROLL, nc))

    def col(off):
        return pl.BlockSpec((seq, 128), lambda b, hp: (b, off + hp))

    vec = pl.BlockSpec((1, 128), lambda b, hp: (0, 0))
    return _pallas(
        body, name="attn_fwd", grid=(nb, ATTN_HEADS // 2),
        in_specs=[col(0), col(4), col(8), pl.BlockSpec((2 * CHUNK, BAND), lambda b, hp: (hp, 0)), vec, vec],
        out_specs=[pl.BlockSpec((seq, 128), lambda b, hp: (b, hp))],
        out_shape=[jax.ShapeDtypeStruct((nb * seq, ATTN_WIDTH), F32)],
        scratch_shapes=[pltpu.VMEM((seq, 128), BF16), pltpu.VMEM((seq + KPAD, 128), BF16),
                        pltpu.VMEM((seq + KPAD, 128), BF16)],
        sem=("parallel", "parallel"), args=(proj, proj, proj, bias, gq, gk))[0]


def _attn_bwd(proj, out, dout, bias, gq, gk, nb, seq):
    nc = seq // CHUNK
    scale = ATTN_HEAD_DIM ** -0.5

    def body(q_ref, k_ref, v_ref, o_ref, do_ref, bias_ref, gq_ref, gk_ref,
             dq_ref, dk_ref, dv_ref, dbias_ref, dgq_ref, dgk_ref,
             qs_scr, k_scr, v_scr, dqn_scr, dk_scr, dv_scr, db_scr):
        qhat, rq, khat, rk = _attn_prepare(q_ref, k_ref, v_ref, gq_ref, gk_ref, qs_scr, k_scr, v_scr)
        dk_scr[...] = jnp.zeros_like(dk_scr)
        dv_scr[...] = jnp.zeros_like(dv_scr)
        db_scr[...] = jnp.zeros_like(db_scr)

        def chunk(c, carry):
            r0 = pl.multiple_of(c * CHUNK, CHUNK)
            qst = _stack_heads(qs_scr[pl.ds(r0, CHUNK), :])
            kb = k_scr[pl.ds(r0, BAND), :]
            vb = v_scr[pl.ds(r0, BAND), :]
            st = lax.dot_general(kb, qst, NT, preferred_element_type=F32) + bias_ref[...]
            key = lax.broadcasted_iota(jnp.int32, (BAND, 2 * CHUNK), 0)
            st = jnp.where(key >= _first_key(c), st, -jnp.inf)
            et = jnp.exp(st - jnp.max(st, axis=0, keepdims=True))
            pt = et * (1.0 / jnp.sum(et, axis=0, keepdims=True))
            dost = _stack_heads(do_ref[pl.ds(r0, CHUNK), :])
            prod = dost * _stack_heads(o_ref[pl.ds(r0, CHUNK), :])
            drow = _row_sums_on_lanes(prod)
            dost16 = dost.astype(BF16)
            dpt = lax.dot_general(vb, dost16, NT, preferred_element_type=F32)
            dst = pt * (dpt - drow)
            db_scr[...] += dst
            dst16 = dst.astype(BF16)
            dqn_scr[pl.ds(r0, CHUNK), :] = scale * _unstack_heads(_dot(dst.T, kb))
            dk_scr[pl.ds(r0, BAND), :] += lax.dot_general(dst16, qst, NN, preferred_element_type=F32)
            dv_scr[pl.ds(r0, BAND), :] += lax.dot_general(pt.astype(BF16), dost16, NN, preferred_element_type=F32)
            return carry

        lax.fori_loop(0, nc, chunk, 0, unroll=min(ATTN_UNROLL, nc))

        def norm_bwd(dn, hat, r, g_ref):
            gd = dn * g_ref[...]
            return r * (gd - hat * _half_mean(gd * hat)), jnp.sum(dn * hat, axis=0, keepdims=True)

        dq, dgq = norm_bwd(dqn_scr[...], qhat, rq, gq_ref)
        dk, dgk = norm_bwd(dk_scr[KPAD:, :], khat, rk, gk_ref)
        dq_ref[...] = dq.astype(BF16)
        dk_ref[...] = dk.astype(BF16)
        dv_ref[...] = dv_scr[KPAD:, :].astype(BF16)
        dbias_ref[0] = db_scr[...]
        dgq_ref[0] = dgq
        dgk_ref[0] = dgk

    def col(off):
        return pl.BlockSpec((seq, 128), lambda b, hp: (b, off + hp))

    vec = pl.BlockSpec((1, 128), lambda b, hp: (0, 0))
    gvec = pl.BlockSpec((1, 1, 128), lambda b, hp: (b * (ATTN_HEADS // 2) + hp, 0, 0))
    t = nb * seq
    return _pallas(
        body, name="attn_bwd", grid=(nb, ATTN_HEADS // 2),
        in_specs=[col(0), col(4), col(8), col(0), col(0),
                  pl.BlockSpec((BAND, 2 * CHUNK), lambda b, hp: (hp, 0)), vec, vec],
        out_specs=[col(0), col(0), col(0), pl.BlockSpec((1, BAND, 2 * CHUNK), lambda b, hp: (b, hp, 0)),
                   gvec, gvec],
        out_shape=[jax.ShapeDtypeStruct((t, ATTN_WIDTH), BF16)] * 3
        + [jax.ShapeDtypeStruct((nb, ATTN_HEADS // 2 * BAND, 2 * CHUNK), F32)]
        + [jax.ShapeDtypeStruct((nb * ATTN_HEADS // 2, 1, 128), F32)] * 2,
        scratch_shapes=[pltpu.VMEM((seq, 128), BF16), pltpu.VMEM((seq + KPAD, 128), BF16),
                        pltpu.VMEM((seq + KPAD, 128), BF16), pltpu.VMEM((seq, 128), F32),
                        pltpu.VMEM((seq + KPAD, 128), F32), pltpu.VMEM((seq + KPAD, 128), F32),
                        pltpu.VMEM((BAND, 2 * CHUNK), F32)],
        sem=("parallel", "parallel"), args=(proj, proj, proj, out, dout, bias, gq, gk))


def _tri(lower):
    r = lax.broadcasted_iota(jnp.int32, (CHUNK, CHUNK), 0)
    c = lax.broadcasted_iota(jnp.int32, (CHUNK, CHUNK), 1)
    return (r >= c) if lower else (r <= c)


def _hgrn_gates(hq, hf, lb):
    sq = _sigmoid(hq)
    sf = _sigmoid(hf)
    return hq * sq, sq, sf, lb + (1.0 - lb) * sf


def _hgrn_offdiag(q_s, k_s, b_s):
    row = lax.broadcasted_iota(jnp.int32, (CHUNK, HGRN_HEAD_DIM), 0)
    bv, qv, kv = b_s[...], q_s[...], k_s[...]
    eqs, eks = [], []
    for i in range(1, N_SUB):
        r = b_s[pl.ds(SUB * i - 1, 1), :]
        in_i = (row >= SUB * i) & (row < SUB * (i + 1))
        eqs.append(jnp.exp(jnp.where(in_i, bv - r, -jnp.inf)))
        eks.append(jnp.exp(jnp.where(row < SUB * i, r - bv, -jnp.inf)))
    eq = jnp.concatenate(eqs, axis=1)
    ek = jnp.concatenate(eks, axis=1)
    qt = jnp.concatenate([qv] * (N_SUB - 1), axis=1) * eq
    kt = jnp.concatenate([kv] * (N_SUB - 1), axis=1) * ek
    return qt, kt, eq, ek


def _hgrn_diag_e(b_s, i, s):
    t_io = lax.broadcasted_iota(jnp.int32, (SUB, HGRN_HEAD_DIM), 0)
    bi = b_s[pl.ds(SUB * i, SUB), :]
    return jnp.exp(jnp.where(t_io >= s, bi - b_s[pl.ds(SUB * i + s, 1), :], -jnp.inf)), t_io


def _hgrn_intra(q_s, k_s, b_s, a_s, qt, kt):
    ktp = jnp.concatenate([kt, jnp.zeros_like(kt)], axis=0)
    a_s[...] = _dot(qt, ktp, NT)
    col = lax.broadcasted_iota(jnp.int32, (SUB, HGRN_HEAD_DIM), 1)
    for i in range(N_SUB):
        qi = q_s[pl.ds(SUB * i, SUB), :]
        ai = jnp.zeros((SUB, HGRN_HEAD_DIM), F32)
        for s in range(SUB):
            e, _ = _hgrn_diag_e(b_s, i, s)
            a_col = jnp.sum(qi * k_s[pl.ds(SUB * i + s, 1), :] * e, axis=-1, keepdims=True)
            ai = ai + jnp.where(col == SUB * i + s, a_col, 0.0)
        a_s[pl.ds(SUB * i, SUB), :] += ai


def _hgrn_fwd(proj, lb, go, nb, seq):
    nc = seq // CHUNK
    hd = HGRN_HEAD_DIM
    rows_blk = min(HGRN_ROWS, seq)
    nblk, nck = seq // rows_blk, rows_blk // CHUNK

    def body(hq_ref, hf_ref, hi_ref, hg_ref, lb_ref, go_ref, y_ref, o_ref, st_ref, a_ref,
             st_all, q_all, k_all, b_all, a_all):
        @pl.when(pl.program_id(1) == 0)
        def _():
            st_all[...] = jnp.zeros_like(st_all)

        lower = _tri(True)

        def head_chunk(hh, c, rows):
            ln = slice(hd * hh, hd * (hh + 1))
            st, q_s, k_s, b_s, a_s = st_all.at[hh], q_all.at[hh], k_all.at[hh], b_all.at[hh], a_all.at[hh]
            q, _, _, f = _hgrn_gates(hq_ref[rows, ln], hf_ref[rows, ln], lb_ref[:, ln])
            v = hi_ref[rows, ln]
            b = _dot_exact01(lower, jnp.log(f))
            q_s[...] = q
            k_s[...] = 1.0 - f
            b_s[...] = b
            st_ref[hh, c] = st[...]
            qt, kt, _, _ = _hgrn_offdiag(q_s, k_s, b_s)
            _hgrn_intra(q_s, k_s, b_s, a_s, qt, kt)
            a16 = a_s[...].astype(BF16)
            a_ref[hh, c] = a16
            vp = jnp.concatenate([v, jnp.zeros_like(v)], axis=0)
            o = _dot(a16, vp) + _dot(q * jnp.exp(b), st[...], NT)
            bl = b_s[pl.ds(CHUNK - 1, 1), :]
            st[...] = st[...] * jnp.exp(bl) + _tn(v, (1.0 - f) * jnp.exp(bl - b))
            o_ref[rows, ln] = o
            n = o * lax.rsqrt(jnp.mean(o * o, axis=-1, keepdims=True) + RMS_EPS) * go_ref[...]
            hg = hg_ref[rows, ln]
            y_ref[rows, ln] = n * hg * _sigmoid(hg)

        def chunk(c, carry):
            rows = pl.ds(pl.multiple_of(c * CHUNK, CHUNK), CHUNK)
            for hh in range(HGRN_HEADS):
                head_chunk(hh, c, rows)
            return carry

        lax.fori_loop(0, nck, chunk, 0)

    hp, wide = HGRN_HEADS, HGRN_HEADS * hd

    def col(off):
        return pl.BlockSpec((rows_blk, wide), lambda b, s: (b * nblk + s, off // hp))

    out = pl.BlockSpec((rows_blk, wide), lambda b, s: (b * nblk + s, 0))
    t = nb * seq
    return pl.pallas_call(
        body, name="hgrn_fwd", grid=(nb, nblk),
        in_specs=[col(12), col(16), col(20), col(24), pl.BlockSpec((1, wide), lambda b, s: (0, 0)),
                  pl.BlockSpec((1, hd), lambda b, s: (0, 0))],
        out_specs=[out, out, pl.BlockSpec((hp, nck, hd, hd), lambda b, s: (b, s, 0, 0)),
                   pl.BlockSpec((hp, nck, CHUNK, hd), lambda b, s: (b, s, 0, 0))],
        out_shape=[jax.ShapeDtypeStruct((t, wide), F32)] * 2
        + [jax.ShapeDtypeStruct((nb * hp, nc, hd, hd), F32), jax.ShapeDtypeStruct((nb * hp, nc, CHUNK, hd), BF16)],
        scratch_shapes=[pltpu.VMEM((hp, hd, hd), F32)] + [pltpu.VMEM((hp, CHUNK, hd), F32)] * 4,
        compiler_params=_params("parallel", "arbitrary"),
    )(proj, proj, proj, proj, lb, go)


def _hgrn_bwd(proj, lb, go, o_pre, states, scores, dout, nb, seq):
    nc = seq // CHUNK
    hd = HGRN_HEAD_DIM
    rows_blk = min(HGRN_ROWS, seq)
    nblk, nck = seq // rows_blk, rows_blk // CHUNK

    def body(hq_ref, hf_ref, hi_ref, hg_ref, lb_ref, go_ref, o_ref, st_ref, a_ref, dy_ref,
             dhq_ref, dhf_ref, dhi_ref, dhg_ref, dlb_ref, dgo_ref,
             dst_all, q_all, k_all, b_all, da_all, dqi_all, dki_all, dlb_all, dgo_all):
        @pl.when(pl.program_id(1) == 0)
        def _():
            dst_all[...] = jnp.zeros_like(dst_all)
            dlb_all[...] = jnp.zeros_like(dlb_all)
            dgo_all[...] = jnp.zeros_like(dgo_all)

        lower, upper = _tri(True), _tri(False)
        gov = go_ref[...]
        row = lax.broadcasted_iota(jnp.int32, (CHUNK, hd), 0)

        def head_chunk(hh, c, rows):
            ln = slice(hd * hh, hd * (hh + 1))
            dst, q_s, k_s, b_s = dst_all.at[hh], q_all.at[hh], k_all.at[hh], b_all.at[hh]
            da_s, dqi_s, dki_s = da_all.at[hh], dqi_all.at[hh], dki_all.at[hh]
            dlb_acc, dgo_acc = dlb_all.at[hh], dgo_all.at[hh]
            lbv = lb_ref[:, ln]
            hq, hf, v, hg = hq_ref[rows, ln], hf_ref[rows, ln], hi_ref[rows, ln], hg_ref[rows, ln]
            q, sq, sf, f = _hgrn_gates(hq, hf, lbv)
            kk = 1.0 - f
            b = _dot_exact01(lower, jnp.log(f))
            q_s[...] = q
            k_s[...] = kk
            b_s[...] = b
            bl = b_s[pl.ds(CHUNK - 1, 1), :]
            ebl = jnp.exp(bl)
            ekd = jnp.exp(bl - b)
            kd = kk * ekd
            eb = jnp.exp(b)
            qb = q * eb
            st0 = st_ref[hh, c]
            dst1 = dst[...]

            o = o_ref[rows, ln]
            dy = dy_ref[rows, ln]
            sg = _sigmoid(hg)
            rstd = lax.rsqrt(jnp.mean(o * o, axis=-1, keepdims=True) + RMS_EPS)
            ohat = o * rstd
            dn = dy * hg * sg
            dhg_ref[rows, ln] = (dy * ohat * gov * (sg * (1.0 + hg * (1.0 - sg)))).astype(BF16)
            dgo_acc[...] += jnp.sum(dn * ohat, axis=0, keepdims=True)
            gdn = dn * gov
            do = rstd * (gdn - ohat * jnp.mean(gdn * ohat, axis=-1, keepdims=True))

            qt, kt, eq, ek = _hgrn_offdiag(q_s, k_s, b_s)
            da = _dot(do, v, NT)
            dat = _dot(v, do, NT)
            da_s[...] = da
            dqo = _dot(da, kt) * eq
            dko = _dot(dat, qt) * ek
            dqi_s[...] = sum(dqo[:, j * hd:(j + 1) * hd] for j in range(N_SUB - 1))
            dki_s[...] = sum(dko[:, j * hd:(j + 1) * hd] for j in range(N_SUB - 1))
            col = lax.broadcasted_iota(jnp.int32, (SUB, CHUNK), 1)
            for i in range(N_SUB):
                qi = q_s[pl.ds(SUB * i, SUB), :]
                dai = da_s[pl.ds(SUB * i, SUB), :]
                dqd = jnp.zeros((SUB, hd), F32)
                dkd_ = jnp.zeros((SUB, hd), F32)
                for s in range(SUB):
                    e, t_io = _hgrn_diag_e(b_s, i, s)
                    dacol = jnp.sum(jnp.where(col == SUB * i + s, dai, 0.0), axis=-1, keepdims=True)
                    w = dacol * e
                    dqd = dqd + w * k_s[pl.ds(SUB * i + s, 1), :]
                    dkd_ = dkd_ + jnp.where(t_io == s, jnp.sum(w * qi, axis=0, keepdims=True), 0.0)
                dqi_s[pl.ds(SUB * i, SUB), :] += dqd
                dki_s[pl.ds(SUB * i, SUB), :] += dkd_
            dqi, dki = dqi_s[...], dki_s[...]

            dv = _tn(a_ref[hh, c].astype(F32), do)[0:CHUNK, :] + _dot(kd, dst1, NT)
            dqb = _dot(do, st0)
            dkd = _dot(v, dst1)
            t2 = dkd * kd
            dq = dqb * eb + dqi
            dk = dkd * ekd + dki
            dbl = jnp.sum(t2, axis=0, keepdims=True) + ebl * jnp.sum(st0 * dst1, axis=0, keepdims=True)
            db = dqb * qb - t2 + q * dqi - kk * dki + jnp.where(row == CHUNK - 1, dbl, 0.0)
            dg = _dot_exact01(upper, db)
            dst[...] = dst1 * ebl + _tn(do, qb)

            df = dg / f - dk
            dhf_ref[rows, ln] = (df * (1.0 - lbv) * sf * (1.0 - sf)).astype(BF16)
            dlb_acc[...] += jnp.sum(df * (1.0 - sf), axis=0, keepdims=True)
            dhq_ref[rows, ln] = (dq * (sq * (1.0 + hq * (1.0 - sq)))).astype(BF16)
            dhi_ref[rows, ln] = dv.astype(BF16)

        def chunk(it, carry):
            c = nck - 1 - it
            rows = pl.ds(pl.multiple_of(c * CHUNK, CHUNK), CHUNK)
            for hh in range(HGRN_HEADS):
                head_chunk(hh, c, rows)
            return carry

        lax.fori_loop(0, nck, chunk, 0)

        @pl.when(pl.program_id(1) == nblk - 1)
        def _():
            dlb_ref[...] = dlb_all[...]
            dgo_ref[...] = dgo_all[...]

    hp, wide = HGRN_HEADS, HGRN_HEADS * hd

    def col(off):
        return pl.BlockSpec((rows_blk, wide), lambda b, s: (b * nblk + nblk - 1 - s, off // hp))

    out = pl.BlockSpec((rows_blk, wide), lambda b, s: (b * nblk + nblk - 1 - s, 0))
    part = pl.BlockSpec((hp, 1, hd), lambda b, s: (b, 0, 0))
    t = nb * seq
    return pl.pallas_call(
        body, name="hgrn_bwd", grid=(nb, nblk),
        in_specs=[col(12), col(16), col(20), col(24), pl.BlockSpec((1, wide), lambda b, s: (0, 0)),
                  pl.BlockSpec((1, hd), lambda b, s: (0, 0)), out,
                  pl.BlockSpec((hp, nck, hd, hd), lambda b, s: (b, nblk - 1 - s, 0, 0)),
                  pl.BlockSpec((hp, nck, CHUNK, hd), lambda b, s: (b, nblk - 1 - s, 0, 0)), col(4)],
        out_specs=[out, out, out, out, part, part],
        out_shape=[jax.ShapeDtypeStruct((t, wide), BF16)] * 4 + [jax.ShapeDtypeStruct((nb * hp, 1, hd), F32)] * 2,
        scratch_shapes=[pltpu.VMEM((hp, hd, hd), F32)] + [pltpu.VMEM((hp, CHUNK, hd), F32)] * 3
        + [pltpu.VMEM((hp, CHUNK, CHUNK), F32)] + [pltpu.VMEM((hp, CHUNK, hd), F32)] * 2
        + [pltpu.VMEM((hp, 1, hd), F32)] * 2,
        compiler_params=_params("parallel", "arbitrary"),
    )(proj, proj, proj, proj, lb, go, o_pre, states, scores, dout)


def _lb_fwd(lower_bounds):
    def body(x_ref, o_ref):
        xv = x_ref[...]
        e = jnp.exp(xv - jnp.max(xv, axis=0, keepdims=True))
        o_ref[...] = e[0:1, :] / jnp.sum(e, axis=0, keepdims=True)

    return pl.pallas_call(body, name="lb_fwd",
                          out_shape=jax.ShapeDtypeStruct((1, lower_bounds.shape[1]), F32))(lower_bounds)


def _lb_bwd(lower_bounds, dlb_parts):
    ng = dlb_parts.shape[0]

    def body(x_ref, d_ref, o_ref):
        xv = x_ref[...]
        e = jnp.exp(xv - jnp.max(xv, axis=0, keepdims=True))
        p = e / jnp.sum(e, axis=0, keepdims=True)
        dlb = d_ref[0]
        for gi in range(1, ng):
            dlb = dlb + d_ref[gi]
        first = lax.broadcasted_iota(jnp.int32, xv.shape, 0) == 0
        o_ref[...] = p * (jnp.where(first, dlb, 0.0) - p[0:1, :] * dlb)

    return pl.pallas_call(body, name="lb_bwd",
                          out_shape=jax.ShapeDtypeStruct(lower_bounds.shape, F32))(lower_bounds, dlb_parts)


def _ffn_bwd(x, g, h, gate, up, dy, dy16, w, put, tag):
    wg, wu, wd = w[tag + "_w_gate"], w[tag + "_w_up"], w[tag + "_w_down"]
    dgate, dup, act = _ffn_bwd_mid(dy16, wd, gate, up, tag + "_bwd_mid")
    put(tag + "_w_down", _mm(act, dy16, ta=True, tm=1408, tn=512, scale=0.5, name=tag + "_dwd"))
    put(tag + "_w_gate", _mm(dgate, h, ta=True, tm=1408, tn=512, name=tag + "_dwg"))
    put(tag + "_w_up", _mm(dup, h, ta=True, tm=1408, tn=512, name=tag + "_dwu"))
    dh = _mm(dgate, wg, tm=512, tn=1024, name=tag + "_dh_gate")
    return _mm(dup, wu, tm=512, tn=1024, add=dh, norm_bwd=(x, g, dy), name=tag + "_dh_up")


def _local_step(x, tgt, sp, w, put, nb, seq):
    d = x.shape[1]
    h1 = _rms_fwd(x, sp["ffn1_norm_g"], "ffn1_norm")
    rb_pad = jnp.pad(sp["attn_rel_bias"], ((0, 0), (0, N_REL_PAD - N_REL)))
    bias = jnp.transpose(_bias_expand(rb_pad), (1, 0, 2)).reshape(ATTN_HEADS * CHUNK, BAND)
    gq2 = jnp.concatenate([sp["attn_q_norm_g"]] * 2, axis=1)
    gk2 = jnp.concatenate([sp["attn_k_norm_g"]] * 2, axis=1)
    lb = _lb_fwd(sp["hgrn_lower_bounds"])
    gate1, up1, x1, h2 = _ffn_fwd(h1, x, w["ffn1_w_gate"], w["ffn1_w_up"], w["ffn1_w_down"], "ffn1_fwd",
                                  next_g=sp["mix_norm_g"])
    proj = _mm(h2, w["w_in"], tb=True, tm=256, tn=w["w_in"].shape[0], name="in_proj")
    attn = _attn_fwd(proj, bias, gq2, gk2, nb, seq)
    hy, ho, hstate, hscore = _hgrn_fwd(proj, lb, sp["hgrn_out_norm_g"], nb, seq)
    mix = jnp.concatenate([attn, hy], axis=1)
    x2, h3 = _mm(mix, w["w_out"], tm=512, tn=1024, add=x1, norm_g=sp["ffn2_norm_g"], name="out_proj")
    gate2, up2, dx3, dx3_16, sq = _ffn_fwd(h3, x2, w["ffn2_w_gate"], w["ffn2_w_up"], w["ffn2_w_down"], "ffn2_fwd",
                                           tgt=tgt)
    loss = 0.5 * jnp.sum(sq) / d

    dx2, dx2_16, dg3 = _ffn_bwd(x2, sp["ffn2_norm_g"], h3, gate2, up2, dx3, dx3_16, w, put, "ffn2")
    dmix = _mm(dx2_16, w["w_out"], tb=True, tm=512, tn=1024, name="out_proj_dx")
    put("w_out", _mm(mix, dx2_16, ta=True, tm=512, tn=1024, name="out_proj_dw"))
    bias_t = jnp.transpose(bias.reshape(ATTN_HEADS // 2, 2 * CHUNK, BAND), (0, 2, 1)).reshape(-1, 2 * CHUNK)
    dq, dk, dv, dbias, dgq, dgk = _attn_bwd(proj, attn, dmix, bias_t, gq2, gk2, nb, seq)
    dbias = jnp.transpose(dbias.reshape(nb, ATTN_HEADS // 2, BAND, 2, CHUNK), (0, 4, 1, 3, 2))
    dbias = dbias.reshape(nb, CHUNK, ATTN_HEADS, BAND)
    dgq = jnp.sum(dgq, axis=(0, 1)).reshape(2, ATTN_HEAD_DIM).sum(axis=0, keepdims=True)
    dgk = jnp.sum(dgk, axis=(0, 1)).reshape(2, ATTN_HEAD_DIM).sum(axis=0, keepdims=True)
    dhq, dhf, dhi, dhg, dlb, dgo = _hgrn_bwd(proj, lb, sp["hgrn_out_norm_g"], ho, hstate, hscore, dmix, nb, seq)
    dproj = jnp.concatenate([dq, dk, dv, dhq, dhf, dhi, dhg], axis=1)
    put("w_in", _mm(dproj, h2, ta=True, tm=512, tn=1024, name="in_proj_dw"))
    dx1, dx1_16, dgm = _mm(dproj, w["w_in"], tm=512, tn=1024, norm_bwd=(x1, sp["mix_norm_g"], dx2),
                           name="in_proj_dx")
    dx0, _, dg1 = _ffn_bwd(x, sp["ffn1_norm_g"], h1, gate1, up1, dx1, dx1_16, w, put, "ffn1")

    small = {
        "ffn1_norm_g": dg1, "mix_norm_g": dgm, "ffn2_norm_g": dg3,
        "attn_q_norm_g": dgq, "attn_k_norm_g": dgk,
        "attn_rel_bias": _bias_fold(dbias)[:, :N_REL],
        "hgrn_lower_bounds": _lb_bwd(sp["hgrn_lower_bounds"], dlb.reshape(nb, 1, HGRN_HEADS * HGRN_HEAD_DIM)),
        "hgrn_out_norm_g": jnp.sum(dgo, axis=(0, 1))[None, :],
    }
    return loss, dx0, small


MESH = pl.DeviceIdType.MESH
ANY = pl.BlockSpec(memory_space=pl.ANY)


def _coords():
    return lax.axis_index("x"), lax.axis_index("y"), lax.axis_index("c")


def _other_chips(x, y):
    return [(1 - x, y), (x, 1 - y), (1 - x, 1 - y)]


def _gather_side(shards):
    n = len(shards)

    def copies(ins, outs, sems):
        send_sems, recv_sems, local_sems = sems
        x, y, c = _coords()
        me, sibling = (x, y, c), (x, y, 1 - c)
        chips = _other_chips(x, y)

        def copy(i, k, block, to, src=None):
            bx, by, bc = block
            dst = outs[i].at[4 * bx + 2 * by + bc]
            return pltpu.make_async_remote_copy(
                src_ref=dst if src is None else src, dst_ref=dst, send_sem=send_sems.at[i, k],
                recv_sem=recv_sems.at[i, k], device_id=to, device_id_type=MESH)

        mine = [pltpu.make_async_copy(ins[i], outs[i].at[4 * x + 2 * y + c], local_sems.at[i]) for i in range(n)]
        own = []
        for i in range(n):
            own.append(copy(i, 0, me, sibling, src=ins[i]))
            own += [copy(i, 1 + j, me, (*chip, c), src=ins[i]) for j, chip in enumerate(chips)]
        return copy, mine, own, me, sibling, chips, c

    def start(ins, outs, sems):
        _, mine, own, *_ = copies(ins, outs, sems)
        for cp in mine + own:
            cp.start()

    def finish(ins, outs, sems):
        copy, mine, own, me, sibling, chips, c = copies(ins, outs, sems)
        passed = []
        for i in range(n):
            for j, chip in enumerate(chips):
                copy(i, 1 + j, (*chip, c), me).wait_recv()
                passed.append(copy(i, 4 + j, (*chip, c), sibling))
                passed[-1].start()
        for i in range(n):
            copy(i, 0, sibling, me).wait_recv()
            for j, chip in enumerate(chips):
                copy(i, 4 + j, (*chip, 1 - c), me).wait_recv()
        for cp in own + passed:
            cp.wait_send()
        for cp in mine:
            cp.wait()

    return _Side(list(shards), [jax.ShapeDtypeStruct((N_DEV,) + s.shape, s.dtype) for s in shards],
                 [pltpu.SemaphoreType.DMA((n, 7)), pltpu.SemaphoreType.DMA((n, 7)), pltpu.SemaphoreType.DMA((n,))],
                 start, finish)


def _pair_side(grads):
    n = len(grads)

    def copies(ins, outs, sems):
        send_sems, recv_sems = sems
        x, y, c = _coords()
        return [pltpu.make_async_remote_copy(
            src_ref=ins[i].at[2 * k + 1 - c], dst_ref=outs[i].at[k], send_sem=send_sems.at[i, k],
            recv_sem=recv_sems.at[i, k], device_id=(x, y, 1 - c), device_id_type=MESH)
            for i in range(n) for k in range(4)]

    def start(ins, outs, sems):
        for cp in copies(ins, outs, sems):
            cp.start()

    def finish(ins, outs, sems):
        for cp in copies(ins, outs, sems):
            cp.wait()

    return _Side(list(grads), [jax.ShapeDtypeStruct((4,) + g.shape[1:], g.dtype) for g in grads],
                 [pltpu.SemaphoreType.DMA((n, 4)), pltpu.SemaphoreType.DMA((n, 4))], start, finish)


def _pair_add(grad, recv, core, name):
    _, r, cdim = grad.shape

    def body(c_ref, g_ref, r_ref, o_ref):
        o_ref[...] = (g_ref[...] + r_ref[...]).astype(BF16)

    blk = (1, r, cdim)
    return pl.pallas_call(
        body, name=name,
        grid_spec=pltpu.PrefetchScalarGridSpec(
            num_scalar_prefetch=1, grid=(4,),
            in_specs=[pl.BlockSpec(blk, lambda k, c_ref: (2 * k + c_ref[0], 0, 0)),
                      pl.BlockSpec(blk, lambda k, c_ref: (k, 0, 0))],
            out_specs=pl.BlockSpec(blk, lambda k, c_ref: (k, 0, 0))),
        out_shape=jax.ShapeDtypeStruct((4, r, cdim), BF16),
        compiler_params=_params("arbitrary"),
    )(core, grad, recv)


def _chip_side(parts):
    n = len(parts)

    def copies(ins, outs, sems):
        send_sems, recv_sems, local_sems = sems
        x, y, c = _coords()
        chips = _other_chips(x, y)
        mine = [pltpu.make_async_copy(ins[i].at[2 * x + y], outs[i].at[2 * x + y], local_sems.at[i])
                for i in range(n)]
        sent = [pltpu.make_async_remote_copy(
            src_ref=ins[i].at[2 * px + py], dst_ref=outs[i].at[2 * x + y], send_sem=send_sems.at[i, j],
            recv_sem=recv_sems.at[i, j], device_id=(px, py, c), device_id_type=MESH)
            for i in range(n) for j, (px, py) in enumerate(chips)]
        return mine, sent, chips, c

    def start(ins, outs, sems):
        mine, sent, _, _ = copies(ins, outs, sems)
        for cp in mine + sent:
            cp.start()

    def finish(ins, outs, sems):
        mine, sent, chips, c = copies(ins, outs, sems)
        send_sems, recv_sems, _ = sems
        for i in range(n):
            for j, (px, py) in enumerate(chips):
                landed = outs[i].at[2 * px + py]
                pltpu.make_async_remote_copy(
                    src_ref=landed, dst_ref=landed, send_sem=send_sems.at[i, j], recv_sem=recv_sems.at[i, j],
                    device_id=(px, py, c), device_id_type=MESH).wait_recv()
        for cp in sent:
            cp.wait_send()
        for cp in mine:
            cp.wait()

    return _Side(list(parts), [jax.ShapeDtypeStruct(p.shape, p.dtype) for p in parts],
                 [pltpu.SemaphoreType.DMA((n, 3)), pltpu.SemaphoreType.DMA((n, 3)), pltpu.SemaphoreType.DMA((n,))],
                 start, finish)


def _all_reduce_small(v):
    r = v.shape[0]

    def body(v_ref, o_ref, buf, send_sems, recv_sems):
        x, y, c = _coords()
        me = 4 * x + 2 * y + c
        buf[me] = v_ref[...]
        cps = []
        for k in range(1, N_DEV):
            px = 1 - x if k & 4 else x
            py = 1 - y if k & 2 else y
            pc = 1 - c if k & 1 else c
            cps.append((pltpu.make_async_remote_copy(
                src_ref=v_ref, dst_ref=buf.at[me], send_sem=send_sems.at[k - 1], recv_sem=recv_sems.at[k - 1],
                device_id=(px, py, pc), device_id_type=MESH), 4 * px + 2 * py + pc))
        for cp, _ in cps:
            cp.start()
        for k, (cp, peer) in enumerate(cps):
            pltpu.make_async_remote_copy(
                src_ref=v_ref, dst_ref=buf.at[peer], send_sem=send_sems.at[k], recv_sem=recv_sems.at[k],
                device_id=(x, y, c), device_id_type=MESH).wait_recv()
        for cp, _ in cps:
            cp.wait_send()
        acc = buf[0]
        for j in range(1, N_DEV):
            acc = acc + buf[j]
        o_ref[...] = acc

    return pl.pallas_call(
        body, name="small_all_reduce", out_shape=jax.ShapeDtypeStruct(v.shape, F32),
        in_specs=[pl.BlockSpec(memory_space=pltpu.VMEM)], out_specs=pl.BlockSpec(memory_space=pltpu.VMEM),
        scratch_shapes=[pltpu.VMEM((N_DEV, r, 128), F32), pltpu.SemaphoreType.DMA((N_DEV - 1,)),
                        pltpu.SemaphoreType.DMA((N_DEV - 1,))],
    )(v)


def _adamw(w, m, v, g, name):
    parts = w.ndim == 3
    r, cdim = w.shape[-2:]
    tr = r // 4 if r % 32 == 0 else r

    def body(w_ref, m_ref, v_ref, g_ref, go_ref, d_ref, mo_ref, vo_ref):
        if parts:
            gv = g_ref[0].astype(F32)
            for k in range(1, 4):
                gv = gv + g_ref[k].astype(F32)
            gv = gv[None]
        else:
            gv = g_ref[...]
        m2 = ADAM_B1 * m_ref[...] + (1.0 - ADAM_B1) * gv
        v2 = ADAM_B2 * v_ref[...] + (1.0 - ADAM_B2) * (gv * gv)
        m_hat = m2 / (1.0 - ADAM_B1 ** ADAM_STEP)
        v_hat = v2 / (1.0 - ADAM_B2 ** ADAM_STEP)
        go_ref[...] = gv
        d_ref[...] = -ADAM_LR * (m_hat / (jnp.sqrt(v_hat) + ADAM_EPS) + ADAM_WD * w_ref[...])
        mo_ref[...] = m2
        vo_ref[...] = v2

    if parts:
        row = pl.BlockSpec((1, tr, cdim), lambda i: (0, i, 0))
        g_spec = pl.BlockSpec((4, tr, cdim), lambda i: (0, i, 0))
    else:
        row = g_spec = pl.BlockSpec((tr, cdim), lambda i: (i, 0))
    return pl.pallas_call(
        body, name=name, grid=(r // tr,), in_specs=[row, row, row, g_spec], out_specs=[row] * 4,
        out_shape=[jax.ShapeDtypeStruct(w.shape, F32)] * 4,
        compiler_params=_params("parallel"),
    )(w, m, v, g)


WEIGHTS = ["ffn1_norm_g", "ffn1_w_gate", "ffn1_w_up", "ffn1_w_down", "mix_norm_g", "w_in", "attn_q_norm_g",
           "attn_k_norm_g", "attn_rel_bias", "hgrn_lower_bounds", "hgrn_out_norm_g", "w_out", "ffn2_norm_g",
           "ffn2_w_gate", "ffn2_w_up", "ffn2_w_down"]
COL_SHARDED = ("ffn1_w_gate", "ffn1_w_up", "w_in", "ffn2_w_gate", "ffn2_w_up")
ROW_SHARDED = ("ffn1_w_down", "w_out", "ffn2_w_down")
BIG = [n for n in WEIGHTS if n in COL_SHARDED or n in ROW_SHARDED]
SMALL = [n for n in WEIGHTS if n not in BIG]
PACK_ROWS = 8
FFN2 = ["ffn2_w_down", "ffn2_w_gate", "ffn2_w_up"]
MIXER = ["w_out", "w_in"]

PLAN = {
    "ffn1_norm": [("gather", ["ffn1_w_down"])],
    "bias_expand": [("gather", ["ffn1_w_gate", "ffn1_w_up"])],
    "ffn1_fwd": [("gather", MIXER)],
    "attn_fwd": [("gather", FFN2)],
    "ffn2_dh_gate": [("pair", FFN2)],
    "attn_bwd": [("chip", FFN2)],
    "in_proj_dx": [("pair", MIXER)],
    "ffn1_bwd_mid": [("chip", MIXER)],
    "ffn1_dwg": [("pair", ["ffn1_w_down"])],
    "ffn1_dwu": [("chip", ["ffn1_w_down"]), ("pair", ["ffn1_w_gate"])],
    "ffn1_dh_gate": [("chip", ["ffn1_w_gate"]), ("pair", ["ffn1_w_up"])],
    "bias_fold": [("chip", ["ffn1_w_up"])],
}


def _join_sides(sides):
    def split(refs, counts):
        out, at = [], 0
        for n in counts:
            out.append(refs[at:at + n])
            at += n
        return out

    n_in, n_out, n_sem = ([len(getattr(s, f)) for s in sides] for f in ("ins", "out_shape", "sems"))

    def run(which):
        def go(ins, outs, sems):
            for s, i, o, m in zip(sides, split(ins, n_in), split(outs, n_out), split(sems, n_sem)):
                getattr(s, which)(i, o, m)
        return go

    return _Side([a for s in sides for a in s.ins], [a for s in sides for a in s.out_shape],
                 [a for s in sides for a in s.sems], run("start"), run("finish"))


class _Schedule:
    def __init__(self, shards):
        self.shards = shards
        self.weights = {}
        self.sliced = {}
        self.partials = {}
        self.reduced = {}

    def put(self, name, grad):
        self.sliced[name] = grad.reshape((N_DEV,) + self.shards[name].shape)

    def side_for(self, call):
        if call not in PLAN:
            return None
        sides = []
        for kind, names in PLAN[call]:
            if kind == "gather":
                sides.append(_gather_side([self.shards[n] for n in names]))
            elif kind == "pair":
                sides.append(_pair_side([self.sliced[n] for n in names]))
            else:
                sides.append(_chip_side([self.partials[n] for n in names]))
        return _join_sides(sides)

    def done(self, call, outs):
        at = 0
        for kind, names in PLAN[call]:
            self.file(kind, names, outs[at:at + len(names)])
            at += len(names)

    def file(self, kind, names, outs):
        for n, o in zip(names, outs):
            if kind == "gather":
                self.weights[n] = o.reshape(N_DEV * o.shape[1], o.shape[2])
            elif kind == "pair":
                core = lax.axis_index("c").astype(jnp.int32).reshape(1)
                self.partials[n] = _pair_add(self.sliced[n], o, core, n + "_pair_add")
            else:
                self.reduced[n] = o


def _pack_small(vals, loss=None):
    parts = []
    for n in SMALL:
        a = vals[n]
        if n == "attn_rel_bias":
            a = jnp.pad(a.reshape(ATTN_HEADS, N_REL), ((0, 0), (0, N_REL_PAD - N_REL)))
        flat = a.reshape(-1)
        size = -(-flat.shape[0] // (PACK_ROWS * 128)) * PACK_ROWS * 128
        parts.append(jnp.pad(flat, (0, size - flat.shape[0])).reshape(-1, 128))
    tail = jnp.zeros((PACK_ROWS, 128), F32)
    if loss is not None:
        tail = tail.at[0, 0].set(loss)
    return jnp.concatenate(parts + [tail], axis=0)


def _unpack_small(packed, shapes):
    out, row = {}, 0
    for n in SMALL:
        shape = shapes[n]
        if n == "attn_rel_bias":
            rows = ATTN_HEADS * N_REL_PAD // 128
            out[n] = packed[row:row + rows].reshape(ATTN_HEADS, N_REL_PAD)[:, :N_REL].reshape(shape)
        else:
            size = 1
            for s in shape:
                size *= s
            rows = -(-size // (PACK_ROWS * 128)) * PACK_ROWS
            out[n] = packed[row:row + rows].reshape(-1)[:size].reshape(shape)
        row += rows
    return out, packed[row, 0]


def kernel(x, ffn1_norm_g, ffn1_w_gate, ffn1_w_up, ffn1_w_down, mix_norm_g, w_in, attn_q_norm_g, attn_k_norm_g, attn_rel_bias, hgrn_lower_bounds, hgrn_out_norm_g, w_out, ffn2_norm_g, ffn2_w_gate, ffn2_w_up, ffn2_w_down, loss_target, m_ffn1_norm_g, m_ffn1_w_gate, m_ffn1_w_up, m_ffn1_w_down, m_mix_norm_g, m_w_in, m_attn_q_norm_g, m_attn_k_norm_g, m_attn_rel_bias, m_hgrn_lower_bounds, m_hgrn_out_norm_g, m_w_out, m_ffn2_norm_g, m_ffn2_w_gate, m_ffn2_w_up, m_ffn2_w_down, v_ffn1_norm_g, v_ffn1_w_gate, v_ffn1_w_up, v_ffn1_w_down, v_mix_norm_g, v_w_in, v_attn_q_norm_g, v_attn_k_norm_g, v_attn_rel_bias, v_hgrn_lower_bounds, v_hgrn_out_norm_g, v_w_out, v_ffn2_norm_g, v_ffn2_w_gate, v_ffn2_w_up, v_ffn2_w_down):
    wts = dict(zip(WEIGHTS, (ffn1_norm_g, ffn1_w_gate, ffn1_w_up, ffn1_w_down, mix_norm_g, w_in, attn_q_norm_g,
                             attn_k_norm_g, attn_rel_bias, hgrn_lower_bounds, hgrn_out_norm_g, w_out, ffn2_norm_g,
                             ffn2_w_gate, ffn2_w_up, ffn2_w_down)))
    mom = dict(zip(WEIGHTS, (m_ffn1_norm_g, m_ffn1_w_gate, m_ffn1_w_up, m_ffn1_w_down, m_mix_norm_g, m_w_in,
                             m_attn_q_norm_g, m_attn_k_norm_g, m_attn_rel_bias, m_hgrn_lower_bounds,
                             m_hgrn_out_norm_g, m_w_out, m_ffn2_norm_g, m_ffn2_w_gate, m_ffn2_w_up, m_ffn2_w_down)))
    var = dict(zip(WEIGHTS, (v_ffn1_norm_g, v_ffn1_w_gate, v_ffn1_w_up, v_ffn1_w_down, v_mix_norm_g, v_w_in,
                             v_attn_q_norm_g, v_attn_k_norm_g, v_attn_rel_bias, v_hgrn_lower_bounds,
                             v_hgrn_out_norm_g, v_w_out, v_ffn2_norm_g, v_ffn2_w_gate, v_ffn2_w_up, v_ffn2_w_down)))
    nb, seq, d = x.shape
    shapes = {n: wts[n].shape for n in WEIGHTS}

    def rows_first(a, n):
        return jnp.swapaxes(a, 1, 2) if n in COL_SHARDED else a

    sched = _Schedule({n: rows_first(wts[n], n)[0].astype(BF16) for n in BIG})
    sp = {n: wts[n] for n in SMALL}
    sp["attn_rel_bias"] = wts["attn_rel_bias"][0]
    _ACTIVE[0] = sched
    try:
        loss, dx, dsmall = _local_step(x.reshape(nb * seq, d), loss_target.reshape(nb * seq, d), sp,
                                       sched.weights, sched.put, nb, seq)
    finally:
        _ACTIVE[0] = None
    reduced = sched.reduced

    small_sum = _all_reduce_small(_pack_small(dsmall, loss))
    gsmall, loss_total = _unpack_small(small_sum, shapes)

    grads, deltas, new_m, new_v = {}, {}, {}, {}
    for n in BIG:
        out = _adamw(rows_first(wts[n], n), rows_first(mom[n], n), rows_first(var[n], n), reduced[n], n + "_adamw")
        grads[n], deltas[n], new_m[n], new_v[n] = (rows_first(o, n) for o in out)
    packed = _adamw(_pack_small(wts), _pack_small(mom), _pack_small(var), small_sum, "small_adamw")
    for dst, p in zip((deltas, new_m, new_v), packed[1:]):
        dst.update(_unpack_small(p, shapes)[0])
    grads.update(gsmall)

    return (loss_total, dx.reshape(nb, seq, d), *[grads[n] for n in WEIGHTS], *[deltas[n] for n in WEIGHTS],
            *[new_m[n] for n in WEIGHTS], *[new_v[n] for n in WEIGHTS])
```

```python
import functools

import jax
import jax.numpy as jnp
from jax import lax
from jax.experimental import pallas as pl
from jax.experimental.pallas import tpu as pltpu

F32 = jnp.float32
BF16 = jnp.bfloat16

RMS_EPS = 1e-6
CHUNK = 64
LEFT_CHUNKS = 8
BAND = (LEFT_CHUNKS + 2) * CHUNK
KPAD = BAND - CHUNK
REL_CLIP = 128
N_REL = 2 * REL_CLIP + 1
N_REL_PAD = 384
ATTN_HEADS = 8
ATTN_HEAD_DIM = 64
ATTN_WIDTH = ATTN_HEADS * ATTN_HEAD_DIM
ATTN_LOCKSTEP = 4
ATTN_UNROLL = 8
HGRN_HEADS = 4
HGRN_HEAD_DIM = 128
HGRN_ROWS = 512
SUB = 16
N_SUB = CHUNK // SUB
DIAG_STAGE = 4
N_DEV = 8

ADAM_LR = 0.001
ADAM_B1 = 0.9
ADAM_B2 = 0.999
ADAM_EPS = 1e-08
ADAM_WD = 0.01
ADAM_STEP = 10

VMEM_LIMIT = 56 * 1024 * 1024
NT = (((1,), (1,)), ((), ()))
NN = (((1,), (0,)), ((), ()))


def _params(*sem):
    return pltpu.CompilerParams(dimension_semantics=sem, vmem_limit_bytes=VMEM_LIMIT)


def _sigmoid(v):
    return 0.5 * jnp.tanh(0.5 * v) + 0.5


def _dot(a, b, dims=NN):
    return lax.dot_general(a.astype(BF16), b.astype(BF16), dims, preferred_element_type=F32)


def _dot_exact01(m01, v):
    m = m01.astype(BF16)
    hi = v.astype(BF16)
    r1 = v - hi.astype(F32)
    mid = r1.astype(BF16)
    lo = (r1 - mid.astype(F32)).astype(BF16)
    out = lax.dot_general(m, hi, NN, preferred_element_type=F32)
    out = out + lax.dot_general(m, mid, NN, preferred_element_type=F32)
    return out + lax.dot_general(m, lo, NN, preferred_element_type=F32)


def _dot_exact01_r(v, m01):
    m = m01.astype(BF16)
    hi = v.astype(BF16)
    r1 = v - hi.astype(F32)
    mid = r1.astype(BF16)
    lo = (r1 - mid.astype(F32)).astype(BF16)
    out = lax.dot_general(hi, m, NN, preferred_element_type=F32)
    out = out + lax.dot_general(mid, m, NN, preferred_element_type=F32)
    return out + lax.dot_general(lo, m, NN, preferred_element_type=F32)


def _lockstep(stages):
    live = list(stages)
    while live:
        still = []
        for g in live:
            try:
                next(g)
                still.append(g)
            except StopIteration:
                pass
        live = still


def _row_sums_on_lanes(v):
    ones = jnp.ones((8, v.shape[1]), BF16)
    hi = v.astype(BF16)
    r1 = v - hi.astype(F32)
    mid = r1.astype(BF16)
    lo = (r1 - mid.astype(F32)).astype(BF16)
    out = lax.dot_general(ones, hi, NT, preferred_element_type=F32)
    out = out + lax.dot_general(ones, mid, NT, preferred_element_type=F32)
    return (out + lax.dot_general(ones, lo, NT, preferred_element_type=F32))[0:1, :]


def _tn(a, b):
    ap = jnp.concatenate([a, jnp.zeros_like(a)], axis=0)
    bp = jnp.concatenate([b, jnp.zeros_like(b)], axis=0)
    return _dot(ap.T, bp)


def _row_tile(t):
    for tm in (512, 256, 128, 64, 32, 16, 8):
        if t % tm == 0:
            return tm
    raise ValueError(t)


class _Side:
    def __init__(self, ins, out_shape, sems, start, finish):
        self.ins, self.out_shape, self.sems, self.start, self.finish = ins, out_shape, sems, start, finish


_ACTIVE = [None]


def _pallas(body, *, name, grid, in_specs, out_specs, out_shape, scratch_shapes=(), sem, args):
    sched = _ACTIVE[0]
    side = sched.side_for(name) if sched is not None else None
    if side is None:
        return pl.pallas_call(
            body, name=name, grid=grid, in_specs=list(in_specs), out_specs=list(out_specs),
            out_shape=list(out_shape), scratch_shapes=list(scratch_shapes), compiler_params=_params(*sem))(*args)
    cuts = [len(in_specs), len(side.ins), len(out_shape), len(side.out_shape), len(scratch_shapes)]

    def with_side(*refs):
        groups, at = [], 0
        for n in cuts:
            groups.append(refs[at:at + n])
            at += n
        ins, side_ins, outs, side_outs, scratch = groups
        side_sems = refs[at:]
        first = pl.program_id(0) == 0
        last = pl.program_id(0) == grid[0] - 1
        for a in range(1, len(grid)):
            first = jnp.logical_and(first, pl.program_id(a) == 0)
            last = jnp.logical_and(last, pl.program_id(a) == grid[a] - 1)

        @pl.when(first)
        def _():
            side.start(side_ins, side_outs, side_sems)

        body(*ins, *outs, *scratch)

        @pl.when(last)
        def _():
            side.finish(side_ins, side_outs, side_sems)

    hbm = pl.BlockSpec(memory_space=pl.ANY)
    res = pl.pallas_call(
        with_side, name=name, grid=grid, in_specs=list(in_specs) + [hbm] * len(side.ins),
        out_specs=list(out_specs) + [hbm] * len(side.out_shape), out_shape=list(out_shape) + list(side.out_shape),
        scratch_shapes=list(scratch_shapes) + list(side.sems),
        compiler_params=_params(*(["arbitrary"] * len(grid))))(*args, *side.ins)
    sched.done(name, res[len(out_shape):])
    return res[:len(out_shape)]


def _rms_fwd(x, g, name):
    t, d = x.shape
    tm = _row_tile(t)

    def body(x_ref, g_ref, h_ref):
        xv = x_ref[...]
        r = lax.rsqrt(jnp.mean(xv * xv, axis=-1, keepdims=True) + RMS_EPS)
        h_ref[...] = (xv * r * g_ref[...]).astype(BF16)

    return _pallas(
        body, name=name, grid=(t // tm,),
        in_specs=[pl.BlockSpec((tm, d), lambda i: (i, 0)), pl.BlockSpec((1, d), lambda i: (0, 0))],
        out_specs=[pl.BlockSpec((tm, d), lambda i: (i, 0))], out_shape=[jax.ShapeDtypeStruct((t, d), BF16)],
        sem=("parallel",), args=(x, g))[0]


def _accumulate(ref, part, step):
    @pl.when(step == 0)
    def _():
        ref[...] = part

    @pl.when(step > 0)
    def _():
        ref[...] += part


def _mm(a, b, *, ta=False, tb=False, tm, tn, out_dtype=F32, add=None, scale=1.0, norm_g=None, norm_bwd=None, name):
    m, k = (a.shape[1], a.shape[0]) if ta else a.shape
    n = b.shape[0] if tb else b.shape[1]
    tm, tn = min(tm, m), min(tn, n)
    assert m % tm == 0 and n % tn == 0, (m, n, tm, tn)
    assert (norm_g is None and norm_bwd is None) or tn == n
    dims = (((0 if ta else 1,), (1 if tb else 0,)), ((), ()))
    n_in = 2 + (add is not None) + (norm_g is not None) + (3 if norm_bwd is not None else 0)

    def body(*refs):
        ins, outs = list(refs[2:n_in]), refs[n_in:]
        r = lax.dot_general(refs[0][...].astype(BF16), refs[1][...].astype(BF16), dims, preferred_element_type=F32)
        if scale != 1.0:
            r = r * scale
        if add is not None:
            r = r + ins.pop(0)[...]
        if norm_bwd is not None:
            xv, gv, dres = (ref[...] for ref in ins)
            rs = lax.rsqrt(jnp.mean(xv * xv, axis=-1, keepdims=True) + RMS_EPS)
            xhat = xv * rs
            gd = r * gv
            dx = dres + rs * (gd - xhat * jnp.mean(gd * xhat, axis=-1, keepdims=True))
            outs[0][...] = dx
            outs[1][...] = dx.astype(BF16)
            _accumulate(outs[2], jnp.sum(r * xhat, axis=0, keepdims=True), pl.program_id(0))
            return
        outs[0][...] = r.astype(out_dtype)
        if norm_g is not None:
            rs = lax.rsqrt(jnp.mean(r * r, axis=-1, keepdims=True) + RMS_EPS)
            outs[1][...] = (r * rs * ins.pop(0)[...]).astype(BF16)

    a_spec = pl.BlockSpec((k, tm), lambda i, j: (0, i)) if ta else pl.BlockSpec((tm, k), lambda i, j: (i, 0))
    b_spec = pl.BlockSpec((tn, k), lambda i, j: (j, 0)) if tb else pl.BlockSpec((k, tn), lambda i, j: (0, j))
    o_spec = pl.BlockSpec((tm, tn), lambda i, j: (i, j))
    vec = pl.BlockSpec((1, tn), lambda i, j: (0, j))
    args, specs = [a, b], [a_spec, b_spec]
    if add is not None:
        args.append(add)
        specs.append(o_spec)
    out_specs, out_shape = [o_spec], [jax.ShapeDtypeStruct((m, n), out_dtype)]
    if norm_g is not None:
        args.append(norm_g)
        specs.append(vec)
        out_specs.append(o_spec)
        out_shape.append(jax.ShapeDtypeStruct((m, n), BF16))
    if norm_bwd is not None:
        args += list(norm_bwd)
        specs += [o_spec, vec, o_spec]
        out_specs = [o_spec, o_spec, vec]
        out_shape = [jax.ShapeDtypeStruct((m, n), F32), jax.ShapeDtypeStruct((m, n), BF16),
                     jax.ShapeDtypeStruct((1, n), F32)]
    res = _pallas(body, name=name, grid=(m // tm, n // tn), in_specs=specs, out_specs=out_specs, out_shape=out_shape,
                  sem=("arbitrary", "arbitrary") if norm_bwd is not None else ("parallel", "parallel"), args=args)
    return res[0] if len(res) == 1 else res


def _ffn_tile(f):
    for tf in (1408, 512, 256, 128):
        if f % tf == 0:
            return tf
    raise ValueError(f)


def _ffn_fwd(h, x, wg, wu, wd, name, next_g=None, tgt=None):
    t, d = x.shape
    f = wg.shape[0]
    tm, tf = _row_tile(t), _ffn_tile(f)
    nf = f // tf
    assert (next_g is None) != (tgt is None)

    def body(h_ref, x_ref, wg_ref, wu_ref, wd_ref, tail_ref, g_ref, u_ref, o0_ref, o1_ref, *rest):
        acc_ref = rest[-1]
        j = pl.program_id(1)
        hv = h_ref[...]
        gv = lax.dot_general(hv, wg_ref[...], NT, preferred_element_type=F32)
        uv = lax.dot_general(hv, wu_ref[...], NT, preferred_element_type=F32)
        av = gv * _sigmoid(gv) * uv
        g_ref[...] = gv.astype(BF16)
        u_ref[...] = uv.astype(BF16)
        _accumulate(acc_ref, lax.dot_general(av.astype(BF16), wd_ref[...], NN, preferred_element_type=F32), j)

        @pl.when(j == nf - 1)
        def _():
            y = x_ref[...] + 0.5 * acc_ref[...]
            if tgt is None:
                o0_ref[...] = y
                rs = lax.rsqrt(jnp.mean(y * y, axis=-1, keepdims=True) + RMS_EPS)
                o1_ref[...] = (y * rs * tail_ref[...]).astype(BF16)
            else:
                e = y - tail_ref[...]
                dy = e * (1.0 / d)
                o0_ref[...] = dy
                o1_ref[...] = dy.astype(BF16)
                _accumulate(rest[0], jnp.sum(e * e, axis=0, keepdims=True), pl.program_id(0))

    row = pl.BlockSpec((tm, d), lambda i, j: (i, 0))
    hid = pl.BlockSpec((tm, tf), lambda i, j: (i, j))
    vec = pl.BlockSpec((1, d), lambda i, j: (0, 0))
    out_specs = [hid, hid, row, row] + ([vec] if tgt is not None else [])
    out_shape = [jax.ShapeDtypeStruct((t, f), BF16)] * 2 + [jax.ShapeDtypeStruct((t, d), F32),
                                                            jax.ShapeDtypeStruct((t, d), BF16)]
    if tgt is not None:
        out_shape.append(jax.ShapeDtypeStruct((1, d), F32))
    return _pallas(
        body, name=name, grid=(t // tm, nf),
        in_specs=[row, row] + [pl.BlockSpec((tf, d), lambda i, j: (j, 0))] * 3 + [vec if tgt is None else row],
        out_specs=out_specs, out_shape=out_shape, scratch_shapes=[pltpu.VMEM((tm, d), F32)],
        sem=("parallel" if tgt is None else "arbitrary", "arbitrary"),
        args=(h, x, wg, wu, wd, next_g if tgt is None else tgt))


def _ffn_bwd_mid(dy, wd, g, u, name):
    t, d = dy.shape
    f = wd.shape[0]
    tm, tf = _row_tile(t), _ffn_tile(f)

    def body(dy_ref, wd_ref, g_ref, u_ref, dg_ref, du_ref, a_ref):
        da = 0.5 * lax.dot_general(dy_ref[...].astype(BF16), wd_ref[...], NT, preferred_element_type=F32)
        gv = g_ref[...].astype(F32)
        uv = u_ref[...].astype(F32)
        s = _sigmoid(gv)
        silu = gv * s
        dg_ref[...] = (da * uv * (s * (1.0 + gv * (1.0 - s)))).astype(BF16)
        du_ref[...] = (da * silu).astype(BF16)
        a_ref[...] = (silu * uv).astype(BF16)

    hid = pl.BlockSpec((tm, tf), lambda i, j: (i, j))
    return _pallas(
        body, name=name, grid=(t // tm, f // tf),
        in_specs=[pl.BlockSpec((tm, d), lambda i, j: (i, 0)), pl.BlockSpec((tf, d), lambda i, j: (j, 0)), hid, hid],
        out_specs=[hid, hid, hid], out_shape=[jax.ShapeDtypeStruct((t, f), BF16)] * 3,
        sem=("parallel", "parallel"), args=(dy, wd, g, u))


def _rel_index(t, s_band):
    return jnp.clip(t + KPAD - s_band, -REL_CLIP, REL_CLIP) + REL_CLIP


def _bias_expand(rel_bias_pad):
    nh = rel_bias_pad.shape[0]

    def body(rb_ref, out_ref):
        rb = rb_ref[...]
        i_io = lax.broadcasted_iota(jnp.int32, (N_REL_PAD, BAND), 0)
        s_io = lax.broadcasted_iota(jnp.int32, (N_REL_PAD, BAND), 1)

        def row(t, carry):
            onehot = (i_io == _rel_index(t, s_io)).astype(F32)
            out_ref[t] = _dot_exact01_r(rb, onehot)
            return carry

        lax.fori_loop(0, CHUNK, row, 0)

    return _pallas(
        body, name="bias_expand", grid=(1,), in_specs=[pl.BlockSpec(rel_bias_pad.shape, lambda i: (0, 0))],
        out_specs=[pl.BlockSpec((CHUNK, nh, BAND), lambda i: (0, 0, 0))],
        out_shape=[jax.ShapeDtypeStruct((CHUNK, nh, BAND), F32)], sem=("arbitrary",), args=(rel_bias_pad,))[0]


def _bias_fold(dbias):
    ng, nh = dbias.shape[0], dbias.shape[2]

    def body(db_ref, out_ref):
        s_io = lax.broadcasted_iota(jnp.int32, (BAND, N_REL_PAD), 0)
        i_io = lax.broadcasted_iota(jnp.int32, (BAND, N_REL_PAD), 1)

        def row(t, acc):
            onehot = (i_io == _rel_index(t, s_io)).astype(F32)
            d = db_ref[0, t]
            for gi in range(1, ng):
                d = d + db_ref[gi, t]
            return acc + _dot_exact01_r(d, onehot)

        out_ref[...] = lax.fori_loop(0, CHUNK, row, jnp.zeros((nh, N_REL_PAD), F32))

    return _pallas(
        body, name="bias_fold", grid=(1,), in_specs=[pl.BlockSpec(dbias.shape, lambda i: (0, 0, 0, 0))],
        out_specs=[pl.BlockSpec((nh, N_REL_PAD), lambda i: (0, 0))],
        out_shape=[jax.ShapeDtypeStruct((nh, N_REL_PAD), F32)], sem=("arbitrary",), args=(dbias,))[0]


def _left_half(shape):
    return lax.broadcasted_iota(jnp.int32, shape, len(shape) - 1) < ATTN_HEAD_DIM


def _stack_heads(v):
    left = _left_half(v.shape)
    zero = jnp.zeros_like(v)
    return jnp.concatenate([jnp.where(left, v, zero), jnp.where(left, zero, v)], axis=0)


def _unstack_heads(v):
    return jnp.where(_left_half((CHUNK, 128)), v[0:CHUNK, :], v[CHUNK:2 * CHUNK, :])


def _half_mean(v):
    r = lax.broadcasted_iota(jnp.int32, (128, 128), 0) < ATTN_HEAD_DIM
    c = lax.broadcasted_iota(jnp.int32, (128, 128), 1) < ATTN_HEAD_DIM
    return _dot_exact01_r(v, r == c) * (1.0 / ATTN_HEAD_DIM)


def _attn_prepare(q_ref, k_ref, v_ref, gq_ref, gk_ref, qs_scr, k_scr, v_scr):
    q, k = q_ref[...], k_ref[...]
    rq = lax.rsqrt(_half_mean(q * q) + RMS_EPS)
    rk = lax.rsqrt(_half_mean(k * k) + RMS_EPS)
    qhat, khat = q * rq, k * rk
    qs_scr[...] = (qhat * gq_ref[...] * ATTN_HEAD_DIM ** -0.5).astype(BF16)
    k_scr[0:KPAD, :] = jnp.zeros((KPAD, 128), BF16)
    v_scr[0:KPAD, :] = jnp.zeros((KPAD, 128), BF16)
    k_scr[KPAD:, :] = (khat * gk_ref[...]).astype(BF16)
    v_scr[KPAD:, :] = v_ref[...].astype(BF16)
    return qhat, rq, khat, rk


def _first_key(c):
    return jnp.maximum(CHUNK, (LEFT_CHUNKS + 1 - c) * CHUNK)


def _attn_fwd(proj, bias, gq, gk, nb, seq):
    nc = seq // CHUNK
    lock = min(ATTN_LOCKSTEP, nc)
    assert nc % lock == 0

    def body(q_ref, k_ref, v_ref, bias_ref, gq_ref, gk_ref, o_ref, qs_scr, k_scr, v_scr):
        _attn_prepare(q_ref, k_ref, v_ref, gq_ref, gk_ref, qs_scr, k_scr, v_scr)

        def chunk(i, carry):
            cs = [i * lock + a for a in range(lock)]
            r0s = [pl.multiple_of(c * CHUNK, CHUNK) for c in cs]
            col = lax.broadcasted_iota(jnp.int32, (2 * CHUNK, BAND), 1)
            ss = [lax.dot_general(_stack_heads(qs_scr[pl.ds(r0, CHUNK), :]), k_scr[pl.ds(r0, BAND), :], NT,
                                  preferred_element_type=F32) for r0 in r0s]
            ss = [jnp.where(col >= _first_key(c), s + bias_ref[...], -jnp.inf) for c, s in zip(cs, ss)]
            ms = [jnp.max(s, axis=-1, keepdims=True) for s in ss]
            es = [jnp.exp(s - m) for s, m in zip(ss, ms)]
            invs = [1.0 / jnp.sum(e, axis=-1, keepdims=True) for e in es]
            os_ = [lax.dot_general(e.astype(BF16), v_scr[pl.ds(r0, BAND), :], NN, preferred_element_type=F32)
                   for e, r0 in zip(es, r0s)]
            for r0, o, inv in zip(r0s, os_, invs):
                o_ref[pl.ds(r0, CHUNK), :] = _unstack_heads(o * inv)
            return carry

        lax.fori_loop(0, nc // lock, chunk, 0, unroll=max(1, min(ATTN_UNROLL, nc) // lock))

    def col(off):
        return pl.BlockSpec((seq, 128), lambda b, hp: (b, off + hp))

    vec = pl.BlockSpec((1, 128), lambda b, hp: (0, 0))
    return _pallas(
        body, name="attn_fwd", grid=(nb, ATTN_HEADS // 2),
        in_specs=[col(0), col(4), col(8), pl.BlockSpec((2 * CHUNK, BAND), lambda b, hp: (hp, 0)), vec, vec],
        out_specs=[pl.BlockSpec((seq, 128), lambda b, hp: (b, hp))],
        out_shape=[jax.ShapeDtypeStruct((nb * seq, ATTN_WIDTH), F32)],
        scratch_shapes=[pltpu.VMEM((seq, 128), BF16), pltpu.VMEM((seq + KPAD, 128), BF16),
                        pltpu.VMEM((seq + KPAD, 128), BF16)],
        sem=("parallel", "parallel"), args=(proj, proj, proj, bias, gq, gk))[0]


def _attn_bwd(proj, out, dout, bias, gq, gk, nb, seq):
    nc = seq // CHUNK
    lock = min(ATTN_LOCKSTEP, nc)
    assert nc % lock == 0
    scale = ATTN_HEAD_DIM ** -0.5

    def body(q_ref, k_ref, v_ref, o_ref, do_ref, bias_ref, gq_ref, gk_ref,
             dq_ref, dk_ref, dv_ref, dbias_ref, dgq_ref, dgk_ref,
             qs_scr, k_scr, v_scr, dqn_scr, dk_scr, dv_scr, db_scr):
        qhat, rq, khat, rk = _attn_prepare(q_ref, k_ref, v_ref, gq_ref, gk_ref, qs_scr, k_scr, v_scr)
        dk_scr[...] = jnp.zeros_like(dk_scr)
        dv_scr[...] = jnp.zeros_like(dv_scr)
        db_scr[...] = jnp.zeros_like(db_scr)

        def one_chunk(c):
            r0 = pl.multiple_of(c * CHUNK, CHUNK)
            qst = _stack_heads(qs_scr[pl.ds(r0, CHUNK), :])
            kb = k_scr[pl.ds(r0, BAND), :]
            vb = v_scr[pl.ds(r0, BAND), :]
            st = lax.dot_general(kb, qst, NT, preferred_element_type=F32) + bias_ref[...]
            dost = _stack_heads(do_ref[pl.ds(r0, CHUNK), :])
            dost16 = dost.astype(BF16)
            dpt = lax.dot_general(vb, dost16, NT, preferred_element_type=F32)
            yield
            key = lax.broadcasted_iota(jnp.int32, (BAND, 2 * CHUNK), 0)
            st = jnp.where(key >= _first_key(c), st, -jnp.inf)
            mx = jnp.max(st, axis=0, keepdims=True)
            drow = _row_sums_on_lanes(dost * _stack_heads(o_ref[pl.ds(r0, CHUNK), :]))
            yield
            et = jnp.exp(st - mx)
            yield
            pt = et * (1.0 / jnp.sum(et, axis=0, keepdims=True))
            yield
            dst = pt * (dpt - drow)
            dst16 = dst.astype(BF16)
            yield
            db_scr[...] += dst
            dqn_scr[pl.ds(r0, CHUNK), :] = scale * _unstack_heads(_dot(dst.T, kb))
            yield
            dk_scr[pl.ds(r0, BAND), :] += lax.dot_general(dst16, qst, NN, preferred_element_type=F32)
            yield
            dv_scr[pl.ds(r0, BAND), :] += lax.dot_general(pt.astype(BF16), dost16, NN, preferred_element_type=F32)

        def chunk(i, carry):
            _lockstep([one_chunk(i * lock + a) for a in range(lock)])
            return carry

        lax.fori_loop(0, nc // lock, chunk, 0, unroll=max(1, min(ATTN_UNROLL, nc) // lock))

        def norm_bwd(dn, hat, r, g_ref):
            gd = dn * g_ref[...]
            return r * (gd - hat * _half_mean(gd * hat)), jnp.sum(dn * hat, axis=0, keepdims=True)

        dq, dgq = norm_bwd(dqn_scr[...], qhat, rq, gq_ref)
        dk, dgk = norm_bwd(dk_scr[KPAD:, :], khat, rk, gk_ref)
        dq_ref[...] = dq.astype(BF16)
        dk_ref[...] = dk.astype(BF16)
        dv_ref[...] = dv_scr[KPAD:, :].astype(BF16)
        dbias_ref[0] = db_scr[...]
        dgq_ref[0] = dgq
        dgk_ref[0] = dgk

    def col(off):
        return pl.BlockSpec((seq, 128), lambda b, hp: (b, off + hp))

    vec = pl.BlockSpec((1, 128), lambda b, hp: (0, 0))
    gvec = pl.BlockSpec((1, 1, 128), lambda b, hp: (b * (ATTN_HEADS // 2) + hp, 0, 0))
    t = nb * seq
    return _pallas(
        body, name="attn_bwd", grid=(nb, ATTN_HEADS // 2),
        in_specs=[col(0), col(4), col(8), col(0), col(0),
                  pl.BlockSpec((BAND, 2 * CHUNK), lambda b, hp: (hp, 0)), vec, vec],
        out_specs=[col(0), col(0), col(0), pl.BlockSpec((1, BAND, 2 * CHUNK), lambda b, hp: (b, hp, 0)),
                   gvec, gvec],
        out_shape=[jax.ShapeDtypeStruct((t, ATTN_WIDTH), BF16)] * 3
        + [jax.ShapeDtypeStruct((nb, ATTN_HEADS // 2 * BAND, 2 * CHUNK), F32)]
        + [jax.ShapeDtypeStruct((nb * ATTN_HEADS // 2, 1, 128), F32)] * 2,
        scratch_shapes=[pltpu.VMEM((seq, 128), BF16), pltpu.VMEM((seq + KPAD, 128), BF16),
                        pltpu.VMEM((seq + KPAD, 128), BF16), pltpu.VMEM((seq, 128), F32),
                        pltpu.VMEM((seq + KPAD, 128), F32), pltpu.VMEM((seq + KPAD, 128), F32),
                        pltpu.VMEM((BAND, 2 * CHUNK), F32)],
        sem=("parallel", "parallel"), args=(proj, proj, proj, out, dout, bias, gq, gk))


def _tri(lower):
    r = lax.broadcasted_iota(jnp.int32, (CHUNK, CHUNK), 0)
    c = lax.broadcasted_iota(jnp.int32, (CHUNK, CHUNK), 1)
    return (r >= c) if lower else (r <= c)


def _hgrn_gates(hq, hf, lb):
    sq = _sigmoid(hq)
    sf = _sigmoid(hf)
    return hq * sq, sq, sf, lb + (1.0 - lb) * sf


def _hgrn_offdiag(q_s, k_s, b_s):
    row = lax.broadcasted_iota(jnp.int32, (CHUNK, HGRN_HEAD_DIM), 0)
    bv, qv, kv = b_s[...], q_s[...], k_s[...]
    eqs, eks = [], []
    for i in range(1, N_SUB):
        r = b_s[pl.ds(SUB * i - 1, 1), :]
        in_i = (row >= SUB * i) & (row < SUB * (i + 1))
        eqs.append(jnp.exp(jnp.where(in_i, bv - r, -jnp.inf)))
        eks.append(jnp.exp(jnp.where(row < SUB * i, r - bv, -jnp.inf)))
    eq = jnp.concatenate(eqs, axis=1)
    ek = jnp.concatenate(eks, axis=1)
    qt = jnp.concatenate([qv] * (N_SUB - 1), axis=1) * eq
    kt = jnp.concatenate([kv] * (N_SUB - 1), axis=1) * ek
    return qt, kt, eq, ek


def _hgrn_diag_e(b_s, i, s):
    t_io = lax.broadcasted_iota(jnp.int32, (SUB, HGRN_HEAD_DIM), 0)
    bi = b_s[pl.ds(SUB * i, SUB), :]
    return jnp.exp(jnp.where(t_io >= s, bi - b_s[pl.ds(SUB * i + s, 1), :], -jnp.inf)), t_io


def _hgrn_intra(q_s, k_s, b_s, a_s, qt, kt):
    ktp = jnp.concatenate([kt, jnp.zeros_like(kt)], axis=0)
    a_s[...] = _dot(qt, ktp, NT)
    yield
    col = lax.broadcasted_iota(jnp.int32, (SUB, HGRN_HEAD_DIM), 1)
    for i in range(N_SUB):
        qi = q_s[pl.ds(SUB * i, SUB), :]
        ai = jnp.zeros((SUB, HGRN_HEAD_DIM), F32)
        for s in range(SUB):
            e, _ = _hgrn_diag_e(b_s, i, s)
            a_col = jnp.sum(qi * k_s[pl.ds(SUB * i + s, 1), :] * e, axis=-1, keepdims=True)
            ai = ai + jnp.where(col == SUB * i + s, a_col, 0.0)
            if s % DIAG_STAGE == DIAG_STAGE - 1:
                yield
        a_s[pl.ds(SUB * i, SUB), :] += ai


def _hgrn_fwd(proj, lb, go, nb, seq):
    nc = seq // CHUNK
    hd = HGRN_HEAD_DIM
    rows_blk = min(HGRN_ROWS, seq)
    nblk, nck = seq // rows_blk, rows_blk // CHUNK

    def body(hq_ref, hf_ref, hi_ref, hg_ref, lb_ref, go_ref, y_ref, o_ref, st_ref, a_ref,
             st_all, q_all, k_all, b_all, a_all):
        @pl.when(pl.program_id(1) == 0)
        def _():
            st_all[...] = jnp.zeros_like(st_all)

        lower = _tri(True)

        def head_chunk(hh, c, rows):
            ln = slice(hd * hh, hd * (hh + 1))
            st, q_s, k_s, b_s, a_s = st_all.at[hh], q_all.at[hh], k_all.at[hh], b_all.at[hh], a_all.at[hh]
            q, _, _, f = _hgrn_gates(hq_ref[rows, ln], hf_ref[rows, ln], lb_ref[:, ln])
            v = hi_ref[rows, ln]
            yield
            b = _dot_exact01(lower, jnp.log(f))
            q_s[...] = q
            k_s[...] = 1.0 - f
            b_s[...] = b
            st_ref[hh, c] = st[...]
            yield
            qt, kt, _, _ = _hgrn_offdiag(q_s, k_s, b_s)
            yield
            yield from _hgrn_intra(q_s, k_s, b_s, a_s, qt, kt)
            a16 = a_s[...].astype(BF16)
            a_ref[hh, c] = a16
            vp = jnp.concatenate([v, jnp.zeros_like(v)], axis=0)
            o = _dot(a16, vp) + _dot(q * jnp.exp(b), st[...], NT)
            yield
            bl = b_s[pl.ds(CHUNK - 1, 1), :]
            st[...] = st[...] * jnp.exp(bl) + _tn(v, (1.0 - f) * jnp.exp(bl - b))
            o_ref[rows, ln] = o
            yield
            n = o * lax.rsqrt(jnp.mean(o * o, axis=-1, keepdims=True) + RMS_EPS) * go_ref[...]
            hg = hg_ref[rows, ln]
            y_ref[rows, ln] = n * hg * _sigmoid(hg)

        def chunk(c, carry):
            rows = pl.ds(pl.multiple_of(c * CHUNK, CHUNK), CHUNK)
            _lockstep([head_chunk(hh, c, rows) for hh in range(HGRN_HEADS)])
            return carry

        lax.fori_loop(0, nck, chunk, 0)

    hp, wide = HGRN_HEADS, HGRN_HEADS * hd

    def col(off):
        return pl.BlockSpec((rows_blk, wide), lambda b, s: (b * nblk + s, off // hp))

    out = pl.BlockSpec((rows_blk, wide), lambda b, s: (b * nblk + s, 0))
    t = nb * seq
    return _pallas(
        body, name="hgrn_fwd", grid=(nb, nblk),
        in_specs=[col(12), col(16), col(20), col(24), pl.BlockSpec((1, wide), lambda b, s: (0, 0)),
                  pl.BlockSpec((1, hd), lambda b, s: (0, 0))],
        out_specs=[out, out, pl.BlockSpec((hp, nck, hd, hd), lambda b, s: (b, s, 0, 0)),
                   pl.BlockSpec((hp, nck, CHUNK, hd), lambda b, s: (b, s, 0, 0))],
        out_shape=[jax.ShapeDtypeStruct((t, wide), F32)] * 2
        + [jax.ShapeDtypeStruct((nb * hp, nc, hd, hd), F32), jax.ShapeDtypeStruct((nb * hp, nc, CHUNK, hd), BF16)],
        scratch_shapes=[pltpu.VMEM((hp, hd, hd), F32)] + [pltpu.VMEM((hp, CHUNK, hd), F32)] * 4,
        sem=("parallel", "arbitrary"), args=(proj, proj, proj, proj, lb, go))


def _hgrn_bwd(proj, lb, go, o_pre, states, scores, dout, nb, seq):
    nc = seq // CHUNK
    hd = HGRN_HEAD_DIM
    rows_blk = min(HGRN_ROWS, seq)
    nblk, nck = seq // rows_blk, rows_blk // CHUNK

    def body(hq_ref, hf_ref, hi_ref, hg_ref, lb_ref, go_ref, o_ref, st_ref, a_ref, dy_ref,
             dhq_ref, dhf_ref, dhi_ref, dhg_ref, dlb_ref, dgo_ref,
             dst_all, q_all, k_all, b_all, da_all, dqi_all, dki_all, dlb_all, dgo_all):
        @pl.when(pl.program_id(1) == 0)
        def _():
            dst_all[...] = jnp.zeros_like(dst_all)
            dlb_all[...] = jnp.zeros_like(dlb_all)
            dgo_all[...] = jnp.zeros_like(dgo_all)

        lower, upper = _tri(True), _tri(False)
        gov = go_ref[...]
        row = lax.broadcasted_iota(jnp.int32, (CHUNK, hd), 0)

        def head_chunk(hh, c, rows):
            ln = slice(hd * hh, hd * (hh + 1))
            dst, q_s, k_s, b_s = dst_all.at[hh], q_all.at[hh], k_all.at[hh], b_all.at[hh]
            da_s, dqi_s, dki_s = da_all.at[hh], dqi_all.at[hh], dki_all.at[hh]
            dlb_acc, dgo_acc = dlb_all.at[hh], dgo_all.at[hh]
            lbv = lb_ref[:, ln]
            hq, hf, v, hg = hq_ref[rows, ln], hf_ref[rows, ln], hi_ref[rows, ln], hg_ref[rows, ln]
            q, sq, sf, f = _hgrn_gates(hq, hf, lbv)
            kk = 1.0 - f
            yield
            b = _dot_exact01(lower, jnp.log(f))
            q_s[...] = q
            k_s[...] = kk
            b_s[...] = b
            yield
            bl = b_s[pl.ds(CHUNK - 1, 1), :]
            ebl = jnp.exp(bl)
            ekd = jnp.exp(bl - b)
            kd = kk * ekd
            eb = jnp.exp(b)
            qb = q * eb
            st0 = st_ref[hh, c]
            dst1 = dst[...]
            yield

            o = o_ref[rows, ln]
            dy = dy_ref[rows, ln]
            sg = _sigmoid(hg)
            rstd = lax.rsqrt(jnp.mean(o * o, axis=-1, keepdims=True) + RMS_EPS)
            ohat = o * rstd
            dn = dy * hg * sg
            dhg_ref[rows, ln] = (dy * ohat * gov * (sg * (1.0 + hg * (1.0 - sg)))).astype(BF16)
            dgo_acc[...] += jnp.sum(dn * ohat, axis=0, keepdims=True)
            gdn = dn * gov
            do = rstd * (gdn - ohat * jnp.mean(gdn * ohat, axis=-1, keepdims=True))
            yield

            qt, kt, eq, ek = _hgrn_offdiag(q_s, k_s, b_s)
            da = _dot(do, v, NT)
            dat = _dot(v, do, NT)
            da_s[...] = da
            yield
            dqo = _dot(da, kt) * eq
            dko = _dot(dat, qt) * ek
            dqi_s[...] = sum(dqo[:, j * hd:(j + 1) * hd] for j in range(N_SUB - 1))
            dki_s[...] = sum(dko[:, j * hd:(j + 1) * hd] for j in range(N_SUB - 1))
            yield
            col = lax.broadcasted_iota(jnp.int32, (SUB, CHUNK), 1)
            for i in range(N_SUB):
                qi = q_s[pl.ds(SUB * i, SUB), :]
                dai = da_s[pl.ds(SUB * i, SUB), :]
                dqd = jnp.zeros((SUB, hd), F32)
                dkd_ = jnp.zeros((SUB, hd), F32)
                for s in range(SUB):
                    e, t_io = _hgrn_diag_e(b_s, i, s)
                    dacol = jnp.sum(jnp.where(col == SUB * i + s, dai, 0.0), axis=-1, keepdims=True)
                    w = dacol * e
                    dqd = dqd + w * k_s[pl.ds(SUB * i + s, 1), :]
                    dkd_ = dkd_ + jnp.where(t_io == s, jnp.sum(w * qi, axis=0, keepdims=True), 0.0)
                    if s % DIAG_STAGE == DIAG_STAGE - 1:
                        yield
                dqi_s[pl.ds(SUB * i, SUB), :] += dqd
                dki_s[pl.ds(SUB * i, SUB), :] += dkd_
            dqi, dki = dqi_s[...], dki_s[...]

            dv = _tn(a_ref[hh, c].astype(F32), do)[0:CHUNK, :] + _dot(kd, dst1, NT)
            dqb = _dot(do, st0)
            dkd = _dot(v, dst1)
            yield
            t2 = dkd * kd
            dq = dqb * eb + dqi
            dk = dkd * ekd + dki
            dbl = jnp.sum(t2, axis=0, keepdims=True) + ebl * jnp.sum(st0 * dst1, axis=0, keepdims=True)
            db = dqb * qb - t2 + q * dqi - kk * dki + jnp.where(row == CHUNK - 1, dbl, 0.0)
            yield
            dg = _dot_exact01(upper, db)
            dst[...] = dst1 * ebl + _tn(do, qb)
            yield

            df = dg / f - dk
            dhf_ref[rows, ln] = (df * (1.0 - lbv) * sf * (1.0 - sf)).astype(BF16)
            dlb_acc[...] += jnp.sum(df * (1.0 - sf), axis=0, keepdims=True)
            dhq_ref[rows, ln] = (dq * (sq * (1.0 + hq * (1.0 - sq)))).astype(BF16)
            dhi_ref[rows, ln] = dv.astype(BF16)

        def chunk(it, carry):
            c = nck - 1 - it
            rows = pl.ds(pl.multiple_of(c * CHUNK, CHUNK), CHUNK)
            _lockstep([head_chunk(hh, c, rows) for hh in range(HGRN_HEADS)])
            return carry

        lax.fori_loop(0, nck, chunk, 0)

        @pl.when(pl.program_id(1) == nblk - 1)
        def _():
            dlb_ref[...] = dlb_all[...]
            dgo_ref[...] = dgo_all[...]

    hp, wide = HGRN_HEADS, HGRN_HEADS * hd

    def col(off):
        return pl.BlockSpec((rows_blk, wide), lambda b, s: (b * nblk + nblk - 1 - s, off // hp))

    out = pl.BlockSpec((rows_blk, wide), lambda b, s: (b * nblk + nblk - 1 - s, 0))
    part = pl.BlockSpec((hp, 1, hd), lambda b, s: (b, 0, 0))
    t = nb * seq
    return pl.pallas_call(
        body, name="hgrn_bwd", grid=(nb, nblk),
        in_specs=[col(12), col(16), col(20), col(24), pl.BlockSpec((1, wide), lambda b, s: (0, 0)),
                  pl.BlockSpec((1, hd), lambda b, s: (0, 0)), out,
                  pl.BlockSpec((hp, nck, hd, hd), lambda b, s: (b, nblk - 1 - s, 0, 0)),
                  pl.BlockSpec((hp, nck, CHUNK, hd), lambda b, s: (b, nblk - 1 - s, 0, 0)), col(4)],
        out_specs=[out, out, out, out, part, part],
        out_shape=[jax.ShapeDtypeStruct((t, wide), BF16)] * 4 + [jax.ShapeDtypeStruct((nb * hp, 1, hd), F32)] * 2,
        scratch_shapes=[pltpu.VMEM((hp, hd, hd), F32)] + [pltpu.VMEM((hp, CHUNK, hd), F32)] * 3
        + [pltpu.VMEM((hp, CHUNK, CHUNK), F32)] + [pltpu.VMEM((hp, CHUNK, hd), F32)] * 2
        + [pltpu.VMEM((hp, 1, hd), F32)] * 2,
        compiler_params=_params("parallel", "arbitrary"),
    )(proj, proj, proj, proj, lb, go, o_pre, states, scores, dout)


def _lb_fwd(lower_bounds):
    def body(x_ref, o_ref):
        xv = x_ref[...]
        e = jnp.exp(xv - jnp.max(xv, axis=0, keepdims=True))
        o_ref[...] = e[0:1, :] / jnp.sum(e, axis=0, keepdims=True)

    return pl.pallas_call(body, name="lb_fwd",
                          out_shape=jax.ShapeDtypeStruct((1, lower_bounds.shape[1]), F32))(lower_bounds)


def _lb_bwd(lower_bounds, dlb_parts):
    ng = dlb_parts.shape[0]

    def body(x_ref, d_ref, o_ref):
        xv = x_ref[...]
        e = jnp.exp(xv - jnp.max(xv, axis=0, keepdims=True))
        p = e / jnp.sum(e, axis=0, keepdims=True)
        dlb = d_ref[0]
        for gi in range(1, ng):
            dlb = dlb + d_ref[gi]
        first = lax.broadcasted_iota(jnp.int32, xv.shape, 0) == 0
        o_ref[...] = p * (jnp.where(first, dlb, 0.0) - p[0:1, :] * dlb)

    return pl.pallas_call(body, name="lb_bwd",
                          out_shape=jax.ShapeDtypeStruct(lower_bounds.shape, F32))(lower_bounds, dlb_parts)


def _ffn_bwd(x, g, h, gate, up, dy, dy16, w, put, tag):
    wg, wu, wd = w[tag + "_w_gate"], w[tag + "_w_up"], w[tag + "_w_down"]
    dgate, dup, act = _ffn_bwd_mid(dy16, wd, gate, up, tag + "_bwd_mid")
    put(tag + "_w_down", _mm(act, dy16, ta=True, tm=1408, tn=512, scale=0.5, name=tag + "_dwd"))
    put(tag + "_w_gate", _mm(dgate, h, ta=True, tm=1408, tn=512, name=tag + "_dwg"))
    put(tag + "_w_up", _mm(dup, h, ta=True, tm=1408, tn=512, name=tag + "_dwu"))
    dh = _mm(dgate, wg, tm=512, tn=1024, name=tag + "_dh_gate")
    return _mm(dup, wu, tm=512, tn=1024, add=dh, norm_bwd=(x, g, dy), name=tag + "_dh_up")


def _local_step(x, tgt, sp, w, put, nb, seq):
    d = x.shape[1]
    h1 = _rms_fwd(x, sp["ffn1_norm_g"], "ffn1_norm")
    rb_pad = jnp.pad(sp["attn_rel_bias"], ((0, 0), (0, N_REL_PAD - N_REL)))
    bias = jnp.transpose(_bias_expand(rb_pad), (1, 0, 2)).reshape(ATTN_HEADS * CHUNK, BAND)
    gq2 = jnp.concatenate([sp["attn_q_norm_g"]] * 2, axis=1)
    gk2 = jnp.concatenate([sp["attn_k_norm_g"]] * 2, axis=1)
    lb = _lb_fwd(sp["hgrn_lower_bounds"])
    gate1, up1, x1, h2 = _ffn_fwd(h1, x, w["ffn1_w_gate"], w["ffn1_w_up"], w["ffn1_w_down"], "ffn1_fwd",
                                  next_g=sp["mix_norm_g"])
    proj = _mm(h2, w["w_in"], tb=True, tm=256, tn=w["w_in"].shape[0], name="in_proj")
    attn = _attn_fwd(proj, bias, gq2, gk2, nb, seq)
    hy, ho, hstate, hscore = _hgrn_fwd(proj, lb, sp["hgrn_out_norm_g"], nb, seq)
    mix = jnp.concatenate([attn, hy], axis=1)
    x2, h3 = _mm(mix, w["w_out"], tm=512, tn=1024, add=x1, norm_g=sp["ffn2_norm_g"], name="out_proj")
    gate2, up2, dx3, dx3_16, sq = _ffn_fwd(h3, x2, w["ffn2_w_gate"], w["ffn2_w_up"], w["ffn2_w_down"], "ffn2_fwd",
                                           tgt=tgt)
    loss = 0.5 * jnp.sum(sq) / d

    dx2, dx2_16, dg3 = _ffn_bwd(x2, sp["ffn2_norm_g"], h3, gate2, up2, dx3, dx3_16, w, put, "ffn2")
    dmix = _mm(dx2_16, w["w_out"], tb=True, tm=512, tn=1024, name="out_proj_dx")
    put("w_out", _mm(mix, dx2_16, ta=True, tm=512, tn=1024, name="out_proj_dw"))
    bias_t = jnp.transpose(bias.reshape(ATTN_HEADS // 2, 2 * CHUNK, BAND), (0, 2, 1)).reshape(-1, 2 * CHUNK)
    dq, dk, dv, dbias, dgq, dgk = _attn_bwd(proj, attn, dmix, bias_t, gq2, gk2, nb, seq)
    dbias = jnp.transpose(dbias.reshape(nb, ATTN_HEADS // 2, BAND, 2, CHUNK), (0, 4, 1, 3, 2))
    dbias = dbias.reshape(nb, CHUNK, ATTN_HEADS, BAND)
    dgq = jnp.sum(dgq, axis=(0, 1)).reshape(2, ATTN_HEAD_DIM).sum(axis=0, keepdims=True)
    dgk = jnp.sum(dgk, axis=(0, 1)).reshape(2, ATTN_HEAD_DIM).sum(axis=0, keepdims=True)
    dhq, dhf, dhi, dhg, dlb, dgo = _hgrn_bwd(proj, lb, sp["hgrn_out_norm_g"], ho, hstate, hscore, dmix, nb, seq)
    dproj = jnp.concatenate([dq, dk, dv, dhq, dhf, dhi, dhg], axis=1)
    put("w_in", _mm(dproj, h2, ta=True, tm=512, tn=1024, name="in_proj_dw"))
    dx1, dx1_16, dgm = _mm(dproj, w["w_in"], tm=512, tn=1024, norm_bwd=(x1, sp["mix_norm_g"], dx2),
                           name="in_proj_dx")
    dx0, _, dg1 = _ffn_bwd(x, sp["ffn1_norm_g"], h1, gate1, up1, dx1, dx1_16, w, put, "ffn1")

    small = {
        "ffn1_norm_g": dg1, "mix_norm_g": dgm, "ffn2_norm_g": dg3,
        "attn_q_norm_g": dgq, "attn_k_norm_g": dgk,
        "attn_rel_bias": _bias_fold(dbias)[:, :N_REL],
        "hgrn_lower_bounds": _lb_bwd(sp["hgrn_lower_bounds"], dlb.reshape(nb, 1, HGRN_HEADS * HGRN_HEAD_DIM)),
        "hgrn_out_norm_g": jnp.sum(dgo, axis=(0, 1))[None, :],
    }
    return loss, dx0, small


MESH = pl.DeviceIdType.MESH
ANY = pl.BlockSpec(memory_space=pl.ANY)


def _coords():
    return lax.axis_index("x"), lax.axis_index("y"), lax.axis_index("c")


def _other_chips(x, y):
    return [(1 - x, y), (x, 1 - y), (1 - x, 1 - y)]


def _gather_side(shards):
    n = len(shards)

    def copies(ins, outs, sems):
        send_sems, recv_sems, local_sems = sems
        x, y, c = _coords()
        me, sibling = (x, y, c), (x, y, 1 - c)
        chips = _other_chips(x, y)

        def copy(i, k, block, to, src=None):
            bx, by, bc = block
            dst = outs[i].at[4 * bx + 2 * by + bc]
            return pltpu.make_async_remote_copy(
                src_ref=dst if src is None else src, dst_ref=dst, send_sem=send_sems.at[i, k],
                recv_sem=recv_sems.at[i, k], device_id=to, device_id_type=MESH)

        mine = [pltpu.make_async_copy(ins[i], outs[i].at[4 * x + 2 * y + c], local_sems.at[i]) for i in range(n)]
        own = []
        for i in range(n):
            own.append(copy(i, 0, me, sibling, src=ins[i]))
            own += [copy(i, 1 + j, me, (*chip, c), src=ins[i]) for j, chip in enumerate(chips)]
        return copy, mine, own, me, sibling, chips, c

    def start(ins, outs, sems):
        _, mine, own, *_ = copies(ins, outs, sems)
        for cp in mine + own:
            cp.start()

    def finish(ins, outs, sems):
        copy, mine, own, me, sibling, chips, c = copies(ins, outs, sems)
        passed = []
        for i in range(n):
            for j, chip in enumerate(chips):
                copy(i, 1 + j, (*chip, c), me).wait_recv()
                passed.append(copy(i, 4 + j, (*chip, c), sibling))
                passed[-1].start()
        for i in range(n):
            copy(i, 0, sibling, me).wait_recv()
            for j, chip in enumerate(chips):
                copy(i, 4 + j, (*chip, 1 - c), me).wait_recv()
        for cp in own + passed:
            cp.wait_send()
        for cp in mine:
            cp.wait()

    return _Side(list(shards), [jax.ShapeDtypeStruct((N_DEV,) + s.shape, s.dtype) for s in shards],
                 [pltpu.SemaphoreType.DMA((n, 7)), pltpu.SemaphoreType.DMA((n, 7)), pltpu.SemaphoreType.DMA((n,))],
                 start, finish)


def _pair_side(grads):
    n = len(grads)

    def copies(ins, outs, sems):
        send_sems, recv_sems = sems
        x, y, c = _coords()
        return [pltpu.make_async_remote_copy(
            src_ref=ins[i].at[2 * k + 1 - c], dst_ref=outs[i].at[k], send_sem=send_sems.at[i, k],
            recv_sem=recv_sems.at[i, k], device_id=(x, y, 1 - c), device_id_type=MESH)
            for i in range(n) for k in range(4)]

    def start(ins, outs, sems):
        for cp in copies(ins, outs, sems):
            cp.start()

    def finish(ins, outs, sems):
        for cp in copies(ins, outs, sems):
            cp.wait()

    return _Side(list(grads), [jax.ShapeDtypeStruct((4,) + g.shape[1:], g.dtype) for g in grads],
                 [pltpu.SemaphoreType.DMA((n, 4)), pltpu.SemaphoreType.DMA((n, 4))], start, finish)


def _pair_add(grad, recv, core, name):
    _, r, cdim = grad.shape

    def body(c_ref, g_ref, r_ref, o_ref):
        o_ref[...] = (g_ref[...] + r_ref[...]).astype(BF16)

    blk = (1, r, cdim)
    return pl.pallas_call(
        body, name=name,
        grid_spec=pltpu.PrefetchScalarGridSpec(
            num_scalar_prefetch=1, grid=(4,),
            in_specs=[pl.BlockSpec(blk, lambda k, c_ref: (2 * k + c_ref[0], 0, 0)),
                      pl.BlockSpec(blk, lambda k, c_ref: (k, 0, 0))],
            out_specs=pl.BlockSpec(blk, lambda k, c_ref: (k, 0, 0))),
        out_shape=jax.ShapeDtypeStruct((4, r, cdim), BF16),
        compiler_params=_params("arbitrary"),
    )(core, grad, recv)


def _chip_side(parts):
    n = len(parts)

    def copies(ins, outs, sems):
        send_sems, recv_sems, local_sems = sems
        x, y, c = _coords()
        chips = _other_chips(x, y)
        mine = [pltpu.make_async_copy(ins[i].at[2 * x + y], outs[i].at[2 * x + y], local_sems.at[i])
                for i in range(n)]
        sent = [pltpu.make_async_remote_copy(
            src_ref=ins[i].at[2 * px + py], dst_ref=outs[i].at[2 * x + y], send_sem=send_sems.at[i, j],
            recv_sem=recv_sems.at[i, j], device_id=(px, py, c), device_id_type=MESH)
            for i in range(n) for j, (px, py) in enumerate(chips)]
        return mine, sent, chips, c

    def start(ins, outs, sems):
        mine, sent, _, _ = copies(ins, outs, sems)
        for cp in mine + sent:
            cp.start()

    def finish(ins, outs, sems):
        mine, sent, chips, c = copies(ins, outs, sems)
        send_sems, recv_sems, _ = sems
        for i in range(n):
            for j, (px, py) in enumerate(chips):
                landed = outs[i].at[2 * px + py]
                pltpu.make_async_remote_copy(
                    src_ref=landed, dst_ref=landed, send_sem=send_sems.at[i, j], recv_sem=recv_sems.at[i, j],
                    device_id=(px, py, c), device_id_type=MESH).wait_recv()
        for cp in sent:
            cp.wait_send()
        for cp in mine:
            cp.wait()

    return _Side(list(parts), [jax.ShapeDtypeStruct(p.shape, p.dtype) for p in parts],
                 [pltpu.SemaphoreType.DMA((n, 3)), pltpu.SemaphoreType.DMA((n, 3)), pltpu.SemaphoreType.DMA((n,))],
                 start, finish)


def _all_reduce_small(v):
    r = v.shape[0]

    def body(v_ref, o_ref, buf, send_sems, recv_sems):
        x, y, c = _coords()
        me = 4 * x + 2 * y + c
        buf[me] = v_ref[...]
        cps = []
        for k in range(1, N_DEV):
            px = 1 - x if k & 4 else x
            py = 1 - y if k & 2 else y
            pc = 1 - c if k & 1 else c
            cps.append((pltpu.make_async_remote_copy(
                src_ref=v_ref, dst_ref=buf.at[me], send_sem=send_sems.at[k - 1], recv_sem=recv_sems.at[k - 1],
                device_id=(px, py, pc), device_id_type=MESH), 4 * px + 2 * py + pc))
        for cp, _ in cps:
            cp.start()
        for k, (cp, peer) in enumerate(cps):
            pltpu.make_async_remote_copy(
                src_ref=v_ref, dst_ref=buf.at[peer], send_sem=send_sems.at[k], recv_sem=recv_sems.at[k],
                device_id=(x, y, c), device_id_type=MESH).wait_recv()
        for cp, _ in cps:
            cp.wait_send()
        acc = buf[0]
        for j in range(1, N_DEV):
            acc = acc + buf[j]
        o_ref[...] = acc

    return pl.pallas_call(
        body, name="small_all_reduce", out_shape=jax.ShapeDtypeStruct(v.shape, F32),
        in_specs=[pl.BlockSpec(memory_space=pltpu.VMEM)], out_specs=pl.BlockSpec(memory_space=pltpu.VMEM),
        scratch_shapes=[pltpu.VMEM((N_DEV, r, 128), F32), pltpu.SemaphoreType.DMA((N_DEV - 1,)),
                        pltpu.SemaphoreType.DMA((N_DEV - 1,))],
    )(v)


def _adamw(w, m, v, g, name):
    parts = w.ndim == 3
    r, cdim = w.shape[-2:]
    tr = r // 4 if r % 32 == 0 else r

    def body(w_ref, m_ref, v_ref, g_ref, go_ref, d_ref, mo_ref, vo_ref):
        if parts:
            gv = g_ref[0].astype(F32)
            for k in range(1, 4):
                gv = gv + g_ref[k].astype(F32)
            gv = gv[None]
        else:
            gv = g_ref[...]
        m2 = ADAM_B1 * m_ref[...] + (1.0 - ADAM_B1) * gv
        v2 = ADAM_B2 * v_ref[...] + (1.0 - ADAM_B2) * (gv * gv)
        m_hat = m2 / (1.0 - ADAM_B1 ** ADAM_STEP)
        v_hat = v2 / (1.0 - ADAM_B2 ** ADAM_STEP)
        go_ref[...] = gv
        d_ref[...] = -ADAM_LR * (m_hat / (jnp.sqrt(v_hat) + ADAM_EPS) + ADAM_WD * w_ref[...])
        mo_ref[...] = m2
        vo_ref[...] = v2

    if parts:
        row = pl.BlockSpec((1, tr, cdim), lambda i: (0, i, 0))
        g_spec = pl.BlockSpec((4, tr, cdim), lambda i: (0, i, 0))
    else:
        row = g_spec = pl.BlockSpec((tr, cdim), lambda i: (i, 0))
    return pl.pallas_call(
        body, name=name, grid=(r // tr,), in_specs=[row, row, row, g_spec], out_specs=[row] * 4,
        out_shape=[jax.ShapeDtypeStruct(w.shape, F32)] * 4,
        compiler_params=_params("parallel"),
    )(w, m, v, g)


WEIGHTS = ["ffn1_norm_g", "ffn1_w_gate", "ffn1_w_up", "ffn1_w_down", "mix_norm_g", "w_in", "attn_q_norm_g",
           "attn_k_norm_g", "attn_rel_bias", "hgrn_lower_bounds", "hgrn_out_norm_g", "w_out", "ffn2_norm_g",
           "ffn2_w_gate", "ffn2_w_up", "ffn2_w_down"]
COL_SHARDED = ("ffn1_w_gate", "ffn1_w_up", "w_in", "ffn2_w_gate", "ffn2_w_up")
ROW_SHARDED = ("ffn1_w_down", "w_out", "ffn2_w_down")
BIG = [n for n in WEIGHTS if n in COL_SHARDED or n in ROW_SHARDED]
SMALL = [n for n in WEIGHTS if n not in BIG]
PACK_ROWS = 8
FFN2 = ["ffn2_w_down", "ffn2_w_gate", "ffn2_w_up"]
MIXER = ["w_out", "w_in"]

PLAN = {
    "ffn1_norm": [("gather", ["ffn1_w_down"])],
    "bias_expand": [("gather", ["ffn1_w_gate", "ffn1_w_up"])],
    "ffn1_fwd": [("gather", MIXER)],
    "attn_fwd": [("gather", ["ffn2_w_down"])],
    "hgrn_fwd": [("gather", ["ffn2_w_gate", "ffn2_w_up"])],
    "ffn2_dh_gate": [("pair", FFN2)],
    "attn_bwd": [("chip", FFN2)],
    "in_proj_dx": [("pair", MIXER)],
    "ffn1_bwd_mid": [("chip", MIXER)],
    "ffn1_dwg": [("pair", ["ffn1_w_down"])],
    "ffn1_dwu": [("chip", ["ffn1_w_down"]), ("pair", ["ffn1_w_gate"])],
    "ffn1_dh_gate": [("chip", ["ffn1_w_gate"]), ("pair", ["ffn1_w_up"])],
    "bias_fold": [("chip", ["ffn1_w_up"])],
}


def _join_sides(sides):
    def split(refs, counts):
        out, at = [], 0
        for n in counts:
            out.append(refs[at:at + n])
            at += n
        return out

    n_in, n_out, n_sem = ([len(getattr(s, f)) for s in sides] for f in ("ins", "out_shape", "sems"))

    def run(which):
        def go(ins, outs, sems):
            for s, i, o, m in zip(sides, split(ins, n_in), split(outs, n_out), split(sems, n_sem)):
                getattr(s, which)(i, o, m)
        return go

    return _Side([a for s in sides for a in s.ins], [a for s in sides for a in s.out_shape],
                 [a for s in sides for a in s.sems], run("start"), run("finish"))


class _Schedule:
    def __init__(self, shards):
        self.shards = shards
        self.weights = {}
        self.sliced = {}
        self.partials = {}
        self.reduced = {}

    def put(self, name, grad):
        self.sliced[name] = grad.reshape((N_DEV,) + self.shards[name].shape)

    def side_for(self, call):
        if call not in PLAN:
            return None
        sides = []
        for kind, names in PLAN[call]:
            if kind == "gather":
                sides.append(_gather_side([self.shards[n] for n in names]))
            elif kind == "pair":
                sides.append(_pair_side([self.sliced[n] for n in names]))
            else:
                sides.append(_chip_side([self.partials[n] for n in names]))
        return _join_sides(sides)

    def done(self, call, outs):
        at = 0
        for kind, names in PLAN[call]:
            self.file(kind, names, outs[at:at + len(names)])
            at += len(names)

    def file(self, kind, names, outs):
        for n, o in zip(names, outs):
            if kind == "gather":
                self.weights[n] = o.reshape(N_DEV * o.shape[1], o.shape[2])
            elif kind == "pair":
                core = lax.axis_index("c").astype(jnp.int32).reshape(1)
                self.partials[n] = _pair_add(self.sliced[n], o, core, n + "_pair_add")
            else:
                self.reduced[n] = o


def _pack_small(vals, loss=None):
    parts = []
    for n in SMALL:
        a = vals[n]
        if n == "attn_rel_bias":
            a = jnp.pad(a.reshape(ATTN_HEADS, N_REL), ((0, 0), (0, N_REL_PAD - N_REL)))
        flat = a.reshape(-1)
        size = -(-flat.shape[0] // (PACK_ROWS * 128)) * PACK_ROWS * 128
        parts.append(jnp.pad(flat, (0, size - flat.shape[0])).reshape(-1, 128))
    tail = jnp.zeros((PACK_ROWS, 128), F32)
    if loss is not None:
        tail = tail.at[0, 0].set(loss)
    return jnp.concatenate(parts + [tail], axis=0)


def _unpack_small(packed, shapes):
    out, row = {}, 0
    for n in SMALL:
        shape = shapes[n]
        if n == "attn_rel_bias":
            rows = ATTN_HEADS * N_REL_PAD // 128
            out[n] = packed[row:row + rows].reshape(ATTN_HEADS, N_REL_PAD)[:, :N_REL].reshape(shape)
        else:
            size = 1
            for s in shape:
                size *= s
            rows = -(-size // (PACK_ROWS * 128)) * PACK_ROWS
            out[n] = packed[row:row + rows].reshape(-1)[:size].reshape(shape)
        row += rows
    return out, packed[row, 0]


def kernel(x, ffn1_norm_g, ffn1_w_gate, ffn1_w_up, ffn1_w_down, mix_norm_g, w_in, attn_q_norm_g, attn_k_norm_g, attn_rel_bias, hgrn_lower_bounds, hgrn_out_norm_g, w_out, ffn2_norm_g, ffn2_w_gate, ffn2_w_up, ffn2_w_down, loss_target, m_ffn1_norm_g, m_ffn1_w_gate, m_ffn1_w_up, m_ffn1_w_down, m_mix_norm_g, m_w_in, m_attn_q_norm_g, m_attn_k_norm_g, m_attn_rel_bias, m_hgrn_lower_bounds, m_hgrn_out_norm_g, m_w_out, m_ffn2_norm_g, m_ffn2_w_gate, m_ffn2_w_up, m_ffn2_w_down, v_ffn1_norm_g, v_ffn1_w_gate, v_ffn1_w_up, v_ffn1_w_down, v_mix_norm_g, v_w_in, v_attn_q_norm_g, v_attn_k_norm_g, v_attn_rel_bias, v_hgrn_lower_bounds, v_hgrn_out_norm_g, v_w_out, v_ffn2_norm_g, v_ffn2_w_gate, v_ffn2_w_up, v_ffn2_w_down):
    wts = dict(zip(WEIGHTS, (ffn1_norm_g, ffn1_w_gate, ffn1_w_up, ffn1_w_down, mix_norm_g, w_in, attn_q_norm_g,
                             attn_k_norm_g, attn_rel_bias, hgrn_lower_bounds, hgrn_out_norm_g, w_out, ffn2_norm_g,
                             ffn2_w_gate, ffn2_w_up, ffn2_w_down)))
    mom = dict(zip(WEIGHTS, (m_ffn1_norm_g, m_ffn1_w_gate, m_ffn1_w_up, m_ffn1_w_down, m_mix_norm_g, m_w_in,
                             m_attn_q_norm_g, m_attn_k_norm_g, m_attn_rel_bias, m_hgrn_lower_bounds,
                             m_hgrn_out_norm_g, m_w_out, m_ffn2_norm_g, m_ffn2_w_gate, m_ffn2_w_up, m_ffn2_w_down)))
    var = dict(zip(WEIGHTS, (v_ffn1_norm_g, v_ffn1_w_gate, v_ffn1_w_up, v_ffn1_w_down, v_mix_norm_g, v_w_in,
                             v_attn_q_norm_g, v_attn_k_norm_g, v_attn_rel_bias, v_hgrn_lower_bounds,
                             v_hgrn_out_norm_g, v_w_out, v_ffn2_norm_g, v_ffn2_w_gate, v_ffn2_w_up, v_ffn2_w_down)))
    nb, seq, d = x.shape
    shapes = {n: wts[n].shape for n in WEIGHTS}

    def rows_first(a, n):
        return jnp.swapaxes(a, 1, 2) if n in COL_SHARDED else a

    sched = _Schedule({n: rows_first(wts[n], n)[0].astype(BF16) for n in BIG})
    sp = {n: wts[n] for n in SMALL}
    sp["attn_rel_bias"] = wts["attn_rel_bias"][0]
    _ACTIVE[0] = sched
    try:
        loss, dx, dsmall = _local_step(x.reshape(nb * seq, d), loss_target.reshape(nb * seq, d), sp,
                                       sched.weights, sched.put, nb, seq)
    finally:
        _ACTIVE[0] = None
    reduced = sched.reduced

    small_sum = _all_reduce_small(_pack_small(dsmall, loss))
    gsmall, loss_total = _unpack_small(small_sum, shapes)

    grads, deltas, new_m, new_v = {}, {}, {}, {}
    for n in BIG:
        out = _adamw(rows_first(wts[n], n), rows_first(mom[n], n), rows_first(var[n], n), reduced[n], n + "_adamw")
        grads[n], deltas[n], new_m[n], new_v[n] = (rows_first(o, n) for o in out)
    packed = _adamw(_pack_small(wts), _pack_small(mom), _pack_small(var), small_sum, "small_adamw")
    for dst, p in zip((deltas, new_m, new_v), packed[1:]):
        dst.update(_unpack_small(p, shapes)[0])
    grads.update(gsmall)

    return (loss_total, dx.reshape(nb, seq, d), *[grads[n] for n in WEIGHTS], *[deltas[n] for n in WEIGHTS],
            *[new_m[n] for n in WEIGHTS], *[new_v[n] for n in WEIGHTS])
```

```python
import functools

import jax
import jax.numpy as jnp
from jax import lax
from jax.experimental import pallas as pl
from jax.experimental.pallas import tpu as pltpu

F32 = jnp.float32
BF16 = jnp.bfloat16

RMS_EPS = 1e-6
CHUNK = 64
LEFT_CHUNKS = 8
BAND = (LEFT_CHUNKS + 2) * CHUNK
KPAD = BAND - CHUNK
REL_CLIP = 128
N_REL = 2 * REL_CLIP + 1
N_REL_PAD = 384
ATTN_HEADS = 8
ATTN_HEAD_DIM = 64
ATTN_WIDTH = ATTN_HEADS * ATTN_HEAD_DIM
ATTN_LOCKSTEP = 4
ATTN_UNROLL = 8
HGRN_HEADS = 4
HGRN_HEAD_DIM = 128
HGRN_ROWS = 512
SUB = 16
N_SUB = CHUNK // SUB
DIAG_STAGE = 4
N_DEV = 8

ADAM_LR = 0.001
ADAM_B1 = 0.9
ADAM_B2 = 0.999
ADAM_EPS = 1e-08
ADAM_WD = 0.01
ADAM_STEP = 10

VMEM_LIMIT = 56 * 1024 * 1024
NT = (((1,), (1,)), ((), ()))
NN = (((1,), (0,)), ((), ()))


def _params(*sem):
    return pltpu.CompilerParams(dimension_semantics=sem, vmem_limit_bytes=VMEM_LIMIT)


def _sigmoid(v):
    return 0.5 * jnp.tanh(0.5 * v) + 0.5


def _dot(a, b, dims=NN):
    return lax.dot_general(a.astype(BF16), b.astype(BF16), dims, preferred_element_type=F32)


def _dot_exact01(m01, v):
    m = m01.astype(BF16)
    hi = v.astype(BF16)
    r1 = v - hi.astype(F32)
    mid = r1.astype(BF16)
    lo = (r1 - mid.astype(F32)).astype(BF16)
    out = lax.dot_general(m, hi, NN, preferred_element_type=F32)
    out = out + lax.dot_general(m, mid, NN, preferred_element_type=F32)
    return out + lax.dot_general(m, lo, NN, preferred_element_type=F32)


def _dot_exact01_r(v, m01):
    m = m01.astype(BF16)
    hi = v.astype(BF16)
    r1 = v - hi.astype(F32)
    mid = r1.astype(BF16)
    lo = (r1 - mid.astype(F32)).astype(BF16)
    out = lax.dot_general(hi, m, NN, preferred_element_type=F32)
    out = out + lax.dot_general(mid, m, NN, preferred_element_type=F32)
    return out + lax.dot_general(lo, m, NN, preferred_element_type=F32)


def _lockstep(stages):
    live = list(stages)
    while live:
        still = []
        for g in live:
            try:
                next(g)
                still.append(g)
            except StopIteration:
                pass
        live = still


def _row_sums_on_lanes(v):
    ones = jnp.ones((8, v.shape[1]), BF16)
    hi = v.astype(BF16)
    r1 = v - hi.astype(F32)
    mid = r1.astype(BF16)
    lo = (r1 - mid.astype(F32)).astype(BF16)
    out = lax.dot_general(ones, hi, NT, preferred_element_type=F32)
    out = out + lax.dot_general(ones, mid, NT, preferred_element_type=F32)
    return (out + lax.dot_general(ones, lo, NT, preferred_element_type=F32))[0:1, :]


def _tn(a, b):
    ap = jnp.concatenate([a, jnp.zeros_like(a)], axis=0)
    bp = jnp.concatenate([b, jnp.zeros_like(b)], axis=0)
    return _dot(ap.T, bp)


def _row_tile(t):
    for tm in (512, 256, 128, 64, 32, 16, 8):
        if t % tm == 0:
            return tm
    raise ValueError(t)


class _Side:
    def __init__(self, ins, out_shape, sems, start, finish):
        self.ins, self.out_shape, self.sems, self.start, self.finish = ins, out_shape, sems, start, finish


_ACTIVE = [None]


def _pallas(body, *, name, grid, in_specs, out_specs, out_shape, scratch_shapes=(), sem, args):
    sched = _ACTIVE[0]
    side = sched.side_for(name) if sched is not None else None
    if side is None:
        return pl.pallas_call(
            body, name=name, grid=grid, in_specs=list(in_specs), out_specs=list(out_specs),
            out_shape=list(out_shape), scratch_shapes=list(scratch_shapes), compiler_params=_params(*sem))(*args)
    cuts = [len(in_specs), len(side.ins), len(out_shape), len(side.out_shape), len(scratch_shapes)]

    def with_side(*refs):
        groups, at = [], 0
        for n in cuts:
            groups.append(refs[at:at + n])
            at += n
        ins, side_ins, outs, side_outs, scratch = groups
        side_sems = refs[at:]
        first = pl.program_id(0) == 0
        last = pl.program_id(0) == grid[0] - 1
        for a in range(1, len(grid)):
            first = jnp.logical_and(first, pl.program_id(a) == 0)
            last = jnp.logical_and(last, pl.program_id(a) == grid[a] - 1)

        @pl.when(first)
        def _():
            side.start(side_ins, side_outs, side_sems)

        body(*ins, *outs, *scratch)

        @pl.when(last)
        def _():
            side.finish(side_ins, side_outs, side_sems)

    hbm = pl.BlockSpec(memory_space=pl.ANY)
    res = pl.pallas_call(
        with_side, name=name, grid=grid, in_specs=list(in_specs) + [hbm] * len(side.ins),
        out_specs=list(out_specs) + [hbm] * len(side.out_shape), out_shape=list(out_shape) + list(side.out_shape),
        scratch_shapes=list(scratch_shapes) + list(side.sems),
        compiler_params=_params(*(["arbitrary"] * len(grid))))(*args, *side.ins)
    sched.done(name, res[len(out_shape):])
    return res[:len(out_shape)]


def _rms_fwd(x, g, name):
    t, d = x.shape
    tm = _row_tile(t)

    def body(x_ref, g_ref, h_ref):
        xv = x_ref[...]
        r = lax.rsqrt(jnp.mean(xv * xv, axis=-1, keepdims=True) + RMS_EPS)
        h_ref[...] = (xv * r * g_ref[...]).astype(BF16)

    return _pallas(
        body, name=name, grid=(t // tm,),
        in_specs=[pl.BlockSpec((tm, d), lambda i: (i, 0)), pl.BlockSpec((1, d), lambda i: (0, 0))],
        out_specs=[pl.BlockSpec((tm, d), lambda i: (i, 0))], out_shape=[jax.ShapeDtypeStruct((t, d), BF16)],
        sem=("parallel",), args=(x, g))[0]


def _accumulate(ref, part, step):
    @pl.when(step == 0)
    def _():
        ref[...] = part

    @pl.when(step > 0)
    def _():
        ref[...] += part


def _mm(a, b, *, ta=False, tb=False, tm, tn, out_dtype=F32, add=None, scale=1.0, norm_g=None, norm_bwd=None, name):
    m, k = (a.shape[1], a.shape[0]) if ta else a.shape
    n = b.shape[0] if tb else b.shape[1]
    tm, tn = min(tm, m), min(tn, n)
    assert m % tm == 0 and n % tn == 0, (m, n, tm, tn)
    assert (norm_g is None and norm_bwd is None) or tn == n
    dims = (((0 if ta else 1,), (1 if tb else 0,)), ((), ()))
    n_in = 2 + (add is not None) + (norm_g is not None) + (3 if norm_bwd is not None else 0)

    def body(*refs):
        ins, outs = list(refs[2:n_in]), refs[n_in:]
        r = lax.dot_general(refs[0][...].astype(BF16), refs[1][...].astype(BF16), dims, preferred_element_type=F32)
        if scale != 1.0:
            r = r * scale
        if add is not None:
            r = r + ins.pop(0)[...]
        if norm_bwd is not None:
            xv, gv, dres = (ref[...] for ref in ins)
            rs = lax.rsqrt(jnp.mean(xv * xv, axis=-1, keepdims=True) + RMS_EPS)
            xhat = xv * rs
            gd = r * gv
            dx = dres + rs * (gd - xhat * jnp.mean(gd * xhat, axis=-1, keepdims=True))
            outs[0][...] = dx
            outs[1][...] = dx.astype(BF16)
            _accumulate(outs[2], jnp.sum(r * xhat, axis=0, keepdims=True), pl.program_id(0))
            return
        outs[0][...] = r.astype(out_dtype)
        if norm_g is not None:
            rs = lax.rsqrt(jnp.mean(r * r, axis=-1, keepdims=True) + RMS_EPS)
            outs[1][...] = (r * rs * ins.pop(0)[...]).astype(BF16)

    a_spec = pl.BlockSpec((k, tm), lambda i, j: (0, i)) if ta else pl.BlockSpec((tm, k), lambda i, j: (i, 0))
    b_spec = pl.BlockSpec((tn, k), lambda i, j: (j, 0)) if tb else pl.BlockSpec((k, tn), lambda i, j: (0, j))
    o_spec = pl.BlockSpec((tm, tn), lambda i, j: (i, j))
    vec = pl.BlockSpec((1, tn), lambda i, j: (0, j))
    args, specs = [a, b], [a_spec, b_spec]
    if add is not None:
        args.append(add)
        specs.append(o_spec)
    out_specs, out_shape = [o_spec], [jax.ShapeDtypeStruct((m, n), out_dtype)]
    if norm_g is not None:
        args.append(norm_g)
        specs.append(vec)
        out_specs.append(o_spec)
        out_shape.append(jax.ShapeDtypeStruct((m, n), BF16))
    if norm_bwd is not None:
        args += list(norm_bwd)
        specs += [o_spec, vec, o_spec]
        out_specs = [o_spec, o_spec, vec]
        out_shape = [jax.ShapeDtypeStruct((m, n), F32), jax.ShapeDtypeStruct((m, n), BF16),
                     jax.ShapeDtypeStruct((1, n), F32)]
    res = _pallas(body, name=name, grid=(m // tm, n // tn), in_specs=specs, out_specs=out_specs, out_shape=out_shape,
                  sem=("arbitrary", "arbitrary") if norm_bwd is not None else ("parallel", "parallel"), args=args)
    return res[0] if len(res) == 1 else res


def _ffn_tile(f):
    for tf in (1408, 512, 256, 128):
        if f % tf == 0:
            return tf
    raise ValueError(f)


def _ffn_fwd(h, x, wg, wu, wd, name, next_g=None, tgt=None):
    t, d = x.shape
    f = wg.shape[0]
    tm, tf = _row_tile(t), _ffn_tile(f)
    nf = f // tf
    assert (next_g is None) != (tgt is None)

    def body(h_ref, x_ref, wg_ref, wu_ref, wd_ref, tail_ref, g_ref, u_ref, o0_ref, o1_ref, *rest):
        acc_ref = rest[-1]
        j = pl.program_id(1)
        hv = h_ref[...]
        gv = lax.dot_general(hv, wg_ref[...], NT, preferred_element_type=F32)
        uv = lax.dot_general(hv, wu_ref[...], NT, preferred_element_type=F32)
        av = gv * _sigmoid(gv) * uv
        g_ref[...] = gv.astype(BF16)
        u_ref[...] = uv.astype(BF16)
        _accumulate(acc_ref, lax.dot_general(av.astype(BF16), wd_ref[...], NN, preferred_element_type=F32), j)

        @pl.when(j == nf - 1)
        def _():
            y = x_ref[...] + 0.5 * acc_ref[...]
            if tgt is None:
                o0_ref[...] = y
                rs = lax.rsqrt(jnp.mean(y * y, axis=-1, keepdims=True) + RMS_EPS)
                o1_ref[...] = (y * rs * tail_ref[...]).astype(BF16)
            else:
                e = y - tail_ref[...]
                dy = e * (1.0 / d)
                o0_ref[...] = dy
                o1_ref[...] = dy.astype(BF16)
                _accumulate(rest[0], jnp.sum(e * e, axis=0, keepdims=True), pl.program_id(0))

    row = pl.BlockSpec((tm, d), lambda i, j: (i, 0))
    hid = pl.BlockSpec((tm, tf), lambda i, j: (i, j))
    vec = pl.BlockSpec((1, d), lambda i, j: (0, 0))
    out_specs = [hid, hid, row, row] + ([vec] if tgt is not None else [])
    out_shape = [jax.ShapeDtypeStruct((t, f), BF16)] * 2 + [jax.ShapeDtypeStruct((t, d), F32),
                                                            jax.ShapeDtypeStruct((t, d), BF16)]
    if tgt is not None:
        out_shape.append(jax.ShapeDtypeStruct((1, d), F32))
    return _pallas(
        body, name=name, grid=(t // tm, nf),
        in_specs=[row, row] + [pl.BlockSpec((tf, d), lambda i, j: (j, 0))] * 3 + [vec if tgt is None else row],
        out_specs=out_specs, out_shape=out_shape, scratch_shapes=[pltpu.VMEM((tm, d), F32)],
        sem=("parallel" if tgt is None else "arbitrary", "arbitrary"),
        args=(h, x, wg, wu, wd, next_g if tgt is None else tgt))


def _ffn_bwd_mid(dy, wd, g, u, name):
    t, d = dy.shape
    f = wd.shape[0]
    tm, tf = _row_tile(t), _ffn_tile(f)

    def body(dy_ref, wd_ref, g_ref, u_ref, dg_ref, du_ref, a_ref):
        da = (0.5 * lax.dot_general(dy_ref[...].astype(BF16), wd_ref[...], NT, preferred_element_type=F32)).astype(BF16)
        gv = g_ref[...]
        uv = u_ref[...]
        s = _sigmoid(gv)
        silu = gv * s
        dg_ref[...] = da * uv * (s * (1.0 + gv * (1.0 - s)))
        du_ref[...] = da * silu
        a_ref[...] = silu * uv

    hid = pl.BlockSpec((tm, tf), lambda i, j: (i, j))
    return _pallas(
        body, name=name, grid=(t // tm, f // tf),
        in_specs=[pl.BlockSpec((tm, d), lambda i, j: (i, 0)), pl.BlockSpec((tf, d), lambda i, j: (j, 0)), hid, hid],
        out_specs=[hid, hid, hid], out_shape=[jax.ShapeDtypeStruct((t, f), BF16)] * 3,
        sem=("parallel", "parallel"), args=(dy, wd, g, u))


def _rel_index(t, s_band):
    return jnp.clip(t + KPAD - s_band, -REL_CLIP, REL_CLIP) + REL_CLIP


def _bias_expand(rel_bias_pad):
    nh = rel_bias_pad.shape[0]

    def body(rb_ref, out_ref):
        rb = rb_ref[...]
        i_io = lax.broadcasted_iota(jnp.int32, (N_REL_PAD, BAND), 0)
        s_io = lax.broadcasted_iota(jnp.int32, (N_REL_PAD, BAND), 1)

        def row(t, carry):
            onehot = (i_io == _rel_index(t, s_io)).astype(F32)
            out_ref[t] = _dot_exact01_r(rb, onehot)
            return carry

        lax.fori_loop(0, CHUNK, row, 0)

    return _pallas(
        body, name="bias_expand", grid=(1,), in_specs=[pl.BlockSpec(rel_bias_pad.shape, lambda i: (0, 0))],
        out_specs=[pl.BlockSpec((CHUNK, nh, BAND), lambda i: (0, 0, 0))],
        out_shape=[jax.ShapeDtypeStruct((CHUNK, nh, BAND), F32)], sem=("arbitrary",), args=(rel_bias_pad,))[0]


def _bias_fold(dbias):
    ng, nh = dbias.shape[0], dbias.shape[2]

    def body(db_ref, out_ref):
        s_io = lax.broadcasted_iota(jnp.int32, (BAND, N_REL_PAD), 0)
        i_io = lax.broadcasted_iota(jnp.int32, (BAND, N_REL_PAD), 1)

        def row(t, acc):
            onehot = (i_io == _rel_index(t, s_io)).astype(F32)
            d = db_ref[0, t]
            for gi in range(1, ng):
                d = d + db_ref[gi, t]
            return acc + _dot_exact01_r(d, onehot)

        out_ref[...] = lax.fori_loop(0, CHUNK, row, jnp.zeros((nh, N_REL_PAD), F32))

    return _pallas(
        body, name="bias_fold", grid=(1,), in_specs=[pl.BlockSpec(dbias.shape, lambda i: (0, 0, 0, 0))],
        out_specs=[pl.BlockSpec((nh, N_REL_PAD), lambda i: (0, 0))],
        out_shape=[jax.ShapeDtypeStruct((nh, N_REL_PAD), F32)], sem=("arbitrary",), args=(dbias,))[0]


def _left_half(shape):
    return lax.broadcasted_iota(jnp.int32, shape, len(shape) - 1) < ATTN_HEAD_DIM


def _stack_heads(v):
    left = _left_half(v.shape)
    zero = jnp.zeros_like(v)
    return jnp.concatenate([jnp.where(left, v, zero), jnp.where(left, zero, v)], axis=0)


def _unstack_heads(v):
    return jnp.where(_left_half((CHUNK, 128)), v[0:CHUNK, :], v[CHUNK:2 * CHUNK, :])


def _half_mean(v):
    r = lax.broadcasted_iota(jnp.int32, (128, 128), 0) < ATTN_HEAD_DIM
    c = lax.broadcasted_iota(jnp.int32, (128, 128), 1) < ATTN_HEAD_DIM
    return _dot_exact01_r(v, r == c) * (1.0 / ATTN_HEAD_DIM)


def _attn_prepare(q_ref, k_ref, v_ref, gq_ref, gk_ref, qs_scr, k_scr, v_scr):
    q, k = q_ref[...], k_ref[...]
    rq = lax.rsqrt(_half_mean(q * q) + RMS_EPS)
    rk = lax.rsqrt(_half_mean(k * k) + RMS_EPS)
    qhat, khat = q * rq, k * rk
    qs_scr[...] = (qhat * gq_ref[...] * ATTN_HEAD_DIM ** -0.5).astype(BF16)
    k_scr[0:KPAD, :] = jnp.zeros((KPAD, 128), BF16)
    v_scr[0:KPAD, :] = jnp.zeros((KPAD, 128), BF16)
    k_scr[KPAD:, :] = (khat * gk_ref[...]).astype(BF16)
    v_scr[KPAD:, :] = v_ref[...].astype(BF16)
    return qhat, rq, khat, rk


def _first_key(c):
    return jnp.maximum(CHUNK, (LEFT_CHUNKS + 1 - c) * CHUNK)


def _attn_fwd_chunk(c, qs_scr, k_scr, v_scr, bias_ref, o_ref):
    r0 = pl.multiple_of(c * CHUNK, CHUNK)
    s = lax.dot_general(_stack_heads(qs_scr[pl.ds(r0, CHUNK), :]), k_scr[pl.ds(r0, BAND), :], NT,
                        preferred_element_type=F32)
    yield
    col = lax.broadcasted_iota(jnp.int32, (2 * CHUNK, BAND), 1)
    s = jnp.where(col >= _first_key(c), s + bias_ref[...], -jnp.inf)
    m = jnp.max(s, axis=-1, keepdims=True)
    yield
    e = jnp.exp(s - m)
    yield
    inv = 1.0 / jnp.sum(e, axis=-1, keepdims=True)
    o = lax.dot_general(e.astype(BF16), v_scr[pl.ds(r0, BAND), :], NN, preferred_element_type=F32)
    yield
    o_ref[pl.ds(r0, CHUNK), :] = _unstack_heads(o * inv)


def _attn_bwd(proj, out, dout, bias, gq, gk, nb, seq):
    nc = seq // CHUNK
    lock = min(ATTN_LOCKSTEP, nc)
    assert nc % lock == 0
    scale = ATTN_HEAD_DIM ** -0.5

    def body(q_ref, k_ref, v_ref, o_ref, do_ref, bias_ref, gq_ref, gk_ref,
             dq_ref, dk_ref, dv_ref, dbias_ref, dgq_ref, dgk_ref,
             qs_scr, k_scr, v_scr, dqn_scr, dk_scr, dv_scr, db_scr):
        qhat, rq, khat, rk = _attn_prepare(q_ref, k_ref, v_ref, gq_ref, gk_ref, qs_scr, k_scr, v_scr)
        dk_scr[...] = jnp.zeros_like(dk_scr)
        dv_scr[...] = jnp.zeros_like(dv_scr)
        db_scr[...] = jnp.zeros_like(db_scr)

        def one_chunk(c):
            r0 = pl.multiple_of(c * CHUNK, CHUNK)
            qst = _stack_heads(qs_scr[pl.ds(r0, CHUNK), :])
            kb = k_scr[pl.ds(r0, BAND), :]
            vb = v_scr[pl.ds(r0, BAND), :]
            st = lax.dot_general(kb, qst, NT, preferred_element_type=F32) + bias_ref[...]
            dost = _stack_heads(do_ref[pl.ds(r0, CHUNK), :])
            dost16 = dost.astype(BF16)
            dpt = lax.dot_general(vb, dost16, NT, preferred_element_type=F32)
            yield
            key = lax.broadcasted_iota(jnp.int32, (BAND, 2 * CHUNK), 0)
            st = jnp.where(key >= _first_key(c), st, -jnp.inf)
            mx = jnp.max(st, axis=0, keepdims=True)
            drow = _row_sums_on_lanes(dost * _stack_heads(o_ref[pl.ds(r0, CHUNK), :]))
            yield
            et = jnp.exp(st - mx)
            yield
            pt = et * (1.0 / jnp.sum(et, axis=0, keepdims=True))
            yield
            dst = pt * (dpt - drow)
            dst16 = dst.astype(BF16)
            yield
            db_scr[...] += dst
            dqn_scr[pl.ds(r0, CHUNK), :] = scale * _unstack_heads(_dot(dst.T, kb))
            yield
            dk_scr[pl.ds(r0, BAND), :] += lax.dot_general(dst16, qst, NN, preferred_element_type=F32)
            yield
            dv_scr[pl.ds(r0, BAND), :] += lax.dot_general(pt.astype(BF16), dost16, NN, preferred_element_type=F32)

        def chunk(i, carry):
            _lockstep([one_chunk(i * lock + a) for a in range(lock)])
            return carry

        lax.fori_loop(0, nc // lock, chunk, 0, unroll=max(1, min(ATTN_UNROLL, nc) // lock))

        def norm_bwd(dn, hat, r, g_ref):
            gd = dn * g_ref[...]
            return r * (gd - hat * _half_mean(gd * hat)), jnp.sum(dn * hat, axis=0, keepdims=True)

        dq, dgq = norm_bwd(dqn_scr[...], qhat, rq, gq_ref)
        dk, dgk = norm_bwd(dk_scr[KPAD:, :], khat, rk, gk_ref)
        dq_ref[...] = dq.astype(BF16)
        dk_ref[...] = dk.astype(BF16)
        dv_ref[...] = dv_scr[KPAD:, :].astype(BF16)
        dbias_ref[0] = db_scr[...]
        dgq_ref[0] = dgq
        dgk_ref[0] = dgk

    def col(off):
        return pl.BlockSpec((seq, 128), lambda b, hp: (b, off + hp))

    vec = pl.BlockSpec((1, 128), lambda b, hp: (0, 0))
    gvec = pl.BlockSpec((1, 1, 128), lambda b, hp: (b * (ATTN_HEADS // 2) + hp, 0, 0))
    t = nb * seq
    return _pallas(
        body, name="attn_bwd", grid=(nb, ATTN_HEADS // 2),
        in_specs=[col(0), col(4), col(8), col(0), col(0),
                  pl.BlockSpec((BAND, 2 * CHUNK), lambda b, hp: (hp, 0)), vec, vec],
        out_specs=[col(0), col(0), col(0), pl.BlockSpec((1, BAND, 2 * CHUNK), lambda b, hp: (b, hp, 0)),
                   gvec, gvec],
        out_shape=[jax.ShapeDtypeStruct((t, ATTN_WIDTH), BF16)] * 3
        + [jax.ShapeDtypeStruct((nb, ATTN_HEADS // 2 * BAND, 2 * CHUNK), F32)]
        + [jax.ShapeDtypeStruct((nb * ATTN_HEADS // 2, 1, 128), F32)] * 2,
        scratch_shapes=[pltpu.VMEM((seq, 128), BF16), pltpu.VMEM((seq + KPAD, 128), BF16),
                        pltpu.VMEM((seq + KPAD, 128), BF16), pltpu.VMEM((seq, 128), F32),
                        pltpu.VMEM((seq + KPAD, 128), F32), pltpu.VMEM((seq + KPAD, 128), F32),
                        pltpu.VMEM((BAND, 2 * CHUNK), F32)],
        sem=("parallel", "parallel"), args=(proj, proj, proj, out, dout, bias, gq, gk))


def _tri(lower):
    r = lax.broadcasted_iota(jnp.int32, (CHUNK, CHUNK), 0)
    c = lax.broadcasted_iota(jnp.int32, (CHUNK, CHUNK), 1)
    return (r >= c) if lower else (r <= c)


def _hgrn_gates(hq, hf, lb):
    sq = _sigmoid(hq)
    sf = _sigmoid(hf)
    return hq * sq, sq, sf, lb + (1.0 - lb) * sf


def _hgrn_offdiag(q_s, k_s, b_s):
    row = lax.broadcasted_iota(jnp.int32, (CHUNK, HGRN_HEAD_DIM), 0)
    bv, qv, kv = b_s[...], q_s[...], k_s[...]
    eqs, eks = [], []
    for i in range(1, N_SUB):
        r = b_s[pl.ds(SUB * i - 1, 1), :]
        in_i = (row >= SUB * i) & (row < SUB * (i + 1))
        eqs.append(jnp.exp(jnp.where(in_i, bv - r, -jnp.inf)))
        eks.append(jnp.exp(jnp.where(row < SUB * i, r - bv, -jnp.inf)))
    eq = jnp.concatenate(eqs, axis=1)
    ek = jnp.concatenate(eks, axis=1)
    qt = jnp.concatenate([qv] * (N_SUB - 1), axis=1) * eq
    kt = jnp.concatenate([kv] * (N_SUB - 1), axis=1) * ek
    return qt, kt, eq, ek


def _hgrn_diag_e(b_s, i, s):
    t_io = lax.broadcasted_iota(jnp.int32, (SUB, HGRN_HEAD_DIM), 0)
    bi = b_s[pl.ds(SUB * i, SUB), :]
    return jnp.exp(jnp.where(t_io >= s, bi - b_s[pl.ds(SUB * i + s, 1), :], -jnp.inf)), t_io


def _hgrn_intra(q_s, k_s, b_s, a_s, qt, kt):
    ktp = jnp.concatenate([kt, jnp.zeros_like(kt)], axis=0)
    a_s[...] = _dot(qt, ktp, NT)
    yield
    col = lax.broadcasted_iota(jnp.int32, (SUB, HGRN_HEAD_DIM), 1)
    for i in range(N_SUB):
        qi = q_s[pl.ds(SUB * i, SUB), :]
        ai = jnp.zeros((SUB, HGRN_HEAD_DIM), F32)
        for s in range(SUB):
            e, _ = _hgrn_diag_e(b_s, i, s)
            a_col = jnp.sum(qi * k_s[pl.ds(SUB * i + s, 1), :] * e, axis=-1, keepdims=True)
            ai = ai + jnp.where(col == SUB * i + s, a_col, 0.0)
            if s % DIAG_STAGE == DIAG_STAGE - 1:
                yield
        a_s[pl.ds(SUB * i, SUB), :] += ai


def _mixer_fwd(proj, bias, gq, gk, lb, go, nb, seq):
    nc = seq // CHUNK
    hd = HGRN_HEAD_DIM
    nblk = ATTN_HEADS // 2
    rows_blk = seq // nblk
    nck = rows_blk // CHUNK
    per = nc // nck
    assert rows_blk % CHUNK == 0

    def body(aq_ref, ak_ref, av_ref, bias_ref, gq_ref, gk_ref, hq_ref, hf_ref, hi_ref, hg_ref, lb_ref, go_ref,
             ao_ref, y_ref, o_ref, st_ref, a_ref, qs_scr, k_scr, v_scr, st_all, q_all, k_all, b_all, a_all):
        _attn_prepare(aq_ref, ak_ref, av_ref, gq_ref, gk_ref, qs_scr, k_scr, v_scr)

        @pl.when(pl.program_id(1) == 0)
        def _():
            st_all[...] = jnp.zeros_like(st_all)

        lower = _tri(True)

        def head_chunk(hh, c, rows):
            ln = slice(hd * hh, hd * (hh + 1))
            st, q_s, k_s, b_s, a_s = st_all.at[hh], q_all.at[hh], k_all.at[hh], b_all.at[hh], a_all.at[hh]
            q, _, _, f = _hgrn_gates(hq_ref[rows, ln], hf_ref[rows, ln], lb_ref[:, ln])
            v = hi_ref[rows, ln]
            yield
            b = _dot_exact01(lower, jnp.log(f))
            q_s[...] = q
            k_s[...] = 1.0 - f
            b_s[...] = b
            st_ref[hh, c] = st[...]
            yield
            qt, kt, _, _ = _hgrn_offdiag(q_s, k_s, b_s)
            yield
            yield from _hgrn_intra(q_s, k_s, b_s, a_s, qt, kt)
            a16 = a_s[...].astype(BF16)
            a_ref[hh, c] = a16
            vp = jnp.concatenate([v, jnp.zeros_like(v)], axis=0)
            o = _dot(a16, vp) + _dot(q * jnp.exp(b), st[...], NT)
            yield
            bl = b_s[pl.ds(CHUNK - 1, 1), :]
            st[...] = st[...] * jnp.exp(bl) + _tn(v, (1.0 - f) * jnp.exp(bl - b))
            o_ref[rows, ln] = o
            yield
            n = o * lax.rsqrt(jnp.mean(o * o, axis=-1, keepdims=True) + RMS_EPS) * go_ref[...]
            hg = hg_ref[rows, ln]
            y_ref[rows, ln] = n * hg * _sigmoid(hg)

        def chunk(c, carry):
            rows = pl.ds(pl.multiple_of(c * CHUNK, CHUNK), CHUNK)
            _lockstep([_attn_fwd_chunk(c * per + a, qs_scr, k_scr, v_scr, bias_ref, ao_ref) for a in range(per)]
                      + [head_chunk(hh, c, rows) for hh in range(HGRN_HEADS)])
            return carry

        lax.fori_loop(0, nck, chunk, 0)

    hp, wide = HGRN_HEADS, HGRN_HEADS * hd

    def acol(off):
        return pl.BlockSpec((seq, 128), lambda b, s: (b, off + s))

    def col(off):
        return pl.BlockSpec((rows_blk, wide), lambda b, s: (b * nblk + s, off // hp))

    out = pl.BlockSpec((rows_blk, wide), lambda b, s: (b * nblk + s, 0))
    vec = pl.BlockSpec((1, 128), lambda b, s: (0, 0))
    t = nb * seq
    return _pallas(
        body, name="mixer_fwd", grid=(nb, nblk),
        in_specs=[acol(0), acol(4), acol(8), pl.BlockSpec((2 * CHUNK, BAND), lambda b, s: (s, 0)), vec, vec,
                  col(12), col(16), col(20), col(24), pl.BlockSpec((1, wide), lambda b, s: (0, 0)), vec],
        out_specs=[pl.BlockSpec((seq, 128), lambda b, s: (b, s)), out, out,
                   pl.BlockSpec((hp, nck, hd, hd), lambda b, s: (b, s, 0, 0)),
                   pl.BlockSpec((hp, nck, CHUNK, hd), lambda b, s: (b, s, 0, 0))],
        out_shape=[jax.ShapeDtypeStruct((t, ATTN_WIDTH), F32)] + [jax.ShapeDtypeStruct((t, wide), F32)] * 2
        + [jax.ShapeDtypeStruct((nb * hp, nc, hd, hd), F32), jax.ShapeDtypeStruct((nb * hp, nc, CHUNK, hd), BF16)],
        scratch_shapes=[pltpu.VMEM((seq, 128), BF16), pltpu.VMEM((seq + KPAD, 128), BF16),
                        pltpu.VMEM((seq + KPAD, 128), BF16), pltpu.VMEM((hp, hd, hd), F32)]
        + [pltpu.VMEM((hp, CHUNK, hd), F32)] * 4,
        sem=("parallel", "arbitrary"), args=(proj,) * 3 + (bias, gq, gk) + (proj,) * 4 + (lb, go))


def _hgrn_bwd(proj, lb, go, o_pre, states, scores, dout, nb, seq):
    nc = seq // CHUNK
    hd = HGRN_HEAD_DIM
    rows_blk = min(HGRN_ROWS, seq)
    nblk, nck = seq // rows_blk, rows_blk // CHUNK

    def body(hq_ref, hf_ref, hi_ref, hg_ref, lb_ref, go_ref, o_ref, st_ref, a_ref, dy_ref,
             dhq_ref, dhf_ref, dhi_ref, dhg_ref, dlb_ref, dgo_ref,
             dst_all, q_all, k_all, b_all, da_all, dqi_all, dki_all, dlb_all, dgo_all):
        @pl.when(pl.program_id(1) == 0)
        def _():
            dst_all[...] = jnp.zeros_like(dst_all)
            dlb_all[...] = jnp.zeros_like(dlb_all)
            dgo_all[...] = jnp.zeros_like(dgo_all)

        lower, upper = _tri(True), _tri(False)
        gov = go_ref[...]
        row = lax.broadcasted_iota(jnp.int32, (CHUNK, hd), 0)

        def head_chunk(hh, c, rows):
            ln = slice(hd * hh, hd * (hh + 1))
            dst, q_s, k_s, b_s = dst_all.at[hh], q_all.at[hh], k_all.at[hh], b_all.at[hh]
            da_s, dqi_s, dki_s = da_all.at[hh], dqi_all.at[hh], dki_all.at[hh]
            dlb_acc, dgo_acc = dlb_all.at[hh], dgo_all.at[hh]
            lbv = lb_ref[:, ln]
            hq, hf, v, hg = hq_ref[rows, ln], hf_ref[rows, ln], hi_ref[rows, ln], hg_ref[rows, ln]
            q, sq, sf, f = _hgrn_gates(hq, hf, lbv)
            kk = 1.0 - f
            yield
            b = _dot_exact01(lower, jnp.log(f))
            q_s[...] = q
            k_s[...] = kk
            b_s[...] = b
            yield
            bl = b_s[pl.ds(CHUNK - 1, 1), :]
            ebl = jnp.exp(bl)
            ekd = jnp.exp(bl - b)
            kd = kk * ekd
            eb = jnp.exp(b)
            qb = q * eb
            st0 = st_ref[hh, c]
            dst1 = dst[...]
            yield

            o = o_ref[rows, ln]
            dy = dy_ref[rows, ln]
            sg = _sigmoid(hg)
            rstd = lax.rsqrt(jnp.mean(o * o, axis=-1, keepdims=True) + RMS_EPS)
            ohat = o * rstd
            dn = dy * hg * sg
            dhg_ref[rows, ln] = (dy * ohat * gov * (sg * (1.0 + hg * (1.0 - sg)))).astype(BF16)
            dgo_acc[...] += jnp.sum(dn * ohat, axis=0, keepdims=True)
            gdn = dn * gov
            do = rstd * (gdn - ohat * jnp.mean(gdn * ohat, axis=-1, keepdims=True))
            yield

            qt, kt, eq, ek = _hgrn_offdiag(q_s, k_s, b_s)
            da = _dot(do, v, NT)
            dat = _dot(v, do, NT)
            da_s[...] = da
            yield
            dqo = _dot(da, kt) * eq
            dko = _dot(dat, qt) * ek
            dqi_s[...] = sum(dqo[:, j * hd:(j + 1) * hd] for j in range(N_SUB - 1))
            dki_s[...] = sum(dko[:, j * hd:(j + 1) * hd] for j in range(N_SUB - 1))
            yield
            col = lax.broadcasted_iota(jnp.int32, (SUB, CHUNK), 1)
            for i in range(N_SUB):
                qi = q_s[pl.ds(SUB * i, SUB), :]
                dai = da_s[pl.ds(SUB * i, SUB), :]
                dqd = jnp.zeros((SUB, hd), F32)
                dkd_ = jnp.zeros((SUB, hd), F32)
                for s in range(SUB):
                    e, t_io = _hgrn_diag_e(b_s, i, s)
                    dacol = jnp.sum(jnp.where(col == SUB * i + s, dai, 0.0), axis=-1, keepdims=True)
                    w = dacol * e
                    dqd = dqd + w * k_s[pl.ds(SUB * i + s, 1), :]
                    dkd_ = dkd_ + jnp.where(t_io == s, jnp.sum(w * qi, axis=0, keepdims=True), 0.0)
                    if s % DIAG_STAGE == DIAG_STAGE - 1:
                        yield
                dqi_s[pl.ds(SUB * i, SUB), :] += dqd
                dki_s[pl.ds(SUB * i, SUB), :] += dkd_
            dqi, dki = dqi_s[...], dki_s[...]

            dv = _tn(a_ref[hh, c].astype(F32), do)[0:CHUNK, :] + _dot(kd, dst1, NT)
            dqb = _dot(do, st0)
            dkd = _dot(v, dst1)
            yield
            t2 = dkd * kd
            dq = dqb * eb + dqi
            dk = dkd * ekd + dki
            dbl = jnp.sum(t2, axis=0, keepdims=True) + ebl * jnp.sum(st0 * dst1, axis=0, keepdims=True)
            db = dqb * qb - t2 + q * dqi - kk * dki + jnp.where(row == CHUNK - 1, dbl, 0.0)
            yield
            dg = _dot_exact01(upper, db)
            dst[...] = dst1 * ebl + _tn(do, qb)
            yield

            df = dg / f - dk
            dhf_ref[rows, ln] = (df * (1.0 - lbv) * sf * (1.0 - sf)).astype(BF16)
            dlb_acc[...] += jnp.sum(df * (1.0 - sf), axis=0, keepdims=True)
            dhq_ref[rows, ln] = (dq * (sq * (1.0 + hq * (1.0 - sq)))).astype(BF16)
            dhi_ref[rows, ln] = dv.astype(BF16)

        def chunk(it, carry):
            c = nck - 1 - it
            rows = pl.ds(pl.multiple_of(c * CHUNK, CHUNK), CHUNK)
            _lockstep([head_chunk(hh, c, rows) for hh in range(HGRN_HEADS)])
            return carry

        lax.fori_loop(0, nck, chunk, 0)

        @pl.when(pl.program_id(1) == nblk - 1)
        def _():
            dlb_ref[...] = dlb_all[...]
            dgo_ref[...] = dgo_all[...]

    hp, wide = HGRN_HEADS, HGRN_HEADS * hd

    def col(off):
        return pl.BlockSpec((rows_blk, wide), lambda b, s: (b * nblk + nblk - 1 - s, off // hp))

    out = pl.BlockSpec((rows_blk, wide), lambda b, s: (b * nblk + nblk - 1 - s, 0))
    part = pl.BlockSpec((hp, 1, hd), lambda b, s: (b, 0, 0))
    t = nb * seq
    return pl.pallas_call(
        body, name="hgrn_bwd", grid=(nb, nblk),
        in_specs=[col(12), col(16), col(20), col(24), pl.BlockSpec((1, wide), lambda b, s: (0, 0)),
                  pl.BlockSpec((1, hd), lambda b, s: (0, 0)), out,
                  pl.BlockSpec((hp, nck, hd, hd), lambda b, s: (b, nblk - 1 - s, 0, 0)),
                  pl.BlockSpec((hp, nck, CHUNK, hd), lambda b, s: (b, nblk - 1 - s, 0, 0)), col(4)],
        out_specs=[out, out, out, out, part, part],
        out_shape=[jax.ShapeDtypeStruct((t, wide), BF16)] * 4 + [jax.ShapeDtypeStruct((nb * hp, 1, hd), F32)] * 2,
        scratch_shapes=[pltpu.VMEM((hp, hd, hd), F32)] + [pltpu.VMEM((hp, CHUNK, hd), F32)] * 3
        + [pltpu.VMEM((hp, CHUNK, CHUNK), F32)] + [pltpu.VMEM((hp, CHUNK, hd), F32)] * 2
        + [pltpu.VMEM((hp, 1, hd), F32)] * 2,
        compiler_params=_params("parallel", "arbitrary"),
    )(proj, proj, proj, proj, lb, go, o_pre, states, scores, dout)


def _lb_fwd(lower_bounds):
    def body(x_ref, o_ref):
        xv = x_ref[...]
        e = jnp.exp(xv - jnp.max(xv, axis=0, keepdims=True))
        o_ref[...] = e[0:1, :] / jnp.sum(e, axis=0, keepdims=True)

    return pl.pallas_call(body, name="lb_fwd",
                          out_shape=jax.ShapeDtypeStruct((1, lower_bounds.shape[1]), F32))(lower_bounds)


def _lb_bwd(lower_bounds, dlb_parts):
    ng = dlb_parts.shape[0]

    def body(x_ref, d_ref, o_ref):
        xv = x_ref[...]
        e = jnp.exp(xv - jnp.max(xv, axis=0, keepdims=True))
        p = e / jnp.sum(e, axis=0, keepdims=True)
        dlb = d_ref[0]
        for gi in range(1, ng):
            dlb = dlb + d_ref[gi]
        first = lax.broadcasted_iota(jnp.int32, xv.shape, 0) == 0
        o_ref[...] = p * (jnp.where(first, dlb, 0.0) - p[0:1, :] * dlb)

    return pl.pallas_call(body, name="lb_bwd",
                          out_shape=jax.ShapeDtypeStruct(lower_bounds.shape, F32))(lower_bounds, dlb_parts)


def _ffn_bwd(x, g, h, gate, up, dy, dy16, w, put, tag):
    wg, wu, wd = w[tag + "_w_gate"], w[tag + "_w_up"], w[tag + "_w_down"]
    dgate, dup, act = _ffn_bwd_mid(dy16, wd, gate, up, tag + "_bwd_mid")
    put(tag + "_w_down", _mm(act, dy16, ta=True, tm=1408, tn=512, scale=0.5, name=tag + "_dwd"))
    put(tag + "_w_gate", _mm(dgate, h, ta=True, tm=1408, tn=512, name=tag + "_dwg"))
    put(tag + "_w_up", _mm(dup, h, ta=True, tm=1408, tn=512, name=tag + "_dwu"))
    dh = _mm(dgate, wg, tm=512, tn=1024, name=tag + "_dh_gate")
    return _mm(dup, wu, tm=512, tn=1024, add=dh, norm_bwd=(x, g, dy), name=tag + "_dh_up")


def _local_step(x, tgt, sp, w, put, nb, seq):
    d = x.shape[1]
    h1 = _rms_fwd(x, sp["ffn1_norm_g"], "ffn1_norm")
    rb_pad = jnp.pad(sp["attn_rel_bias"], ((0, 0), (0, N_REL_PAD - N_REL)))
    bias = jnp.transpose(_bias_expand(rb_pad), (1, 0, 2)).reshape(ATTN_HEADS * CHUNK, BAND)
    gq2 = jnp.concatenate([sp["attn_q_norm_g"]] * 2, axis=1)
    gk2 = jnp.concatenate([sp["attn_k_norm_g"]] * 2, axis=1)
    lb = _lb_fwd(sp["hgrn_lower_bounds"])
    gate1, up1, x1, h2 = _ffn_fwd(h1, x, w["ffn1_w_gate"], w["ffn1_w_up"], w["ffn1_w_down"], "ffn1_fwd",
                                  next_g=sp["mix_norm_g"])
    proj = _mm(h2, w["w_in"], tb=True, tm=256, tn=w["w_in"].shape[0], name="in_proj")
    attn, hy, ho, hstate, hscore = _mixer_fwd(proj, bias, gq2, gk2, lb, sp["hgrn_out_norm_g"], nb, seq)
    mix = jnp.concatenate([attn, hy], axis=1)
    x2, h3 = _mm(mix, w["w_out"], tm=512, tn=1024, add=x1, norm_g=sp["ffn2_norm_g"], name="out_proj")
    gate2, up2, dx3, dx3_16, sq = _ffn_fwd(h3, x2, w["ffn2_w_gate"], w["ffn2_w_up"], w["ffn2_w_down"], "ffn2_fwd",
                                           tgt=tgt)
    loss = 0.5 * jnp.sum(sq) / d

    dx2, dx2_16, dg3 = _ffn_bwd(x2, sp["ffn2_norm_g"], h3, gate2, up2, dx3, dx3_16, w, put, "ffn2")
    dmix = _mm(dx2_16, w["w_out"], tb=True, tm=512, tn=1024, name="out_proj_dx")
    put("w_out", _mm(mix, dx2_16, ta=True, tm=512, tn=1024, name="out_proj_dw"))
    bias_t = jnp.transpose(bias.reshape(ATTN_HEADS // 2, 2 * CHUNK, BAND), (0, 2, 1)).reshape(-1, 2 * CHUNK)
    dq, dk, dv, dbias, dgq, dgk = _attn_bwd(proj, attn, dmix, bias_t, gq2, gk2, nb, seq)
    dbias = jnp.transpose(dbias.reshape(nb, ATTN_HEADS // 2, BAND, 2, CHUNK), (0, 4, 1, 3, 2))
    dbias = dbias.reshape(nb, CHUNK, ATTN_HEADS, BAND)
    dgq = jnp.sum(dgq, axis=(0, 1)).reshape(2, ATTN_HEAD_DIM).sum(axis=0, keepdims=True)
    dgk = jnp.sum(dgk, axis=(0, 1)).reshape(2, ATTN_HEAD_DIM).sum(axis=0, keepdims=True)
    dhq, dhf, dhi, dhg, dlb, dgo = _hgrn_bwd(proj, lb, sp["hgrn_out_norm_g"], ho, hstate, hscore, dmix, nb, seq)
    dproj = jnp.concatenate([dq, dk, dv, dhq, dhf, dhi, dhg], axis=1)
    put("w_in", _mm(dproj, h2, ta=True, tm=512, tn=1024, name="in_proj_dw"))
    dx1, dx1_16, dgm = _mm(dproj, w["w_in"], tm=512, tn=1024, norm_bwd=(x1, sp["mix_norm_g"], dx2),
                           name="in_proj_dx")
    dx0, _, dg1 = _ffn_bwd(x, sp["ffn1_norm_g"], h1, gate1, up1, dx1, dx1_16, w, put, "ffn1")

    small = {
        "ffn1_norm_g": dg1, "mix_norm_g": dgm, "ffn2_norm_g": dg3,
        "attn_q_norm_g": dgq, "attn_k_norm_g": dgk,
        "attn_rel_bias": _bias_fold(dbias)[:, :N_REL],
        "hgrn_lower_bounds": _lb_bwd(sp["hgrn_lower_bounds"], dlb.reshape(nb, 1, HGRN_HEADS * HGRN_HEAD_DIM)),
        "hgrn_out_norm_g": jnp.sum(dgo, axis=(0, 1))[None, :],
    }
    return loss, dx0, small


MESH = pl.DeviceIdType.MESH
ANY = pl.BlockSpec(memory_space=pl.ANY)


def _coords():
    return lax.axis_index("x"), lax.axis_index("y"), lax.axis_index("c")


def _other_chips(x, y):
    return [(1 - x, y), (x, 1 - y), (1 - x, 1 - y)]


def _gather_side(shards):
    n = len(shards)

    def copies(ins, outs, sems):
        send_sems, recv_sems, local_sems = sems
        x, y, c = _coords()
        me, sibling = (x, y, c), (x, y, 1 - c)
        chips = _other_chips(x, y)

        def copy(i, k, block, to, src=None):
            bx, by, bc = block
            dst = outs[i].at[4 * bx + 2 * by + bc]
            return pltpu.make_async_remote_copy(
                src_ref=dst if src is None else src, dst_ref=dst, send_sem=send_sems.at[i, k],
                recv_sem=recv_sems.at[i, k], device_id=to, device_id_type=MESH)

        mine = [pltpu.make_async_copy(ins[i], outs[i].at[4 * x + 2 * y + c], local_sems.at[i]) for i in range(n)]
        own = []
        for i in range(n):
            own.append(copy(i, 0, me, sibling, src=ins[i]))
            own += [copy(i, 1 + j, me, (*chip, c), src=ins[i]) for j, chip in enumerate(chips)]
        return copy, mine, own, me, sibling, chips, c

    def start(ins, outs, sems):
        _, mine, own, *_ = copies(ins, outs, sems)
        for cp in mine + own:
            cp.start()

    def finish(ins, outs, sems):
        copy, mine, own, me, sibling, chips, c = copies(ins, outs, sems)
        passed = []
        for i in range(n):
            for j, chip in enumerate(chips):
                copy(i, 1 + j, (*chip, c), me).wait_recv()
                passed.append(copy(i, 4 + j, (*chip, c), sibling))
                passed[-1].start()
        for i in range(n):
            copy(i, 0, sibling, me).wait_recv()
            for j, chip in enumerate(chips):
                copy(i, 4 + j, (*chip, 1 - c), me).wait_recv()
        for cp in own + passed:
            cp.wait_send()
        for cp in mine:
            cp.wait()

    return _Side(list(shards), [jax.ShapeDtypeStruct((N_DEV,) + s.shape, s.dtype) for s in shards],
                 [pltpu.SemaphoreType.DMA((n, 7)), pltpu.SemaphoreType.DMA((n, 7)), pltpu.SemaphoreType.DMA((n,))],
                 start, finish)


def _pair_side(grads):
    n = len(grads)

    def copies(ins, outs, sems):
        send_sems, recv_sems = sems
        x, y, c = _coords()
        return [pltpu.make_async_remote_copy(
            src_ref=ins[i].at[2 * k + 1 - c], dst_ref=outs[i].at[k], send_sem=send_sems.at[i, k],
            recv_sem=recv_sems.at[i, k], device_id=(x, y, 1 - c), device_id_type=MESH)
            for i in range(n) for k in range(4)]

    def start(ins, outs, sems):
        for cp in copies(ins, outs, sems):
            cp.start()

    def finish(ins, outs, sems):
        for cp in copies(ins, outs, sems):
            cp.wait()

    return _Side(list(grads), [jax.ShapeDtypeStruct((4,) + g.shape[1:], g.dtype) for g in grads],
                 [pltpu.SemaphoreType.DMA((n, 4)), pltpu.SemaphoreType.DMA((n, 4))], start, finish)


def _pair_add(grad, recv, core, name):
    _, r, cdim = grad.shape

    def body(c_ref, g_ref, r_ref, o_ref):
        o_ref[...] = (g_ref[...] + r_ref[...]).astype(BF16)

    blk = (1, r, cdim)
    return pl.pallas_call(
        body, name=name,
        grid_spec=pltpu.PrefetchScalarGridSpec(
            num_scalar_prefetch=1, grid=(4,),
            in_specs=[pl.BlockSpec(blk, lambda k, c_ref: (2 * k + c_ref[0], 0, 0)),
                      pl.BlockSpec(blk, lambda k, c_ref: (k, 0, 0))],
            out_specs=pl.BlockSpec(blk, lambda k, c_ref: (k, 0, 0))),
        out_shape=jax.ShapeDtypeStruct((4, r, cdim), BF16),
        compiler_params=_params("arbitrary"),
    )(core, grad, recv)


def _chip_side(parts):
    n = len(parts)

    def copies(ins, outs, sems):
        send_sems, recv_sems, local_sems = sems
        x, y, c = _coords()
        chips = _other_chips(x, y)
        mine = [pltpu.make_async_copy(ins[i].at[2 * x + y], outs[i].at[2 * x + y], local_sems.at[i])
                for i in range(n)]
        sent = [pltpu.make_async_remote_copy(
            src_ref=ins[i].at[2 * px + py], dst_ref=outs[i].at[2 * x + y], send_sem=send_sems.at[i, j],
            recv_sem=recv_sems.at[i, j], device_id=(px, py, c), device_id_type=MESH)
            for i in range(n) for j, (px, py) in enumerate(chips)]
        return mine, sent, chips, c

    def start(ins, outs, sems):
        mine, sent, _, _ = copies(ins, outs, sems)
        for cp in mine + sent:
            cp.start()

    def finish(ins, outs, sems):
        mine, sent, chips, c = copies(ins, outs, sems)
        send_sems, recv_sems, _ = sems
        for i in range(n):
            for j, (px, py) in enumerate(chips):
                landed = outs[i].at[2 * px + py]
                pltpu.make_async_remote_copy(
                    src_ref=landed, dst_ref=landed, send_sem=send_sems.at[i, j], recv_sem=recv_sems.at[i, j],
                    device_id=(px, py, c), device_id_type=MESH).wait_recv()
        for cp in sent:
            cp.wait_send()
        for cp in mine:
            cp.wait()

    return _Side(list(parts), [jax.ShapeDtypeStruct(p.shape, p.dtype) for p in parts],
                 [pltpu.SemaphoreType.DMA((n, 3)), pltpu.SemaphoreType.DMA((n, 3)), pltpu.SemaphoreType.DMA((n,))],
                 start, finish)


def _all_reduce_small(v):
    r = v.shape[0]

    def body(v_ref, o_ref, buf, send_sems, recv_sems):
        x, y, c = _coords()
        me = 4 * x + 2 * y + c
        buf[me] = v_ref[...]
        cps = []
        for k in range(1, N_DEV):
            px = 1 - x if k & 4 else x
            py = 1 - y if k & 2 else y
            pc = 1 - c if k & 1 else c
            cps.append((pltpu.make_async_remote_copy(
                src_ref=v_ref, dst_ref=buf.at[me], send_sem=send_sems.at[k - 1], recv_sem=recv_sems.at[k - 1],
                device_id=(px, py, pc), device_id_type=MESH), 4 * px + 2 * py + pc))
        for cp, _ in cps:
            cp.start()
        for k, (cp, peer) in enumerate(cps):
            pltpu.make_async_remote_copy(
                src_ref=v_ref, dst_ref=buf.at[peer], send_sem=send_sems.at[k], recv_sem=recv_sems.at[k],
                device_id=(x, y, c), device_id_type=MESH).wait_recv()
        for cp, _ in cps:
            cp.wait_send()
        acc = buf[0]
        for j in range(1, N_DEV):
            acc = acc + buf[j]
        o_ref[...] = acc

    return pl.pallas_call(
        body, name="small_all_reduce", out_shape=jax.ShapeDtypeStruct(v.shape, F32),
        in_specs=[pl.BlockSpec(memory_space=pltpu.VMEM)], out_specs=pl.BlockSpec(memory_space=pltpu.VMEM),
        scratch_shapes=[pltpu.VMEM((N_DEV, r, 128), F32), pltpu.SemaphoreType.DMA((N_DEV - 1,)),
                        pltpu.SemaphoreType.DMA((N_DEV - 1,))],
    )(v)


def _adamw(w, m, v, g, name):
    parts = w.ndim == 3
    r, cdim = w.shape[-2:]
    tr = r // 4 if r % 32 == 0 else r

    def body(w_ref, m_ref, v_ref, g_ref, go_ref, d_ref, mo_ref, vo_ref):
        if parts:
            gv = g_ref[0].astype(F32)
            for k in range(1, 4):
                gv = gv + g_ref[k].astype(F32)
            gv = gv[None]
        else:
            gv = g_ref[...]
        m2 = ADAM_B1 * m_ref[...] + (1.0 - ADAM_B1) * gv
        v2 = ADAM_B2 * v_ref[...] + (1.0 - ADAM_B2) * (gv * gv)
        m_hat = m2 / (1.0 - ADAM_B1 ** ADAM_STEP)
        v_hat = v2 / (1.0 - ADAM_B2 ** ADAM_STEP)
        go_ref[...] = gv
        d_ref[...] = -ADAM_LR * (m_hat / (jnp.sqrt(v_hat) + ADAM_EPS) + ADAM_WD * w_ref[...])
        mo_ref[...] = m2
        vo_ref[...] = v2

    if parts:
        row = pl.BlockSpec((1, tr, cdim), lambda i: (0, i, 0))
        g_spec = pl.BlockSpec((4, tr, cdim), lambda i: (0, i, 0))
    else:
        row = g_spec = pl.BlockSpec((tr, cdim), lambda i: (i, 0))
    return pl.pallas_call(
        body, name=name, grid=(r // tr,), in_specs=[row, row, row, g_spec], out_specs=[row] * 4,
        out_shape=[jax.ShapeDtypeStruct(w.shape, F32)] * 4,
        compiler_params=_params("parallel"),
    )(w, m, v, g)


WEIGHTS = ["ffn1_norm_g", "ffn1_w_gate", "ffn1_w_up", "ffn1_w_down", "mix_norm_g", "w_in", "attn_q_norm_g",
           "attn_k_norm_g", "attn_rel_bias", "hgrn_lower_bounds", "hgrn_out_norm_g", "w_out", "ffn2_norm_g",
           "ffn2_w_gate", "ffn2_w_up", "ffn2_w_down"]
COL_SHARDED = ("ffn1_w_gate", "ffn1_w_up", "w_in", "ffn2_w_gate", "ffn2_w_up")
ROW_SHARDED = ("ffn1_w_down", "w_out", "ffn2_w_down")
BIG = [n for n in WEIGHTS if n in COL_SHARDED or n in ROW_SHARDED]
SMALL = [n for n in WEIGHTS if n not in BIG]
PACK_ROWS = 8
FFN2 = ["ffn2_w_down", "ffn2_w_gate", "ffn2_w_up"]
MIXER = ["w_out", "w_in"]

PLAN = {
    "ffn1_norm": [("gather", ["ffn1_w_down"])],
    "bias_expand": [("gather", ["ffn1_w_gate", "ffn1_w_up"])],
    "ffn1_fwd": [("gather", MIXER)],
    "mixer_fwd": [("gather", FFN2)],
    "ffn2_dh_gate": [("pair", FFN2)],
    "attn_bwd": [("chip", FFN2)],
    "in_proj_dx": [("pair", MIXER)],
    "ffn1_bwd_mid": [("chip", MIXER)],
    "ffn1_dwg": [("pair", ["ffn1_w_down"])],
    "ffn1_dwu": [("chip", ["ffn1_w_down"]), ("pair", ["ffn1_w_gate"])],
    "ffn1_dh_gate": [("chip", ["ffn1_w_gate"]), ("pair", ["ffn1_w_up"])],
    "bias_fold": [("chip", ["ffn1_w_up"])],
}


def _join_sides(sides):
    def split(refs, counts):
        out, at = [], 0
        for n in counts:
            out.append(refs[at:at + n])
            at += n
        return out

    n_in, n_out, n_sem = ([len(getattr(s, f)) for s in sides] for f in ("ins", "out_shape", "sems"))

    def run(which):
        def go(ins, outs, sems):
            for s, i, o, m in zip(sides, split(ins, n_in), split(outs, n_out), split(sems, n_sem)):
                getattr(s, which)(i, o, m)
        return go

    return _Side([a for s in sides for a in s.ins], [a for s in sides for a in s.out_shape],
                 [a for s in sides for a in s.sems], run("start"), run("finish"))


class _Schedule:
    def __init__(self, shards):
        self.shards = shards
        self.weights = {}
        self.sliced = {}
        self.partials = {}
        self.reduced = {}

    def put(self, name, grad):
        self.sliced[name] = grad.reshape((N_DEV,) + self.shards[name].shape)

    def side_for(self, call):
        if call not in PLAN:
            return None
        sides = []
        for kind, names in PLAN[call]:
            if kind == "gather":
                sides.append(_gather_side([self.shards[n] for n in names]))
            elif kind == "pair":
                sides.append(_pair_side([self.sliced[n] for n in names]))
            else:
                sides.append(_chip_side([self.partials[n] for n in names]))
        return _join_sides(sides)

    def done(self, call, outs):
        at = 0
        for kind, names in PLAN[call]:
            self.file(kind, names, outs[at:at + len(names)])
            at += len(names)

    def file(self, kind, names, outs):
        for n, o in zip(names, outs):
            if kind == "gather":
                self.weights[n] = o.reshape(N_DEV * o.shape[1], o.shape[2])
            elif kind == "pair":
                core = lax.axis_index("c").astype(jnp.int32).reshape(1)
                self.partials[n] = _pair_add(self.sliced[n], o, core, n + "_pair_add")
            else:
                self.reduced[n] = o


def _pack_small(vals, loss=None):
    parts = []
    for n in SMALL:
        a = vals[n]
        if n == "attn_rel_bias":
            a = jnp.pad(a.reshape(ATTN_HEADS, N_REL), ((0, 0), (0, N_REL_PAD - N_REL)))
        flat = a.reshape(-1)
        size = -(-flat.shape[0] // (PACK_ROWS * 128)) * PACK_ROWS * 128
        parts.append(jnp.pad(flat, (0, size - flat.shape[0])).reshape(-1, 128))
    tail = jnp.zeros((PACK_ROWS, 128), F32)
    if loss is not None:
        tail = tail.at[0, 0].set(loss)
    return jnp.concatenate(parts + [tail], axis=0)


def _unpack_small(packed, shapes):
    out, row = {}, 0
    for n in SMALL:
        shape = shapes[n]
        if n == "attn_rel_bias":
            rows = ATTN_HEADS * N_REL_PAD // 128
            out[n] = packed[row:row + rows].reshape(ATTN_HEADS, N_REL_PAD)[:, :N_REL].reshape(shape)
        else:
            size = 1
            for s in shape:
                size *= s
            rows = -(-size // (PACK_ROWS * 128)) * PACK_ROWS
            out[n] = packed[row:row + rows].reshape(-1)[:size].reshape(shape)
        row += rows
    return out, packed[row, 0]


def kernel(x, ffn1_norm_g, ffn1_w_gate, ffn1_w_up, ffn1_w_down, mix_norm_g, w_in, attn_q_norm_g, attn_k_norm_g, attn_rel_bias, hgrn_lower_bounds, hgrn_out_norm_g, w_out, ffn2_norm_g, ffn2_w_gate, ffn2_w_up, ffn2_w_down, loss_target, m_ffn1_norm_g, m_ffn1_w_gate, m_ffn1_w_up, m_ffn1_w_down, m_mix_norm_g, m_w_in, m_attn_q_norm_g, m_attn_k_norm_g, m_attn_rel_bias, m_hgrn_lower_bounds, m_hgrn_out_norm_g, m_w_out, m_ffn2_norm_g, m_ffn2_w_gate, m_ffn2_w_up, m_ffn2_w_down, v_ffn1_norm_g, v_ffn1_w_gate, v_ffn1_w_up, v_ffn1_w_down, v_mix_norm_g, v_w_in, v_attn_q_norm_g, v_attn_k_norm_g, v_attn_rel_bias, v_hgrn_lower_bounds, v_hgrn_out_norm_g, v_w_out, v_ffn2_norm_g, v_ffn2_w_gate, v_ffn2_w_up, v_ffn2_w_down):
    wts = dict(zip(WEIGHTS, (ffn1_norm_g, ffn1_w_gate, ffn1_w_up, ffn1_w_down, mix_norm_g, w_in, attn_q_norm_g,
                             attn_k_norm_g, attn_rel_bias, hgrn_lower_bounds, hgrn_out_norm_g, w_out, ffn2_norm_g,
                             ffn2_w_gate, ffn2_w_up, ffn2_w_down)))
    mom = dict(zip(WEIGHTS, (m_ffn1_norm_g, m_ffn1_w_gate, m_ffn1_w_up, m_ffn1_w_down, m_mix_norm_g, m_w_in,
                             m_attn_q_norm_g, m_attn_k_norm_g, m_attn_rel_bias, m_hgrn_lower_bounds,
                             m_hgrn_out_norm_g, m_w_out, m_ffn2_norm_g, m_ffn2_w_gate, m_ffn2_w_up, m_ffn2_w_down)))
    var = dict(zip(WEIGHTS, (v_ffn1_norm_g, v_ffn1_w_gate, v_ffn1_w_up, v_ffn1_w_down, v_mix_norm_g, v_w_in,
                             v_attn_q_norm_g, v_attn_k_norm_g, v_attn_rel_bias, v_hgrn_lower_bounds,
                             v_hgrn_out_norm_g, v_w_out, v_ffn2_norm_g, v_ffn2_w_gate, v_ffn2_w_up, v_ffn2_w_down)))
    nb, seq, d = x.shape
    shapes = {n: wts[n].shape for n in WEIGHTS}

    def rows_first(a, n):
        return jnp.swapaxes(a, 1, 2) if n in COL_SHARDED else a

    sched = _Schedule({n: rows_first(wts[n], n)[0].astype(BF16) for n in BIG})
    sp = {n: wts[n] for n in SMALL}
    sp["attn_rel_bias"] = wts["attn_rel_bias"][0]
    _ACTIVE[0] = sched
    try:
        loss, dx, dsmall = _local_step(x.reshape(nb * seq, d), loss_target.reshape(nb * seq, d), sp,
                                       sched.weights, sched.put, nb, seq)
    finally:
        _ACTIVE[0] = None
    reduced = sched.reduced

    small_sum = _all_reduce_small(_pack_small(dsmall, loss))
    gsmall, loss_total = _unpack_small(small_sum, shapes)

    grads, deltas, new_m, new_v = {}, {}, {}, {}
    for n in BIG:
        out = _adamw(rows_first(wts[n], n), rows_first(mom[n], n), rows_first(var[n], n), reduced[n], n + "_adamw")
        grads[n], deltas[n], new_m[n], new_v[n] = (rows_first(o, n) for o in out)
    packed = _adamw(_pack_small(wts), _pack_small(mom), _pack_small(var), small_sum, "small_adamw")
    for dst, p in zip((deltas, new_m, new_v), packed[1:]):
        dst.update(_unpack_small(p, shapes)[0])
    grads.update(gsmall)

    return (loss_total, dx.reshape(nb, seq, d), *[grads[n] for n in WEIGHTS], *[deltas[n] for n in WEIGHTS],
            *[new_m[n] for n in WEIGHTS], *[new_v[n] for n in WEIGHTS])
```

```python
import functools

import jax
import jax.numpy as jnp
from jax import lax
from jax.experimental import pallas as pl
from jax.experimental.pallas import tpu as pltpu

F32 = jnp.float32
BF16 = jnp.bfloat16

RMS_EPS = 1e-6
CHUNK = 64
LEFT_CHUNKS = 8
BAND = (LEFT_CHUNKS + 2) * CHUNK
KPAD = BAND - CHUNK
REL_CLIP = 128
N_REL = 2 * REL_CLIP + 1
N_REL_PAD = 384
ATTN_HEADS = 8
ATTN_HEAD_DIM = 64
ATTN_WIDTH = ATTN_HEADS * ATTN_HEAD_DIM
ATTN_LOCKSTEP = 4
ATTN_UNROLL = 8
HGRN_HEADS = 4
HGRN_HEAD_DIM = 128
HGRN_ROWS = 512
SUB = 16
N_SUB = CHUNK // SUB
DIAG_STAGE = 4
N_DEV = 8

ADAM_LR = 0.001
ADAM_B1 = 0.9
ADAM_B2 = 0.999
ADAM_EPS = 1e-08
ADAM_WD = 0.01
ADAM_STEP = 10

VMEM_LIMIT = 56 * 1024 * 1024
NT = (((1,), (1,)), ((), ()))
NN = (((1,), (0,)), ((), ()))


def _params(*sem):
    return pltpu.CompilerParams(dimension_semantics=sem, vmem_limit_bytes=VMEM_LIMIT)


def _sigmoid(v):
    return 0.5 * jnp.tanh(0.5 * v) + 0.5


def _dot(a, b, dims=NN):
    return lax.dot_general(a.astype(BF16), b.astype(BF16), dims, preferred_element_type=F32)


def _dot_exact01(m01, v):
    m = m01.astype(BF16)
    hi = v.astype(BF16)
    r1 = v - hi.astype(F32)
    mid = r1.astype(BF16)
    lo = (r1 - mid.astype(F32)).astype(BF16)
    out = lax.dot_general(m, hi, NN, preferred_element_type=F32)
    out = out + lax.dot_general(m, mid, NN, preferred_element_type=F32)
    return out + lax.dot_general(m, lo, NN, preferred_element_type=F32)


def _dot_exact01_r(v, m01):
    m = m01.astype(BF16)
    hi = v.astype(BF16)
    r1 = v - hi.astype(F32)
    mid = r1.astype(BF16)
    lo = (r1 - mid.astype(F32)).astype(BF16)
    out = lax.dot_general(hi, m, NN, preferred_element_type=F32)
    out = out + lax.dot_general(mid, m, NN, preferred_element_type=F32)
    return out + lax.dot_general(lo, m, NN, preferred_element_type=F32)


def _lockstep(stages):
    live = list(stages)
    while live:
        still = []
        for g in live:
            try:
                next(g)
                still.append(g)
            except StopIteration:
                pass
        live = still


def _row_sums_on_lanes(v):
    ones = jnp.ones((8, v.shape[1]), BF16)
    hi = v.astype(BF16)
    r1 = v - hi.astype(F32)
    mid = r1.astype(BF16)
    lo = (r1 - mid.astype(F32)).astype(BF16)
    out = lax.dot_general(ones, hi, NT, preferred_element_type=F32)
    out = out + lax.dot_general(ones, mid, NT, preferred_element_type=F32)
    return (out + lax.dot_general(ones, lo, NT, preferred_element_type=F32))[0:1, :]


def _tn(a, b):
    ap = jnp.concatenate([a, jnp.zeros_like(a)], axis=0)
    bp = jnp.concatenate([b, jnp.zeros_like(b)], axis=0)
    return _dot(ap.T, bp)


def _row_tile(t):
    for tm in (512, 256, 128, 64, 32, 16, 8):
        if t % tm == 0:
            return tm
    raise ValueError(t)


class _Side:
    def __init__(self, ins, out_shape, sems, start, finish):
        self.ins, self.out_shape, self.sems, self.start, self.finish = ins, out_shape, sems, start, finish


_ACTIVE = [None]


def _pallas(body, *, name, grid, in_specs, out_specs, out_shape, scratch_shapes=(), sem, args):
    sched = _ACTIVE[0]
    side = sched.side_for(name) if sched is not None else None
    if side is None:
        return pl.pallas_call(
            body, name=name, grid=grid, in_specs=list(in_specs), out_specs=list(out_specs),
            out_shape=list(out_shape), scratch_shapes=list(scratch_shapes), compiler_params=_params(*sem))(*args)
    cuts = [len(in_specs), len(side.ins), len(out_shape), len(side.out_shape), len(scratch_shapes)]

    def with_side(*refs):
        groups, at = [], 0
        for n in cuts:
            groups.append(refs[at:at + n])
            at += n
        ins, side_ins, outs, side_outs, scratch = groups
        side_sems = refs[at:]
        first = pl.program_id(0) == 0
        last = pl.program_id(0) == grid[0] - 1
        for a in range(1, len(grid)):
            first = jnp.logical_and(first, pl.program_id(a) == 0)
            last = jnp.logical_and(last, pl.program_id(a) == grid[a] - 1)

        @pl.when(first)
        def _():
            side.start(side_ins, side_outs, side_sems)

        body(*ins, *outs, *scratch)

        @pl.when(last)
        def _():
            side.finish(side_ins, side_outs, side_sems)

    hbm = pl.BlockSpec(memory_space=pl.ANY)
    res = pl.pallas_call(
        with_side, name=name, grid=grid, in_specs=list(in_specs) + [hbm] * len(side.ins),
        out_specs=list(out_specs) + [hbm] * len(side.out_shape), out_shape=list(out_shape) + list(side.out_shape),
        scratch_shapes=list(scratch_shapes) + list(side.sems),
        compiler_params=_params(*(["arbitrary"] * len(grid))))(*args, *side.ins)
    sched.done(name, res[len(out_shape):])
    return res[:len(out_shape)]


def _rms_fwd(x, g, name):
    t, d = x.shape
    tm = _row_tile(t)

    def body(x_ref, g_ref, h_ref):
        xv = x_ref[...]
        r = lax.rsqrt(jnp.mean(xv * xv, axis=-1, keepdims=True) + RMS_EPS)
        h_ref[...] = (xv * r * g_ref[...]).astype(BF16)

    return _pallas(
        body, name=name, grid=(t // tm,),
        in_specs=[pl.BlockSpec((tm, d), lambda i: (i, 0)), pl.BlockSpec((1, d), lambda i: (0, 0))],
        out_specs=[pl.BlockSpec((tm, d), lambda i: (i, 0))], out_shape=[jax.ShapeDtypeStruct((t, d), BF16)],
        sem=("parallel",), args=(x, g))[0]


def _accumulate(ref, part, step):
    @pl.when(step == 0)
    def _():
        ref[...] = part

    @pl.when(step > 0)
    def _():
        ref[...] += part


def _mm(a, b, *, ta=False, tb=False, tm, tn, out_dtype=F32, add=None, scale=1.0, norm_g=None, norm_bwd=None, name):
    m, k = (a.shape[1], a.shape[0]) if ta else a.shape
    n = b.shape[0] if tb else b.shape[1]
    tm, tn = min(tm, m), min(tn, n)
    assert m % tm == 0 and n % tn == 0, (m, n, tm, tn)
    assert (norm_g is None and norm_bwd is None) or tn == n
    dims = (((0 if ta else 1,), (1 if tb else 0,)), ((), ()))
    n_in = 2 + (add is not None) + (norm_g is not None) + (3 if norm_bwd is not None else 0)

    def body(*refs):
        ins, outs = list(refs[2:n_in]), refs[n_in:]
        r = lax.dot_general(refs[0][...].astype(BF16), refs[1][...].astype(BF16), dims, preferred_element_type=F32)
        if scale != 1.0:
            r = r * scale
        if add is not None:
            r = r + ins.pop(0)[...]
        if norm_bwd is not None:
            xv, gv, dres = (ref[...] for ref in ins)
            rs = lax.rsqrt(jnp.mean(xv * xv, axis=-1, keepdims=True) + RMS_EPS)
            xhat = xv * rs
            gd = r * gv
            dx = dres + rs * (gd - xhat * jnp.mean(gd * xhat, axis=-1, keepdims=True))
            outs[0][...] = dx
            outs[1][...] = dx.astype(BF16)
            _accumulate(outs[2], jnp.sum(r * xhat, axis=0, keepdims=True), pl.program_id(0))
            return
        outs[0][...] = r.astype(out_dtype)
        if norm_g is not None:
            rs = lax.rsqrt(jnp.mean(r * r, axis=-1, keepdims=True) + RMS_EPS)
            outs[1][...] = (r * rs * ins.pop(0)[...]).astype(BF16)

    a_spec = pl.BlockSpec((k, tm), lambda i, j: (0, i)) if ta else pl.BlockSpec((tm, k), lambda i, j: (i, 0))
    b_spec = pl.BlockSpec((tn, k), lambda i, j: (j, 0)) if tb else pl.BlockSpec((k, tn), lambda i, j: (0, j))
    o_spec = pl.BlockSpec((tm, tn), lambda i, j: (i, j))
    vec = pl.BlockSpec((1, tn), lambda i, j: (0, j))
    args, specs = [a, b], [a_spec, b_spec]
    if add is not None:
        args.append(add)
        specs.append(o_spec)
    out_specs, out_shape = [o_spec], [jax.ShapeDtypeStruct((m, n), out_dtype)]
    if norm_g is not None:
        args.append(norm_g)
        specs.append(vec)
        out_specs.append(o_spec)
        out_shape.append(jax.ShapeDtypeStruct((m, n), BF16))
    if norm_bwd is not None:
        args += list(norm_bwd)
        specs += [o_spec, vec, o_spec]
        out_specs = [o_spec, o_spec, vec]
        out_shape = [jax.ShapeDtypeStruct((m, n), F32), jax.ShapeDtypeStruct((m, n), BF16),
                     jax.ShapeDtypeStruct((1, n), F32)]
    res = _pallas(body, name=name, grid=(m // tm, n // tn), in_specs=specs, out_specs=out_specs, out_shape=out_shape,
                  sem=("arbitrary", "arbitrary") if norm_bwd is not None else ("parallel", "parallel"), args=args)
    return res[0] if len(res) == 1 else res


def _ffn_tile(f):
    for tf in (1408, 512, 256, 128):
        if f % tf == 0:
            return tf
    raise ValueError(f)


def _ffn_fwd(h, x, wg, wu, wd, name, next_g=None, tgt=None):
    t, d = x.shape
    f = wg.shape[0]
    tm, tf = _row_tile(t), _ffn_tile(f)
    nf = f // tf
    assert (next_g is None) != (tgt is None)

    def body(h_ref, x_ref, wg_ref, wu_ref, wd_ref, tail_ref, g_ref, u_ref, o0_ref, o1_ref, *rest):
        acc_ref = rest[-1]
        j = pl.program_id(1)
        hv = h_ref[...]
        gv = lax.dot_general(hv, wg_ref[...], NT, preferred_element_type=F32)
        uv = lax.dot_general(hv, wu_ref[...], NT, preferred_element_type=F32)
        av = gv * _sigmoid(gv) * uv
        g_ref[...] = gv.astype(BF16)
        u_ref[...] = uv.astype(BF16)
        _accumulate(acc_ref, lax.dot_general(av.astype(BF16), wd_ref[...], NN, preferred_element_type=F32), j)

        @pl.when(j == nf - 1)
        def _():
            y = x_ref[...] + 0.5 * acc_ref[...]
            if tgt is None:
                o0_ref[...] = y
                rs = lax.rsqrt(jnp.mean(y * y, axis=-1, keepdims=True) + RMS_EPS)
                o1_ref[...] = (y * rs * tail_ref[...]).astype(BF16)
            else:
                e = y - tail_ref[...]
                dy = e * (1.0 / d)
                o0_ref[...] = dy
                o1_ref[...] = dy.astype(BF16)
                _accumulate(rest[0], jnp.sum(e * e, axis=0, keepdims=True), pl.program_id(0))

    row = pl.BlockSpec((tm, d), lambda i, j: (i, 0))
    hid = pl.BlockSpec((tm, tf), lambda i, j: (i, j))
    vec = pl.BlockSpec((1, d), lambda i, j: (0, 0))
    out_specs = [hid, hid, row, row] + ([vec] if tgt is not None else [])
    out_shape = [jax.ShapeDtypeStruct((t, f), BF16)] * 2 + [jax.ShapeDtypeStruct((t, d), F32),
                                                            jax.ShapeDtypeStruct((t, d), BF16)]
    if tgt is not None:
        out_shape.append(jax.ShapeDtypeStruct((1, d), F32))
    return _pallas(
        body, name=name, grid=(t // tm, nf),
        in_specs=[row, row] + [pl.BlockSpec((tf, d), lambda i, j: (j, 0))] * 3 + [vec if tgt is None else row],
        out_specs=out_specs, out_shape=out_shape, scratch_shapes=[pltpu.VMEM((tm, d), F32)],
        sem=("parallel" if tgt is None else "arbitrary", "arbitrary"),
        args=(h, x, wg, wu, wd, next_g if tgt is None else tgt))


def _ffn_bwd_mid(dy, wd, g, u, name):
    t, d = dy.shape
    f = wd.shape[0]
    tm, tf = _row_tile(t), _ffn_tile(f)

    def body(dy_ref, wd_ref, g_ref, u_ref, dg_ref, du_ref, dwd_ref):
        dy16 = dy_ref[...]
        da = 0.5 * lax.dot_general(dy16, wd_ref[...], NT, preferred_element_type=F32)
        gv = g_ref[...].astype(F32)
        uv = u_ref[...].astype(F32)
        s = _sigmoid(gv)
        silu = gv * s
        dg_ref[...] = (da * uv * (s * (1.0 + gv * (1.0 - s)))).astype(BF16)
        du_ref[...] = (da * silu).astype(BF16)
        part = 0.5 * lax.dot_general((silu * uv).astype(BF16), dy16, (((0,), (0,)), ((), ())),
                                     preferred_element_type=F32)
        _accumulate(dwd_ref, part, pl.program_id(1))

    hid = pl.BlockSpec((tm, tf), lambda j, i: (i, j))
    wrow = pl.BlockSpec((tf, d), lambda j, i: (j, 0))
    return _pallas(
        body, name=name, grid=(f // tf, t // tm),
        in_specs=[pl.BlockSpec((tm, d), lambda j, i: (i, 0)), wrow, hid, hid],
        out_specs=[hid, hid, wrow],
        out_shape=[jax.ShapeDtypeStruct((t, f), BF16)] * 2 + [jax.ShapeDtypeStruct((f, d), F32)],
        sem=("parallel", "arbitrary"), args=(dy, wd, g, u))


def _rel_index(t, s_band):
    return jnp.clip(t + KPAD - s_band, -REL_CLIP, REL_CLIP) + REL_CLIP


def _bias_expand(rel_bias_pad):
    nh = rel_bias_pad.shape[0]

    def body(rb_ref, out_ref):
        rb = rb_ref[...]
        i_io = lax.broadcasted_iota(jnp.int32, (N_REL_PAD, BAND), 0)
        s_io = lax.broadcasted_iota(jnp.int32, (N_REL_PAD, BAND), 1)

        def row(t, carry):
            onehot = (i_io == _rel_index(t, s_io)).astype(F32)
            out_ref[t] = _dot_exact01_r(rb, onehot)
            return carry

        lax.fori_loop(0, CHUNK, row, 0)

    return _pallas(
        body, name="bias_expand", grid=(1,), in_specs=[pl.BlockSpec(rel_bias_pad.shape, lambda i: (0, 0))],
        out_specs=[pl.BlockSpec((CHUNK, nh, BAND), lambda i: (0, 0, 0))],
        out_shape=[jax.ShapeDtypeStruct((CHUNK, nh, BAND), F32)], sem=("arbitrary",), args=(rel_bias_pad,))[0]


def _bias_fold(dbias):
    ng, nh = dbias.shape[0], dbias.shape[2]

    def body(db_ref, out_ref):
        s_io = lax.broadcasted_iota(jnp.int32, (BAND, N_REL_PAD), 0)
        i_io = lax.broadcasted_iota(jnp.int32, (BAND, N_REL_PAD), 1)

        def row(t, acc):
            onehot = (i_io == _rel_index(t, s_io)).astype(F32)
            d = db_ref[0, t]
            for gi in range(1, ng):
                d = d + db_ref[gi, t]
            return acc + _dot_exact01_r(d, onehot)

        out_ref[...] = lax.fori_loop(0, CHUNK, row, jnp.zeros((nh, N_REL_PAD), F32))

    return _pallas(
        body, name="bias_fold", grid=(1,), in_specs=[pl.BlockSpec(dbias.shape, lambda i: (0, 0, 0, 0))],
        out_specs=[pl.BlockSpec((nh, N_REL_PAD), lambda i: (0, 0))],
        out_shape=[jax.ShapeDtypeStruct((nh, N_REL_PAD), F32)], sem=("arbitrary",), args=(dbias,))[0]


def _left_half(shape):
    return lax.broadcasted_iota(jnp.int32, shape, len(shape) - 1) < ATTN_HEAD_DIM


def _stack_heads(v):
    left = _left_half(v.shape)
    zero = jnp.zeros_like(v)
    return jnp.concatenate([jnp.where(left, v, zero), jnp.where(left, zero, v)], axis=0)


def _unstack_heads(v):
    return jnp.where(_left_half((CHUNK, 128)), v[0:CHUNK, :], v[CHUNK:2 * CHUNK, :])


def _half_mean(v):
    r = lax.broadcasted_iota(jnp.int32, (128, 128), 0) < ATTN_HEAD_DIM
    c = lax.broadcasted_iota(jnp.int32, (128, 128), 1) < ATTN_HEAD_DIM
    return _dot_exact01_r(v, r == c) * (1.0 / ATTN_HEAD_DIM)


def _attn_prepare(q_ref, k_ref, v_ref, gq_ref, gk_ref, qs_scr, k_scr, v_scr):
    q, k = q_ref[...], k_ref[...]
    rq = lax.rsqrt(_half_mean(q * q) + RMS_EPS)
    rk = lax.rsqrt(_half_mean(k * k) + RMS_EPS)
    qhat, khat = q * rq, k * rk
    qs_scr[...] = (qhat * gq_ref[...] * ATTN_HEAD_DIM ** -0.5).astype(BF16)
    k_scr[0:KPAD, :] = jnp.zeros((KPAD, 128), BF16)
    v_scr[0:KPAD, :] = jnp.zeros((KPAD, 128), BF16)
    k_scr[KPAD:, :] = (khat * gk_ref[...]).astype(BF16)
    v_scr[KPAD:, :] = v_ref[...].astype(BF16)
    return qhat, rq, khat, rk


def _first_key(c):
    return jnp.maximum(CHUNK, (LEFT_CHUNKS + 1 - c) * CHUNK)


def _attn_fwd_chunk(c, qs_scr, k_scr, v_scr, bias_ref, o_ref):
    r0 = pl.multiple_of(c * CHUNK, CHUNK)
    s = lax.dot_general(_stack_heads(qs_scr[pl.ds(r0, CHUNK), :]), k_scr[pl.ds(r0, BAND), :], NT,
                        preferred_element_type=F32)
    yield
    col = lax.broadcasted_iota(jnp.int32, (2 * CHUNK, BAND), 1)
    s = jnp.where(col >= _first_key(c), s + bias_ref[...], -jnp.inf)
    m = jnp.max(s, axis=-1, keepdims=True)
    yield
    e = jnp.exp(s - m)
    yield
    inv = 1.0 / jnp.sum(e, axis=-1, keepdims=True)
    o = lax.dot_general(e.astype(BF16), v_scr[pl.ds(r0, BAND), :], NN, preferred_element_type=F32)
    yield
    o_ref[pl.ds(r0, CHUNK), :] = _unstack_heads(o * inv)


def _attn_bwd(proj, out, dout, bias, gq, gk, nb, seq):
    nc = seq // CHUNK
    lock = min(ATTN_LOCKSTEP, nc)
    assert nc % lock == 0
    scale = ATTN_HEAD_DIM ** -0.5

    def body(q_ref, k_ref, v_ref, o_ref, do_ref, bias_ref, gq_ref, gk_ref,
             dq_ref, dk_ref, dv_ref, dbias_ref, dgq_ref, dgk_ref,
             qs_scr, k_scr, v_scr, dqn_scr, dk_scr, dv_scr, db_scr):
        qhat, rq, khat, rk = _attn_prepare(q_ref, k_ref, v_ref, gq_ref, gk_ref, qs_scr, k_scr, v_scr)
        dk_scr[...] = jnp.zeros_like(dk_scr)
        dv_scr[...] = jnp.zeros_like(dv_scr)
        db_scr[...] = jnp.zeros_like(db_scr)

        def one_chunk(c):
            r0 = pl.multiple_of(c * CHUNK, CHUNK)
            qst = _stack_heads(qs_scr[pl.ds(r0, CHUNK), :])
            kb = k_scr[pl.ds(r0, BAND), :]
            vb = v_scr[pl.ds(r0, BAND), :]
            st = lax.dot_general(kb, qst, NT, preferred_element_type=F32) + bias_ref[...]
            dost = _stack_heads(do_ref[pl.ds(r0, CHUNK), :])
            dost16 = dost.astype(BF16)
            dpt = lax.dot_general(vb, dost16, NT, preferred_element_type=F32)
            yield
            key = lax.broadcasted_iota(jnp.int32, (BAND, 2 * CHUNK), 0)
            st = jnp.where(key >= _first_key(c), st, -jnp.inf)
            mx = jnp.max(st, axis=0, keepdims=True)
            drow = _row_sums_on_lanes(dost * _stack_heads(o_ref[pl.ds(r0, CHUNK), :]))
            yield
            et = jnp.exp(st - mx)
            yield
            pt = et * (1.0 / jnp.sum(et, axis=0, keepdims=True))
            yield
            dst = pt * (dpt - drow)
            dst16 = dst.astype(BF16)
            yield
            db_scr[...] += dst
            dqn_scr[pl.ds(r0, CHUNK), :] = scale * _unstack_heads(_dot(dst.T, kb))
            yield
            dk_scr[pl.ds(r0, BAND), :] += lax.dot_general(dst16, qst, NN, preferred_element_type=F32)
            yield
            dv_scr[pl.ds(r0, BAND), :] += lax.dot_general(pt.astype(BF16), dost16, NN, preferred_element_type=F32)

        def chunk(i, carry):
            _lockstep([one_chunk(i * lock + a) for a in range(lock)])
            return carry

        lax.fori_loop(0, nc // lock, chunk, 0, unroll=max(1, min(ATTN_UNROLL, nc) // lock))

        def norm_bwd(dn, hat, r, g_ref):
            gd = dn * g_ref[...]
            return r * (gd - hat * _half_mean(gd * hat)), jnp.sum(dn * hat, axis=0, keepdims=True)

        dq, dgq = norm_bwd(dqn_scr[...], qhat, rq, gq_ref)
        dk, dgk = norm_bwd(dk_scr[KPAD:, :], khat, rk, gk_ref)
        dq_ref[...] = dq.astype(BF16)
        dk_ref[...] = dk.astype(BF16)
        dv_ref[...] = dv_scr[KPAD:, :].astype(BF16)
        dbias_ref[0] = db_scr[...]
        dgq_ref[0] = dgq
        dgk_ref[0] = dgk

    def col(off):
        return pl.BlockSpec((seq, 128), lambda b, hp: (b, off + hp))

    vec = pl.BlockSpec((1, 128), lambda b, hp: (0, 0))
    gvec = pl.BlockSpec((1, 1, 128), lambda b, hp: (b * (ATTN_HEADS // 2) + hp, 0, 0))
    t = nb * seq
    return _pallas(
        body, name="attn_bwd", grid=(nb, ATTN_HEADS // 2),
        in_specs=[col(0), col(4), col(8), col(0), col(0),
                  pl.BlockSpec((BAND, 2 * CHUNK), lambda b, hp: (hp, 0)), vec, vec],
        out_specs=[col(0), col(0), col(0), pl.BlockSpec((1, BAND, 2 * CHUNK), lambda b, hp: (b, hp, 0)),
                   gvec, gvec],
        out_shape=[jax.ShapeDtypeStruct((t, ATTN_WIDTH), BF16)] * 3
        + [jax.ShapeDtypeStruct((nb, ATTN_HEADS // 2 * BAND, 2 * CHUNK), F32)]
        + [jax.ShapeDtypeStruct((nb * ATTN_HEADS // 2, 1, 128), F32)] * 2,
        scratch_shapes=[pltpu.VMEM((seq, 128), BF16), pltpu.VMEM((seq + KPAD, 128), BF16),
                        pltpu.VMEM((seq + KPAD, 128), BF16), pltpu.VMEM((seq, 128), F32),
                        pltpu.VMEM((seq + KPAD, 128), F32), pltpu.VMEM((seq + KPAD, 128), F32),
                        pltpu.VMEM((BAND, 2 * CHUNK), F32)],
        sem=("parallel", "parallel"), args=(proj, proj, proj, out, dout, bias, gq, gk))


def _tri(lower):
    r = lax.broadcasted_iota(jnp.int32, (CHUNK, CHUNK), 0)
    c = lax.broadcasted_iota(jnp.int32, (CHUNK, CHUNK), 1)
    return (r >= c) if lower else (r <= c)


def _hgrn_gates(hq, hf, lb):
    sq = _sigmoid(hq)
    sf = _sigmoid(hf)
    return hq * sq, sq, sf, lb + (1.0 - lb) * sf


def _hgrn_offdiag(q_s, k_s, b_s):
    row = lax.broadcasted_iota(jnp.int32, (CHUNK, HGRN_HEAD_DIM), 0)
    bv, qv, kv = b_s[...], q_s[...], k_s[...]
    eqs, eks = [], []
    for i in range(1, N_SUB):
        r = b_s[pl.ds(SUB * i - 1, 1), :]
        in_i = (row >= SUB * i) & (row < SUB * (i + 1))
        eqs.append(jnp.exp(jnp.where(in_i, bv - r, -jnp.inf)))
        eks.append(jnp.exp(jnp.where(row < SUB * i, r - bv, -jnp.inf)))
    eq = jnp.concatenate(eqs, axis=1)
    ek = jnp.concatenate(eks, axis=1)
    qt = jnp.concatenate([qv] * (N_SUB - 1), axis=1) * eq
    kt = jnp.concatenate([kv] * (N_SUB - 1), axis=1) * ek
    return qt, kt, eq, ek


def _hgrn_diag_e(b_s, i, s):
    t_io = lax.broadcasted_iota(jnp.int32, (SUB, HGRN_HEAD_DIM), 0)
    bi = b_s[pl.ds(SUB * i, SUB), :]
    return jnp.exp(jnp.where(t_io >= s, bi - b_s[pl.ds(SUB * i + s, 1), :], -jnp.inf)), t_io


def _hgrn_intra(q_s, k_s, b_s, a_s, qt, kt):
    ktp = jnp.concatenate([kt, jnp.zeros_like(kt)], axis=0)
    a_s[...] = _dot(qt, ktp, NT)
    yield
    col = lax.broadcasted_iota(jnp.int32, (SUB, HGRN_HEAD_DIM), 1)
    for i in range(N_SUB):
        qi = q_s[pl.ds(SUB * i, SUB), :]
        ai = jnp.zeros((SUB, HGRN_HEAD_DIM), F32)
        for s in range(SUB):
            e, _ = _hgrn_diag_e(b_s, i, s)
            a_col = jnp.sum(qi * k_s[pl.ds(SUB * i + s, 1), :] * e, axis=-1, keepdims=True)
            ai = ai + jnp.where(col == SUB * i + s, a_col, 0.0)
            if s % DIAG_STAGE == DIAG_STAGE - 1:
                yield
        a_s[pl.ds(SUB * i, SUB), :] += ai


def _mixer_fwd(proj, bias, gq, gk, lb, go, nb, seq):
    nc = seq // CHUNK
    hd = HGRN_HEAD_DIM
    nblk = ATTN_HEADS // 2
    rows_blk = seq // nblk
    nck = rows_blk // CHUNK
    per = nc // nck
    assert rows_blk % CHUNK == 0

    def body(aq_ref, ak_ref, av_ref, bias_ref, gq_ref, gk_ref, hq_ref, hf_ref, hi_ref, hg_ref, lb_ref, go_ref,
             ao_ref, y_ref, o_ref, st_ref, a_ref, qs_scr, k_scr, v_scr, st_all, q_all, k_all, b_all, a_all):
        _attn_prepare(aq_ref, ak_ref, av_ref, gq_ref, gk_ref, qs_scr, k_scr, v_scr)

        @pl.when(pl.program_id(1) == 0)
        def _():
            st_all[...] = jnp.zeros_like(st_all)

        lower = _tri(True)

        def head_chunk(hh, c, rows):
            ln = slice(hd * hh, hd * (hh + 1))
            st, q_s, k_s, b_s, a_s = st_all.at[hh], q_all.at[hh], k_all.at[hh], b_all.at[hh], a_all.at[hh]
            q, _, _, f = _hgrn_gates(hq_ref[rows, ln], hf_ref[rows, ln], lb_ref[:, ln])
            v = hi_ref[rows, ln]
            yield
            b = _dot_exact01(lower, jnp.log(f))
            q_s[...] = q
            k_s[...] = 1.0 - f
            b_s[...] = b
            st_ref[hh, c] = st[...]
            yield
            qt, kt, _, _ = _hgrn_offdiag(q_s, k_s, b_s)
            yield
            yield from _hgrn_intra(q_s, k_s, b_s, a_s, qt, kt)
            a16 = a_s[...].astype(BF16)
            a_ref[hh, c] = a16
            vp = jnp.concatenate([v, jnp.zeros_like(v)], axis=0)
            o = _dot(a16, vp) + _dot(q * jnp.exp(b), st[...], NT)
            yield
            bl = b_s[pl.ds(CHUNK - 1, 1), :]
            st[...] = st[...] * jnp.exp(bl) + _tn(v, (1.0 - f) * jnp.exp(bl - b))
            o_ref[rows, ln] = o
            yield
            n = o * lax.rsqrt(jnp.mean(o * o, axis=-1, keepdims=True) + RMS_EPS) * go_ref[...]
            hg = hg_ref[rows, ln]
            y_ref[rows, ln] = n * hg * _sigmoid(hg)

        def chunk(c, carry):
            rows = pl.ds(pl.multiple_of(c * CHUNK, CHUNK), CHUNK)
            _lockstep([_attn_fwd_chunk(c * per + a, qs_scr, k_scr, v_scr, bias_ref, ao_ref) for a in range(per)]
                      + [head_chunk(hh, c, rows) for hh in range(HGRN_HEADS)])
            return carry

        lax.fori_loop(0, nck, chunk, 0)

    hp, wide = HGRN_HEADS, HGRN_HEADS * hd

    def acol(off):
        return pl.BlockSpec((seq, 128), lambda b, s: (b, off + s))

    def col(off):
        return pl.BlockSpec((rows_blk, wide), lambda b, s: (b * nblk + s, off // hp))

    out = pl.BlockSpec((rows_blk, wide), lambda b, s: (b * nblk + s, 0))
    vec = pl.BlockSpec((1, 128), lambda b, s: (0, 0))
    t = nb * seq
    return _pallas(
        body, name="mixer_fwd", grid=(nb, nblk),
        in_specs=[acol(0), acol(4), acol(8), pl.BlockSpec((2 * CHUNK, BAND), lambda b, s: (s, 0)), vec, vec,
                  col(12), col(16), col(20), col(24), pl.BlockSpec((1, wide), lambda b, s: (0, 0)), vec],
        out_specs=[pl.BlockSpec((seq, 128), lambda b, s: (b, s)), out, out,
                   pl.BlockSpec((hp, nck, hd, hd), lambda b, s: (b, s, 0, 0)),
                   pl.BlockSpec((hp, nck, CHUNK, hd), lambda b, s: (b, s, 0, 0))],
        out_shape=[jax.ShapeDtypeStruct((t, ATTN_WIDTH), F32)] + [jax.ShapeDtypeStruct((t, wide), F32)] * 2
        + [jax.ShapeDtypeStruct((nb * hp, nc, hd, hd), F32), jax.ShapeDtypeStruct((nb * hp, nc, CHUNK, hd), BF16)],
        scratch_shapes=[pltpu.VMEM((seq, 128), BF16), pltpu.VMEM((seq + KPAD, 128), BF16),
                        pltpu.VMEM((seq + KPAD, 128), BF16), pltpu.VMEM((hp, hd, hd), F32)]
        + [pltpu.VMEM((hp, CHUNK, hd), F32)] * 4,
        sem=("parallel", "arbitrary"), args=(proj,) * 3 + (bias, gq, gk) + (proj,) * 4 + (lb, go))


def _hgrn_bwd(proj, lb, go, o_pre, states, scores, dout, nb, seq):
    nc = seq // CHUNK
    hd = HGRN_HEAD_DIM
    rows_blk = min(HGRN_ROWS, seq)
    nblk, nck = seq // rows_blk, rows_blk // CHUNK

    def body(hq_ref, hf_ref, hi_ref, hg_ref, lb_ref, go_ref, o_ref, st_ref, a_ref, dy_ref,
             dhq_ref, dhf_ref, dhi_ref, dhg_ref, dlb_ref, dgo_ref,
             dst_all, q_all, k_all, b_all, da_all, dqi_all, dki_all, dlb_all, dgo_all):
        @pl.when(pl.program_id(1) == 0)
        def _():
            dst_all[...] = jnp.zeros_like(dst_all)
            dlb_all[...] = jnp.zeros_like(dlb_all)
            dgo_all[...] = jnp.zeros_like(dgo_all)

        lower, upper = _tri(True), _tri(False)
        gov = go_ref[...]
        row = lax.broadcasted_iota(jnp.int32, (CHUNK, hd), 0)

        def head_chunk(hh, c, rows):
            ln = slice(hd * hh, hd * (hh + 1))
            dst, q_s, k_s, b_s = dst_all.at[hh], q_all.at[hh], k_all.at[hh], b_all.at[hh]
            da_s, dqi_s, dki_s = da_all.at[hh], dqi_all.at[hh], dki_all.at[hh]
            dlb_acc, dgo_acc = dlb_all.at[hh], dgo_all.at[hh]
            lbv = lb_ref[:, ln]
            hq, hf, v, hg = hq_ref[rows, ln], hf_ref[rows, ln], hi_ref[rows, ln], hg_ref[rows, ln]
            q, sq, sf, f = _hgrn_gates(hq, hf, lbv)
            kk = 1.0 - f
            yield
            b = _dot_exact01(lower, jnp.log(f))
            q_s[...] = q
            k_s[...] = kk
            b_s[...] = b
            yield
            bl = b_s[pl.ds(CHUNK - 1, 1), :]
            ebl = jnp.exp(bl)
            ekd = jnp.exp(bl - b)
            kd = kk * ekd
            eb = jnp.exp(b)
            qb = q * eb
            st0 = st_ref[hh, c]
            dst1 = dst[...]
            yield

            o = o_ref[rows, ln]
            dy = dy_ref[rows, ln]
            sg = _sigmoid(hg)
            rstd = lax.rsqrt(jnp.mean(o * o, axis=-1, keepdims=True) + RMS_EPS)
            ohat = o * rstd
            dn = dy * hg * sg
            dhg_ref[rows, ln] = (dy * ohat * gov * (sg * (1.0 + hg * (1.0 - sg)))).astype(BF16)
            dgo_acc[...] += jnp.sum(dn * ohat, axis=0, keepdims=True)
            gdn = dn * gov
            do = rstd * (gdn - ohat * jnp.mean(gdn * ohat, axis=-1, keepdims=True))
            yield

            qt, kt, eq, ek = _hgrn_offdiag(q_s, k_s, b_s)
            da = _dot(do, v, NT)
            dat = _dot(v, do, NT)
            da_s[...] = da
            yield
            dqo = _dot(da, kt) * eq
            dko = _dot(dat, qt) * ek
            dqi_s[...] = sum(dqo[:, j * hd:(j + 1) * hd] for j in range(N_SUB - 1))
            dki_s[...] = sum(dko[:, j * hd:(j + 1) * hd] for j in range(N_SUB - 1))
            yield
            col = lax.broadcasted_iota(jnp.int32, (SUB, CHUNK), 1)
            for i in range(N_SUB):
                qi = q_s[pl.ds(SUB * i, SUB), :]
                dai = da_s[pl.ds(SUB * i, SUB), :]
                dqd = jnp.zeros((SUB, hd), F32)
                dkd_ = jnp.zeros((SUB, hd), F32)
                for s in range(SUB):
                    e, t_io = _hgrn_diag_e(b_s, i, s)
                    dacol = jnp.sum(jnp.where(col == SUB * i + s, dai, 0.0), axis=-1, keepdims=True)
                    w = dacol * e
                    dqd = dqd + w * k_s[pl.ds(SUB * i + s, 1), :]
                    dkd_ = dkd_ + jnp.where(t_io == s, jnp.sum(w * qi, axis=0, keepdims=True), 0.0)
                    if s % DIAG_STAGE == DIAG_STAGE - 1:
                        yield
                dqi_s[pl.ds(SUB * i, SUB), :] += dqd
                dki_s[pl.ds(SUB * i, SUB), :] += dkd_
            dqi, dki = dqi_s[...], dki_s[...]

            dv = _tn(a_ref[hh, c].astype(F32), do)[0:CHUNK, :] + _dot(kd, dst1, NT)
            dqb = _dot(do, st0)
            dkd = _dot(v, dst1)
            yield
            t2 = dkd * kd
            dq = dqb * eb + dqi
            dk = dkd * ekd + dki
            dbl = jnp.sum(t2, axis=0, keepdims=True) + ebl * jnp.sum(st0 * dst1, axis=0, keepdims=True)
            db = dqb * qb - t2 + q * dqi - kk * dki + jnp.where(row == CHUNK - 1, dbl, 0.0)
            yield
            dg = _dot_exact01(upper, db)
            dst[...] = dst1 * ebl + _tn(do, qb)
            yield

            df = dg / f - dk
            dhf_ref[rows, ln] = (df * (1.0 - lbv) * sf * (1.0 - sf)).astype(BF16)
            dlb_acc[...] += jnp.sum(df * (1.0 - sf), axis=0, keepdims=True)
            dhq_ref[rows, ln] = (dq * (sq * (1.0 + hq * (1.0 - sq)))).astype(BF16)
            dhi_ref[rows, ln] = dv.astype(BF16)

        def chunk(it, carry):
            c = nck - 1 - it
            rows = pl.ds(pl.multiple_of(c * CHUNK, CHUNK), CHUNK)
            _lockstep([head_chunk(hh, c, rows) for hh in range(HGRN_HEADS)])
            return carry

        lax.fori_loop(0, nck, chunk, 0)

        @pl.when(pl.program_id(1) == nblk - 1)
        def _():
            dlb_ref[...] = dlb_all[...]
            dgo_ref[...] = dgo_all[...]

    hp, wide = HGRN_HEADS, HGRN_HEADS * hd

    def col(off):
        return pl.BlockSpec((rows_blk, wide), lambda b, s: (b * nblk + nblk - 1 - s, off // hp))

    out = pl.BlockSpec((rows_blk, wide), lambda b, s: (b * nblk + nblk - 1 - s, 0))
    part = pl.BlockSpec((hp, 1, hd), lambda b, s: (b, 0, 0))
    t = nb * seq
    return pl.pallas_call(
        body, name="hgrn_bwd", grid=(nb, nblk),
        in_specs=[col(12), col(16), col(20), col(24), pl.BlockSpec((1, wide), lambda b, s: (0, 0)),
                  pl.BlockSpec((1, hd), lambda b, s: (0, 0)), out,
                  pl.BlockSpec((hp, nck, hd, hd), lambda b, s: (b, nblk - 1 - s, 0, 0)),
                  pl.BlockSpec((hp, nck, CHUNK, hd), lambda b, s: (b, nblk - 1 - s, 0, 0)), col(4)],
        out_specs=[out, out, out, out, part, part],
        out_shape=[jax.ShapeDtypeStruct((t, wide), BF16)] * 4 + [jax.ShapeDtypeStruct((nb * hp, 1, hd), F32)] * 2,
        scratch_shapes=[pltpu.VMEM((hp, hd, hd), F32)] + [pltpu.VMEM((hp, CHUNK, hd), F32)] * 3
        + [pltpu.VMEM((hp, CHUNK, CHUNK), F32)] + [pltpu.VMEM((hp, CHUNK, hd), F32)] * 2
        + [pltpu.VMEM((hp, 1, hd), F32)] * 2,
        compiler_params=_params("parallel", "arbitrary"),
    )(proj, proj, proj, proj, lb, go, o_pre, states, scores, dout)


def _lb_fwd(lower_bounds):
    def body(x_ref, o_ref):
        xv = x_ref[...]
        e = jnp.exp(xv - jnp.max(xv, axis=0, keepdims=True))
        o_ref[...] = e[0:1, :] / jnp.sum(e, axis=0, keepdims=True)

    return pl.pallas_call(body, name="lb_fwd",
                          out_shape=jax.ShapeDtypeStruct((1, lower_bounds.shape[1]), F32))(lower_bounds)


def _lb_bwd(lower_bounds, dlb_parts):
    ng = dlb_parts.shape[0]

    def body(x_ref, d_ref, o_ref):
        xv = x_ref[...]
        e = jnp.exp(xv - jnp.max(xv, axis=0, keepdims=True))
        p = e / jnp.sum(e, axis=0, keepdims=True)
        dlb = d_ref[0]
        for gi in range(1, ng):
            dlb = dlb + d_ref[gi]
        first = lax.broadcasted_iota(jnp.int32, xv.shape, 0) == 0
        o_ref[...] = p * (jnp.where(first, dlb, 0.0) - p[0:1, :] * dlb)

    return pl.pallas_call(body, name="lb_bwd",
                          out_shape=jax.ShapeDtypeStruct(lower_bounds.shape, F32))(lower_bounds, dlb_parts)


def _ffn_bwd(x, g, h, gate, up, dy, dy16, w, put, tag):
    wg, wu, wd = w[tag + "_w_gate"], w[tag + "_w_up"], w[tag + "_w_down"]
    dgate, dup, dwd = _ffn_bwd_mid(dy16, wd, gate, up, tag + "_bwd_mid")
    put(tag + "_w_down", dwd)
    put(tag + "_w_gate", _mm(dgate, h, ta=True, tm=1408, tn=512, name=tag + "_dwg"))
    put(tag + "_w_up", _mm(dup, h, ta=True, tm=1408, tn=512, name=tag + "_dwu"))
    dh = _mm(dgate, wg, tm=512, tn=1024, name=tag + "_dh_gate")
    return _mm(dup, wu, tm=512, tn=1024, add=dh, norm_bwd=(x, g, dy), name=tag + "_dh_up")


def _local_step(x, tgt, sp, w, put, nb, seq):
    d = x.shape[1]
    h1 = _rms_fwd(x, sp["ffn1_norm_g"], "ffn1_norm")
    rb_pad = jnp.pad(sp["attn_rel_bias"], ((0, 0), (0, N_REL_PAD - N_REL)))
    bias = jnp.transpose(_bias_expand(rb_pad), (1, 0, 2)).reshape(ATTN_HEADS * CHUNK, BAND)
    gq2 = jnp.concatenate([sp["attn_q_norm_g"]] * 2, axis=1)
    gk2 = jnp.concatenate([sp["attn_k_norm_g"]] * 2, axis=1)
    lb = _lb_fwd(sp["hgrn_lower_bounds"])
    gate1, up1, x1, h2 = _ffn_fwd(h1, x, w["ffn1_w_gate"], w["ffn1_w_up"], w["ffn1_w_down"], "ffn1_fwd",
                                  next_g=sp["mix_norm_g"])
    proj = _mm(h2, w["w_in"], tb=True, tm=256, tn=w["w_in"].shape[0], name="in_proj")
    attn, hy, ho, hstate, hscore = _mixer_fwd(proj, bias, gq2, gk2, lb, sp["hgrn_out_norm_g"], nb, seq)
    mix = jnp.concatenate([attn, hy], axis=1)
    x2, h3 = _mm(mix, w["w_out"], tm=512, tn=1024, add=x1, norm_g=sp["ffn2_norm_g"], name="out_proj")
    gate2, up2, dx3, dx3_16, sq = _ffn_fwd(h3, x2, w["ffn2_w_gate"], w["ffn2_w_up"], w["ffn2_w_down"], "ffn2_fwd",
                                           tgt=tgt)
    loss = 0.5 * jnp.sum(sq) / d

    dx2, dx2_16, dg3 = _ffn_bwd(x2, sp["ffn2_norm_g"], h3, gate2, up2, dx3, dx3_16, w, put, "ffn2")
    dmix = _mm(dx2_16, w["w_out"], tb=True, tm=512, tn=1024, name="out_proj_dx")
    put("w_out", _mm(mix, dx2_16, ta=True, tm=512, tn=1024, name="out_proj_dw"))
    bias_t = jnp.transpose(bias.reshape(ATTN_HEADS // 2, 2 * CHUNK, BAND), (0, 2, 1)).reshape(-1, 2 * CHUNK)
    dq, dk, dv, dbias, dgq, dgk = _attn_bwd(proj, attn, dmix, bias_t, gq2, gk2, nb, seq)
    dbias = jnp.transpose(dbias.reshape(nb, ATTN_HEADS // 2, BAND, 2, CHUNK), (0, 4, 1, 3, 2))
    dbias = dbias.reshape(nb, CHUNK, ATTN_HEADS, BAND)
    dgq = jnp.sum(dgq, axis=(0, 1)).reshape(2, ATTN_HEAD_DIM).sum(axis=0, keepdims=True)
    dgk = jnp.sum(dgk, axis=(0, 1)).reshape(2, ATTN_HEAD_DIM).sum(axis=0, keepdims=True)
    dhq, dhf, dhi, dhg, dlb, dgo = _hgrn_bwd(proj, lb, sp["hgrn_out_norm_g"], ho, hstate, hscore, dmix, nb, seq)
    dproj = jnp.concatenate([dq, dk, dv, dhq, dhf, dhi, dhg], axis=1)
    put("w_in", _mm(dproj, h2, ta=True, tm=512, tn=1024, name="in_proj_dw"))
    dx1, dx1_16, dgm = _mm(dproj, w["w_in"], tm=512, tn=1024, norm_bwd=(x1, sp["mix_norm_g"], dx2),
                           name="in_proj_dx")
    dx0, _, dg1 = _ffn_bwd(x, sp["ffn1_norm_g"], h1, gate1, up1, dx1, dx1_16, w, put, "ffn1")

    small = {
        "ffn1_norm_g": dg1, "mix_norm_g": dgm, "ffn2_norm_g": dg3,
        "attn_q_norm_g": dgq, "attn_k_norm_g": dgk,
        "attn_rel_bias": _bias_fold(dbias)[:, :N_REL],
        "hgrn_lower_bounds": _lb_bwd(sp["hgrn_lower_bounds"], dlb.reshape(nb, 1, HGRN_HEADS * HGRN_HEAD_DIM)),
        "hgrn_out_norm_g": jnp.sum(dgo, axis=(0, 1))[None, :],
    }
    return loss, dx0, small


MESH = pl.DeviceIdType.MESH
ANY = pl.BlockSpec(memory_space=pl.ANY)


def _coords():
    return lax.axis_index("x"), lax.axis_index("y"), lax.axis_index("c")


def _other_chips(x, y):
    return [(1 - x, y), (x, 1 - y), (1 - x, 1 - y)]


def _gather_side(shards):
    n = len(shards)

    def copies(ins, outs, sems):
        send_sems, recv_sems, local_sems = sems
        x, y, c = _coords()
        me, sibling = (x, y, c), (x, y, 1 - c)
        chips = _other_chips(x, y)

        def copy(i, k, block, to, src=None):
            bx, by, bc = block
            dst = outs[i].at[4 * bx + 2 * by + bc]
            return pltpu.make_async_remote_copy(
                src_ref=dst if src is None else src, dst_ref=dst, send_sem=send_sems.at[i, k],
                recv_sem=recv_sems.at[i, k], device_id=to, device_id_type=MESH)

        mine = [pltpu.make_async_copy(ins[i], outs[i].at[4 * x + 2 * y + c], local_sems.at[i]) for i in range(n)]
        own = []
        for i in range(n):
            own.append(copy(i, 0, me, sibling, src=ins[i]))
            own += [copy(i, 1 + j, me, (*chip, c), src=ins[i]) for j, chip in enumerate(chips)]
        return copy, mine, own, me, sibling, chips, c

    def start(ins, outs, sems):
        _, mine, own, *_ = copies(ins, outs, sems)
        for cp in mine + own:
            cp.start()

    def finish(ins, outs, sems):
        copy, mine, own, me, sibling, chips, c = copies(ins, outs, sems)
        passed = []
        for i in range(n):
            for j, chip in enumerate(chips):
                copy(i, 1 + j, (*chip, c), me).wait_recv()
                passed.append(copy(i, 4 + j, (*chip, c), sibling))
                passed[-1].start()
        for i in range(n):
            copy(i, 0, sibling, me).wait_recv()
            for j, chip in enumerate(chips):
                copy(i, 4 + j, (*chip, 1 - c), me).wait_recv()
        for cp in own + passed:
            cp.wait_send()
        for cp in mine:
            cp.wait()

    return _Side(list(shards), [jax.ShapeDtypeStruct((N_DEV,) + s.shape, s.dtype) for s in shards],
                 [pltpu.SemaphoreType.DMA((n, 7)), pltpu.SemaphoreType.DMA((n, 7)), pltpu.SemaphoreType.DMA((n,))],
                 start, finish)


def _pair_side(grads):
    n = len(grads)

    def copies(ins, outs, sems):
        send_sems, recv_sems = sems
        x, y, c = _coords()
        return [pltpu.make_async_remote_copy(
            src_ref=ins[i].at[2 * k + 1 - c], dst_ref=outs[i].at[k], send_sem=send_sems.at[i, k],
            recv_sem=recv_sems.at[i, k], device_id=(x, y, 1 - c), device_id_type=MESH)
            for i in range(n) for k in range(4)]

    def start(ins, outs, sems):
        for cp in copies(ins, outs, sems):
            cp.start()

    def finish(ins, outs, sems):
        for cp in copies(ins, outs, sems):
            cp.wait()

    return _Side(list(grads), [jax.ShapeDtypeStruct((4,) + g.shape[1:], g.dtype) for g in grads],
                 [pltpu.SemaphoreType.DMA((n, 4)), pltpu.SemaphoreType.DMA((n, 4))], start, finish)


def _pair_add(grad, recv, core, name):
    _, r, cdim = grad.shape

    def body(c_ref, g_ref, r_ref, o_ref):
        o_ref[...] = (g_ref[...] + r_ref[...]).astype(BF16)

    blk = (1, r, cdim)
    return pl.pallas_call(
        body, name=name,
        grid_spec=pltpu.PrefetchScalarGridSpec(
            num_scalar_prefetch=1, grid=(4,),
            in_specs=[pl.BlockSpec(blk, lambda k, c_ref: (2 * k + c_ref[0], 0, 0)),
                      pl.BlockSpec(blk, lambda k, c_ref: (k, 0, 0))],
            out_specs=pl.BlockSpec(blk, lambda k, c_ref: (k, 0, 0))),
        out_shape=jax.ShapeDtypeStruct((4, r, cdim), BF16),
        compiler_params=_params("arbitrary"),
    )(core, grad, recv)


def _chip_side(parts):
    n = len(parts)

    def copies(ins, outs, sems):
        send_sems, recv_sems, local_sems = sems
        x, y, c = _coords()
        chips = _other_chips(x, y)
        mine = [pltpu.make_async_copy(ins[i].at[2 * x + y], outs[i].at[2 * x + y], local_sems.at[i])
                for i in range(n)]
        sent = [pltpu.make_async_remote_copy(
            src_ref=ins[i].at[2 * px + py], dst_ref=outs[i].at[2 * x + y], send_sem=send_sems.at[i, j],
            recv_sem=recv_sems.at[i, j], device_id=(px, py, c), device_id_type=MESH)
            for i in range(n) for j, (px, py) in enumerate(chips)]
        return mine, sent, chips, c

    def start(ins, outs, sems):
        mine, sent, _, _ = copies(ins, outs, sems)
        for cp in mine + sent:
            cp.start()

    def finish(ins, outs, sems):
        mine, sent, chips, c = copies(ins, outs, sems)
        send_sems, recv_sems, _ = sems
        for i in range(n):
            for j, (px, py) in enumerate(chips):
                landed = outs[i].at[2 * px + py]
                pltpu.make_async_remote_copy(
                    src_ref=landed, dst_ref=landed, send_sem=send_sems.at[i, j], recv_sem=recv_sems.at[i, j],
                    device_id=(px, py, c), device_id_type=MESH).wait_recv()
        for cp in sent:
            cp.wait_send()
        for cp in mine:
            cp.wait()

    return _Side(list(parts), [jax.ShapeDtypeStruct(p.shape, p.dtype) for p in parts],
                 [pltpu.SemaphoreType.DMA((n, 3)), pltpu.SemaphoreType.DMA((n, 3)), pltpu.SemaphoreType.DMA((n,))],
                 start, finish)


def _all_reduce_small(v):
    r = v.shape[0]

    def body(v_ref, o_ref, buf, send_sems, recv_sems):
        x, y, c = _coords()
        me = 4 * x + 2 * y + c
        buf[me] = v_ref[...]
        cps = []
        for k in range(1, N_DEV):
            px = 1 - x if k & 4 else x
            py = 1 - y if k & 2 else y
            pc = 1 - c if k & 1 else c
            cps.append((pltpu.make_async_remote_copy(
                src_ref=v_ref, dst_ref=buf.at[me], send_sem=send_sems.at[k - 1], recv_sem=recv_sems.at[k - 1],
                device_id=(px, py, pc), device_id_type=MESH), 4 * px + 2 * py + pc))
        for cp, _ in cps:
            cp.start()
        for k, (cp, peer) in enumerate(cps):
            pltpu.make_async_remote_copy(
                src_ref=v_ref, dst_ref=buf.at[peer], send_sem=send_sems.at[k], recv_sem=recv_sems.at[k],
                device_id=(x, y, c), device_id_type=MESH).wait_recv()
        for cp, _ in cps:
            cp.wait_send()
        acc = buf[0]
        for j in range(1, N_DEV):
            acc = acc + buf[j]
        o_ref[...] = acc

    return pl.pallas_call(
        body, name="small_all_reduce", out_shape=jax.ShapeDtypeStruct(v.shape, F32),
        in_specs=[pl.BlockSpec(memory_space=pltpu.VMEM)], out_specs=pl.BlockSpec(memory_space=pltpu.VMEM),
        scratch_shapes=[pltpu.VMEM((N_DEV, r, 128), F32), pltpu.SemaphoreType.DMA((N_DEV - 1,)),
                        pltpu.SemaphoreType.DMA((N_DEV - 1,))],
    )(v)


def _adamw(w, m, v, g, name):
    parts = w.ndim == 3
    r, cdim = w.shape[-2:]
    tr = r // 4 if r % 32 == 0 else r

    def body(w_ref, m_ref, v_ref, g_ref, go_ref, d_ref, mo_ref, vo_ref):
        if parts:
            gv = g_ref[0].astype(F32)
            for k in range(1, 4):
                gv = gv + g_ref[k].astype(F32)
            gv = gv[None]
        else:
            gv = g_ref[...]
        m2 = ADAM_B1 * m_ref[...] + (1.0 - ADAM_B1) * gv
        v2 = ADAM_B2 * v_ref[...] + (1.0 - ADAM_B2) * (gv * gv)
        m_hat = m2 / (1.0 - ADAM_B1 ** ADAM_STEP)
        v_hat = v2 / (1.0 - ADAM_B2 ** ADAM_STEP)
        go_ref[...] = gv
        d_ref[...] = -ADAM_LR * (m_hat / (jnp.sqrt(v_hat) + ADAM_EPS) + ADAM_WD * w_ref[...])
        mo_ref[...] = m2
        vo_ref[...] = v2

    if parts:
        row = pl.BlockSpec((1, tr, cdim), lambda i: (0, i, 0))
        g_spec = pl.BlockSpec((4, tr, cdim), lambda i: (0, i, 0))
    else:
        row = g_spec = pl.BlockSpec((tr, cdim), lambda i: (i, 0))
    return pl.pallas_call(
        body, name=name, grid=(r // tr,), in_specs=[row, row, row, g_spec], out_specs=[row] * 4,
        out_shape=[jax.ShapeDtypeStruct(w.shape, F32)] * 4,
        compiler_params=_params("parallel"),
    )(w, m, v, g)


WEIGHTS = ["ffn1_norm_g", "ffn1_w_gate", "ffn1_w_up", "ffn1_w_down", "mix_norm_g", "w_in", "attn_q_norm_g",
           "attn_k_norm_g", "attn_rel_bias", "hgrn_lower_bounds", "hgrn_out_norm_g", "w_out", "ffn2_norm_g",
           "ffn2_w_gate", "ffn2_w_up", "ffn2_w_down"]
COL_SHARDED = ("ffn1_w_gate", "ffn1_w_up", "w_in", "ffn2_w_gate", "ffn2_w_up")
ROW_SHARDED = ("ffn1_w_down", "w_out", "ffn2_w_down")
BIG = [n for n in WEIGHTS if n in COL_SHARDED or n in ROW_SHARDED]
SMALL = [n for n in WEIGHTS if n not in BIG]
PACK_ROWS = 8
FFN2 = ["ffn2_w_down", "ffn2_w_gate", "ffn2_w_up"]
MIXER = ["w_out", "w_in"]

PLAN = {
    "ffn1_norm": [("gather", ["ffn1_w_down"])],
    "bias_expand": [("gather", ["ffn1_w_gate", "ffn1_w_up"])],
    "ffn1_fwd": [("gather", MIXER)],
    "mixer_fwd": [("gather", FFN2)],
    "ffn2_dh_gate": [("pair", FFN2)],
    "attn_bwd": [("chip", FFN2)],
    "in_proj_dx": [("pair", MIXER)],
    "ffn1_bwd_mid": [("chip", MIXER)],
    "ffn1_dwg": [("pair", ["ffn1_w_down"])],
    "ffn1_dwu": [("chip", ["ffn1_w_down"]), ("pair", ["ffn1_w_gate"])],
    "ffn1_dh_gate": [("chip", ["ffn1_w_gate"]), ("pair", ["ffn1_w_up"])],
    "bias_fold": [("chip", ["ffn1_w_up"])],
}


def _join_sides(sides):
    def split(refs, counts):
        out, at = [], 0
        for n in counts:
            out.append(refs[at:at + n])
            at += n
        return out

    n_in, n_out, n_sem = ([len(getattr(s, f)) for s in sides] for f in ("ins", "out_shape", "sems"))

    def run(which):
        def go(ins, outs, sems):
            for s, i, o, m in zip(sides, split(ins, n_in), split(outs, n_out), split(sems, n_sem)):
                getattr(s, which)(i, o, m)
        return go

    return _Side([a for s in sides for a in s.ins], [a for s in sides for a in s.out_shape],
                 [a for s in sides for a in s.sems], run("start"), run("finish"))


class _Schedule:
    def __init__(self, shards):
        self.shards = shards
        self.weights = {}
        self.sliced = {}
        self.partials = {}
        self.reduced = {}

    def put(self, name, grad):
        self.sliced[name] = grad.reshape((N_DEV,) + self.shards[name].shape)

    def side_for(self, call):
        if call not in PLAN:
            return None
        sides = []
        for kind, names in PLAN[call]:
            if kind == "gather":
                sides.append(_gather_side([self.shards[n] for n in names]))
            elif kind == "pair":
                sides.append(_pair_side([self.sliced[n] for n in names]))
            else:
                sides.append(_chip_side([self.partials[n] for n in names]))
        return _join_sides(sides)

    def done(self, call, outs):
        at = 0
        for kind, names in PLAN[call]:
            self.file(kind, names, outs[at:at + len(names)])
            at += len(names)

    def file(self, kind, names, outs):
        for n, o in zip(names, outs):
            if kind == "gather":
                self.weights[n] = o.reshape(N_DEV * o.shape[1], o.shape[2])
            elif kind == "pair":
                core = lax.axis_index("c").astype(jnp.int32).reshape(1)
                self.partials[n] = _pair_add(self.sliced[n], o, core, n + "_pair_add")
            else:
                self.reduced[n] = o


def _pack_small(vals, loss=None):
    parts = []
    for n in SMALL:
        a = vals[n]
        if n == "attn_rel_bias":
            a = jnp.pad(a.reshape(ATTN_HEADS, N_REL), ((0, 0), (0, N_REL_PAD - N_REL)))
        flat = a.reshape(-1)
        size = -(-flat.shape[0] // (PACK_ROWS * 128)) * PACK_ROWS * 128
        parts.append(jnp.pad(flat, (0, size - flat.shape[0])).reshape(-1, 128))
    tail = jnp.zeros((PACK_ROWS, 128), F32)
    if loss is not None:
        tail = tail.at[0, 0].set(loss)
    return jnp.concatenate(parts + [tail], axis=0)


def _unpack_small(packed, shapes):
    out, row = {}, 0
    for n in SMALL:
        shape = shapes[n]
        if n == "attn_rel_bias":
            rows = ATTN_HEADS * N_REL_PAD // 128
            out[n] = packed[row:row + rows].reshape(ATTN_HEADS, N_REL_PAD)[:, :N_REL].reshape(shape)
        else:
            size = 1
            for s in shape:
                size *= s
            rows = -(-size // (PACK_ROWS * 128)) * PACK_ROWS
            out[n] = packed[row:row + rows].reshape(-1)[:size].reshape(shape)
        row += rows
    return out, packed[row, 0]


def kernel(x, ffn1_norm_g, ffn1_w_gate, ffn1_w_up, ffn1_w_down, mix_norm_g, w_in, attn_q_norm_g, attn_k_norm_g, attn_rel_bias, hgrn_lower_bounds, hgrn_out_norm_g, w_out, ffn2_norm_g, ffn2_w_gate, ffn2_w_up, ffn2_w_down, loss_target, m_ffn1_norm_g, m_ffn1_w_gate, m_ffn1_w_up, m_ffn1_w_down, m_mix_norm_g, m_w_in, m_attn_q_norm_g, m_attn_k_norm_g, m_attn_rel_bias, m_hgrn_lower_bounds, m_hgrn_out_norm_g, m_w_out, m_ffn2_norm_g, m_ffn2_w_gate, m_ffn2_w_up, m_ffn2_w_down, v_ffn1_norm_g, v_ffn1_w_gate, v_ffn1_w_up, v_ffn1_w_down, v_mix_norm_g, v_w_in, v_attn_q_norm_g, v_attn_k_norm_g, v_attn_rel_bias, v_hgrn_lower_bounds, v_hgrn_out_norm_g, v_w_out, v_ffn2_norm_g, v_ffn2_w_gate, v_ffn2_w_up, v_ffn2_w_down):
    wts = dict(zip(WEIGHTS, (ffn1_norm_g, ffn1_w_gate, ffn1_w_up, ffn1_w_down, mix_norm_g, w_in, attn_q_norm_g,
                             attn_k_norm_g, attn_rel_bias, hgrn_lower_bounds, hgrn_out_norm_g, w_out, ffn2_norm_g,
                             ffn2_w_gate, ffn2_w_up, ffn2_w_down)))
    mom = dict(zip(WEIGHTS, (m_ffn1_norm_g, m_ffn1_w_gate, m_ffn1_w_up, m_ffn1_w_down, m_mix_norm_g, m_w_in,
                             m_attn_q_norm_g, m_attn_k_norm_g, m_attn_rel_bias, m_hgrn_lower_bounds,
                             m_hgrn_out_norm_g, m_w_out, m_ffn2_norm_g, m_ffn2_w_gate, m_ffn2_w_up, m_ffn2_w_down)))
    var = dict(zip(WEIGHTS, (v_ffn1_norm_g, v_ffn1_w_gate, v_ffn1_w_up, v_ffn1_w_down, v_mix_norm_g, v_w_in,
                             v_attn_q_norm_g, v_attn_k_norm_g, v_attn_rel_bias, v_hgrn_lower_bounds,
                             v_hgrn_out_norm_g, v_w_out, v_ffn2_norm_g, v_ffn2_w_gate, v_ffn2_w_up, v_ffn2_w_down)))
    nb, seq, d = x.shape
    shapes = {n: wts[n].shape for n in WEIGHTS}

    def rows_first(a, n):
        return jnp.swapaxes(a, 1, 2) if n in COL_SHARDED else a

    sched = _Schedule({n: rows_first(wts[n], n)[0].astype(BF16) for n in BIG})
    sp = {n: wts[n] for n in SMALL}
    sp["attn_rel_bias"] = wts["attn_rel_bias"][0]
    _ACTIVE[0] = sched
    try:
        loss, dx, dsmall = _local_step(x.reshape(nb * seq, d), loss_target.reshape(nb * seq, d), sp,
                                       sched.weights, sched.put, nb, seq)
    finally:
        _ACTIVE[0] = None
    reduced = sched.reduced

    small_sum = _all_reduce_small(_pack_small(dsmall, loss))
    gsmall, loss_total = _unpack_small(small_sum, shapes)

    grads, deltas, new_m, new_v = {}, {}, {}, {}
    for n in BIG:
        out = _adamw(rows_first(wts[n], n), rows_first(mom[n], n), rows_first(var[n], n), reduced[n], n + "_adamw")
        grads[n], deltas[n], new_m[n], new_v[n] = (rows_first(o, n) for o in out)
    packed = _adamw(_pack_small(wts), _pack_small(mom), _pack_small(var), small_sum, "small_adamw")
    for dst, p in zip((deltas, new_m, new_v), packed[1:]):
        dst.update(_unpack_small(p, shapes)[0])
    grads.update(gsmall)

    return (loss_total, dx.reshape(nb, seq, d), *[grads[n] for n in WEIGHTS], *[deltas[n] for n in WEIGHTS],
            *[new_m[n] for n in WEIGHTS], *[new_v[n] for n in WEIGHTS])
```

```python
import functools

import jax
import jax.numpy as jnp
from jax import lax
from jax.experimental import pallas as pl
from jax.experimental.pallas import tpu as pltpu

F32 = jnp.float32
BF16 = jnp.bfloat16

RMS_EPS = 1e-6
CHUNK = 64
LEFT_CHUNKS = 8
BAND = (LEFT_CHUNKS + 2) * CHUNK
KPAD = BAND - CHUNK
REL_CLIP = 128
N_REL = 2 * REL_CLIP + 1
N_REL_PAD = 384
ATTN_HEADS = 8
ATTN_HEAD_DIM = 64
ATTN_WIDTH = ATTN_HEADS * ATTN_HEAD_DIM
ATTN_LOCKSTEP = 4
ATTN_UNROLL = 8
HGRN_HEADS = 4
HGRN_HEAD_DIM = 128
HGRN_ROWS = 512
SUB = 16
N_SUB = CHUNK // SUB
DIAG_STAGE = 4
N_DEV = 8

ADAM_LR = 0.001
ADAM_B1 = 0.9
ADAM_B2 = 0.999
ADAM_EPS = 1e-08
ADAM_WD = 0.01
ADAM_STEP = 10

VMEM_LIMIT = 56 * 1024 * 1024

NT = (((1,), (1,)), ((), ()))
NN = (((1,), (0,)), ((), ()))


def _params(*sem):
    return pltpu.CompilerParams(dimension_semantics=sem, vmem_limit_bytes=VMEM_LIMIT)


def _sigmoid(v):
    return 0.5 * jnp.tanh(0.5 * v) + 0.5


def _dot(a, b, dims=NN):
    return lax.dot_general(a.astype(BF16), b.astype(BF16), dims, preferred_element_type=F32)


def _dot_exact01(m01, v):
    m = m01.astype(BF16)
    hi = v.astype(BF16)
    r1 = v - hi.astype(F32)
    mid = r1.astype(BF16)
    lo = (r1 - mid.astype(F32)).astype(BF16)
    out = lax.dot_general(m, hi, NN, preferred_element_type=F32)
    out = out + lax.dot_general(m, mid, NN, preferred_element_type=F32)
    return out + lax.dot_general(m, lo, NN, preferred_element_type=F32)


def _dot_exact01_r(v, m01):
    m = m01.astype(BF16)
    hi = v.astype(BF16)
    r1 = v - hi.astype(F32)
    mid = r1.astype(BF16)
    lo = (r1 - mid.astype(F32)).astype(BF16)
    out = lax.dot_general(hi, m, NN, preferred_element_type=F32)
    out = out + lax.dot_general(mid, m, NN, preferred_element_type=F32)
    return out + lax.dot_general(lo, m, NN, preferred_element_type=F32)


def _lockstep(stages):
    live = list(stages)
    while live:
        still = []
        for g in live:
            try:
                next(g)
                still.append(g)
            except StopIteration:
                pass
        live = still


def _row_sums_on_lanes(v):
    ones = jnp.ones((8, v.shape[1]), BF16)
    hi = v.astype(BF16)
    r1 = v - hi.astype(F32)
    mid = r1.astype(BF16)
    lo = (r1 - mid.astype(F32)).astype(BF16)
    out = lax.dot_general(ones, hi, NT, preferred_element_type=F32)
    out = out + lax.dot_general(ones, mid, NT, preferred_element_type=F32)
    return (out + lax.dot_general(ones, lo, NT, preferred_element_type=F32))[0:1, :]


def _tn(a, b):
    ap = jnp.concatenate([a, jnp.zeros_like(a)], axis=0)
    bp = jnp.concatenate([b, jnp.zeros_like(b)], axis=0)
    return _dot(ap.T, bp)


def _row_tile(t):
    for tm in (512, 256, 128, 64, 32, 16, 8):
        if t % tm == 0:
            return tm
    raise ValueError(t)


class _Side:
    def __init__(self, ins, out_shape, sems, start, finish):
        self.ins, self.out_shape, self.sems, self.start, self.finish = ins, out_shape, sems, start, finish


_ACTIVE = [None]


def _pallas(body, *, name, grid, in_specs, out_specs, out_shape, scratch_shapes=(), sem, args):
    sched = _ACTIVE[0]
    side = sched.side_for(name) if sched is not None else None
    if side is None:
        return pl.pallas_call(
            body, name=name, grid=grid, in_specs=list(in_specs), out_specs=list(out_specs),
            out_shape=list(out_shape), scratch_shapes=list(scratch_shapes), compiler_params=_params(*sem))(*args)
    cuts = [len(in_specs), len(side.ins), len(out_shape), len(side.out_shape), len(scratch_shapes)]

    def with_side(*refs):
        groups, at = [], 0
        for n in cuts:
            groups.append(refs[at:at + n])
            at += n
        ins, side_ins, outs, side_outs, scratch = groups
        side_sems = refs[at:]
        first = pl.program_id(0) == 0
        last = pl.program_id(0) == grid[0] - 1
        for a in range(1, len(grid)):
            first = jnp.logical_and(first, pl.program_id(a) == 0)
            last = jnp.logical_and(last, pl.program_id(a) == grid[a] - 1)

        @pl.when(first)
        def _():
            side.start(side_ins, side_outs, side_sems)

        body(*ins, *outs, *scratch)

        @pl.when(last)
        def _():
            side.finish(side_ins, side_outs, side_sems)

    hbm = pl.BlockSpec(memory_space=pl.ANY)
    res = pl.pallas_call(
        with_side, name=name, grid=grid, in_specs=list(in_specs) + [hbm] * len(side.ins),
        out_specs=list(out_specs) + [hbm] * len(side.out_shape), out_shape=list(out_shape) + list(side.out_shape),
        scratch_shapes=list(scratch_shapes) + list(side.sems),
        compiler_params=_params(*(["arbitrary"] * len(grid))))(*args, *side.ins)
    sched.done(name, res[len(out_shape):])
    return res[:len(out_shape)]


def _rms_fwd(x, g, name):
    t, d = x.shape
    tm = _row_tile(t)

    def body(x_ref, g_ref, h_ref):
        xv = x_ref[...]
        r = lax.rsqrt(jnp.mean(xv * xv, axis=-1, keepdims=True) + RMS_EPS)
        h_ref[...] = (xv * r * g_ref[...]).astype(BF16)

    return _pallas(
        body, name=name, grid=(t // tm,),
        in_specs=[pl.BlockSpec((tm, d), lambda i: (i, 0)), pl.BlockSpec((1, d), lambda i: (0, 0))],
        out_specs=[pl.BlockSpec((tm, d), lambda i: (i, 0))], out_shape=[jax.ShapeDtypeStruct((t, d), BF16)],
        sem=("parallel",), args=(x, g))[0]


def _accumulate(ref, part, step):
    @pl.when(step == 0)
    def _():
        ref[...] = part

    @pl.when(step > 0)
    def _():
        ref[...] += part


def _mm(a, b, *, ta=False, tb=False, tm, tn, out_dtype=F32, add=None, scale=1.0, norm_g=None, norm_bwd=None, name):
    m, k = (a.shape[1], a.shape[0]) if ta else a.shape
    n = b.shape[0] if tb else b.shape[1]
    tm, tn = min(tm, m), min(tn, n)
    assert m % tm == 0 and n % tn == 0, (m, n, tm, tn)
    assert (norm_g is None and norm_bwd is None) or tn == n
    dims = (((0 if ta else 1,), (1 if tb else 0,)), ((), ()))
    n_in = 2 + (add is not None) + (norm_g is not None) + (3 if norm_bwd is not None else 0)

    def body(*refs):
        ins, outs = list(refs[2:n_in]), refs[n_in:]
        r = lax.dot_general(refs[0][...].astype(BF16), refs[1][...].astype(BF16), dims, preferred_element_type=F32)
        if scale != 1.0:
            r = r * scale
        if add is not None:
            r = r + ins.pop(0)[...]
        if norm_bwd is not None:
            xv, gv, dres = (ref[...] for ref in ins)
            rs = lax.rsqrt(jnp.mean(xv * xv, axis=-1, keepdims=True) + RMS_EPS)
            xhat = xv * rs
            gd = r * gv
            dx = dres + rs * (gd - xhat * jnp.mean(gd * xhat, axis=-1, keepdims=True))
            outs[0][...] = dx
            outs[1][...] = dx.astype(BF16)
            _accumulate(outs[2], jnp.sum(r * xhat, axis=0, keepdims=True), pl.program_id(0))
            return
        outs[0][...] = r.astype(out_dtype)
        if norm_g is not None:
            rs = lax.rsqrt(jnp.mean(r * r, axis=-1, keepdims=True) + RMS_EPS)
            outs[1][...] = (r * rs * ins.pop(0)[...]).astype(BF16)

    a_spec = pl.BlockSpec((k, tm), lambda i, j: (0, i)) if ta else pl.BlockSpec((tm, k), lambda i, j: (i, 0))
    b_spec = pl.BlockSpec((tn, k), lambda i, j: (j, 0)) if tb else pl.BlockSpec((k, tn), lambda i, j: (0, j))
    o_spec = pl.BlockSpec((tm, tn), lambda i, j: (i, j))
    vec = pl.BlockSpec((1, tn), lambda i, j: (0, j))
    args, specs = [a, b], [a_spec, b_spec]
    if add is not None:
        args.append(add)
        specs.append(o_spec)
    out_specs, out_shape = [o_spec], [jax.ShapeDtypeStruct((m, n), out_dtype)]
    if norm_g is not None:
        args.append(norm_g)
        specs.append(vec)
        out_specs.append(o_spec)
        out_shape.append(jax.ShapeDtypeStruct((m, n), BF16))
    if norm_bwd is not None:
        args += list(norm_bwd)
        specs += [o_spec, vec, o_spec]
        out_specs = [o_spec, o_spec, vec]
        out_shape = [jax.ShapeDtypeStruct((m, n), F32), jax.ShapeDtypeStruct((m, n), BF16),
                     jax.ShapeDtypeStruct((1, n), F32)]
    res = _pallas(body, name=name, grid=(m // tm, n // tn), in_specs=specs, out_specs=out_specs, out_shape=out_shape,
                  sem=("arbitrary", "arbitrary") if norm_bwd is not None else ("parallel", "parallel"), args=args)
    return res[0] if len(res) == 1 else res


def _ffn_tile(f):
    for tf in (1408, 512, 256, 128):
        if f % tf == 0:
            return tf
    raise ValueError(f)


def _ffn_fwd(h, x, wg, wu, wd, name, next_g=None, tgt=None):
    t, d = x.shape
    f = wg.shape[0]
    tm, tf = _row_tile(t), _ffn_tile(f)
    nf = f // tf
    assert (next_g is None) != (tgt is None)

    def body(h_ref, x_ref, wg_ref, wu_ref, wd_ref, tail_ref, g_ref, u_ref, o0_ref, o1_ref, *rest):
        acc_ref = rest[-1]
        j = pl.program_id(1)
        hv = h_ref[...]
        gv = lax.dot_general(hv, wg_ref[...], NT, preferred_element_type=F32)
        uv = lax.dot_general(hv, wu_ref[...], NT, preferred_element_type=F32)
        av = gv * _sigmoid(gv) * uv
        g_ref[...] = gv.astype(BF16)
        u_ref[...] = uv.astype(BF16)
        _accumulate(acc_ref, lax.dot_general(av.astype(BF16), wd_ref[...], NN, preferred_element_type=F32), j)

        @pl.when(j == nf - 1)
        def _():
            y = x_ref[...] + 0.5 * acc_ref[...]
            if tgt is None:
                o0_ref[...] = y
                rs = lax.rsqrt(jnp.mean(y * y, axis=-1, keepdims=True) + RMS_EPS)
                o1_ref[...] = (y * rs * tail_ref[...]).astype(BF16)
            else:
                e = y - tail_ref[...]
                dy = e * (1.0 / d)
                o0_ref[...] = dy
                o1_ref[...] = dy.astype(BF16)
                _accumulate(rest[0], jnp.sum(e * e, axis=0, keepdims=True), pl.program_id(0))

    row = pl.BlockSpec((tm, d), lambda i, j: (i, 0))
    hid = pl.BlockSpec((tm, tf), lambda i, j: (i, j))
    vec = pl.BlockSpec((1, d), lambda i, j: (0, 0))
    out_specs = [hid, hid, row, row] + ([vec] if tgt is not None else [])
    out_shape = [jax.ShapeDtypeStruct((t, f), BF16)] * 2 + [jax.ShapeDtypeStruct((t, d), F32),
                                                            jax.ShapeDtypeStruct((t, d), BF16)]
    if tgt is not None:
        out_shape.append(jax.ShapeDtypeStruct((1, d), F32))
    return _pallas(
        body, name=name, grid=(t // tm, nf),
        in_specs=[row, row] + [pl.BlockSpec((tf, d), lambda i, j: (j, 0))] * 3 + [vec if tgt is None else row],
        out_specs=out_specs, out_shape=out_shape, scratch_shapes=[pltpu.VMEM((tm, d), F32)],
        sem=("parallel" if tgt is None else "arbitrary", "arbitrary"),
        args=(h, x, wg, wu, wd, next_g if tgt is None else tgt))


def _ffn_up(h, wg, wu, name):
    t, d = h.shape
    f = wg.shape[0]
    tm, tf = _row_tile(t), _ffn_tile(f)

    def body(h_ref, wg_ref, wu_ref, g_ref, u_ref, a_ref):
        hv = h_ref[...]
        gv = lax.dot_general(hv, wg_ref[...], NT, preferred_element_type=F32)
        uv = lax.dot_general(hv, wu_ref[...], NT, preferred_element_type=F32)
        g_ref[...] = gv.astype(BF16)
        u_ref[...] = uv.astype(BF16)
        a_ref[...] = (gv * _sigmoid(gv) * uv).astype(BF16)

    hid = pl.BlockSpec((tm, tf), lambda i, j: (i, j))
    wrow = pl.BlockSpec((tf, d), lambda i, j: (j, 0))
    return _pallas(
        body, name=name, grid=(t // tm, f // tf), in_specs=[pl.BlockSpec((tm, d), lambda i, j: (i, 0)), wrow, wrow],
        out_specs=[hid, hid, hid], out_shape=[jax.ShapeDtypeStruct((t, f), BF16)] * 3,
        sem=("parallel", "parallel"), args=(h, wg, wu))


def _ffn_bwd_mid(dy, wd, g, u, name):
    t, d = dy.shape
    f = wd.shape[0]
    tm, tf = _row_tile(t), _ffn_tile(f)

    def body(dy_ref, wd_ref, g_ref, u_ref, dg_ref, du_ref, dwd_ref):
        dy16 = dy_ref[...]
        da = 0.5 * lax.dot_general(dy16, wd_ref[...], NT, preferred_element_type=F32)
        gv = g_ref[...].astype(F32)
        uv = u_ref[...].astype(F32)
        s = _sigmoid(gv)
        silu = gv * s
        dg_ref[...] = (da * uv * (s * (1.0 + gv * (1.0 - s)))).astype(BF16)
        du_ref[...] = (da * silu).astype(BF16)
        part = 0.5 * lax.dot_general((silu * uv).astype(BF16), dy16, (((0,), (0,)), ((), ())),
                                     preferred_element_type=F32)
        _accumulate(dwd_ref, part, pl.program_id(1))

    hid = pl.BlockSpec((tm, tf), lambda j, i: (i, j))
    wrow = pl.BlockSpec((tf, d), lambda j, i: (j, 0))
    return _pallas(
        body, name=name, grid=(f // tf, t // tm),
        in_specs=[pl.BlockSpec((tm, d), lambda j, i: (i, 0)), wrow, hid, hid],
        out_specs=[hid, hid, wrow],
        out_shape=[jax.ShapeDtypeStruct((t, f), BF16)] * 2 + [jax.ShapeDtypeStruct((f, d), F32)],
        sem=("parallel", "arbitrary"), args=(dy, wd, g, u))


def _rel_index(t, s_band):
    return jnp.clip(t + KPAD - s_band, -REL_CLIP, REL_CLIP) + REL_CLIP


def _bias_expand(rel_bias_pad):
    nh = rel_bias_pad.shape[0]

    def body(rb_ref, out_ref):
        rb = rb_ref[...]
        i_io = lax.broadcasted_iota(jnp.int32, (N_REL_PAD, BAND), 0)
        s_io = lax.broadcasted_iota(jnp.int32, (N_REL_PAD, BAND), 1)

        def row(t, carry):
            onehot = (i_io == _rel_index(t, s_io)).astype(F32)
            out_ref[t] = _dot_exact01_r(rb, onehot)
            return carry

        lax.fori_loop(0, CHUNK, row, 0)

    return _pallas(
        body, name="bias_expand", grid=(1,), in_specs=[pl.BlockSpec(rel_bias_pad.shape, lambda i: (0, 0))],
        out_specs=[pl.BlockSpec((CHUNK, nh, BAND), lambda i: (0, 0, 0))],
        out_shape=[jax.ShapeDtypeStruct((CHUNK, nh, BAND), F32)], sem=("arbitrary",), args=(rel_bias_pad,))[0]


def _bias_fold(dbias):
    ng, nh = dbias.shape[0], dbias.shape[2]

    def body(db_ref, out_ref):
        s_io = lax.broadcasted_iota(jnp.int32, (BAND, N_REL_PAD), 0)
        i_io = lax.broadcasted_iota(jnp.int32, (BAND, N_REL_PAD), 1)

        def row(t, acc):
            onehot = (i_io == _rel_index(t, s_io)).astype(F32)
            d = db_ref[0, t]
            for gi in range(1, ng):
                d = d + db_ref[gi, t]
            return acc + _dot_exact01_r(d, onehot)

        out_ref[...] = lax.fori_loop(0, CHUNK, row, jnp.zeros((nh, N_REL_PAD), F32))

    return _pallas(
        body, name="bias_fold", grid=(1,), in_specs=[pl.BlockSpec(dbias.shape, lambda i: (0, 0, 0, 0))],
        out_specs=[pl.BlockSpec((nh, N_REL_PAD), lambda i: (0, 0))],
        out_shape=[jax.ShapeDtypeStruct((nh, N_REL_PAD), F32)], sem=("arbitrary",), args=(dbias,))[0]


def _left_half(shape):
    return lax.broadcasted_iota(jnp.int32, shape, len(shape) - 1) < ATTN_HEAD_DIM


def _stack_heads(v):
    left = _left_half(v.shape)
    zero = jnp.zeros_like(v)
    return jnp.concatenate([jnp.where(left, v, zero), jnp.where(left, zero, v)], axis=0)


def _unstack_heads(v):
    return jnp.where(_left_half((CHUNK, 128)), v[0:CHUNK, :], v[CHUNK:2 * CHUNK, :])


def _half_mean(v):
    r = lax.broadcasted_iota(jnp.int32, (128, 128), 0) < ATTN_HEAD_DIM
    c = lax.broadcasted_iota(jnp.int32, (128, 128), 1) < ATTN_HEAD_DIM
    return _dot_exact01_r(v, r == c) * (1.0 / ATTN_HEAD_DIM)


def _attn_prepare(q_ref, k_ref, v_ref, gq_ref, gk_ref, qs_scr, k_scr, v_scr):
    q, k = q_ref[...], k_ref[...]
    rq = lax.rsqrt(_half_mean(q * q) + RMS_EPS)
    rk = lax.rsqrt(_half_mean(k * k) + RMS_EPS)
    qhat, khat = q * rq, k * rk
    qs_scr[...] = (qhat * gq_ref[...] * ATTN_HEAD_DIM ** -0.5).astype(BF16)
    k_scr[0:KPAD, :] = jnp.zeros((KPAD, 128), BF16)
    v_scr[0:KPAD, :] = jnp.zeros((KPAD, 128), BF16)
    k_scr[KPAD:, :] = (khat * gk_ref[...]).astype(BF16)
    v_scr[KPAD:, :] = v_ref[...].astype(BF16)
    return qhat, rq, khat, rk


def _first_key(c):
    return jnp.maximum(CHUNK, (LEFT_CHUNKS + 1 - c) * CHUNK)


def _attn_fwd_chunk(c, qs_scr, k_scr, v_scr, bias_ref, o_ref):
    r0 = pl.multiple_of(c * CHUNK, CHUNK)
    s = lax.dot_general(_stack_heads(qs_scr[pl.ds(r0, CHUNK), :]), k_scr[pl.ds(r0, BAND), :], NT,
                        preferred_element_type=F32)
    yield
    col = lax.broadcasted_iota(jnp.int32, (2 * CHUNK, BAND), 1)
    s = jnp.where(col >= _first_key(c), s + bias_ref[...], -jnp.inf)
    m = jnp.max(s, axis=-1, keepdims=True)
    yield
    e = jnp.exp(s - m)
    yield
    inv = 1.0 / jnp.sum(e, axis=-1, keepdims=True)
    o = lax.dot_general(e.astype(BF16), v_scr[pl.ds(r0, BAND), :], NN, preferred_element_type=F32)
    yield
    o_ref[pl.ds(r0, CHUNK), :] = _unstack_heads(o * inv)


def _attn_bwd(proj, out, dout, bias, gq, gk, nb, seq):
    nc = seq // CHUNK
    lock = min(ATTN_LOCKSTEP, nc)
    assert nc % lock == 0
    scale = ATTN_HEAD_DIM ** -0.5

    def body(q_ref, k_ref, v_ref, o_ref, do_ref, bias_ref, gq_ref, gk_ref,
             dq_ref, dk_ref, dv_ref, dbias_ref, dgq_ref, dgk_ref,
             qs_scr, k_scr, v_scr, dqn_scr, dk_scr, dv_scr, db_scr):
        qhat, rq, khat, rk = _attn_prepare(q_ref, k_ref, v_ref, gq_ref, gk_ref, qs_scr, k_scr, v_scr)
        dk_scr[...] = jnp.zeros_like(dk_scr)
        dv_scr[...] = jnp.zeros_like(dv_scr)
        db_scr[...] = jnp.zeros_like(db_scr)

        def one_chunk(c):
            r0 = pl.multiple_of(c * CHUNK, CHUNK)
            qst = _stack_heads(qs_scr[pl.ds(r0, CHUNK), :])
            kb = k_scr[pl.ds(r0, BAND), :]
            vb = v_scr[pl.ds(r0, BAND), :]
            st = lax.dot_general(kb, qst, NT, preferred_element_type=F32) + bias_ref[...]
            dost = _stack_heads(do_ref[pl.ds(r0, CHUNK), :])
            dost16 = dost.astype(BF16)
            dpt = lax.dot_general(vb, dost16, NT, preferred_element_type=F32)
            yield
            key = lax.broadcasted_iota(jnp.int32, (BAND, 2 * CHUNK), 0)
            st = jnp.where(key >= _first_key(c), st, -jnp.inf)
            mx = jnp.max(st, axis=0, keepdims=True)
            drow = _row_sums_on_lanes(dost * _stack_heads(o_ref[pl.ds(r0, CHUNK), :]))
            yield
            et = jnp.exp(st - mx)
            yield
            pt = et * (1.0 / jnp.sum(et, axis=0, keepdims=True))
            yield
            dst = pt * (dpt - drow)
            dst16 = dst.astype(BF16)
            yield
            db_scr[...] += dst
            dqn_scr[pl.ds(r0, CHUNK), :] = scale * _unstack_heads(_dot(dst.T, kb))
            yield
            dk_scr[pl.ds(r0, BAND), :] += lax.dot_general(dst16, qst, NN, preferred_element_type=F32)
            yield
            dv_scr[pl.ds(r0, BAND), :] += lax.dot_general(pt.astype(BF16), dost16, NN, preferred_element_type=F32)

        def chunk(i, carry):
            _lockstep([one_chunk(i * lock + a) for a in range(lock)])
            return carry

        lax.fori_loop(0, nc // lock, chunk, 0, unroll=max(1, min(ATTN_UNROLL, nc) // lock))

        def norm_bwd(dn, hat, r, g_ref):
            gd = dn * g_ref[...]
            return r * (gd - hat * _half_mean(gd * hat)), jnp.sum(dn * hat, axis=0, keepdims=True)

        dq, dgq = norm_bwd(dqn_scr[...], qhat, rq, gq_ref)
        dk, dgk = norm_bwd(dk_scr[KPAD:, :], khat, rk, gk_ref)
        dq_ref[...] = dq.astype(BF16)
        dk_ref[...] = dk.astype(BF16)
        dv_ref[...] = dv_scr[KPAD:, :].astype(BF16)
        dbias_ref[0] = db_scr[...]
        dgq_ref[0] = dgq
        dgk_ref[0] = dgk

    def col(off):
        return pl.BlockSpec((seq, 128), lambda b, hp: (b, off + hp))

    vec = pl.BlockSpec((1, 128), lambda b, hp: (0, 0))
    gvec = pl.BlockSpec((1, 1, 128), lambda b, hp: (b * (ATTN_HEADS // 2) + hp, 0, 0))
    t = nb * seq
    return _pallas(
        body, name="attn_bwd", grid=(nb, ATTN_HEADS // 2),
        in_specs=[col(0), col(4), col(8), col(0), col(0),
                  pl.BlockSpec((BAND, 2 * CHUNK), lambda b, hp: (hp, 0)), vec, vec],
        out_specs=[col(0), col(0), col(0), pl.BlockSpec((1, BAND, 2 * CHUNK), lambda b, hp: (b, hp, 0)),
                   gvec, gvec],
        out_shape=[jax.ShapeDtypeStruct((t, ATTN_WIDTH), BF16)] * 3
        + [jax.ShapeDtypeStruct((nb, ATTN_HEADS // 2 * BAND, 2 * CHUNK), F32)]
        + [jax.ShapeDtypeStruct((nb * ATTN_HEADS // 2, 1, 128), F32)] * 2,
        scratch_shapes=[pltpu.VMEM((seq, 128), BF16), pltpu.VMEM((seq + KPAD, 128), BF16),
                        pltpu.VMEM((seq + KPAD, 128), BF16), pltpu.VMEM((seq, 128), F32),
                        pltpu.VMEM((seq + KPAD, 128), F32), pltpu.VMEM((seq + KPAD, 128), F32),
                        pltpu.VMEM((BAND, 2 * CHUNK), F32)],
        sem=("parallel", "parallel"), args=(proj, proj, proj, out, dout, bias, gq, gk))


def _tri(lower):
    r = lax.broadcasted_iota(jnp.int32, (CHUNK, CHUNK), 0)
    c = lax.broadcasted_iota(jnp.int32, (CHUNK, CHUNK), 1)
    return (r >= c) if lower else (r <= c)


def _hgrn_gates(hq, hf, lb):
    sq = _sigmoid(hq)
    sf = _sigmoid(hf)
    return hq * sq, sq, sf, lb + (1.0 - lb) * sf


def _hgrn_offdiag(q_s, k_s, b_s):
    row = lax.broadcasted_iota(jnp.int32, (CHUNK, HGRN_HEAD_DIM), 0)
    bv, qv, kv = b_s[...], q_s[...], k_s[...]
    eqs, eks = [], []
    for i in range(1, N_SUB):
        r = b_s[pl.ds(SUB * i - 1, 1), :]
        in_i = (row >= SUB * i) & (row < SUB * (i + 1))
        eqs.append(jnp.exp(jnp.where(in_i, bv - r, -jnp.inf)))
        eks.append(jnp.exp(jnp.where(row < SUB * i, r - bv, -jnp.inf)))
    eq = jnp.concatenate(eqs, axis=1)
    ek = jnp.concatenate(eks, axis=1)
    qt = jnp.concatenate([qv] * (N_SUB - 1), axis=1) * eq
    kt = jnp.concatenate([kv] * (N_SUB - 1), axis=1) * ek
    return qt, kt, eq, ek


def _hgrn_diag_e(b_s, i, s):
    t_io = lax.broadcasted_iota(jnp.int32, (SUB, HGRN_HEAD_DIM), 0)
    bi = b_s[pl.ds(SUB * i, SUB), :]
    return jnp.exp(jnp.where(t_io >= s, bi - b_s[pl.ds(SUB * i + s, 1), :], -jnp.inf)), t_io


def _hgrn_intra(q_s, k_s, b_s, a_s, qt, kt):
    ktp = jnp.concatenate([kt, jnp.zeros_like(kt)], axis=0)
    a_s[...] = _dot(qt, ktp, NT)
    yield
    col = lax.broadcasted_iota(jnp.int32, (SUB, HGRN_HEAD_DIM), 1)
    for i in range(N_SUB):
        qi = q_s[pl.ds(SUB * i, SUB), :]
        ai = jnp.zeros((SUB, HGRN_HEAD_DIM), F32)
        for s in range(SUB):
            e, _ = _hgrn_diag_e(b_s, i, s)
            a_col = jnp.sum(qi * k_s[pl.ds(SUB * i + s, 1), :] * e, axis=-1, keepdims=True)
            ai = ai + jnp.where(col == SUB * i + s, a_col, 0.0)
            if s % DIAG_STAGE == DIAG_STAGE - 1:
                yield
        a_s[pl.ds(SUB * i, SUB), :] += ai


def _mixer_fwd(proj, bias, gq, gk, lb, go, nb, seq):
    nc = seq // CHUNK
    hd = HGRN_HEAD_DIM
    nblk = ATTN_HEADS // 2
    rows_blk = seq // nblk
    nck = rows_blk // CHUNK
    per = nc // nck
    assert rows_blk % CHUNK == 0

    def body(aq_ref, ak_ref, av_ref, bias_ref, gq_ref, gk_ref, hq_ref, hf_ref, hi_ref, hg_ref, lb_ref, go_ref,
             ao_ref, y_ref, o_ref, st_ref, a_ref, qs_scr, k_scr, v_scr, st_all, q_all, k_all, b_all, a_all):
        _attn_prepare(aq_ref, ak_ref, av_ref, gq_ref, gk_ref, qs_scr, k_scr, v_scr)

        @pl.when(pl.program_id(1) == 0)
        def _():
            st_all[...] = jnp.zeros_like(st_all)

        lower = _tri(True)

        def head_chunk(hh, c, rows):
            ln = slice(hd * hh, hd * (hh + 1))
            st, q_s, k_s, b_s, a_s = st_all.at[hh], q_all.at[hh], k_all.at[hh], b_all.at[hh], a_all.at[hh]
            q, _, _, f = _hgrn_gates(hq_ref[rows, ln], hf_ref[rows, ln], lb_ref[:, ln])
            v = hi_ref[rows, ln]
            yield
            b = _dot_exact01(lower, jnp.log(f))
            q_s[...] = q
            k_s[...] = 1.0 - f
            b_s[...] = b
            st_ref[hh, c] = st[...]
            yield
            qt, kt, _, _ = _hgrn_offdiag(q_s, k_s, b_s)
            yield
            yield from _hgrn_intra(q_s, k_s, b_s, a_s, qt, kt)
            a16 = a_s[...].astype(BF16)
            a_ref[hh, c] = a16
            vp = jnp.concatenate([v, jnp.zeros_like(v)], axis=0)
            o = _dot(a16, vp) + _dot(q * jnp.exp(b), st[...], NT)
            yield
            bl = b_s[pl.ds(CHUNK - 1, 1), :]
            st[...] = st[...] * jnp.exp(bl) + _tn(v, (1.0 - f) * jnp.exp(bl - b))
            o_ref[rows, ln] = o
            yield
            n = o * lax.rsqrt(jnp.mean(o * o, axis=-1, keepdims=True) + RMS_EPS) * go_ref[...]
            hg = hg_ref[rows, ln]
            y_ref[rows, ln] = n * hg * _sigmoid(hg)

        def chunk(c, carry):
            rows = pl.ds(pl.multiple_of(c * CHUNK, CHUNK), CHUNK)
            _lockstep([_attn_fwd_chunk(c * per + a, qs_scr, k_scr, v_scr, bias_ref, ao_ref) for a in range(per)]
                      + [head_chunk(hh, c, rows) for hh in range(HGRN_HEADS)])
            return carry

        lax.fori_loop(0, nck, chunk, 0)

    hp, wide = HGRN_HEADS, HGRN_HEADS * hd

    def acol(off):
        return pl.BlockSpec((seq, 128), lambda b, s: (b, off + s))

    def col(off):
        return pl.BlockSpec((rows_blk, wide), lambda b, s: (b * nblk + s, off // hp))

    out = pl.BlockSpec((rows_blk, wide), lambda b, s: (b * nblk + s, 0))
    vec = pl.BlockSpec((1, 128), lambda b, s: (0, 0))
    t = nb * seq
    return _pallas(
        body, name="mixer_fwd", grid=(nb, nblk),
        in_specs=[acol(0), acol(4), acol(8), pl.BlockSpec((2 * CHUNK, BAND), lambda b, s: (s, 0)), vec, vec,
                  col(12), col(16), col(20), col(24), pl.BlockSpec((1, wide), lambda b, s: (0, 0)), vec],
        out_specs=[pl.BlockSpec((seq, 128), lambda b, s: (b, s)), out, out,
                   pl.BlockSpec((hp, nck, hd, hd), lambda b, s: (b, s, 0, 0)),
                   pl.BlockSpec((hp, nck, CHUNK, hd), lambda b, s: (b, s, 0, 0))],
        out_shape=[jax.ShapeDtypeStruct((t, ATTN_WIDTH), F32)] + [jax.ShapeDtypeStruct((t, wide), F32)] * 2
        + [jax.ShapeDtypeStruct((nb * hp, nc, hd, hd), F32), jax.ShapeDtypeStruct((nb * hp, nc, CHUNK, hd), BF16)],
        scratch_shapes=[pltpu.VMEM((seq, 128), BF16), pltpu.VMEM((seq + KPAD, 128), BF16),
                        pltpu.VMEM((seq + KPAD, 128), BF16), pltpu.VMEM((hp, hd, hd), F32)]
        + [pltpu.VMEM((hp, CHUNK, hd), F32)] * 4,
        sem=("parallel", "arbitrary"), args=(proj,) * 3 + (bias, gq, gk) + (proj,) * 4 + (lb, go))


def _hgrn_bwd(proj, lb, go, o_pre, states, scores, dout, nb, seq):
    nc = seq // CHUNK
    hd = HGRN_HEAD_DIM
    rows_blk = min(HGRN_ROWS, seq)
    nblk, nck = seq // rows_blk, rows_blk // CHUNK

    def body(hq_ref, hf_ref, hi_ref, hg_ref, lb_ref, go_ref, o_ref, st_ref, a_ref, dy_ref,
             dhq_ref, dhf_ref, dhi_ref, dhg_ref, dlb_ref, dgo_ref,
             dst_all, q_all, k_all, b_all, da_all, dqi_all, dki_all, dlb_all, dgo_all):
        @pl.when(pl.program_id(1) == 0)
        def _():
            dst_all[...] = jnp.zeros_like(dst_all)
            dlb_all[...] = jnp.zeros_like(dlb_all)
            dgo_all[...] = jnp.zeros_like(dgo_all)

        lower, upper = _tri(True), _tri(False)
        gov = go_ref[...]
        row = lax.broadcasted_iota(jnp.int32, (CHUNK, hd), 0)

        def head_chunk(hh, c, rows):
            ln = slice(hd * hh, hd * (hh + 1))
            dst, q_s, k_s, b_s = dst_all.at[hh], q_all.at[hh], k_all.at[hh], b_all.at[hh]
            da_s, dqi_s, dki_s = da_all.at[hh], dqi_all.at[hh], dki_all.at[hh]
            dlb_acc, dgo_acc = dlb_all.at[hh], dgo_all.at[hh]
            lbv = lb_ref[:, ln]
            hq, hf, v, hg = hq_ref[rows, ln], hf_ref[rows, ln], hi_ref[rows, ln], hg_ref[rows, ln]
            q, sq, sf, f = _hgrn_gates(hq, hf, lbv)
            kk = 1.0 - f
            yield
            b = _dot_exact01(lower, jnp.log(f))
            q_s[...] = q
            k_s[...] = kk
            b_s[...] = b
            yield
            bl = b_s[pl.ds(CHUNK - 1, 1), :]
            ebl = jnp.exp(bl)
            ekd = jnp.exp(bl - b)
            kd = kk * ekd
            eb = jnp.exp(b)
            qb = q * eb
            st0 = st_ref[hh, c]
            dst1 = dst[...]
            yield

            o = o_ref[rows, ln]
            dy = dy_ref[rows, ln]
            sg = _sigmoid(hg)
            rstd = lax.rsqrt(jnp.mean(o * o, axis=-1, keepdims=True) + RMS_EPS)
            ohat = o * rstd
            dn = dy * hg * sg
            dhg_ref[rows, ln] = (dy * ohat * gov * (sg * (1.0 + hg * (1.0 - sg)))).astype(BF16)
            dgo_acc[...] += jnp.sum(dn * ohat, axis=0, keepdims=True)
            gdn = dn * gov
            do = rstd * (gdn - ohat * jnp.mean(gdn * ohat, axis=-1, keepdims=True))
            yield

            qt, kt, eq, ek = _hgrn_offdiag(q_s, k_s, b_s)
            da = _dot(do, v, NT)
            dat = _dot(v, do, NT)
            da_s[...] = da
            yield
            dqo = _dot(da, kt) * eq
            dko = _dot(dat, qt) * ek
            dqi_s[...] = sum(dqo[:, j * hd:(j + 1) * hd] for j in range(N_SUB - 1))
            dki_s[...] = sum(dko[:, j * hd:(j + 1) * hd] for j in range(N_SUB - 1))
            yield
            col = lax.broadcasted_iota(jnp.int32, (SUB, CHUNK), 1)
            for i in range(N_SUB):
                qi = q_s[pl.ds(SUB * i, SUB), :]
                dai = da_s[pl.ds(SUB * i, SUB), :]
                dqd = jnp.zeros((SUB, hd), F32)
                for s in range(SUB):
                    e, _ = _hgrn_diag_e(b_s, i, s)
                    dacol = jnp.sum(jnp.where(col == SUB * i + s, dai, 0.0), axis=-1, keepdims=True)
                    w = dacol * e
                    dqd = dqd + w * k_s[pl.ds(SUB * i + s, 1), :]
                    dki_s[pl.ds(SUB * i + s, 1), :] += jnp.sum(w * qi, axis=0, keepdims=True)
                    if s % DIAG_STAGE == DIAG_STAGE - 1:
                        yield
                dqi_s[pl.ds(SUB * i, SUB), :] += dqd
            dqi, dki = dqi_s[...], dki_s[...]

            dv = _tn(a_ref[hh, c].astype(F32), do)[0:CHUNK, :] + _dot(kd, dst1, NT)
            dqb = _dot(do, st0)
            dkd = _dot(v, dst1)
            yield
            t2 = dkd * kd
            dq = dqb * eb + dqi
            dk = dkd * ekd + dki
            dbl = jnp.sum(t2, axis=0, keepdims=True) + ebl * jnp.sum(st0 * dst1, axis=0, keepdims=True)
            db = dqb * qb - t2 + q * dqi - kk * dki + jnp.where(row == CHUNK - 1, dbl, 0.0)
            yield
            dg = _dot_exact01(upper, db)
            dst[...] = dst1 * ebl + _tn(do, qb)
            yield

            df = dg / f - dk
            dhf_ref[rows, ln] = (df * (1.0 - lbv) * sf * (1.0 - sf)).astype(BF16)
            dlb_acc[...] += jnp.sum(df * (1.0 - sf), axis=0, keepdims=True)
            dhq_ref[rows, ln] = (dq * (sq * (1.0 + hq * (1.0 - sq)))).astype(BF16)
            dhi_ref[rows, ln] = dv.astype(BF16)

        def chunk(it, carry):
            c = nck - 1 - it
            rows = pl.ds(pl.multiple_of(c * CHUNK, CHUNK), CHUNK)
            _lockstep([head_chunk(hh, c, rows) for hh in range(HGRN_HEADS)])
            return carry

        lax.fori_loop(0, nck, chunk, 0)

        @pl.when(pl.program_id(1) == nblk - 1)
        def _():
            dlb_ref[...] = dlb_all[...]
            dgo_ref[...] = dgo_all[...]

    hp, wide = HGRN_HEADS, HGRN_HEADS * hd

    def col(off):
        return pl.BlockSpec((rows_blk, wide), lambda b, s: (b * nblk + nblk - 1 - s, off // hp))

    out = pl.BlockSpec((rows_blk, wide), lambda b, s: (b * nblk + nblk - 1 - s, 0))
    part = pl.BlockSpec((hp, 1, hd), lambda b, s: (b, 0, 0))
    t = nb * seq
    return pl.pallas_call(
        body, name="hgrn_bwd", grid=(nb, nblk),
        in_specs=[col(12), col(16), col(20), col(24), pl.BlockSpec((1, wide), lambda b, s: (0, 0)),
                  pl.BlockSpec((1, hd), lambda b, s: (0, 0)), out,
                  pl.BlockSpec((hp, nck, hd, hd), lambda b, s: (b, nblk - 1 - s, 0, 0)),
                  pl.BlockSpec((hp, nck, CHUNK, hd), lambda b, s: (b, nblk - 1 - s, 0, 0)), col(4)],
        out_specs=[out, out, out, out, part, part],
        out_shape=[jax.ShapeDtypeStruct((t, wide), BF16)] * 4 + [jax.ShapeDtypeStruct((nb * hp, 1, hd), F32)] * 2,
        scratch_shapes=[pltpu.VMEM((hp, hd, hd), F32)] + [pltpu.VMEM((hp, CHUNK, hd), F32)] * 3
        + [pltpu.VMEM((hp, CHUNK, CHUNK), F32)] + [pltpu.VMEM((hp, CHUNK, hd), F32)] * 2
        + [pltpu.VMEM((hp, 1, hd), F32)] * 2,
        compiler_params=_params("parallel", "arbitrary"),
    )(proj, proj, proj, proj, lb, go, o_pre, states, scores, dout)


def _lb_fwd(lower_bounds):
    def body(x_ref, o_ref):
        xv = x_ref[...]
        e = jnp.exp(xv - jnp.max(xv, axis=0, keepdims=True))
        o_ref[...] = e[0:1, :] / jnp.sum(e, axis=0, keepdims=True)

    return pl.pallas_call(body, name="lb_fwd",
                          out_shape=jax.ShapeDtypeStruct((1, lower_bounds.shape[1]), F32))(lower_bounds)


def _lb_bwd(lower_bounds, dlb_parts):
    ng = dlb_parts.shape[0]

    def body(x_ref, d_ref, o_ref):
        xv = x_ref[...]
        e = jnp.exp(xv - jnp.max(xv, axis=0, keepdims=True))
        p = e / jnp.sum(e, axis=0, keepdims=True)
        dlb = d_ref[0]
        for gi in range(1, ng):
            dlb = dlb + d_ref[gi]
        first = lax.broadcasted_iota(jnp.int32, xv.shape, 0) == 0
        o_ref[...] = p * (jnp.where(first, dlb, 0.0) - p[0:1, :] * dlb)

    return pl.pallas_call(body, name="lb_bwd",
                          out_shape=jax.ShapeDtypeStruct(lower_bounds.shape, F32))(lower_bounds, dlb_parts)


def _ffn_bwd(x, g, h, gate, up, dy, dy16, w, put, tag):
    wg, wu, wd = w[tag + "_w_gate"], w[tag + "_w_up"], w[tag + "_w_down"]
    dgate, dup, dwd = _ffn_bwd_mid(dy16, wd, gate, up, tag + "_bwd_mid")
    put(tag + "_w_down", dwd)
    put(tag + "_w_gate", _mm(dgate, h, ta=True, tm=1408, tn=512, name=tag + "_dwg"))
    put(tag + "_w_up", _mm(dup, h, ta=True, tm=1408, tn=512, name=tag + "_dwu"))
    dh = _mm(dgate, wg, tm=512, tn=1024, name=tag + "_dh_gate")
    return _mm(dup, wu, tm=512, tn=1024, add=dh, norm_bwd=(x, g, dy), name=tag + "_dh_up")


def _local_step(x, tgt, sp, w, put, nb, seq):
    d = x.shape[1]
    h1 = _rms_fwd(x, sp["ffn1_norm_g"], "ffn1_norm")
    rb_pad = jnp.pad(sp["attn_rel_bias"], ((0, 0), (0, N_REL_PAD - N_REL)))
    bias = jnp.transpose(_bias_expand(rb_pad), (1, 0, 2)).reshape(ATTN_HEADS * CHUNK, BAND)
    gq2 = jnp.concatenate([sp["attn_q_norm_g"]] * 2, axis=1)
    gk2 = jnp.concatenate([sp["attn_k_norm_g"]] * 2, axis=1)
    lb = _lb_fwd(sp["hgrn_lower_bounds"])
    gate1, up1, act1 = _ffn_up(h1, w["ffn1_w_gate"], w["ffn1_w_up"], "ffn1_up")
    x1, h2 = _mm(act1, w["ffn1_w_down"], tm=512, tn=d, add=x, scale=0.5, norm_g=sp["mix_norm_g"],
                 name="ffn1_down")
    proj = _mm(h2, w["w_in"], tb=True, tm=256, tn=w["w_in"].shape[0], name="in_proj")
    attn, hy, ho, hstate, hscore = _mixer_fwd(proj, bias, gq2, gk2, lb, sp["hgrn_out_norm_g"], nb, seq)
    mix = jnp.concatenate([attn, hy], axis=1)
    x2, h3 = _mm(mix, w["w_out"], tm=512, tn=1024, add=x1, norm_g=sp["ffn2_norm_g"], name="out_proj")
    gate2, up2, dx3, dx3_16, sq = _ffn_fwd(h3, x2, w["ffn2_w_gate"], w["ffn2_w_up"], w["ffn2_w_down"], "ffn2_fwd",
                                           tgt=tgt)
    loss = 0.5 * jnp.sum(sq) / d

    dx2, dx2_16, dg3 = _ffn_bwd(x2, sp["ffn2_norm_g"], h3, gate2, up2, dx3, dx3_16, w, put, "ffn2")
    dmix = _mm(dx2_16, w["w_out"], tb=True, tm=512, tn=1024, name="out_proj_dx")
    put("w_out", _mm(mix, dx2_16, ta=True, tm=512, tn=1024, name="out_proj_dw"))
    bias_t = jnp.transpose(bias.reshape(ATTN_HEADS // 2, 2 * CHUNK, BAND), (0, 2, 1)).reshape(-1, 2 * CHUNK)
    dq, dk, dv, dbias, dgq, dgk = _attn_bwd(proj, attn, dmix, bias_t, gq2, gk2, nb, seq)
    dbias = jnp.transpose(dbias.reshape(nb, ATTN_HEADS // 2, BAND, 2, CHUNK), (0, 4, 1, 3, 2))
    dbias = dbias.reshape(nb, CHUNK, ATTN_HEADS, BAND)
    dgq = jnp.sum(dgq, axis=(0, 1)).reshape(2, ATTN_HEAD_DIM).sum(axis=0, keepdims=True)
    dgk = jnp.sum(dgk, axis=(0, 1)).reshape(2, ATTN_HEAD_DIM).sum(axis=0, keepdims=True)
    dhq, dhf, dhi, dhg, dlb, dgo = _hgrn_bwd(proj, lb, sp["hgrn_out_norm_g"], ho, hstate, hscore, dmix, nb, seq)
    dproj = jnp.concatenate([dq, dk, dv, dhq, dhf, dhi, dhg], axis=1)
    put("w_in", _mm(dproj, h2, ta=True, tm=512, tn=1024, name="in_proj_dw"))
    dx1, dx1_16, dgm = _mm(dproj, w["w_in"], tm=512, tn=1024, norm_bwd=(x1, sp["mix_norm_g"], dx2),
                           name="in_proj_dx")
    dx0, _, dg1 = _ffn_bwd(x, sp["ffn1_norm_g"], h1, gate1, up1, dx1, dx1_16, w, put, "ffn1")

    small = {
        "ffn1_norm_g": dg1, "mix_norm_g": dgm, "ffn2_norm_g": dg3,
        "attn_q_norm_g": dgq, "attn_k_norm_g": dgk,
        "attn_rel_bias": _bias_fold(dbias)[:, :N_REL],
        "hgrn_lower_bounds": _lb_bwd(sp["hgrn_lower_bounds"], dlb.reshape(nb, 1, HGRN_HEADS * HGRN_HEAD_DIM)),
        "hgrn_out_norm_g": jnp.sum(dgo, axis=(0, 1))[None, :],
    }
    return loss, dx0, small


MESH = pl.DeviceIdType.MESH
ANY = pl.BlockSpec(memory_space=pl.ANY)


def _coords():
    return lax.axis_index("x"), lax.axis_index("y"), lax.axis_index("c")


def _other_chips(x, y):
    return [(1 - x, y), (x, 1 - y), (1 - x, 1 - y)]


def _gather_side(shards):
    n = len(shards)

    def copies(ins, outs, sems):
        send_sems, recv_sems, local_sems = sems
        x, y, c = _coords()
        me, sibling = (x, y, c), (x, y, 1 - c)
        chips = _other_chips(x, y)

        def copy(i, k, block, to, src=None):
            bx, by, bc = block
            dst = outs[i].at[4 * bx + 2 * by + bc]
            return pltpu.make_async_remote_copy(
                src_ref=dst if src is None else src, dst_ref=dst, send_sem=send_sems.at[i, k],
                recv_sem=recv_sems.at[i, k], device_id=to, device_id_type=MESH)

        mine = [pltpu.make_async_copy(ins[i], outs[i].at[4 * x + 2 * y + c], local_sems.at[i]) for i in range(n)]
        own = []
        for i in range(n):
            own.append(copy(i, 0, me, sibling, src=ins[i]))
            own += [copy(i, 1 + j, me, (*chip, c), src=ins[i]) for j, chip in enumerate(chips)]
        return copy, mine, own, me, sibling, chips, c

    def start(ins, outs, sems):
        _, mine, own, *_ = copies(ins, outs, sems)
        for cp in mine + own:
            cp.start()

    def finish(ins, outs, sems):
        copy, mine, own, me, sibling, chips, c = copies(ins, outs, sems)
        passed = []
        for i in range(n):
            for j, chip in enumerate(chips):
                copy(i, 1 + j, (*chip, c), me).wait_recv()
                passed.append(copy(i, 4 + j, (*chip, c), sibling))
                passed[-1].start()
        for i in range(n):
            copy(i, 0, sibling, me).wait_recv()
            for j, chip in enumerate(chips):
                copy(i, 4 + j, (*chip, 1 - c), me).wait_recv()
        for cp in own + passed:
            cp.wait_send()
        for cp in mine:
            cp.wait()

    return _Side(list(shards), [jax.ShapeDtypeStruct((N_DEV,) + s.shape, s.dtype) for s in shards],
                 [pltpu.SemaphoreType.DMA((n, 7)), pltpu.SemaphoreType.DMA((n, 7)), pltpu.SemaphoreType.DMA((n,))],
                 start, finish)


def _pair_side(grads):
    n = len(grads)

    def copies(ins, outs, sems):
        send_sems, recv_sems = sems
        x, y, c = _coords()
        return [pltpu.make_async_remote_copy(
            src_ref=ins[i].at[2 * k + 1 - c], dst_ref=outs[i].at[k], send_sem=send_sems.at[i, k],
            recv_sem=recv_sems.at[i, k], device_id=(x, y, 1 - c), device_id_type=MESH)
            for i in range(n) for k in range(4)]

    def start(ins, outs, sems):
        for cp in copies(ins, outs, sems):
            cp.start()

    def finish(ins, outs, sems):
        for cp in copies(ins, outs, sems):
            cp.wait()

    return _Side(list(grads), [jax.ShapeDtypeStruct((4,) + g.shape[1:], g.dtype) for g in grads],
                 [pltpu.SemaphoreType.DMA((n, 4)), pltpu.SemaphoreType.DMA((n, 4))], start, finish)


def _pair_add(grad, recv, core, name):
    _, r, cdim = grad.shape

    def body(c_ref, g_ref, r_ref, o_ref):
        o_ref[...] = (g_ref[...] + r_ref[...]).astype(BF16)

    blk = (1, r, cdim)
    return pl.pallas_call(
        body, name=name,
        grid_spec=pltpu.PrefetchScalarGridSpec(
            num_scalar_prefetch=1, grid=(4,),
            in_specs=[pl.BlockSpec(blk, lambda k, c_ref: (2 * k + c_ref[0], 0, 0)),
                      pl.BlockSpec(blk, lambda k, c_ref: (k, 0, 0))],
            out_specs=pl.BlockSpec(blk, lambda k, c_ref: (k, 0, 0))),
        out_shape=jax.ShapeDtypeStruct((4, r, cdim), BF16),
        compiler_params=_params("arbitrary"),
    )(core, grad, recv)


def _chip_side(parts):
    n = len(parts)

    def copies(ins, outs, sems):
        send_sems, recv_sems, local_sems = sems
        x, y, c = _coords()
        chips = _other_chips(x, y)
        mine = [pltpu.make_async_copy(ins[i].at[2 * x + y], outs[i].at[2 * x + y], local_sems.at[i])
                for i in range(n)]
        sent = [pltpu.make_async_remote_copy(
            src_ref=ins[i].at[2 * px + py], dst_ref=outs[i].at[2 * x + y], send_sem=send_sems.at[i, j],
            recv_sem=recv_sems.at[i, j], device_id=(px, py, c), device_id_type=MESH)
            for i in range(n) for j, (px, py) in enumerate(chips)]
        return mine, sent, chips, c

    def start(ins, outs, sems):
        mine, sent, _, _ = copies(ins, outs, sems)
        for cp in mine + sent:
            cp.start()

    def finish(ins, outs, sems):
        mine, sent, chips, c = copies(ins, outs, sems)
        send_sems, recv_sems, _ = sems
        for i in range(n):
            for j, (px, py) in enumerate(chips):
                landed = outs[i].at[2 * px + py]
                pltpu.make_async_remote_copy(
                    src_ref=landed, dst_ref=landed, send_sem=send_sems.at[i, j], recv_sem=recv_sems.at[i, j],
                    device_id=(px, py, c), device_id_type=MESH).wait_recv()
        for cp in sent:
            cp.wait_send()
        for cp in mine:
            cp.wait()

    return _Side(list(parts), [jax.ShapeDtypeStruct(p.shape, p.dtype) for p in parts],
                 [pltpu.SemaphoreType.DMA((n, 3)), pltpu.SemaphoreType.DMA((n, 3)), pltpu.SemaphoreType.DMA((n,))],
                 start, finish)


def _all_reduce_small(v):
    r = v.shape[0]

    def body(v_ref, o_ref, buf, send_sems, recv_sems):
        x, y, c = _coords()
        me = 4 * x + 2 * y + c
        buf[me] = v_ref[...]
        cps = []
        for k in range(1, N_DEV):
            px = 1 - x if k & 4 else x
            py = 1 - y if k & 2 else y
            pc = 1 - c if k & 1 else c
            cps.append((pltpu.make_async_remote_copy(
                src_ref=v_ref, dst_ref=buf.at[me], send_sem=send_sems.at[k - 1], recv_sem=recv_sems.at[k - 1],
                device_id=(px, py, pc), device_id_type=MESH), 4 * px + 2 * py + pc))
        for cp, _ in cps:
            cp.start()
        for k, (cp, peer) in enumerate(cps):
            pltpu.make_async_remote_copy(
                src_ref=v_ref, dst_ref=buf.at[peer], send_sem=send_sems.at[k], recv_sem=recv_sems.at[k],
                device_id=(x, y, c), device_id_type=MESH).wait_recv()
        for cp, _ in cps:
            cp.wait_send()
        acc = buf[0]
        for j in range(1, N_DEV):
            acc = acc + buf[j]
        o_ref[...] = acc

    return pl.pallas_call(
        body, name="small_all_reduce", out_shape=jax.ShapeDtypeStruct(v.shape, F32),
        in_specs=[pl.BlockSpec(memory_space=pltpu.VMEM)], out_specs=pl.BlockSpec(memory_space=pltpu.VMEM),
        scratch_shapes=[pltpu.VMEM((N_DEV, r, 128), F32), pltpu.SemaphoreType.DMA((N_DEV - 1,)),
                        pltpu.SemaphoreType.DMA((N_DEV - 1,))],
    )(v)


def _adamw(w, m, v, g, name):
    parts = w.ndim == 3
    r, cdim = w.shape[-2:]
    tr = r // 4 if r % 32 == 0 else r

    def body(w_ref, m_ref, v_ref, g_ref, go_ref, d_ref, mo_ref, vo_ref):
        if parts:
            gv = g_ref[0].astype(F32)
            for k in range(1, 4):
                gv = gv + g_ref[k].astype(F32)
            gv = gv[None]
        else:
            gv = g_ref[...]
        m2 = ADAM_B1 * m_ref[...] + (1.0 - ADAM_B1) * gv
        v2 = ADAM_B2 * v_ref[...] + (1.0 - ADAM_B2) * (gv * gv)
        m_hat = m2 / (1.0 - ADAM_B1 ** ADAM_STEP)
        v_hat = v2 / (1.0 - ADAM_B2 ** ADAM_STEP)
        go_ref[...] = gv
        d_ref[...] = -ADAM_LR * (m_hat / (jnp.sqrt(v_hat) + ADAM_EPS) + ADAM_WD * w_ref[...])
        mo_ref[...] = m2
        vo_ref[...] = v2

    if parts:
        row = pl.BlockSpec((1, tr, cdim), lambda i: (0, i, 0))
        g_spec = pl.BlockSpec((4, tr, cdim), lambda i: (0, i, 0))
    else:
        row = g_spec = pl.BlockSpec((tr, cdim), lambda i: (i, 0))
    return pl.pallas_call(
        body, name=name, grid=(r // tr,), in_specs=[row, row, row, g_spec], out_specs=[row] * 4,
        out_shape=[jax.ShapeDtypeStruct(w.shape, F32)] * 4,
        compiler_params=_params("parallel"),
    )(w, m, v, g)


WEIGHTS = ["ffn1_norm_g", "ffn1_w_gate", "ffn1_w_up", "ffn1_w_down", "mix_norm_g", "w_in", "attn_q_norm_g",
           "attn_k_norm_g", "attn_rel_bias", "hgrn_lower_bounds", "hgrn_out_norm_g", "w_out", "ffn2_norm_g",
           "ffn2_w_gate", "ffn2_w_up", "ffn2_w_down"]
COL_SHARDED = ("ffn1_w_gate", "ffn1_w_up", "w_in", "ffn2_w_gate", "ffn2_w_up")
ROW_SHARDED = ("ffn1_w_down", "w_out", "ffn2_w_down")
BIG = [n for n in WEIGHTS if n in COL_SHARDED or n in ROW_SHARDED]
SMALL = [n for n in WEIGHTS if n not in BIG]
PACK_ROWS = 8
FFN2 = ["ffn2_w_down", "ffn2_w_gate", "ffn2_w_up"]
MIXER = ["w_out", "w_in"]

PLAN = {
    "ffn1_norm": [("gather", ["ffn1_w_gate"])],
    "bias_expand": [("gather", ["ffn1_w_up"])],
    "ffn1_up": [("gather", ["ffn1_w_down", "w_out"])],
    "ffn1_down": [("gather", ["w_in"])],
    "mixer_fwd": [("gather", FFN2)],
    "ffn2_dh_gate": [("pair", FFN2)],
    "attn_bwd": [("chip", FFN2)],
    "in_proj_dx": [("pair", MIXER)],
    "ffn1_bwd_mid": [("chip", MIXER)],
    "ffn1_dwg": [("pair", ["ffn1_w_down"])],
    "ffn1_dwu": [("chip", ["ffn1_w_down"]), ("pair", ["ffn1_w_gate"])],
    "ffn1_dh_gate": [("chip", ["ffn1_w_gate"]), ("pair", ["ffn1_w_up"])],
    "bias_fold": [("chip", ["ffn1_w_up"])],
}


def _join_sides(sides):
    def split(refs, counts):
        out, at = [], 0
        for n in counts:
            out.append(refs[at:at + n])
            at += n
        return out

    n_in, n_out, n_sem = ([len(getattr(s, f)) for s in sides] for f in ("ins", "out_shape", "sems"))

    def run(which):
        def go(ins, outs, sems):
            for s, i, o, m in zip(sides, split(ins, n_in), split(outs, n_out), split(sems, n_sem)):
                getattr(s, which)(i, o, m)
        return go

    return _Side([a for s in sides for a in s.ins], [a for s in sides for a in s.out_shape],
                 [a for s in sides for a in s.sems], run("start"), run("finish"))


class _Schedule:
    def __init__(self, shards):
        self.shards = shards
        self.weights = {}
        self.sliced = {}
        self.partials = {}
        self.reduced = {}

    def put(self, name, grad):
        self.sliced[name] = grad.reshape((N_DEV,) + self.shards[name].shape)

    def side_for(self, call):
        if call not in PLAN:
            return None
        sides = []
        for kind, names in PLAN[call]:
            if kind == "gather":
                sides.append(_gather_side([self.shards[n] for n in names]))
            elif kind == "pair":
                sides.append(_pair_side([self.sliced[n] for n in names]))
            else:
                sides.append(_chip_side([self.partials[n] for n in names]))
        return _join_sides(sides)

    def done(self, call, outs):
        at = 0
        for kind, names in PLAN[call]:
            self.file(kind, names, outs[at:at + len(names)])
            at += len(names)

    def file(self, kind, names, outs):
        for n, o in zip(names, outs):
            if kind == "gather":
                self.weights[n] = o.reshape(N_DEV * o.shape[1], o.shape[2])
            elif kind == "pair":
                core = lax.axis_index("c").astype(jnp.int32).reshape(1)
                self.partials[n] = _pair_add(self.sliced[n], o, core, n + "_pair_add")
            else:
                self.reduced[n] = o


def _pack_small(vals, loss=None):
    parts = []
    for n in SMALL:
        a = vals[n]
        if n == "attn_rel_bias":
            a = jnp.pad(a.reshape(ATTN_HEADS, N_REL), ((0, 0), (0, N_REL_PAD - N_REL)))
        flat = a.reshape(-1)
        size = -(-flat.shape[0] // (PACK_ROWS * 128)) * PACK_ROWS * 128
        parts.append(jnp.pad(flat, (0, size - flat.shape[0])).reshape(-1, 128))
    tail = jnp.zeros((PACK_ROWS, 128), F32)
    if loss is not None:
        tail = tail.at[0, 0].set(loss)
    return jnp.concatenate(parts + [tail], axis=0)


def _unpack_small(packed, shapes):
    out, row = {}, 0
    for n in SMALL:
        shape = shapes[n]
        if n == "attn_rel_bias":
            rows = ATTN_HEADS * N_REL_PAD // 128
            out[n] = packed[row:row + rows].reshape(ATTN_HEADS, N_REL_PAD)[:, :N_REL].reshape(shape)
        else:
            size = 1
            for s in shape:
                size *= s
            rows = -(-size // (PACK_ROWS * 128)) * PACK_ROWS
            out[n] = packed[row:row + rows].reshape(-1)[:size].reshape(shape)
        row += rows
    return out, packed[row, 0]


def kernel(x, ffn1_norm_g, ffn1_w_gate, ffn1_w_up, ffn1_w_down, mix_norm_g, w_in, attn_q_norm_g, attn_k_norm_g, attn_rel_bias, hgrn_lower_bounds, hgrn_out_norm_g, w_out, ffn2_norm_g, ffn2_w_gate, ffn2_w_up, ffn2_w_down, loss_target, m_ffn1_norm_g, m_ffn1_w_gate, m_ffn1_w_up, m_ffn1_w_down, m_mix_norm_g, m_w_in, m_attn_q_norm_g, m_attn_k_norm_g, m_attn_rel_bias, m_hgrn_lower_bounds, m_hgrn_out_norm_g, m_w_out, m_ffn2_norm_g, m_ffn2_w_gate, m_ffn2_w_up, m_ffn2_w_down, v_ffn1_norm_g, v_ffn1_w_gate, v_ffn1_w_up, v_ffn1_w_down, v_mix_norm_g, v_w_in, v_attn_q_norm_g, v_attn_k_norm_g, v_attn_rel_bias, v_hgrn_lower_bounds, v_hgrn_out_norm_g, v_w_out, v_ffn2_norm_g, v_ffn2_w_gate, v_ffn2_w_up, v_ffn2_w_down):
    wts = dict(zip(WEIGHTS, (ffn1_norm_g, ffn1_w_gate, ffn1_w_up, ffn1_w_down, mix_norm_g, w_in, attn_q_norm_g,
                             attn_k_norm_g, attn_rel_bias, hgrn_lower_bounds, hgrn_out_norm_g, w_out, ffn2_norm_g,
                             ffn2_w_gate, ffn2_w_up, ffn2_w_down)))
    mom = dict(zip(WEIGHTS, (m_ffn1_norm_g, m_ffn1_w_gate, m_ffn1_w_up, m_ffn1_w_down, m_mix_norm_g, m_w_in,
                             m_attn_q_norm_g, m_attn_k_norm_g, m_attn_rel_bias, m_hgrn_lower_bounds,
                             m_hgrn_out_norm_g, m_w_out, m_ffn2_norm_g, m_ffn2_w_gate, m_ffn2_w_up, m_ffn2_w_down)))
    var = dict(zip(WEIGHTS, (v_ffn1_norm_g, v_ffn1_w_gate, v_ffn1_w_up, v_ffn1_w_down, v_mix_norm_g, v_w_in,
                             v_attn_q_norm_g, v_attn_k_norm_g, v_attn_rel_bias, v_hgrn_lower_bounds,
                             v_hgrn_out_norm_g, v_w_out, v_ffn2_norm_g, v_ffn2_w_gate, v_ffn2_w_up, v_ffn2_w_down)))
    nb, seq, d = x.shape
    shapes = {n: wts[n].shape for n in WEIGHTS}

    def rows_first(a, n):
        return jnp.swapaxes(a, 1, 2) if n in COL_SHARDED else a

    sched = _Schedule({n: rows_first(wts[n], n)[0].astype(BF16) for n in BIG})
    sp = {n: wts[n] for n in SMALL}
    sp["attn_rel_bias"] = wts["attn_rel_bias"][0]
    _ACTIVE[0] = sched
    try:
        loss, dx, dsmall = _local_step(x.reshape(nb * seq, d), loss_target.reshape(nb * seq, d), sp,
                                       sched.weights, sched.put, nb, seq)
    finally:
        _ACTIVE[0] = None
    reduced = sched.reduced

    small_sum = _all_reduce_small(_pack_small(dsmall, loss))
    gsmall, loss_total = _unpack_small(small_sum, shapes)

    grads, deltas, new_m, new_v = {}, {}, {}, {}
    for n in BIG:
        out = _adamw(rows_first(wts[n], n), rows_first(mom[n], n), rows_first(var[n], n), reduced[n], n + "_adamw")
        grads[n], deltas[n], new_m[n], new_v[n] = (rows_first(o, n) for o in out)
    packed = _adamw(_pack_small(wts), _pack_small(mom), _pack_small(var), small_sum, "small_adamw")
    for dst, p in zip((deltas, new_m, new_v), packed[1:]):
        dst.update(_unpack_small(p, shapes)[0])
    grads.update(gsmall)

    return (loss_total, dx.reshape(nb, seq, d), *[grads[n] for n in WEIGHTS], *[deltas[n] for n in WEIGHTS],
            *[new_m[n] for n in WEIGHTS], *[new_v[n] for n in WEIGHTS])
```

```python
import functools

import jax
import jax.numpy as jnp
from jax import lax
from jax.experimental import pallas as pl
from jax.experimental.pallas import tpu as pltpu

F32 = jnp.float32
BF16 = jnp.bfloat16

RMS_EPS = 1e-6
CHUNK = 64
LEFT_CHUNKS = 8
BAND = (LEFT_CHUNKS + 2) * CHUNK
KPAD = BAND - CHUNK
REL_CLIP = 128
N_REL = 2 * REL_CLIP + 1
N_REL_PAD = 384
ATTN_HEADS = 8
ATTN_HEAD_DIM = 64
ATTN_WIDTH = ATTN_HEADS * ATTN_HEAD_DIM
ATTN_LOCKSTEP = 4
ATTN_UNROLL = 8
HGRN_HEADS = 4
HGRN_HEAD_DIM = 128
HGRN_ROWS = 512
SUB = 16
N_SUB = CHUNK // SUB
DIAG_STAGE = 4
N_DEV = 8

ADAM_LR = 0.001
ADAM_B1 = 0.9
ADAM_B2 = 0.999
ADAM_EPS = 1e-08
ADAM_WD = 0.01
ADAM_STEP = 10

VMEM_LIMIT = 56 * 1024 * 1024

NT = (((1,), (1,)), ((), ()))
NN = (((1,), (0,)), ((), ()))


def _params(*sem):
    return pltpu.CompilerParams(dimension_semantics=sem, vmem_limit_bytes=VMEM_LIMIT)


def _sigmoid(v):
    return 0.5 * jnp.tanh(0.5 * v) + 0.5


def _dot(a, b, dims=NN):
    return lax.dot_general(a.astype(BF16), b.astype(BF16), dims, preferred_element_type=F32)


def _dot_exact01(m01, v):
    m = m01.astype(BF16)
    hi = v.astype(BF16)
    r1 = v - hi.astype(F32)
    mid = r1.astype(BF16)
    lo = (r1 - mid.astype(F32)).astype(BF16)
    out = lax.dot_general(m, hi, NN, preferred_element_type=F32)
    out = out + lax.dot_general(m, mid, NN, preferred_element_type=F32)
    return out + lax.dot_general(m, lo, NN, preferred_element_type=F32)


def _dot_exact01_r(v, m01):
    m = m01.astype(BF16)
    hi = v.astype(BF16)
    r1 = v - hi.astype(F32)
    mid = r1.astype(BF16)
    lo = (r1 - mid.astype(F32)).astype(BF16)
    out = lax.dot_general(hi, m, NN, preferred_element_type=F32)
    out = out + lax.dot_general(mid, m, NN, preferred_element_type=F32)
    return out + lax.dot_general(lo, m, NN, preferred_element_type=F32)


def _lockstep(stages):
    live = list(stages)
    while live:
        still = []
        for g in live:
            try:
                next(g)
                still.append(g)
            except StopIteration:
                pass
        live = still


def _row_sums_on_lanes(v):
    ones = jnp.ones((8, v.shape[1]), BF16)
    hi = v.astype(BF16)
    r1 = v - hi.astype(F32)
    mid = r1.astype(BF16)
    lo = (r1 - mid.astype(F32)).astype(BF16)
    out = lax.dot_general(ones, hi, NT, preferred_element_type=F32)
    out = out + lax.dot_general(ones, mid, NT, preferred_element_type=F32)
    return (out + lax.dot_general(ones, lo, NT, preferred_element_type=F32))[0:1, :]


def _tn(a, b):
    ap = jnp.concatenate([a, jnp.zeros_like(a)], axis=0)
    bp = jnp.concatenate([b, jnp.zeros_like(b)], axis=0)
    return _dot(ap.T, bp)


def _row_tile(t):
    for tm in (512, 256, 128, 64, 32, 16, 8):
        if t % tm == 0:
            return tm
    raise ValueError(t)


class _Side:
    def __init__(self, ins, out_shape, sems, start, finish, middle=None):
        self.ins, self.out_shape, self.sems = ins, out_shape, sems
        self.start, self.middle, self.finish = start, middle, finish


_ACTIVE = [None]


def _pallas(body, *, name, grid, in_specs, out_specs, out_shape, scratch_shapes=(), sem, args):
    sched = _ACTIVE[0]
    side = sched.side_for(name) if sched is not None else None
    if side is None:
        return pl.pallas_call(
            body, name=name, grid=grid, in_specs=list(in_specs), out_specs=list(out_specs),
            out_shape=list(out_shape), scratch_shapes=list(scratch_shapes), compiler_params=_params(*sem))(*args)
    cuts = [len(in_specs), len(side.ins), len(out_shape), len(side.out_shape), len(scratch_shapes)]

    def with_side(*refs):
        groups, at = [], 0
        for n in cuts:
            groups.append(refs[at:at + n])
            at += n
        ins, side_ins, outs, side_outs, scratch = groups
        side_sems = refs[at:]
        step, total = pl.program_id(0), grid[0]
        for a in range(1, len(grid)):
            step, total = step * grid[a] + pl.program_id(a), total * grid[a]
        has_middle = side.middle is not None and total >= 3

        @pl.when(step == 0)
        def _():
            side.start(side_ins, side_outs, side_sems)

        if has_middle:
            @pl.when(step == total // 2)
            def _():
                side.middle(side_ins, side_outs, side_sems)

        body(*ins, *outs, *scratch)

        @pl.when(step == total - 1)
        def _():
            if side.middle is not None and not has_middle:
                side.middle(side_ins, side_outs, side_sems)
            side.finish(side_ins, side_outs, side_sems)

    hbm = pl.BlockSpec(memory_space=pl.ANY)
    res = pl.pallas_call(
        with_side, name=name, grid=grid, in_specs=list(in_specs) + [hbm] * len(side.ins),
        out_specs=list(out_specs) + [hbm] * len(side.out_shape), out_shape=list(out_shape) + list(side.out_shape),
        scratch_shapes=list(scratch_shapes) + list(side.sems),
        compiler_params=_params(*(["arbitrary"] * len(grid))))(*args, *side.ins)
    sched.done(name, res[len(out_shape):])
    return res[:len(out_shape)]


def _rms_fwd(x, g, name):
    t, d = x.shape
    tm = _row_tile(t)

    def body(x_ref, g_ref, h_ref):
        xv = x_ref[...]
        r = lax.rsqrt(jnp.mean(xv * xv, axis=-1, keepdims=True) + RMS_EPS)
        h_ref[...] = (xv * r * g_ref[...]).astype(BF16)

    return _pallas(
        body, name=name, grid=(t // tm,),
        in_specs=[pl.BlockSpec((tm, d), lambda i: (i, 0)), pl.BlockSpec((1, d), lambda i: (0, 0))],
        out_specs=[pl.BlockSpec((tm, d), lambda i: (i, 0))], out_shape=[jax.ShapeDtypeStruct((t, d), BF16)],
        sem=("parallel",), args=(x, g))[0]


def _accumulate(ref, part, step):
    @pl.when(step == 0)
    def _():
        ref[...] = part

    @pl.when(step > 0)
    def _():
        ref[...] += part


def _mm(a, b, *, ta=False, tb=False, tm, tn, out_dtype=F32, add=None, scale=1.0, norm_g=None, norm_bwd=None, name):
    m, k = (a.shape[1], a.shape[0]) if ta else a.shape
    n = b.shape[0] if tb else b.shape[1]
    tm, tn = min(tm, m), min(tn, n)
    assert m % tm == 0 and n % tn == 0, (m, n, tm, tn)
    assert (norm_g is None and norm_bwd is None) or tn == n
    dims = (((0 if ta else 1,), (1 if tb else 0,)), ((), ()))
    n_in = 2 + (add is not None) + (norm_g is not None) + (3 if norm_bwd is not None else 0)

    def body(*refs):
        ins, outs = list(refs[2:n_in]), refs[n_in:]
        r = lax.dot_general(refs[0][...].astype(BF16), refs[1][...].astype(BF16), dims, preferred_element_type=F32)
        if scale != 1.0:
            r = r * scale
        if add is not None:
            r = r + ins.pop(0)[...]
        if norm_bwd is not None:
            xv, gv, dres = (ref[...] for ref in ins)
            rs = lax.rsqrt(jnp.mean(xv * xv, axis=-1, keepdims=True) + RMS_EPS)
            xhat = xv * rs
            gd = r * gv
            dx = dres + rs * (gd - xhat * jnp.mean(gd * xhat, axis=-1, keepdims=True))
            outs[0][...] = dx
            outs[1][...] = dx.astype(BF16)
            _accumulate(outs[2], jnp.sum(r * xhat, axis=0, keepdims=True), pl.program_id(0))
            return
        outs[0][...] = r.astype(out_dtype)
        if norm_g is not None:
            rs = lax.rsqrt(jnp.mean(r * r, axis=-1, keepdims=True) + RMS_EPS)
            outs[1][...] = (r * rs * ins.pop(0)[...]).astype(BF16)

    a_spec = pl.BlockSpec((k, tm), lambda i, j: (0, i)) if ta else pl.BlockSpec((tm, k), lambda i, j: (i, 0))
    b_spec = pl.BlockSpec((tn, k), lambda i, j: (j, 0)) if tb else pl.BlockSpec((k, tn), lambda i, j: (0, j))
    o_spec = pl.BlockSpec((tm, tn), lambda i, j: (i, j))
    vec = pl.BlockSpec((1, tn), lambda i, j: (0, j))
    args, specs = [a, b], [a_spec, b_spec]
    if add is not None:
        args.append(add)
        specs.append(o_spec)
    out_specs, out_shape = [o_spec], [jax.ShapeDtypeStruct((m, n), out_dtype)]
    if norm_g is not None:
        args.append(norm_g)
        specs.append(vec)
        out_specs.append(o_spec)
        out_shape.append(jax.ShapeDtypeStruct((m, n), BF16))
    if norm_bwd is not None:
        args += list(norm_bwd)
        specs += [o_spec, vec, o_spec]
        out_specs = [o_spec, o_spec, vec]
        out_shape = [jax.ShapeDtypeStruct((m, n), F32), jax.ShapeDtypeStruct((m, n), BF16),
                     jax.ShapeDtypeStruct((1, n), F32)]
    res = _pallas(body, name=name, grid=(m // tm, n // tn), in_specs=specs, out_specs=out_specs, out_shape=out_shape,
                  sem=("arbitrary", "arbitrary") if norm_bwd is not None else ("parallel", "parallel"), args=args)
    return res[0] if len(res) == 1 else res


def _ffn_tile(f):
    for tf in (1408, 512, 256, 128):
        if f % tf == 0:
            return tf
    raise ValueError(f)


def _ffn_fwd(h, x, wg, wu, wd, name, next_g=None, tgt=None):
    t, d = x.shape
    f = wg.shape[0]
    tm, tf = _row_tile(t), _ffn_tile(f)
    nf = f // tf
    assert (next_g is None) != (tgt is None)

    def body(h_ref, x_ref, wg_ref, wu_ref, wd_ref, tail_ref, g_ref, u_ref, o0_ref, o1_ref, *rest):
        acc_ref = rest[-1]
        j = pl.program_id(1)
        hv = h_ref[...]
        gv = lax.dot_general(hv, wg_ref[...], NT, preferred_element_type=F32)
        uv = lax.dot_general(hv, wu_ref[...], NT, preferred_element_type=F32)
        av = gv * _sigmoid(gv) * uv
        g_ref[...] = gv.astype(BF16)
        u_ref[...] = uv.astype(BF16)
        _accumulate(acc_ref, lax.dot_general(av.astype(BF16), wd_ref[...], NN, preferred_element_type=F32), j)

        @pl.when(j == nf - 1)
        def _():
            y = x_ref[...] + 0.5 * acc_ref[...]
            if tgt is None:
                o0_ref[...] = y
                rs = lax.rsqrt(jnp.mean(y * y, axis=-1, keepdims=True) + RMS_EPS)
                o1_ref[...] = (y * rs * tail_ref[...]).astype(BF16)
            else:
                e = y - tail_ref[...]
                dy = e * (1.0 / d)
                o0_ref[...] = dy
                o1_ref[...] = dy.astype(BF16)
                _accumulate(rest[0], jnp.sum(e * e, axis=0, keepdims=True), pl.program_id(0))

    row = pl.BlockSpec((tm, d), lambda i, j: (i, 0))
    hid = pl.BlockSpec((tm, tf), lambda i, j: (i, j))
    vec = pl.BlockSpec((1, d), lambda i, j: (0, 0))
    out_specs = [hid, hid, row, row] + ([vec] if tgt is not None else [])
    out_shape = [jax.ShapeDtypeStruct((t, f), BF16)] * 2 + [jax.ShapeDtypeStruct((t, d), F32),
                                                            jax.ShapeDtypeStruct((t, d), BF16)]
    if tgt is not None:
        out_shape.append(jax.ShapeDtypeStruct((1, d), F32))
    return _pallas(
        body, name=name, grid=(t // tm, nf),
        in_specs=[row, row] + [pl.BlockSpec((tf, d), lambda i, j: (j, 0))] * 3 + [vec if tgt is None else row],
        out_specs=out_specs, out_shape=out_shape, scratch_shapes=[pltpu.VMEM((tm, d), F32)],
        sem=("parallel" if tgt is None else "arbitrary", "arbitrary"),
        args=(h, x, wg, wu, wd, next_g if tgt is None else tgt))


def _ffn_up(h, wg, wu, name):
    t, d = h.shape
    f = wg.shape[0]
    tm, tf = _row_tile(t), _ffn_tile(f)

    def body(h_ref, wg_ref, wu_ref, g_ref, u_ref, a_ref):
        hv = h_ref[...]
        gv = lax.dot_general(hv, wg_ref[...], NT, preferred_element_type=F32)
        uv = lax.dot_general(hv, wu_ref[...], NT, preferred_element_type=F32)
        g_ref[...] = gv.astype(BF16)
        u_ref[...] = uv.astype(BF16)
        a_ref[...] = (gv * _sigmoid(gv) * uv).astype(BF16)

    hid = pl.BlockSpec((tm, tf), lambda i, j: (i, j))
    wrow = pl.BlockSpec((tf, d), lambda i, j: (j, 0))
    return _pallas(
        body, name=name, grid=(t // tm, f // tf), in_specs=[pl.BlockSpec((tm, d), lambda i, j: (i, 0)), wrow, wrow],
        out_specs=[hid, hid, hid], out_shape=[jax.ShapeDtypeStruct((t, f), BF16)] * 3,
        sem=("parallel", "parallel"), args=(h, wg, wu))


def _ffn_bwd_mid(dy, wd, g, u, name):
    t, d = dy.shape
    f = wd.shape[0]
    tm, tf = _row_tile(t), _ffn_tile(f)

    def body(dy_ref, wd_ref, g_ref, u_ref, dg_ref, du_ref, dwd_ref):
        dy16 = dy_ref[...]
        da = 0.5 * lax.dot_general(dy16, wd_ref[...], NT, preferred_element_type=F32)
        gv = g_ref[...].astype(F32)
        uv = u_ref[...].astype(F32)
        s = _sigmoid(gv)
        silu = gv * s
        dg_ref[...] = (da * uv * (s * (1.0 + gv * (1.0 - s)))).astype(BF16)
        du_ref[...] = (da * silu).astype(BF16)
        part = 0.5 * lax.dot_general((silu * uv).astype(BF16), dy16, (((0,), (0,)), ((), ())),
                                     preferred_element_type=F32)
        _accumulate(dwd_ref, part, pl.program_id(1))

    hid = pl.BlockSpec((tm, tf), lambda j, i: (i, j))
    wrow = pl.BlockSpec((tf, d), lambda j, i: (j, 0))
    return _pallas(
        body, name=name, grid=(f // tf, t // tm),
        in_specs=[pl.BlockSpec((tm, d), lambda j, i: (i, 0)), wrow, hid, hid],
        out_specs=[hid, hid, wrow],
        out_shape=[jax.ShapeDtypeStruct((t, f), BF16)] * 2 + [jax.ShapeDtypeStruct((f, d), F32)],
        sem=("parallel", "arbitrary"), args=(dy, wd, g, u))


def _rel_index(t, s_band):
    return jnp.clip(t + KPAD - s_band, -REL_CLIP, REL_CLIP) + REL_CLIP


def _bias_expand(rel_bias_pad):
    nh = rel_bias_pad.shape[0]

    def body(rb_ref, out_ref):
        rb = rb_ref[...]
        i_io = lax.broadcasted_iota(jnp.int32, (N_REL_PAD, BAND), 0)
        s_io = lax.broadcasted_iota(jnp.int32, (N_REL_PAD, BAND), 1)

        def row(r, carry):
            onehot = (i_io == _rel_index(pl.program_id(0) * rows + r, s_io)).astype(F32)
            out_ref[r] = _dot_exact01_r(rb, onehot)
            return carry

        lax.fori_loop(0, rows, row, 0)

    rows = 8
    return _pallas(
        body, name="bias_expand", grid=(CHUNK // rows,),
        in_specs=[pl.BlockSpec(rel_bias_pad.shape, lambda i: (0, 0))],
        out_specs=[pl.BlockSpec((rows, nh, BAND), lambda i: (i, 0, 0))],
        out_shape=[jax.ShapeDtypeStruct((CHUNK, nh, BAND), F32)], sem=("arbitrary",), args=(rel_bias_pad,))[0]


def _bias_fold(dbias):
    ng, nh = dbias.shape[0], dbias.shape[2]

    def body(db_ref, out_ref):
        s_io = lax.broadcasted_iota(jnp.int32, (BAND, N_REL_PAD), 0)
        i_io = lax.broadcasted_iota(jnp.int32, (BAND, N_REL_PAD), 1)

        def row(t, acc):
            onehot = (i_io == _rel_index(t, s_io)).astype(F32)
            d = db_ref[0, t]
            for gi in range(1, ng):
                d = d + db_ref[gi, t]
            return acc + _dot_exact01_r(d, onehot)

        out_ref[...] = lax.fori_loop(0, CHUNK, row, jnp.zeros((nh, N_REL_PAD), F32))

    return _pallas(
        body, name="bias_fold", grid=(1,), in_specs=[pl.BlockSpec(dbias.shape, lambda i: (0, 0, 0, 0))],
        out_specs=[pl.BlockSpec((nh, N_REL_PAD), lambda i: (0, 0))],
        out_shape=[jax.ShapeDtypeStruct((nh, N_REL_PAD), F32)], sem=("arbitrary",), args=(dbias,))[0]


def _left_half(shape):
    return lax.broadcasted_iota(jnp.int32, shape, len(shape) - 1) < ATTN_HEAD_DIM


def _stack_heads(v):
    left = _left_half(v.shape)
    zero = jnp.zeros_like(v)
    return jnp.concatenate([jnp.where(left, v, zero), jnp.where(left, zero, v)], axis=0)


def _unstack_heads(v):
    return jnp.where(_left_half((CHUNK, 128)), v[0:CHUNK, :], v[CHUNK:2 * CHUNK, :])


def _half_mean(v):
    r = lax.broadcasted_iota(jnp.int32, (128, 128), 0) < ATTN_HEAD_DIM
    c = lax.broadcasted_iota(jnp.int32, (128, 128), 1) < ATTN_HEAD_DIM
    return _dot_exact01_r(v, r == c) * (1.0 / ATTN_HEAD_DIM)


def _attn_prepare(q_ref, k_ref, v_ref, gq_ref, gk_ref, qs_scr, k_scr, v_scr):
    q, k = q_ref[...], k_ref[...]
    rq = lax.rsqrt(_half_mean(q * q) + RMS_EPS)
    rk = lax.rsqrt(_half_mean(k * k) + RMS_EPS)
    qhat, khat = q * rq, k * rk
    qs_scr[...] = (qhat * gq_ref[...] * ATTN_HEAD_DIM ** -0.5).astype(BF16)
    k_scr[0:KPAD, :] = jnp.zeros((KPAD, 128), BF16)
    v_scr[0:KPAD, :] = jnp.zeros((KPAD, 128), BF16)
    k_scr[KPAD:, :] = (khat * gk_ref[...]).astype(BF16)
    v_scr[KPAD:, :] = v_ref[...].astype(BF16)
    return qhat, rq, khat, rk


def _first_key(c):
    return jnp.maximum(CHUNK, (LEFT_CHUNKS + 1 - c) * CHUNK)


def _attn_fwd_chunk(c, qs_scr, k_scr, v_scr, bias_ref, o_ref):
    r0 = pl.multiple_of(c * CHUNK, CHUNK)
    s = lax.dot_general(_stack_heads(qs_scr[pl.ds(r0, CHUNK), :]), k_scr[pl.ds(r0, BAND), :], NT,
                        preferred_element_type=F32)
    yield
    col = lax.broadcasted_iota(jnp.int32, (2 * CHUNK, BAND), 1)
    s = jnp.where(col >= _first_key(c), s + bias_ref[...], -jnp.inf)
    m = jnp.max(s, axis=-1, keepdims=True)
    yield
    e = jnp.exp(s - m)
    yield
    inv = 1.0 / jnp.sum(e, axis=-1, keepdims=True)
    o = lax.dot_general(e.astype(BF16), v_scr[pl.ds(r0, BAND), :], NN, preferred_element_type=F32)
    yield
    o_ref[pl.ds(r0, CHUNK), :] = _unstack_heads(o * inv)


def _attn_bwd(proj, out, dout, bias, gq, gk, nb, seq):
    nc = seq // CHUNK
    lock = min(ATTN_LOCKSTEP, nc)
    assert nc % lock == 0
    scale = ATTN_HEAD_DIM ** -0.5

    def body(q_ref, k_ref, v_ref, o_ref, do_ref, bias_ref, gq_ref, gk_ref,
             dq_ref, dk_ref, dv_ref, dbias_ref, dgq_ref, dgk_ref,
             qs_scr, k_scr, v_scr, dqn_scr, dk_scr, dv_scr, db_scr):
        qhat, rq, khat, rk = _attn_prepare(q_ref, k_ref, v_ref, gq_ref, gk_ref, qs_scr, k_scr, v_scr)
        dk_scr[...] = jnp.zeros_like(dk_scr)
        dv_scr[...] = jnp.zeros_like(dv_scr)
        db_scr[...] = jnp.zeros_like(db_scr)

        def one_chunk(c):
            r0 = pl.multiple_of(c * CHUNK, CHUNK)
            qst = _stack_heads(qs_scr[pl.ds(r0, CHUNK), :])
            kb = k_scr[pl.ds(r0, BAND), :]
            vb = v_scr[pl.ds(r0, BAND), :]
            st = lax.dot_general(kb, qst, NT, preferred_element_type=F32) + bias_ref[...]
            dost = _stack_heads(do_ref[pl.ds(r0, CHUNK), :])
            dost16 = dost.astype(BF16)
            dpt = lax.dot_general(vb, dost16, NT, preferred_element_type=F32)
            yield
            key = lax.broadcasted_iota(jnp.int32, (BAND, 2 * CHUNK), 0)
            st = jnp.where(key >= _first_key(c), st, -jnp.inf)
            mx = jnp.max(st, axis=0, keepdims=True)
            drow = _row_sums_on_lanes(dost * _stack_heads(o_ref[pl.ds(r0, CHUNK), :]))
            yield
            et = jnp.exp(st - mx)
            yield
            pt = et * (1.0 / jnp.sum(et, axis=0, keepdims=True))
            yield
            dst = pt * (dpt - drow)
            dst16 = dst.astype(BF16)
            yield
            db_scr[...] += dst
            dqn_scr[pl.ds(r0, CHUNK), :] = scale * _unstack_heads(_dot(dst.T, kb))
            yield
            dk_scr[pl.ds(r0, BAND), :] += lax.dot_general(dst16, qst, NN, preferred_element_type=F32)
            yield
            dv_scr[pl.ds(r0, BAND), :] += lax.dot_general(pt.astype(BF16), dost16, NN, preferred_element_type=F32)

        def chunk(i, carry):
            _lockstep([one_chunk(i * lock + a) for a in range(lock)])
            return carry

        lax.fori_loop(0, nc // lock, chunk, 0, unroll=max(1, min(ATTN_UNROLL, nc) // lock))

        def norm_bwd(dn, hat, r, g_ref):
            gd = dn * g_ref[...]
            return r * (gd - hat * _half_mean(gd * hat)), jnp.sum(dn * hat, axis=0, keepdims=True)

        dq, dgq = norm_bwd(dqn_scr[...], qhat, rq, gq_ref)
        dk, dgk = norm_bwd(dk_scr[KPAD:, :], khat, rk, gk_ref)
        dq_ref[...] = dq.astype(BF16)
        dk_ref[...] = dk.astype(BF16)
        dv_ref[...] = dv_scr[KPAD:, :].astype(BF16)
        dbias_ref[0] = db_scr[...]
        dgq_ref[0] = dgq
        dgk_ref[0] = dgk

    def col(off):
        return pl.BlockSpec((seq, 128), lambda b, hp: (b, off + hp))

    vec = pl.BlockSpec((1, 128), lambda b, hp: (0, 0))
    gvec = pl.BlockSpec((1, 1, 128), lambda b, hp: (b * (ATTN_HEADS // 2) + hp, 0, 0))
    t = nb * seq
    return _pallas(
        body, name="attn_bwd", grid=(nb, ATTN_HEADS // 2),
        in_specs=[col(0), col(4), col(8), col(0), col(0),
                  pl.BlockSpec((BAND, 2 * CHUNK), lambda b, hp: (hp, 0)), vec, vec],
        out_specs=[col(0), col(0), col(0), pl.BlockSpec((1, BAND, 2 * CHUNK), lambda b, hp: (b, hp, 0)),
                   gvec, gvec],
        out_shape=[jax.ShapeDtypeStruct((t, ATTN_WIDTH), BF16)] * 3
        + [jax.ShapeDtypeStruct((nb, ATTN_HEADS // 2 * BAND, 2 * CHUNK), F32)]
        + [jax.ShapeDtypeStruct((nb * ATTN_HEADS // 2, 1, 128), F32)] * 2,
        scratch_shapes=[pltpu.VMEM((seq, 128), BF16), pltpu.VMEM((seq + KPAD, 128), BF16),
                        pltpu.VMEM((seq + KPAD, 128), BF16), pltpu.VMEM((seq, 128), F32),
                        pltpu.VMEM((seq + KPAD, 128), F32), pltpu.VMEM((seq + KPAD, 128), F32),
                        pltpu.VMEM((BAND, 2 * CHUNK), F32)],
        sem=("parallel", "parallel"), args=(proj, proj, proj, out, dout, bias, gq, gk))


def _tri(lower):
    r = lax.broadcasted_iota(jnp.int32, (CHUNK, CHUNK), 0)
    c = lax.broadcasted_iota(jnp.int32, (CHUNK, CHUNK), 1)
    return (r >= c) if lower else (r <= c)


def _hgrn_gates(hq, hf, lb):
    sq = _sigmoid(hq)
    sf = _sigmoid(hf)
    return hq * sq, sq, sf, lb + (1.0 - lb) * sf


def _hgrn_offdiag(q_s, k_s, b_s):
    row = lax.broadcasted_iota(jnp.int32, (CHUNK, HGRN_HEAD_DIM), 0)
    bv, qv, kv = b_s[...], q_s[...], k_s[...]
    eqs, eks = [], []
    for i in range(1, N_SUB):
        r = b_s[pl.ds(SUB * i - 1, 1), :]
        in_i = (row >= SUB * i) & (row < SUB * (i + 1))
        eqs.append(jnp.exp(jnp.where(in_i, bv - r, -jnp.inf)))
        eks.append(jnp.exp(jnp.where(row < SUB * i, r - bv, -jnp.inf)))
    eq = jnp.concatenate(eqs, axis=1)
    ek = jnp.concatenate(eks, axis=1)
    qt = jnp.concatenate([qv] * (N_SUB - 1), axis=1) * eq
    kt = jnp.concatenate([kv] * (N_SUB - 1), axis=1) * ek
    return qt, kt, eq, ek


def _hgrn_diag_e(b_s, i, s):
    t_io = lax.broadcasted_iota(jnp.int32, (SUB, HGRN_HEAD_DIM), 0)
    bi = b_s[pl.ds(SUB * i, SUB), :]
    return jnp.exp(jnp.where(t_io >= s, bi - b_s[pl.ds(SUB * i + s, 1), :], -jnp.inf)), t_io


def _hgrn_intra(q_s, k_s, b_s, a_s, qt, kt):
    ktp = jnp.concatenate([kt, jnp.zeros_like(kt)], axis=0)
    a_s[...] = _dot(qt, ktp, NT)
    yield
    col = lax.broadcasted_iota(jnp.int32, (SUB, HGRN_HEAD_DIM), 1)
    for i in range(N_SUB):
        qi = q_s[pl.ds(SUB * i, SUB), :]
        ai = jnp.zeros((SUB, HGRN_HEAD_DIM), F32)
        for s in range(SUB):
            e, _ = _hgrn_diag_e(b_s, i, s)
            a_col = jnp.sum(qi * k_s[pl.ds(SUB * i + s, 1), :] * e, axis=-1, keepdims=True)
            ai = ai + jnp.where(col == SUB * i + s, a_col, 0.0)
            if s % DIAG_STAGE == DIAG_STAGE - 1:
                yield
        a_s[pl.ds(SUB * i, SUB), :] += ai


def _mixer_fwd(proj, bias, gq, gk, lb, go, nb, seq):
    nc = seq // CHUNK
    hd = HGRN_HEAD_DIM
    nblk = ATTN_HEADS // 2
    rows_blk = seq // nblk
    nck = rows_blk // CHUNK
    per = nc // nck
    assert rows_blk % CHUNK == 0

    def body(aq_ref, ak_ref, av_ref, bias_ref, gq_ref, gk_ref, hq_ref, hf_ref, hi_ref, hg_ref, lb_ref, go_ref,
             ao_ref, y_ref, o_ref, st_ref, a_ref, qs_scr, k_scr, v_scr, st_all, q_all, k_all, b_all, a_all):
        _attn_prepare(aq_ref, ak_ref, av_ref, gq_ref, gk_ref, qs_scr, k_scr, v_scr)

        @pl.when(pl.program_id(1) == 0)
        def _():
            st_all[...] = jnp.zeros_like(st_all)

        lower = _tri(True)

        def head_chunk(hh, c, rows):
            ln = slice(hd * hh, hd * (hh + 1))
            st, q_s, k_s, b_s, a_s = st_all.at[hh], q_all.at[hh], k_all.at[hh], b_all.at[hh], a_all.at[hh]
            q, _, _, f = _hgrn_gates(hq_ref[rows, ln], hf_ref[rows, ln], lb_ref[:, ln])
            v = hi_ref[rows, ln]
            yield
            b = _dot_exact01(lower, jnp.log(f))
            q_s[...] = q
            k_s[...] = 1.0 - f
            b_s[...] = b
            st_ref[hh, c] = st[...]
            yield
            qt, kt, _, _ = _hgrn_offdiag(q_s, k_s, b_s)
            yield
            yield from _hgrn_intra(q_s, k_s, b_s, a_s, qt, kt)
            a16 = a_s[...].astype(BF16)
            a_ref[hh, c] = a16
            vp = jnp.concatenate([v, jnp.zeros_like(v)], axis=0)
            o = _dot(a16, vp) + _dot(q * jnp.exp(b), st[...], NT)
            yield
            bl = b_s[pl.ds(CHUNK - 1, 1), :]
            st[...] = st[...] * jnp.exp(bl) + _tn(v, (1.0 - f) * jnp.exp(bl - b))
            o_ref[rows, ln] = o
            yield
            n = o * lax.rsqrt(jnp.mean(o * o, axis=-1, keepdims=True) + RMS_EPS) * go_ref[...]
            hg = hg_ref[rows, ln]
            y_ref[rows, ln] = n * hg * _sigmoid(hg)

        def chunk(c, carry):
            rows = pl.ds(pl.multiple_of(c * CHUNK, CHUNK), CHUNK)
            _lockstep([_attn_fwd_chunk(c * per + a, qs_scr, k_scr, v_scr, bias_ref, ao_ref) for a in range(per)]
                      + [head_chunk(hh, c, rows) for hh in range(HGRN_HEADS)])
            return carry

        lax.fori_loop(0, nck, chunk, 0)

    hp, wide = HGRN_HEADS, HGRN_HEADS * hd

    def acol(off):
        return pl.BlockSpec((seq, 128), lambda b, s: (b, off + s))

    def col(off):
        return pl.BlockSpec((rows_blk, wide), lambda b, s: (b * nblk + s, off // hp))

    out = pl.BlockSpec((rows_blk, wide), lambda b, s: (b * nblk + s, 0))
    vec = pl.BlockSpec((1, 128), lambda b, s: (0, 0))
    t = nb * seq
    return _pallas(
        body, name="mixer_fwd", grid=(nb, nblk),
        in_specs=[acol(0), acol(4), acol(8), pl.BlockSpec((2 * CHUNK, BAND), lambda b, s: (s, 0)), vec, vec,
                  col(12), col(16), col(20), col(24), pl.BlockSpec((1, wide), lambda b, s: (0, 0)), vec],
        out_specs=[pl.BlockSpec((seq, 128), lambda b, s: (b, s)), out, out,
                   pl.BlockSpec((hp, nck, hd, hd), lambda b, s: (b, s, 0, 0)),
                   pl.BlockSpec((hp, nck, CHUNK, hd), lambda b, s: (b, s, 0, 0))],
        out_shape=[jax.ShapeDtypeStruct((t, ATTN_WIDTH), F32)] + [jax.ShapeDtypeStruct((t, wide), F32)] * 2
        + [jax.ShapeDtypeStruct((nb * hp, nc, hd, hd), F32), jax.ShapeDtypeStruct((nb * hp, nc, CHUNK, hd), BF16)],
        scratch_shapes=[pltpu.VMEM((seq, 128), BF16), pltpu.VMEM((seq + KPAD, 128), BF16),
                        pltpu.VMEM((seq + KPAD, 128), BF16), pltpu.VMEM((hp, hd, hd), F32)]
        + [pltpu.VMEM((hp, CHUNK, hd), F32)] * 4,
        sem=("parallel", "arbitrary"), args=(proj,) * 3 + (bias, gq, gk) + (proj,) * 4 + (lb, go))


def _hgrn_bwd(proj, lb, go, o_pre, states, scores, dout, nb, seq):
    nc = seq // CHUNK
    hd = HGRN_HEAD_DIM
    rows_blk = min(HGRN_ROWS, seq)
    nblk, nck = seq // rows_blk, rows_blk // CHUNK

    def body(hq_ref, hf_ref, hi_ref, hg_ref, lb_ref, go_ref, o_ref, st_ref, a_ref, dy_ref,
             dhq_ref, dhf_ref, dhi_ref, dhg_ref, dlb_ref, dgo_ref,
             dst_all, q_all, k_all, b_all, da_all, dqi_all, dki_all, dlb_all, dgo_all):
        @pl.when(pl.program_id(1) == 0)
        def _():
            dst_all[...] = jnp.zeros_like(dst_all)
            dlb_all[...] = jnp.zeros_like(dlb_all)
            dgo_all[...] = jnp.zeros_like(dgo_all)

        lower, upper = _tri(True), _tri(False)
        gov = go_ref[...]
        row = lax.broadcasted_iota(jnp.int32, (CHUNK, hd), 0)

        def head_chunk(hh, c, rows):
            ln = slice(hd * hh, hd * (hh + 1))
            dst, q_s, k_s, b_s = dst_all.at[hh], q_all.at[hh], k_all.at[hh], b_all.at[hh]
            da_s, dqi_s, dki_s = da_all.at[hh], dqi_all.at[hh], dki_all.at[hh]
            dlb_acc, dgo_acc = dlb_all.at[hh], dgo_all.at[hh]
            lbv = lb_ref[:, ln]
            hq, hf, v, hg = hq_ref[rows, ln], hf_ref[rows, ln], hi_ref[rows, ln], hg_ref[rows, ln]
            q, sq, sf, f = _hgrn_gates(hq, hf, lbv)
            kk = 1.0 - f
            yield
            b = _dot_exact01(lower, jnp.log(f))
            q_s[...] = q
            k_s[...] = kk
            b_s[...] = b
            yield
            bl = b_s[pl.ds(CHUNK - 1, 1), :]
            ebl = jnp.exp(bl)
            ekd = jnp.exp(bl - b)
            kd = kk * ekd
            eb = jnp.exp(b)
            qb = q * eb
            st0 = st_ref[hh, c]
            dst1 = dst[...]
            yield

            o = o_ref[rows, ln]
            dy = dy_ref[rows, ln]
            sg = _sigmoid(hg)
            rstd = lax.rsqrt(jnp.mean(o * o, axis=-1, keepdims=True) + RMS_EPS)
            ohat = o * rstd
            dn = dy * hg * sg
            dhg_ref[rows, ln] = (dy * ohat * gov * (sg * (1.0 + hg * (1.0 - sg)))).astype(BF16)
            dgo_acc[...] += jnp.sum(dn * ohat, axis=0, keepdims=True)
            gdn = dn * gov
            do = rstd * (gdn - ohat * jnp.mean(gdn * ohat, axis=-1, keepdims=True))
            yield

            qt, kt, eq, ek = _hgrn_offdiag(q_s, k_s, b_s)
            da = _dot(do, v, NT)
            dat = _dot(v, do, NT)
            da_s[...] = da
            yield
            dqo = _dot(da, kt) * eq
            dko = _dot(dat, qt) * ek
            dqi_s[...] = sum(dqo[:, j * hd:(j + 1) * hd] for j in range(N_SUB - 1))
            dki_s[...] = sum(dko[:, j * hd:(j + 1) * hd] for j in range(N_SUB - 1))
            yield
            col = lax.broadcasted_iota(jnp.int32, (SUB, CHUNK), 1)
            for i in range(N_SUB):
                qi = q_s[pl.ds(SUB * i, SUB), :]
                dai = da_s[pl.ds(SUB * i, SUB), :]
                dqd = jnp.zeros((SUB, hd), F32)
                for s in range(SUB):
                    e, _ = _hgrn_diag_e(b_s, i, s)
                    dacol = jnp.sum(jnp.where(col == SUB * i + s, dai, 0.0), axis=-1, keepdims=True)
                    w = dacol * e
                    dqd = dqd + w * k_s[pl.ds(SUB * i + s, 1), :]
                    dki_s[pl.ds(SUB * i + s, 1), :] += jnp.sum(w * qi, axis=0, keepdims=True)
                    if s % DIAG_STAGE == DIAG_STAGE - 1:
                        yield
                dqi_s[pl.ds(SUB * i, SUB), :] += dqd
            dqi, dki = dqi_s[...], dki_s[...]

            dv = _tn(a_ref[hh, c].astype(F32), do)[0:CHUNK, :] + _dot(kd, dst1, NT)
            dqb = _dot(do, st0)
            dkd = _dot(v, dst1)
            yield
            t2 = dkd * kd
            dq = dqb * eb + dqi
            dk = dkd * ekd + dki
            dbl = jnp.sum(t2, axis=0, keepdims=True) + ebl * jnp.sum(st0 * dst1, axis=0, keepdims=True)
            db = dqb * qb - t2 + q * dqi - kk * dki + jnp.where(row == CHUNK - 1, dbl, 0.0)
            yield
            dg = _dot_exact01(upper, db)
            dst[...] = dst1 * ebl + _tn(do, qb)
            yield

            df = dg / f - dk
            dhf_ref[rows, ln] = (df * (1.0 - lbv) * sf * (1.0 - sf)).astype(BF16)
            dlb_acc[...] += jnp.sum(df * (1.0 - sf), axis=0, keepdims=True)
            dhq_ref[rows, ln] = (dq * (sq * (1.0 + hq * (1.0 - sq)))).astype(BF16)
            dhi_ref[rows, ln] = dv.astype(BF16)

        def chunk(it, carry):
            c = nck - 1 - it
            rows = pl.ds(pl.multiple_of(c * CHUNK, CHUNK), CHUNK)
            _lockstep([head_chunk(hh, c, rows) for hh in range(HGRN_HEADS)])
            return carry

        lax.fori_loop(0, nck, chunk, 0)

        @pl.when(pl.program_id(1) == nblk - 1)
        def _():
            dlb_ref[...] = dlb_all[...]
            dgo_ref[...] = dgo_all[...]

    hp, wide = HGRN_HEADS, HGRN_HEADS * hd

    def col(off):
        return pl.BlockSpec((rows_blk, wide), lambda b, s: (b * nblk + nblk - 1 - s, off // hp))

    out = pl.BlockSpec((rows_blk, wide), lambda b, s: (b * nblk + nblk - 1 - s, 0))
    part = pl.BlockSpec((hp, 1, hd), lambda b, s: (b, 0, 0))
    t = nb * seq
    return pl.pallas_call(
        body, name="hgrn_bwd", grid=(nb, nblk),
        in_specs=[col(12), col(16), col(20), col(24), pl.BlockSpec((1, wide), lambda b, s: (0, 0)),
                  pl.BlockSpec((1, hd), lambda b, s: (0, 0)), out,
                  pl.BlockSpec((hp, nck, hd, hd), lambda b, s: (b, nblk - 1 - s, 0, 0)),
                  pl.BlockSpec((hp, nck, CHUNK, hd), lambda b, s: (b, nblk - 1 - s, 0, 0)), col(4)],
        out_specs=[out, out, out, out, part, part],
        out_shape=[jax.ShapeDtypeStruct((t, wide), BF16)] * 4 + [jax.ShapeDtypeStruct((nb * hp, 1, hd), F32)] * 2,
        scratch_shapes=[pltpu.VMEM((hp, hd, hd), F32)] + [pltpu.VMEM((hp, CHUNK, hd), F32)] * 3
        + [pltpu.VMEM((hp, CHUNK, CHUNK), F32)] + [pltpu.VMEM((hp, CHUNK, hd), F32)] * 2
        + [pltpu.VMEM((hp, 1, hd), F32)] * 2,
        compiler_params=_params("parallel", "arbitrary"),
    )(proj, proj, proj, proj, lb, go, o_pre, states, scores, dout)


def _lb_fwd(lower_bounds):
    def body(x_ref, o_ref):
        xv = x_ref[...]
        e = jnp.exp(xv - jnp.max(xv, axis=0, keepdims=True))
        o_ref[...] = e[0:1, :] / jnp.sum(e, axis=0, keepdims=True)

    return pl.pallas_call(body, name="lb_fwd",
                          out_shape=jax.ShapeDtypeStruct((1, lower_bounds.shape[1]), F32))(lower_bounds)


def _lb_bwd(lower_bounds, dlb_parts):
    ng = dlb_parts.shape[0]

    def body(x_ref, d_ref, o_ref):
        xv = x_ref[...]
        e = jnp.exp(xv - jnp.max(xv, axis=0, keepdims=True))
        p = e / jnp.sum(e, axis=0, keepdims=True)
        dlb = d_ref[0]
        for gi in range(1, ng):
            dlb = dlb + d_ref[gi]
        first = lax.broadcasted_iota(jnp.int32, xv.shape, 0) == 0
        o_ref[...] = p * (jnp.where(first, dlb, 0.0) - p[0:1, :] * dlb)

    return pl.pallas_call(body, name="lb_bwd",
                          out_shape=jax.ShapeDtypeStruct(lower_bounds.shape, F32))(lower_bounds, dlb_parts)


def _ffn_bwd(x, g, h, gate, up, dy, dy16, w, put, tag):
    wg, wu, wd = w[tag + "_w_gate"], w[tag + "_w_up"], w[tag + "_w_down"]
    dgate, dup, dwd = _ffn_bwd_mid(dy16, wd, gate, up, tag + "_bwd_mid")
    put(tag + "_w_down", dwd)
    put(tag + "_w_gate", _mm(dgate, h, ta=True, tm=1408, tn=512, name=tag + "_dwg"))
    put(tag + "_w_up", _mm(dup, h, ta=True, tm=1408, tn=512, name=tag + "_dwu"))
    dh = _mm(dgate, wg, tm=512, tn=1024, name=tag + "_dh_gate")
    return _mm(dup, wu, tm=512, tn=1024, add=dh, norm_bwd=(x, g, dy), name=tag + "_dh_up")


def _local_step(x, tgt, sp, w, put, nb, seq):
    d = x.shape[1]
    h1 = _rms_fwd(x, sp["ffn1_norm_g"], "ffn1_norm")
    rb_pad = jnp.pad(sp["attn_rel_bias"], ((0, 0), (0, N_REL_PAD - N_REL)))
    bias = jnp.transpose(_bias_expand(rb_pad), (1, 0, 2)).reshape(ATTN_HEADS * CHUNK, BAND)
    gq2 = jnp.concatenate([sp["attn_q_norm_g"]] * 2, axis=1)
    gk2 = jnp.concatenate([sp["attn_k_norm_g"]] * 2, axis=1)
    lb = _lb_fwd(sp["hgrn_lower_bounds"])
    gate1, up1, act1 = _ffn_up(h1, w["ffn1_w_gate"], w["ffn1_w_up"], "ffn1_up")
    x1, h2 = _mm(act1, w["ffn1_w_down"], tm=512, tn=d, add=x, scale=0.5, norm_g=sp["mix_norm_g"],
                 name="ffn1_down")
    proj = _mm(h2, w["w_in"], tb=True, tm=256, tn=w["w_in"].shape[0], name="in_proj")
    attn, hy, ho, hstate, hscore = _mixer_fwd(proj, bias, gq2, gk2, lb, sp["hgrn_out_norm_g"], nb, seq)
    mix = jnp.concatenate([attn, hy], axis=1)
    x2, h3 = _mm(mix, w["w_out"], tm=512, tn=1024, add=x1, norm_g=sp["ffn2_norm_g"], name="out_proj")
    gate2, up2, dx3, dx3_16, sq = _ffn_fwd(h3, x2, w["ffn2_w_gate"], w["ffn2_w_up"], w["ffn2_w_down"], "ffn2_fwd",
                                           tgt=tgt)
    loss = 0.5 * jnp.sum(sq) / d

    dx2, dx2_16, dg3 = _ffn_bwd(x2, sp["ffn2_norm_g"], h3, gate2, up2, dx3, dx3_16, w, put, "ffn2")
    dmix = _mm(dx2_16, w["w_out"], tb=True, tm=512, tn=1024, name="out_proj_dx")
    put("w_out", _mm(mix, dx2_16, ta=True, tm=512, tn=1024, name="out_proj_dw"))
    bias_t = jnp.transpose(bias.reshape(ATTN_HEADS // 2, 2 * CHUNK, BAND), (0, 2, 1)).reshape(-1, 2 * CHUNK)
    dq, dk, dv, dbias, dgq, dgk = _attn_bwd(proj, attn, dmix, bias_t, gq2, gk2, nb, seq)
    dbias = jnp.transpose(dbias.reshape(nb, ATTN_HEADS // 2, BAND, 2, CHUNK), (0, 4, 1, 3, 2))
    dbias = dbias.reshape(nb, CHUNK, ATTN_HEADS, BAND)
    dgq = jnp.sum(dgq, axis=(0, 1)).reshape(2, ATTN_HEAD_DIM).sum(axis=0, keepdims=True)
    dgk = jnp.sum(dgk, axis=(0, 1)).reshape(2, ATTN_HEAD_DIM).sum(axis=0, keepdims=True)
    dhq, dhf, dhi, dhg, dlb, dgo = _hgrn_bwd(proj, lb, sp["hgrn_out_norm_g"], ho, hstate, hscore, dmix, nb, seq)
    dproj = jnp.concatenate([dq, dk, dv, dhq, dhf, dhi, dhg], axis=1)
    put("w_in", _mm(dproj, h2, ta=True, tm=512, tn=1024, name="in_proj_dw"))
    dx1, dx1_16, dgm = _mm(dproj, w["w_in"], tm=512, tn=1024, norm_bwd=(x1, sp["mix_norm_g"], dx2),
                           name="in_proj_dx")
    dx0, _, dg1 = _ffn_bwd(x, sp["ffn1_norm_g"], h1, gate1, up1, dx1, dx1_16, w, put, "ffn1")

    small = {
        "ffn1_norm_g": dg1, "mix_norm_g": dgm, "ffn2_norm_g": dg3,
        "attn_q_norm_g": dgq, "attn_k_norm_g": dgk,
        "attn_rel_bias": _bias_fold(dbias)[:, :N_REL],
        "hgrn_lower_bounds": _lb_bwd(sp["hgrn_lower_bounds"], dlb.reshape(nb, 1, HGRN_HEADS * HGRN_HEAD_DIM)),
        "hgrn_out_norm_g": jnp.sum(dgo, axis=(0, 1))[None, :],
    }
    return loss, dx0, small


MESH = pl.DeviceIdType.MESH
ANY = pl.BlockSpec(memory_space=pl.ANY)


def _coords():
    return lax.axis_index("x"), lax.axis_index("y"), lax.axis_index("c")


def _other_chips(x, y):
    return [(1 - x, y), (x, 1 - y), (1 - x, 1 - y)]


def _gather_side(shards):
    n = len(shards)

    def copies(ins, outs, sems):
        send_sems, recv_sems, local_sems = sems
        x, y, c = _coords()
        xn, yn, dg = (1 - x, y), (x, 1 - y), (1 - x, 1 - y)

        def copy(i, k, block, to, half=None, src=None):
            bx, by, bc = block
            dst = outs[i].at[4 * bx + 2 * by + bc]
            if half is not None:
                rows = shards[i].shape[0] // 2
                dst = dst.at[pl.ds(half * rows, rows)]
            return pltpu.make_async_remote_copy(
                src_ref=dst if src is None else src, dst_ref=dst, send_sem=send_sems.at[i, k],
                recv_sem=recv_sems.at[i, k], device_id=to, device_id_type=MESH)

        mine = [pltpu.make_async_copy(ins[i], outs[i].at[4 * x + 2 * y + c], local_sems.at[i]) for i in range(n)]
        return copy, mine, (x, y, c), (x, y, 1 - c), xn, yn, dg, c

    def own(copy, i, ins, me, sibling, xn, yn, c):
        return [copy(i, 0, me, sibling, src=ins[i]), copy(i, 1, me, (*xn, c), src=ins[i]),
                copy(i, 2, me, (*yn, c), src=ins[i])]

    def passed_on(copy, i, sibling, xn, yn, c):
        return [copy(i, 3, (*xn, c), sibling), copy(i, 5, (*xn, c), (*yn, c), half=0),
                copy(i, 4, (*yn, c), sibling), copy(i, 6, (*yn, c), (*xn, c), half=1)]

    def diagonal(copy, i, sibling, dg, c):
        return [copy(i, 7, (*dg, c), sibling, half=0), copy(i, 8, (*dg, c), sibling, half=1)]

    def start(ins, outs, sems):
        copy, mine, me, sibling, xn, yn, dg, c = copies(ins, outs, sems)
        for cp in mine + [cp for i in range(n) for cp in own(copy, i, ins, me, sibling, xn, yn, c)]:
            cp.start()

    def middle(ins, outs, sems):
        copy, mine, me, sibling, xn, yn, dg, c = copies(ins, outs, sems)
        for i in range(n):
            fwd_x, relay_x, fwd_y, relay_y = passed_on(copy, i, sibling, xn, yn, c)
            copy(i, 1, (*xn, c), me).wait_recv()
            fwd_x.start()
            relay_x.start()
            copy(i, 2, (*yn, c), me).wait_recv()
            fwd_y.start()
            relay_y.start()

    def finish(ins, outs, sems):
        copy, mine, me, sibling, xn, yn, dg, c = copies(ins, outs, sems)
        for i in range(n):
            top, bottom = diagonal(copy, i, sibling, dg, c)
            copy(i, 5, (*dg, c), me, half=0).wait_recv()
            top.start()
            copy(i, 6, (*dg, c), me, half=1).wait_recv()
            bottom.start()
        for i in range(n):
            copy(i, 0, sibling, me).wait_recv()
            copy(i, 3, (*xn, 1 - c), me).wait_recv()
            copy(i, 4, (*yn, 1 - c), me).wait_recv()
            copy(i, 7, (*dg, 1 - c), me, half=0).wait_recv()
            copy(i, 8, (*dg, 1 - c), me, half=1).wait_recv()
        for i in range(n):
            for cp in (own(copy, i, ins, me, sibling, xn, yn, c) + passed_on(copy, i, sibling, xn, yn, c)
                       + diagonal(copy, i, sibling, dg, c)):
                cp.wait_send()
        for cp in mine:
            cp.wait()

    return _Side(list(shards), [jax.ShapeDtypeStruct((N_DEV,) + s.shape, s.dtype) for s in shards],
                 [pltpu.SemaphoreType.DMA((n, 9)), pltpu.SemaphoreType.DMA((n, 9)), pltpu.SemaphoreType.DMA((n,))],
                 start, finish, middle)


def _pair_side(grads):
    n = len(grads)

    def copies(ins, outs, sems):
        send_sems, recv_sems = sems
        x, y, c = _coords()
        return [pltpu.make_async_remote_copy(
            src_ref=ins[i].at[2 * k + 1 - c], dst_ref=outs[i].at[k], send_sem=send_sems.at[i, k],
            recv_sem=recv_sems.at[i, k], device_id=(x, y, 1 - c), device_id_type=MESH)
            for i in range(n) for k in range(4)]

    def start(ins, outs, sems):
        for cp in copies(ins, outs, sems):
            cp.start()

    def finish(ins, outs, sems):
        for cp in copies(ins, outs, sems):
            cp.wait()

    return _Side(list(grads), [jax.ShapeDtypeStruct((4,) + g.shape[1:], g.dtype) for g in grads],
                 [pltpu.SemaphoreType.DMA((n, 4)), pltpu.SemaphoreType.DMA((n, 4))], start, finish)


def _pair_add(grad, recv, core, name):
    _, r, cdim = grad.shape

    def body(c_ref, g_ref, r_ref, o_ref):
        o_ref[...] = (g_ref[...] + r_ref[...]).astype(BF16)

    blk = (1, r, cdim)
    return pl.pallas_call(
        body, name=name,
        grid_spec=pltpu.PrefetchScalarGridSpec(
            num_scalar_prefetch=1, grid=(4,),
            in_specs=[pl.BlockSpec(blk, lambda k, c_ref: (2 * k + c_ref[0], 0, 0)),
                      pl.BlockSpec(blk, lambda k, c_ref: (k, 0, 0))],
            out_specs=pl.BlockSpec(blk, lambda k, c_ref: (k, 0, 0))),
        out_shape=jax.ShapeDtypeStruct((4, r, cdim), BF16),
        compiler_params=_params("arbitrary"),
    )(core, grad, recv)


def _chip_side(parts):
    n = len(parts)

    def copies(ins, outs, sems):
        send_sems, recv_sems, local_sems = sems
        x, y, c = _coords()
        chips = _other_chips(x, y)
        mine = [pltpu.make_async_copy(ins[i].at[2 * x + y], outs[i].at[2 * x + y], local_sems.at[i])
                for i in range(n)]
        sent = [pltpu.make_async_remote_copy(
            src_ref=ins[i].at[2 * px + py], dst_ref=outs[i].at[2 * x + y], send_sem=send_sems.at[i, j],
            recv_sem=recv_sems.at[i, j], device_id=(px, py, c), device_id_type=MESH)
            for i in range(n) for j, (px, py) in enumerate(chips)]
        return mine, sent, chips, c

    def start(ins, outs, sems):
        mine, sent, _, _ = copies(ins, outs, sems)
        for cp in mine + sent:
            cp.start()

    def finish(ins, outs, sems):
        mine, sent, chips, c = copies(ins, outs, sems)
        send_sems, recv_sems, _ = sems
        for i in range(n):
            for j, (px, py) in enumerate(chips):
                landed = outs[i].at[2 * px + py]
                pltpu.make_async_remote_copy(
                    src_ref=landed, dst_ref=landed, send_sem=send_sems.at[i, j], recv_sem=recv_sems.at[i, j],
                    device_id=(px, py, c), device_id_type=MESH).wait_recv()
        for cp in sent:
            cp.wait_send()
        for cp in mine:
            cp.wait()

    return _Side(list(parts), [jax.ShapeDtypeStruct(p.shape, p.dtype) for p in parts],
                 [pltpu.SemaphoreType.DMA((n, 3)), pltpu.SemaphoreType.DMA((n, 3)), pltpu.SemaphoreType.DMA((n,))],
                 start, finish)


def _all_reduce_small(v):
    r = v.shape[0]

    def body(v_ref, o_ref, buf, send_sems, recv_sems):
        x, y, c = _coords()
        me = 4 * x + 2 * y + c
        buf[me] = v_ref[...]
        cps = []
        for k in range(1, N_DEV):
            px = 1 - x if k & 4 else x
            py = 1 - y if k & 2 else y
            pc = 1 - c if k & 1 else c
            cps.append((pltpu.make_async_remote_copy(
                src_ref=v_ref, dst_ref=buf.at[me], send_sem=send_sems.at[k - 1], recv_sem=recv_sems.at[k - 1],
                device_id=(px, py, pc), device_id_type=MESH), 4 * px + 2 * py + pc))
        for cp, _ in cps:
            cp.start()
        for k, (cp, peer) in enumerate(cps):
            pltpu.make_async_remote_copy(
                src_ref=v_ref, dst_ref=buf.at[peer], send_sem=send_sems.at[k], recv_sem=recv_sems.at[k],
                device_id=(x, y, c), device_id_type=MESH).wait_recv()
        for cp, _ in cps:
            cp.wait_send()
        acc = buf[0]
        for j in range(1, N_DEV):
            acc = acc + buf[j]
        o_ref[...] = acc

    return pl.pallas_call(
        body, name="small_all_reduce", out_shape=jax.ShapeDtypeStruct(v.shape, F32),
        in_specs=[pl.BlockSpec(memory_space=pltpu.VMEM)], out_specs=pl.BlockSpec(memory_space=pltpu.VMEM),
        scratch_shapes=[pltpu.VMEM((N_DEV, r, 128), F32), pltpu.SemaphoreType.DMA((N_DEV - 1,)),
                        pltpu.SemaphoreType.DMA((N_DEV - 1,))],
    )(v)


def _adamw(w, m, v, g, name):
    parts = w.ndim == 3
    r, cdim = w.shape[-2:]
    tr = r // 4 if r % 32 == 0 else r

    def body(w_ref, m_ref, v_ref, g_ref, go_ref, d_ref, mo_ref, vo_ref):
        if parts:
            gv = g_ref[0].astype(F32)
            for k in range(1, 4):
                gv = gv + g_ref[k].astype(F32)
            gv = gv[None]
        else:
            gv = g_ref[...]
        m2 = ADAM_B1 * m_ref[...] + (1.0 - ADAM_B1) * gv
        v2 = ADAM_B2 * v_ref[...] + (1.0 - ADAM_B2) * (gv * gv)
        m_hat = m2 / (1.0 - ADAM_B1 ** ADAM_STEP)
        v_hat = v2 / (1.0 - ADAM_B2 ** ADAM_STEP)
        go_ref[...] = gv
        d_ref[...] = -ADAM_LR * (m_hat / (jnp.sqrt(v_hat) + ADAM_EPS) + ADAM_WD * w_ref[...])
        mo_ref[...] = m2
        vo_ref[...] = v2

    if parts:
        row = pl.BlockSpec((1, tr, cdim), lambda i: (0, i, 0))
        g_spec = pl.BlockSpec((4, tr, cdim), lambda i: (0, i, 0))
    else:
        row = g_spec = pl.BlockSpec((tr, cdim), lambda i: (i, 0))
    return pl.pallas_call(
        body, name=name, grid=(r // tr,), in_specs=[row, row, row, g_spec], out_specs=[row] * 4,
        out_shape=[jax.ShapeDtypeStruct(w.shape, F32)] * 4,
        compiler_params=_params("parallel"),
    )(w, m, v, g)


WEIGHTS = ["ffn1_norm_g", "ffn1_w_gate", "ffn1_w_up", "ffn1_w_down", "mix_norm_g", "w_in", "attn_q_norm_g",
           "attn_k_norm_g", "attn_rel_bias", "hgrn_lower_bounds", "hgrn_out_norm_g", "w_out", "ffn2_norm_g",
           "ffn2_w_gate", "ffn2_w_up", "ffn2_w_down"]
COL_SHARDED = ("ffn1_w_gate", "ffn1_w_up", "w_in", "ffn2_w_gate", "ffn2_w_up")
ROW_SHARDED = ("ffn1_w_down", "w_out", "ffn2_w_down")
BIG = [n for n in WEIGHTS if n in COL_SHARDED or n in ROW_SHARDED]
SMALL = [n for n in WEIGHTS if n not in BIG]
PACK_ROWS = 8
FFN2 = ["ffn2_w_down", "ffn2_w_gate", "ffn2_w_up"]
MIXER = ["w_out", "w_in"]

PLAN = {
    "ffn1_norm": [("gather", ["ffn1_w_gate"])],
    "bias_expand": [("gather", ["ffn1_w_up"])],
    "ffn1_up": [("gather", ["ffn1_w_down", "w_out"])],
    "ffn1_down": [("gather", ["w_in"])],
    "mixer_fwd": [("gather", FFN2)],
    "ffn2_dh_gate": [("pair", FFN2)],
    "attn_bwd": [("chip", FFN2)],
    "in_proj_dx": [("pair", MIXER)],
    "ffn1_bwd_mid": [("chip", MIXER)],
    "ffn1_dwg": [("pair", ["ffn1_w_down"])],
    "ffn1_dwu": [("chip", ["ffn1_w_down"]), ("pair", ["ffn1_w_gate"])],
    "ffn1_dh_gate": [("chip", ["ffn1_w_gate"]), ("pair", ["ffn1_w_up"])],
    "bias_fold": [("chip", ["ffn1_w_up"])],
}


def _join_sides(sides):
    def split(refs, counts):
        out, at = [], 0
        for n in counts:
            out.append(refs[at:at + n])
            at += n
        return out

    n_in, n_out, n_sem = ([len(getattr(s, f)) for s in sides] for f in ("ins", "out_shape", "sems"))

    def run(which):
        def go(ins, outs, sems):
            for s, i, o, m in zip(sides, split(ins, n_in), split(outs, n_out), split(sems, n_sem)):
                if getattr(s, which) is not None:
                    getattr(s, which)(i, o, m)
        return go

    return _Side([a for s in sides for a in s.ins], [a for s in sides for a in s.out_shape],
                 [a for s in sides for a in s.sems], run("start"), run("finish"),
                 run("middle") if any(s.middle is not None for s in sides) else None)


class _Schedule:
    def __init__(self, shards):
        self.shards = shards
        self.weights = {}
        self.sliced = {}
        self.partials = {}
        self.reduced = {}

    def put(self, name, grad):
        self.sliced[name] = grad.reshape((N_DEV,) + self.shards[name].shape)

    def side_for(self, call):
        if call not in PLAN:
            return None
        sides = []
        for kind, names in PLAN[call]:
            if kind == "gather":
                sides.append(_gather_side([self.shards[n] for n in names]))
            elif kind == "pair":
                sides.append(_pair_side([self.sliced[n] for n in names]))
            else:
                sides.append(_chip_side([self.partials[n] for n in names]))
        return _join_sides(sides)

    def done(self, call, outs):
        at = 0
        for kind, names in PLAN[call]:
            self.file(kind, names, outs[at:at + len(names)])
            at += len(names)

    def file(self, kind, names, outs):
        for n, o in zip(names, outs):
            if kind == "gather":
                self.weights[n] = o.reshape(N_DEV * o.shape[1], o.shape[2])
            elif kind == "pair":
                core = lax.axis_index("c").astype(jnp.int32).reshape(1)
                self.partials[n] = _pair_add(self.sliced[n], o, core, n + "_pair_add")
            else:
                self.reduced[n] = o


def _pack_small(vals, loss=None):
    parts = []
    for n in SMALL:
        a = vals[n]
        if n == "attn_rel_bias":
            a = jnp.pad(a.reshape(ATTN_HEADS, N_REL), ((0, 0), (0, N_REL_PAD - N_REL)))
        flat = a.reshape(-1)
        size = -(-flat.shape[0] // (PACK_ROWS * 128)) * PACK_ROWS * 128
        parts.append(jnp.pad(flat, (0, size - flat.shape[0])).reshape(-1, 128))
    tail = jnp.zeros((PACK_ROWS, 128), F32)
    if loss is not None:
        tail = tail.at[0, 0].set(loss)
    return jnp.concatenate(parts + [tail], axis=0)


def _unpack_small(packed, shapes):
    out, row = {}, 0
    for n in SMALL:
        shape = shapes[n]
        if n == "attn_rel_bias":
            rows = ATTN_HEADS * N_REL_PAD // 128
            out[n] = packed[row:row + rows].reshape(ATTN_HEADS, N_REL_PAD)[:, :N_REL].reshape(shape)
        else:
            size = 1
            for s in shape:
                size *= s
            rows = -(-size // (PACK_ROWS * 128)) * PACK_ROWS
            out[n] = packed[row:row + rows].reshape(-1)[:size].reshape(shape)
        row += rows
    return out, packed[row, 0]


def kernel(x, ffn1_norm_g, ffn1_w_gate, ffn1_w_up, ffn1_w_down, mix_norm_g, w_in, attn_q_norm_g, attn_k_norm_g, attn_rel_bias, hgrn_lower_bounds, hgrn_out_norm_g, w_out, ffn2_norm_g, ffn2_w_gate, ffn2_w_up, ffn2_w_down, loss_target, m_ffn1_norm_g, m_ffn1_w_gate, m_ffn1_w_up, m_ffn1_w_down, m_mix_norm_g, m_w_in, m_attn_q_norm_g, m_attn_k_norm_g, m_attn_rel_bias, m_hgrn_lower_bounds, m_hgrn_out_norm_g, m_w_out, m_ffn2_norm_g, m_ffn2_w_gate, m_ffn2_w_up, m_ffn2_w_down, v_ffn1_norm_g, v_ffn1_w_gate, v_ffn1_w_up, v_ffn1_w_down, v_mix_norm_g, v_w_in, v_attn_q_norm_g, v_attn_k_norm_g, v_attn_rel_bias, v_hgrn_lower_bounds, v_hgrn_out_norm_g, v_w_out, v_ffn2_norm_g, v_ffn2_w_gate, v_ffn2_w_up, v_ffn2_w_down):
    wts = dict(zip(WEIGHTS, (ffn1_norm_g, ffn1_w_gate, ffn1_w_up, ffn1_w_down, mix_norm_g, w_in, attn_q_norm_g,
                             attn_k_norm_g, attn_rel_bias, hgrn_lower_bounds, hgrn_out_norm_g, w_out, ffn2_norm_g,
                             ffn2_w_gate, ffn2_w_up, ffn2_w_down)))
    mom = dict(zip(WEIGHTS, (m_ffn1_norm_g, m_ffn1_w_gate, m_ffn1_w_up, m_ffn1_w_down, m_mix_norm_g, m_w_in,
                             m_attn_q_norm_g, m_attn_k_norm_g, m_attn_rel_bias, m_hgrn_lower_bounds,
                             m_hgrn_out_norm_g, m_w_out, m_ffn2_norm_g, m_ffn2_w_gate, m_ffn2_w_up, m_ffn2_w_down)))
    var = dict(zip(WEIGHTS, (v_ffn1_norm_g, v_ffn1_w_gate, v_ffn1_w_up, v_ffn1_w_down, v_mix_norm_g, v_w_in,
                             v_attn_q_norm_g, v_attn_k_norm_g, v_attn_rel_bias, v_hgrn_lower_bounds,
                             v_hgrn_out_norm_g, v_w_out, v_ffn2_norm_g, v_ffn2_w_gate, v_ffn2_w_up, v_ffn2_w_down)))
    nb, seq, d = x.shape
    shapes = {n: wts[n].shape for n in WEIGHTS}

    def rows_first(a, n):
        return jnp.swapaxes(a, 1, 2) if n in COL_SHARDED else a

    sched = _Schedule({n: rows_first(wts[n], n)[0].astype(BF16) for n in BIG})
    sp = {n: wts[n] for n in SMALL}
    sp["attn_rel_bias"] = wts["attn_rel_bias"][0]
    _ACTIVE[0] = sched
    try:
        loss, dx, dsmall = _local_step(x.reshape(nb * seq, d), loss_target.reshape(nb * seq, d), sp,
                                       sched.weights, sched.put, nb, seq)
    finally:
        _ACTIVE[0] = None
    reduced = sched.reduced

    small_sum = _all_reduce_small(_pack_small(dsmall, loss))
    gsmall, loss_total = _unpack_small(small_sum, shapes)

    grads, deltas, new_m, new_v = {}, {}, {}, {}
    for n in BIG:
        out = _adamw(rows_first(wts[n], n), rows_first(mom[n], n), rows_first(var[n], n), reduced[n], n + "_adamw")
        grads[n], deltas[n], new_m[n], new_v[n] = (rows_first(o, n) for o in out)
    packed = _adamw(_pack_small(wts), _pack_small(mom), _pack_small(var), small_sum, "small_adamw")
    for dst, p in zip((deltas, new_m, new_v), packed[1:]):
        dst.update(_unpack_small(p, shapes)[0])
    grads.update(gsmall)

    return (loss_total, dx.reshape(nb, seq, d), *[grads[n] for n in WEIGHTS], *[deltas[n] for n in WEIGHTS],
            *[new_m[n] for n in WEIGHTS], *[new_v[n] for n in WEIGHTS])
```

```python
import functools

import jax
import jax.numpy as jnp
from jax import lax
from jax.experimental import pallas as pl
from jax.experimental.pallas import tpu as pltpu

F32 = jnp.float32
BF16 = jnp.bfloat16

RMS_EPS = 1e-6
CHUNK = 64
LEFT_CHUNKS = 8
BAND = (LEFT_CHUNKS + 2) * CHUNK
KPAD = BAND - CHUNK
REL_CLIP = 128
N_REL = 2 * REL_CLIP + 1
N_REL_PAD = 384
ATTN_HEADS = 8
ATTN_HEAD_DIM = 64
ATTN_WIDTH = ATTN_HEADS * ATTN_HEAD_DIM
ATTN_LOCKSTEP = 4
ATTN_UNROLL = 8
HGRN_HEADS = 4
HGRN_HEAD_DIM = 128
HGRN_ROWS = 512
SUB = 16
N_SUB = CHUNK // SUB
DIAG_STAGE = 4
N_DEV = 8

ADAM_LR = 0.001
ADAM_B1 = 0.9
ADAM_B2 = 0.999
ADAM_EPS = 1e-08
ADAM_WD = 0.01
ADAM_STEP = 10

VMEM_LIMIT = 56 * 1024 * 1024

NT = (((1,), (1,)), ((), ()))
NN = (((1,), (0,)), ((), ()))


def _params(*sem):
    return pltpu.CompilerParams(dimension_semantics=sem, vmem_limit_bytes=VMEM_LIMIT)


def _sigmoid(v):
    return 0.5 * jnp.tanh(0.5 * v) + 0.5


def _dot(a, b, dims=NN):
    return lax.dot_general(a.astype(BF16), b.astype(BF16), dims, preferred_element_type=F32)


def _dot_exact01(m01, v):
    m = m01.astype(BF16)
    hi = v.astype(BF16)
    r1 = v - hi.astype(F32)
    mid = r1.astype(BF16)
    lo = (r1 - mid.astype(F32)).astype(BF16)
    out = lax.dot_general(m, hi, NN, preferred_element_type=F32)
    out = out + lax.dot_general(m, mid, NN, preferred_element_type=F32)
    return out + lax.dot_general(m, lo, NN, preferred_element_type=F32)


def _dot_exact01_r(v, m01):
    m = m01.astype(BF16)
    hi = v.astype(BF16)
    r1 = v - hi.astype(F32)
    mid = r1.astype(BF16)
    lo = (r1 - mid.astype(F32)).astype(BF16)
    out = lax.dot_general(hi, m, NN, preferred_element_type=F32)
    out = out + lax.dot_general(mid, m, NN, preferred_element_type=F32)
    return out + lax.dot_general(lo, m, NN, preferred_element_type=F32)


def _lockstep(stages):
    live = list(stages)
    while live:
        still = []
        for g in live:
            try:
                next(g)
                still.append(g)
            except StopIteration:
                pass
        live = still


def _row_sums_on_lanes(v):
    ones = jnp.ones((8, v.shape[1]), BF16)
    hi = v.astype(BF16)
    r1 = v - hi.astype(F32)
    mid = r1.astype(BF16)
    lo = (r1 - mid.astype(F32)).astype(BF16)
    out = lax.dot_general(ones, hi, NT, preferred_element_type=F32)
    out = out + lax.dot_general(ones, mid, NT, preferred_element_type=F32)
    return (out + lax.dot_general(ones, lo, NT, preferred_element_type=F32))[0:1, :]


def _tn(a, b):
    ap = jnp.concatenate([a, jnp.zeros_like(a)], axis=0)
    bp = jnp.concatenate([b, jnp.zeros_like(b)], axis=0)
    return _dot(ap.T, bp)


def _row_tile(t):
    for tm in (512, 256, 128, 64, 32, 16, 8):
        if t % tm == 0:
            return tm
    raise ValueError(t)


class _Side:
    def __init__(self, ins, out_shape, sems, start, finish, middle=None):
        self.ins, self.out_shape, self.sems = ins, out_shape, sems
        self.start, self.middle, self.finish = start, middle, finish


_ACTIVE = [None]


def _pallas(body, *, name, grid, in_specs, out_specs, out_shape, scratch_shapes=(), sem, args):
    sched = _ACTIVE[0]
    side = sched.side_for(name) if sched is not None else None
    if side is None:
        return pl.pallas_call(
            body, name=name, grid=grid, in_specs=list(in_specs), out_specs=list(out_specs),
            out_shape=list(out_shape), scratch_shapes=list(scratch_shapes), compiler_params=_params(*sem))(*args)
    cuts = [len(in_specs), len(side.ins), len(out_shape), len(side.out_shape), len(scratch_shapes)]

    def with_side(*refs):
        groups, at = [], 0
        for n in cuts:
            groups.append(refs[at:at + n])
            at += n
        ins, side_ins, outs, side_outs, scratch = groups
        side_sems = refs[at:]
        step, total = pl.program_id(0), grid[0]
        for a in range(1, len(grid)):
            step, total = step * grid[a] + pl.program_id(a), total * grid[a]
        has_middle = side.middle is not None and total >= 3

        @pl.when(step == 0)
        def _():
            side.start(side_ins, side_outs, side_sems)

        if has_middle:
            @pl.when(step == total // 2)
            def _():
                side.middle(side_ins, side_outs, side_sems)

        body(*ins, *outs, *scratch)

        @pl.when(step == total - 1)
        def _():
            if side.middle is not None and not has_middle:
                side.middle(side_ins, side_outs, side_sems)
            side.finish(side_ins, side_outs, side_sems)

    hbm = pl.BlockSpec(memory_space=pl.ANY)
    res = pl.pallas_call(
        with_side, name=name, grid=grid, in_specs=list(in_specs) + [hbm] * len(side.ins),
        out_specs=list(out_specs) + [hbm] * len(side.out_shape), out_shape=list(out_shape) + list(side.out_shape),
        scratch_shapes=list(scratch_shapes) + list(side.sems),
        compiler_params=_params(*(["arbitrary"] * len(grid))))(*args, *side.ins)
    sched.done(name, res[len(out_shape):])
    return res[:len(out_shape)]


def _rms_fwd(x, g, name):
    t, d = x.shape
    tm = _row_tile(t)

    def body(x_ref, g_ref, h_ref):
        xv = x_ref[...]
        r = lax.rsqrt(jnp.mean(xv * xv, axis=-1, keepdims=True) + RMS_EPS)
        h_ref[...] = (xv * r * g_ref[...]).astype(BF16)

    return _pallas(
        body, name=name, grid=(t // tm,),
        in_specs=[pl.BlockSpec((tm, d), lambda i: (i, 0)), pl.BlockSpec((1, d), lambda i: (0, 0))],
        out_specs=[pl.BlockSpec((tm, d), lambda i: (i, 0))], out_shape=[jax.ShapeDtypeStruct((t, d), BF16)],
        sem=("parallel",), args=(x, g))[0]


def _accumulate(ref, part, step):
    @pl.when(step == 0)
    def _():
        ref[...] = part

    @pl.when(step > 0)
    def _():
        ref[...] += part


def _mm(a, b, *, ta=False, tb=False, tm, tn, out_dtype=F32, add=None, scale=1.0, norm_g=None, norm_bwd=None, name):
    m, k = (a.shape[1], a.shape[0]) if ta else a.shape
    n = b.shape[0] if tb else b.shape[1]
    tm, tn = min(tm, m), min(tn, n)
    assert m % tm == 0 and n % tn == 0, (m, n, tm, tn)
    assert (norm_g is None and norm_bwd is None) or tn == n
    dims = (((0 if ta else 1,), (1 if tb else 0,)), ((), ()))
    n_in = 2 + (add is not None) + (norm_g is not None) + (3 if norm_bwd is not None else 0)

    def body(*refs):
        ins, outs = list(refs[2:n_in]), refs[n_in:]
        r = lax.dot_general(refs[0][...].astype(BF16), refs[1][...].astype(BF16), dims, preferred_element_type=F32)
        if scale != 1.0:
            r = r * scale
        if add is not None:
            r = r + ins.pop(0)[...]
        if norm_bwd is not None:
            xv, gv, dres = (ref[...] for ref in ins)
            rs = lax.rsqrt(jnp.mean(xv * xv, axis=-1, keepdims=True) + RMS_EPS)
            xhat = xv * rs
            gd = r * gv
            dx = dres + rs * (gd - xhat * jnp.mean(gd * xhat, axis=-1, keepdims=True))
            outs[0][...] = dx
            outs[1][...] = dx.astype(BF16)
            _accumulate(outs[2], jnp.sum(r * xhat, axis=0, keepdims=True), pl.program_id(0))
            return
        outs[0][...] = r.astype(out_dtype)
        if norm_g is not None:
            rs = lax.rsqrt(jnp.mean(r * r, axis=-1, keepdims=True) + RMS_EPS)
            outs[1][...] = (r * rs * ins.pop(0)[...]).astype(BF16)

    a_spec = pl.BlockSpec((k, tm), lambda i, j: (0, i)) if ta else pl.BlockSpec((tm, k), lambda i, j: (i, 0))
    b_spec = pl.BlockSpec((tn, k), lambda i, j: (j, 0)) if tb else pl.BlockSpec((k, tn), lambda i, j: (0, j))
    o_spec = pl.BlockSpec((tm, tn), lambda i, j: (i, j))
    vec = pl.BlockSpec((1, tn), lambda i, j: (0, j))
    args, specs = [a, b], [a_spec, b_spec]
    if add is not None:
        args.append(add)
        specs.append(o_spec)
    out_specs, out_shape = [o_spec], [jax.ShapeDtypeStruct((m, n), out_dtype)]
    if norm_g is not None:
        args.append(norm_g)
        specs.append(vec)
        out_specs.append(o_spec)
        out_shape.append(jax.ShapeDtypeStruct((m, n), BF16))
    if norm_bwd is not None:
        args += list(norm_bwd)
        specs += [o_spec, vec, o_spec]
        out_specs = [o_spec, o_spec, vec]
        out_shape = [jax.ShapeDtypeStruct((m, n), F32), jax.ShapeDtypeStruct((m, n), BF16),
                     jax.ShapeDtypeStruct((1, n), F32)]
    res = _pallas(body, name=name, grid=(m // tm, n // tn), in_specs=specs, out_specs=out_specs, out_shape=out_shape,
                  sem=("arbitrary", "arbitrary") if norm_bwd is not None else ("parallel", "parallel"), args=args)
    return res[0] if len(res) == 1 else res


def _ffn_tile(f):
    for tf in (1408, 512, 256, 128):
        if f % tf == 0:
            return tf
    raise ValueError(f)


def _ffn_fwd(h, x, wg, wu, wd, name, next_g=None, tgt=None):
    t, d = x.shape
    f = wg.shape[0]
    tm, tf = _row_tile(t), _ffn_tile(f)
    nf = f // tf
    assert (next_g is None) != (tgt is None)

    def body(h_ref, x_ref, wg_ref, wu_ref, wd_ref, tail_ref, g_ref, u_ref, o0_ref, o1_ref, *rest):
        acc_ref = rest[-1]
        j = pl.program_id(1)
        hv = h_ref[...]
        gv = lax.dot_general(hv, wg_ref[...], NT, preferred_element_type=F32)
        uv = lax.dot_general(hv, wu_ref[...], NT, preferred_element_type=F32)
        av = gv * _sigmoid(gv) * uv
        g_ref[...] = gv.astype(BF16)
        u_ref[...] = uv.astype(BF16)
        _accumulate(acc_ref, lax.dot_general(av.astype(BF16), wd_ref[...], NN, preferred_element_type=F32), j)

        @pl.when(j == nf - 1)
        def _():
            y = x_ref[...] + 0.5 * acc_ref[...]
            if tgt is None:
                o0_ref[...] = y
                rs = lax.rsqrt(jnp.mean(y * y, axis=-1, keepdims=True) + RMS_EPS)
                o1_ref[...] = (y * rs * tail_ref[...]).astype(BF16)
            else:
                e = y - tail_ref[...]
                dy = e * (1.0 / d)
                o0_ref[...] = dy
                o1_ref[...] = dy.astype(BF16)
                _accumulate(rest[0], jnp.sum(e * e, axis=0, keepdims=True), pl.program_id(0))

    row = pl.BlockSpec((tm, d), lambda i, j: (i, 0))
    hid = pl.BlockSpec((tm, tf), lambda i, j: (i, j))
    vec = pl.BlockSpec((1, d), lambda i, j: (0, 0))
    out_specs = [hid, hid, row, row] + ([vec] if tgt is not None else [])
    out_shape = [jax.ShapeDtypeStruct((t, f), BF16)] * 2 + [jax.ShapeDtypeStruct((t, d), F32),
                                                            jax.ShapeDtypeStruct((t, d), BF16)]
    if tgt is not None:
        out_shape.append(jax.ShapeDtypeStruct((1, d), F32))
    return _pallas(
        body, name=name, grid=(t // tm, nf),
        in_specs=[row, row] + [pl.BlockSpec((tf, d), lambda i, j: (j, 0))] * 3 + [vec if tgt is None else row],
        out_specs=out_specs, out_shape=out_shape, scratch_shapes=[pltpu.VMEM((tm, d), F32)],
        sem=("parallel" if tgt is None else "arbitrary", "arbitrary"),
        args=(h, x, wg, wu, wd, next_g if tgt is None else tgt))


def _ffn_up(h, wg, wu, name):
    t, d = h.shape
    f = wg.shape[0]
    tm, tf = _row_tile(t), _ffn_tile(f)

    def body(h_ref, wg_ref, wu_ref, g_ref, u_ref, a_ref):
        hv = h_ref[...]
        gv = lax.dot_general(hv, wg_ref[...], NT, preferred_element_type=F32)
        uv = lax.dot_general(hv, wu_ref[...], NT, preferred_element_type=F32)
        g_ref[...] = gv.astype(BF16)
        u_ref[...] = uv.astype(BF16)
        a_ref[...] = (gv * _sigmoid(gv) * uv).astype(BF16)

    hid = pl.BlockSpec((tm, tf), lambda i, j: (i, j))
    wrow = pl.BlockSpec((tf, d), lambda i, j: (j, 0))
    return _pallas(
        body, name=name, grid=(t // tm, f // tf), in_specs=[pl.BlockSpec((tm, d), lambda i, j: (i, 0)), wrow, wrow],
        out_specs=[hid, hid, hid], out_shape=[jax.ShapeDtypeStruct((t, f), BF16)] * 3,
        sem=("parallel", "parallel"), args=(h, wg, wu))


def _ffn_bwd_mid(dy, wd, g, u, name):
    t, d = dy.shape
    f = wd.shape[0]
    tm, tf = _row_tile(t), _ffn_tile(f)

    def body(dy_ref, wd_ref, g_ref, u_ref, dg_ref, du_ref, dwd_ref):
        dy16 = dy_ref[...]
        da = 0.5 * lax.dot_general(dy16, wd_ref[...], NT, preferred_element_type=F32)
        gv = g_ref[...].astype(F32)
        uv = u_ref[...].astype(F32)
        s = _sigmoid(gv)
        silu = gv * s
        dg_ref[...] = (da * uv * (s * (1.0 + gv * (1.0 - s)))).astype(BF16)
        du_ref[...] = (da * silu).astype(BF16)
        part = 0.5 * lax.dot_general((silu * uv).astype(BF16), dy16, (((0,), (0,)), ((), ())),
                                     preferred_element_type=F32)
        _accumulate(dwd_ref, part, pl.program_id(1))

    hid = pl.BlockSpec((tm, tf), lambda j, i: (i, j))
    wrow = pl.BlockSpec((tf, d), lambda j, i: (j, 0))
    return _pallas(
        body, name=name, grid=(f // tf, t // tm),
        in_specs=[pl.BlockSpec((tm, d), lambda j, i: (i, 0)), wrow, hid, hid],
        out_specs=[hid, hid, wrow],
        out_shape=[jax.ShapeDtypeStruct((t, f), BF16)] * 2 + [jax.ShapeDtypeStruct((f, d), F32)],
        sem=("parallel", "arbitrary"), args=(dy, wd, g, u))


def _rel_index(t, s_band):
    return jnp.clip(t + KPAD - s_band, -REL_CLIP, REL_CLIP) + REL_CLIP


def _bias_expand(rel_bias_pad):
    nh = rel_bias_pad.shape[0]

    def body(rb_ref, out_ref):
        rb = rb_ref[...]
        i_io = lax.broadcasted_iota(jnp.int32, (N_REL_PAD, BAND), 0)
        s_io = lax.broadcasted_iota(jnp.int32, (N_REL_PAD, BAND), 1)

        def row(r, carry):
            onehot = (i_io == _rel_index(pl.program_id(0) * rows + r, s_io)).astype(F32)
            out_ref[r] = _dot_exact01_r(rb, onehot)
            return carry

        lax.fori_loop(0, rows, row, 0)

    rows = 8
    return _pallas(
        body, name="bias_expand", grid=(CHUNK // rows,),
        in_specs=[pl.BlockSpec(rel_bias_pad.shape, lambda i: (0, 0))],
        out_specs=[pl.BlockSpec((rows, nh, BAND), lambda i: (i, 0, 0))],
        out_shape=[jax.ShapeDtypeStruct((CHUNK, nh, BAND), F32)], sem=("arbitrary",), args=(rel_bias_pad,))[0]


def _bias_fold(dbias):
    ng, nh = dbias.shape[0], dbias.shape[2]

    def body(db_ref, out_ref):
        s_io = lax.broadcasted_iota(jnp.int32, (BAND, N_REL_PAD), 0)
        i_io = lax.broadcasted_iota(jnp.int32, (BAND, N_REL_PAD), 1)

        def row(t, acc):
            onehot = (i_io == _rel_index(t, s_io)).astype(F32)
            d = db_ref[0, t]
            for gi in range(1, ng):
                d = d + db_ref[gi, t]
            return acc + _dot_exact01_r(d, onehot)

        out_ref[...] = lax.fori_loop(0, CHUNK, row, jnp.zeros((nh, N_REL_PAD), F32))

    return _pallas(
        body, name="bias_fold", grid=(1,), in_specs=[pl.BlockSpec(dbias.shape, lambda i: (0, 0, 0, 0))],
        out_specs=[pl.BlockSpec((nh, N_REL_PAD), lambda i: (0, 0))],
        out_shape=[jax.ShapeDtypeStruct((nh, N_REL_PAD), F32)], sem=("arbitrary",), args=(dbias,))[0]


def _left_half(shape):
    return lax.broadcasted_iota(jnp.int32, shape, len(shape) - 1) < ATTN_HEAD_DIM


def _stack_heads(v):
    left = _left_half(v.shape)
    zero = jnp.zeros_like(v)
    return jnp.concatenate([jnp.where(left, v, zero), jnp.where(left, zero, v)], axis=0)


def _unstack_heads(v):
    return jnp.where(_left_half((CHUNK, 128)), v[0:CHUNK, :], v[CHUNK:2 * CHUNK, :])


def _half_mean(v):
    r = lax.broadcasted_iota(jnp.int32, (128, 128), 0) < ATTN_HEAD_DIM
    c = lax.broadcasted_iota(jnp.int32, (128, 128), 1) < ATTN_HEAD_DIM
    return _dot_exact01_r(v, r == c) * (1.0 / ATTN_HEAD_DIM)


def _attn_prepare(q_ref, k_ref, v_ref, gq_ref, gk_ref, qs_scr, k_scr, v_scr):
    q, k = q_ref[...], k_ref[...]
    rq = lax.rsqrt(_half_mean(q * q) + RMS_EPS)
    rk = lax.rsqrt(_half_mean(k * k) + RMS_EPS)
    qhat, khat = q * rq, k * rk
    qs_scr[...] = (qhat * gq_ref[...] * ATTN_HEAD_DIM ** -0.5).astype(BF16)
    k_scr[0:KPAD, :] = jnp.zeros((KPAD, 128), BF16)
    v_scr[0:KPAD, :] = jnp.zeros((KPAD, 128), BF16)
    k_scr[KPAD:, :] = (khat * gk_ref[...]).astype(BF16)
    v_scr[KPAD:, :] = v_ref[...].astype(BF16)
    return qhat, rq, khat, rk


def _first_key(c):
    return jnp.maximum(CHUNK, (LEFT_CHUNKS + 1 - c) * CHUNK)


def _attn_fwd_chunk(c, qs_scr, k_scr, v_scr, bias_ref, o_ref):
    r0 = pl.multiple_of(c * CHUNK, CHUNK)
    s = lax.dot_general(_stack_heads(qs_scr[pl.ds(r0, CHUNK), :]), k_scr[pl.ds(r0, BAND), :], NT,
                        preferred_element_type=F32)
    yield
    col = lax.broadcasted_iota(jnp.int32, (2 * CHUNK, BAND), 1)
    s = jnp.where(col >= _first_key(c), s + bias_ref[...], -jnp.inf)
    m = jnp.max(s, axis=-1, keepdims=True)
    yield
    e = jnp.exp(s - m)
    yield
    inv = 1.0 / jnp.sum(e, axis=-1, keepdims=True)
    o = lax.dot_general(e.astype(BF16), v_scr[pl.ds(r0, BAND), :], NN, preferred_element_type=F32)
    yield
    o_ref[pl.ds(r0, CHUNK), :] = _unstack_heads(o * inv)


def _attn_bwd(proj, out, dout, bias, gq, gk, nb, seq):
    nc = seq // CHUNK
    lock = min(ATTN_LOCKSTEP, nc)
    assert nc % lock == 0
    scale = ATTN_HEAD_DIM ** -0.5

    def body(q_ref, k_ref, v_ref, o_ref, do_ref, bias_ref, gq_ref, gk_ref,
             dq_ref, dk_ref, dv_ref, dbias_ref, dgq_ref, dgk_ref,
             qs_scr, k_scr, v_scr, dqn_scr, dk_scr, dv_scr, db_scr):
        qhat, rq, khat, rk = _attn_prepare(q_ref, k_ref, v_ref, gq_ref, gk_ref, qs_scr, k_scr, v_scr)
        dk_scr[...] = jnp.zeros_like(dk_scr)
        dv_scr[...] = jnp.zeros_like(dv_scr)
        db_scr[...] = jnp.zeros_like(db_scr)

        def one_chunk(c):
            r0 = pl.multiple_of(c * CHUNK, CHUNK)
            qst = _stack_heads(qs_scr[pl.ds(r0, CHUNK), :])
            kb = k_scr[pl.ds(r0, BAND), :]
            vb = v_scr[pl.ds(r0, BAND), :]
            st = lax.dot_general(kb, qst, NT, preferred_element_type=F32) + bias_ref[...]
            dost = _stack_heads(do_ref[pl.ds(r0, CHUNK), :])
            dost16 = dost.astype(BF16)
            dpt = lax.dot_general(vb, dost16, NT, preferred_element_type=F32)
            yield
            key = lax.broadcasted_iota(jnp.int32, (BAND, 2 * CHUNK), 0)
            st = jnp.where(key >= _first_key(c), st, -jnp.inf)
            mx = jnp.max(st, axis=0, keepdims=True)
            drow = _row_sums_on_lanes(dost * _stack_heads(o_ref[pl.ds(r0, CHUNK), :]))
            yield
            et = jnp.exp(st - mx)
            yield
            pt = et * (1.0 / jnp.sum(et, axis=0, keepdims=True))
            yield
            dst = pt * (dpt - drow)
            dst16 = dst.astype(BF16)
            yield
            db_scr[...] += dst
            dqn_scr[pl.ds(r0, CHUNK), :] = scale * _unstack_heads(_dot(dst.T, kb))
            yield
            dk_scr[pl.ds(r0, BAND), :] += lax.dot_general(dst16, qst, NN, preferred_element_type=F32)
            yield
            dv_scr[pl.ds(r0, BAND), :] += lax.dot_general(pt.astype(BF16), dost16, NN, preferred_element_type=F32)

        def chunk(i, carry):
            _lockstep([one_chunk(i * lock + a) for a in range(lock)])
            return carry

        lax.fori_loop(0, nc // lock, chunk, 0, unroll=max(1, min(ATTN_UNROLL, nc) // lock))

        def norm_bwd(dn, hat, r, g_ref):
            gd = dn * g_ref[...]
            return r * (gd - hat * _half_mean(gd * hat)), jnp.sum(dn * hat, axis=0, keepdims=True)

        dq, dgq = norm_bwd(dqn_scr[...], qhat, rq, gq_ref)
        dk, dgk = norm_bwd(dk_scr[KPAD:, :], khat, rk, gk_ref)
        dq_ref[...] = dq.astype(BF16)
        dk_ref[...] = dk.astype(BF16)
        dv_ref[...] = dv_scr[KPAD:, :].astype(BF16)
        dbias_ref[0] = db_scr[...]
        dgq_ref[0] = dgq
        dgk_ref[0] = dgk

    def col(off):
        return pl.BlockSpec((seq, 128), lambda b, hp: (b, off + hp))

    vec = pl.BlockSpec((1, 128), lambda b, hp: (0, 0))
    gvec = pl.BlockSpec((1, 1, 128), lambda b, hp: (b * (ATTN_HEADS // 2) + hp, 0, 0))
    t = nb * seq
    return _pallas(
        body, name="attn_bwd", grid=(nb, ATTN_HEADS // 2),
        in_specs=[col(0), col(4), col(8), col(0), col(0),
                  pl.BlockSpec((BAND, 2 * CHUNK), lambda b, hp: (hp, 0)), vec, vec],
        out_specs=[col(0), col(0), col(0), pl.BlockSpec((1, BAND, 2 * CHUNK), lambda b, hp: (b, hp, 0)),
                   gvec, gvec],
        out_shape=[jax.ShapeDtypeStruct((t, ATTN_WIDTH), BF16)] * 3
        + [jax.ShapeDtypeStruct((nb, ATTN_HEADS // 2 * BAND, 2 * CHUNK), F32)]
        + [jax.ShapeDtypeStruct((nb * ATTN_HEADS // 2, 1, 128), F32)] * 2,
        scratch_shapes=[pltpu.VMEM((seq, 128), BF16), pltpu.VMEM((seq + KPAD, 128), BF16),
                        pltpu.VMEM((seq + KPAD, 128), BF16), pltpu.VMEM((seq, 128), F32),
                        pltpu.VMEM((seq + KPAD, 128), F32), pltpu.VMEM((seq + KPAD, 128), F32),
                        pltpu.VMEM((BAND, 2 * CHUNK), F32)],
        sem=("parallel", "parallel"), args=(proj, proj, proj, out, dout, bias, gq, gk))


def _tri(lower):
    r = lax.broadcasted_iota(jnp.int32, (CHUNK, CHUNK), 0)
    c = lax.broadcasted_iota(jnp.int32, (CHUNK, CHUNK), 1)
    return (r >= c) if lower else (r <= c)


def _hgrn_gates(hq, hf, lb):
    sq = _sigmoid(hq)
    sf = _sigmoid(hf)
    return hq * sq, sq, sf, lb + (1.0 - lb) * sf


def _hgrn_offdiag(q_s, k_s, b_s):
    row = lax.broadcasted_iota(jnp.int32, (CHUNK, HGRN_HEAD_DIM), 0)
    bv, qv, kv = b_s[...], q_s[...], k_s[...]
    eqs, eks = [], []
    for i in range(1, N_SUB):
        r = b_s[pl.ds(SUB * i - 1, 1), :]
        in_i = (row >= SUB * i) & (row < SUB * (i + 1))
        eqs.append(jnp.exp(jnp.where(in_i, bv - r, -jnp.inf)))
        eks.append(jnp.exp(jnp.where(row < SUB * i, r - bv, -jnp.inf)))
    eq = jnp.concatenate(eqs, axis=1)
    ek = jnp.concatenate(eks, axis=1)
    qt = jnp.concatenate([qv] * (N_SUB - 1), axis=1) * eq
    kt = jnp.concatenate([kv] * (N_SUB - 1), axis=1) * ek
    return qt, kt, eq, ek


def _hgrn_diag_e(b_s, i, s):
    t_io = lax.broadcasted_iota(jnp.int32, (SUB, HGRN_HEAD_DIM), 0)
    bi = b_s[pl.ds(SUB * i, SUB), :]
    return jnp.exp(jnp.where(t_io >= s, bi - b_s[pl.ds(SUB * i + s, 1), :], -jnp.inf)), t_io


def _hgrn_intra(q_s, k_s, b_s, a_s, qt, kt):
    ktp = jnp.concatenate([kt, jnp.zeros_like(kt)], axis=0)
    a_s[...] = _dot(qt, ktp, NT)
    yield
    col = lax.broadcasted_iota(jnp.int32, (SUB, HGRN_HEAD_DIM), 1)
    for i in range(N_SUB):
        qi = q_s[pl.ds(SUB * i, SUB), :]
        ai = jnp.zeros((SUB, HGRN_HEAD_DIM), F32)
        for s in range(SUB):
            e, _ = _hgrn_diag_e(b_s, i, s)
            a_col = jnp.sum(qi * k_s[pl.ds(SUB * i + s, 1), :] * e, axis=-1, keepdims=True)
            ai = ai + jnp.where(col == SUB * i + s, a_col, 0.0)
            if s % DIAG_STAGE == DIAG_STAGE - 1:
                yield
        a_s[pl.ds(SUB * i, SUB), :] += ai


def _mixer_fwd(proj, bias, gq, gk, lb, go, nb, seq):
    nc = seq // CHUNK
    hd = HGRN_HEAD_DIM
    nblk = ATTN_HEADS // 2
    rows_blk = seq // nblk
    nck = rows_blk // CHUNK
    per = nc // nck
    assert rows_blk % CHUNK == 0

    def body(aq_ref, ak_ref, av_ref, bias_ref, gq_ref, gk_ref, hq_ref, hf_ref, hi_ref, hg_ref, lb_ref, go_ref,
             ao_ref, y_ref, o_ref, st_ref, a_ref, qs_scr, k_scr, v_scr, st_all, q_all, k_all, b_all, a_all):
        _attn_prepare(aq_ref, ak_ref, av_ref, gq_ref, gk_ref, qs_scr, k_scr, v_scr)

        @pl.when(pl.program_id(1) == 0)
        def _():
            st_all[...] = jnp.zeros_like(st_all)

        lower = _tri(True)

        def head_chunk(hh, c, rows):
            ln = slice(hd * hh, hd * (hh + 1))
            st, q_s, k_s, b_s, a_s = st_all.at[hh], q_all.at[hh], k_all.at[hh], b_all.at[hh], a_all.at[hh]
            q, _, _, f = _hgrn_gates(hq_ref[rows, ln], hf_ref[rows, ln], lb_ref[:, ln])
            v = hi_ref[rows, ln]
            yield
            b = _dot_exact01(lower, jnp.log(f))
            q_s[...] = q
            k_s[...] = 1.0 - f
            b_s[...] = b
            st_ref[hh, c] = st[...]
            yield
            qt, kt, _, _ = _hgrn_offdiag(q_s, k_s, b_s)
            yield
            yield from _hgrn_intra(q_s, k_s, b_s, a_s, qt, kt)
            a16 = a_s[...].astype(BF16)
            a_ref[hh, c] = a16
            vp = jnp.concatenate([v, jnp.zeros_like(v)], axis=0)
            o = _dot(a16, vp) + _dot(q * jnp.exp(b), st[...], NT)
            yield
            bl = b_s[pl.ds(CHUNK - 1, 1), :]
            st[...] = st[...] * jnp.exp(bl) + _tn(v, (1.0 - f) * jnp.exp(bl - b))
            o_ref[rows, ln] = o
            yield
            n = o * lax.rsqrt(jnp.mean(o * o, axis=-1, keepdims=True) + RMS_EPS) * go_ref[...]
            hg = hg_ref[rows, ln]
            y_ref[rows, ln] = n * hg * _sigmoid(hg)

        def chunk(c, carry):
            rows = pl.ds(pl.multiple_of(c * CHUNK, CHUNK), CHUNK)
            _lockstep([_attn_fwd_chunk(c * per + a, qs_scr, k_scr, v_scr, bias_ref, ao_ref) for a in range(per)]
                      + [head_chunk(hh, c, rows) for hh in range(HGRN_HEADS)])
            return carry

        lax.fori_loop(0, nck, chunk, 0)

    hp, wide = HGRN_HEADS, HGRN_HEADS * hd

    def acol(off):
        return pl.BlockSpec((seq, 128), lambda b, s: (b, off + s))

    def col(off):
        return pl.BlockSpec((rows_blk, wide), lambda b, s: (b * nblk + s, off // hp))

    out = pl.BlockSpec((rows_blk, wide), lambda b, s: (b * nblk + s, 0))
    vec = pl.BlockSpec((1, 128), lambda b, s: (0, 0))
    t = nb * seq
    return _pallas(
        body, name="mixer_fwd", grid=(nb, nblk),
        in_specs=[acol(0), acol(4), acol(8), pl.BlockSpec((2 * CHUNK, BAND), lambda b, s: (s, 0)), vec, vec,
                  col(12), col(16), col(20), col(24), pl.BlockSpec((1, wide), lambda b, s: (0, 0)), vec],
        out_specs=[pl.BlockSpec((seq, 128), lambda b, s: (b, s)), out, out,
                   pl.BlockSpec((hp, nck, hd, hd), lambda b, s: (b, s, 0, 0)),
                   pl.BlockSpec((hp, nck, CHUNK, hd), lambda b, s: (b, s, 0, 0))],
        out_shape=[jax.ShapeDtypeStruct((t, ATTN_WIDTH), F32)] + [jax.ShapeDtypeStruct((t, wide), F32)] * 2
        + [jax.ShapeDtypeStruct((nb * hp, nc, hd, hd), F32), jax.ShapeDtypeStruct((nb * hp, nc, CHUNK, hd), BF16)],
        scratch_shapes=[pltpu.VMEM((seq, 128), BF16), pltpu.VMEM((seq + KPAD, 128), BF16),
                        pltpu.VMEM((seq + KPAD, 128), BF16), pltpu.VMEM((hp, hd, hd), F32)]
        + [pltpu.VMEM((hp, CHUNK, hd), F32)] * 4,
        sem=("parallel", "arbitrary"), args=(proj,) * 3 + (bias, gq, gk) + (proj,) * 4 + (lb, go))


def _hgrn_bwd(proj, lb, go, o_pre, states, scores, dout, nb, seq):
    nc = seq // CHUNK
    hd = HGRN_HEAD_DIM
    rows_blk = min(HGRN_ROWS, seq)
    nblk, nck = seq // rows_blk, rows_blk // CHUNK

    def body(hq_ref, hf_ref, hi_ref, hg_ref, lb_ref, go_ref, o_ref, st_ref, a_ref, dy_ref,
             dhq_ref, dhf_ref, dhi_ref, dhg_ref, dlb_ref, dgo_ref,
             dst_all, q_all, k_all, b_all, da_all, dqi_all, dki_all, dlb_all, dgo_all):
        @pl.when(pl.program_id(1) == 0)
        def _():
            dst_all[...] = jnp.zeros_like(dst_all)
            dlb_all[...] = jnp.zeros_like(dlb_all)
            dgo_all[...] = jnp.zeros_like(dgo_all)

        lower, upper = _tri(True), _tri(False)
        gov = go_ref[...]
        row = lax.broadcasted_iota(jnp.int32, (CHUNK, hd), 0)

        def head_chunk(hh, c, rows):
            ln = slice(hd * hh, hd * (hh + 1))
            dst, q_s, k_s, b_s = dst_all.at[hh], q_all.at[hh], k_all.at[hh], b_all.at[hh]
            da_s, dqi_s, dki_s = da_all.at[hh], dqi_all.at[hh], dki_all.at[hh]
            dlb_acc, dgo_acc = dlb_all.at[hh], dgo_all.at[hh]
            lbv = lb_ref[:, ln]
            hq, hf, v, hg = hq_ref[rows, ln], hf_ref[rows, ln], hi_ref[rows, ln], hg_ref[rows, ln]
            q, sq, sf, f = _hgrn_gates(hq, hf, lbv)
            kk = 1.0 - f
            yield
            b = _dot_exact01(lower, jnp.log(f))
            q_s[...] = q
            k_s[...] = kk
            b_s[...] = b
            yield
            bl = b_s[pl.ds(CHUNK - 1, 1), :]
            ebl = jnp.exp(bl)
            ekd = jnp.exp(bl - b)
            kd = kk * ekd
            eb = jnp.exp(b)
            qb = q * eb
            st0 = st_ref[hh, c]
            dst1 = dst[...]
            yield

            o = o_ref[rows, ln]
            dy = dy_ref[rows, ln]
            sg = _sigmoid(hg)
            rstd = lax.rsqrt(jnp.mean(o * o, axis=-1, keepdims=True) + RMS_EPS)
            ohat = o * rstd
            dn = dy * hg * sg
            dhg_ref[rows, ln] = (dy * ohat * gov * (sg * (1.0 + hg * (1.0 - sg)))).astype(BF16)
            dgo_acc[...] += jnp.sum(dn * ohat, axis=0, keepdims=True)
            gdn = dn * gov
            do = rstd * (gdn - ohat * jnp.mean(gdn * ohat, axis=-1, keepdims=True))
            yield

            qt, kt, eq, ek = _hgrn_offdiag(q_s, k_s, b_s)
            da = _dot(do, v, NT)
            dat = _dot(v, do, NT)
            da_s[...] = da
            yield
            dqo = _dot(da, kt) * eq
            dko = _dot(dat, qt) * ek
            dqi_s[...] = sum(dqo[:, j * hd:(j + 1) * hd] for j in range(N_SUB - 1))
            dki_s[...] = sum(dko[:, j * hd:(j + 1) * hd] for j in range(N_SUB - 1))
            yield
            col = lax.broadcasted_iota(jnp.int32, (SUB, CHUNK), 1)
            for i in range(N_SUB):
                qi = q_s[pl.ds(SUB * i, SUB), :]
                dai = da_s[pl.ds(SUB * i, SUB), :]
                dqd = jnp.zeros((SUB, hd), F32)
                for s in range(SUB):
                    e, _ = _hgrn_diag_e(b_s, i, s)
                    dacol = jnp.sum(jnp.where(col == SUB * i + s, dai, 0.0), axis=-1, keepdims=True)
                    w = dacol * e
                    dqd = dqd + w * k_s[pl.ds(SUB * i + s, 1), :]
                    dki_s[pl.ds(SUB * i + s, 1), :] += jnp.sum(w * qi, axis=0, keepdims=True)
                    if s % DIAG_STAGE == DIAG_STAGE - 1:
                        yield
                dqi_s[pl.ds(SUB * i, SUB), :] += dqd
            dqi, dki = dqi_s[...], dki_s[...]

            dv = _tn(a_ref[hh, c].astype(F32), do)[0:CHUNK, :] + _dot(kd, dst1, NT)
            dqb = _dot(do, st0)
            dkd = _dot(v, dst1)
            yield
            t2 = dkd * kd
            dq = dqb * eb + dqi
            dk = dkd * ekd + dki
            dbl = jnp.sum(t2, axis=0, keepdims=True) + ebl * jnp.sum(st0 * dst1, axis=0, keepdims=True)
            db = dqb * qb - t2 + q * dqi - kk * dki + jnp.where(row == CHUNK - 1, dbl, 0.0)
            yield
            dg = _dot_exact01(upper, db)
            dst[...] = dst1 * ebl + _tn(do, qb)
            yield

            df = dg / f - dk
            dhf_ref[rows, ln] = (df * (1.0 - lbv) * sf * (1.0 - sf)).astype(BF16)
            dlb_acc[...] += jnp.sum(df * (1.0 - sf), axis=0, keepdims=True)
            dhq_ref[rows, ln] = (dq * (sq * (1.0 + hq * (1.0 - sq)))).astype(BF16)
            dhi_ref[rows, ln] = dv.astype(BF16)

        def chunk(it, carry):
            c = nck - 1 - it
            rows = pl.ds(pl.multiple_of(c * CHUNK, CHUNK), CHUNK)
            _lockstep([head_chunk(hh, c, rows) for hh in range(HGRN_HEADS)])
            return carry

        lax.fori_loop(0, nck, chunk, 0)

        @pl.when(pl.program_id(1) == nblk - 1)
        def _():
            dlb_ref[...] = dlb_all[...]
            dgo_ref[...] = dgo_all[...]

    hp, wide = HGRN_HEADS, HGRN_HEADS * hd

    def col(off):
        return pl.BlockSpec((rows_blk, wide), lambda b, s: (b * nblk + nblk - 1 - s, off // hp))

    out = pl.BlockSpec((rows_blk, wide), lambda b, s: (b * nblk + nblk - 1 - s, 0))
    part = pl.BlockSpec((hp, 1, hd), lambda b, s: (b, 0, 0))
    t = nb * seq
    return pl.pallas_call(
        body, name="hgrn_bwd", grid=(nb, nblk),
        in_specs=[col(12), col(16), col(20), col(24), pl.BlockSpec((1, wide), lambda b, s: (0, 0)),
                  pl.BlockSpec((1, hd), lambda b, s: (0, 0)), out,
                  pl.BlockSpec((hp, nck, hd, hd), lambda b, s: (b, nblk - 1 - s, 0, 0)),
                  pl.BlockSpec((hp, nck, CHUNK, hd), lambda b, s: (b, nblk - 1 - s, 0, 0)), col(4)],
        out_specs=[out, out, out, out, part, part],
        out_shape=[jax.ShapeDtypeStruct((t, wide), BF16)] * 4 + [jax.ShapeDtypeStruct((nb * hp, 1, hd), F32)] * 2,
        scratch_shapes=[pltpu.VMEM((hp, hd, hd), F32)] + [pltpu.VMEM((hp, CHUNK, hd), F32)] * 3
        + [pltpu.VMEM((hp, CHUNK, CHUNK), F32)] + [pltpu.VMEM((hp, CHUNK, hd), F32)] * 2
        + [pltpu.VMEM((hp, 1, hd), F32)] * 2,
        compiler_params=_params("parallel", "arbitrary"),
    )(proj, proj, proj, proj, lb, go, o_pre, states, scores, dout)


def _lb_fwd(lower_bounds):
    def body(x_ref, o_ref):
        xv = x_ref[...]
        e = jnp.exp(xv - jnp.max(xv, axis=0, keepdims=True))
        o_ref[...] = e[0:1, :] / jnp.sum(e, axis=0, keepdims=True)

    return pl.pallas_call(body, name="lb_fwd",
                          out_shape=jax.ShapeDtypeStruct((1, lower_bounds.shape[1]), F32))(lower_bounds)


def _lb_bwd(lower_bounds, dlb_parts):
    ng = dlb_parts.shape[0]

    def body(x_ref, d_ref, o_ref):
        xv = x_ref[...]
        e = jnp.exp(xv - jnp.max(xv, axis=0, keepdims=True))
        p = e / jnp.sum(e, axis=0, keepdims=True)
        dlb = d_ref[0]
        for gi in range(1, ng):
            dlb = dlb + d_ref[gi]
        first = lax.broadcasted_iota(jnp.int32, xv.shape, 0) == 0
        o_ref[...] = p * (jnp.where(first, dlb, 0.0) - p[0:1, :] * dlb)

    return pl.pallas_call(body, name="lb_bwd",
                          out_shape=jax.ShapeDtypeStruct(lower_bounds.shape, F32))(lower_bounds, dlb_parts)


def _ffn_bwd(x, g, h, gate, up, dy, dy16, w, put, tag):
    wg, wu, wd = w[tag + "_w_gate"], w[tag + "_w_up"], w[tag + "_w_down"]
    dgate, dup, dwd = _ffn_bwd_mid(dy16, wd, gate, up, tag + "_bwd_mid")
    put(tag + "_w_down", dwd)
    put(tag + "_w_gate", _mm(dgate, h, ta=True, tm=1408, tn=512, name=tag + "_dwg"))
    put(tag + "_w_up", _mm(dup, h, ta=True, tm=1408, tn=512, name=tag + "_dwu"))
    dh = _mm(dgate, wg, tm=512, tn=1024, name=tag + "_dh_gate")
    return _mm(dup, wu, tm=512, tn=1024, add=dh, norm_bwd=(x, g, dy), name=tag + "_dh_up")


def _local_step(x, tgt, sp, w, put, nb, seq):
    d = x.shape[1]
    h1 = _rms_fwd(x, sp["ffn1_norm_g"], "ffn1_norm")
    rb_pad = jnp.pad(sp["attn_rel_bias"], ((0, 0), (0, N_REL_PAD - N_REL)))
    bias = jnp.transpose(_bias_expand(rb_pad), (1, 0, 2)).reshape(ATTN_HEADS * CHUNK, BAND)
    gq2 = jnp.concatenate([sp["attn_q_norm_g"]] * 2, axis=1)
    gk2 = jnp.concatenate([sp["attn_k_norm_g"]] * 2, axis=1)
    lb = _lb_fwd(sp["hgrn_lower_bounds"])
    gate1, up1, act1 = _ffn_up(h1, w["ffn1_w_gate"], w["ffn1_w_up"], "ffn1_up")
    x1, h2 = _mm(act1, w["ffn1_w_down"], tm=512, tn=d, add=x, scale=0.5, norm_g=sp["mix_norm_g"],
                 name="ffn1_down")
    proj = _mm(h2, w["w_in"], tb=True, tm=256, tn=w["w_in"].shape[0], name="in_proj")
    attn, hy, ho, hstate, hscore = _mixer_fwd(proj, bias, gq2, gk2, lb, sp["hgrn_out_norm_g"], nb, seq)
    mix = jnp.concatenate([attn, hy], axis=1)
    x2, h3 = _mm(mix, w["w_out"], tm=512, tn=1024, add=x1, norm_g=sp["ffn2_norm_g"], name="out_proj")
    gate2, up2, dx3, dx3_16, sq = _ffn_fwd(h3, x2, w["ffn2_w_gate"], w["ffn2_w_up"], w["ffn2_w_down"], "ffn2_fwd",
                                           tgt=tgt)
    loss = 0.5 * jnp.sum(sq) / d

    dx2, dx2_16, dg3 = _ffn_bwd(x2, sp["ffn2_norm_g"], h3, gate2, up2, dx3, dx3_16, w, put, "ffn2")
    dmix = _mm(dx2_16, w["w_out"], tb=True, tm=512, tn=1024, name="out_proj_dx")
    put("w_out", _mm(mix, dx2_16, ta=True, tm=512, tn=1024, name="out_proj_dw"))
    bias_t = jnp.transpose(bias.reshape(ATTN_HEADS // 2, 2 * CHUNK, BAND), (0, 2, 1)).reshape(-1, 2 * CHUNK)
    dq, dk, dv, dbias, dgq, dgk = _attn_bwd(proj, attn, dmix, bias_t, gq2, gk2, nb, seq)
    dbias = jnp.transpose(dbias.reshape(nb, ATTN_HEADS // 2, BAND, 2, CHUNK), (0, 4, 1, 3, 2))
    dbias = dbias.reshape(nb, CHUNK, ATTN_HEADS, BAND)
    dgq = jnp.sum(dgq, axis=(0, 1)).reshape(2, ATTN_HEAD_DIM).sum(axis=0, keepdims=True)
    dgk = jnp.sum(dgk, axis=(0, 1)).reshape(2, ATTN_HEAD_DIM).sum(axis=0, keepdims=True)
    dhq, dhf, dhi, dhg, dlb, dgo = _hgrn_bwd(proj, lb, sp["hgrn_out_norm_g"], ho, hstate, hscore, dmix, nb, seq)
    dproj = jnp.concatenate([dq, dk, dv, dhq, dhf, dhi, dhg], axis=1)
    put("w_in", _mm(dproj, h2, ta=True, tm=512, tn=1024, name="in_proj_dw"))
    dx1, dx1_16, dgm = _mm(dproj, w["w_in"], tm=512, tn=1024, norm_bwd=(x1, sp["mix_norm_g"], dx2),
                           name="in_proj_dx")
    dx0, _, dg1 = _ffn_bwd(x, sp["ffn1_norm_g"], h1, gate1, up1, dx1, dx1_16, w, put, "ffn1")

    small = {
        "ffn1_norm_g": dg1, "mix_norm_g": dgm, "ffn2_norm_g": dg3,
        "attn_q_norm_g": dgq, "attn_k_norm_g": dgk,
        "attn_rel_bias": _bias_fold(dbias)[:, :N_REL],
        "hgrn_lower_bounds": _lb_bwd(sp["hgrn_lower_bounds"], dlb.reshape(nb, 1, HGRN_HEADS * HGRN_HEAD_DIM)),
        "hgrn_out_norm_g": jnp.sum(dgo, axis=(0, 1))[None, :],
    }
    return loss, dx0, small


MESH = pl.DeviceIdType.MESH
ANY = pl.BlockSpec(memory_space=pl.ANY)


def _coords():
    return lax.axis_index("x"), lax.axis_index("y"), lax.axis_index("c")


def _other_chips(x, y):
    return [(1 - x, y), (x, 1 - y), (1 - x, 1 - y)]


def _gather_side(shards):
    n = len(shards)

    def copies(ins, outs, sems):
        send_sems, recv_sems, local_sems = sems
        x, y, c = _coords()
        xn, yn, dg = (1 - x, y), (x, 1 - y), (1 - x, 1 - y)

        def copy(i, k, block, to, half=None, src=None):
            bx, by, bc = block
            dst = outs[i].at[4 * bx + 2 * by + bc]
            if half is not None:
                rows = shards[i].shape[0] // 2
                dst = dst.at[pl.ds(half * rows, rows)]
            return pltpu.make_async_remote_copy(
                src_ref=dst if src is None else src, dst_ref=dst, send_sem=send_sems.at[i, k],
                recv_sem=recv_sems.at[i, k], device_id=to, device_id_type=MESH)

        mine = [pltpu.make_async_copy(ins[i], outs[i].at[4 * x + 2 * y + c], local_sems.at[i]) for i in range(n)]
        return copy, mine, (x, y, c), (x, y, 1 - c), xn, yn, dg, c

    def own(copy, i, ins, me, sibling, xn, yn, c):
        return [copy(i, 0, me, sibling, src=ins[i]), copy(i, 1, me, (*xn, c), src=ins[i]),
                copy(i, 2, me, (*yn, c), src=ins[i])]

    def passed_on(copy, i, sibling, xn, yn, c):
        return [copy(i, 3, (*xn, c), sibling), copy(i, 5, (*xn, c), (*yn, c), half=0),
                copy(i, 4, (*yn, c), sibling), copy(i, 6, (*yn, c), (*xn, c), half=1)]

    def diagonal(copy, i, sibling, dg, c):
        return [copy(i, 7, (*dg, c), sibling, half=0), copy(i, 8, (*dg, c), sibling, half=1)]

    def start(ins, outs, sems):
        copy, mine, me, sibling, xn, yn, dg, c = copies(ins, outs, sems)
        for cp in mine + [cp for i in range(n) for cp in own(copy, i, ins, me, sibling, xn, yn, c)]:
            cp.start()

    def middle(ins, outs, sems):
        copy, mine, me, sibling, xn, yn, dg, c = copies(ins, outs, sems)
        for i in range(n):
            fwd_x, relay_x, fwd_y, relay_y = passed_on(copy, i, sibling, xn, yn, c)
            copy(i, 1, (*xn, c), me).wait_recv()
            fwd_x.start()
            relay_x.start()
            copy(i, 2, (*yn, c), me).wait_recv()
            fwd_y.start()
            relay_y.start()

    def finish(ins, outs, sems):
        copy, mine, me, sibling, xn, yn, dg, c = copies(ins, outs, sems)
        for i in range(n):
            top, bottom = diagonal(copy, i, sibling, dg, c)
            copy(i, 5, (*dg, c), me, half=0).wait_recv()
            top.start()
            copy(i, 6, (*dg, c), me, half=1).wait_recv()
            bottom.start()
        for i in range(n):
            copy(i, 0, sibling, me).wait_recv()
            copy(i, 3, (*xn, 1 - c), me).wait_recv()
            copy(i, 4, (*yn, 1 - c), me).wait_recv()
            copy(i, 7, (*dg, 1 - c), me, half=0).wait_recv()
            copy(i, 8, (*dg, 1 - c), me, half=1).wait_recv()
        for i in range(n):
            for cp in (own(copy, i, ins, me, sibling, xn, yn, c) + passed_on(copy, i, sibling, xn, yn, c)
                       + diagonal(copy, i, sibling, dg, c)):
                cp.wait_send()
        for cp in mine:
            cp.wait()

    return _Side(list(shards), [jax.ShapeDtypeStruct((N_DEV,) + s.shape, s.dtype) for s in shards],
                 [pltpu.SemaphoreType.DMA((n, 9)), pltpu.SemaphoreType.DMA((n, 9)), pltpu.SemaphoreType.DMA((n,))],
                 start, finish, middle)


def _pair_side(grads):
    n = len(grads)

    def copies(ins, outs, sems):
        send_sems, recv_sems = sems
        x, y, c = _coords()
        return [pltpu.make_async_remote_copy(
            src_ref=ins[i].at[2 * k + 1 - c], dst_ref=outs[i].at[k], send_sem=send_sems.at[i, k],
            recv_sem=recv_sems.at[i, k], device_id=(x, y, 1 - c), device_id_type=MESH)
            for i in range(n) for k in range(4)]

    def start(ins, outs, sems):
        for cp in copies(ins, outs, sems):
            cp.start()

    def finish(ins, outs, sems):
        for cp in copies(ins, outs, sems):
            cp.wait()

    return _Side(list(grads), [jax.ShapeDtypeStruct((4,) + g.shape[1:], g.dtype) for g in grads],
                 [pltpu.SemaphoreType.DMA((n, 4)), pltpu.SemaphoreType.DMA((n, 4))], start, finish)


def _pair_add(grads, recvs, core, name):
    count = len(grads)
    _, r, cdim = grads[0].shape

    def body(c_ref, *refs):
        for n in range(count):
            refs[2 * count + n][...] = (refs[2 * n][...] + refs[2 * n + 1][...]).astype(BF16)

    blk = (1, r, cdim)
    out = pl.pallas_call(
        body, name=name,
        grid_spec=pltpu.PrefetchScalarGridSpec(
            num_scalar_prefetch=1, grid=(4,),
            in_specs=[pl.BlockSpec(blk, lambda k, c_ref: (2 * k + c_ref[0], 0, 0)),
                      pl.BlockSpec(blk, lambda k, c_ref: (k, 0, 0))] * count,
            out_specs=[pl.BlockSpec(blk, lambda k, c_ref: (k, 0, 0))] * count),
        out_shape=[jax.ShapeDtypeStruct((4, r, cdim), BF16)] * count,
        compiler_params=_params("arbitrary"),
    )(core, *[a for pair in zip(grads, recvs) for a in pair])
    return list(out)


def _chip_side(parts):
    n = len(parts)

    def copies(ins, outs, sems):
        send_sems, recv_sems, local_sems = sems
        x, y, c = _coords()
        chips = _other_chips(x, y)
        mine = [pltpu.make_async_copy(ins[i].at[2 * x + y], outs[i].at[2 * x + y], local_sems.at[i])
                for i in range(n)]
        sent = [pltpu.make_async_remote_copy(
            src_ref=ins[i].at[2 * px + py], dst_ref=outs[i].at[2 * x + y], send_sem=send_sems.at[i, j],
            recv_sem=recv_sems.at[i, j], device_id=(px, py, c), device_id_type=MESH)
            for i in range(n) for j, (px, py) in enumerate(chips)]
        return mine, sent, chips, c

    def start(ins, outs, sems):
        mine, sent, _, _ = copies(ins, outs, sems)
        for cp in mine + sent:
            cp.start()

    def finish(ins, outs, sems):
        mine, sent, chips, c = copies(ins, outs, sems)
        send_sems, recv_sems, _ = sems
        for i in range(n):
            for j, (px, py) in enumerate(chips):
                landed = outs[i].at[2 * px + py]
                pltpu.make_async_remote_copy(
                    src_ref=landed, dst_ref=landed, send_sem=send_sems.at[i, j], recv_sem=recv_sems.at[i, j],
                    device_id=(px, py, c), device_id_type=MESH).wait_recv()
        for cp in sent:
            cp.wait_send()
        for cp in mine:
            cp.wait()

    return _Side(list(parts), [jax.ShapeDtypeStruct(p.shape, p.dtype) for p in parts],
                 [pltpu.SemaphoreType.DMA((n, 3)), pltpu.SemaphoreType.DMA((n, 3)), pltpu.SemaphoreType.DMA((n,))],
                 start, finish)


def _all_reduce_small(v):
    r = v.shape[0]

    def body(v_ref, o_ref, buf, send_sems, recv_sems):
        x, y, c = _coords()
        me = 4 * x + 2 * y + c
        buf[me] = v_ref[...]
        cps = []
        for k in range(1, N_DEV):
            px = 1 - x if k & 4 else x
            py = 1 - y if k & 2 else y
            pc = 1 - c if k & 1 else c
            cps.append((pltpu.make_async_remote_copy(
                src_ref=v_ref, dst_ref=buf.at[me], send_sem=send_sems.at[k - 1], recv_sem=recv_sems.at[k - 1],
                device_id=(px, py, pc), device_id_type=MESH), 4 * px + 2 * py + pc))
        for cp, _ in cps:
            cp.start()
        for k, (cp, peer) in enumerate(cps):
            pltpu.make_async_remote_copy(
                src_ref=v_ref, dst_ref=buf.at[peer], send_sem=send_sems.at[k], recv_sem=recv_sems.at[k],
                device_id=(x, y, c), device_id_type=MESH).wait_recv()
        for cp, _ in cps:
            cp.wait_send()
        acc = buf[0]
        for j in range(1, N_DEV):
            acc = acc + buf[j]
        o_ref[...] = acc

    return pl.pallas_call(
        body, name="small_all_reduce", out_shape=jax.ShapeDtypeStruct(v.shape, F32),
        in_specs=[pl.BlockSpec(memory_space=pltpu.VMEM)], out_specs=pl.BlockSpec(memory_space=pltpu.VMEM),
        scratch_shapes=[pltpu.VMEM((N_DEV, r, 128), F32), pltpu.SemaphoreType.DMA((N_DEV - 1,)),
                        pltpu.SemaphoreType.DMA((N_DEV - 1,))],
    )(v)


def _adamw(ws, ms, vs, gs, name):
    count = len(ws)
    parts = ws[0].ndim == 3
    r, cdim = ws[0].shape[-2:]
    tr = r // 4 if r % 32 == 0 else r

    def body(*refs):
        for n in range(count):
            w_ref, m_ref, v_ref, g_ref = refs[4 * n:4 * n + 4]
            go_ref, d_ref, mo_ref, vo_ref = refs[4 * count + 4 * n:4 * count + 4 * n + 4]
            if parts:
                gv = g_ref[0].astype(F32)
                for k in range(1, 4):
                    gv = gv + g_ref[k].astype(F32)
                gv = gv[None]
            else:
                gv = g_ref[...]
            m2 = ADAM_B1 * m_ref[...] + (1.0 - ADAM_B1) * gv
            v2 = ADAM_B2 * v_ref[...] + (1.0 - ADAM_B2) * (gv * gv)
            m_hat = m2 / (1.0 - ADAM_B1 ** ADAM_STEP)
            v_hat = v2 / (1.0 - ADAM_B2 ** ADAM_STEP)
            go_ref[...] = gv
            d_ref[...] = -ADAM_LR * (m_hat / (jnp.sqrt(v_hat) + ADAM_EPS) + ADAM_WD * w_ref[...])
            mo_ref[...] = m2
            vo_ref[...] = v2

    if parts:
        row = pl.BlockSpec((1, tr, cdim), lambda i: (0, i, 0))
        g_spec = pl.BlockSpec((4, tr, cdim), lambda i: (0, i, 0))
    else:
        row = g_spec = pl.BlockSpec((tr, cdim), lambda i: (i, 0))
    args = [a for group in zip(ws, ms, vs, gs) for a in group]
    out = pl.pallas_call(
        body, name=name, grid=(r // tr,), in_specs=[row, row, row, g_spec] * count, out_specs=[row] * (4 * count),
        out_shape=[jax.ShapeDtypeStruct(ws[0].shape, F32)] * (4 * count),
        compiler_params=_params("parallel"),
    )(*args)
    return [out[4 * n:4 * n + 4] for n in range(count)]


WEIGHTS = ["ffn1_norm_g", "ffn1_w_gate", "ffn1_w_up", "ffn1_w_down", "mix_norm_g", "w_in", "attn_q_norm_g",
           "attn_k_norm_g", "attn_rel_bias", "hgrn_lower_bounds", "hgrn_out_norm_g", "w_out", "ffn2_norm_g",
           "ffn2_w_gate", "ffn2_w_up", "ffn2_w_down"]
COL_SHARDED = ("ffn1_w_gate", "ffn1_w_up", "w_in", "ffn2_w_gate", "ffn2_w_up")
ROW_SHARDED = ("ffn1_w_down", "w_out", "ffn2_w_down")
BIG = [n for n in WEIGHTS if n in COL_SHARDED or n in ROW_SHARDED]
SMALL = [n for n in WEIGHTS if n not in BIG]
PACK_ROWS = 8
FFN2 = ["ffn2_w_down", "ffn2_w_gate", "ffn2_w_up"]
MIXER = ["w_out", "w_in"]

PLAN = {
    "ffn1_norm": [("gather", ["ffn1_w_gate"])],
    "bias_expand": [("gather", ["ffn1_w_up"])],
    "ffn1_up": [("gather", ["ffn1_w_down", "w_out"])],
    "ffn1_down": [("gather", ["w_in"])],
    "mixer_fwd": [("gather", FFN2)],
    "ffn2_dh_gate": [("pair", FFN2)],
    "attn_bwd": [("chip", FFN2)],
    "in_proj_dx": [("pair", MIXER)],
    "ffn1_bwd_mid": [("chip", MIXER)],
    "ffn1_dwg": [("pair", ["ffn1_w_down"])],
    "ffn1_dwu": [("chip", ["ffn1_w_down"]), ("pair", ["ffn1_w_gate"])],
    "ffn1_dh_gate": [("chip", ["ffn1_w_gate"]), ("pair", ["ffn1_w_up"])],
    "bias_fold": [("chip", ["ffn1_w_up"])],
}


def _join_sides(sides):
    def split(refs, counts):
        out, at = [], 0
        for n in counts:
            out.append(refs[at:at + n])
            at += n
        return out

    n_in, n_out, n_sem = ([len(getattr(s, f)) for s in sides] for f in ("ins", "out_shape", "sems"))

    def run(which):
        def go(ins, outs, sems):
            for s, i, o, m in zip(sides, split(ins, n_in), split(outs, n_out), split(sems, n_sem)):
                if getattr(s, which) is not None:
                    getattr(s, which)(i, o, m)
        return go

    return _Side([a for s in sides for a in s.ins], [a for s in sides for a in s.out_shape],
                 [a for s in sides for a in s.sems], run("start"), run("finish"),
                 run("middle") if any(s.middle is not None for s in sides) else None)


class _Schedule:
    def __init__(self, shards):
        self.shards = shards
        self.weights = {}
        self.sliced = {}
        self.partials = {}
        self.reduced = {}

    def put(self, name, grad):
        self.sliced[name] = grad.reshape((N_DEV,) + self.shards[name].shape)

    def side_for(self, call):
        if call not in PLAN:
            return None
        sides = []
        for kind, names in PLAN[call]:
            if kind == "gather":
                sides.append(_gather_side([self.shards[n] for n in names]))
            elif kind == "pair":
                sides.append(_pair_side([self.sliced[n] for n in names]))
            else:
                sides.append(_chip_side([self.partials[n] for n in names]))
        return _join_sides(sides)

    def done(self, call, outs):
        at = 0
        for kind, names in PLAN[call]:
            self.file(kind, names, outs[at:at + len(names)])
            at += len(names)

    def file(self, kind, names, outs):
        if kind == "pair":
            core = lax.axis_index("c").astype(jnp.int32).reshape(1)
            for shape in dict.fromkeys(o.shape for o in outs):
                group = [(n, o) for n, o in zip(names, outs) if o.shape == shape]
                sums = _pair_add([self.sliced[n] for n, _ in group], [o for _, o in group], core,
                                 group[0][0] + "_pair_add")
                self.partials.update({n: s for (n, _), s in zip(group, sums)})
            return
        for n, o in zip(names, outs):
            if kind == "gather":
                self.weights[n] = o.reshape(N_DEV * o.shape[1], o.shape[2])
            else:
                self.reduced[n] = o


def _pack_small(vals, loss=None):
    parts = []
    for n in SMALL:
        a = vals[n]
        if n == "attn_rel_bias":
            a = jnp.pad(a.reshape(ATTN_HEADS, N_REL), ((0, 0), (0, N_REL_PAD - N_REL)))
        flat = a.reshape(-1)
        size = -(-flat.shape[0] // (PACK_ROWS * 128)) * PACK_ROWS * 128
        parts.append(jnp.pad(flat, (0, size - flat.shape[0])).reshape(-1, 128))
    tail = jnp.zeros((PACK_ROWS, 128), F32)
    if loss is not None:
        tail = tail.at[0, 0].set(loss)
    return jnp.concatenate(parts + [tail], axis=0)


def _unpack_small(packed, shapes):
    out, row = {}, 0
    for n in SMALL:
        shape = shapes[n]
        if n == "attn_rel_bias":
            rows = ATTN_HEADS * N_REL_PAD // 128
            out[n] = packed[row:row + rows].reshape(ATTN_HEADS, N_REL_PAD)[:, :N_REL].reshape(shape)
        else:
            size = 1
            for s in shape:
                size *= s
            rows = -(-size // (PACK_ROWS * 128)) * PACK_ROWS
            out[n] = packed[row:row + rows].reshape(-1)[:size].reshape(shape)
        row += rows
    return out, packed[row, 0]


def kernel(x, ffn1_norm_g, ffn1_w_gate, ffn1_w_up, ffn1_w_down, mix_norm_g, w_in, attn_q_norm_g, attn_k_norm_g, attn_rel_bias, hgrn_lower_bounds, hgrn_out_norm_g, w_out, ffn2_norm_g, ffn2_w_gate, ffn2_w_up, ffn2_w_down, loss_target, m_ffn1_norm_g, m_ffn1_w_gate, m_ffn1_w_up, m_ffn1_w_down, m_mix_norm_g, m_w_in, m_attn_q_norm_g, m_attn_k_norm_g, m_attn_rel_bias, m_hgrn_lower_bounds, m_hgrn_out_norm_g, m_w_out, m_ffn2_norm_g, m_ffn2_w_gate, m_ffn2_w_up, m_ffn2_w_down, v_ffn1_norm_g, v_ffn1_w_gate, v_ffn1_w_up, v_ffn1_w_down, v_mix_norm_g, v_w_in, v_attn_q_norm_g, v_attn_k_norm_g, v_attn_rel_bias, v_hgrn_lower_bounds, v_hgrn_out_norm_g, v_w_out, v_ffn2_norm_g, v_ffn2_w_gate, v_ffn2_w_up, v_ffn2_w_down):
    wts = dict(zip(WEIGHTS, (ffn1_norm_g, ffn1_w_gate, ffn1_w_up, ffn1_w_down, mix_norm_g, w_in, attn_q_norm_g,
                             attn_k_norm_g, attn_rel_bias, hgrn_lower_bounds, hgrn_out_norm_g, w_out, ffn2_norm_g,
                             ffn2_w_gate, ffn2_w_up, ffn2_w_down)))
    mom = dict(zip(WEIGHTS, (m_ffn1_norm_g, m_ffn1_w_gate, m_ffn1_w_up, m_ffn1_w_down, m_mix_norm_g, m_w_in,
                             m_attn_q_norm_g, m_attn_k_norm_g, m_attn_rel_bias, m_hgrn_lower_bounds,
                             m_hgrn_out_norm_g, m_w_out, m_ffn2_norm_g, m_ffn2_w_gate, m_ffn2_w_up, m_ffn2_w_down)))
    var = dict(zip(WEIGHTS, (v_ffn1_norm_g, v_ffn1_w_gate, v_ffn1_w_up, v_ffn1_w_down, v_mix_norm_g, v_w_in,
                             v_attn_q_norm_g, v_attn_k_norm_g, v_attn_rel_bias, v_hgrn_lower_bounds,
                             v_hgrn_out_norm_g, v_w_out, v_ffn2_norm_g, v_ffn2_w_gate, v_ffn2_w_up, v_ffn2_w_down)))
    nb, seq, d = x.shape
    shapes = {n: wts[n].shape for n in WEIGHTS}

    def rows_first(a, n):
        return jnp.swapaxes(a, 1, 2) if n in COL_SHARDED else a

    sched = _Schedule({n: rows_first(wts[n], n)[0].astype(BF16) for n in BIG})
    sp = {n: wts[n] for n in SMALL}
    sp["attn_rel_bias"] = wts["attn_rel_bias"][0]
    _ACTIVE[0] = sched
    try:
        loss, dx, dsmall = _local_step(x.reshape(nb * seq, d), loss_target.reshape(nb * seq, d), sp,
                                       sched.weights, sched.put, nb, seq)
    finally:
        _ACTIVE[0] = None
    reduced = sched.reduced

    small_sum = _all_reduce_small(_pack_small(dsmall, loss))
    gsmall, loss_total = _unpack_small(small_sum, shapes)

    grads, deltas, new_m, new_v = {}, {}, {}, {}
    ffn = [n for n in BIG if n.startswith("ffn")]
    for group, tag in ((ffn, "ffn_adamw"), (["w_in"], "w_in_adamw"), (["w_out"], "w_out_adamw")):
        outs = _adamw([rows_first(wts[n], n) for n in group], [rows_first(mom[n], n) for n in group],
                      [rows_first(var[n], n) for n in group], [reduced[n] for n in group], tag)
        for n, out in zip(group, outs):
            grads[n], deltas[n], new_m[n], new_v[n] = (rows_first(o, n) for o in out)
    packed = _adamw([_pack_small(wts)], [_pack_small(mom)], [_pack_small(var)], [small_sum], "small_adamw")[0]
    for dst, p in zip((deltas, new_m, new_v), packed[1:]):
        dst.update(_unpack_small(p, shapes)[0])
    grads.update(gsmall)

    return (loss_total, dx.reshape(nb, seq, d), *[grads[n] for n in WEIGHTS], *[deltas[n] for n in WEIGHTS],
            *[new_m[n] for n in WEIGHTS], *[new_v[n] for n in WEIGHTS])
```

```python
import functools

import jax
import jax.numpy as jnp
from jax import lax
from jax.experimental import pallas as pl
from jax.experimental.pallas import tpu as pltpu

F32 = jnp.float32
BF16 = jnp.bfloat16

RMS_EPS = 1e-6
CHUNK = 64
LEFT_CHUNKS = 8
BAND = (LEFT_CHUNKS + 2) * CHUNK
KPAD = BAND - CHUNK
REL_CLIP = 128
N_REL = 2 * REL_CLIP + 1
N_REL_PAD = 384
ATTN_HEADS = 8
ATTN_HEAD_DIM = 64
ATTN_WIDTH = ATTN_HEADS * ATTN_HEAD_DIM
ATTN_LOCKSTEP = 4
ATTN_UNROLL = 8
HGRN_HEADS = 4
HGRN_HEAD_DIM = 128
HGRN_ROWS = 512
SUB = 16
N_SUB = CHUNK // SUB
DIAG_STAGE = 4
N_DEV = 8

ADAM_LR = 0.001
ADAM_B1 = 0.9
ADAM_B2 = 0.999
ADAM_EPS = 1e-08
ADAM_WD = 0.01
ADAM_STEP = 10

VMEM_LIMIT = 56 * 1024 * 1024

NT = (((1,), (1,)), ((), ()))
NN = (((1,), (0,)), ((), ()))


def _params(*sem):
    return pltpu.CompilerParams(dimension_semantics=sem, vmem_limit_bytes=VMEM_LIMIT)


def _sigmoid(v):
    return 0.5 * jnp.tanh(0.5 * v) + 0.5


def _dot(a, b, dims=NN):
    return lax.dot_general(a.astype(BF16), b.astype(BF16), dims, preferred_element_type=F32)


def _dot_exact01(m01, v):
    m = m01.astype(BF16)
    hi = v.astype(BF16)
    r1 = v - hi.astype(F32)
    mid = r1.astype(BF16)
    lo = (r1 - mid.astype(F32)).astype(BF16)
    out = lax.dot_general(m, hi, NN, preferred_element_type=F32)
    out = out + lax.dot_general(m, mid, NN, preferred_element_type=F32)
    return out + lax.dot_general(m, lo, NN, preferred_element_type=F32)


def _dot_exact01_r(v, m01):
    m = m01.astype(BF16)
    hi = v.astype(BF16)
    r1 = v - hi.astype(F32)
    mid = r1.astype(BF16)
    lo = (r1 - mid.astype(F32)).astype(BF16)
    out = lax.dot_general(hi, m, NN, preferred_element_type=F32)
    out = out + lax.dot_general(mid, m, NN, preferred_element_type=F32)
    return out + lax.dot_general(lo, m, NN, preferred_element_type=F32)


def _lockstep(stages):
    live = list(stages)
    while live:
        still = []
        for g in live:
            try:
                next(g)
                still.append(g)
            except StopIteration:
                pass
        live = still


def _row_sums_on_lanes(v):
    ones = jnp.ones((8, v.shape[1]), BF16)
    hi = v.astype(BF16)
    r1 = v - hi.astype(F32)
    mid = r1.astype(BF16)
    lo = (r1 - mid.astype(F32)).astype(BF16)
    out = lax.dot_general(ones, hi, NT, preferred_element_type=F32)
    out = out + lax.dot_general(ones, mid, NT, preferred_element_type=F32)
    return (out + lax.dot_general(ones, lo, NT, preferred_element_type=F32))[0:1, :]


def _tn(a, b):
    ap = jnp.concatenate([a, jnp.zeros_like(a)], axis=0)
    bp = jnp.concatenate([b, jnp.zeros_like(b)], axis=0)
    return _dot(ap.T, bp)


def _row_tile(t):
    for tm in (512, 256, 128, 64, 32, 16, 8):
        if t % tm == 0:
            return tm
    raise ValueError(t)


class _Side:
    def __init__(self, ins, out_shape, sems, start, finish, middle=None):
        self.ins, self.out_shape, self.sems = ins, out_shape, sems
        self.start, self.middle, self.finish = start, middle, finish


_ACTIVE = [None]


def _pallas(body, *, name, grid, in_specs, out_specs, out_shape, scratch_shapes=(), sem, args):
    sched = _ACTIVE[0]
    side = sched.side_for(name) if sched is not None else None
    if side is None:
        return pl.pallas_call(
            body, name=name, grid=grid, in_specs=list(in_specs), out_specs=list(out_specs),
            out_shape=list(out_shape), scratch_shapes=list(scratch_shapes), compiler_params=_params(*sem))(*args)
    cuts = [len(in_specs), len(side.ins), len(out_shape), len(side.out_shape), len(scratch_shapes)]

    def with_side(*refs):
        groups, at = [], 0
        for n in cuts:
            groups.append(refs[at:at + n])
            at += n
        ins, side_ins, outs, side_outs, scratch = groups
        side_sems = refs[at:]
        step, total = pl.program_id(0), grid[0]
        for a in range(1, len(grid)):
            step, total = step * grid[a] + pl.program_id(a), total * grid[a]
        has_middle = side.middle is not None and total >= 3

        @pl.when(step == 0)
        def _():
            side.start(side_ins, side_outs, side_sems)

        if has_middle:
            @pl.when(step == total // 2)
            def _():
                side.middle(side_ins, side_outs, side_sems)

        body(*ins, *outs, *scratch)

        @pl.when(step == total - 1)
        def _():
            if side.middle is not None and not has_middle:
                side.middle(side_ins, side_outs, side_sems)
            side.finish(side_ins, side_outs, side_sems)

    hbm = pl.BlockSpec(memory_space=pl.ANY)
    res = pl.pallas_call(
        with_side, name=name, grid=grid, in_specs=list(in_specs) + [hbm] * len(side.ins),
        out_specs=list(out_specs) + [hbm] * len(side.out_shape), out_shape=list(out_shape) + list(side.out_shape),
        scratch_shapes=list(scratch_shapes) + list(side.sems),
        compiler_params=_params(*(["arbitrary"] * len(grid))))(*args, *side.ins)
    sched.done(name, res[len(out_shape):])
    return res[:len(out_shape)]


def _rms_fwd(x, g, name):
    t, d = x.shape
    tm = _row_tile(t)

    def body(x_ref, g_ref, h_ref):
        xv = x_ref[...]
        r = lax.rsqrt(jnp.mean(xv * xv, axis=-1, keepdims=True) + RMS_EPS)
        h_ref[...] = (xv * r * g_ref[...]).astype(BF16)

    return _pallas(
        body, name=name, grid=(t // tm,),
        in_specs=[pl.BlockSpec((tm, d), lambda i: (i, 0)), pl.BlockSpec((1, d), lambda i: (0, 0))],
        out_specs=[pl.BlockSpec((tm, d), lambda i: (i, 0))], out_shape=[jax.ShapeDtypeStruct((t, d), BF16)],
        sem=("parallel",), args=(x, g))[0]


def _accumulate(ref, part, step):
    @pl.when(step == 0)
    def _():
        ref[...] = part

    @pl.when(step > 0)
    def _():
        ref[...] += part


def _mm(a, b, *, ta=False, tb=False, tm, tn, out_dtype=F32, add=None, scale=1.0, norm_g=None, norm_bwd=None, name):
    m, k = (a.shape[1], a.shape[0]) if ta else a.shape
    n = b.shape[0] if tb else b.shape[1]
    tm, tn = min(tm, m), min(tn, n)
    assert m % tm == 0 and n % tn == 0, (m, n, tm, tn)
    assert (norm_g is None and norm_bwd is None) or tn == n
    dims = (((0 if ta else 1,), (1 if tb else 0,)), ((), ()))
    n_in = 2 + (add is not None) + (norm_g is not None) + (3 if norm_bwd is not None else 0)

    def body(*refs):
        ins, outs = list(refs[2:n_in]), refs[n_in:]
        r = lax.dot_general(refs[0][...].astype(BF16), refs[1][...].astype(BF16), dims, preferred_element_type=F32)
        if scale != 1.0:
            r = r * scale
        if add is not None:
            r = r + ins.pop(0)[...]
        if norm_bwd is not None:
            xv, gv, dres = (ref[...] for ref in ins)
            rs = lax.rsqrt(jnp.mean(xv * xv, axis=-1, keepdims=True) + RMS_EPS)
            xhat = xv * rs
            gd = r * gv
            dx = dres + rs * (gd - xhat * jnp.mean(gd * xhat, axis=-1, keepdims=True))
            outs[0][...] = dx
            outs[1][...] = dx.astype(BF16)
            _accumulate(outs[2], jnp.sum(r * xhat, axis=0, keepdims=True), pl.program_id(0))
            return
        outs[0][...] = r.astype(out_dtype)
        if norm_g is not None:
            rs = lax.rsqrt(jnp.mean(r * r, axis=-1, keepdims=True) + RMS_EPS)
            outs[1][...] = (r * rs * ins.pop(0)[...]).astype(BF16)

    a_spec = pl.BlockSpec((k, tm), lambda i, j: (0, i)) if ta else pl.BlockSpec((tm, k), lambda i, j: (i, 0))
    b_spec = pl.BlockSpec((tn, k), lambda i, j: (j, 0)) if tb else pl.BlockSpec((k, tn), lambda i, j: (0, j))
    o_spec = pl.BlockSpec((tm, tn), lambda i, j: (i, j))
    vec = pl.BlockSpec((1, tn), lambda i, j: (0, j))
    args, specs = [a, b], [a_spec, b_spec]
    if add is not None:
        args.append(add)
        specs.append(o_spec)
    out_specs, out_shape = [o_spec], [jax.ShapeDtypeStruct((m, n), out_dtype)]
    if norm_g is not None:
        args.append(norm_g)
        specs.append(vec)
        out_specs.append(o_spec)
        out_shape.append(jax.ShapeDtypeStruct((m, n), BF16))
    if norm_bwd is not None:
        args += list(norm_bwd)
        specs += [o_spec, vec, o_spec]
        out_specs = [o_spec, o_spec, vec]
        out_shape = [jax.ShapeDtypeStruct((m, n), F32), jax.ShapeDtypeStruct((m, n), BF16),
                     jax.ShapeDtypeStruct((1, n), F32)]
    res = _pallas(body, name=name, grid=(m // tm, n // tn), in_specs=specs, out_specs=out_specs, out_shape=out_shape,
                  sem=("arbitrary", "arbitrary") if norm_bwd is not None else ("parallel", "parallel"), args=args)
    return res[0] if len(res) == 1 else res


def _ffn_tile(f):
    for tf in (1408, 512, 256, 128):
        if f % tf == 0:
            return tf
    raise ValueError(f)


def _ffn_fwd(h, x, wg, wu, wd, name, next_g=None, tgt=None):
    t, d = x.shape
    f = wg.shape[0]
    tm, tf = _row_tile(t), _ffn_tile(f)
    nf = f // tf
    assert (next_g is None) != (tgt is None)

    def body(h_ref, x_ref, wg_ref, wu_ref, wd_ref, tail_ref, g_ref, u_ref, o0_ref, o1_ref, *rest):
        acc_ref = rest[-1]
        j = pl.program_id(1)
        hv = h_ref[...]
        gv = lax.dot_general(hv, wg_ref[...], NT, preferred_element_type=F32)
        uv = lax.dot_general(hv, wu_ref[...], NT, preferred_element_type=F32)
        av = gv * _sigmoid(gv) * uv
        g_ref[...] = gv.astype(BF16)
        u_ref[...] = uv.astype(BF16)
        _accumulate(acc_ref, lax.dot_general(av.astype(BF16), wd_ref[...], NN, preferred_element_type=F32), j)

        @pl.when(j == nf - 1)
        def _():
            y = x_ref[...] + 0.5 * acc_ref[...]
            if tgt is None:
                o0_ref[...] = y
                rs = lax.rsqrt(jnp.mean(y * y, axis=-1, keepdims=True) + RMS_EPS)
                o1_ref[...] = (y * rs * tail_ref[...]).astype(BF16)
            else:
                e = y - tail_ref[...]
                dy = e * (1.0 / d)
                o0_ref[...] = dy
                o1_ref[...] = dy.astype(BF16)
                _accumulate(rest[0], jnp.sum(e * e, axis=0, keepdims=True), pl.program_id(0))

    row = pl.BlockSpec((tm, d), lambda i, j: (i, 0))
    hid = pl.BlockSpec((tm, tf), lambda i, j: (i, j))
    vec = pl.BlockSpec((1, d), lambda i, j: (0, 0))
    out_specs = [hid, hid, row, row] + ([vec] if tgt is not None else [])
    out_shape = [jax.ShapeDtypeStruct((t, f), BF16)] * 2 + [jax.ShapeDtypeStruct((t, d), F32),
                                                            jax.ShapeDtypeStruct((t, d), BF16)]
    if tgt is not None:
        out_shape.append(jax.ShapeDtypeStruct((1, d), F32))
    return _pallas(
        body, name=name, grid=(t // tm, nf),
        in_specs=[row, row] + [pl.BlockSpec((tf, d), lambda i, j: (j, 0))] * 3 + [vec if tgt is None else row],
        out_specs=out_specs, out_shape=out_shape, scratch_shapes=[pltpu.VMEM((tm, d), F32)],
        sem=("parallel" if tgt is None else "arbitrary", "arbitrary"),
        args=(h, x, wg, wu, wd, next_g if tgt is None else tgt))


def _ffn_up(h, wg, wu, name):
    t, d = h.shape
    f = wg.shape[0]
    tm, tf = _row_tile(t), _ffn_tile(f)

    def body(h_ref, wg_ref, wu_ref, g_ref, u_ref, a_ref):
        hv = h_ref[...]
        gv = lax.dot_general(hv, wg_ref[...], NT, preferred_element_type=F32)
        uv = lax.dot_general(hv, wu_ref[...], NT, preferred_element_type=F32)
        g_ref[...] = gv.astype(BF16)
        u_ref[...] = uv.astype(BF16)
        a_ref[...] = (gv * _sigmoid(gv) * uv).astype(BF16)

    hid = pl.BlockSpec((tm, tf), lambda i, j: (i, j))
    wrow = pl.BlockSpec((tf, d), lambda i, j: (j, 0))
    return _pallas(
        body, name=name, grid=(t // tm, f // tf), in_specs=[pl.BlockSpec((tm, d), lambda i, j: (i, 0)), wrow, wrow],
        out_specs=[hid, hid, hid], out_shape=[jax.ShapeDtypeStruct((t, f), BF16)] * 3,
        sem=("parallel", "parallel"), args=(h, wg, wu))


def _ffn_bwd_mid(dy, wd, g, u, name):
    t, d = dy.shape
    f = wd.shape[0]
    tm, tf = _row_tile(t), _ffn_tile(f)

    def body(dy_ref, wd_ref, g_ref, u_ref, dg_ref, du_ref, dwd_ref):
        dy16 = dy_ref[...]
        da = 0.5 * lax.dot_general(dy16, wd_ref[...], NT, preferred_element_type=F32)
        gv = g_ref[...].astype(F32)
        uv = u_ref[...].astype(F32)
        s = _sigmoid(gv)
        silu = gv * s
        dg_ref[...] = (da * uv * (s * (1.0 + gv * (1.0 - s)))).astype(BF16)
        du_ref[...] = (da * silu).astype(BF16)
        part = 0.5 * lax.dot_general((silu * uv).astype(BF16), dy16, (((0,), (0,)), ((), ())),
                                     preferred_element_type=F32)
        _accumulate(dwd_ref, part, pl.program_id(1))

    hid = pl.BlockSpec((tm, tf), lambda j, i: (i, j))
    wrow = pl.BlockSpec((tf, d), lambda j, i: (j, 0))
    return _pallas(
        body, name=name, grid=(f // tf, t // tm),
        in_specs=[pl.BlockSpec((tm, d), lambda j, i: (i, 0)), wrow, hid, hid],
        out_specs=[hid, hid, wrow],
        out_shape=[jax.ShapeDtypeStruct((t, f), BF16)] * 2 + [jax.ShapeDtypeStruct((f, d), F32)],
        sem=("parallel", "arbitrary"), args=(dy, wd, g, u))


def _rel_index(t, s_band):
    return jnp.clip(t + KPAD - s_band, -REL_CLIP, REL_CLIP) + REL_CLIP


def _bias_expand(rel_bias_pad):
    nh = rel_bias_pad.shape[0]

    def body(rb_ref, out_ref):
        rb = rb_ref[...]
        i_io = lax.broadcasted_iota(jnp.int32, (N_REL_PAD, BAND), 0)
        s_io = lax.broadcasted_iota(jnp.int32, (N_REL_PAD, BAND), 1)

        def row(r, carry):
            onehot = (i_io == _rel_index(pl.program_id(0) * rows + r, s_io)).astype(F32)
            out_ref[r] = _dot_exact01_r(rb, onehot)
            return carry

        lax.fori_loop(0, rows, row, 0)

    rows = 8
    return _pallas(
        body, name="bias_expand", grid=(CHUNK // rows,),
        in_specs=[pl.BlockSpec(rel_bias_pad.shape, lambda i: (0, 0))],
        out_specs=[pl.BlockSpec((rows, nh, BAND), lambda i: (i, 0, 0))],
        out_shape=[jax.ShapeDtypeStruct((CHUNK, nh, BAND), F32)], sem=("arbitrary",), args=(rel_bias_pad,))[0]


def _bias_fold(dbias):
    ng, nh = dbias.shape[0], dbias.shape[2]

    def body(db_ref, out_ref):
        s_io = lax.broadcasted_iota(jnp.int32, (BAND, N_REL_PAD), 0)
        i_io = lax.broadcasted_iota(jnp.int32, (BAND, N_REL_PAD), 1)

        def row(t, acc):
            onehot = (i_io == _rel_index(t, s_io)).astype(F32)
            d = db_ref[0, t]
            for gi in range(1, ng):
                d = d + db_ref[gi, t]
            return acc + _dot_exact01_r(d, onehot)

        out_ref[...] = lax.fori_loop(0, CHUNK, row, jnp.zeros((nh, N_REL_PAD), F32))

    return _pallas(
        body, name="bias_fold", grid=(1,), in_specs=[pl.BlockSpec(dbias.shape, lambda i: (0, 0, 0, 0))],
        out_specs=[pl.BlockSpec((nh, N_REL_PAD), lambda i: (0, 0))],
        out_shape=[jax.ShapeDtypeStruct((nh, N_REL_PAD), F32)], sem=("arbitrary",), args=(dbias,))[0]


def _left_half(shape):
    return lax.broadcasted_iota(jnp.int32, shape, len(shape) - 1) < ATTN_HEAD_DIM


def _stack_heads(v):
    left = _left_half(v.shape)
    zero = jnp.zeros_like(v)
    return jnp.concatenate([jnp.where(left, v, zero), jnp.where(left, zero, v)], axis=0)


def _unstack_heads(v):
    return jnp.where(_left_half((CHUNK, 128)), v[0:CHUNK, :], v[CHUNK:2 * CHUNK, :])


def _half_mean(v):
    r = lax.broadcasted_iota(jnp.int32, (128, 128), 0) < ATTN_HEAD_DIM
    c = lax.broadcasted_iota(jnp.int32, (128, 128), 1) < ATTN_HEAD_DIM
    return _dot_exact01_r(v, r == c) * (1.0 / ATTN_HEAD_DIM)


def _attn_prepare(q_ref, k_ref, v_ref, gq_ref, gk_ref, qs_scr, k_scr, v_scr):
    q, k = q_ref[...], k_ref[...]
    rq = lax.rsqrt(_half_mean(q * q) + RMS_EPS)
    rk = lax.rsqrt(_half_mean(k * k) + RMS_EPS)
    qhat, khat = q * rq, k * rk
    qs_scr[...] = (qhat * gq_ref[...] * ATTN_HEAD_DIM ** -0.5).astype(BF16)
    k_scr[0:KPAD, :] = jnp.zeros((KPAD, 128), BF16)
    v_scr[0:KPAD, :] = jnp.zeros((KPAD, 128), BF16)
    k_scr[KPAD:, :] = (khat * gk_ref[...]).astype(BF16)
    v_scr[KPAD:, :] = v_ref[...].astype(BF16)
    return qhat, rq, khat, rk


def _first_key(c):
    return jnp.maximum(CHUNK, (LEFT_CHUNKS + 1 - c) * CHUNK)


def _attn_fwd_chunk(c, qs_scr, k_scr, v_scr, bias_ref, o_ref):
    r0 = pl.multiple_of(c * CHUNK, CHUNK)
    s = lax.dot_general(_stack_heads(qs_scr[pl.ds(r0, CHUNK), :]), k_scr[pl.ds(r0, BAND), :], NT,
                        preferred_element_type=F32)
    yield
    col = lax.broadcasted_iota(jnp.int32, (2 * CHUNK, BAND), 1)
    s = jnp.where(col >= _first_key(c), s + bias_ref[...], -jnp.inf)
    m = jnp.max(s, axis=-1, keepdims=True)
    yield
    e = jnp.exp(s - m)
    yield
    inv = 1.0 / jnp.sum(e, axis=-1, keepdims=True)
    o = lax.dot_general(e.astype(BF16), v_scr[pl.ds(r0, BAND), :], NN, preferred_element_type=F32)
    yield
    o_ref[pl.ds(r0, CHUNK), :] = _unstack_heads(o * inv)


def _attn_bwd(proj, out, dout, bias, gq, gk, nb, seq):
    nc = seq // CHUNK
    lock = min(ATTN_LOCKSTEP, nc)
    assert nc % lock == 0
    scale = ATTN_HEAD_DIM ** -0.5

    def body(q_ref, k_ref, v_ref, o_ref, do_ref, bias_ref, gq_ref, gk_ref,
             dq_ref, dk_ref, dv_ref, dbias_ref, dgq_ref, dgk_ref,
             qs_scr, k_scr, v_scr, dqn_scr, dk_scr, dv_scr, db_scr):
        qhat, rq, khat, rk = _attn_prepare(q_ref, k_ref, v_ref, gq_ref, gk_ref, qs_scr, k_scr, v_scr)
        dk_scr[...] = jnp.zeros_like(dk_scr)
        dv_scr[...] = jnp.zeros_like(dv_scr)
        db_scr[...] = jnp.zeros_like(db_scr)

        def one_chunk(c):
            r0 = pl.multiple_of(c * CHUNK, CHUNK)
            qst = _stack_heads(qs_scr[pl.ds(r0, CHUNK), :])
            kb = k_scr[pl.ds(r0, BAND), :]
            vb = v_scr[pl.ds(r0, BAND), :]
            st = lax.dot_general(kb, qst, NT, preferred_element_type=F32) + bias_ref[...]
            dost = _stack_heads(do_ref[pl.ds(r0, CHUNK), :])
            dost16 = dost.astype(BF16)
            dpt = lax.dot_general(vb, dost16, NT, preferred_element_type=F32)
            yield
            key = lax.broadcasted_iota(jnp.int32, (BAND, 2 * CHUNK), 0)
            st = jnp.where(key >= _first_key(c), st, -jnp.inf)
            mx = jnp.max(st, axis=0, keepdims=True)
            drow = _row_sums_on_lanes(dost * _stack_heads(o_ref[pl.ds(r0, CHUNK), :]))
            yield
            et = jnp.exp(st - mx)
            yield
            pt = et * (1.0 / jnp.sum(et, axis=0, keepdims=True))
            yield
            dst = pt * (dpt - drow)
            dst16 = dst.astype(BF16)
            yield
            db_scr[...] += dst
            dqn_scr[pl.ds(r0, CHUNK), :] = scale * _unstack_heads(_dot(dst.T, kb))
            yield
            dk_scr[pl.ds(r0, BAND), :] += lax.dot_general(dst16, qst, NN, preferred_element_type=F32)
            yield
            dv_scr[pl.ds(r0, BAND), :] += lax.dot_general(pt.astype(BF16), dost16, NN, preferred_element_type=F32)

        def chunk(i, carry):
            _lockstep([one_chunk(i * lock + a) for a in range(lock)])
            return carry

        lax.fori_loop(0, nc // lock, chunk, 0, unroll=max(1, min(ATTN_UNROLL, nc) // lock))

        def norm_bwd(dn, hat, r, g_ref):
            gd = dn * g_ref[...]
            return r * (gd - hat * _half_mean(gd * hat)), jnp.sum(dn * hat, axis=0, keepdims=True)

        dq, dgq = norm_bwd(dqn_scr[...], qhat, rq, gq_ref)
        dk, dgk = norm_bwd(dk_scr[KPAD:, :], khat, rk, gk_ref)
        dq_ref[...] = dq.astype(BF16)
        dk_ref[...] = dk.astype(BF16)
        dv_ref[...] = dv_scr[KPAD:, :].astype(BF16)
        dbias_ref[0] = db_scr[...]
        dgq_ref[0] = dgq
        dgk_ref[0] = dgk

    def col(off):
        return pl.BlockSpec((seq, 128), lambda b, hp: (b, off + hp))

    vec = pl.BlockSpec((1, 128), lambda b, hp: (0, 0))
    gvec = pl.BlockSpec((1, 1, 128), lambda b, hp: (b * (ATTN_HEADS // 2) + hp, 0, 0))
    t = nb * seq
    return _pallas(
        body, name="attn_bwd", grid=(nb, ATTN_HEADS // 2),
        in_specs=[col(0), col(4), col(8), col(0), col(0),
                  pl.BlockSpec((BAND, 2 * CHUNK), lambda b, hp: (hp, 0)), vec, vec],
        out_specs=[col(0), col(0), col(0), pl.BlockSpec((1, BAND, 2 * CHUNK), lambda b, hp: (b, hp, 0)),
                   gvec, gvec],
        out_shape=[jax.ShapeDtypeStruct((t, ATTN_WIDTH), BF16)] * 3
        + [jax.ShapeDtypeStruct((nb, ATTN_HEADS // 2 * BAND, 2 * CHUNK), F32)]
        + [jax.ShapeDtypeStruct((nb * ATTN_HEADS // 2, 1, 128), F32)] * 2,
        scratch_shapes=[pltpu.VMEM((seq, 128), BF16), pltpu.VMEM((seq + KPAD, 128), BF16),
                        pltpu.VMEM((seq + KPAD, 128), BF16), pltpu.VMEM((seq, 128), F32),
                        pltpu.VMEM((seq + KPAD, 128), F32), pltpu.VMEM((seq + KPAD, 128), F32),
                        pltpu.VMEM((BAND, 2 * CHUNK), F32)],
        sem=("parallel", "parallel"), args=(proj, proj, proj, out, dout, bias, gq, gk))


def _tri(lower):
    r = lax.broadcasted_iota(jnp.int32, (CHUNK, CHUNK), 0)
    c = lax.broadcasted_iota(jnp.int32, (CHUNK, CHUNK), 1)
    return (r >= c) if lower else (r <= c)


def _hgrn_gates(hq, hf, lb):
    sq = _sigmoid(hq)
    sf = _sigmoid(hf)
    return hq * sq, sq, sf, lb + (1.0 - lb) * sf


def _hgrn_offdiag(q_s, k_s, b_s):
    row = lax.broadcasted_iota(jnp.int32, (CHUNK, HGRN_HEAD_DIM), 0)
    bv, qv, kv = b_s[...], q_s[...], k_s[...]
    eqs, eks = [], []
    for i in range(1, N_SUB):
        r = b_s[pl.ds(SUB * i - 1, 1), :]
        in_i = (row >= SUB * i) & (row < SUB * (i + 1))
        eqs.append(jnp.exp(jnp.where(in_i, bv - r, -jnp.inf)))
        eks.append(jnp.exp(jnp.where(row < SUB * i, r - bv, -jnp.inf)))
    eq = jnp.concatenate(eqs, axis=1)
    ek = jnp.concatenate(eks, axis=1)
    qt = jnp.concatenate([qv] * (N_SUB - 1), axis=1) * eq
    kt = jnp.concatenate([kv] * (N_SUB - 1), axis=1) * ek
    return qt, kt, eq, ek


def _hgrn_diag_e(b_s, i, s):
    t_io = lax.broadcasted_iota(jnp.int32, (SUB, HGRN_HEAD_DIM), 0)
    bi = b_s[pl.ds(SUB * i, SUB), :]
    return jnp.exp(jnp.where(t_io >= s, bi - b_s[pl.ds(SUB * i + s, 1), :], -jnp.inf)), t_io


def _hgrn_intra(q_s, k_s, b_s, a_s, qt, kt):
    ktp = jnp.concatenate([kt, jnp.zeros_like(kt)], axis=0)
    a_s[...] = _dot(qt, ktp, NT)
    yield
    col = lax.broadcasted_iota(jnp.int32, (SUB, HGRN_HEAD_DIM), 1)
    for i in range(N_SUB):
        qi = q_s[pl.ds(SUB * i, SUB), :]
        ai = jnp.zeros((SUB, HGRN_HEAD_DIM), F32)
        for s in range(SUB):
            e, _ = _hgrn_diag_e(b_s, i, s)
            a_col = jnp.sum(qi * k_s[pl.ds(SUB * i + s, 1), :] * e, axis=-1, keepdims=True)
            ai = ai + jnp.where(col == SUB * i + s, a_col, 0.0)
            if s % DIAG_STAGE == DIAG_STAGE - 1:
                yield
        a_s[pl.ds(SUB * i, SUB), :] += ai


def _mixer_fwd(proj, bias, gq, gk, lb, go, nb, seq):
    nc = seq // CHUNK
    hd = HGRN_HEAD_DIM
    nblk = ATTN_HEADS // 2
    rows_blk = seq // nblk
    nck = rows_blk // CHUNK
    per = nc // nck
    assert rows_blk % CHUNK == 0

    def body(aq_ref, ak_ref, av_ref, bias_ref, gq_ref, gk_ref, hq_ref, hf_ref, hi_ref, hg_ref, lb_ref, go_ref,
             ao_ref, y_ref, o_ref, st_ref, a_ref, qs_scr, k_scr, v_scr, st_all, q_all, k_all, b_all, a_all):
        _attn_prepare(aq_ref, ak_ref, av_ref, gq_ref, gk_ref, qs_scr, k_scr, v_scr)

        @pl.when(pl.program_id(1) == 0)
        def _():
            st_all[...] = jnp.zeros_like(st_all)

        lower = _tri(True)

        def head_chunk(hh, c, rows):
            ln = slice(hd * hh, hd * (hh + 1))
            st, q_s, k_s, b_s, a_s = st_all.at[hh], q_all.at[hh], k_all.at[hh], b_all.at[hh], a_all.at[hh]
            q, _, _, f = _hgrn_gates(hq_ref[rows, ln], hf_ref[rows, ln], lb_ref[:, ln])
            v = hi_ref[rows, ln]
            yield
            b = _dot_exact01(lower, jnp.log(f))
            q_s[...] = q
            k_s[...] = 1.0 - f
            b_s[...] = b
            st_ref[hh, c] = st[...]
            yield
            qt, kt, _, _ = _hgrn_offdiag(q_s, k_s, b_s)
            yield
            yield from _hgrn_intra(q_s, k_s, b_s, a_s, qt, kt)
            a16 = a_s[...].astype(BF16)
            a_ref[hh, c] = a16
            vp = jnp.concatenate([v, jnp.zeros_like(v)], axis=0)
            o = _dot(a16, vp) + _dot(q * jnp.exp(b), st[...], NT)
            yield
            bl = b_s[pl.ds(CHUNK - 1, 1), :]
            st[...] = st[...] * jnp.exp(bl) + _tn(v, (1.0 - f) * jnp.exp(bl - b))
            o_ref[rows, ln] = o
            yield
            n = o * lax.rsqrt(jnp.mean(o * o, axis=-1, keepdims=True) + RMS_EPS) * go_ref[...]
            hg = hg_ref[rows, ln]
            y_ref[rows, ln] = n * hg * _sigmoid(hg)

        def chunk(c, carry):
            rows = pl.ds(pl.multiple_of(c * CHUNK, CHUNK), CHUNK)
            _lockstep([_attn_fwd_chunk(c * per + a, qs_scr, k_scr, v_scr, bias_ref, ao_ref) for a in range(per)]
                      + [head_chunk(hh, c, rows) for hh in range(HGRN_HEADS)])
            return carry

        lax.fori_loop(0, nck, chunk, 0)

    hp, wide = HGRN_HEADS, HGRN_HEADS * hd

    def acol(off):
        return pl.BlockSpec((seq, 128), lambda b, s: (b, off + s))

    def col(off):
        return pl.BlockSpec((rows_blk, wide), lambda b, s: (b * nblk + s, off // hp))

    out = pl.BlockSpec((rows_blk, wide), lambda b, s: (b * nblk + s, 0))
    vec = pl.BlockSpec((1, 128), lambda b, s: (0, 0))
    t = nb * seq
    return _pallas(
        body, name="mixer_fwd", grid=(nb, nblk),
        in_specs=[acol(0), acol(4), acol(8), pl.BlockSpec((2 * CHUNK, BAND), lambda b, s: (s, 0)), vec, vec,
                  col(12), col(16), col(20), col(24), pl.BlockSpec((1, wide), lambda b, s: (0, 0)), vec],
        out_specs=[pl.BlockSpec((seq, 128), lambda b, s: (b, s)), out, out,
                   pl.BlockSpec((hp, nck, hd, hd), lambda b, s: (b, s, 0, 0)),
                   pl.BlockSpec((hp, nck, CHUNK, hd), lambda b, s: (b, s, 0, 0))],
        out_shape=[jax.ShapeDtypeStruct((t, ATTN_WIDTH), F32)] + [jax.ShapeDtypeStruct((t, wide), F32)] * 2
        + [jax.ShapeDtypeStruct((nb * hp, nc, hd, hd), F32), jax.ShapeDtypeStruct((nb * hp, nc, CHUNK, hd), BF16)],
        scratch_shapes=[pltpu.VMEM((seq, 128), BF16), pltpu.VMEM((seq + KPAD, 128), BF16),
                        pltpu.VMEM((seq + KPAD, 128), BF16), pltpu.VMEM((hp, hd, hd), F32)]
        + [pltpu.VMEM((hp, CHUNK, hd), F32)] * 4,
        sem=("parallel", "arbitrary"), args=(proj,) * 3 + (bias, gq, gk) + (proj,) * 4 + (lb, go))


def _hgrn_bwd(proj, lb, go, o_pre, states, scores, dout, nb, seq):
    nc = seq // CHUNK
    hd = HGRN_HEAD_DIM
    rows_blk = min(HGRN_ROWS, seq)
    nblk, nck = seq // rows_blk, rows_blk // CHUNK

    def body(hq_ref, hf_ref, hi_ref, hg_ref, lb_ref, go_ref, o_ref, st_ref, a_ref, dy_ref,
             dhq_ref, dhf_ref, dhi_ref, dhg_ref, dlb_ref, dgo_ref,
             dst_all, q_all, k_all, b_all, da_all, dqi_all, dki_all, dlb_all, dgo_all):
        @pl.when(pl.program_id(1) == 0)
        def _():
            dst_all[...] = jnp.zeros_like(dst_all)
            dlb_all[...] = jnp.zeros_like(dlb_all)
            dgo_all[...] = jnp.zeros_like(dgo_all)

        lower, upper = _tri(True), _tri(False)
        gov = go_ref[...]
        row = lax.broadcasted_iota(jnp.int32, (CHUNK, hd), 0)

        def head_chunk(hh, c, rows):
            ln = slice(hd * hh, hd * (hh + 1))
            dst, q_s, k_s, b_s = dst_all.at[hh], q_all.at[hh], k_all.at[hh], b_all.at[hh]
            da_s, dqi_s, dki_s = da_all.at[hh], dqi_all.at[hh], dki_all.at[hh]
            dlb_acc, dgo_acc = dlb_all.at[hh], dgo_all.at[hh]
            lbv = lb_ref[:, ln]
            hq, hf, v, hg = hq_ref[rows, ln], hf_ref[rows, ln], hi_ref[rows, ln], hg_ref[rows, ln]
            q, sq, sf, f = _hgrn_gates(hq, hf, lbv)
            kk = 1.0 - f
            yield
            b = _dot_exact01(lower, jnp.log(f))
            q_s[...] = q
            k_s[...] = kk
            b_s[...] = b
            yield
            bl = b_s[pl.ds(CHUNK - 1, 1), :]
            ebl = jnp.exp(bl)
            ekd = jnp.exp(bl - b)
            kd = kk * ekd
            eb = jnp.exp(b)
            qb = q * eb
            st0 = st_ref[hh, c]
            dst1 = dst[...]
            yield

            o = o_ref[rows, ln]
            dy = dy_ref[rows, ln]
            sg = _sigmoid(hg)
            rstd = lax.rsqrt(jnp.mean(o * o, axis=-1, keepdims=True) + RMS_EPS)
            ohat = o * rstd
            dn = dy * hg * sg
            dhg_ref[rows, ln] = (dy * ohat * gov * (sg * (1.0 + hg * (1.0 - sg)))).astype(BF16)
            dgo_acc[...] += jnp.sum(dn * ohat, axis=0, keepdims=True)
            gdn = dn * gov
            do = rstd * (gdn - ohat * jnp.mean(gdn * ohat, axis=-1, keepdims=True))
            yield

            qt, kt, eq, ek = _hgrn_offdiag(q_s, k_s, b_s)
            da = _dot(do, v, NT)
            dat = _dot(v, do, NT)
            da_s[...] = da
            yield
            dqo = _dot(da, kt) * eq
            dko = _dot(dat, qt) * ek
            dqi_s[...] = sum(dqo[:, j * hd:(j + 1) * hd] for j in range(N_SUB - 1))
            dki_s[...] = sum(dko[:, j * hd:(j + 1) * hd] for j in range(N_SUB - 1))
            yield
            col = lax.broadcasted_iota(jnp.int32, (SUB, CHUNK), 1)
            for i in range(N_SUB):
                qi = q_s[pl.ds(SUB * i, SUB), :]
                dai = da_s[pl.ds(SUB * i, SUB), :]
                dqd = jnp.zeros((SUB, hd), F32)
                for s in range(SUB):
                    e, _ = _hgrn_diag_e(b_s, i, s)
                    dacol = jnp.sum(jnp.where(col == SUB * i + s, dai, 0.0), axis=-1, keepdims=True)
                    w = dacol * e
                    dqd = dqd + w * k_s[pl.ds(SUB * i + s, 1), :]
                    dki_s[pl.ds(SUB * i + s, 1), :] += jnp.sum(w * qi, axis=0, keepdims=True)
                    if s % DIAG_STAGE == DIAG_STAGE - 1:
                        yield
                dqi_s[pl.ds(SUB * i, SUB), :] += dqd
            dqi, dki = dqi_s[...], dki_s[...]

            dv = _tn(a_ref[hh, c].astype(F32), do)[0:CHUNK, :] + _dot(kd, dst1, NT)
            dqb = _dot(do, st0)
            dkd = _dot(v, dst1)
            yield
            t2 = dkd * kd
            dq = dqb * eb + dqi
            dk = dkd * ekd + dki
            dbl = jnp.sum(t2, axis=0, keepdims=True) + ebl * jnp.sum(st0 * dst1, axis=0, keepdims=True)
            db = dqb * qb - t2 + q * dqi - kk * dki + jnp.where(row == CHUNK - 1, dbl, 0.0)
            yield
            dg = _dot_exact01(upper, db)
            dst[...] = dst1 * ebl + _tn(do, qb)
            yield

            df = dg / f - dk
            dhf_ref[rows, ln] = (df * (1.0 - lbv) * sf * (1.0 - sf)).astype(BF16)
            dlb_acc[...] += jnp.sum(df * (1.0 - sf), axis=0, keepdims=True)
            dhq_ref[rows, ln] = (dq * (sq * (1.0 + hq * (1.0 - sq)))).astype(BF16)
            dhi_ref[rows, ln] = dv.astype(BF16)

        def chunk(it, carry):
            c = nck - 1 - it
            rows = pl.ds(pl.multiple_of(c * CHUNK, CHUNK), CHUNK)
            _lockstep([head_chunk(hh, c, rows) for hh in range(HGRN_HEADS)])
            return carry

        lax.fori_loop(0, nck, chunk, 0)

        @pl.when(pl.program_id(1) == nblk - 1)
        def _():
            dlb_ref[...] = dlb_all[...]
            dgo_ref[...] = dgo_all[...]

    hp, wide = HGRN_HEADS, HGRN_HEADS * hd

    def col(off):
        return pl.BlockSpec((rows_blk, wide), lambda b, s: (b * nblk + nblk - 1 - s, off // hp))

    out = pl.BlockSpec((rows_blk, wide), lambda b, s: (b * nblk + nblk - 1 - s, 0))
    part = pl.BlockSpec((hp, 1, hd), lambda b, s: (b, 0, 0))
    t = nb * seq
    return pl.pallas_call(
        body, name="hgrn_bwd", grid=(nb, nblk),
        in_specs=[col(12), col(16), col(20), col(24), pl.BlockSpec((1, wide), lambda b, s: (0, 0)),
                  pl.BlockSpec((1, hd), lambda b, s: (0, 0)), out,
                  pl.BlockSpec((hp, nck, hd, hd), lambda b, s: (b, nblk - 1 - s, 0, 0)),
                  pl.BlockSpec((hp, nck, CHUNK, hd), lambda b, s: (b, nblk - 1 - s, 0, 0)), col(4)],
        out_specs=[out, out, out, out, part, part],
        out_shape=[jax.ShapeDtypeStruct((t, wide), BF16)] * 4 + [jax.ShapeDtypeStruct((nb * hp, 1, hd), F32)] * 2,
        scratch_shapes=[pltpu.VMEM((hp, hd, hd), F32)] + [pltpu.VMEM((hp, CHUNK, hd), F32)] * 3
        + [pltpu.VMEM((hp, CHUNK, CHUNK), F32)] + [pltpu.VMEM((hp, CHUNK, hd), F32)] * 2
        + [pltpu.VMEM((hp, 1, hd), F32)] * 2,
        compiler_params=_params("parallel", "arbitrary"),
    )(proj, proj, proj, proj, lb, go, o_pre, states, scores, dout)


def _lb_fwd(lower_bounds):
    def body(x_ref, o_ref):
        xv = x_ref[...]
        e = jnp.exp(xv - jnp.max(xv, axis=0, keepdims=True))
        o_ref[...] = e[0:1, :] / jnp.sum(e, axis=0, keepdims=True)

    return pl.pallas_call(body, name="lb_fwd",
                          out_shape=jax.ShapeDtypeStruct((1, lower_bounds.shape[1]), F32))(lower_bounds)


def _lb_bwd(lower_bounds, dlb_parts):
    ng = dlb_parts.shape[0]

    def body(x_ref, d_ref, o_ref):
        xv = x_ref[...]
        e = jnp.exp(xv - jnp.max(xv, axis=0, keepdims=True))
        p = e / jnp.sum(e, axis=0, keepdims=True)
        dlb = d_ref[0]
        for gi in range(1, ng):
            dlb = dlb + d_ref[gi]
        first = lax.broadcasted_iota(jnp.int32, xv.shape, 0) == 0
        o_ref[...] = p * (jnp.where(first, dlb, 0.0) - p[0:1, :] * dlb)

    return pl.pallas_call(body, name="lb_bwd",
                          out_shape=jax.ShapeDtypeStruct(lower_bounds.shape, F32))(lower_bounds, dlb_parts)


def _ffn_bwd(x, g, h, gate, up, dy, dy16, w, put, tag):
    wg, wu, wd = w[tag + "_w_gate"], w[tag + "_w_up"], w[tag + "_w_down"]
    dgate, dup, dwd = _ffn_bwd_mid(dy16, wd, gate, up, tag + "_bwd_mid")
    put(tag + "_w_down", dwd)
    put(tag + "_w_gate", _mm(dgate, h, ta=True, tm=1408, tn=512, name=tag + "_dwg"))
    put(tag + "_w_up", _mm(dup, h, ta=True, tm=1408, tn=512, name=tag + "_dwu"))
    dh = _mm(dgate, wg, tm=512, tn=1024, name=tag + "_dh_gate")
    return _mm(dup, wu, tm=512, tn=1024, add=dh, norm_bwd=(x, g, dy), name=tag + "_dh_up")


def _local_step(x, tgt, sp, w, put, nb, seq):
    d = x.shape[1]
    h1 = _rms_fwd(x, sp["ffn1_norm_g"], "ffn1_norm")
    rb_pad = jnp.pad(sp["attn_rel_bias"], ((0, 0), (0, N_REL_PAD - N_REL)))
    bias = jnp.transpose(_bias_expand(rb_pad), (1, 0, 2)).reshape(ATTN_HEADS * CHUNK, BAND)
    gq2 = jnp.concatenate([sp["attn_q_norm_g"]] * 2, axis=1)
    gk2 = jnp.concatenate([sp["attn_k_norm_g"]] * 2, axis=1)
    lb = _lb_fwd(sp["hgrn_lower_bounds"])
    gate1, up1, act1 = _ffn_up(h1, w["ffn1_w_gate"], w["ffn1_w_up"], "ffn1_up")
    x1, h2 = _mm(act1, w["ffn1_w_down"], tm=512, tn=d, add=x, scale=0.5, norm_g=sp["mix_norm_g"],
                 name="ffn1_down")
    proj = _mm(h2, w["w_in"], tb=True, tm=256, tn=w["w_in"].shape[0], name="in_proj")
    attn, hy, ho, hstate, hscore = _mixer_fwd(proj, bias, gq2, gk2, lb, sp["hgrn_out_norm_g"], nb, seq)
    mix = jnp.concatenate([attn, hy], axis=1)
    x2, h3 = _mm(mix, w["w_out"], tm=512, tn=1024, add=x1, norm_g=sp["ffn2_norm_g"], name="out_proj")
    gate2, up2, dx3, dx3_16, sq = _ffn_fwd(h3, x2, w["ffn2_w_gate"], w["ffn2_w_up"], w["ffn2_w_down"], "ffn2_fwd",
                                           tgt=tgt)
    loss = 0.5 * jnp.sum(sq) / d

    dx2, dx2_16, dg3 = _ffn_bwd(x2, sp["ffn2_norm_g"], h3, gate2, up2, dx3, dx3_16, w, put, "ffn2")
    dmix = _mm(dx2_16, w["w_out"], tb=True, tm=512, tn=1024, name="out_proj_dx")
    put("w_out", _mm(mix, dx2_16, ta=True, tm=512, tn=1024, name="out_proj_dw"))
    bias_t = jnp.transpose(bias.reshape(ATTN_HEADS // 2, 2 * CHUNK, BAND), (0, 2, 1)).reshape(-1, 2 * CHUNK)
    dq, dk, dv, dbias, dgq, dgk = _attn_bwd(proj, attn, dmix, bias_t, gq2, gk2, nb, seq)
    dbias = jnp.transpose(dbias.reshape(nb, ATTN_HEADS // 2, BAND, 2, CHUNK), (0, 4, 1, 3, 2))
    dbias = dbias.reshape(nb, CHUNK, ATTN_HEADS, BAND)
    dgq = jnp.sum(dgq, axis=(0, 1)).reshape(2, ATTN_HEAD_DIM).sum(axis=0, keepdims=True)
    dgk = jnp.sum(dgk, axis=(0, 1)).reshape(2, ATTN_HEAD_DIM).sum(axis=0, keepdims=True)
    dhq, dhf, dhi, dhg, dlb, dgo = _hgrn_bwd(proj, lb, sp["hgrn_out_norm_g"], ho, hstate, hscore, dmix, nb, seq)
    dproj = jnp.concatenate([dq, dk, dv, dhq, dhf, dhi, dhg], axis=1)
    put("w_in", _mm(dproj, h2, ta=True, tm=512, tn=1024, name="in_proj_dw"))
    dx1, dx1_16, dgm = _mm(dproj, w["w_in"], tm=512, tn=1024, norm_bwd=(x1, sp["mix_norm_g"], dx2),
                           name="in_proj_dx")
    dx0, _, dg1 = _ffn_bwd(x, sp["ffn1_norm_g"], h1, gate1, up1, dx1, dx1_16, w, put, "ffn1")

    small = {
        "ffn1_norm_g": dg1, "mix_norm_g": dgm, "ffn2_norm_g": dg3,
        "attn_q_norm_g": dgq, "attn_k_norm_g": dgk,
        "attn_rel_bias": _bias_fold(dbias)[:, :N_REL],
        "hgrn_lower_bounds": _lb_bwd(sp["hgrn_lower_bounds"], dlb.reshape(nb, 1, HGRN_HEADS * HGRN_HEAD_DIM)),
        "hgrn_out_norm_g": jnp.sum(dgo, axis=(0, 1))[None, :],
    }
    return loss, dx0, small


MESH = pl.DeviceIdType.MESH
ANY = pl.BlockSpec(memory_space=pl.ANY)


def _coords():
    return lax.axis_index("x"), lax.axis_index("y"), lax.axis_index("c")


def _other_chips(x, y):
    return [(1 - x, y), (x, 1 - y), (1 - x, 1 - y)]


def _gather_side(shards):
    n = len(shards)

    def copies(ins, outs, sems):
        send_sems, recv_sems, local_sems = sems
        x, y, c = _coords()
        xn, yn, dg = (1 - x, y), (x, 1 - y), (1 - x, 1 - y)

        def copy(i, k, block, to, half=None, src=None):
            bx, by, bc = block
            dst = outs[i].at[4 * bx + 2 * by + bc]
            if half is not None:
                rows = shards[i].shape[0] // 2
                dst = dst.at[pl.ds(half * rows, rows)]
            return pltpu.make_async_remote_copy(
                src_ref=dst if src is None else src, dst_ref=dst, send_sem=send_sems.at[i, k],
                recv_sem=recv_sems.at[i, k], device_id=to, device_id_type=MESH)

        mine = [pltpu.make_async_copy(ins[i], outs[i].at[4 * x + 2 * y + c], local_sems.at[i]) for i in range(n)]
        return copy, mine, (x, y, c), (x, y, 1 - c), xn, yn, dg, c

    def own(copy, i, ins, me, sibling, xn, yn, c):
        return [copy(i, 0, me, sibling, src=ins[i]), copy(i, 1, me, (*xn, c), src=ins[i]),
                copy(i, 2, me, (*yn, c), src=ins[i])]

    def passed_on(copy, i, sibling, xn, yn, c):
        return [copy(i, 3, (*xn, c), sibling), copy(i, 5, (*xn, c), (*yn, c), half=0),
                copy(i, 4, (*yn, c), sibling), copy(i, 6, (*yn, c), (*xn, c), half=1)]

    def diagonal(copy, i, sibling, dg, c):
        return [copy(i, 7, (*dg, c), sibling, half=0), copy(i, 8, (*dg, c), sibling, half=1)]

    def start(ins, outs, sems):
        copy, mine, me, sibling, xn, yn, dg, c = copies(ins, outs, sems)
        for cp in mine + [cp for i in range(n) for cp in own(copy, i, ins, me, sibling, xn, yn, c)]:
            cp.start()

    def middle(ins, outs, sems):
        copy, mine, me, sibling, xn, yn, dg, c = copies(ins, outs, sems)
        for i in range(n):
            fwd_x, relay_x, fwd_y, relay_y = passed_on(copy, i, sibling, xn, yn, c)
            copy(i, 1, (*xn, c), me).wait_recv()
            fwd_x.start()
            relay_x.start()
            copy(i, 2, (*yn, c), me).wait_recv()
            fwd_y.start()
            relay_y.start()

    def finish(ins, outs, sems):
        copy, mine, me, sibling, xn, yn, dg, c = copies(ins, outs, sems)
        for i in range(n):
            top, bottom = diagonal(copy, i, sibling, dg, c)
            copy(i, 5, (*dg, c), me, half=0).wait_recv()
            top.start()
            copy(i, 6, (*dg, c), me, half=1).wait_recv()
            bottom.start()
        for i in range(n):
            copy(i, 0, sibling, me).wait_recv()
            copy(i, 3, (*xn, 1 - c), me).wait_recv()
            copy(i, 4, (*yn, 1 - c), me).wait_recv()
            copy(i, 7, (*dg, 1 - c), me, half=0).wait_recv()
            copy(i, 8, (*dg, 1 - c), me, half=1).wait_recv()
        for i in range(n):
            for cp in (own(copy, i, ins, me, sibling, xn, yn, c) + passed_on(copy, i, sibling, xn, yn, c)
                       + diagonal(copy, i, sibling, dg, c)):
                cp.wait_send()
        for cp in mine:
            cp.wait()

    return _Side(list(shards), [jax.ShapeDtypeStruct((N_DEV,) + s.shape, s.dtype) for s in shards],
                 [pltpu.SemaphoreType.DMA((n, 9)), pltpu.SemaphoreType.DMA((n, 9)), pltpu.SemaphoreType.DMA((n,))],
                 start, finish, middle)


def _pair_side(grads):
    n = len(grads)

    def copies(ins, outs, sems):
        send_sems, recv_sems = sems
        x, y, c = _coords()
        return [pltpu.make_async_remote_copy(
            src_ref=ins[i].at[2 * k + 1 - c], dst_ref=outs[i].at[k], send_sem=send_sems.at[i, k],
            recv_sem=recv_sems.at[i, k], device_id=(x, y, 1 - c), device_id_type=MESH)
            for i in range(n) for k in range(4)]

    def start(ins, outs, sems):
        for cp in copies(ins, outs, sems):
            cp.start()

    def finish(ins, outs, sems):
        for cp in copies(ins, outs, sems):
            cp.wait()

    return _Side(list(grads), [jax.ShapeDtypeStruct((4,) + g.shape[1:], g.dtype) for g in grads],
                 [pltpu.SemaphoreType.DMA((n, 4)), pltpu.SemaphoreType.DMA((n, 4))], start, finish)


def _pair_add(grads, recvs, core, name):
    count = len(grads)

    def body(c_ref, *refs):
        for n in range(count):
            refs[2 * count + n][...] = (refs[2 * n][...] + refs[2 * n + 1][...]).astype(BF16)

    in_specs, out_specs = [], []
    for g in grads:
        blk = (1,) + g.shape[1:]
        in_specs += [pl.BlockSpec(blk, lambda k, c_ref: (2 * k + c_ref[0], 0, 0)),
                     pl.BlockSpec(blk, lambda k, c_ref: (k, 0, 0))]
        out_specs.append(pl.BlockSpec(blk, lambda k, c_ref: (k, 0, 0)))
    out = pl.pallas_call(
        body, name=name,
        grid_spec=pltpu.PrefetchScalarGridSpec(num_scalar_prefetch=1, grid=(4,), in_specs=in_specs,
                                               out_specs=out_specs),
        out_shape=[jax.ShapeDtypeStruct((4,) + g.shape[1:], BF16) for g in grads],
        compiler_params=_params("arbitrary"),
    )(core, *[a for pair in zip(grads, recvs) for a in pair])
    return list(out)


def _chip_side(parts):
    n = len(parts)

    def copies(ins, outs, sems):
        send_sems, recv_sems, local_sems = sems
        x, y, c = _coords()
        chips = _other_chips(x, y)
        mine = [pltpu.make_async_copy(ins[i].at[2 * x + y], outs[i].at[2 * x + y], local_sems.at[i])
                for i in range(n)]
        sent = [pltpu.make_async_remote_copy(
            src_ref=ins[i].at[2 * px + py], dst_ref=outs[i].at[2 * x + y], send_sem=send_sems.at[i, j],
            recv_sem=recv_sems.at[i, j], device_id=(px, py, c), device_id_type=MESH)
            for i in range(n) for j, (px, py) in enumerate(chips)]
        return mine, sent, chips, c

    def start(ins, outs, sems):
        mine, sent, _, _ = copies(ins, outs, sems)
        for cp in mine + sent:
            cp.start()

    def finish(ins, outs, sems):
        mine, sent, chips, c = copies(ins, outs, sems)
        send_sems, recv_sems, _ = sems
        for i in range(n):
            for j, (px, py) in enumerate(chips):
                landed = outs[i].at[2 * px + py]
                pltpu.make_async_remote_copy(
                    src_ref=landed, dst_ref=landed, send_sem=send_sems.at[i, j], recv_sem=recv_sems.at[i, j],
                    device_id=(px, py, c), device_id_type=MESH).wait_recv()
        for cp in sent:
            cp.wait_send()
        for cp in mine:
            cp.wait()

    return _Side(list(parts), [jax.ShapeDtypeStruct(p.shape, p.dtype) for p in parts],
                 [pltpu.SemaphoreType.DMA((n, 3)), pltpu.SemaphoreType.DMA((n, 3)), pltpu.SemaphoreType.DMA((n,))],
                 start, finish)


def _all_reduce_small(v):
    r = v.shape[0]

    def body(v_ref, o_ref, buf, send_sems, recv_sems):
        x, y, c = _coords()
        me = 4 * x + 2 * y + c
        buf[me] = v_ref[...]
        cps = []
        for k in range(1, N_DEV):
            px = 1 - x if k & 4 else x
            py = 1 - y if k & 2 else y
            pc = 1 - c if k & 1 else c
            cps.append((pltpu.make_async_remote_copy(
                src_ref=v_ref, dst_ref=buf.at[me], send_sem=send_sems.at[k - 1], recv_sem=recv_sems.at[k - 1],
                device_id=(px, py, pc), device_id_type=MESH), 4 * px + 2 * py + pc))
        for cp, _ in cps:
            cp.start()
        for k, (cp, peer) in enumerate(cps):
            pltpu.make_async_remote_copy(
                src_ref=v_ref, dst_ref=buf.at[peer], send_sem=send_sems.at[k], recv_sem=recv_sems.at[k],
                device_id=(x, y, c), device_id_type=MESH).wait_recv()
        for cp, _ in cps:
            cp.wait_send()
        acc = buf[0]
        for j in range(1, N_DEV):
            acc = acc + buf[j]
        o_ref[...] = acc

    return pl.pallas_call(
        body, name="small_all_reduce", out_shape=jax.ShapeDtypeStruct(v.shape, F32),
        in_specs=[pl.BlockSpec(memory_space=pltpu.VMEM)], out_specs=pl.BlockSpec(memory_space=pltpu.VMEM),
        scratch_shapes=[pltpu.VMEM((N_DEV, r, 128), F32), pltpu.SemaphoreType.DMA((N_DEV - 1,)),
                        pltpu.SemaphoreType.DMA((N_DEV - 1,))],
    )(v)


def _adamw(ws, ms, vs, gs, name):
    count = len(ws)
    parts = ws[0].ndim == 3
    steps = 4 if all(w.shape[-2] % 32 == 0 for w in ws) else 1

    def body(*refs):
        for n in range(count):
            w_ref, m_ref, v_ref, g_ref = refs[4 * n:4 * n + 4]
            go_ref, d_ref, mo_ref, vo_ref = refs[4 * count + 4 * n:4 * count + 4 * n + 4]
            if parts:
                gv = g_ref[0].astype(F32)
                for k in range(1, 4):
                    gv = gv + g_ref[k].astype(F32)
                gv = gv[None]
            else:
                gv = g_ref[...]
            m2 = ADAM_B1 * m_ref[...] + (1.0 - ADAM_B1) * gv
            v2 = ADAM_B2 * v_ref[...] + (1.0 - ADAM_B2) * (gv * gv)
            m_hat = m2 / (1.0 - ADAM_B1 ** ADAM_STEP)
            v_hat = v2 / (1.0 - ADAM_B2 ** ADAM_STEP)
            go_ref[...] = gv
            d_ref[...] = -ADAM_LR * (m_hat / (jnp.sqrt(v_hat) + ADAM_EPS) + ADAM_WD * w_ref[...])
            mo_ref[...] = m2
            vo_ref[...] = v2

    in_specs, out_specs, out_shape = [], [], []
    for w in ws:
        r, cdim = w.shape[-2:]
        if parts:
            row = pl.BlockSpec((1, r // steps, cdim), lambda i: (0, i, 0))
            g_spec = pl.BlockSpec((4, r // steps, cdim), lambda i: (0, i, 0))
        else:
            row = g_spec = pl.BlockSpec((r // steps, cdim), lambda i: (i, 0))
        in_specs += [row, row, row, g_spec]
        out_specs += [row] * 4
        out_shape += [jax.ShapeDtypeStruct(w.shape, F32)] * 4
    args = [a for group in zip(ws, ms, vs, gs) for a in group]
    out = pl.pallas_call(
        body, name=name, grid=(steps,), in_specs=in_specs, out_specs=out_specs, out_shape=out_shape,
        compiler_params=_params("parallel"),
    )(*args)
    return [out[4 * n:4 * n + 4] for n in range(count)]


WEIGHTS = ["ffn1_norm_g", "ffn1_w_gate", "ffn1_w_up", "ffn1_w_down", "mix_norm_g", "w_in", "attn_q_norm_g",
           "attn_k_norm_g", "attn_rel_bias", "hgrn_lower_bounds", "hgrn_out_norm_g", "w_out", "ffn2_norm_g",
           "ffn2_w_gate", "ffn2_w_up", "ffn2_w_down"]
COL_SHARDED = ("ffn1_w_gate", "ffn1_w_up", "w_in", "ffn2_w_gate", "ffn2_w_up")
ROW_SHARDED = ("ffn1_w_down", "w_out", "ffn2_w_down")
BIG = [n for n in WEIGHTS if n in COL_SHARDED or n in ROW_SHARDED]
SMALL = [n for n in WEIGHTS if n not in BIG]
PACK_ROWS = 8
FFN2 = ["ffn2_w_down", "ffn2_w_gate", "ffn2_w_up"]
MIXER = ["w_out", "w_in"]

PLAN = {
    "ffn1_norm": [("gather", ["ffn1_w_gate"])],
    "bias_expand": [("gather", ["ffn1_w_up"])],
    "ffn1_up": [("gather", ["ffn1_w_down", "w_out"])],
    "ffn1_down": [("gather", ["w_in"])],
    "mixer_fwd": [("gather", FFN2)],
    "ffn2_dh_gate": [("pair", FFN2)],
    "attn_bwd": [("chip", FFN2)],
    "in_proj_dx": [("pair", MIXER)],
    "ffn1_bwd_mid": [("chip", MIXER)],
    "ffn1_dwg": [("pair", ["ffn1_w_down"])],
    "ffn1_dwu": [("chip", ["ffn1_w_down"]), ("pair", ["ffn1_w_gate"])],
    "ffn1_dh_gate": [("chip", ["ffn1_w_gate"]), ("pair", ["ffn1_w_up"])],
    "bias_fold": [("chip", ["ffn1_w_up"])],
}


def _join_sides(sides):
    def split(refs, counts):
        out, at = [], 0
        for n in counts:
            out.append(refs[at:at + n])
            at += n
        return out

    n_in, n_out, n_sem = ([len(getattr(s, f)) for s in sides] for f in ("ins", "out_shape", "sems"))

    def run(which):
        def go(ins, outs, sems):
            for s, i, o, m in zip(sides, split(ins, n_in), split(outs, n_out), split(sems, n_sem)):
                if getattr(s, which) is not None:
                    getattr(s, which)(i, o, m)
        return go

    return _Side([a for s in sides for a in s.ins], [a for s in sides for a in s.out_shape],
                 [a for s in sides for a in s.sems], run("start"), run("finish"),
                 run("middle") if any(s.middle is not None for s in sides) else None)


class _Schedule:
    def __init__(self, shards):
        self.shards = shards
        self.weights = {}
        self.sliced = {}
        self.partials = {}
        self.reduced = {}

    def put(self, name, grad):
        self.sliced[name] = grad.reshape((N_DEV,) + self.shards[name].shape)

    def side_for(self, call):
        if call not in PLAN:
            return None
        sides = []
        for kind, names in PLAN[call]:
            if kind == "gather":
                sides.append(_gather_side([self.shards[n] for n in names]))
            elif kind == "pair":
                sides.append(_pair_side([self.sliced[n] for n in names]))
            else:
                sides.append(_chip_side([self.partials[n] for n in names]))
        return _join_sides(sides)

    def done(self, call, outs):
        at = 0
        for kind, names in PLAN[call]:
            self.file(kind, names, outs[at:at + len(names)])
            at += len(names)

    def file(self, kind, names, outs):
        if kind == "pair":
            core = lax.axis_index("c").astype(jnp.int32).reshape(1)
            sums = _pair_add([self.sliced[n] for n in names], list(outs), core, names[0] + "_pair_add")
            self.partials.update(dict(zip(names, sums)))
            return
        for n, o in zip(names, outs):
            if kind == "gather":
                self.weights[n] = o.reshape(N_DEV * o.shape[1], o.shape[2])
            else:
                self.reduced[n] = o


def _pack_small(vals, loss=None):
    parts = []
    for n in SMALL:
        a = vals[n]
        if n == "attn_rel_bias":
            a = jnp.pad(a.reshape(ATTN_HEADS, N_REL), ((0, 0), (0, N_REL_PAD - N_REL)))
        flat = a.reshape(-1)
        size = -(-flat.shape[0] // (PACK_ROWS * 128)) * PACK_ROWS * 128
        parts.append(jnp.pad(flat, (0, size - flat.shape[0])).reshape(-1, 128))
    tail = jnp.zeros((PACK_ROWS, 128), F32)
    if loss is not None:
        tail = tail.at[0, 0].set(loss)
    return jnp.concatenate(parts + [tail], axis=0)


def _unpack_small(packed, shapes):
    out, row = {}, 0
    for n in SMALL:
        shape = shapes[n]
        if n == "attn_rel_bias":
            rows = ATTN_HEADS * N_REL_PAD // 128
            out[n] = packed[row:row + rows].reshape(ATTN_HEADS, N_REL_PAD)[:, :N_REL].reshape(shape)
        else:
            size = 1
            for s in shape:
                size *= s
            rows = -(-size // (PACK_ROWS * 128)) * PACK_ROWS
            out[n] = packed[row:row + rows].reshape(-1)[:size].reshape(shape)
        row += rows
    return out, packed[row, 0]


def kernel(x, ffn1_norm_g, ffn1_w_gate, ffn1_w_up, ffn1_w_down, mix_norm_g, w_in, attn_q_norm_g, attn_k_norm_g, attn_rel_bias, hgrn_lower_bounds, hgrn_out_norm_g, w_out, ffn2_norm_g, ffn2_w_gate, ffn2_w_up, ffn2_w_down, loss_target, m_ffn1_norm_g, m_ffn1_w_gate, m_ffn1_w_up, m_ffn1_w_down, m_mix_norm_g, m_w_in, m_attn_q_norm_g, m_attn_k_norm_g, m_attn_rel_bias, m_hgrn_lower_bounds, m_hgrn_out_norm_g, m_w_out, m_ffn2_norm_g, m_ffn2_w_gate, m_ffn2_w_up, m_ffn2_w_down, v_ffn1_norm_g, v_ffn1_w_gate, v_ffn1_w_up, v_ffn1_w_down, v_mix_norm_g, v_w_in, v_attn_q_norm_g, v_attn_k_norm_g, v_attn_rel_bias, v_hgrn_lower_bounds, v_hgrn_out_norm_g, v_w_out, v_ffn2_norm_g, v_ffn2_w_gate, v_ffn2_w_up, v_ffn2_w_down):
    wts = dict(zip(WEIGHTS, (ffn1_norm_g, ffn1_w_gate, ffn1_w_up, ffn1_w_down, mix_norm_g, w_in, attn_q_norm_g,
                             attn_k_norm_g, attn_rel_bias, hgrn_lower_bounds, hgrn_out_norm_g, w_out, ffn2_norm_g,
                             ffn2_w_gate, ffn2_w_up, ffn2_w_down)))
    mom = dict(zip(WEIGHTS, (m_ffn1_norm_g, m_ffn1_w_gate, m_ffn1_w_up, m_ffn1_w_down, m_mix_norm_g, m_w_in,
                             m_attn_q_norm_g, m_attn_k_norm_g, m_attn_rel_bias, m_hgrn_lower_bounds,
                             m_hgrn_out_norm_g, m_w_out, m_ffn2_norm_g, m_ffn2_w_gate, m_ffn2_w_up, m_ffn2_w_down)))
    var = dict(zip(WEIGHTS, (v_ffn1_norm_g, v_ffn1_w_gate, v_ffn1_w_up, v_ffn1_w_down, v_mix_norm_g, v_w_in,
                             v_attn_q_norm_g, v_attn_k_norm_g, v_attn_rel_bias, v_hgrn_lower_bounds,
                             v_hgrn_out_norm_g, v_w_out, v_ffn2_norm_g, v_ffn2_w_gate, v_ffn2_w_up, v_ffn2_w_down)))
    nb, seq, d = x.shape
    shapes = {n: wts[n].shape for n in WEIGHTS}

    def rows_first(a, n):
        return jnp.swapaxes(a, 1, 2) if n in COL_SHARDED else a

    sched = _Schedule({n: rows_first(wts[n], n)[0].astype(BF16) for n in BIG})
    sp = {n: wts[n] for n in SMALL}
    sp["attn_rel_bias"] = wts["attn_rel_bias"][0]
    _ACTIVE[0] = sched
    try:
        loss, dx, dsmall = _local_step(x.reshape(nb * seq, d), loss_target.reshape(nb * seq, d), sp,
                                       sched.weights, sched.put, nb, seq)
    finally:
        _ACTIVE[0] = None
    reduced = sched.reduced

    small_sum = _all_reduce_small(_pack_small(dsmall, loss))
    gsmall, loss_total = _unpack_small(small_sum, shapes)

    grads, deltas, new_m, new_v = {}, {}, {}, {}
    for group, tag in (([n for n in BIG if n not in MIXER], "ffn_adamw"), (MIXER, "mixer_adamw")):
        outs = _adamw([rows_first(wts[n], n) for n in group], [rows_first(mom[n], n) for n in group],
                      [rows_first(var[n], n) for n in group], [reduced[n] for n in group], tag)
        for n, out in zip(group, outs):
            grads[n], deltas[n], new_m[n], new_v[n] = (rows_first(o, n) for o in out)
    packed = _adamw([_pack_small(wts)], [_pack_small(mom)], [_pack_small(var)], [small_sum], "small_adamw")[0]
    for dst, p in zip((deltas, new_m, new_v), packed[1:]):
        dst.update(_unpack_small(p, shapes)[0])
    grads.update(gsmall)

    return (loss_total, dx.reshape(nb, seq, d), *[grads[n] for n in WEIGHTS], *[deltas[n] for n in WEIGHTS],
            *[new_m[n] for n in WEIGHTS], *[new_v[n] for n in WEIGHTS])
```

```python
import functools

import jax
import jax.numpy as jnp
from jax import lax
from jax.experimental import pallas as pl
from jax.experimental.pallas import tpu as pltpu

F32 = jnp.float32
BF16 = jnp.bfloat16

RMS_EPS = 1e-6
CHUNK = 64
LEFT_CHUNKS = 8
BAND = (LEFT_CHUNKS + 2) * CHUNK
KPAD = BAND - CHUNK
REL_CLIP = 128
N_REL = 2 * REL_CLIP + 1
N_REL_PAD = 384
ATTN_HEADS = 8
ATTN_HEAD_DIM = 64
ATTN_WIDTH = ATTN_HEADS * ATTN_HEAD_DIM
ATTN_LOCKSTEP = 4
ATTN_UNROLL = 8
HGRN_HEADS = 4
HGRN_HEAD_DIM = 128
HGRN_ROWS = 512
SUB = 16
N_SUB = CHUNK // SUB
DIAG_STAGE = 4
N_DEV = 8

ADAM_LR = 0.001
ADAM_B1 = 0.9
ADAM_B2 = 0.999
ADAM_EPS = 1e-08
ADAM_WD = 0.01
ADAM_STEP = 10

VMEM_LIMIT = 56 * 1024 * 1024

NT = (((1,), (1,)), ((), ()))
NN = (((1,), (0,)), ((), ()))


def _params(*sem):
    return pltpu.CompilerParams(dimension_semantics=sem, vmem_limit_bytes=VMEM_LIMIT)


def _sigmoid(v):
    return 0.5 * jnp.tanh(0.5 * v) + 0.5


def _dot(a, b, dims=NN):
    return lax.dot_general(a.astype(BF16), b.astype(BF16), dims, preferred_element_type=F32)


def _dot_exact01(m01, v):
    m = m01.astype(BF16)
    hi = v.astype(BF16)
    r1 = v - hi.astype(F32)
    mid = r1.astype(BF16)
    lo = (r1 - mid.astype(F32)).astype(BF16)
    out = lax.dot_general(m, hi, NN, preferred_element_type=F32)
    out = out + lax.dot_general(m, mid, NN, preferred_element_type=F32)
    return out + lax.dot_general(m, lo, NN, preferred_element_type=F32)


def _dot_exact01_r(v, m01):
    m = m01.astype(BF16)
    hi = v.astype(BF16)
    r1 = v - hi.astype(F32)
    mid = r1.astype(BF16)
    lo = (r1 - mid.astype(F32)).astype(BF16)
    out = lax.dot_general(hi, m, NN, preferred_element_type=F32)
    out = out + lax.dot_general(mid, m, NN, preferred_element_type=F32)
    return out + lax.dot_general(lo, m, NN, preferred_element_type=F32)


def _lockstep(stages):
    live = list(stages)
    while live:
        still = []
        for g in live:
            try:
                next(g)
                still.append(g)
            except StopIteration:
                pass
        live = still


def _row_sums_on_lanes(v):
    ones = jnp.ones((8, v.shape[1]), BF16)
    hi = v.astype(BF16)
    r1 = v - hi.astype(F32)
    mid = r1.astype(BF16)
    lo = (r1 - mid.astype(F32)).astype(BF16)
    out = lax.dot_general(ones, hi, NT, preferred_element_type=F32)
    out = out + lax.dot_general(ones, mid, NT, preferred_element_type=F32)
    return (out + lax.dot_general(ones, lo, NT, preferred_element_type=F32))[0:1, :]


def _tn(a, b):
    ap = jnp.concatenate([a, jnp.zeros_like(a)], axis=0)
    bp = jnp.concatenate([b, jnp.zeros_like(b)], axis=0)
    return _dot(ap.T, bp)


def _row_tile(t):
    for tm in (512, 256, 128, 64, 32, 16, 8):
        if t % tm == 0:
            return tm
    raise ValueError(t)


class _Side:
    def __init__(self, ins, out_shape, sems, start, finish, middle=None):
        self.ins, self.out_shape, self.sems = ins, out_shape, sems
        self.start, self.middle, self.finish = start, middle, finish


_ACTIVE = [None]


def _pallas(body, *, name, grid, in_specs, out_specs, out_shape, scratch_shapes=(), sem, args):
    sched = _ACTIVE[0]
    side = sched.side_for(name) if sched is not None else None
    if side is None:
        return pl.pallas_call(
            body, name=name, grid=grid, in_specs=list(in_specs), out_specs=list(out_specs),
            out_shape=list(out_shape), scratch_shapes=list(scratch_shapes), compiler_params=_params(*sem))(*args)
    cuts = [len(in_specs), len(side.ins), len(out_shape), len(side.out_shape), len(scratch_shapes)]

    def with_side(*refs):
        groups, at = [], 0
        for n in cuts:
            groups.append(refs[at:at + n])
            at += n
        ins, side_ins, outs, side_outs, scratch = groups
        side_sems = refs[at:]
        step, total = pl.program_id(0), grid[0]
        for a in range(1, len(grid)):
            step, total = step * grid[a] + pl.program_id(a), total * grid[a]
        has_middle = side.middle is not None and total >= 3

        @pl.when(step == 0)
        def _():
            side.start(side_ins, side_outs, side_sems)

        if has_middle:
            @pl.when(step == total // 2)
            def _():
                side.middle(side_ins, side_outs, side_sems)

        body(*ins, *outs, *scratch)

        @pl.when(step == total - 1)
        def _():
            if side.middle is not None and not has_middle:
                side.middle(side_ins, side_outs, side_sems)
            side.finish(side_ins, side_outs, side_sems)

    hbm = pl.BlockSpec(memory_space=pl.ANY)
    res = pl.pallas_call(
        with_side, name=name, grid=grid, in_specs=list(in_specs) + [hbm] * len(side.ins),
        out_specs=list(out_specs) + [hbm] * len(side.out_shape), out_shape=list(out_shape) + list(side.out_shape),
        scratch_shapes=list(scratch_shapes) + list(side.sems),
        compiler_params=_params(*(["arbitrary"] * len(grid))))(*args, *side.ins)
    sched.done(name, res[len(out_shape):])
    return res[:len(out_shape)]


def _rms_fwd(x, g, name):
    t, d = x.shape
    tm = _row_tile(t)

    def body(x_ref, g_ref, h_ref):
        xv = x_ref[...]
        r = lax.rsqrt(jnp.mean(xv * xv, axis=-1, keepdims=True) + RMS_EPS)
        h_ref[...] = (xv * r * g_ref[...]).astype(BF16)

    return _pallas(
        body, name=name, grid=(t // tm,),
        in_specs=[pl.BlockSpec((tm, d), lambda i: (i, 0)), pl.BlockSpec((1, d), lambda i: (0, 0))],
        out_specs=[pl.BlockSpec((tm, d), lambda i: (i, 0))], out_shape=[jax.ShapeDtypeStruct((t, d), BF16)],
        sem=("parallel",), args=(x, g))[0]


def _accumulate(ref, part, step):
    @pl.when(step == 0)
    def _():
        ref[...] = part

    @pl.when(step > 0)
    def _():
        ref[...] += part


def _mm(a, b, *, ta=False, tb=False, tm, tn, out_dtype=F32, add=None, scale=1.0, norm_g=None, norm_bwd=None, name):
    m, k = (a.shape[1], a.shape[0]) if ta else a.shape
    n = b.shape[0] if tb else b.shape[1]
    tm, tn = min(tm, m), min(tn, n)
    assert m % tm == 0 and n % tn == 0, (m, n, tm, tn)
    assert (norm_g is None and norm_bwd is None) or tn == n
    dims = (((0 if ta else 1,), (1 if tb else 0,)), ((), ()))
    n_in = 2 + (add is not None) + (norm_g is not None) + (3 if norm_bwd is not None else 0)

    def body(*refs):
        ins, outs = list(refs[2:n_in]), refs[n_in:]
        r = lax.dot_general(refs[0][...].astype(BF16), refs[1][...].astype(BF16), dims, preferred_element_type=F32)
        if scale != 1.0:
            r = r * scale
        if add is not None:
            r = r + ins.pop(0)[...]
        if norm_bwd is not None:
            xv, gv, dres = (ref[...] for ref in ins)
            rs = lax.rsqrt(jnp.mean(xv * xv, axis=-1, keepdims=True) + RMS_EPS)
            xhat = xv * rs
            gd = r * gv
            dx = dres + rs * (gd - xhat * jnp.mean(gd * xhat, axis=-1, keepdims=True))
            outs[0][...] = dx
            outs[1][...] = dx.astype(BF16)
            _accumulate(outs[2], jnp.sum(r * xhat, axis=0, keepdims=True), pl.program_id(0))
            return
        outs[0][...] = r.astype(out_dtype)
        if norm_g is not None:
            rs = lax.rsqrt(jnp.mean(r * r, axis=-1, keepdims=True) + RMS_EPS)
            outs[1][...] = (r * rs * ins.pop(0)[...]).astype(BF16)

    a_spec = pl.BlockSpec((k, tm), lambda i, j: (0, i)) if ta else pl.BlockSpec((tm, k), lambda i, j: (i, 0))
    b_spec = pl.BlockSpec((tn, k), lambda i, j: (j, 0)) if tb else pl.BlockSpec((k, tn), lambda i, j: (0, j))
    o_spec = pl.BlockSpec((tm, tn), lambda i, j: (i, j))
    vec = pl.BlockSpec((1, tn), lambda i, j: (0, j))
    args, specs = [a, b], [a_spec, b_spec]
    if add is not None:
        args.append(add)
        specs.append(o_spec)
    out_specs, out_shape = [o_spec], [jax.ShapeDtypeStruct((m, n), out_dtype)]
    if norm_g is not None:
        args.append(norm_g)
        specs.append(vec)
        out_specs.append(o_spec)
        out_shape.append(jax.ShapeDtypeStruct((m, n), BF16))
    if norm_bwd is not None:
        args += list(norm_bwd)
        specs += [o_spec, vec, o_spec]
        out_specs = [o_spec, o_spec, vec]
        out_shape = [jax.ShapeDtypeStruct((m, n), F32), jax.ShapeDtypeStruct((m, n), BF16),
                     jax.ShapeDtypeStruct((1, n), F32)]
    res = _pallas(body, name=name, grid=(m // tm, n // tn), in_specs=specs, out_specs=out_specs, out_shape=out_shape,
                  sem=("arbitrary", "arbitrary") if norm_bwd is not None else ("parallel", "parallel"), args=args)
    return res[0] if len(res) == 1 else res


def _ffn_tile(f):
    for tf in (1408, 512, 256, 128):
        if f % tf == 0:
            return tf
    raise ValueError(f)


def _ffn_fwd(h, x, wg, wu, wd, name, next_g=None, tgt=None):
    t, d = x.shape
    f = wg.shape[0]
    tm, tf = _row_tile(t), _ffn_tile(f)
    nf = f // tf
    assert (next_g is None) != (tgt is None)

    def body(h_ref, x_ref, wg_ref, wu_ref, wd_ref, tail_ref, g_ref, u_ref, o0_ref, o1_ref, *rest):
        acc_ref = rest[-1]
        j = pl.program_id(1)
        hv = h_ref[...]
        gv = lax.dot_general(hv, wg_ref[...], NT, preferred_element_type=F32)
        uv = lax.dot_general(hv, wu_ref[...], NT, preferred_element_type=F32)
        av = gv * _sigmoid(gv) * uv
        g_ref[...] = gv.astype(BF16)
        u_ref[...] = uv.astype(BF16)
        _accumulate(acc_ref, lax.dot_general(av.astype(BF16), wd_ref[...], NN, preferred_element_type=F32), j)

        @pl.when(j == nf - 1)
        def _():
            y = x_ref[...] + 0.5 * acc_ref[...]
            if tgt is None:
                o0_ref[...] = y
                rs = lax.rsqrt(jnp.mean(y * y, axis=-1, keepdims=True) + RMS_EPS)
                o1_ref[...] = (y * rs * tail_ref[...]).astype(BF16)
            else:
                e = y - tail_ref[...]
                dy = e * (1.0 / d)
                o0_ref[...] = dy
                o1_ref[...] = dy.astype(BF16)
                _accumulate(rest[0], jnp.sum(e * e, axis=0, keepdims=True), pl.program_id(0))

    row = pl.BlockSpec((tm, d), lambda i, j: (i, 0))
    hid = pl.BlockSpec((tm, tf), lambda i, j: (i, j))
    vec = pl.BlockSpec((1, d), lambda i, j: (0, 0))
    out_specs = [hid, hid, row, row] + ([vec] if tgt is not None else [])
    out_shape = [jax.ShapeDtypeStruct((t, f), BF16)] * 2 + [jax.ShapeDtypeStruct((t, d), F32),
                                                            jax.ShapeDtypeStruct((t, d), BF16)]
    if tgt is not None:
        out_shape.append(jax.ShapeDtypeStruct((1, d), F32))
    return _pallas(
        body, name=name, grid=(t // tm, nf),
        in_specs=[row, row] + [pl.BlockSpec((tf, d), lambda i, j: (j, 0))] * 3 + [vec if tgt is None else row],
        out_specs=out_specs, out_shape=out_shape, scratch_shapes=[pltpu.VMEM((tm, d), F32)],
        sem=("parallel" if tgt is None else "arbitrary", "arbitrary"),
        args=(h, x, wg, wu, wd, next_g if tgt is None else tgt))


def _ffn_up(h, wg, wu, name):
    t, d = h.shape
    f = wg.shape[0]
    tm, tf = _row_tile(t), _ffn_tile(f)

    def body(h_ref, wg_ref, wu_ref, g_ref, u_ref, a_ref):
        hv = h_ref[...]
        gv = lax.dot_general(hv, wg_ref[...], NT, preferred_element_type=F32)
        uv = lax.dot_general(hv, wu_ref[...], NT, preferred_element_type=F32)
        g_ref[...] = gv.astype(BF16)
        u_ref[...] = uv.astype(BF16)
        a_ref[...] = (gv * _sigmoid(gv) * uv).astype(BF16)

    hid = pl.BlockSpec((tm, tf), lambda i, j: (i, j))
    wrow = pl.BlockSpec((tf, d), lambda i, j: (j, 0))
    return _pallas(
        body, name=name, grid=(t // tm, f // tf), in_specs=[pl.BlockSpec((tm, d), lambda i, j: (i, 0)), wrow, wrow],
        out_specs=[hid, hid, hid], out_shape=[jax.ShapeDtypeStruct((t, f), BF16)] * 3,
        sem=("parallel", "parallel"), args=(h, wg, wu))


def _ffn_bwd_mid(dy, wd, g, u, name):
    t, d = dy.shape
    f = wd.shape[0]
    tm, tf = _row_tile(t), _ffn_tile(f)

    def body(dy_ref, wd_ref, g_ref, u_ref, dg_ref, du_ref, dwd_ref):
        dy16 = dy_ref[...]
        da = 0.5 * lax.dot_general(dy16, wd_ref[...], NT, preferred_element_type=F32)
        gv = g_ref[...].astype(F32)
        uv = u_ref[...].astype(F32)
        s = _sigmoid(gv)
        silu = gv * s
        dg_ref[...] = (da * uv * (s * (1.0 + gv * (1.0 - s)))).astype(BF16)
        du_ref[...] = (da * silu).astype(BF16)
        part = 0.5 * lax.dot_general((silu * uv).astype(BF16), dy16, (((0,), (0,)), ((), ())),
                                     preferred_element_type=F32)
        _accumulate(dwd_ref, part, pl.program_id(1))

    hid = pl.BlockSpec((tm, tf), lambda j, i: (i, j))
    wrow = pl.BlockSpec((tf, d), lambda j, i: (j, 0))
    return _pallas(
        body, name=name, grid=(f // tf, t // tm),
        in_specs=[pl.BlockSpec((tm, d), lambda j, i: (i, 0)), wrow, hid, hid],
        out_specs=[hid, hid, wrow],
        out_shape=[jax.ShapeDtypeStruct((t, f), BF16)] * 2 + [jax.ShapeDtypeStruct((f, d), F32)],
        sem=("parallel", "arbitrary"), args=(dy, wd, g, u))


def _rel_index(t, s_band):
    return jnp.clip(t + KPAD - s_band, -REL_CLIP, REL_CLIP) + REL_CLIP


def _bias_expand(rel_bias_pad):
    nh = rel_bias_pad.shape[0]

    def body(rb_ref, out_ref):
        rb = rb_ref[...]
        i_io = lax.broadcasted_iota(jnp.int32, (N_REL_PAD, BAND), 0)
        s_io = lax.broadcasted_iota(jnp.int32, (N_REL_PAD, BAND), 1)

        def row(r, carry):
            onehot = (i_io == _rel_index(pl.program_id(0) * rows + r, s_io)).astype(F32)
            out_ref[r] = _dot_exact01_r(rb, onehot)
            return carry

        lax.fori_loop(0, rows, row, 0)

    rows = 8
    return _pallas(
        body, name="bias_expand", grid=(CHUNK // rows,),
        in_specs=[pl.BlockSpec(rel_bias_pad.shape, lambda i: (0, 0))],
        out_specs=[pl.BlockSpec((rows, nh, BAND), lambda i: (i, 0, 0))],
        out_shape=[jax.ShapeDtypeStruct((CHUNK, nh, BAND), F32)], sem=("arbitrary",), args=(rel_bias_pad,))[0]


def _bias_fold(dbias):
    ng, nh = dbias.shape[0], dbias.shape[2]

    def body(db_ref, out_ref):
        s_io = lax.broadcasted_iota(jnp.int32, (BAND, N_REL_PAD), 0)
        i_io = lax.broadcasted_iota(jnp.int32, (BAND, N_REL_PAD), 1)

        def row(t, acc):
            onehot = (i_io == _rel_index(t, s_io)).astype(F32)
            d = db_ref[0, t]
            for gi in range(1, ng):
                d = d + db_ref[gi, t]
            return acc + _dot_exact01_r(d, onehot)

        out_ref[...] = lax.fori_loop(0, CHUNK, row, jnp.zeros((nh, N_REL_PAD), F32))

    return _pallas(
        body, name="bias_fold", grid=(1,), in_specs=[pl.BlockSpec(dbias.shape, lambda i: (0, 0, 0, 0))],
        out_specs=[pl.BlockSpec((nh, N_REL_PAD), lambda i: (0, 0))],
        out_shape=[jax.ShapeDtypeStruct((nh, N_REL_PAD), F32)], sem=("arbitrary",), args=(dbias,))[0]


def _left_half(shape):
    return lax.broadcasted_iota(jnp.int32, shape, len(shape) - 1) < ATTN_HEAD_DIM


def _stack_heads(v):
    left = _left_half(v.shape)
    zero = jnp.zeros_like(v)
    return jnp.concatenate([jnp.where(left, v, zero), jnp.where(left, zero, v)], axis=0)


def _unstack_heads(v):
    return jnp.where(_left_half((CHUNK, 128)), v[0:CHUNK, :], v[CHUNK:2 * CHUNK, :])


def _half_mean(v):
    r = lax.broadcasted_iota(jnp.int32, (128, 128), 0) < ATTN_HEAD_DIM
    c = lax.broadcasted_iota(jnp.int32, (128, 128), 1) < ATTN_HEAD_DIM
    return _dot_exact01_r(v, r == c) * (1.0 / ATTN_HEAD_DIM)


def _attn_prepare(q_ref, k_ref, v_ref, gq_ref, gk_ref, qs_scr, k_scr, v_scr):
    q, k = q_ref[...], k_ref[...]
    rq = lax.rsqrt(_half_mean(q * q) + RMS_EPS)
    rk = lax.rsqrt(_half_mean(k * k) + RMS_EPS)
    qhat, khat = q * rq, k * rk
    qs_scr[...] = (qhat * gq_ref[...] * ATTN_HEAD_DIM ** -0.5).astype(BF16)
    k_scr[0:KPAD, :] = jnp.zeros((KPAD, 128), BF16)
    v_scr[0:KPAD, :] = jnp.zeros((KPAD, 128), BF16)
    k_scr[KPAD:, :] = (khat * gk_ref[...]).astype(BF16)
    v_scr[KPAD:, :] = v_ref[...].astype(BF16)
    return qhat, rq, khat, rk


def _first_key(c):
    return jnp.maximum(CHUNK, (LEFT_CHUNKS + 1 - c) * CHUNK)


def _attn_fwd_chunk(c, qs_scr, k_scr, v_scr, bias_ref, o_ref):
    r0 = pl.multiple_of(c * CHUNK, CHUNK)
    s = lax.dot_general(_stack_heads(qs_scr[pl.ds(r0, CHUNK), :]), k_scr[pl.ds(r0, BAND), :], NT,
                        preferred_element_type=F32)
    yield
    col = lax.broadcasted_iota(jnp.int32, (2 * CHUNK, BAND), 1)
    s = jnp.where(col >= _first_key(c), s + bias_ref[...], -jnp.inf)
    m = jnp.max(s, axis=-1, keepdims=True)
    yield
    e = jnp.exp(s - m)
    yield
    inv = 1.0 / jnp.sum(e, axis=-1, keepdims=True)
    o = lax.dot_general(e.astype(BF16), v_scr[pl.ds(r0, BAND), :], NN, preferred_element_type=F32)
    yield
    o_ref[pl.ds(r0, CHUNK), :] = _unstack_heads(o * inv)


def _attn_bwd(proj, out, dout, bias, gq, gk, nb, seq):
    nc = seq // CHUNK
    lock = min(ATTN_LOCKSTEP, nc)
    assert nc % lock == 0
    scale = ATTN_HEAD_DIM ** -0.5

    def body(q_ref, k_ref, v_ref, o_ref, do_ref, bias_ref, gq_ref, gk_ref,
             dq_ref, dk_ref, dv_ref, dbias_ref, dgq_ref, dgk_ref,
             qs_scr, k_scr, v_scr, dqn_scr, dk_scr, dv_scr, db_scr):
        qhat, rq, khat, rk = _attn_prepare(q_ref, k_ref, v_ref, gq_ref, gk_ref, qs_scr, k_scr, v_scr)
        dk_scr[...] = jnp.zeros_like(dk_scr)
        dv_scr[...] = jnp.zeros_like(dv_scr)
        db_scr[...] = jnp.zeros_like(db_scr)

        def one_chunk(c):
            r0 = pl.multiple_of(c * CHUNK, CHUNK)
            qst = _stack_heads(qs_scr[pl.ds(r0, CHUNK), :])
            kb = k_scr[pl.ds(r0, BAND), :]
            vb = v_scr[pl.ds(r0, BAND), :]
            st = lax.dot_general(kb, qst, NT, preferred_element_type=F32) + bias_ref[...]
            dost = _stack_heads(do_ref[pl.ds(r0, CHUNK), :])
            dost16 = dost.astype(BF16)
            dpt = lax.dot_general(vb, dost16, NT, preferred_element_type=F32)
            yield
            key = lax.broadcasted_iota(jnp.int32, (BAND, 2 * CHUNK), 0)
            st = jnp.where(key >= _first_key(c), st, -jnp.inf)
            mx = jnp.max(st, axis=0, keepdims=True)
            drow = _row_sums_on_lanes(dost * _stack_heads(o_ref[pl.ds(r0, CHUNK), :]))
            yield
            et = jnp.exp(st - mx)
            yield
            pt = et * (1.0 / jnp.sum(et, axis=0, keepdims=True))
            yield
            dst = pt * (dpt - drow)
            dst16 = dst.astype(BF16)
            yield
            db_scr[...] += dst
            dqn_scr[pl.ds(r0, CHUNK), :] = scale * _unstack_heads(_dot(dst.T, kb))
            yield
            dk_scr[pl.ds(r0, BAND), :] += lax.dot_general(dst16, qst, NN, preferred_element_type=F32)
            yield
            dv_scr[pl.ds(r0, BAND), :] += lax.dot_general(pt.astype(BF16), dost16, NN, preferred_element_type=F32)

        def chunk(i, carry):
            _lockstep([one_chunk(i * lock + a) for a in range(lock)])
            return carry

        lax.fori_loop(0, nc // lock, chunk, 0, unroll=max(1, min(ATTN_UNROLL, nc) // lock))

        def norm_bwd(dn, hat, r, g_ref):
            gd = dn * g_ref[...]
            return r * (gd - hat * _half_mean(gd * hat)), jnp.sum(dn * hat, axis=0, keepdims=True)

        dq, dgq = norm_bwd(dqn_scr[...], qhat, rq, gq_ref)
        dk, dgk = norm_bwd(dk_scr[KPAD:, :], khat, rk, gk_ref)
        dq_ref[...] = dq.astype(BF16)
        dk_ref[...] = dk.astype(BF16)
        dv_ref[...] = dv_scr[KPAD:, :].astype(BF16)
        dbias_ref[0] = db_scr[...]
        dgq_ref[0] = dgq
        dgk_ref[0] = dgk

    def col(off):
        return pl.BlockSpec((seq, 128), lambda b, hp: (b, off + hp))

    vec = pl.BlockSpec((1, 128), lambda b, hp: (0, 0))
    gvec = pl.BlockSpec((1, 1, 128), lambda b, hp: (b * (ATTN_HEADS // 2) + hp, 0, 0))
    t = nb * seq
    return _pallas(
        body, name="attn_bwd", grid=(nb, ATTN_HEADS // 2),
        in_specs=[col(0), col(4), col(8), col(0), col(0),
                  pl.BlockSpec((BAND, 2 * CHUNK), lambda b, hp: (hp, 0)), vec, vec],
        out_specs=[col(0), col(0), col(0), pl.BlockSpec((1, BAND, 2 * CHUNK), lambda b, hp: (b, hp, 0)),
                   gvec, gvec],
        out_shape=[jax.ShapeDtypeStruct((t, ATTN_WIDTH), BF16)] * 3
        + [jax.ShapeDtypeStruct((nb, ATTN_HEADS // 2 * BAND, 2 * CHUNK), F32)]
        + [jax.ShapeDtypeStruct((nb * ATTN_HEADS // 2, 1, 128), F32)] * 2,
        scratch_shapes=[pltpu.VMEM((seq, 128), BF16), pltpu.VMEM((seq + KPAD, 128), BF16),
                        pltpu.VMEM((seq + KPAD, 128), BF16), pltpu.VMEM((seq, 128), F32),
                        pltpu.VMEM((seq + KPAD, 128), F32), pltpu.VMEM((seq + KPAD, 128), F32),
                        pltpu.VMEM((BAND, 2 * CHUNK), F32)],
        sem=("parallel", "parallel"), args=(proj, proj, proj, out, dout, bias, gq, gk))


def _tri(lower):
    r = lax.broadcasted_iota(jnp.int32, (CHUNK, CHUNK), 0)
    c = lax.broadcasted_iota(jnp.int32, (CHUNK, CHUNK), 1)
    return (r >= c) if lower else (r <= c)


def _hgrn_gates(hq, hf, lb):
    sq = _sigmoid(hq)
    sf = _sigmoid(hf)
    return hq * sq, sq, sf, lb + (1.0 - lb) * sf


def _hgrn_offdiag(q_s, k_s, b_s):
    row = lax.broadcasted_iota(jnp.int32, (CHUNK, HGRN_HEAD_DIM), 0)
    bv, qv, kv = b_s[...], q_s[...], k_s[...]
    eqs, eks = [], []
    for i in range(1, N_SUB):
        r = b_s[pl.ds(SUB * i - 1, 1), :]
        in_i = (row >= SUB * i) & (row < SUB * (i + 1))
        eqs.append(jnp.exp(jnp.where(in_i, bv - r, -jnp.inf)))
        eks.append(jnp.exp(jnp.where(row < SUB * i, r - bv, -jnp.inf)))
    eq = jnp.concatenate(eqs, axis=1)
    ek = jnp.concatenate(eks, axis=1)
    qt = jnp.concatenate([qv] * (N_SUB - 1), axis=1) * eq
    kt = jnp.concatenate([kv] * (N_SUB - 1), axis=1) * ek
    return qt, kt, eq, ek


def _hgrn_diag_e(b_s, i, s):
    t_io = lax.broadcasted_iota(jnp.int32, (SUB, HGRN_HEAD_DIM), 0)
    bi = b_s[pl.ds(SUB * i, SUB), :]
    return jnp.exp(jnp.where(t_io >= s, bi - b_s[pl.ds(SUB * i + s, 1), :], -jnp.inf)), t_io


def _hgrn_intra(q_s, k_s, b_s, a_s, qt, kt):
    ktp = jnp.concatenate([kt, jnp.zeros_like(kt)], axis=0)
    a_s[...] = _dot(qt, ktp, NT)
    yield
    col = lax.broadcasted_iota(jnp.int32, (SUB, HGRN_HEAD_DIM), 1)
    for i in range(N_SUB):
        qi = q_s[pl.ds(SUB * i, SUB), :]
        ai = jnp.zeros((SUB, HGRN_HEAD_DIM), F32)
        for s in range(SUB):
            e, _ = _hgrn_diag_e(b_s, i, s)
            a_col = jnp.sum(qi * k_s[pl.ds(SUB * i + s, 1), :] * e, axis=-1, keepdims=True)
            ai = ai + jnp.where(col == SUB * i + s, a_col, 0.0)
            if s % DIAG_STAGE == DIAG_STAGE - 1:
                yield
        a_s[pl.ds(SUB * i, SUB), :] += ai


def _mixer_fwd(proj, bias, gq, gk, lb, go, nb, seq):
    nc = seq // CHUNK
    hd = HGRN_HEAD_DIM
    nblk = ATTN_HEADS // 2
    rows_blk = seq // nblk
    nck = rows_blk // CHUNK
    per = nc // nck
    assert rows_blk % CHUNK == 0

    def body(aq_ref, ak_ref, av_ref, bias_ref, gq_ref, gk_ref, hq_ref, hf_ref, hi_ref, hg_ref, lb_ref, go_ref,
             ao_ref, y_ref, o_ref, st_ref, a_ref, qs_scr, k_scr, v_scr, st_all, q_all, k_all, b_all, a_all):
        _attn_prepare(aq_ref, ak_ref, av_ref, gq_ref, gk_ref, qs_scr, k_scr, v_scr)

        @pl.when(pl.program_id(1) == 0)
        def _():
            st_all[...] = jnp.zeros_like(st_all)

        lower = _tri(True)

        def head_chunk(hh, c, rows):
            ln = slice(hd * hh, hd * (hh + 1))
            st, q_s, k_s, b_s, a_s = st_all.at[hh], q_all.at[hh], k_all.at[hh], b_all.at[hh], a_all.at[hh]
            q, _, _, f = _hgrn_gates(hq_ref[rows, ln], hf_ref[rows, ln], lb_ref[:, ln])
            v = hi_ref[rows, ln]
            yield
            b = _dot_exact01(lower, jnp.log(f))
            q_s[...] = q
            k_s[...] = 1.0 - f
            b_s[...] = b
            st_ref[hh, c] = st[...]
            yield
            qt, kt, _, _ = _hgrn_offdiag(q_s, k_s, b_s)
            yield
            yield from _hgrn_intra(q_s, k_s, b_s, a_s, qt, kt)
            a16 = a_s[...].astype(BF16)
            a_ref[hh, c] = a16
            vp = jnp.concatenate([v, jnp.zeros_like(v)], axis=0)
            o = _dot(a16, vp) + _dot(q * jnp.exp(b), st[...], NT)
            yield
            bl = b_s[pl.ds(CHUNK - 1, 1), :]
            st[...] = st[...] * jnp.exp(bl) + _tn(v, (1.0 - f) * jnp.exp(bl - b))
            o_ref[rows, ln] = o
            yield
            n = o * lax.rsqrt(jnp.mean(o * o, axis=-1, keepdims=True) + RMS_EPS) * go_ref[...]
            hg = hg_ref[rows, ln]
            y_ref[rows, ln] = n * hg * _sigmoid(hg)

        def chunk(c, carry):
            rows = pl.ds(pl.multiple_of(c * CHUNK, CHUNK), CHUNK)
            _lockstep([_attn_fwd_chunk(c * per + a, qs_scr, k_scr, v_scr, bias_ref, ao_ref) for a in range(per)]
                      + [head_chunk(hh, c, rows) for hh in range(HGRN_HEADS)])
            return carry

        lax.fori_loop(0, nck, chunk, 0)

    hp, wide = HGRN_HEADS, HGRN_HEADS * hd

    def acol(off):
        return pl.BlockSpec((seq, 128), lambda b, s: (b, off + s))

    def col(off):
        return pl.BlockSpec((rows_blk, wide), lambda b, s: (b * nblk + s, off // hp))

    out = pl.BlockSpec((rows_blk, wide), lambda b, s: (b * nblk + s, 0))
    vec = pl.BlockSpec((1, 128), lambda b, s: (0, 0))
    t = nb * seq
    return _pallas(
        body, name="mixer_fwd", grid=(nb, nblk),
        in_specs=[acol(0), acol(4), acol(8), pl.BlockSpec((2 * CHUNK, BAND), lambda b, s: (s, 0)), vec, vec,
                  col(12), col(16), col(20), col(24), pl.BlockSpec((1, wide), lambda b, s: (0, 0)), vec],
        out_specs=[pl.BlockSpec((seq, 128), lambda b, s: (b, s)), out, out,
                   pl.BlockSpec((hp, nck, hd, hd), lambda b, s: (b, s, 0, 0)),
                   pl.BlockSpec((hp, nck, CHUNK, hd), lambda b, s: (b, s, 0, 0))],
        out_shape=[jax.ShapeDtypeStruct((t, ATTN_WIDTH), F32)] + [jax.ShapeDtypeStruct((t, wide), F32)] * 2
        + [jax.ShapeDtypeStruct((nb * hp, nc, hd, hd), F32), jax.ShapeDtypeStruct((nb * hp, nc, CHUNK, hd), BF16)],
        scratch_shapes=[pltpu.VMEM((seq, 128), BF16), pltpu.VMEM((seq + KPAD, 128), BF16),
                        pltpu.VMEM((seq + KPAD, 128), BF16), pltpu.VMEM((hp, hd, hd), F32)]
        + [pltpu.VMEM((hp, CHUNK, hd), F32)] * 4,
        sem=("parallel", "arbitrary"), args=(proj,) * 3 + (bias, gq, gk) + (proj,) * 4 + (lb, go))


def _hgrn_bwd(proj, lb, go, o_pre, states, scores, dout, nb, seq):
    nc = seq // CHUNK
    hd = HGRN_HEAD_DIM
    rows_blk = min(HGRN_ROWS, seq)
    nblk, nck = seq // rows_blk, rows_blk // CHUNK

    def body(hq_ref, hf_ref, hi_ref, hg_ref, lb_ref, go_ref, o_ref, st_ref, a_ref, dy_ref,
             dhq_ref, dhf_ref, dhi_ref, dhg_ref, dlb_ref, dgo_ref,
             dst_all, q_all, k_all, b_all, da_all, dqi_all, dki_all, dlb_all, dgo_all):
        @pl.when(pl.program_id(1) == 0)
        def _():
            dst_all[...] = jnp.zeros_like(dst_all)
            dlb_all[...] = jnp.zeros_like(dlb_all)
            dgo_all[...] = jnp.zeros_like(dgo_all)

        lower, upper = _tri(True), _tri(False)
        gov = go_ref[...]
        row = lax.broadcasted_iota(jnp.int32, (CHUNK, hd), 0)

        def head_chunk(hh, c, rows):
            ln = slice(hd * hh, hd * (hh + 1))
            dst, q_s, k_s, b_s = dst_all.at[hh], q_all.at[hh], k_all.at[hh], b_all.at[hh]
            da_s, dqi_s, dki_s = da_all.at[hh], dqi_all.at[hh], dki_all.at[hh]
            dlb_acc, dgo_acc = dlb_all.at[hh], dgo_all.at[hh]
            lbv = lb_ref[:, ln]
            hq, hf, v, hg = hq_ref[rows, ln], hf_ref[rows, ln], hi_ref[rows, ln], hg_ref[rows, ln]
            q, sq, sf, f = _hgrn_gates(hq, hf, lbv)
            kk = 1.0 - f
            yield
            b = _dot_exact01(lower, jnp.log(f))
            q_s[...] = q
            k_s[...] = kk
            b_s[...] = b
            yield
            bl = b_s[pl.ds(CHUNK - 1, 1), :]
            ebl = jnp.exp(bl)
            ekd = jnp.exp(bl - b)
            kd = kk * ekd
            eb = jnp.exp(b)
            qb = q * eb
            st0 = st_ref[hh, c]
            dst1 = dst[...]
            yield

            o = o_ref[rows, ln]
            dy = dy_ref[rows, ln]
            sg = _sigmoid(hg)
            rstd = lax.rsqrt(jnp.mean(o * o, axis=-1, keepdims=True) + RMS_EPS)
            ohat = o * rstd
            dn = dy * hg * sg
            dhg_ref[rows, ln] = (dy * ohat * gov * (sg * (1.0 + hg * (1.0 - sg)))).astype(BF16)
            dgo_acc[...] += jnp.sum(dn * ohat, axis=0, keepdims=True)
            gdn = dn * gov
            do = rstd * (gdn - ohat * jnp.mean(gdn * ohat, axis=-1, keepdims=True))
            yield

            qt, kt, eq, ek = _hgrn_offdiag(q_s, k_s, b_s)
            da = _dot(do, v, NT)
            dat = _dot(v, do, NT)
            da_s[...] = da
            yield
            dqo = _dot(da, kt) * eq
            dko = _dot(dat, qt) * ek
            dqi_s[...] = sum(dqo[:, j * hd:(j + 1) * hd] for j in range(N_SUB - 1))
            dki_s[...] = sum(dko[:, j * hd:(j + 1) * hd] for j in range(N_SUB - 1))
            yield
            col = lax.broadcasted_iota(jnp.int32, (SUB, CHUNK), 1)
            for i in range(N_SUB):
                qi = q_s[pl.ds(SUB * i, SUB), :]
                dai = da_s[pl.ds(SUB * i, SUB), :]
                dqd = jnp.zeros((SUB, hd), F32)
                for s in range(SUB):
                    e, _ = _hgrn_diag_e(b_s, i, s)
                    dacol = jnp.sum(jnp.where(col == SUB * i + s, dai, 0.0), axis=-1, keepdims=True)
                    w = dacol * e
                    dqd = dqd + w * k_s[pl.ds(SUB * i + s, 1), :]
                    dki_s[pl.ds(SUB * i + s, 1), :] += jnp.sum(w * qi, axis=0, keepdims=True)
                    if s % DIAG_STAGE == DIAG_STAGE - 1:
                        yield
                dqi_s[pl.ds(SUB * i, SUB), :] += dqd
            dqi, dki = dqi_s[...], dki_s[...]

            dv = _tn(a_ref[hh, c].astype(F32), do)[0:CHUNK, :] + _dot(kd, dst1, NT)
            dqb = _dot(do, st0)
            dkd = _dot(v, dst1)
            yield
            t2 = dkd * kd
            dq = dqb * eb + dqi
            dk = dkd * ekd + dki
            dbl = jnp.sum(t2, axis=0, keepdims=True) + ebl * jnp.sum(st0 * dst1, axis=0, keepdims=True)
            db = dqb * qb - t2 + q * dqi - kk * dki + jnp.where(row == CHUNK - 1, dbl, 0.0)
            yield
            dg = _dot_exact01(upper, db)
            dst[...] = dst1 * ebl + _tn(do, qb)
            yield

            df = dg / f - dk
            dhf_ref[rows, ln] = (df * (1.0 - lbv) * sf * (1.0 - sf)).astype(BF16)
            dlb_acc[...] += jnp.sum(df * (1.0 - sf), axis=0, keepdims=True)
            dhq_ref[rows, ln] = (dq * (sq * (1.0 + hq * (1.0 - sq)))).astype(BF16)
            dhi_ref[rows, ln] = dv.astype(BF16)

        def chunk(it, carry):
            c = nck - 1 - it
            rows = pl.ds(pl.multiple_of(c * CHUNK, CHUNK), CHUNK)
            _lockstep([head_chunk(hh, c, rows) for hh in range(HGRN_HEADS)])
            return carry

        lax.fori_loop(0, nck, chunk, 0)

        @pl.when(pl.program_id(1) == nblk - 1)
        def _():
            dlb_ref[...] = dlb_all[...]
            dgo_ref[...] = dgo_all[...]

    hp, wide = HGRN_HEADS, HGRN_HEADS * hd

    def col(off):
        return pl.BlockSpec((rows_blk, wide), lambda b, s: (b * nblk + nblk - 1 - s, off // hp))

    out = pl.BlockSpec((rows_blk, wide), lambda b, s: (b * nblk + nblk - 1 - s, 0))
    part = pl.BlockSpec((hp, 1, hd), lambda b, s: (b, 0, 0))
    t = nb * seq
    return pl.pallas_call(
        body, name="hgrn_bwd", grid=(nb, nblk),
        in_specs=[col(12), col(16), col(20), col(24), pl.BlockSpec((1, wide), lambda b, s: (0, 0)),
                  pl.BlockSpec((1, hd), lambda b, s: (0, 0)), out,
                  pl.BlockSpec((hp, nck, hd, hd), lambda b, s: (b, nblk - 1 - s, 0, 0)),
                  pl.BlockSpec((hp, nck, CHUNK, hd), lambda b, s: (b, nblk - 1 - s, 0, 0)), col(4)],
        out_specs=[out, out, out, out, part, part],
        out_shape=[jax.ShapeDtypeStruct((t, wide), BF16)] * 4 + [jax.ShapeDtypeStruct((nb * hp, 1, hd), F32)] * 2,
        scratch_shapes=[pltpu.VMEM((hp, hd, hd), F32)] + [pltpu.VMEM((hp, CHUNK, hd), F32)] * 3
        + [pltpu.VMEM((hp, CHUNK, CHUNK), F32)] + [pltpu.VMEM((hp, CHUNK, hd), F32)] * 2
        + [pltpu.VMEM((hp, 1, hd), F32)] * 2,
        compiler_params=_params("parallel", "arbitrary"),
    )(proj, proj, proj, proj, lb, go, o_pre, states, scores, dout)


def _lb_fwd(lower_bounds):
    def body(x_ref, o_ref):
        xv = x_ref[...]
        e = jnp.exp(xv - jnp.max(xv, axis=0, keepdims=True))
        o_ref[...] = e[0:1, :] / jnp.sum(e, axis=0, keepdims=True)

    return pl.pallas_call(body, name="lb_fwd",
                          out_shape=jax.ShapeDtypeStruct((1, lower_bounds.shape[1]), F32))(lower_bounds)


def _lb_bwd(lower_bounds, dlb_parts):
    ng = dlb_parts.shape[0]

    def body(x_ref, d_ref, o_ref):
        xv = x_ref[...]
        e = jnp.exp(xv - jnp.max(xv, axis=0, keepdims=True))
        p = e / jnp.sum(e, axis=0, keepdims=True)
        dlb = d_ref[0]
        for gi in range(1, ng):
            dlb = dlb + d_ref[gi]
        first = lax.broadcasted_iota(jnp.int32, xv.shape, 0) == 0
        o_ref[...] = p * (jnp.where(first, dlb, 0.0) - p[0:1, :] * dlb)

    return pl.pallas_call(body, name="lb_bwd",
                          out_shape=jax.ShapeDtypeStruct(lower_bounds.shape, F32))(lower_bounds, dlb_parts)


def _ffn_bwd(x, g, h, gate, up, dy, dy16, w, put, tag):
    wg, wu, wd = w[tag + "_w_gate"], w[tag + "_w_up"], w[tag + "_w_down"]
    dgate, dup, dwd = _ffn_bwd_mid(dy16, wd, gate, up, tag + "_bwd_mid")
    put(tag + "_w_down", dwd)
    put(tag + "_w_gate", _mm(dgate, h, ta=True, tm=1408, tn=512, name=tag + "_dwg"))
    put(tag + "_w_up", _mm(dup, h, ta=True, tm=1408, tn=512, name=tag + "_dwu"))
    dh = _mm(dgate, wg, tm=512, tn=1024, name=tag + "_dh_gate")
    return _mm(dup, wu, tm=512, tn=1024, add=dh, norm_bwd=(x, g, dy), name=tag + "_dh_up")


def _local_step(x, tgt, sp, w, put, nb, seq):
    d = x.shape[1]
    h1 = _rms_fwd(x, sp["ffn1_norm_g"], "ffn1_norm")
    rb_pad = jnp.pad(sp["attn_rel_bias"], ((0, 0), (0, N_REL_PAD - N_REL)))
    bias = jnp.transpose(_bias_expand(rb_pad), (1, 0, 2)).reshape(ATTN_HEADS * CHUNK, BAND)
    gq2 = jnp.concatenate([sp["attn_q_norm_g"]] * 2, axis=1)
    gk2 = jnp.concatenate([sp["attn_k_norm_g"]] * 2, axis=1)
    lb = _lb_fwd(sp["hgrn_lower_bounds"])
    gate1, up1, act1 = _ffn_up(h1, w["ffn1_w_gate"], w["ffn1_w_up"], "ffn1_up")
    x1, h2 = _mm(act1, w["ffn1_w_down"], tm=512, tn=d, add=x, scale=0.5, norm_g=sp["mix_norm_g"],
                 name="ffn1_down")
    proj = _mm(h2, w["w_in"], tb=True, tm=256, tn=w["w_in"].shape[0], name="in_proj")
    attn, hy, ho, hstate, hscore = _mixer_fwd(proj, bias, gq2, gk2, lb, sp["hgrn_out_norm_g"], nb, seq)
    mix = jnp.concatenate([attn, hy], axis=1)
    x2, h3 = _mm(mix, w["w_out"], tm=512, tn=1024, add=x1, norm_g=sp["ffn2_norm_g"], name="out_proj")
    gate2, up2, dx3, dx3_16, sq = _ffn_fwd(h3, x2, w["ffn2_w_gate"], w["ffn2_w_up"], w["ffn2_w_down"], "ffn2_fwd",
                                           tgt=tgt)
    loss = 0.5 * jnp.sum(sq) / d

    dx2, dx2_16, dg3 = _ffn_bwd(x2, sp["ffn2_norm_g"], h3, gate2, up2, dx3, dx3_16, w, put, "ffn2")
    dmix = _mm(dx2_16, w["w_out"], tb=True, tm=512, tn=1024, name="out_proj_dx")
    put("w_out", _mm(mix, dx2_16, ta=True, tm=512, tn=1024, name="out_proj_dw"))
    bias_t = jnp.transpose(bias.reshape(ATTN_HEADS // 2, 2 * CHUNK, BAND), (0, 2, 1)).reshape(-1, 2 * CHUNK)
    dq, dk, dv, dbias, dgq, dgk = _attn_bwd(proj, attn, dmix, bias_t, gq2, gk2, nb, seq)
    dbias = jnp.transpose(dbias.reshape(nb, ATTN_HEADS // 2, BAND, 2, CHUNK), (0, 4, 1, 3, 2))
    dbias = dbias.reshape(nb, CHUNK, ATTN_HEADS, BAND)
    dgq = jnp.sum(dgq, axis=(0, 1)).reshape(2, ATTN_HEAD_DIM).sum(axis=0, keepdims=True)
    dgk = jnp.sum(dgk, axis=(0, 1)).reshape(2, ATTN_HEAD_DIM).sum(axis=0, keepdims=True)
    dhq, dhf, dhi, dhg, dlb, dgo = _hgrn_bwd(proj, lb, sp["hgrn_out_norm_g"], ho, hstate, hscore, dmix, nb, seq)
    dproj = jnp.concatenate([dq, dk, dv, dhq, dhf, dhi, dhg], axis=1)
    put("w_in", _mm(dproj, h2, ta=True, tm=512, tn=1024, name="in_proj_dw"))
    dx1, dx1_16, dgm = _mm(dproj, w["w_in"], tm=512, tn=1024, norm_bwd=(x1, sp["mix_norm_g"], dx2),
                           name="in_proj_dx")
    dx0, _, dg1 = _ffn_bwd(x, sp["ffn1_norm_g"], h1, gate1, up1, dx1, dx1_16, w, put, "ffn1")

    small = {
        "ffn1_norm_g": dg1, "mix_norm_g": dgm, "ffn2_norm_g": dg3,
        "attn_q_norm_g": dgq, "attn_k_norm_g": dgk,
        "attn_rel_bias": _bias_fold(dbias)[:, :N_REL],
        "hgrn_lower_bounds": _lb_bwd(sp["hgrn_lower_bounds"], dlb.reshape(nb, 1, HGRN_HEADS * HGRN_HEAD_DIM)),
        "hgrn_out_norm_g": jnp.sum(dgo, axis=(0, 1))[None, :],
    }
    return loss, dx0, small


MESH = pl.DeviceIdType.MESH
ANY = pl.BlockSpec(memory_space=pl.ANY)


def _coords():
    return lax.axis_index("x"), lax.axis_index("y"), lax.axis_index("c")


def _other_chips(x, y):
    return [(1 - x, y), (x, 1 - y), (1 - x, 1 - y)]


def _gather_side(shards):
    n = len(shards)

    def copies(ins, outs, sems):
        send_sems, recv_sems, local_sems = sems
        x, y, c = _coords()
        xn, yn, dg = (1 - x, y), (x, 1 - y), (1 - x, 1 - y)

        def copy(i, k, block, to, half=None, src=None):
            bx, by, bc = block
            dst = outs[i].at[4 * bx + 2 * by + bc]
            if half is not None:
                rows = shards[i].shape[0] // 2
                dst = dst.at[pl.ds(half * rows, rows)]
            return pltpu.make_async_remote_copy(
                src_ref=dst if src is None else src, dst_ref=dst, send_sem=send_sems.at[i, k],
                recv_sem=recv_sems.at[i, k], device_id=to, device_id_type=MESH)

        mine = [pltpu.make_async_copy(ins[i], outs[i].at[4 * x + 2 * y + c], local_sems.at[i]) for i in range(n)]
        return copy, mine, (x, y, c), (x, y, 1 - c), xn, yn, dg, c

    def own(copy, i, ins, me, sibling, xn, yn, c):
        return [copy(i, 0, me, sibling, src=ins[i]), copy(i, 1, me, (*xn, c), src=ins[i]),
                copy(i, 2, me, (*yn, c), src=ins[i])]

    def passed_on(copy, i, sibling, xn, yn, c):
        return [copy(i, 3, (*xn, c), sibling), copy(i, 5, (*xn, c), (*yn, c), half=0),
                copy(i, 4, (*yn, c), sibling), copy(i, 6, (*yn, c), (*xn, c), half=1)]

    def diagonal(copy, i, sibling, dg, c):
        return [copy(i, 7, (*dg, c), sibling, half=0), copy(i, 8, (*dg, c), sibling, half=1)]

    def start(ins, outs, sems):
        copy, mine, me, sibling, xn, yn, dg, c = copies(ins, outs, sems)
        for cp in mine + [cp for i in range(n) for cp in own(copy, i, ins, me, sibling, xn, yn, c)]:
            cp.start()

    def middle(ins, outs, sems):
        copy, mine, me, sibling, xn, yn, dg, c = copies(ins, outs, sems)
        for i in range(n):
            fwd_x, relay_x, fwd_y, relay_y = passed_on(copy, i, sibling, xn, yn, c)
            copy(i, 1, (*xn, c), me).wait_recv()
            fwd_x.start()
            relay_x.start()
            copy(i, 2, (*yn, c), me).wait_recv()
            fwd_y.start()
            relay_y.start()

    def finish(ins, outs, sems):
        copy, mine, me, sibling, xn, yn, dg, c = copies(ins, outs, sems)
        for i in range(n):
            top, bottom = diagonal(copy, i, sibling, dg, c)
            copy(i, 5, (*dg, c), me, half=0).wait_recv()
            top.start()
            copy(i, 6, (*dg, c), me, half=1).wait_recv()
            bottom.start()
        for i in range(n):
            copy(i, 0, sibling, me).wait_recv()
            copy(i, 3, (*xn, 1 - c), me).wait_recv()
            copy(i, 4, (*yn, 1 - c), me).wait_recv()
            copy(i, 7, (*dg, 1 - c), me, half=0).wait_recv()
            copy(i, 8, (*dg, 1 - c), me, half=1).wait_recv()
        for i in range(n):
            for cp in (own(copy, i, ins, me, sibling, xn, yn, c) + passed_on(copy, i, sibling, xn, yn, c)
                       + diagonal(copy, i, sibling, dg, c)):
                cp.wait_send()
        for cp in mine:
            cp.wait()

    return _Side(list(shards), [jax.ShapeDtypeStruct((N_DEV,) + s.shape, s.dtype) for s in shards],
                 [pltpu.SemaphoreType.DMA((n, 9)), pltpu.SemaphoreType.DMA((n, 9)), pltpu.SemaphoreType.DMA((n,))],
                 start, finish, middle)


def _pair_side(grads):
    n = len(grads)

    def copies(ins, outs, sems):
        send_sems, recv_sems = sems
        x, y, c = _coords()
        return [pltpu.make_async_remote_copy(
            src_ref=ins[i].at[2 * k + 1 - c], dst_ref=outs[i].at[k], send_sem=send_sems.at[i, k],
            recv_sem=recv_sems.at[i, k], device_id=(x, y, 1 - c), device_id_type=MESH)
            for i in range(n) for k in range(4)]

    def start(ins, outs, sems):
        for cp in copies(ins, outs, sems):
            cp.start()

    def finish(ins, outs, sems):
        for cp in copies(ins, outs, sems):
            cp.wait()

    return _Side(list(grads), [jax.ShapeDtypeStruct((4,) + g.shape[1:], g.dtype) for g in grads],
                 [pltpu.SemaphoreType.DMA((n, 4)), pltpu.SemaphoreType.DMA((n, 4))], start, finish)


def _pair_add(grads, recvs, core, name):
    count = len(grads)

    def body(c_ref, *refs):
        for n in range(count):
            refs[2 * count + n][...] = (refs[2 * n][...] + refs[2 * n + 1][...]).astype(BF16)

    in_specs, out_specs = [], []
    for g in grads:
        blk = (1,) + g.shape[1:]
        in_specs += [pl.BlockSpec(blk, lambda k, c_ref: (2 * k + c_ref[0], 0, 0)),
                     pl.BlockSpec(blk, lambda k, c_ref: (k, 0, 0))]
        out_specs.append(pl.BlockSpec(blk, lambda k, c_ref: (k, 0, 0)))
    out = pl.pallas_call(
        body, name=name,
        grid_spec=pltpu.PrefetchScalarGridSpec(num_scalar_prefetch=1, grid=(4,), in_specs=in_specs,
                                               out_specs=out_specs),
        out_shape=[jax.ShapeDtypeStruct((4,) + g.shape[1:], BF16) for g in grads],
        compiler_params=_params("arbitrary"),
    )(core, *[a for pair in zip(grads, recvs) for a in pair])
    return list(out)


def _chip_side(parts):
    n = len(parts)

    def copies(ins, outs, sems):
        send_sems, recv_sems, local_sems = sems
        x, y, c = _coords()
        chips = _other_chips(x, y)
        mine = [pltpu.make_async_copy(ins[i].at[2 * x + y], outs[i].at[2 * x + y], local_sems.at[i])
                for i in range(n)]
        sent = [pltpu.make_async_remote_copy(
            src_ref=ins[i].at[2 * px + py], dst_ref=outs[i].at[2 * x + y], send_sem=send_sems.at[i, j],
            recv_sem=recv_sems.at[i, j], device_id=(px, py, c), device_id_type=MESH)
            for i in range(n) for j, (px, py) in enumerate(chips)]
        return mine, sent, chips, c

    def start(ins, outs, sems):
        mine, sent, _, _ = copies(ins, outs, sems)
        for cp in mine + sent:
            cp.start()

    def finish(ins, outs, sems):
        mine, sent, chips, c = copies(ins, outs, sems)
        send_sems, recv_sems, _ = sems
        for i in range(n):
            for j, (px, py) in enumerate(chips):
                landed = outs[i].at[2 * px + py]
                pltpu.make_async_remote_copy(
                    src_ref=landed, dst_ref=landed, send_sem=send_sems.at[i, j], recv_sem=recv_sems.at[i, j],
                    device_id=(px, py, c), device_id_type=MESH).wait_recv()
        for cp in sent:
            cp.wait_send()
        for cp in mine:
            cp.wait()

    return _Side(list(parts), [jax.ShapeDtypeStruct(p.shape, p.dtype) for p in parts],
                 [pltpu.SemaphoreType.DMA((n, 3)), pltpu.SemaphoreType.DMA((n, 3)), pltpu.SemaphoreType.DMA((n,))],
                 start, finish)


def _all_reduce_small(v):
    r = v.shape[0]

    def body(v_ref, o_ref, buf, send_sems, recv_sems):
        x, y, c = _coords()
        me = 4 * x + 2 * y + c
        buf[me] = v_ref[...]
        cps = []
        for k in range(1, N_DEV):
            px = 1 - x if k & 4 else x
            py = 1 - y if k & 2 else y
            pc = 1 - c if k & 1 else c
            cps.append((pltpu.make_async_remote_copy(
                src_ref=v_ref, dst_ref=buf.at[me], send_sem=send_sems.at[k - 1], recv_sem=recv_sems.at[k - 1],
                device_id=(px, py, pc), device_id_type=MESH), 4 * px + 2 * py + pc))
        for cp, _ in cps:
            cp.start()
        for k, (cp, peer) in enumerate(cps):
            pltpu.make_async_remote_copy(
                src_ref=v_ref, dst_ref=buf.at[peer], send_sem=send_sems.at[k], recv_sem=recv_sems.at[k],
                device_id=(x, y, c), device_id_type=MESH).wait_recv()
        for cp, _ in cps:
            cp.wait_send()
        acc = buf[0]
        for j in range(1, N_DEV):
            acc = acc + buf[j]
        o_ref[...] = acc

    return pl.pallas_call(
        body, name="small_all_reduce", out_shape=jax.ShapeDtypeStruct(v.shape, F32),
        in_specs=[pl.BlockSpec(memory_space=pltpu.VMEM)], out_specs=pl.BlockSpec(memory_space=pltpu.VMEM),
        scratch_shapes=[pltpu.VMEM((N_DEV, r, 128), F32), pltpu.SemaphoreType.DMA((N_DEV - 1,)),
                        pltpu.SemaphoreType.DMA((N_DEV - 1,))],
    )(v)


def _adamw(ws, ms, vs, gs, name):
    count = len(ws)
    parts = ws[0].ndim == 3
    steps = 4 if all(w.shape[-2] % 32 == 0 for w in ws) else 1

    def body(*refs):
        for n in range(count):
            w_ref, m_ref, v_ref, g_ref = refs[4 * n:4 * n + 4]
            go_ref, d_ref, mo_ref, vo_ref = refs[4 * count + 4 * n:4 * count + 4 * n + 4]
            if parts:
                gv = g_ref[0].astype(F32)
                for k in range(1, 4):
                    gv = gv + g_ref[k].astype(F32)
                gv = gv[None]
            else:
                gv = g_ref[...]
            m2 = ADAM_B1 * m_ref[...] + (1.0 - ADAM_B1) * gv
            v2 = ADAM_B2 * v_ref[...] + (1.0 - ADAM_B2) * (gv * gv)
            m_hat = m2 / (1.0 - ADAM_B1 ** ADAM_STEP)
            v_hat = v2 / (1.0 - ADAM_B2 ** ADAM_STEP)
            go_ref[...] = gv
            d_ref[...] = -ADAM_LR * (m_hat / (jnp.sqrt(v_hat) + ADAM_EPS) + ADAM_WD * w_ref[...])
            mo_ref[...] = m2
            vo_ref[...] = v2

    in_specs, out_specs, out_shape = [], [], []
    for w in ws:
        r, cdim = w.shape[-2:]
        if parts:
            row = pl.BlockSpec((1, r // steps, cdim), lambda i: (0, i, 0))
            g_spec = pl.BlockSpec((4, r // steps, cdim), lambda i: (0, i, 0))
        else:
            row = g_spec = pl.BlockSpec((r // steps, cdim), lambda i: (i, 0))
        in_specs += [row, row, row, g_spec]
        out_specs += [row] * 4
        out_shape += [jax.ShapeDtypeStruct(w.shape, F32)] * 4
    args = [a for group in zip(ws, ms, vs, gs) for a in group]
    out = pl.pallas_call(
        body, name=name, grid=(steps,), in_specs=in_specs, out_specs=out_specs, out_shape=out_shape,
        compiler_params=_params("parallel"),
    )(*args)
    return [out[4 * n:4 * n + 4] for n in range(count)]


def _adamw_small(ws, ms, vs, gs):
    count = len(ws)

    def body(*refs):
        for n in range(count):
            w_ref, m_ref, v_ref, g_ref = refs[4 * n:4 * n + 4]
            d_ref, mo_ref, vo_ref = refs[4 * count + 3 * n:4 * count + 3 * n + 3]
            gv = g_ref[...]
            m2 = ADAM_B1 * m_ref[...] + (1.0 - ADAM_B1) * gv
            v2 = ADAM_B2 * v_ref[...] + (1.0 - ADAM_B2) * (gv * gv)
            m_hat = m2 / (1.0 - ADAM_B1 ** ADAM_STEP)
            v_hat = v2 / (1.0 - ADAM_B2 ** ADAM_STEP)
            d_ref[...] = -ADAM_LR * (m_hat / (jnp.sqrt(v_hat) + ADAM_EPS) + ADAM_WD * w_ref[...])
            mo_ref[...] = m2
            vo_ref[...] = v2

    out = pl.pallas_call(
        body, name="small_adamw",
        out_shape=[jax.ShapeDtypeStruct(w.shape, F32) for w in ws for _ in range(3)],
    )(*[a for group in zip(ws, ms, vs, gs) for a in group])
    return [out[3 * n:3 * n + 3] for n in range(count)]


WEIGHTS = ["ffn1_norm_g", "ffn1_w_gate", "ffn1_w_up", "ffn1_w_down", "mix_norm_g", "w_in", "attn_q_norm_g",
           "attn_k_norm_g", "attn_rel_bias", "hgrn_lower_bounds", "hgrn_out_norm_g", "w_out", "ffn2_norm_g",
           "ffn2_w_gate", "ffn2_w_up", "ffn2_w_down"]
COL_SHARDED = ("ffn1_w_gate", "ffn1_w_up", "w_in", "ffn2_w_gate", "ffn2_w_up")
ROW_SHARDED = ("ffn1_w_down", "w_out", "ffn2_w_down")
BIG = [n for n in WEIGHTS if n in COL_SHARDED or n in ROW_SHARDED]
SMALL = [n for n in WEIGHTS if n not in BIG]
PACK_ROWS = 8
FFN2 = ["ffn2_w_down", "ffn2_w_gate", "ffn2_w_up"]
MIXER = ["w_out", "w_in"]

PLAN = {
    "ffn1_norm": [("gather", ["ffn1_w_gate"])],
    "bias_expand": [("gather", ["ffn1_w_up"])],
    "ffn1_up": [("gather", ["ffn1_w_down", "w_out"])],
    "ffn1_down": [("gather", ["w_in"])],
    "mixer_fwd": [("gather", FFN2)],
    "ffn2_dh_gate": [("pair", FFN2)],
    "attn_bwd": [("chip", FFN2)],
    "in_proj_dx": [("pair", MIXER)],
    "ffn1_bwd_mid": [("chip", MIXER)],
    "ffn1_dwg": [("pair", ["ffn1_w_down"])],
    "ffn1_dwu": [("chip", ["ffn1_w_down"]), ("pair", ["ffn1_w_gate"])],
    "ffn1_dh_gate": [("chip", ["ffn1_w_gate"]), ("pair", ["ffn1_w_up"])],
    "bias_fold": [("chip", ["ffn1_w_up"])],
}


def _join_sides(sides):
    def split(refs, counts):
        out, at = [], 0
        for n in counts:
            out.append(refs[at:at + n])
            at += n
        return out

    n_in, n_out, n_sem = ([len(getattr(s, f)) for s in sides] for f in ("ins", "out_shape", "sems"))

    def run(which):
        def go(ins, outs, sems):
            for s, i, o, m in zip(sides, split(ins, n_in), split(outs, n_out), split(sems, n_sem)):
                if getattr(s, which) is not None:
                    getattr(s, which)(i, o, m)
        return go

    return _Side([a for s in sides for a in s.ins], [a for s in sides for a in s.out_shape],
                 [a for s in sides for a in s.sems], run("start"), run("finish"),
                 run("middle") if any(s.middle is not None for s in sides) else None)


class _Schedule:
    def __init__(self, shards):
        self.shards = shards
        self.weights = {}
        self.sliced = {}
        self.partials = {}
        self.reduced = {}

    def put(self, name, grad):
        self.sliced[name] = grad.reshape((N_DEV,) + self.shards[name].shape)

    def side_for(self, call):
        if call not in PLAN:
            return None
        sides = []
        for kind, names in PLAN[call]:
            if kind == "gather":
                sides.append(_gather_side([self.shards[n] for n in names]))
            elif kind == "pair":
                sides.append(_pair_side([self.sliced[n] for n in names]))
            else:
                sides.append(_chip_side([self.partials[n] for n in names]))
        return _join_sides(sides)

    def done(self, call, outs):
        at = 0
        for kind, names in PLAN[call]:
            self.file(kind, names, outs[at:at + len(names)])
            at += len(names)

    def file(self, kind, names, outs):
        if kind == "pair":
            core = lax.axis_index("c").astype(jnp.int32).reshape(1)
            sums = _pair_add([self.sliced[n] for n in names], list(outs), core, names[0] + "_pair_add")
            self.partials.update(dict(zip(names, sums)))
            return
        for n, o in zip(names, outs):
            if kind == "gather":
                self.weights[n] = o.reshape(N_DEV * o.shape[1], o.shape[2])
            else:
                self.reduced[n] = o


def _pack_small(vals, loss=None):
    parts = []
    for n in SMALL:
        a = vals[n]
        if n == "attn_rel_bias":
            a = jnp.pad(a.reshape(ATTN_HEADS, N_REL), ((0, 0), (0, N_REL_PAD - N_REL)))
        flat = a.reshape(-1)
        size = -(-flat.shape[0] // (PACK_ROWS * 128)) * PACK_ROWS * 128
        parts.append(jnp.pad(flat, (0, size - flat.shape[0])).reshape(-1, 128))
    tail = jnp.zeros((PACK_ROWS, 128), F32)
    if loss is not None:
        tail = tail.at[0, 0].set(loss)
    return jnp.concatenate(parts + [tail], axis=0)


def _unpack_small(packed, shapes):
    out, row = {}, 0
    for n in SMALL:
        shape = shapes[n]
        if n == "attn_rel_bias":
            rows = ATTN_HEADS * N_REL_PAD // 128
            out[n] = packed[row:row + rows].reshape(ATTN_HEADS, N_REL_PAD)[:, :N_REL].reshape(shape)
        else:
            size = 1
            for s in shape:
                size *= s
            rows = -(-size // (PACK_ROWS * 128)) * PACK_ROWS
            out[n] = packed[row:row + rows].reshape(-1)[:size].reshape(shape)
        row += rows
    return out, packed[row, 0]


def kernel(x, ffn1_norm_g, ffn1_w_gate, ffn1_w_up, ffn1_w_down, mix_norm_g, w_in, attn_q_norm_g, attn_k_norm_g, attn_rel_bias, hgrn_lower_bounds, hgrn_out_norm_g, w_out, ffn2_norm_g, ffn2_w_gate, ffn2_w_up, ffn2_w_down, loss_target, m_ffn1_norm_g, m_ffn1_w_gate, m_ffn1_w_up, m_ffn1_w_down, m_mix_norm_g, m_w_in, m_attn_q_norm_g, m_attn_k_norm_g, m_attn_rel_bias, m_hgrn_lower_bounds, m_hgrn_out_norm_g, m_w_out, m_ffn2_norm_g, m_ffn2_w_gate, m_ffn2_w_up, m_ffn2_w_down, v_ffn1_norm_g, v_ffn1_w_gate, v_ffn1_w_up, v_ffn1_w_down, v_mix_norm_g, v_w_in, v_attn_q_norm_g, v_attn_k_norm_g, v_attn_rel_bias, v_hgrn_lower_bounds, v_hgrn_out_norm_g, v_w_out, v_ffn2_norm_g, v_ffn2_w_gate, v_ffn2_w_up, v_ffn2_w_down):
    wts = dict(zip(WEIGHTS, (ffn1_norm_g, ffn1_w_gate, ffn1_w_up, ffn1_w_down, mix_norm_g, w_in, attn_q_norm_g,
                             attn_k_norm_g, attn_rel_bias, hgrn_lower_bounds, hgrn_out_norm_g, w_out, ffn2_norm_g,
                             ffn2_w_gate, ffn2_w_up, ffn2_w_down)))
    mom = dict(zip(WEIGHTS, (m_ffn1_norm_g, m_ffn1_w_gate, m_ffn1_w_up, m_ffn1_w_down, m_mix_norm_g, m_w_in,
                             m_attn_q_norm_g, m_attn_k_norm_g, m_attn_rel_bias, m_hgrn_lower_bounds,
                             m_hgrn_out_norm_g, m_w_out, m_ffn2_norm_g, m_ffn2_w_gate, m_ffn2_w_up, m_ffn2_w_down)))
    var = dict(zip(WEIGHTS, (v_ffn1_norm_g, v_ffn1_w_gate, v_ffn1_w_up, v_ffn1_w_down, v_mix_norm_g, v_w_in,
                             v_attn_q_norm_g, v_attn_k_norm_g, v_attn_rel_bias, v_hgrn_lower_bounds,
                             v_hgrn_out_norm_g, v_w_out, v_ffn2_norm_g, v_ffn2_w_gate, v_ffn2_w_up, v_ffn2_w_down)))
    nb, seq, d = x.shape
    shapes = {n: wts[n].shape for n in WEIGHTS}

    def rows_first(a, n):
        return jnp.swapaxes(a, 1, 2) if n in COL_SHARDED else a

    sched = _Schedule({n: rows_first(wts[n], n)[0].astype(BF16) for n in BIG})
    sp = {n: wts[n] for n in SMALL}
    sp["attn_rel_bias"] = wts["attn_rel_bias"][0]
    _ACTIVE[0] = sched
    try:
        loss, dx, dsmall = _local_step(x.reshape(nb * seq, d), loss_target.reshape(nb * seq, d), sp,
                                       sched.weights, sched.put, nb, seq)
    finally:
        _ACTIVE[0] = None
    reduced = sched.reduced

    small_sum = _all_reduce_small(_pack_small(dsmall, loss))
    gsmall, loss_total = _unpack_small(small_sum, shapes)

    grads, deltas, new_m, new_v = {}, {}, {}, {}
    for group, tag in (([n for n in BIG if n not in MIXER], "ffn_adamw"), (MIXER, "mixer_adamw")):
        outs = _adamw([rows_first(wts[n], n) for n in group], [rows_first(mom[n], n) for n in group],
                      [rows_first(var[n], n) for n in group], [reduced[n] for n in group], tag)
        for n, out in zip(group, outs):
            grads[n], deltas[n], new_m[n], new_v[n] = (rows_first(o, n) for o in out)
    outs = _adamw_small([wts[n] for n in SMALL], [mom[n] for n in SMALL], [var[n] for n in SMALL],
                        [gsmall[n] for n in SMALL])
    for n, (delta, m2, v2) in zip(SMALL, outs):
        deltas[n], new_m[n], new_v[n] = delta, m2, v2
    grads.update(gsmall)

    return (loss_total, dx.reshape(nb, seq, d), *[grads[n] for n in WEIGHTS], *[deltas[n] for n in WEIGHTS],
            *[new_m[n] for n in WEIGHTS], *[new_v[n] for n in WEIGHTS])
```

```python
import functools

import jax
import jax.numpy as jnp
from jax import lax
from jax.experimental import pallas as pl
from jax.experimental.pallas import tpu as pltpu

F32 = jnp.float32
BF16 = jnp.bfloat16

RMS_EPS = 1e-6
CHUNK = 64
LEFT_CHUNKS = 8
BAND = (LEFT_CHUNKS + 2) * CHUNK
KPAD = BAND - CHUNK
REL_CLIP = 128
N_REL = 2 * REL_CLIP + 1
N_REL_PAD = 384
ATTN_HEADS = 8
ATTN_HEAD_DIM = 64
ATTN_WIDTH = ATTN_HEADS * ATTN_HEAD_DIM
ATTN_LOCKSTEP = 4
ATTN_UNROLL = 8
HGRN_HEADS = 4
HGRN_HEAD_DIM = 128
HGRN_ROWS = 512
SUB = 16
N_SUB = CHUNK // SUB
DIAG_STAGE = 4
N_DEV = 8

ADAM_LR = 0.001
ADAM_B1 = 0.9
ADAM_B2 = 0.999
ADAM_EPS = 1e-08
ADAM_WD = 0.01
ADAM_STEP = 10

VMEM_LIMIT = 56 * 1024 * 1024

NT = (((1,), (1,)), ((), ()))
NN = (((1,), (0,)), ((), ()))


def _params(*sem):
    return pltpu.CompilerParams(dimension_semantics=sem, vmem_limit_bytes=VMEM_LIMIT)


def _sigmoid(v):
    return 0.5 * jnp.tanh(0.5 * v) + 0.5


def _dot(a, b, dims=NN):
    return lax.dot_general(a.astype(BF16), b.astype(BF16), dims, preferred_element_type=F32)


def _dot_exact01(m01, v):
    m = m01.astype(BF16)
    hi = v.astype(BF16)
    r1 = v - hi.astype(F32)
    mid = r1.astype(BF16)
    lo = (r1 - mid.astype(F32)).astype(BF16)
    out = lax.dot_general(m, hi, NN, preferred_element_type=F32)
    out = out + lax.dot_general(m, mid, NN, preferred_element_type=F32)
    return out + lax.dot_general(m, lo, NN, preferred_element_type=F32)


def _dot_exact01_r(v, m01):
    m = m01.astype(BF16)
    hi = v.astype(BF16)
    r1 = v - hi.astype(F32)
    mid = r1.astype(BF16)
    lo = (r1 - mid.astype(F32)).astype(BF16)
    out = lax.dot_general(hi, m, NN, preferred_element_type=F32)
    out = out + lax.dot_general(mid, m, NN, preferred_element_type=F32)
    return out + lax.dot_general(lo, m, NN, preferred_element_type=F32)


def _lockstep(stages):
    live = list(stages)
    while live:
        still = []
        for g in live:
            try:
                next(g)
                still.append(g)
            except StopIteration:
                pass
        live = still


def _row_sums_on_lanes(v):
    ones = jnp.ones((8, v.shape[1]), BF16)
    hi = v.astype(BF16)
    r1 = v - hi.astype(F32)
    mid = r1.astype(BF16)
    lo = (r1 - mid.astype(F32)).astype(BF16)
    out = lax.dot_general(ones, hi, NT, preferred_element_type=F32)
    out = out + lax.dot_general(ones, mid, NT, preferred_element_type=F32)
    return (out + lax.dot_general(ones, lo, NT, preferred_element_type=F32))[0:1, :]


def _tn(a, b):
    ap = jnp.concatenate([a, jnp.zeros_like(a)], axis=0)
    bp = jnp.concatenate([b, jnp.zeros_like(b)], axis=0)
    return _dot(ap.T, bp)


def _row_tile(t):
    for tm in (512, 256, 128, 64, 32, 16, 8):
        if t % tm == 0:
            return tm
    raise ValueError(t)


class _Side:
    def __init__(self, ins, out_shape, sems, start, finish, middle=None):
        self.ins, self.out_shape, self.sems = ins, out_shape, sems
        self.start, self.middle, self.finish = start, middle, finish


_ACTIVE = [None]


def _pallas(body, *, name, grid, in_specs, out_specs, out_shape, scratch_shapes=(), sem, args):
    sched = _ACTIVE[0]
    side = sched.side_for(name) if sched is not None else None
    if side is None:
        return pl.pallas_call(
            body, name=name, grid=grid, in_specs=list(in_specs), out_specs=list(out_specs),
            out_shape=list(out_shape), scratch_shapes=list(scratch_shapes), compiler_params=_params(*sem))(*args)
    cuts = [len(in_specs), len(side.ins), len(out_shape), len(side.out_shape), len(scratch_shapes)]

    def with_side(*refs):
        groups, at = [], 0
        for n in cuts:
            groups.append(refs[at:at + n])
            at += n
        ins, side_ins, outs, side_outs, scratch = groups
        side_sems = refs[at:]
        step, total = pl.program_id(0), grid[0]
        for a in range(1, len(grid)):
            step, total = step * grid[a] + pl.program_id(a), total * grid[a]
        has_middle = side.middle is not None and total >= 3

        @pl.when(step == 0)
        def _():
            side.start(side_ins, side_outs, side_sems)

        if has_middle:
            @pl.when(step == total // 2)
            def _():
                side.middle(side_ins, side_outs, side_sems)

        body(*ins, *outs, *scratch)

        @pl.when(step == total - 1)
        def _():
            if side.middle is not None and not has_middle:
                side.middle(side_ins, side_outs, side_sems)
            side.finish(side_ins, side_outs, side_sems)

    hbm = pl.BlockSpec(memory_space=pl.ANY)
    res = pl.pallas_call(
        with_side, name=name, grid=grid, in_specs=list(in_specs) + [hbm] * len(side.ins),
        out_specs=list(out_specs) + [hbm] * len(side.out_shape), out_shape=list(out_shape) + list(side.out_shape),
        scratch_shapes=list(scratch_shapes) + list(side.sems),
        compiler_params=_params(*(["arbitrary"] * len(grid))))(*args, *side.ins)
    sched.done(name, res[len(out_shape):])
    return res[:len(out_shape)]


def _rms_fwd(x, g, name):
    t, d = x.shape
    tm = _row_tile(t)

    def body(x_ref, g_ref, h_ref):
        xv = x_ref[...]
        r = lax.rsqrt(jnp.mean(xv * xv, axis=-1, keepdims=True) + RMS_EPS)
        h_ref[...] = (xv * r * g_ref[...]).astype(BF16)

    return _pallas(
        body, name=name, grid=(t // tm,),
        in_specs=[pl.BlockSpec((tm, d), lambda i: (i, 0)), pl.BlockSpec((1, d), lambda i: (0, 0))],
        out_specs=[pl.BlockSpec((tm, d), lambda i: (i, 0))], out_shape=[jax.ShapeDtypeStruct((t, d), BF16)],
        sem=("parallel",), args=(x, g))[0]


def _accumulate(ref, part, step):
    @pl.when(step == 0)
    def _():
        ref[...] = part

    @pl.when(step > 0)
    def _():
        ref[...] += part


def _mm(a, b, *, ta=False, tb=False, tm, tn, out_dtype=F32, add=None, scale=1.0, norm_g=None, norm_bwd=None, name):
    m, k = (a.shape[1], a.shape[0]) if ta else a.shape
    n = b.shape[0] if tb else b.shape[1]
    tm, tn = min(tm, m), min(tn, n)
    assert m % tm == 0 and n % tn == 0, (m, n, tm, tn)
    assert (norm_g is None and norm_bwd is None) or tn == n
    dims = (((0 if ta else 1,), (1 if tb else 0,)), ((), ()))
    n_in = 2 + (add is not None) + (norm_g is not None) + (3 if norm_bwd is not None else 0)

    def body(*refs):
        ins, outs = list(refs[2:n_in]), refs[n_in:]
        r = lax.dot_general(refs[0][...].astype(BF16), refs[1][...].astype(BF16), dims, preferred_element_type=F32)
        if scale != 1.0:
            r = r * scale
        if add is not None:
            r = r + ins.pop(0)[...]
        if norm_bwd is not None:
            xv, gv, dres = (ref[...] for ref in ins)
            rs = lax.rsqrt(jnp.mean(xv * xv, axis=-1, keepdims=True) + RMS_EPS)
            xhat = xv * rs
            gd = r * gv
            dx = dres + rs * (gd - xhat * jnp.mean(gd * xhat, axis=-1, keepdims=True))
            outs[0][...] = dx
            outs[1][...] = dx.astype(BF16)
            _accumulate(outs[2], jnp.sum(r * xhat, axis=0, keepdims=True), pl.program_id(0))
            return
        outs[0][...] = r.astype(out_dtype)
        if norm_g is not None:
            rs = lax.rsqrt(jnp.mean(r * r, axis=-1, keepdims=True) + RMS_EPS)
            outs[1][...] = (r * rs * ins.pop(0)[...]).astype(BF16)

    a_spec = pl.BlockSpec((k, tm), lambda i, j: (0, i)) if ta else pl.BlockSpec((tm, k), lambda i, j: (i, 0))
    b_spec = pl.BlockSpec((tn, k), lambda i, j: (j, 0)) if tb else pl.BlockSpec((k, tn), lambda i, j: (0, j))
    o_spec = pl.BlockSpec((tm, tn), lambda i, j: (i, j))
    vec = pl.BlockSpec((1, tn), lambda i, j: (0, j))
    args, specs = [a, b], [a_spec, b_spec]
    if add is not None:
        args.append(add)
        specs.append(o_spec)
    out_specs, out_shape = [o_spec], [jax.ShapeDtypeStruct((m, n), out_dtype)]
    if norm_g is not None:
        args.append(norm_g)
        specs.append(vec)
        out_specs.append(o_spec)
        out_shape.append(jax.ShapeDtypeStruct((m, n), BF16))
    if norm_bwd is not None:
        args += list(norm_bwd)
        specs += [o_spec, vec, o_spec]
        out_specs = [o_spec, o_spec, vec]
        out_shape = [jax.ShapeDtypeStruct((m, n), F32), jax.ShapeDtypeStruct((m, n), BF16),
                     jax.ShapeDtypeStruct((1, n), F32)]
    res = _pallas(body, name=name, grid=(m // tm, n // tn), in_specs=specs, out_specs=out_specs, out_shape=out_shape,
                  sem=("arbitrary", "arbitrary") if norm_bwd is not None else ("parallel", "parallel"), args=args)
    return res[0] if len(res) == 1 else res


def _ffn_tile(f):
    for tf in (1408, 512, 256, 128):
        if f % tf == 0:
            return tf
    raise ValueError(f)


def _ffn_fwd(h, x, wg, wu, wd, name, next_g=None, tgt=None):
    t, d = x.shape
    f = wg.shape[0]
    tm, tf = _row_tile(t), _ffn_tile(f)
    nf = f // tf
    assert (next_g is None) != (tgt is None)

    def body(h_ref, x_ref, wg_ref, wu_ref, wd_ref, tail_ref, g_ref, u_ref, o0_ref, o1_ref, *rest):
        acc_ref = rest[-1]
        j = pl.program_id(1)
        hv = h_ref[...]
        gv = lax.dot_general(hv, wg_ref[...], NT, preferred_element_type=F32)
        uv = lax.dot_general(hv, wu_ref[...], NT, preferred_element_type=F32)
        av = gv * _sigmoid(gv) * uv
        g_ref[...] = gv.astype(BF16)
        u_ref[...] = uv.astype(BF16)
        _accumulate(acc_ref, lax.dot_general(av.astype(BF16), wd_ref[...], NN, preferred_element_type=F32), j)

        @pl.when(j == nf - 1)
        def _():
            y = x_ref[...] + 0.5 * acc_ref[...]
            if tgt is None:
                o0_ref[...] = y
                rs = lax.rsqrt(jnp.mean(y * y, axis=-1, keepdims=True) + RMS_EPS)
                o1_ref[...] = (y * rs * tail_ref[...]).astype(BF16)
            else:
                e = y - tail_ref[...]
                dy = e * (1.0 / d)
                o0_ref[...] = dy
                o1_ref[...] = dy.astype(BF16)
                _accumulate(rest[0], jnp.sum(e * e, axis=0, keepdims=True), pl.program_id(0))

    row = pl.BlockSpec((tm, d), lambda i, j: (i, 0))
    hid = pl.BlockSpec((tm, tf), lambda i, j: (i, j))
    vec = pl.BlockSpec((1, d), lambda i, j: (0, 0))
    out_specs = [hid, hid, row, row] + ([vec] if tgt is not None else [])
    out_shape = [jax.ShapeDtypeStruct((t, f), BF16)] * 2 + [jax.ShapeDtypeStruct((t, d), F32),
                                                            jax.ShapeDtypeStruct((t, d), BF16)]
    if tgt is not None:
        out_shape.append(jax.ShapeDtypeStruct((1, d), F32))
    return _pallas(
        body, name=name, grid=(t // tm, nf),
        in_specs=[row, row] + [pl.BlockSpec((tf, d), lambda i, j: (j, 0))] * 3 + [vec if tgt is None else row],
        out_specs=out_specs, out_shape=out_shape, scratch_shapes=[pltpu.VMEM((tm, d), F32)],
        sem=("parallel" if tgt is None else "arbitrary", "arbitrary"),
        args=(h, x, wg, wu, wd, next_g if tgt is None else tgt))


def _ffn_up(h, wg, wu, name):
    t, d = h.shape
    f = wg.shape[0]
    tm, tf = _row_tile(t), _ffn_tile(f)

    def body(h_ref, wg_ref, wu_ref, g_ref, u_ref, a_ref):
        hv = h_ref[...]
        gv = lax.dot_general(hv, wg_ref[...], NT, preferred_element_type=F32)
        uv = lax.dot_general(hv, wu_ref[...], NT, preferred_element_type=F32)
        g_ref[...] = gv.astype(BF16)
        u_ref[...] = uv.astype(BF16)
        a_ref[...] = (gv * _sigmoid(gv) * uv).astype(BF16)

    hid = pl.BlockSpec((tm, tf), lambda i, j: (i, j))
    wrow = pl.BlockSpec((tf, d), lambda i, j: (j, 0))
    return _pallas(
        body, name=name, grid=(t // tm, f // tf), in_specs=[pl.BlockSpec((tm, d), lambda i, j: (i, 0)), wrow, wrow],
        out_specs=[hid, hid, hid], out_shape=[jax.ShapeDtypeStruct((t, f), BF16)] * 3,
        sem=("parallel", "parallel"), args=(h, wg, wu))


def _ffn_bwd_mid(dy, wd, g, u, name):
    t, d = dy.shape
    f = wd.shape[0]
    tm, tf = _row_tile(t), _ffn_tile(f)

    def body(dy_ref, wd_ref, g_ref, u_ref, dg_ref, du_ref, dwd_ref):
        dy16 = dy_ref[...]
        da = 0.5 * lax.dot_general(dy16, wd_ref[...], NT, preferred_element_type=F32)
        gv = g_ref[...].astype(F32)
        uv = u_ref[...].astype(F32)
        s = _sigmoid(gv)
        silu = gv * s
        dg_ref[...] = (da * uv * (s * (1.0 + gv * (1.0 - s)))).astype(BF16)
        du_ref[...] = (da * silu).astype(BF16)
        part = 0.5 * lax.dot_general((silu * uv).astype(BF16), dy16, (((0,), (0,)), ((), ())),
                                     preferred_element_type=F32)
        _accumulate(dwd_ref, part, pl.program_id(1))

    hid = pl.BlockSpec((tm, tf), lambda j, i: (i, j))
    wrow = pl.BlockSpec((tf, d), lambda j, i: (j, 0))
    return _pallas(
        body, name=name, grid=(f // tf, t // tm),
        in_specs=[pl.BlockSpec((tm, d), lambda j, i: (i, 0)), wrow, hid, hid],
        out_specs=[hid, hid, wrow],
        out_shape=[jax.ShapeDtypeStruct((t, f), BF16)] * 2 + [jax.ShapeDtypeStruct((f, d), F32)],
        sem=("parallel", "arbitrary"), args=(dy, wd, g, u))


def _rel_index(t, s_band):
    return jnp.clip(t + KPAD - s_band, -REL_CLIP, REL_CLIP) + REL_CLIP


def _bias_expand(rel_bias_pad):
    nh = rel_bias_pad.shape[0]

    def body(rb_ref, out_ref):
        rb = rb_ref[...]
        i_io = lax.broadcasted_iota(jnp.int32, (N_REL_PAD, BAND), 0)
        s_io = lax.broadcasted_iota(jnp.int32, (N_REL_PAD, BAND), 1)

        def row(r, carry):
            onehot = (i_io == _rel_index(pl.program_id(0) * rows + r, s_io)).astype(F32)
            out_ref[r] = _dot_exact01_r(rb, onehot)
            return carry

        lax.fori_loop(0, rows, row, 0)

    rows = 8
    return _pallas(
        body, name="bias_expand", grid=(CHUNK // rows,),
        in_specs=[pl.BlockSpec(rel_bias_pad.shape, lambda i: (0, 0))],
        out_specs=[pl.BlockSpec((rows, nh, BAND), lambda i: (i, 0, 0))],
        out_shape=[jax.ShapeDtypeStruct((CHUNK, nh, BAND), F32)], sem=("arbitrary",), args=(rel_bias_pad,))[0]


def _bias_fold(dbias):
    ng, nh = dbias.shape[0], dbias.shape[2]

    def body(db_ref, out_ref):
        s_io = lax.broadcasted_iota(jnp.int32, (BAND, N_REL_PAD), 0)
        i_io = lax.broadcasted_iota(jnp.int32, (BAND, N_REL_PAD), 1)

        def row(t, acc):
            onehot = (i_io == _rel_index(t, s_io)).astype(F32)
            d = db_ref[0, t]
            for gi in range(1, ng):
                d = d + db_ref[gi, t]
            return acc + _dot_exact01_r(d, onehot)

        out_ref[...] = lax.fori_loop(0, CHUNK, row, jnp.zeros((nh, N_REL_PAD), F32))

    return _pallas(
        body, name="bias_fold", grid=(1,), in_specs=[pl.BlockSpec(dbias.shape, lambda i: (0, 0, 0, 0))],
        out_specs=[pl.BlockSpec((nh, N_REL_PAD), lambda i: (0, 0))],
        out_shape=[jax.ShapeDtypeStruct((nh, N_REL_PAD), F32)], sem=("arbitrary",), args=(dbias,))[0]


def _left_half(shape):
    return lax.broadcasted_iota(jnp.int32, shape, len(shape) - 1) < ATTN_HEAD_DIM


def _stack_heads(v):
    left = _left_half(v.shape)
    zero = jnp.zeros_like(v)
    return jnp.concatenate([jnp.where(left, v, zero), jnp.where(left, zero, v)], axis=0)


def _unstack_heads(v):
    return jnp.where(_left_half((CHUNK, 128)), v[0:CHUNK, :], v[CHUNK:2 * CHUNK, :])


def _half_mean(v):
    r = lax.broadcasted_iota(jnp.int32, (128, 128), 0) < ATTN_HEAD_DIM
    c = lax.broadcasted_iota(jnp.int32, (128, 128), 1) < ATTN_HEAD_DIM
    return _dot_exact01_r(v, r == c) * (1.0 / ATTN_HEAD_DIM)


def _attn_prepare(q_ref, k_ref, v_ref, gq_ref, gk_ref, qs_scr, k_scr, v_scr):
    q, k = q_ref[...], k_ref[...]
    rq = lax.rsqrt(_half_mean(q * q) + RMS_EPS)
    rk = lax.rsqrt(_half_mean(k * k) + RMS_EPS)
    qhat, khat = q * rq, k * rk
    qs_scr[...] = (qhat * gq_ref[...] * ATTN_HEAD_DIM ** -0.5).astype(BF16)
    k_scr[0:KPAD, :] = jnp.zeros((KPAD, 128), BF16)
    v_scr[0:KPAD, :] = jnp.zeros((KPAD, 128), BF16)
    k_scr[KPAD:, :] = (khat * gk_ref[...]).astype(BF16)
    v_scr[KPAD:, :] = v_ref[...].astype(BF16)
    return qhat, rq, khat, rk


def _first_key(c):
    return jnp.maximum(CHUNK, (LEFT_CHUNKS + 1 - c) * CHUNK)


def _attn_fwd_chunk(c, qs_scr, k_scr, v_scr, bias_ref, o_ref):
    r0 = pl.multiple_of(c * CHUNK, CHUNK)
    s = lax.dot_general(_stack_heads(qs_scr[pl.ds(r0, CHUNK), :]), k_scr[pl.ds(r0, BAND), :], NT,
                        preferred_element_type=F32)
    yield
    col = lax.broadcasted_iota(jnp.int32, (2 * CHUNK, BAND), 1)
    s = jnp.where(col >= _first_key(c), s + bias_ref[...], -jnp.inf)
    m = jnp.max(s, axis=-1, keepdims=True)
    yield
    e = jnp.exp(s - m)
    yield
    inv = 1.0 / jnp.sum(e, axis=-1, keepdims=True)
    o = lax.dot_general(e.astype(BF16), v_scr[pl.ds(r0, BAND), :], NN, preferred_element_type=F32)
    yield
    o_ref[pl.ds(r0, CHUNK), :] = _unstack_heads(o * inv)


def _attn_bwd(proj, out, dout, bias, gq, gk, nb, seq):
    nc = seq // CHUNK
    lock = min(ATTN_LOCKSTEP, nc)
    assert nc % lock == 0
    scale = ATTN_HEAD_DIM ** -0.5

    def body(q_ref, k_ref, v_ref, o_ref, do_ref, bias_ref, gq_ref, gk_ref,
             dq_ref, dk_ref, dv_ref, dbias_ref, dgq_ref, dgk_ref,
             qs_scr, k_scr, v_scr, dqn_scr, dk_scr, dv_scr, db_scr):
        qhat, rq, khat, rk = _attn_prepare(q_ref, k_ref, v_ref, gq_ref, gk_ref, qs_scr, k_scr, v_scr)
        dk_scr[...] = jnp.zeros_like(dk_scr)
        dv_scr[...] = jnp.zeros_like(dv_scr)
        db_scr[...] = jnp.zeros_like(db_scr)

        def one_chunk(c):
            r0 = pl.multiple_of(c * CHUNK, CHUNK)
            qst = _stack_heads(qs_scr[pl.ds(r0, CHUNK), :])
            kb = k_scr[pl.ds(r0, BAND), :]
            vb = v_scr[pl.ds(r0, BAND), :]
            st = lax.dot_general(kb, qst, NT, preferred_element_type=F32) + bias_ref[...]
            dost = _stack_heads(do_ref[pl.ds(r0, CHUNK), :])
            dost16 = dost.astype(BF16)
            dpt = lax.dot_general(vb, dost16, NT, preferred_element_type=F32)
            yield
            key = lax.broadcasted_iota(jnp.int32, (BAND, 2 * CHUNK), 0)
            st = jnp.where(key >= _first_key(c), st, -jnp.inf)
            mx = jnp.max(st, axis=0, keepdims=True)
            drow = _row_sums_on_lanes(dost * _stack_heads(o_ref[pl.ds(r0, CHUNK), :]))
            yield
            et = jnp.exp(st - mx)
            yield
            pt = et * (1.0 / jnp.sum(et, axis=0, keepdims=True))
            yield
            dst = pt * (dpt - drow)
            dst16 = dst.astype(BF16)
            yield
            db_scr[...] += dst
            dqn_scr[pl.ds(r0, CHUNK), :] = scale * _unstack_heads(_dot(dst.T, kb))
            yield
            dk_scr[pl.ds(r0, BAND), :] += lax.dot_general(dst16, qst, NN, preferred_element_type=F32)
            yield
            dv_scr[pl.ds(r0, BAND), :] += lax.dot_general(pt.astype(BF16), dost16, NN, preferred_element_type=F32)

        def chunk(i, carry):
            _lockstep([one_chunk(i * lock + a) for a in range(lock)])
            return carry

        lax.fori_loop(0, nc // lock, chunk, 0, unroll=max(1, min(ATTN_UNROLL, nc) // lock))

        def norm_bwd(dn, hat, r, g_ref):
            gd = dn * g_ref[...]
            return r * (gd - hat * _half_mean(gd * hat)), jnp.sum(dn * hat, axis=0, keepdims=True)

        dq, dgq = norm_bwd(dqn_scr[...], qhat, rq, gq_ref)
        dk, dgk = norm_bwd(dk_scr[KPAD:, :], khat, rk, gk_ref)
        dq_ref[...] = dq.astype(BF16)
        dk_ref[...] = dk.astype(BF16)
        dv_ref[...] = dv_scr[KPAD:, :].astype(BF16)
        dbias_ref[0] = db_scr[...]
        dgq_ref[0] = dgq
        dgk_ref[0] = dgk

    def col(off):
        return pl.BlockSpec((seq, 128), lambda b, hp: (b, off + hp))

    vec = pl.BlockSpec((1, 128), lambda b, hp: (0, 0))
    gvec = pl.BlockSpec((1, 1, 128), lambda b, hp: (b * (ATTN_HEADS // 2) + hp, 0, 0))
    t = nb * seq
    return _pallas(
        body, name="attn_bwd", grid=(nb, ATTN_HEADS // 2),
        in_specs=[col(0), col(4), col(8), col(0), col(0),
                  pl.BlockSpec((BAND, 2 * CHUNK), lambda b, hp: (hp, 0)), vec, vec],
        out_specs=[col(0), col(0), col(0), pl.BlockSpec((1, BAND, 2 * CHUNK), lambda b, hp: (b, hp, 0)),
                   gvec, gvec],
        out_shape=[jax.ShapeDtypeStruct((t, ATTN_WIDTH), BF16)] * 3
        + [jax.ShapeDtypeStruct((nb, ATTN_HEADS // 2 * BAND, 2 * CHUNK), F32)]
        + [jax.ShapeDtypeStruct((nb * ATTN_HEADS // 2, 1, 128), F32)] * 2,
        scratch_shapes=[pltpu.VMEM((seq, 128), BF16), pltpu.VMEM((seq + KPAD, 128), BF16),
                        pltpu.VMEM((seq + KPAD, 128), BF16), pltpu.VMEM((seq, 128), F32),
                        pltpu.VMEM((seq + KPAD, 128), F32), pltpu.VMEM((seq + KPAD, 128), F32),
                        pltpu.VMEM((BAND, 2 * CHUNK), F32)],
        sem=("parallel", "parallel"), args=(proj, proj, proj, out, dout, bias, gq, gk))


def _tri(lower):
    r = lax.broadcasted_iota(jnp.int32, (CHUNK, CHUNK), 0)
    c = lax.broadcasted_iota(jnp.int32, (CHUNK, CHUNK), 1)
    return (r >= c) if lower else (r <= c)


def _hgrn_gates(hq, hf, lb):
    sq = _sigmoid(hq)
    sf = _sigmoid(hf)
    return hq * sq, sq, sf, lb + (1.0 - lb) * sf


def _hgrn_offdiag(q_s, k_s, b_s):
    row = lax.broadcasted_iota(jnp.int32, (CHUNK, HGRN_HEAD_DIM), 0)
    bv, qv, kv = b_s[...], q_s[...], k_s[...]
    eqs, eks = [], []
    for i in range(1, N_SUB):
        r = b_s[pl.ds(SUB * i - 1, 1), :]
        in_i = (row >= SUB * i) & (row < SUB * (i + 1))
        eqs.append(jnp.exp(jnp.where(in_i, bv - r, -jnp.inf)))
        eks.append(jnp.exp(jnp.where(row < SUB * i, r - bv, -jnp.inf)))
    eq = jnp.concatenate(eqs, axis=1)
    ek = jnp.concatenate(eks, axis=1)
    qt = jnp.concatenate([qv] * (N_SUB - 1), axis=1) * eq
    kt = jnp.concatenate([kv] * (N_SUB - 1), axis=1) * ek
    return qt, kt, eq, ek


def _hgrn_diag_e(b_s, i, s):
    t_io = lax.broadcasted_iota(jnp.int32, (SUB, HGRN_HEAD_DIM), 0)
    bi = b_s[pl.ds(SUB * i, SUB), :]
    return jnp.exp(jnp.where(t_io >= s, bi - b_s[pl.ds(SUB * i + s, 1), :], -jnp.inf)), t_io


def _hgrn_intra(q_s, k_s, b_s, a_s, qt, kt):
    ktp = jnp.concatenate([kt, jnp.zeros_like(kt)], axis=0)
    a_s[...] = _dot(qt, ktp, NT)
    yield
    col = lax.broadcasted_iota(jnp.int32, (SUB, HGRN_HEAD_DIM), 1)
    for i in range(N_SUB):
        qi = q_s[pl.ds(SUB * i, SUB), :]
        ai = jnp.zeros((SUB, HGRN_HEAD_DIM), F32)
        for s in range(SUB):
            e, _ = _hgrn_diag_e(b_s, i, s)
            a_col = jnp.sum(qi * k_s[pl.ds(SUB * i + s, 1), :] * e, axis=-1, keepdims=True)
            ai = ai + jnp.where(col == SUB * i + s, a_col, 0.0)
            if s % DIAG_STAGE == DIAG_STAGE - 1:
                yield
        a_s[pl.ds(SUB * i, SUB), :] += ai


def _mixer_fwd(proj, bias, gq, gk, lb, go, nb, seq):
    nc = seq // CHUNK
    hd = HGRN_HEAD_DIM
    nblk = ATTN_HEADS // 2
    rows_blk = seq // nblk
    nck = rows_blk // CHUNK
    per = nc // nck
    assert rows_blk % CHUNK == 0

    def body(aq_ref, ak_ref, av_ref, bias_ref, gq_ref, gk_ref, hq_ref, hf_ref, hi_ref, hg_ref, lb_ref, go_ref,
             ao_ref, y_ref, o_ref, st_ref, a_ref, qs_scr, k_scr, v_scr, st_all, q_all, k_all, b_all, a_all):
        _attn_prepare(aq_ref, ak_ref, av_ref, gq_ref, gk_ref, qs_scr, k_scr, v_scr)

        @pl.when(pl.program_id(1) == 0)
        def _():
            st_all[...] = jnp.zeros_like(st_all)

        lower = _tri(True)

        def head_chunk(hh, c, rows):
            ln = slice(hd * hh, hd * (hh + 1))
            st, q_s, k_s, b_s, a_s = st_all.at[hh], q_all.at[hh], k_all.at[hh], b_all.at[hh], a_all.at[hh]
            q, _, _, f = _hgrn_gates(hq_ref[rows, ln], hf_ref[rows, ln], lb_ref[:, ln])
            v = hi_ref[rows, ln]
            yield
            b = _dot_exact01(lower, jnp.log(f))
            q_s[...] = q
            k_s[...] = 1.0 - f
            b_s[...] = b
            st_ref[hh, c] = st[...]
            yield
            qt, kt, _, _ = _hgrn_offdiag(q_s, k_s, b_s)
            yield
            yield from _hgrn_intra(q_s, k_s, b_s, a_s, qt, kt)
            a16 = a_s[...].astype(BF16)
            a_ref[hh, c] = a16
            vp = jnp.concatenate([v, jnp.zeros_like(v)], axis=0)
            o = _dot(a16, vp) + _dot(q * jnp.exp(b), st[...], NT)
            yield
            bl = b_s[pl.ds(CHUNK - 1, 1), :]
            st[...] = st[...] * jnp.exp(bl) + _tn(v, (1.0 - f) * jnp.exp(bl - b))
            o_ref[rows, ln] = o
            yield
            n = o * lax.rsqrt(jnp.mean(o * o, axis=-1, keepdims=True) + RMS_EPS) * go_ref[...]
            hg = hg_ref[rows, ln]
            y_ref[rows, ln] = n * hg * _sigmoid(hg)

        def chunk(c, carry):
            rows = pl.ds(pl.multiple_of(c * CHUNK, CHUNK), CHUNK)
            _lockstep([_attn_fwd_chunk(c * per + a, qs_scr, k_scr, v_scr, bias_ref, ao_ref) for a in range(per)]
                      + [head_chunk(hh, c, rows) for hh in range(HGRN_HEADS)])
            return carry

        lax.fori_loop(0, nck, chunk, 0)

    hp, wide = HGRN_HEADS, HGRN_HEADS * hd

    def acol(off):
        return pl.BlockSpec((seq, 128), lambda b, s: (b, off + s))

    def col(off):
        return pl.BlockSpec((rows_blk, wide), lambda b, s: (b * nblk + s, off // hp))

    out = pl.BlockSpec((rows_blk, wide), lambda b, s: (b * nblk + s, 0))
    vec = pl.BlockSpec((1, 128), lambda b, s: (0, 0))
    t = nb * seq
    return _pallas(
        body, name="mixer_fwd", grid=(nb, nblk),
        in_specs=[acol(0), acol(4), acol(8), pl.BlockSpec((2 * CHUNK, BAND), lambda b, s: (s, 0)), vec, vec,
                  col(12), col(16), col(20), col(24), pl.BlockSpec((1, wide), lambda b, s: (0, 0)), vec],
        out_specs=[pl.BlockSpec((seq, 128), lambda b, s: (b, s)), out, out,
                   pl.BlockSpec((hp, nck, hd, hd), lambda b, s: (b, s, 0, 0)),
                   pl.BlockSpec((hp, nck, CHUNK, hd), lambda b, s: (b, s, 0, 0))],
        out_shape=[jax.ShapeDtypeStruct((t, ATTN_WIDTH), F32)] + [jax.ShapeDtypeStruct((t, wide), F32)] * 2
        + [jax.ShapeDtypeStruct((nb * hp, nc, hd, hd), F32), jax.ShapeDtypeStruct((nb * hp, nc, CHUNK, hd), BF16)],
        scratch_shapes=[pltpu.VMEM((seq, 128), BF16), pltpu.VMEM((seq + KPAD, 128), BF16),
                        pltpu.VMEM((seq + KPAD, 128), BF16), pltpu.VMEM((hp, hd, hd), F32)]
        + [pltpu.VMEM((hp, CHUNK, hd), F32)] * 4,
        sem=("parallel", "arbitrary"), args=(proj,) * 3 + (bias, gq, gk) + (proj,) * 4 + (lb, go))


def _hgrn_bwd(proj, lb, go, o_pre, states, scores, dout, nb, seq):
    nc = seq // CHUNK
    hd = HGRN_HEAD_DIM
    rows_blk = min(HGRN_ROWS, seq)
    nblk, nck = seq // rows_blk, rows_blk // CHUNK

    def body(hq_ref, hf_ref, hi_ref, hg_ref, lb_ref, go_ref, o_ref, st_ref, a_ref, dy_ref,
             dhq_ref, dhf_ref, dhi_ref, dhg_ref, dlb_ref, dgo_ref,
             dst_all, q_all, k_all, b_all, da_all, dqi_all, dki_all, dlb_all, dgo_all):
        @pl.when(pl.program_id(1) == 0)
        def _():
            dst_all[...] = jnp.zeros_like(dst_all)
            dlb_all[...] = jnp.zeros_like(dlb_all)
            dgo_all[...] = jnp.zeros_like(dgo_all)

        lower, upper = _tri(True), _tri(False)
        gov = go_ref[...]
        row = lax.broadcasted_iota(jnp.int32, (CHUNK, hd), 0)

        def head_chunk(hh, c, rows):
            ln = slice(hd * hh, hd * (hh + 1))
            dst, q_s, k_s, b_s = dst_all.at[hh], q_all.at[hh], k_all.at[hh], b_all.at[hh]
            da_s, dqi_s, dki_s = da_all.at[hh], dqi_all.at[hh], dki_all.at[hh]
            dlb_acc, dgo_acc = dlb_all.at[hh], dgo_all.at[hh]
            lbv = lb_ref[:, ln]
            hq, hf, v, hg = hq_ref[rows, ln], hf_ref[rows, ln], hi_ref[rows, ln], hg_ref[rows, ln]
            q, sq, sf, f = _hgrn_gates(hq, hf, lbv)
            kk = 1.0 - f
            yield
            b = _dot_exact01(lower, jnp.log(f))
            q_s[...] = q
            k_s[...] = kk
            b_s[...] = b
            yield
            bl = b_s[pl.ds(CHUNK - 1, 1), :]
            ebl = jnp.exp(bl)
            ekd = jnp.exp(bl - b)
            kd = kk * ekd
            eb = jnp.exp(b)
            qb = q * eb
            st0 = st_ref[hh, c]
            dst1 = dst[...]
            yield

            o = o_ref[rows, ln]
            dy = dy_ref[rows, ln]
            sg = _sigmoid(hg)
            rstd = lax.rsqrt(jnp.mean(o * o, axis=-1, keepdims=True) + RMS_EPS)
            ohat = o * rstd
            dn = dy * hg * sg
            dhg_ref[rows, ln] = (dy * ohat * gov * (sg * (1.0 + hg * (1.0 - sg)))).astype(BF16)
            dgo_acc[...] += jnp.sum(dn * ohat, axis=0, keepdims=True)
            gdn = dn * gov
            do = rstd * (gdn - ohat * jnp.mean(gdn * ohat, axis=-1, keepdims=True))
            yield

            qt, kt, eq, ek = _hgrn_offdiag(q_s, k_s, b_s)
            da = _dot(do, v, NT)
            dat = _dot(v, do, NT)
            da_s[...] = da
            yield
            dqo = _dot(da, kt) * eq
            dko = _dot(dat, qt) * ek
            dqi_s[...] = sum(dqo[:, j * hd:(j + 1) * hd] for j in range(N_SUB - 1))
            dki_s[...] = sum(dko[:, j * hd:(j + 1) * hd] for j in range(N_SUB - 1))
            yield
            col = lax.broadcasted_iota(jnp.int32, (SUB, CHUNK), 1)
            for i in range(N_SUB):
                qi = q_s[pl.ds(SUB * i, SUB), :]
                dai = da_s[pl.ds(SUB * i, SUB), :]
                dqd = jnp.zeros((SUB, hd), F32)
                for s in range(SUB):
                    e, _ = _hgrn_diag_e(b_s, i, s)
                    dacol = jnp.sum(jnp.where(col == SUB * i + s, dai, 0.0), axis=-1, keepdims=True)
                    w = dacol * e
                    dqd = dqd + w * k_s[pl.ds(SUB * i + s, 1), :]
                    dki_s[pl.ds(SUB * i + s, 1), :] += jnp.sum(w * qi, axis=0, keepdims=True)
                    if s % DIAG_STAGE == DIAG_STAGE - 1:
                        yield
                dqi_s[pl.ds(SUB * i, SUB), :] += dqd
            dqi, dki = dqi_s[...], dki_s[...]

            dv = _tn(a_ref[hh, c].astype(F32), do)[0:CHUNK, :] + _dot(kd, dst1, NT)
            dqb = _dot(do, st0)
            dkd = _dot(v, dst1)
            yield
            t2 = dkd * kd
            dq = dqb * eb + dqi
            dk = dkd * ekd + dki
            dbl = jnp.sum(t2, axis=0, keepdims=True) + ebl * jnp.sum(st0 * dst1, axis=0, keepdims=True)
            db = dqb * qb - t2 + q * dqi - kk * dki + jnp.where(row == CHUNK - 1, dbl, 0.0)
            yield
            dg = _dot_exact01(upper, db)
            dst[...] = dst1 * ebl + _tn(do, qb)
            yield

            df = dg / f - dk
            dhf_ref[rows, ln] = (df * (1.0 - lbv) * sf * (1.0 - sf)).astype(BF16)
            dlb_acc[...] += jnp.sum(df * (1.0 - sf), axis=0, keepdims=True)
            dhq_ref[rows, ln] = (dq * (sq * (1.0 + hq * (1.0 - sq)))).astype(BF16)
            dhi_ref[rows, ln] = dv.astype(BF16)

        def chunk(it, carry):
            c = nck - 1 - it
            rows = pl.ds(pl.multiple_of(c * CHUNK, CHUNK), CHUNK)
            _lockstep([head_chunk(hh, c, rows) for hh in range(HGRN_HEADS)])
            return carry

        lax.fori_loop(0, nck, chunk, 0)

        @pl.when(pl.program_id(1) == nblk - 1)
        def _():
            dlb_ref[...] = dlb_all[...]
            dgo_ref[...] = dgo_all[...]

    hp, wide = HGRN_HEADS, HGRN_HEADS * hd

    def col(off):
        return pl.BlockSpec((rows_blk, wide), lambda b, s: (b * nblk + nblk - 1 - s, off // hp))

    out = pl.BlockSpec((rows_blk, wide), lambda b, s: (b * nblk + nblk - 1 - s, 0))
    part = pl.BlockSpec((hp, 1, hd), lambda b, s: (b, 0, 0))
    t = nb * seq
    return pl.pallas_call(
        body, name="hgrn_bwd", grid=(nb, nblk),
        in_specs=[col(12), col(16), col(20), col(24), pl.BlockSpec((1, wide), lambda b, s: (0, 0)),
                  pl.BlockSpec((1, hd), lambda b, s: (0, 0)), out,
                  pl.BlockSpec((hp, nck, hd, hd), lambda b, s: (b, nblk - 1 - s, 0, 0)),
                  pl.BlockSpec((hp, nck, CHUNK, hd), lambda b, s: (b, nblk - 1 - s, 0, 0)), col(4)],
        out_specs=[out, out, out, out, part, part],
        out_shape=[jax.ShapeDtypeStruct((t, wide), BF16)] * 4 + [jax.ShapeDtypeStruct((nb * hp, 1, hd), F32)] * 2,
        scratch_shapes=[pltpu.VMEM((hp, hd, hd), F32)] + [pltpu.VMEM((hp, CHUNK, hd), F32)] * 3
        + [pltpu.VMEM((hp, CHUNK, CHUNK), F32)] + [pltpu.VMEM((hp, CHUNK, hd), F32)] * 2
        + [pltpu.VMEM((hp, 1, hd), F32)] * 2,
        compiler_params=_params("parallel", "arbitrary"),
    )(proj, proj, proj, proj, lb, go, o_pre, states, scores, dout)


def _lb_fwd(lower_bounds):
    def body(x_ref, o_ref):
        xv = x_ref[...]
        e = jnp.exp(xv - jnp.max(xv, axis=0, keepdims=True))
        o_ref[...] = e[0:1, :] / jnp.sum(e, axis=0, keepdims=True)

    return pl.pallas_call(body, name="lb_fwd",
                          out_shape=jax.ShapeDtypeStruct((1, lower_bounds.shape[1]), F32))(lower_bounds)


def _lb_bwd(lower_bounds, dlb_parts):
    ng = dlb_parts.shape[0]

    def body(x_ref, d_ref, o_ref):
        xv = x_ref[...]
        e = jnp.exp(xv - jnp.max(xv, axis=0, keepdims=True))
        p = e / jnp.sum(e, axis=0, keepdims=True)
        dlb = d_ref[0]
        for gi in range(1, ng):
            dlb = dlb + d_ref[gi]
        first = lax.broadcasted_iota(jnp.int32, xv.shape, 0) == 0
        o_ref[...] = p * (jnp.where(first, dlb, 0.0) - p[0:1, :] * dlb)

    return pl.pallas_call(body, name="lb_bwd",
                          out_shape=jax.ShapeDtypeStruct(lower_bounds.shape, F32))(lower_bounds, dlb_parts)


def _ffn_bwd(x, g, h, gate, up, dy, dy16, w, put, tag):
    wg, wu, wd = w[tag + "_w_gate"], w[tag + "_w_up"], w[tag + "_w_down"]
    dgate, dup, dwd = _ffn_bwd_mid(dy16, wd, gate, up, tag + "_bwd_mid")
    put(tag + "_w_down", dwd)
    put(tag + "_w_gate", _mm(dgate, h, ta=True, tm=1408, tn=512, name=tag + "_dwg"))
    put(tag + "_w_up", _mm(dup, h, ta=True, tm=1408, tn=512, name=tag + "_dwu"))
    dh = _mm(dgate, wg, tm=512, tn=1024, name=tag + "_dh_gate")
    return _mm(dup, wu, tm=512, tn=1024, add=dh, norm_bwd=(x, g, dy), name=tag + "_dh_up")


def _local_step(x, tgt, sp, w, put, nb, seq):
    d = x.shape[1]
    h1 = _rms_fwd(x, sp["ffn1_norm_g"], "ffn1_norm")
    rb_pad = jnp.pad(sp["attn_rel_bias"], ((0, 0), (0, N_REL_PAD - N_REL)))
    bias = jnp.transpose(_bias_expand(rb_pad), (1, 0, 2)).reshape(ATTN_HEADS * CHUNK, BAND)
    gq2 = jnp.concatenate([sp["attn_q_norm_g"]] * 2, axis=1)
    gk2 = jnp.concatenate([sp["attn_k_norm_g"]] * 2, axis=1)
    lb = _lb_fwd(sp["hgrn_lower_bounds"])
    gate1, up1, act1 = _ffn_up(h1, w["ffn1_w_gate"], w["ffn1_w_up"], "ffn1_up")
    x1, h2 = _mm(act1, w["ffn1_w_down"], tm=512, tn=d, add=x, scale=0.5, norm_g=sp["mix_norm_g"],
                 name="ffn1_down")
    proj = _mm(h2, w["w_in"], tb=True, tm=256, tn=w["w_in"].shape[0], name="in_proj")
    attn, hy, ho, hstate, hscore = _mixer_fwd(proj, bias, gq2, gk2, lb, sp["hgrn_out_norm_g"], nb, seq)
    mix = jnp.concatenate([attn, hy], axis=1)
    x2, h3 = _mm(mix, w["w_out"], tm=512, tn=1024, add=x1, norm_g=sp["ffn2_norm_g"], name="out_proj")
    gate2, up2, dx3, dx3_16, sq = _ffn_fwd(h3, x2, w["ffn2_w_gate"], w["ffn2_w_up"], w["ffn2_w_down"], "ffn2_fwd",
                                           tgt=tgt)
    loss = 0.5 * jnp.sum(sq) / d

    dx2, dx2_16, dg3 = _ffn_bwd(x2, sp["ffn2_norm_g"], h3, gate2, up2, dx3, dx3_16, w, put, "ffn2")
    dmix = _mm(dx2_16, w["w_out"], tb=True, tm=512, tn=1024, name="out_proj_dx")
    put("w_out", _mm(mix, dx2_16, ta=True, tm=512, tn=1024, name="out_proj_dw"))
    bias_t = jnp.transpose(bias.reshape(ATTN_HEADS // 2, 2 * CHUNK, BAND), (0, 2, 1)).reshape(-1, 2 * CHUNK)
    dq, dk, dv, dbias, dgq, dgk = _attn_bwd(proj, attn, dmix, bias_t, gq2, gk2, nb, seq)
    dbias = jnp.transpose(dbias.reshape(nb, ATTN_HEADS // 2, BAND, 2, CHUNK), (0, 4, 1, 3, 2))
    dbias = dbias.reshape(nb, CHUNK, ATTN_HEADS, BAND)
    dgq = jnp.sum(dgq, axis=(0, 1)).reshape(2, ATTN_HEAD_DIM).sum(axis=0, keepdims=True)
    dgk = jnp.sum(dgk, axis=(0, 1)).reshape(2, ATTN_HEAD_DIM).sum(axis=0, keepdims=True)
    dhq, dhf, dhi, dhg, dlb, dgo = _hgrn_bwd(proj, lb, sp["hgrn_out_norm_g"], ho, hstate, hscore, dmix, nb, seq)
    dproj = jnp.concatenate([dq, dk, dv, dhq, dhf, dhi, dhg], axis=1)
    put("w_in", _mm(dproj, h2, ta=True, tm=512, tn=1024, name="in_proj_dw"))
    dx1, dx1_16, dgm = _mm(dproj, w["w_in"], tm=512, tn=1024, norm_bwd=(x1, sp["mix_norm_g"], dx2),
                           name="in_proj_dx")
    dx0, _, dg1 = _ffn_bwd(x, sp["ffn1_norm_g"], h1, gate1, up1, dx1, dx1_16, w, put, "ffn1")

    small = {
        "ffn1_norm_g": dg1, "mix_norm_g": dgm, "ffn2_norm_g": dg3,
        "attn_q_norm_g": dgq, "attn_k_norm_g": dgk,
        "attn_rel_bias": _bias_fold(dbias)[:, :N_REL],
        "hgrn_lower_bounds": _lb_bwd(sp["hgrn_lower_bounds"], dlb.reshape(nb, 1, HGRN_HEADS * HGRN_HEAD_DIM)),
        "hgrn_out_norm_g": jnp.sum(dgo, axis=(0, 1))[None, :],
    }
    return loss, dx0, small


MESH = pl.DeviceIdType.MESH
ANY = pl.BlockSpec(memory_space=pl.ANY)


def _coords():
    return lax.axis_index("x"), lax.axis_index("y"), lax.axis_index("c")


def _other_chips(x, y):
    return [(1 - x, y), (x, 1 - y), (1 - x, 1 - y)]


def _gather_side(shards):
    n = len(shards)

    def copies(ins, outs, sems):
        send_sems, recv_sems, local_sems = sems
        x, y, c = _coords()
        xn, yn, dg = (1 - x, y), (x, 1 - y), (1 - x, 1 - y)

        def copy(i, k, block, to, half=None, src=None):
            bx, by, bc = block
            dst = outs[i].at[4 * bx + 2 * by + bc]
            if half is not None:
                rows = shards[i].shape[0] // 2
                dst = dst.at[pl.ds(half * rows, rows)]
            return pltpu.make_async_remote_copy(
                src_ref=dst if src is None else src, dst_ref=dst, send_sem=send_sems.at[i, k],
                recv_sem=recv_sems.at[i, k], device_id=to, device_id_type=MESH)

        mine = [pltpu.make_async_copy(ins[i], outs[i].at[4 * x + 2 * y + c], local_sems.at[i]) for i in range(n)]
        return copy, mine, (x, y, c), (x, y, 1 - c), xn, yn, dg, c

    def own(copy, i, ins, me, sibling, xn, yn, c):
        return [copy(i, 0, me, sibling, src=ins[i]), copy(i, 1, me, (*xn, c), src=ins[i]),
                copy(i, 2, me, (*yn, c), src=ins[i])]

    def passed_on(copy, i, sibling, xn, yn, c):
        return [copy(i, 3, (*xn, c), sibling), copy(i, 5, (*xn, c), (*yn, c), half=0),
                copy(i, 4, (*yn, c), sibling), copy(i, 6, (*yn, c), (*xn, c), half=1)]

    def diagonal(copy, i, sibling, dg, c):
        return [copy(i, 7, (*dg, c), sibling, half=0), copy(i, 8, (*dg, c), sibling, half=1)]

    def start(ins, outs, sems):
        copy, mine, me, sibling, xn, yn, dg, c = copies(ins, outs, sems)
        for cp in mine + [cp for i in range(n) for cp in own(copy, i, ins, me, sibling, xn, yn, c)]:
            cp.start()

    def middle(ins, outs, sems):
        copy, mine, me, sibling, xn, yn, dg, c = copies(ins, outs, sems)
        for i in range(n):
            fwd_x, relay_x, fwd_y, relay_y = passed_on(copy, i, sibling, xn, yn, c)
            copy(i, 1, (*xn, c), me).wait_recv()
            fwd_x.start()
            relay_x.start()
            copy(i, 2, (*yn, c), me).wait_recv()
            fwd_y.start()
            relay_y.start()

    def finish(ins, outs, sems):
        copy, mine, me, sibling, xn, yn, dg, c = copies(ins, outs, sems)
        for i in range(n):
            top, bottom = diagonal(copy, i, sibling, dg, c)
            copy(i, 5, (*dg, c), me, half=0).wait_recv()
            top.start()
            copy(i, 6, (*dg, c), me, half=1).wait_recv()
            bottom.start()
        for i in range(n):
            copy(i, 0, sibling, me).wait_recv()
            copy(i, 3, (*xn, 1 - c), me).wait_recv()
            copy(i, 4, (*yn, 1 - c), me).wait_recv()
            copy(i, 7, (*dg, 1 - c), me, half=0).wait_recv()
            copy(i, 8, (*dg, 1 - c), me, half=1).wait_recv()
        for i in range(n):
            for cp in (own(copy, i, ins, me, sibling, xn, yn, c) + passed_on(copy, i, sibling, xn, yn, c)
                       + diagonal(copy, i, sibling, dg, c)):
                cp.wait_send()
        for cp in mine:
            cp.wait()

    return _Side(list(shards), [jax.ShapeDtypeStruct((N_DEV,) + s.shape, s.dtype) for s in shards],
                 [pltpu.SemaphoreType.DMA((n, 9)), pltpu.SemaphoreType.DMA((n, 9)), pltpu.SemaphoreType.DMA((n,))],
                 start, finish, middle)


def _pair_side(grads):
    n = len(grads)

    def copies(ins, outs, sems):
        send_sems, recv_sems = sems
        x, y, c = _coords()
        return [pltpu.make_async_remote_copy(
            src_ref=ins[i].at[2 * k + 1 - c], dst_ref=outs[i].at[k], send_sem=send_sems.at[i, k],
            recv_sem=recv_sems.at[i, k], device_id=(x, y, 1 - c), device_id_type=MESH)
            for i in range(n) for k in range(4)]

    def start(ins, outs, sems):
        for cp in copies(ins, outs, sems):
            cp.start()

    def finish(ins, outs, sems):
        for cp in copies(ins, outs, sems):
            cp.wait()

    return _Side(list(grads), [jax.ShapeDtypeStruct((4,) + g.shape[1:], g.dtype) for g in grads],
                 [pltpu.SemaphoreType.DMA((n, 4)), pltpu.SemaphoreType.DMA((n, 4))], start, finish)


def _pair_add(grads, recvs, core, name):
    count = len(grads)

    def body(c_ref, *refs):
        for n in range(count):
            refs[2 * count + n][...] = (refs[2 * n][...] + refs[2 * n + 1][...]).astype(BF16)

    in_specs, out_specs = [], []
    for g in grads:
        blk = (1,) + g.shape[1:]
        in_specs += [pl.BlockSpec(blk, lambda k, c_ref: (2 * k + c_ref[0], 0, 0)),
                     pl.BlockSpec(blk, lambda k, c_ref: (k, 0, 0))]
        out_specs.append(pl.BlockSpec(blk, lambda k, c_ref: (k, 0, 0)))
    out = pl.pallas_call(
        body, name=name,
        grid_spec=pltpu.PrefetchScalarGridSpec(num_scalar_prefetch=1, grid=(4,), in_specs=in_specs,
                                               out_specs=out_specs),
        out_shape=[jax.ShapeDtypeStruct((4,) + g.shape[1:], BF16) for g in grads],
        compiler_params=_params("arbitrary"),
    )(core, *[a for pair in zip(grads, recvs) for a in pair])
    return list(out)


def _chip_side(parts):
    n = len(parts)

    def copies(ins, outs, sems):
        send_sems, recv_sems, local_sems = sems
        x, y, c = _coords()
        chips = _other_chips(x, y)
        mine = [pltpu.make_async_copy(ins[i].at[2 * x + y], outs[i].at[2 * x + y], local_sems.at[i])
                for i in range(n)]
        sent = [pltpu.make_async_remote_copy(
            src_ref=ins[i].at[2 * px + py], dst_ref=outs[i].at[2 * x + y], send_sem=send_sems.at[i, j],
            recv_sem=recv_sems.at[i, j], device_id=(px, py, c), device_id_type=MESH)
            for i in range(n) for j, (px, py) in enumerate(chips)]
        return mine, sent, chips, c

    def start(ins, outs, sems):
        mine, sent, _, _ = copies(ins, outs, sems)
        for cp in mine + sent:
            cp.start()

    def finish(ins, outs, sems):
        mine, sent, chips, c = copies(ins, outs, sems)
        send_sems, recv_sems, _ = sems
        for i in range(n):
            for j, (px, py) in enumerate(chips):
                landed = outs[i].at[2 * px + py]
                pltpu.make_async_remote_copy(
                    src_ref=landed, dst_ref=landed, send_sem=send_sems.at[i, j], recv_sem=recv_sems.at[i, j],
                    device_id=(px, py, c), device_id_type=MESH).wait_recv()
        for cp in sent:
            cp.wait_send()
        for cp in mine:
            cp.wait()

    return _Side(list(parts), [jax.ShapeDtypeStruct(p.shape, p.dtype) for p in parts],
                 [pltpu.SemaphoreType.DMA((n, 3)), pltpu.SemaphoreType.DMA((n, 3)), pltpu.SemaphoreType.DMA((n,))],
                 start, finish)


def _all_reduce_small(vals):
    n = len(vals)

    def body(*refs):
        ins, outs, bufs = refs[:n], refs[n:2 * n], refs[2 * n:3 * n]
        send_sems, recv_sems = refs[3 * n:]
        x, y, c = _coords()
        me = 4 * x + 2 * y + c
        for i in range(n):
            bufs[i][me] = ins[i][...]
        sent, landed = [], []
        for k in range(1, N_DEV):
            px = 1 - x if k & 4 else x
            py = 1 - y if k & 2 else y
            pc = 1 - c if k & 1 else c
            for i in range(n):
                sent.append(pltpu.make_async_remote_copy(
                    src_ref=ins[i], dst_ref=bufs[i].at[me], send_sem=send_sems.at[i, k - 1],
                    recv_sem=recv_sems.at[i, k - 1], device_id=(px, py, pc), device_id_type=MESH))
                landed.append(pltpu.make_async_remote_copy(
                    src_ref=ins[i], dst_ref=bufs[i].at[4 * px + 2 * py + pc], send_sem=send_sems.at[i, k - 1],
                    recv_sem=recv_sems.at[i, k - 1], device_id=(x, y, c), device_id_type=MESH))
        for cp in sent:
            cp.start()
        for cp in landed:
            cp.wait_recv()
        for cp in sent:
            cp.wait_send()
        for i in range(n):
            acc = bufs[i][0]
            for j in range(1, N_DEV):
                acc = acc + bufs[i][j]
            outs[i][...] = acc

    vmem = pl.BlockSpec(memory_space=pltpu.VMEM)
    return pl.pallas_call(
        body, name="small_all_reduce", out_shape=[jax.ShapeDtypeStruct(v.shape, F32) for v in vals],
        in_specs=[vmem] * n, out_specs=[vmem] * n,
        scratch_shapes=[pltpu.VMEM((N_DEV,) + v.shape, F32) for v in vals]
        + [pltpu.SemaphoreType.DMA((n, N_DEV - 1)), pltpu.SemaphoreType.DMA((n, N_DEV - 1))],
    )(*vals)


def _adamw(ws, ms, vs, gs, name):
    count = len(ws)
    parts = ws[0].ndim == 3
    steps = 4 if all(w.shape[-2] % 32 == 0 for w in ws) else 1

    def body(*refs):
        for n in range(count):
            w_ref, m_ref, v_ref, g_ref = refs[4 * n:4 * n + 4]
            go_ref, d_ref, mo_ref, vo_ref = refs[4 * count + 4 * n:4 * count + 4 * n + 4]
            if parts:
                gv = g_ref[0].astype(F32)
                for k in range(1, 4):
                    gv = gv + g_ref[k].astype(F32)
                gv = gv[None]
            else:
                gv = g_ref[...]
            m2 = ADAM_B1 * m_ref[...] + (1.0 - ADAM_B1) * gv
            v2 = ADAM_B2 * v_ref[...] + (1.0 - ADAM_B2) * (gv * gv)
            m_hat = m2 / (1.0 - ADAM_B1 ** ADAM_STEP)
            v_hat = v2 / (1.0 - ADAM_B2 ** ADAM_STEP)
            go_ref[...] = gv
            d_ref[...] = -ADAM_LR * (m_hat / (jnp.sqrt(v_hat) + ADAM_EPS) + ADAM_WD * w_ref[...])
            mo_ref[...] = m2
            vo_ref[...] = v2

    in_specs, out_specs, out_shape = [], [], []
    for w in ws:
        r, cdim = w.shape[-2:]
        if parts:
            row = pl.BlockSpec((1, r // steps, cdim), lambda i: (0, i, 0))
            g_spec = pl.BlockSpec((4, r // steps, cdim), lambda i: (0, i, 0))
        else:
            row = g_spec = pl.BlockSpec((r // steps, cdim), lambda i: (i, 0))
        in_specs += [row, row, row, g_spec]
        out_specs += [row] * 4
        out_shape += [jax.ShapeDtypeStruct(w.shape, F32)] * 4
    args = [a for group in zip(ws, ms, vs, gs) for a in group]
    out = pl.pallas_call(
        body, name=name, grid=(steps,), in_specs=in_specs, out_specs=out_specs, out_shape=out_shape,
        compiler_params=_params("parallel"),
    )(*args)
    return [out[4 * n:4 * n + 4] for n in range(count)]


def _adamw_small(ws, ms, vs, gs):
    count = len(ws)

    def body(*refs):
        for n in range(count):
            w_ref, m_ref, v_ref, g_ref = refs[4 * n:4 * n + 4]
            d_ref, mo_ref, vo_ref = refs[4 * count + 3 * n:4 * count + 3 * n + 3]
            gv = g_ref[...]
            m2 = ADAM_B1 * m_ref[...] + (1.0 - ADAM_B1) * gv
            v2 = ADAM_B2 * v_ref[...] + (1.0 - ADAM_B2) * (gv * gv)
            m_hat = m2 / (1.0 - ADAM_B1 ** ADAM_STEP)
            v_hat = v2 / (1.0 - ADAM_B2 ** ADAM_STEP)
            d_ref[...] = -ADAM_LR * (m_hat / (jnp.sqrt(v_hat) + ADAM_EPS) + ADAM_WD * w_ref[...])
            mo_ref[...] = m2
            vo_ref[...] = v2

    out = pl.pallas_call(
        body, name="small_adamw",
        out_shape=[jax.ShapeDtypeStruct(w.shape, F32) for w in ws for _ in range(3)],
    )(*[a for group in zip(ws, ms, vs, gs) for a in group])
    return [out[3 * n:3 * n + 3] for n in range(count)]


WEIGHTS = ["ffn1_norm_g", "ffn1_w_gate", "ffn1_w_up", "ffn1_w_down", "mix_norm_g", "w_in", "attn_q_norm_g",
           "attn_k_norm_g", "attn_rel_bias", "hgrn_lower_bounds", "hgrn_out_norm_g", "w_out", "ffn2_norm_g",
           "ffn2_w_gate", "ffn2_w_up", "ffn2_w_down"]
COL_SHARDED = ("ffn1_w_gate", "ffn1_w_up", "w_in", "ffn2_w_gate", "ffn2_w_up")
ROW_SHARDED = ("ffn1_w_down", "w_out", "ffn2_w_down")
BIG = [n for n in WEIGHTS if n in COL_SHARDED or n in ROW_SHARDED]
SMALL = [n for n in WEIGHTS if n not in BIG]
FFN2 = ["ffn2_w_down", "ffn2_w_gate", "ffn2_w_up"]
MIXER = ["w_out", "w_in"]

PLAN = {
    "ffn1_norm": [("gather", ["ffn1_w_gate"])],
    "bias_expand": [("gather", ["ffn1_w_up"])],
    "ffn1_up": [("gather", ["ffn1_w_down", "w_out"])],
    "ffn1_down": [("gather", ["w_in"])],
    "mixer_fwd": [("gather", FFN2)],
    "ffn2_dh_gate": [("pair", FFN2)],
    "attn_bwd": [("chip", FFN2)],
    "in_proj_dx": [("pair", MIXER)],
    "ffn1_bwd_mid": [("chip", MIXER)],
    "ffn1_dwg": [("pair", ["ffn1_w_down"])],
    "ffn1_dwu": [("chip", ["ffn1_w_down"]), ("pair", ["ffn1_w_gate"])],
    "ffn1_dh_gate": [("chip", ["ffn1_w_gate"]), ("pair", ["ffn1_w_up"])],
    "bias_fold": [("chip", ["ffn1_w_up"])],
}


def _join_sides(sides):
    def split(refs, counts):
        out, at = [], 0
        for n in counts:
            out.append(refs[at:at + n])
            at += n
        return out

    n_in, n_out, n_sem = ([len(getattr(s, f)) for s in sides] for f in ("ins", "out_shape", "sems"))

    def run(which):
        def go(ins, outs, sems):
            for s, i, o, m in zip(sides, split(ins, n_in), split(outs, n_out), split(sems, n_sem)):
                if getattr(s, which) is not None:
                    getattr(s, which)(i, o, m)
        return go

    return _Side([a for s in sides for a in s.ins], [a for s in sides for a in s.out_shape],
                 [a for s in sides for a in s.sems], run("start"), run("finish"),
                 run("middle") if any(s.middle is not None for s in sides) else None)


class _Schedule:
    def __init__(self, shards):
        self.shards = shards
        self.weights = {}
        self.sliced = {}
        self.partials = {}
        self.reduced = {}

    def put(self, name, grad):
        self.sliced[name] = grad.reshape((N_DEV,) + self.shards[name].shape)

    def side_for(self, call):
        if call not in PLAN:
            return None
        sides = []
        for kind, names in PLAN[call]:
            if kind == "gather":
                sides.append(_gather_side([self.shards[n] for n in names]))
            elif kind == "pair":
                sides.append(_pair_side([self.sliced[n] for n in names]))
            else:
                sides.append(_chip_side([self.partials[n] for n in names]))
        return _join_sides(sides)

    def done(self, call, outs):
        at = 0
        for kind, names in PLAN[call]:
            self.file(kind, names, outs[at:at + len(names)])
            at += len(names)

    def file(self, kind, names, outs):
        if kind == "pair":
            core = lax.axis_index("c").astype(jnp.int32).reshape(1)
            sums = _pair_add([self.sliced[n] for n in names], list(outs), core, names[0] + "_pair_add")
            self.partials.update(dict(zip(names, sums)))
            return
        for n, o in zip(names, outs):
            if kind == "gather":
                self.weights[n] = o.reshape(N_DEV * o.shape[1], o.shape[2])
            else:
                self.reduced[n] = o


def kernel(x, ffn1_norm_g, ffn1_w_gate, ffn1_w_up, ffn1_w_down, mix_norm_g, w_in, attn_q_norm_g, attn_k_norm_g, attn_rel_bias, hgrn_lower_bounds, hgrn_out_norm_g, w_out, ffn2_norm_g, ffn2_w_gate, ffn2_w_up, ffn2_w_down, loss_target, m_ffn1_norm_g, m_ffn1_w_gate, m_ffn1_w_up, m_ffn1_w_down, m_mix_norm_g, m_w_in, m_attn_q_norm_g, m_attn_k_norm_g, m_attn_rel_bias, m_hgrn_lower_bounds, m_hgrn_out_norm_g, m_w_out, m_ffn2_norm_g, m_ffn2_w_gate, m_ffn2_w_up, m_ffn2_w_down, v_ffn1_norm_g, v_ffn1_w_gate, v_ffn1_w_up, v_ffn1_w_down, v_mix_norm_g, v_w_in, v_attn_q_norm_g, v_attn_k_norm_g, v_attn_rel_bias, v_hgrn_lower_bounds, v_hgrn_out_norm_g, v_w_out, v_ffn2_norm_g, v_ffn2_w_gate, v_ffn2_w_up, v_ffn2_w_down):
    wts = dict(zip(WEIGHTS, (ffn1_norm_g, ffn1_w_gate, ffn1_w_up, ffn1_w_down, mix_norm_g, w_in, attn_q_norm_g,
                             attn_k_norm_g, attn_rel_bias, hgrn_lower_bounds, hgrn_out_norm_g, w_out, ffn2_norm_g,
                             ffn2_w_gate, ffn2_w_up, ffn2_w_down)))
    mom = dict(zip(WEIGHTS, (m_ffn1_norm_g, m_ffn1_w_gate, m_ffn1_w_up, m_ffn1_w_down, m_mix_norm_g, m_w_in,
                             m_attn_q_norm_g, m_attn_k_norm_g, m_attn_rel_bias, m_hgrn_lower_bounds,
                             m_hgrn_out_norm_g, m_w_out, m_ffn2_norm_g, m_ffn2_w_gate, m_ffn2_w_up, m_ffn2_w_down)))
    var = dict(zip(WEIGHTS, (v_ffn1_norm_g, v_ffn1_w_gate, v_ffn1_w_up, v_ffn1_w_down, v_mix_norm_g, v_w_in,
                             v_attn_q_norm_g, v_attn_k_norm_g, v_attn_rel_bias, v_hgrn_lower_bounds,
                             v_hgrn_out_norm_g, v_w_out, v_ffn2_norm_g, v_ffn2_w_gate, v_ffn2_w_up, v_ffn2_w_down)))
    nb, seq, d = x.shape
    shapes = {n: wts[n].shape for n in WEIGHTS}

    def rows_first(a, n):
        return jnp.swapaxes(a, 1, 2) if n in COL_SHARDED else a

    sched = _Schedule({n: rows_first(wts[n], n)[0].astype(BF16) for n in BIG})
    sp = {n: wts[n] for n in SMALL}
    sp["attn_rel_bias"] = wts["attn_rel_bias"][0]
    _ACTIVE[0] = sched
    try:
        loss, dx, dsmall = _local_step(x.reshape(nb * seq, d), loss_target.reshape(nb * seq, d), sp,
                                       sched.weights, sched.put, nb, seq)
    finally:
        _ACTIVE[0] = None
    reduced = sched.reduced

    sums = _all_reduce_small([dsmall[n] for n in SMALL] + [jnp.full((1, 128), loss, F32)])
    gsmall = {n: s.reshape(shapes[n]) for n, s in zip(SMALL, sums)}
    loss_total = sums[-1][0, 0]

    grads, deltas, new_m, new_v = {}, {}, {}, {}
    for group, tag in (([n for n in BIG if n not in MIXER], "ffn_adamw"), (MIXER, "mixer_adamw")):
        outs = _adamw([rows_first(wts[n], n) for n in group], [rows_first(mom[n], n) for n in group],
                      [rows_first(var[n], n) for n in group], [reduced[n] for n in group], tag)
        for n, out in zip(group, outs):
            grads[n], deltas[n], new_m[n], new_v[n] = (rows_first(o, n) for o in out)
    outs = _adamw_small([wts[n] for n in SMALL], [mom[n] for n in SMALL], [var[n] for n in SMALL],
                        [gsmall[n] for n in SMALL])
    for n, (delta, m2, v2) in zip(SMALL, outs):
        deltas[n], new_m[n], new_v[n] = delta, m2, v2
    grads.update(gsmall)

    return (loss_total, dx.reshape(nb, seq, d), *[grads[n] for n in WEIGHTS], *[deltas[n] for n in WEIGHTS],
            *[new_m[n] for n in WEIGHTS], *[new_v[n] for n in WEIGHTS])
```

```python
import functools

import jax
import jax.numpy as jnp
from jax import lax
from jax.experimental import pallas as pl
from jax.experimental.pallas import tpu as pltpu

F32 = jnp.float32
BF16 = jnp.bfloat16

RMS_EPS = 1e-6
CHUNK = 64
LEFT_CHUNKS = 8
BAND = (LEFT_CHUNKS + 2) * CHUNK
KPAD = BAND - CHUNK
REL_CLIP = 128
N_REL = 2 * REL_CLIP + 1
N_REL_PAD = 384
ATTN_HEADS = 8
ATTN_HEAD_DIM = 64
ATTN_WIDTH = ATTN_HEADS * ATTN_HEAD_DIM
ATTN_LOCKSTEP = 4
ATTN_UNROLL = 8
HGRN_HEADS = 4
HGRN_HEAD_DIM = 128
HGRN_ROWS = 512
SUB = 16
N_SUB = CHUNK // SUB
DIAG_STAGE = 4
N_DEV = 8

ADAM_LR = 0.001
ADAM_B1 = 0.9
ADAM_B2 = 0.999
ADAM_EPS = 1e-08
ADAM_WD = 0.01
ADAM_STEP = 10

VMEM_LIMIT = 56 * 1024 * 1024

NT = (((1,), (1,)), ((), ()))
NN = (((1,), (0,)), ((), ()))


def _params(*sem):
    return pltpu.CompilerParams(dimension_semantics=sem, vmem_limit_bytes=VMEM_LIMIT)


def _sigmoid(v):
    return 0.5 * jnp.tanh(0.5 * v) + 0.5


def _dot(a, b, dims=NN):
    return lax.dot_general(a.astype(BF16), b.astype(BF16), dims, preferred_element_type=F32)


def _dot_exact01(m01, v):
    m = m01.astype(BF16)
    hi = v.astype(BF16)
    r1 = v - hi.astype(F32)
    mid = r1.astype(BF16)
    lo = (r1 - mid.astype(F32)).astype(BF16)
    out = lax.dot_general(m, hi, NN, preferred_element_type=F32)
    out = out + lax.dot_general(m, mid, NN, preferred_element_type=F32)
    return out + lax.dot_general(m, lo, NN, preferred_element_type=F32)


def _dot_exact01_r(v, m01):
    m = m01.astype(BF16)
    hi = v.astype(BF16)
    r1 = v - hi.astype(F32)
    mid = r1.astype(BF16)
    lo = (r1 - mid.astype(F32)).astype(BF16)
    out = lax.dot_general(hi, m, NN, preferred_element_type=F32)
    out = out + lax.dot_general(mid, m, NN, preferred_element_type=F32)
    return out + lax.dot_general(lo, m, NN, preferred_element_type=F32)


def _lockstep(stages):
    live = list(stages)
    while live:
        still = []
        for g in live:
            try:
                next(g)
                still.append(g)
            except StopIteration:
                pass
        live = still


def _row_sums_on_lanes(v):
    ones = jnp.ones((8, v.shape[1]), BF16)
    hi = v.astype(BF16)
    r1 = v - hi.astype(F32)
    mid = r1.astype(BF16)
    lo = (r1 - mid.astype(F32)).astype(BF16)
    out = lax.dot_general(ones, hi, NT, preferred_element_type=F32)
    out = out + lax.dot_general(ones, mid, NT, preferred_element_type=F32)
    return (out + lax.dot_general(ones, lo, NT, preferred_element_type=F32))[0:1, :]


def _tn(a, b):
    ap = jnp.concatenate([a, jnp.zeros_like(a)], axis=0)
    bp = jnp.concatenate([b, jnp.zeros_like(b)], axis=0)
    return _dot(ap.T, bp)


def _row_tile(t):
    for tm in (512, 256, 128, 64, 32, 16, 8):
        if t % tm == 0:
            return tm
    raise ValueError(t)


class _Side:
    def __init__(self, ins, out_shape, sems, start, finish, middle=None):
        self.ins, self.out_shape, self.sems = ins, out_shape, sems
        self.start, self.middle, self.finish = start, middle, finish


_ACTIVE = [None]


def _pallas(body, *, name, grid, in_specs, out_specs, out_shape, scratch_shapes=(), sem, args):
    sched = _ACTIVE[0]
    side = sched.side_for(name) if sched is not None else None
    if side is None:
        return pl.pallas_call(
            body, name=name, grid=grid, in_specs=list(in_specs), out_specs=list(out_specs),
            out_shape=list(out_shape), scratch_shapes=list(scratch_shapes), compiler_params=_params(*sem))(*args)
    cuts = [len(in_specs), len(side.ins), len(out_shape), len(side.out_shape), len(scratch_shapes)]

    def with_side(*refs):
        groups, at = [], 0
        for n in cuts:
            groups.append(refs[at:at + n])
            at += n
        ins, side_ins, outs, side_outs, scratch = groups
        side_sems = refs[at:]
        step, total = pl.program_id(0), grid[0]
        for a in range(1, len(grid)):
            step, total = step * grid[a] + pl.program_id(a), total * grid[a]
        has_middle = side.middle is not None and total >= 3

        @pl.when(step == 0)
        def _():
            side.start(side_ins, side_outs, side_sems)

        if has_middle:
            @pl.when(step == total // 2)
            def _():
                side.middle(side_ins, side_outs, side_sems)

        body(*ins, *outs, *scratch)

        @pl.when(step == total - 1)
        def _():
            if side.middle is not None and not has_middle:
                side.middle(side_ins, side_outs, side_sems)
            side.finish(side_ins, side_outs, side_sems)

    hbm = pl.BlockSpec(memory_space=pl.ANY)
    res = pl.pallas_call(
        with_side, name=name, grid=grid, in_specs=list(in_specs) + [hbm] * len(side.ins),
        out_specs=list(out_specs) + [hbm] * len(side.out_shape), out_shape=list(out_shape) + list(side.out_shape),
        scratch_shapes=list(scratch_shapes) + list(side.sems),
        compiler_params=_params(*(["arbitrary"] * len(grid))))(*args, *side.ins)
    sched.done(name, res[len(out_shape):])
    return res[:len(out_shape)]


def _rms_fwd(x, g, name):
    t, d = x.shape
    tm = _row_tile(t)

    def body(x_ref, g_ref, h_ref):
        xv = x_ref[...]
        r = lax.rsqrt(jnp.mean(xv * xv, axis=-1, keepdims=True) + RMS_EPS)
        h_ref[...] = (xv * r * g_ref[...]).astype(BF16)

    return _pallas(
        body, name=name, grid=(t // tm,),
        in_specs=[pl.BlockSpec((tm, d), lambda i: (i, 0)), pl.BlockSpec((1, d), lambda i: (0, 0))],
        out_specs=[pl.BlockSpec((tm, d), lambda i: (i, 0))], out_shape=[jax.ShapeDtypeStruct((t, d), BF16)],
        sem=("parallel",), args=(x, g))[0]


def _accumulate(ref, part, step):
    @pl.when(step == 0)
    def _():
        ref[...] = part

    @pl.when(step > 0)
    def _():
        ref[...] += part


def _mm(a, b, *, ta=False, tb=False, tm, tn, out_dtype=F32, add=None, scale=1.0, norm_g=None, norm_bwd=None, name):
    m, k = (a.shape[1], a.shape[0]) if ta else a.shape
    n = b.shape[0] if tb else b.shape[1]
    tm, tn = min(tm, m), min(tn, n)
    assert m % tm == 0 and n % tn == 0, (m, n, tm, tn)
    assert (norm_g is None and norm_bwd is None) or tn == n
    dims = (((0 if ta else 1,), (1 if tb else 0,)), ((), ()))
    n_in = 2 + (add is not None) + (norm_g is not None) + (3 if norm_bwd is not None else 0)

    def body(*refs):
        ins, outs = list(refs[2:n_in]), refs[n_in:]
        r = lax.dot_general(refs[0][...].astype(BF16), refs[1][...].astype(BF16), dims, preferred_element_type=F32)
        if scale != 1.0:
            r = r * scale
        if add is not None:
            r = r + ins.pop(0)[...]
        if norm_bwd is not None:
            xv, gv, dres = (ref[...] for ref in ins)
            rs = lax.rsqrt(jnp.mean(xv * xv, axis=-1, keepdims=True) + RMS_EPS)
            xhat = xv * rs
            gd = r * gv
            dx = dres + rs * (gd - xhat * jnp.mean(gd * xhat, axis=-1, keepdims=True))
            outs[0][...] = dx
            outs[1][...] = dx.astype(BF16)
            _accumulate(outs[2], jnp.sum(r * xhat, axis=0, keepdims=True), pl.program_id(0))
            return
        outs[0][...] = r.astype(out_dtype)
        if norm_g is not None:
            rs = lax.rsqrt(jnp.mean(r * r, axis=-1, keepdims=True) + RMS_EPS)
            outs[1][...] = (r * rs * ins.pop(0)[...]).astype(BF16)

    a_spec = pl.BlockSpec((k, tm), lambda i, j: (0, i)) if ta else pl.BlockSpec((tm, k), lambda i, j: (i, 0))
    b_spec = pl.BlockSpec((tn, k), lambda i, j: (j, 0)) if tb else pl.BlockSpec((k, tn), lambda i, j: (0, j))
    o_spec = pl.BlockSpec((tm, tn), lambda i, j: (i, j))
    vec = pl.BlockSpec((1, tn), lambda i, j: (0, j))
    args, specs = [a, b], [a_spec, b_spec]
    if add is not None:
        args.append(add)
        specs.append(o_spec)
    out_specs, out_shape = [o_spec], [jax.ShapeDtypeStruct((m, n), out_dtype)]
    if norm_g is not None:
        args.append(norm_g)
        specs.append(vec)
        out_specs.append(o_spec)
        out_shape.append(jax.ShapeDtypeStruct((m, n), BF16))
    if norm_bwd is not None:
        args += list(norm_bwd)
        specs += [o_spec, vec, o_spec]
        out_specs = [o_spec, o_spec, vec]
        out_shape = [jax.ShapeDtypeStruct((m, n), F32), jax.ShapeDtypeStruct((m, n), BF16),
                     jax.ShapeDtypeStruct((1, n), F32)]
    res = _pallas(body, name=name, grid=(m // tm, n // tn), in_specs=specs, out_specs=out_specs, out_shape=out_shape,
                  sem=("arbitrary", "arbitrary") if norm_bwd is not None else ("parallel", "parallel"), args=args)
    return res[0] if len(res) == 1 else res


def _ffn_tile(f):
    for tf in (1408, 512, 256, 128):
        if f % tf == 0:
            return tf
    raise ValueError(f)


def _ffn_fwd(h, x, wg, wu, wd, name, next_g=None, tgt=None):
    t, d = x.shape
    f = wg.shape[0]
    tm, tf = _row_tile(t), _ffn_tile(f)
    nf = f // tf
    assert (next_g is None) != (tgt is None)

    def body(h_ref, x_ref, wg_ref, wu_ref, wd_ref, tail_ref, g_ref, u_ref, o0_ref, o1_ref, *rest):
        acc_ref = rest[-1]
        j = pl.program_id(1)
        hv = h_ref[...]
        gv = lax.dot_general(hv, wg_ref[...], NT, preferred_element_type=F32)
        uv = lax.dot_general(hv, wu_ref[...], NT, preferred_element_type=F32)
        av = gv * _sigmoid(gv) * uv
        g_ref[...] = gv.astype(BF16)
        u_ref[...] = uv.astype(BF16)
        _accumulate(acc_ref, lax.dot_general(av.astype(BF16), wd_ref[...], NN, preferred_element_type=F32), j)

        @pl.when(j == nf - 1)
        def _():
            y = x_ref[...] + 0.5 * acc_ref[...]
            if tgt is None:
                o0_ref[...] = y
                rs = lax.rsqrt(jnp.mean(y * y, axis=-1, keepdims=True) + RMS_EPS)
                o1_ref[...] = (y * rs * tail_ref[...]).astype(BF16)
            else:
                e = y - tail_ref[...]
                dy = e * (1.0 / d)
                o0_ref[...] = dy
                o1_ref[...] = dy.astype(BF16)
                _accumulate(rest[0], jnp.sum(e * e, axis=0, keepdims=True), pl.program_id(0))

    row = pl.BlockSpec((tm, d), lambda i, j: (i, 0))
    hid = pl.BlockSpec((tm, tf), lambda i, j: (i, j))
    vec = pl.BlockSpec((1, d), lambda i, j: (0, 0))
    out_specs = [hid, hid, row, row] + ([vec] if tgt is not None else [])
    out_shape = [jax.ShapeDtypeStruct((t, f), BF16)] * 2 + [jax.ShapeDtypeStruct((t, d), F32),
                                                            jax.ShapeDtypeStruct((t, d), BF16)]
    if tgt is not None:
        out_shape.append(jax.ShapeDtypeStruct((1, d), F32))
    return _pallas(
        body, name=name, grid=(t // tm, nf),
        in_specs=[row, row] + [pl.BlockSpec((tf, d), lambda i, j: (j, 0))] * 3 + [vec if tgt is None else row],
        out_specs=out_specs, out_shape=out_shape, scratch_shapes=[pltpu.VMEM((tm, d), F32)],
        sem=("parallel" if tgt is None else "arbitrary", "arbitrary"),
        args=(h, x, wg, wu, wd, next_g if tgt is None else tgt))


def _ffn_up(h, wg, wu, name):
    t, d = h.shape
    f = wg.shape[0]
    tm, tf = _row_tile(t), _ffn_tile(f)

    def body(h_ref, wg_ref, wu_ref, g_ref, u_ref, a_ref):
        hv = h_ref[...]
        gv = lax.dot_general(hv, wg_ref[...], NT, preferred_element_type=F32)
        uv = lax.dot_general(hv, wu_ref[...], NT, preferred_element_type=F32)
        g_ref[...] = gv.astype(BF16)
        u_ref[...] = uv.astype(BF16)
        a_ref[...] = (gv * _sigmoid(gv) * uv).astype(BF16)

    hid = pl.BlockSpec((tm, tf), lambda i, j: (i, j))
    wrow = pl.BlockSpec((tf, d), lambda i, j: (j, 0))
    return _pallas(
        body, name=name, grid=(t // tm, f // tf), in_specs=[pl.BlockSpec((tm, d), lambda i, j: (i, 0)), wrow, wrow],
        out_specs=[hid, hid, hid], out_shape=[jax.ShapeDtypeStruct((t, f), BF16)] * 3,
        sem=("parallel", "parallel"), args=(h, wg, wu))


def _ffn_bwd_mid(dy, wd, g, u, name):
    t, d = dy.shape
    f = wd.shape[0]
    tm, tf = _row_tile(t), _ffn_tile(f)

    def body(dy_ref, wd_ref, g_ref, u_ref, dg_ref, du_ref, dwd_ref):
        dy16 = dy_ref[...]
        da = 0.5 * lax.dot_general(dy16, wd_ref[...], NT, preferred_element_type=F32)
        gv = g_ref[...].astype(F32)
        uv = u_ref[...].astype(F32)
        s = _sigmoid(gv)
        silu = gv * s
        dg_ref[...] = (da * uv * (s * (1.0 + gv * (1.0 - s)))).astype(BF16)
        du_ref[...] = (da * silu).astype(BF16)
        part = 0.5 * lax.dot_general((silu * uv).astype(BF16), dy16, (((0,), (0,)), ((), ())),
                                     preferred_element_type=F32)
        _accumulate(dwd_ref, part, pl.program_id(1))

    hid = pl.BlockSpec((tm, tf), lambda j, i: (i, j))
    wrow = pl.BlockSpec((tf, d), lambda j, i: (j, 0))
    return _pallas(
        body, name=name, grid=(f // tf, t // tm),
        in_specs=[pl.BlockSpec((tm, d), lambda j, i: (i, 0)), wrow, hid, hid],
        out_specs=[hid, hid, wrow],
        out_shape=[jax.ShapeDtypeStruct((t, f), BF16)] * 2 + [jax.ShapeDtypeStruct((f, d), F32)],
        sem=("parallel", "arbitrary"), args=(dy, wd, g, u))


def _rel_index(t, s_band):
    return jnp.clip(t + KPAD - s_band, -REL_CLIP, REL_CLIP) + REL_CLIP


def _bias_expand(rel_bias_pad):
    nh = rel_bias_pad.shape[0]

    def body(rb_ref, out_ref):
        rb = rb_ref[...]
        i_io = lax.broadcasted_iota(jnp.int32, (N_REL_PAD, BAND), 0)
        s_io = lax.broadcasted_iota(jnp.int32, (N_REL_PAD, BAND), 1)

        def row(r, carry):
            onehot = (i_io == _rel_index(pl.program_id(0) * rows + r, s_io)).astype(F32)
            out_ref[r] = _dot_exact01_r(rb, onehot)
            return carry

        lax.fori_loop(0, rows, row, 0)

    rows = 8
    return _pallas(
        body, name="bias_expand", grid=(CHUNK // rows,),
        in_specs=[pl.BlockSpec(rel_bias_pad.shape, lambda i: (0, 0))],
        out_specs=[pl.BlockSpec((rows, nh, BAND), lambda i: (i, 0, 0))],
        out_shape=[jax.ShapeDtypeStruct((CHUNK, nh, BAND), F32)], sem=("arbitrary",), args=(rel_bias_pad,))[0]


def _bias_fold(dbias):
    ng, nh = dbias.shape[0], dbias.shape[2]

    def body(db_ref, out_ref):
        s_io = lax.broadcasted_iota(jnp.int32, (BAND, N_REL_PAD), 0)
        i_io = lax.broadcasted_iota(jnp.int32, (BAND, N_REL_PAD), 1)

        def row(t, acc):
            onehot = (i_io == _rel_index(t, s_io)).astype(F32)
            d = db_ref[0, t]
            for gi in range(1, ng):
                d = d + db_ref[gi, t]
            return acc + _dot_exact01_r(d, onehot)

        out_ref[...] = lax.fori_loop(0, CHUNK, row, jnp.zeros((nh, N_REL_PAD), F32))

    return _pallas(
        body, name="bias_fold", grid=(1,), in_specs=[pl.BlockSpec(dbias.shape, lambda i: (0, 0, 0, 0))],
        out_specs=[pl.BlockSpec((nh, N_REL_PAD), lambda i: (0, 0))],
        out_shape=[jax.ShapeDtypeStruct((nh, N_REL_PAD), F32)], sem=("arbitrary",), args=(dbias,))[0]


def _left_half(shape):
    return lax.broadcasted_iota(jnp.int32, shape, len(shape) - 1) < ATTN_HEAD_DIM


def _stack_heads(v):
    left = _left_half(v.shape)
    zero = jnp.zeros_like(v)
    return jnp.concatenate([jnp.where(left, v, zero), jnp.where(left, zero, v)], axis=0)


def _unstack_heads(v):
    return jnp.where(_left_half((CHUNK, 128)), v[0:CHUNK, :], v[CHUNK:2 * CHUNK, :])


def _half_mean(v):
    r = lax.broadcasted_iota(jnp.int32, (128, 128), 0) < ATTN_HEAD_DIM
    c = lax.broadcasted_iota(jnp.int32, (128, 128), 1) < ATTN_HEAD_DIM
    return _dot_exact01_r(v, r == c) * (1.0 / ATTN_HEAD_DIM)


def _attn_prepare(q_ref, k_ref, v_ref, gq_ref, gk_ref, qs_scr, k_scr, v_scr):
    q, k = q_ref[...], k_ref[...]
    rq = lax.rsqrt(_half_mean(q * q) + RMS_EPS)
    rk = lax.rsqrt(_half_mean(k * k) + RMS_EPS)
    qhat, khat = q * rq, k * rk
    qs_scr[...] = (qhat * gq_ref[...] * ATTN_HEAD_DIM ** -0.5).astype(BF16)
    k_scr[0:KPAD, :] = jnp.zeros((KPAD, 128), BF16)
    v_scr[0:KPAD, :] = jnp.zeros((KPAD, 128), BF16)
    k_scr[KPAD:, :] = (khat * gk_ref[...]).astype(BF16)
    v_scr[KPAD:, :] = v_ref[...].astype(BF16)
    return qhat, rq, khat, rk


def _first_key(c):
    return jnp.maximum(CHUNK, (LEFT_CHUNKS + 1 - c) * CHUNK)


def _attn_fwd_chunk(c, qs_scr, k_scr, v_scr, bias_ref, o_ref):
    r0 = pl.multiple_of(c * CHUNK, CHUNK)
    s = lax.dot_general(_stack_heads(qs_scr[pl.ds(r0, CHUNK), :]), k_scr[pl.ds(r0, BAND), :], NT,
                        preferred_element_type=F32)
    yield
    col = lax.broadcasted_iota(jnp.int32, (2 * CHUNK, BAND), 1)
    s = jnp.where(col >= _first_key(c), s + bias_ref[...], -jnp.inf)
    m = jnp.max(s, axis=-1, keepdims=True)
    yield
    e = jnp.exp(s - m)
    yield
    inv = 1.0 / jnp.sum(e, axis=-1, keepdims=True)
    o = lax.dot_general(e.astype(BF16), v_scr[pl.ds(r0, BAND), :], NN, preferred_element_type=F32)
    yield
    o_ref[pl.ds(r0, CHUNK), :] = _unstack_heads(o * inv)


def _attn_bwd(proj, out, dout, bias, gq, gk, nb, seq):
    nc = seq // CHUNK
    lock = min(ATTN_LOCKSTEP, nc)
    assert nc % lock == 0
    scale = ATTN_HEAD_DIM ** -0.5

    def body(q_ref, k_ref, v_ref, o_ref, do_ref, bias_ref, gq_ref, gk_ref,
             dq_ref, dk_ref, dv_ref, dbias_ref, dgq_ref, dgk_ref,
             qs_scr, k_scr, v_scr, dqn_scr, dk_scr, dv_scr, db_scr):
        qhat, rq, khat, rk = _attn_prepare(q_ref, k_ref, v_ref, gq_ref, gk_ref, qs_scr, k_scr, v_scr)
        dk_scr[...] = jnp.zeros_like(dk_scr)
        dv_scr[...] = jnp.zeros_like(dv_scr)
        db_scr[...] = jnp.zeros_like(db_scr)

        def one_chunk(c):
            r0 = pl.multiple_of(c * CHUNK, CHUNK)
            qst = _stack_heads(qs_scr[pl.ds(r0, CHUNK), :])
            kb = k_scr[pl.ds(r0, BAND), :]
            vb = v_scr[pl.ds(r0, BAND), :]
            st = lax.dot_general(kb, qst, NT, preferred_element_type=F32) + bias_ref[...]
            dost = _stack_heads(do_ref[pl.ds(r0, CHUNK), :])
            dost16 = dost.astype(BF16)
            dpt = lax.dot_general(vb, dost16, NT, preferred_element_type=F32)
            yield
            key = lax.broadcasted_iota(jnp.int32, (BAND, 2 * CHUNK), 0)
            st = jnp.where(key >= _first_key(c), st, -jnp.inf)
            mx = jnp.max(st, axis=0, keepdims=True)
            drow = _row_sums_on_lanes(dost * _stack_heads(o_ref[pl.ds(r0, CHUNK), :]))
            yield
            et = jnp.exp(st - mx)
            yield
            pt = et * (1.0 / jnp.sum(et, axis=0, keepdims=True))
            yield
            dst = pt * (dpt - drow)
            dst16 = dst.astype(BF16)
            yield
            db_scr[...] += dst
            dqn_scr[pl.ds(r0, CHUNK), :] = scale * _unstack_heads(_dot(dst.T, kb))
            yield
            dk_scr[pl.ds(r0, BAND), :] += lax.dot_general(dst16, qst, NN, preferred_element_type=F32)
            yield
            dv_scr[pl.ds(r0, BAND), :] += lax.dot_general(pt.astype(BF16), dost16, NN, preferred_element_type=F32)

        def chunk(i, carry):
            _lockstep([one_chunk(i * lock + a) for a in range(lock)])
            return carry

        lax.fori_loop(0, nc // lock, chunk, 0, unroll=max(1, min(ATTN_UNROLL, nc) // lock))

        def norm_bwd(dn, hat, r, g_ref):
            gd = dn * g_ref[...]
            return r * (gd - hat * _half_mean(gd * hat)), jnp.sum(dn * hat, axis=0, keepdims=True)

        dq, dgq = norm_bwd(dqn_scr[...], qhat, rq, gq_ref)
        dk, dgk = norm_bwd(dk_scr[KPAD:, :], khat, rk, gk_ref)
        dq_ref[...] = dq.astype(BF16)
        dk_ref[...] = dk.astype(BF16)
        dv_ref[...] = dv_scr[KPAD:, :].astype(BF16)
        dbias_ref[0] = db_scr[...]
        dgq_ref[0] = dgq
        dgk_ref[0] = dgk

    def col(off):
        return pl.BlockSpec((seq, 128), lambda b, hp: (b, off + hp))

    vec = pl.BlockSpec((1, 128), lambda b, hp: (0, 0))
    gvec = pl.BlockSpec((1, 1, 128), lambda b, hp: (b * (ATTN_HEADS // 2) + hp, 0, 0))
    t = nb * seq
    return _pallas(
        body, name="attn_bwd", grid=(nb, ATTN_HEADS // 2),
        in_specs=[col(0), col(4), col(8), col(0), col(0),
                  pl.BlockSpec((BAND, 2 * CHUNK), lambda b, hp: (hp, 0)), vec, vec],
        out_specs=[col(0), col(0), col(0), pl.BlockSpec((1, BAND, 2 * CHUNK), lambda b, hp: (b, hp, 0)),
                   gvec, gvec],
        out_shape=[jax.ShapeDtypeStruct((t, ATTN_WIDTH), BF16)] * 3
        + [jax.ShapeDtypeStruct((nb, ATTN_HEADS // 2 * BAND, 2 * CHUNK), F32)]
        + [jax.ShapeDtypeStruct((nb * ATTN_HEADS // 2, 1, 128), F32)] * 2,
        scratch_shapes=[pltpu.VMEM((seq, 128), BF16), pltpu.VMEM((seq + KPAD, 128), BF16),
                        pltpu.VMEM((seq + KPAD, 128), BF16), pltpu.VMEM((seq, 128), F32),
                        pltpu.VMEM((seq + KPAD, 128), F32), pltpu.VMEM((seq + KPAD, 128), F32),
                        pltpu.VMEM((BAND, 2 * CHUNK), F32)],
        sem=("parallel", "parallel"), args=(proj, proj, proj, out, dout, bias, gq, gk))


def _tri(lower):
    r = lax.broadcasted_iota(jnp.int32, (CHUNK, CHUNK), 0)
    c = lax.broadcasted_iota(jnp.int32, (CHUNK, CHUNK), 1)
    return (r >= c) if lower else (r <= c)


def _hgrn_gates(hq, hf, lb):
    sq = _sigmoid(hq)
    sf = _sigmoid(hf)
    return hq * sq, sq, sf, lb + (1.0 - lb) * sf


def _hgrn_offdiag(q_s, k_s, b_s):
    row = lax.broadcasted_iota(jnp.int32, (CHUNK, HGRN_HEAD_DIM), 0)
    bv, qv, kv = b_s[...], q_s[...], k_s[...]
    eqs, eks = [], []
    for i in range(1, N_SUB):
        r = b_s[pl.ds(SUB * i - 1, 1), :]
        in_i = (row >= SUB * i) & (row < SUB * (i + 1))
        eqs.append(jnp.exp(jnp.where(in_i, bv - r, -jnp.inf)))
        eks.append(jnp.exp(jnp.where(row < SUB * i, r - bv, -jnp.inf)))
    eq = jnp.concatenate(eqs, axis=1)
    ek = jnp.concatenate(eks, axis=1)
    qt = jnp.concatenate([qv] * (N_SUB - 1), axis=1) * eq
    kt = jnp.concatenate([kv] * (N_SUB - 1), axis=1) * ek
    return qt, kt, eq, ek


def _hgrn_diag_e(b_s, i, s):
    t_io = lax.broadcasted_iota(jnp.int32, (SUB, HGRN_HEAD_DIM), 0)
    bi = b_s[pl.ds(SUB * i, SUB), :]
    return jnp.exp(jnp.where(t_io >= s, bi - b_s[pl.ds(SUB * i + s, 1), :], -jnp.inf)), t_io


def _hgrn_intra(q_s, k_s, b_s, a_s, qt, kt):
    ktp = jnp.concatenate([kt, jnp.zeros_like(kt)], axis=0)
    a_s[...] = _dot(qt, ktp, NT)
    yield
    col = lax.broadcasted_iota(jnp.int32, (SUB, HGRN_HEAD_DIM), 1)
    for i in range(N_SUB):
        qi = q_s[pl.ds(SUB * i, SUB), :]
        ai = jnp.zeros((SUB, HGRN_HEAD_DIM), F32)
        for s in range(SUB):
            e, _ = _hgrn_diag_e(b_s, i, s)
            a_col = jnp.sum(qi * k_s[pl.ds(SUB * i + s, 1), :] * e, axis=-1, keepdims=True)
            ai = ai + jnp.where(col == SUB * i + s, a_col, 0.0)
            if s % DIAG_STAGE == DIAG_STAGE - 1:
                yield
        a_s[pl.ds(SUB * i, SUB), :] += ai


def _mixer_fwd(proj, bias, gq, gk, lb, go, nb, seq):
    nc = seq // CHUNK
    hd = HGRN_HEAD_DIM
    nblk = ATTN_HEADS // 2
    rows_blk = seq // nblk
    nck = rows_blk // CHUNK
    per = nc // nck
    assert rows_blk % CHUNK == 0

    def body(aq_ref, ak_ref, av_ref, bias_ref, gq_ref, gk_ref, hq_ref, hf_ref, hi_ref, hg_ref, lb_ref, go_ref,
             ao_ref, y_ref, o_ref, st_ref, a_ref, qs_scr, k_scr, v_scr, st_all, q_all, k_all, b_all, a_all):
        _attn_prepare(aq_ref, ak_ref, av_ref, gq_ref, gk_ref, qs_scr, k_scr, v_scr)

        @pl.when(pl.program_id(1) == 0)
        def _():
            st_all[...] = jnp.zeros_like(st_all)

        lower = _tri(True)

        def head_chunk(hh, c, rows):
            ln = slice(hd * hh, hd * (hh + 1))
            st, q_s, k_s, b_s, a_s = st_all.at[hh], q_all.at[hh], k_all.at[hh], b_all.at[hh], a_all.at[hh]
            q, _, _, f = _hgrn_gates(hq_ref[rows, ln], hf_ref[rows, ln], lb_ref[:, ln])
            v = hi_ref[rows, ln]
            yield
            b = _dot_exact01(lower, jnp.log(f))
            q_s[...] = q
            k_s[...] = 1.0 - f
            b_s[...] = b
            st_ref[hh, c] = st[...]
            yield
            qt, kt, _, _ = _hgrn_offdiag(q_s, k_s, b_s)
            yield
            yield from _hgrn_intra(q_s, k_s, b_s, a_s, qt, kt)
            a16 = a_s[...].astype(BF16)
            a_ref[hh, c] = a16
            vp = jnp.concatenate([v, jnp.zeros_like(v)], axis=0)
            o = _dot(a16, vp) + _dot(q * jnp.exp(b), st[...], NT)
            yield
            bl = b_s[pl.ds(CHUNK - 1, 1), :]
            st[...] = st[...] * jnp.exp(bl) + _tn(v, (1.0 - f) * jnp.exp(bl - b))
            o_ref[rows, ln] = o
            yield
            n = o * lax.rsqrt(jnp.mean(o * o, axis=-1, keepdims=True) + RMS_EPS) * go_ref[...]
            hg = hg_ref[rows, ln]
            y_ref[rows, ln] = n * hg * _sigmoid(hg)

        def chunk(c, carry):
            rows = pl.ds(pl.multiple_of(c * CHUNK, CHUNK), CHUNK)
            _lockstep([_attn_fwd_chunk(c * per + a, qs_scr, k_scr, v_scr, bias_ref, ao_ref) for a in range(per)]
                      + [head_chunk(hh, c, rows) for hh in range(HGRN_HEADS)])
            return carry

        lax.fori_loop(0, nck, chunk, 0, unroll=min(2, nck))

    hp, wide = HGRN_HEADS, HGRN_HEADS * hd

    def acol(off):
        return pl.BlockSpec((seq, 128), lambda b, s: (b, off + s))

    def col(off):
        return pl.BlockSpec((rows_blk, wide), lambda b, s: (b * nblk + s, off // hp))

    out = pl.BlockSpec((rows_blk, wide), lambda b, s: (b * nblk + s, 0))
    vec = pl.BlockSpec((1, 128), lambda b, s: (0, 0))
    t = nb * seq
    return _pallas(
        body, name="mixer_fwd", grid=(nb, nblk),
        in_specs=[acol(0), acol(4), acol(8), pl.BlockSpec((2 * CHUNK, BAND), lambda b, s: (s, 0)), vec, vec,
                  col(12), col(16), col(20), col(24), pl.BlockSpec((1, wide), lambda b, s: (0, 0)), vec],
        out_specs=[pl.BlockSpec((seq, 128), lambda b, s: (b, s)), out, out,
                   pl.BlockSpec((hp, nck, hd, hd), lambda b, s: (b, s, 0, 0)),
                   pl.BlockSpec((hp, nck, CHUNK, hd), lambda b, s: (b, s, 0, 0))],
        out_shape=[jax.ShapeDtypeStruct((t, ATTN_WIDTH), F32)] + [jax.ShapeDtypeStruct((t, wide), F32)] * 2
        + [jax.ShapeDtypeStruct((nb * hp, nc, hd, hd), F32), jax.ShapeDtypeStruct((nb * hp, nc, CHUNK, hd), BF16)],
        scratch_shapes=[pltpu.VMEM((seq, 128), BF16), pltpu.VMEM((seq + KPAD, 128), BF16),
                        pltpu.VMEM((seq + KPAD, 128), BF16), pltpu.VMEM((hp, hd, hd), F32)]
        + [pltpu.VMEM((hp, CHUNK, hd), F32)] * 4,
        sem=("parallel", "arbitrary"), args=(proj,) * 3 + (bias, gq, gk) + (proj,) * 4 + (lb, go))


def _hgrn_bwd(proj, lb, go, o_pre, states, scores, dout, nb, seq):
    nc = seq // CHUNK
    hd = HGRN_HEAD_DIM
    rows_blk = min(HGRN_ROWS, seq)
    nblk, nck = seq // rows_blk, rows_blk // CHUNK

    def body(hq_ref, hf_ref, hi_ref, hg_ref, lb_ref, go_ref, o_ref, st_ref, a_ref, dy_ref,
             dhq_ref, dhf_ref, dhi_ref, dhg_ref, dlb_ref, dgo_ref,
             dst_all, q_all, k_all, b_all, da_all, dqi_all, dki_all, dlb_all, dgo_all):
        @pl.when(pl.program_id(1) == 0)
        def _():
            dst_all[...] = jnp.zeros_like(dst_all)
            dlb_all[...] = jnp.zeros_like(dlb_all)
            dgo_all[...] = jnp.zeros_like(dgo_all)

        lower, upper = _tri(True), _tri(False)
        gov = go_ref[...]
        row = lax.broadcasted_iota(jnp.int32, (CHUNK, hd), 0)

        def head_chunk(hh, c, rows):
            ln = slice(hd * hh, hd * (hh + 1))
            dst, q_s, k_s, b_s = dst_all.at[hh], q_all.at[hh], k_all.at[hh], b_all.at[hh]
            da_s, dqi_s, dki_s = da_all.at[hh], dqi_all.at[hh], dki_all.at[hh]
            dlb_acc, dgo_acc = dlb_all.at[hh], dgo_all.at[hh]
            lbv = lb_ref[:, ln]
            hq, hf, v, hg = hq_ref[rows, ln], hf_ref[rows, ln], hi_ref[rows, ln], hg_ref[rows, ln]
            q, sq, sf, f = _hgrn_gates(hq, hf, lbv)
            kk = 1.0 - f
            yield
            b = _dot_exact01(lower, jnp.log(f))
            q_s[...] = q
            k_s[...] = kk
            b_s[...] = b
            yield
            bl = b_s[pl.ds(CHUNK - 1, 1), :]
            ebl = jnp.exp(bl)
            ekd = jnp.exp(bl - b)
            kd = kk * ekd
            eb = jnp.exp(b)
            qb = q * eb
            st0 = st_ref[hh, c]
            dst1 = dst[...]
            yield

            o = o_ref[rows, ln]
            dy = dy_ref[rows, ln]
            sg = _sigmoid(hg)
            rstd = lax.rsqrt(jnp.mean(o * o, axis=-1, keepdims=True) + RMS_EPS)
            ohat = o * rstd
            dn = dy * hg * sg
            dhg_ref[rows, ln] = (dy * ohat * gov * (sg * (1.0 + hg * (1.0 - sg)))).astype(BF16)
            dgo_acc[...] += jnp.sum(dn * ohat, axis=0, keepdims=True)
            gdn = dn * gov
            do = rstd * (gdn - ohat * jnp.mean(gdn * ohat, axis=-1, keepdims=True))
            yield

            qt, kt, eq, ek = _hgrn_offdiag(q_s, k_s, b_s)
            da = _dot(do, v, NT)
            dat = _dot(v, do, NT)
            da_s[...] = da
            yield
            dqo = _dot(da, kt) * eq
            dko = _dot(dat, qt) * ek
            dqi_s[...] = sum(dqo[:, j * hd:(j + 1) * hd] for j in range(N_SUB - 1))
            dki_s[...] = sum(dko[:, j * hd:(j + 1) * hd] for j in range(N_SUB - 1))
            yield
            col = lax.broadcasted_iota(jnp.int32, (SUB, CHUNK), 1)
            for i in range(N_SUB):
                qi = q_s[pl.ds(SUB * i, SUB), :]
                dai = da_s[pl.ds(SUB * i, SUB), :]
                dqd = jnp.zeros((SUB, hd), F32)
                for s in range(SUB):
                    e, _ = _hgrn_diag_e(b_s, i, s)
                    dacol = jnp.sum(jnp.where(col == SUB * i + s, dai, 0.0), axis=-1, keepdims=True)
                    w = dacol * e
                    dqd = dqd + w * k_s[pl.ds(SUB * i + s, 1), :]
                    dki_s[pl.ds(SUB * i + s, 1), :] += jnp.sum(w * qi, axis=0, keepdims=True)
                    if s % DIAG_STAGE == DIAG_STAGE - 1:
                        yield
                dqi_s[pl.ds(SUB * i, SUB), :] += dqd
            dqi, dki = dqi_s[...], dki_s[...]

            dv = _tn(a_ref[hh, c].astype(F32), do)[0:CHUNK, :] + _dot(kd, dst1, NT)
            dqb = _dot(do, st0)
            dkd = _dot(v, dst1)
            yield
            t2 = dkd * kd
            dq = dqb * eb + dqi
            dk = dkd * ekd + dki
            dbl = jnp.sum(t2, axis=0, keepdims=True) + ebl * jnp.sum(st0 * dst1, axis=0, keepdims=True)
            db = dqb * qb - t2 + q * dqi - kk * dki + jnp.where(row == CHUNK - 1, dbl, 0.0)
            yield
            dg = _dot_exact01(upper, db)
            dst[...] = dst1 * ebl + _tn(do, qb)
            yield

            df = dg / f - dk
            dhf_ref[rows, ln] = (df * (1.0 - lbv) * sf * (1.0 - sf)).astype(BF16)
            dlb_acc[...] += jnp.sum(df * (1.0 - sf), axis=0, keepdims=True)
            dhq_ref[rows, ln] = (dq * (sq * (1.0 + hq * (1.0 - sq)))).astype(BF16)
            dhi_ref[rows, ln] = dv.astype(BF16)

        def chunk(it, carry):
            c = nck - 1 - it
            rows = pl.ds(pl.multiple_of(c * CHUNK, CHUNK), CHUNK)
            _lockstep([head_chunk(hh, c, rows) for hh in range(HGRN_HEADS)])
            return carry

        lax.fori_loop(0, nck, chunk, 0, unroll=min(2, nck))

        @pl.when(pl.program_id(1) == nblk - 1)
        def _():
            dlb_ref[...] = dlb_all[...]
            dgo_ref[...] = dgo_all[...]

    hp, wide = HGRN_HEADS, HGRN_HEADS * hd

    def col(off):
        return pl.BlockSpec((rows_blk, wide), lambda b, s: (b * nblk + nblk - 1 - s, off // hp))

    out = pl.BlockSpec((rows_blk, wide), lambda b, s: (b * nblk + nblk - 1 - s, 0))
    part = pl.BlockSpec((hp, 1, hd), lambda b, s: (b, 0, 0))
    t = nb * seq
    return pl.pallas_call(
        body, name="hgrn_bwd", grid=(nb, nblk),
        in_specs=[col(12), col(16), col(20), col(24), pl.BlockSpec((1, wide), lambda b, s: (0, 0)),
                  pl.BlockSpec((1, hd), lambda b, s: (0, 0)), out,
                  pl.BlockSpec((hp, nck, hd, hd), lambda b, s: (b, nblk - 1 - s, 0, 0)),
                  pl.BlockSpec((hp, nck, CHUNK, hd), lambda b, s: (b, nblk - 1 - s, 0, 0)), col(4)],
        out_specs=[out, out, out, out, part, part],
        out_shape=[jax.ShapeDtypeStruct((t, wide), BF16)] * 4 + [jax.ShapeDtypeStruct((nb * hp, 1, hd), F32)] * 2,
        scratch_shapes=[pltpu.VMEM((hp, hd, hd), F32)] + [pltpu.VMEM((hp, CHUNK, hd), F32)] * 3
        + [pltpu.VMEM((hp, CHUNK, CHUNK), F32)] + [pltpu.VMEM((hp, CHUNK, hd), F32)] * 2
        + [pltpu.VMEM((hp, 1, hd), F32)] * 2,
        compiler_params=_params("parallel", "arbitrary"),
    )(proj, proj, proj, proj, lb, go, o_pre, states, scores, dout)


def _lb_fwd(lower_bounds):
    def body(x_ref, o_ref):
        xv = x_ref[...]
        e = jnp.exp(xv - jnp.max(xv, axis=0, keepdims=True))
        o_ref[...] = e[0:1, :] / jnp.sum(e, axis=0, keepdims=True)

    return pl.pallas_call(body, name="lb_fwd",
                          out_shape=jax.ShapeDtypeStruct((1, lower_bounds.shape[1]), F32))(lower_bounds)


def _lb_bwd(lower_bounds, dlb_parts):
    ng = dlb_parts.shape[0]

    def body(x_ref, d_ref, o_ref):
        xv = x_ref[...]
        e = jnp.exp(xv - jnp.max(xv, axis=0, keepdims=True))
        p = e / jnp.sum(e, axis=0, keepdims=True)
        dlb = d_ref[0]
        for gi in range(1, ng):
            dlb = dlb + d_ref[gi]
        first = lax.broadcasted_iota(jnp.int32, xv.shape, 0) == 0
        o_ref[...] = p * (jnp.where(first, dlb, 0.0) - p[0:1, :] * dlb)

    return pl.pallas_call(body, name="lb_bwd",
                          out_shape=jax.ShapeDtypeStruct(lower_bounds.shape, F32))(lower_bounds, dlb_parts)


def _ffn_bwd(x, g, h, gate, up, dy, dy16, w, put, tag):
    wg, wu, wd = w[tag + "_w_gate"], w[tag + "_w_up"], w[tag + "_w_down"]
    dgate, dup, dwd = _ffn_bwd_mid(dy16, wd, gate, up, tag + "_bwd_mid")
    put(tag + "_w_down", dwd)
    put(tag + "_w_gate", _mm(dgate, h, ta=True, tm=1408, tn=512, name=tag + "_dwg"))
    put(tag + "_w_up", _mm(dup, h, ta=True, tm=1408, tn=512, name=tag + "_dwu"))
    dh = _mm(dgate, wg, tm=512, tn=1024, name=tag + "_dh_gate")
    return _mm(dup, wu, tm=512, tn=1024, add=dh, norm_bwd=(x, g, dy), name=tag + "_dh_up")


def _local_step(x, tgt, sp, w, put, nb, seq):
    d = x.shape[1]
    h1 = _rms_fwd(x, sp["ffn1_norm_g"], "ffn1_norm")
    rb_pad = jnp.pad(sp["attn_rel_bias"], ((0, 0), (0, N_REL_PAD - N_REL)))
    bias = jnp.transpose(_bias_expand(rb_pad), (1, 0, 2)).reshape(ATTN_HEADS * CHUNK, BAND)
    gq2 = jnp.concatenate([sp["attn_q_norm_g"]] * 2, axis=1)
    gk2 = jnp.concatenate([sp["attn_k_norm_g"]] * 2, axis=1)
    lb = _lb_fwd(sp["hgrn_lower_bounds"])
    gate1, up1, act1 = _ffn_up(h1, w["ffn1_w_gate"], w["ffn1_w_up"], "ffn1_up")
    x1, h2 = _mm(act1, w["ffn1_w_down"], tm=512, tn=d, add=x, scale=0.5, norm_g=sp["mix_norm_g"],
                 name="ffn1_down")
    proj = _mm(h2, w["w_in"], tb=True, tm=256, tn=w["w_in"].shape[0], name="in_proj")
    attn, hy, ho, hstate, hscore = _mixer_fwd(proj, bias, gq2, gk2, lb, sp["hgrn_out_norm_g"], nb, seq)
    mix = jnp.concatenate([attn, hy], axis=1)
    x2, h3 = _mm(mix, w["w_out"], tm=512, tn=1024, add=x1, norm_g=sp["ffn2_norm_g"], name="out_proj")
    gate2, up2, dx3, dx3_16, sq = _ffn_fwd(h3, x2, w["ffn2_w_gate"], w["ffn2_w_up"], w["ffn2_w_down"], "ffn2_fwd",
                                           tgt=tgt)
    loss = 0.5 * jnp.sum(sq) / d

    dx2, dx2_16, dg3 = _ffn_bwd(x2, sp["ffn2_norm_g"], h3, gate2, up2, dx3, dx3_16, w, put, "ffn2")
    dmix = _mm(dx2_16, w["w_out"], tb=True, tm=512, tn=1024, name="out_proj_dx")
    put("w_out", _mm(mix, dx2_16, ta=True, tm=512, tn=1024, name="out_proj_dw"))
    bias_t = jnp.transpose(bias.reshape(ATTN_HEADS // 2, 2 * CHUNK, BAND), (0, 2, 1)).reshape(-1, 2 * CHUNK)
    dq, dk, dv, dbias, dgq, dgk = _attn_bwd(proj, attn, dmix, bias_t, gq2, gk2, nb, seq)
    dbias = jnp.transpose(dbias.reshape(nb, ATTN_HEADS // 2, BAND, 2, CHUNK), (0, 4, 1, 3, 2))
    dbias = dbias.reshape(nb, CHUNK, ATTN_HEADS, BAND)
    dgq = jnp.sum(dgq, axis=(0, 1)).reshape(2, ATTN_HEAD_DIM).sum(axis=0, keepdims=True)
    dgk = jnp.sum(dgk, axis=(0, 1)).reshape(2, ATTN_HEAD_DIM).sum(axis=0, keepdims=True)
    dhq, dhf, dhi, dhg, dlb, dgo = _hgrn_bwd(proj, lb, sp["hgrn_out_norm_g"], ho, hstate, hscore, dmix, nb, seq)
    dproj = jnp.concatenate([dq, dk, dv, dhq, dhf, dhi, dhg], axis=1)
    put("w_in", _mm(dproj, h2, ta=True, tm=512, tn=1024, name="in_proj_dw"))
    dx1, dx1_16, dgm = _mm(dproj, w["w_in"], tm=512, tn=1024, norm_bwd=(x1, sp["mix_norm_g"], dx2),
                           name="in_proj_dx")
    dx0, _, dg1 = _ffn_bwd(x, sp["ffn1_norm_g"], h1, gate1, up1, dx1, dx1_16, w, put, "ffn1")

    small = {
        "ffn1_norm_g": dg1, "mix_norm_g": dgm, "ffn2_norm_g": dg3,
        "attn_q_norm_g": dgq, "attn_k_norm_g": dgk,
        "attn_rel_bias": _bias_fold(dbias)[:, :N_REL],
        "hgrn_lower_bounds": _lb_bwd(sp["hgrn_lower_bounds"], dlb.reshape(nb, 1, HGRN_HEADS * HGRN_HEAD_DIM)),
        "hgrn_out_norm_g": jnp.sum(dgo, axis=(0, 1))[None, :],
    }
    return loss, dx0, small


MESH = pl.DeviceIdType.MESH
ANY = pl.BlockSpec(memory_space=pl.ANY)


def _coords():
    return lax.axis_index("x"), lax.axis_index("y"), lax.axis_index("c")


def _other_chips(x, y):
    return [(1 - x, y), (x, 1 - y), (1 - x, 1 - y)]


def _gather_side(shards):
    n = len(shards)

    def copies(ins, outs, sems):
        send_sems, recv_sems, local_sems = sems
        x, y, c = _coords()
        xn, yn, dg = (1 - x, y), (x, 1 - y), (1 - x, 1 - y)

        def copy(i, k, block, to, half=None, src=None):
            bx, by, bc = block
            dst = outs[i].at[4 * bx + 2 * by + bc]
            if half is not None:
                rows = shards[i].shape[0] // 2
                dst = dst.at[pl.ds(half * rows, rows)]
            return pltpu.make_async_remote_copy(
                src_ref=dst if src is None else src, dst_ref=dst, send_sem=send_sems.at[i, k],
                recv_sem=recv_sems.at[i, k], device_id=to, device_id_type=MESH)

        mine = [pltpu.make_async_copy(ins[i], outs[i].at[4 * x + 2 * y + c], local_sems.at[i]) for i in range(n)]
        return copy, mine, (x, y, c), (x, y, 1 - c), xn, yn, dg, c

    def own(copy, i, ins, me, sibling, xn, yn, c):
        return [copy(i, 0, me, sibling, src=ins[i]), copy(i, 1, me, (*xn, c), src=ins[i]),
                copy(i, 2, me, (*yn, c), src=ins[i])]

    def passed_on(copy, i, sibling, xn, yn, c):
        return [copy(i, 3, (*xn, c), sibling), copy(i, 5, (*xn, c), (*yn, c), half=0),
                copy(i, 4, (*yn, c), sibling), copy(i, 6, (*yn, c), (*xn, c), half=1)]

    def diagonal(copy, i, sibling, dg, c):
        return [copy(i, 7, (*dg, c), sibling, half=0), copy(i, 8, (*dg, c), sibling, half=1)]

    def start(ins, outs, sems):
        copy, mine, me, sibling, xn, yn, dg, c = copies(ins, outs, sems)
        for cp in mine + [cp for i in range(n) for cp in own(copy, i, ins, me, sibling, xn, yn, c)]:
            cp.start()

    def middle(ins, outs, sems):
        copy, mine, me, sibling, xn, yn, dg, c = copies(ins, outs, sems)
        for i in range(n):
            fwd_x, relay_x, fwd_y, relay_y = passed_on(copy, i, sibling, xn, yn, c)
            copy(i, 1, (*xn, c), me).wait_recv()
            fwd_x.start()
            relay_x.start()
            copy(i, 2, (*yn, c), me).wait_recv()
            fwd_y.start()
            relay_y.start()

    def finish(ins, outs, sems):
        copy, mine, me, sibling, xn, yn, dg, c = copies(ins, outs, sems)
        for i in range(n):
            top, bottom = diagonal(copy, i, sibling, dg, c)
            copy(i, 5, (*dg, c), me, half=0).wait_recv()
            top.start()
            copy(i, 6, (*dg, c), me, half=1).wait_recv()
            bottom.start()
        for i in range(n):
            copy(i, 0, sibling, me).wait_recv()
            copy(i, 3, (*xn, 1 - c), me).wait_recv()
            copy(i, 4, (*yn, 1 - c), me).wait_recv()
            copy(i, 7, (*dg, 1 - c), me, half=0).wait_recv()
            copy(i, 8, (*dg, 1 - c), me, half=1).wait_recv()
        for i in range(n):
            for cp in (own(copy, i, ins, me, sibling, xn, yn, c) + passed_on(copy, i, sibling, xn, yn, c)
                       + diagonal(copy, i, sibling, dg, c)):
                cp.wait_send()
        for cp in mine:
            cp.wait()

    return _Side(list(shards), [jax.ShapeDtypeStruct((N_DEV,) + s.shape, s.dtype) for s in shards],
                 [pltpu.SemaphoreType.DMA((n, 9)), pltpu.SemaphoreType.DMA((n, 9)), pltpu.SemaphoreType.DMA((n,))],
                 start, finish, middle)


def _pair_side(grads):
    n = len(grads)

    def copies(ins, outs, sems):
        send_sems, recv_sems = sems
        x, y, c = _coords()
        return [pltpu.make_async_remote_copy(
            src_ref=ins[i].at[2 * k + 1 - c], dst_ref=outs[i].at[k], send_sem=send_sems.at[i, k],
            recv_sem=recv_sems.at[i, k], device_id=(x, y, 1 - c), device_id_type=MESH)
            for i in range(n) for k in range(4)]

    def start(ins, outs, sems):
        for cp in copies(ins, outs, sems):
            cp.start()

    def finish(ins, outs, sems):
        for cp in copies(ins, outs, sems):
            cp.wait()

    return _Side(list(grads), [jax.ShapeDtypeStruct((4,) + g.shape[1:], g.dtype) for g in grads],
                 [pltpu.SemaphoreType.DMA((n, 4)), pltpu.SemaphoreType.DMA((n, 4))], start, finish)


def _pair_add(grads, recvs, core, name):
    count = len(grads)

    def body(c_ref, *refs):
        for n in range(count):
            refs[2 * count + n][...] = (refs[2 * n][...] + refs[2 * n + 1][...]).astype(BF16)

    in_specs, out_specs = [], []
    for g in grads:
        blk = (1,) + g.shape[1:]
        in_specs += [pl.BlockSpec(blk, lambda k, c_ref: (2 * k + c_ref[0], 0, 0)),
                     pl.BlockSpec(blk, lambda k, c_ref: (k, 0, 0))]
        out_specs.append(pl.BlockSpec(blk, lambda k, c_ref: (k, 0, 0)))
    out = pl.pallas_call(
        body, name=name,
        grid_spec=pltpu.PrefetchScalarGridSpec(num_scalar_prefetch=1, grid=(4,), in_specs=in_specs,
                                               out_specs=out_specs),
        out_shape=[jax.ShapeDtypeStruct((4,) + g.shape[1:], BF16) for g in grads],
        compiler_params=_params("arbitrary"),
    )(core, *[a for pair in zip(grads, recvs) for a in pair])
    return list(out)


def _chip_side(parts):
    n = len(parts)

    def copies(ins, outs, sems):
        send_sems, recv_sems, local_sems = sems
        x, y, c = _coords()
        chips = _other_chips(x, y)
        mine = [pltpu.make_async_copy(ins[i].at[2 * x + y], outs[i].at[2 * x + y], local_sems.at[i])
                for i in range(n)]
        sent = [pltpu.make_async_remote_copy(
            src_ref=ins[i].at[2 * px + py], dst_ref=outs[i].at[2 * x + y], send_sem=send_sems.at[i, j],
            recv_sem=recv_sems.at[i, j], device_id=(px, py, c), device_id_type=MESH)
            for i in range(n) for j, (px, py) in enumerate(chips)]
        return mine, sent, chips, c

    def start(ins, outs, sems):
        mine, sent, _, _ = copies(ins, outs, sems)
        for cp in mine + sent:
            cp.start()

    def finish(ins, outs, sems):
        mine, sent, chips, c = copies(ins, outs, sems)
        send_sems, recv_sems, _ = sems
        for i in range(n):
            for j, (px, py) in enumerate(chips):
                landed = outs[i].at[2 * px + py]
                pltpu.make_async_remote_copy(
                    src_ref=landed, dst_ref=landed, send_sem=send_sems.at[i, j], recv_sem=recv_sems.at[i, j],
                    device_id=(px, py, c), device_id_type=MESH).wait_recv()
        for cp in sent:
            cp.wait_send()
        for cp in mine:
            cp.wait()

    return _Side(list(parts), [jax.ShapeDtypeStruct(p.shape, p.dtype) for p in parts],
                 [pltpu.SemaphoreType.DMA((n, 3)), pltpu.SemaphoreType.DMA((n, 3)), pltpu.SemaphoreType.DMA((n,))],
                 start, finish)


def _all_reduce_small(vals):
    n = len(vals)

    def body(*refs):
        ins, outs, bufs = refs[:n], refs[n:2 * n], refs[2 * n:3 * n]
        send_sems, recv_sems = refs[3 * n:]
        x, y, c = _coords()
        me = 4 * x + 2 * y + c
        for i in range(n):
            bufs[i][me] = ins[i][...]
        sent, landed = [], []
        for k in range(1, N_DEV):
            px = 1 - x if k & 4 else x
            py = 1 - y if k & 2 else y
            pc = 1 - c if k & 1 else c
            for i in range(n):
                sent.append(pltpu.make_async_remote_copy(
                    src_ref=ins[i], dst_ref=bufs[i].at[me], send_sem=send_sems.at[i, k - 1],
                    recv_sem=recv_sems.at[i, k - 1], device_id=(px, py, pc), device_id_type=MESH))
                landed.append(pltpu.make_async_remote_copy(
                    src_ref=ins[i], dst_ref=bufs[i].at[4 * px + 2 * py + pc], send_sem=send_sems.at[i, k - 1],
                    recv_sem=recv_sems.at[i, k - 1], device_id=(x, y, c), device_id_type=MESH))
        for cp in sent:
            cp.start()
        for cp in landed:
            cp.wait_recv()
        for cp in sent:
            cp.wait_send()
        for i in range(n):
            acc = bufs[i][0]
            for j in range(1, N_DEV):
                acc = acc + bufs[i][j]
            outs[i][...] = acc

    vmem = pl.BlockSpec(memory_space=pltpu.VMEM)
    return pl.pallas_call(
        body, name="small_all_reduce", out_shape=[jax.ShapeDtypeStruct(v.shape, F32) for v in vals],
        in_specs=[vmem] * n, out_specs=[vmem] * n,
        scratch_shapes=[pltpu.VMEM((N_DEV,) + v.shape, F32) for v in vals]
        + [pltpu.SemaphoreType.DMA((n, N_DEV - 1)), pltpu.SemaphoreType.DMA((n, N_DEV - 1))],
    )(*vals)


def _adamw(ws, ms, vs, gs, name):
    count = len(ws)
    parts = ws[0].ndim == 3
    steps = 4 if all(w.shape[-2] % 32 == 0 for w in ws) else 1

    def body(*refs):
        for n in range(count):
            w_ref, m_ref, v_ref, g_ref = refs[4 * n:4 * n + 4]
            go_ref, d_ref, mo_ref, vo_ref = refs[4 * count + 4 * n:4 * count + 4 * n + 4]
            if parts:
                gv = g_ref[0].astype(F32)
                for k in range(1, 4):
                    gv = gv + g_ref[k].astype(F32)
                gv = gv[None]
            else:
                gv = g_ref[...]
            m2 = ADAM_B1 * m_ref[...] + (1.0 - ADAM_B1) * gv
            v2 = ADAM_B2 * v_ref[...] + (1.0 - ADAM_B2) * (gv * gv)
            m_hat = m2 / (1.0 - ADAM_B1 ** ADAM_STEP)
            v_hat = v2 / (1.0 - ADAM_B2 ** ADAM_STEP)
            go_ref[...] = gv
            d_ref[...] = -ADAM_LR * (m_hat / (jnp.sqrt(v_hat) + ADAM_EPS) + ADAM_WD * w_ref[...])
            mo_ref[...] = m2
            vo_ref[...] = v2

    in_specs, out_specs, out_shape = [], [], []
    for w in ws:
        r, cdim = w.shape[-2:]
        if parts:
            row = pl.BlockSpec((1, r // steps, cdim), lambda i: (0, i, 0))
            g_spec = pl.BlockSpec((4, r // steps, cdim), lambda i: (0, i, 0))
        else:
            row = g_spec = pl.BlockSpec((r // steps, cdim), lambda i: (i, 0))
        in_specs += [row, row, row, g_spec]
        out_specs += [row] * 4
        out_shape += [jax.ShapeDtypeStruct(w.shape, F32)] * 4
    args = [a for group in zip(ws, ms, vs, gs) for a in group]
    out = pl.pallas_call(
        body, name=name, grid=(steps,), in_specs=in_specs, out_specs=out_specs, out_shape=out_shape,
        compiler_params=_params("parallel"),
    )(*args)
    return [out[4 * n:4 * n + 4] for n in range(count)]


def _adamw_small(ws, ms, vs, gs):
    count = len(ws)

    def body(*refs):
        for n in range(count):
            w_ref, m_ref, v_ref, g_ref = refs[4 * n:4 * n + 4]
            d_ref, mo_ref, vo_ref = refs[4 * count + 3 * n:4 * count + 3 * n + 3]
            gv = g_ref[...]
            m2 = ADAM_B1 * m_ref[...] + (1.0 - ADAM_B1) * gv
            v2 = ADAM_B2 * v_ref[...] + (1.0 - ADAM_B2) * (gv * gv)
            m_hat = m2 / (1.0 - ADAM_B1 ** ADAM_STEP)
            v_hat = v2 / (1.0 - ADAM_B2 ** ADAM_STEP)
            d_ref[...] = -ADAM_LR * (m_hat / (jnp.sqrt(v_hat) + ADAM_EPS) + ADAM_WD * w_ref[...])
            mo_ref[...] = m2
            vo_ref[...] = v2

    out = pl.pallas_call(
        body, name="small_adamw",
        out_shape=[jax.ShapeDtypeStruct(w.shape, F32) for w in ws for _ in range(3)],
    )(*[a for group in zip(ws, ms, vs, gs) for a in group])
    return [out[3 * n:3 * n + 3] for n in range(count)]


WEIGHTS = ["ffn1_norm_g", "ffn1_w_gate", "ffn1_w_up", "ffn1_w_down", "mix_norm_g", "w_in", "attn_q_norm_g",
           "attn_k_norm_g", "attn_rel_bias", "hgrn_lower_bounds", "hgrn_out_norm_g", "w_out", "ffn2_norm_g",
           "ffn2_w_gate", "ffn2_w_up", "ffn2_w_down"]
COL_SHARDED = ("ffn1_w_gate", "ffn1_w_up", "w_in", "ffn2_w_gate", "ffn2_w_up")
ROW_SHARDED = ("ffn1_w_down", "w_out", "ffn2_w_down")
BIG = [n for n in WEIGHTS if n in COL_SHARDED or n in ROW_SHARDED]
SMALL = [n for n in WEIGHTS if n not in BIG]
FFN2 = ["ffn2_w_down", "ffn2_w_gate", "ffn2_w_up"]
MIXER = ["w_out", "w_in"]

PLAN = {
    "ffn1_norm": [("gather", ["ffn1_w_gate"])],
    "bias_expand": [("gather", ["ffn1_w_up"])],
    "ffn1_up": [("gather", ["ffn1_w_down", "w_out"])],
    "ffn1_down": [("gather", ["w_in"])],
    "mixer_fwd": [("gather", FFN2)],
    "ffn2_dh_gate": [("pair", FFN2)],
    "attn_bwd": [("chip", FFN2)],
    "in_proj_dx": [("pair", MIXER)],
    "ffn1_bwd_mid": [("chip", MIXER)],
    "ffn1_dwg": [("pair", ["ffn1_w_down"])],
    "ffn1_dwu": [("chip", ["ffn1_w_down"]), ("pair", ["ffn1_w_gate"])],
    "ffn1_dh_gate": [("chip", ["ffn1_w_gate"]), ("pair", ["ffn1_w_up"])],
    "bias_fold": [("chip", ["ffn1_w_up"])],
}


def _join_sides(sides):
    def split(refs, counts):
        out, at = [], 0
        for n in counts:
            out.append(refs[at:at + n])
            at += n
        return out

    n_in, n_out, n_sem = ([len(getattr(s, f)) for s in sides] for f in ("ins", "out_shape", "sems"))

    def run(which):
        def go(ins, outs, sems):
            for s, i, o, m in zip(sides, split(ins, n_in), split(outs, n_out), split(sems, n_sem)):
                if getattr(s, which) is not None:
                    getattr(s, which)(i, o, m)
        return go

    return _Side([a for s in sides for a in s.ins], [a for s in sides for a in s.out_shape],
                 [a for s in sides for a in s.sems], run("start"), run("finish"),
                 run("middle") if any(s.middle is not None for s in sides) else None)


class _Schedule:
    def __init__(self, shards):
        self.shards = shards
        self.weights = {}
        self.sliced = {}
        self.partials = {}
        self.reduced = {}

    def put(self, name, grad):
        self.sliced[name] = grad.reshape((N_DEV,) + self.shards[name].shape)

    def side_for(self, call):
        if call not in PLAN:
            return None
        sides = []
        for kind, names in PLAN[call]:
            if kind == "gather":
                sides.append(_gather_side([self.shards[n] for n in names]))
            elif kind == "pair":
                sides.append(_pair_side([self.sliced[n] for n in names]))
            else:
                sides.append(_chip_side([self.partials[n] for n in names]))
        return _join_sides(sides)

    def done(self, call, outs):
        at = 0
        for kind, names in PLAN[call]:
            self.file(kind, names, outs[at:at + len(names)])
            at += len(names)

    def file(self, kind, names, outs):
        if kind == "pair":
            core = lax.axis_index("c").astype(jnp.int32).reshape(1)
            sums = _pair_add([self.sliced[n] for n in names], list(outs), core, names[0] + "_pair_add")
            self.partials.update(dict(zip(names, sums)))
            return
        for n, o in zip(names, outs):
            if kind == "gather":
                self.weights[n] = o.reshape(N_DEV * o.shape[1], o.shape[2])
            else:
                self.reduced[n] = o


def kernel(x, ffn1_norm_g, ffn1_w_gate, ffn1_w_up, ffn1_w_down, mix_norm_g, w_in, attn_q_norm_g, attn_k_norm_g, attn_rel_bias, hgrn_lower_bounds, hgrn_out_norm_g, w_out, ffn2_norm_g, ffn2_w_gate, ffn2_w_up, ffn2_w_down, loss_target, m_ffn1_norm_g, m_ffn1_w_gate, m_ffn1_w_up, m_ffn1_w_down, m_mix_norm_g, m_w_in, m_attn_q_norm_g, m_attn_k_norm_g, m_attn_rel_bias, m_hgrn_lower_bounds, m_hgrn_out_norm_g, m_w_out, m_ffn2_norm_g, m_ffn2_w_gate, m_ffn2_w_up, m_ffn2_w_down, v_ffn1_norm_g, v_ffn1_w_gate, v_ffn1_w_up, v_ffn1_w_down, v_mix_norm_g, v_w_in, v_attn_q_norm_g, v_attn_k_norm_g, v_attn_rel_bias, v_hgrn_lower_bounds, v_hgrn_out_norm_g, v_w_out, v_ffn2_norm_g, v_ffn2_w_gate, v_ffn2_w_up, v_ffn2_w_down):
    wts = dict(zip(WEIGHTS, (ffn1_norm_g, ffn1_w_gate, ffn1_w_up, ffn1_w_down, mix_norm_g, w_in, attn_q_norm_g,
                             attn_k_norm_g, attn_rel_bias, hgrn_lower_bounds, hgrn_out_norm_g, w_out, ffn2_norm_g,
                             ffn2_w_gate, ffn2_w_up, ffn2_w_down)))
    mom = dict(zip(WEIGHTS, (m_ffn1_norm_g, m_ffn1_w_gate, m_ffn1_w_up, m_ffn1_w_down, m_mix_norm_g, m_w_in,
                             m_attn_q_norm_g, m_attn_k_norm_g, m_attn_rel_bias, m_hgrn_lower_bounds,
                             m_hgrn_out_norm_g, m_w_out, m_ffn2_norm_g, m_ffn2_w_gate, m_ffn2_w_up, m_ffn2_w_down)))
    var = dict(zip(WEIGHTS, (v_ffn1_norm_g, v_ffn1_w_gate, v_ffn1_w_up, v_ffn1_w_down, v_mix_norm_g, v_w_in,
                             v_attn_q_norm_g, v_attn_k_norm_g, v_attn_rel_bias, v_hgrn_lower_bounds,
                             v_hgrn_out_norm_g, v_w_out, v_ffn2_norm_g, v_ffn2_w_gate, v_ffn2_w_up, v_ffn2_w_down)))
    nb, seq, d = x.shape
    shapes = {n: wts[n].shape for n in WEIGHTS}

    def rows_first(a, n):
        return jnp.swapaxes(a, 1, 2) if n in COL_SHARDED else a

    sched = _Schedule({n: rows_first(wts[n], n)[0].astype(BF16) for n in BIG})
    sp = {n: wts[n] for n in SMALL}
    sp["attn_rel_bias"] = wts["attn_rel_bias"][0]
    _ACTIVE[0] = sched
    try:
        loss, dx, dsmall = _local_step(x.reshape(nb * seq, d), loss_target.reshape(nb * seq, d), sp,
                                       sched.weights, sched.put, nb, seq)
    finally:
        _ACTIVE[0] = None
    reduced = sched.reduced

    sums = _all_reduce_small([dsmall[n] for n in SMALL] + [jnp.full((1, 128), loss, F32)])
    gsmall = {n: s.reshape(shapes[n]) for n, s in zip(SMALL, sums)}
    loss_total = sums[-1][0, 0]

    grads, deltas, new_m, new_v = {}, {}, {}, {}
    for group, tag in (([n for n in BIG if n not in MIXER], "ffn_adamw"), (MIXER, "mixer_adamw")):
        outs = _adamw([rows_first(wts[n], n) for n in group], [rows_first(mom[n], n) for n in group],
                      [rows_first(var[n], n) for n in group], [reduced[n] for n in group], tag)
        for n, out in zip(group, outs):
            grads[n], deltas[n], new_m[n], new_v[n] = (rows_first(o, n) for o in out)
    outs = _adamw_small([wts[n] for n in SMALL], [mom[n] for n in SMALL], [var[n] for n in SMALL],
                        [gsmall[n] for n in SMALL])
    for n, (delta, m2, v2) in zip(SMALL, outs):
        deltas[n], new_m[n], new_v[n] = delta, m2, v2
    grads.update(gsmall)

    return (loss_total, dx.reshape(nb, seq, d), *[grads[n] for n in WEIGHTS], *[deltas[n] for n in WEIGHTS],
            *[new_m[n] for n in WEIGHTS], *[new_v[n] for n in WEIGHTS])
```

```python
import functools

import jax
import jax.numpy as jnp
from jax import lax
from jax.experimental import pallas as pl
from jax.experimental.pallas import tpu as pltpu

F32 = jnp.float32
BF16 = jnp.bfloat16

RMS_EPS = 1e-6
CHUNK = 64
LEFT_CHUNKS = 8
BAND = (LEFT_CHUNKS + 2) * CHUNK
KPAD = BAND - CHUNK
REL_CLIP = 128
N_REL = 2 * REL_CLIP + 1
N_REL_PAD = 384
ATTN_HEADS = 8
ATTN_HEAD_DIM = 64
ATTN_WIDTH = ATTN_HEADS * ATTN_HEAD_DIM
ATTN_LOCKSTEP = 4
ATTN_UNROLL = 16
HGRN_HEADS = 4
HGRN_HEAD_DIM = 128
HGRN_ROWS = 512
SUB = 16
N_SUB = CHUNK // SUB
DIAG_STAGE = 4
N_DEV = 8

ADAM_LR = 0.001
ADAM_B1 = 0.9
ADAM_B2 = 0.999
ADAM_EPS = 1e-08
ADAM_WD = 0.01
ADAM_STEP = 10

VMEM_LIMIT = 56 * 1024 * 1024

NT = (((1,), (1,)), ((), ()))
NN = (((1,), (0,)), ((), ()))


def _params(*sem):
    return pltpu.CompilerParams(dimension_semantics=sem, vmem_limit_bytes=VMEM_LIMIT)


def _sigmoid(v):
    return 0.5 * jnp.tanh(0.5 * v) + 0.5


def _dot(a, b, dims=NN):
    return lax.dot_general(a.astype(BF16), b.astype(BF16), dims, preferred_element_type=F32)


def _dot_exact01(m01, v):
    m = m01.astype(BF16)
    hi = v.astype(BF16)
    r1 = v - hi.astype(F32)
    mid = r1.astype(BF16)
    lo = (r1 - mid.astype(F32)).astype(BF16)
    out = lax.dot_general(m, hi, NN, preferred_element_type=F32)
    out = out + lax.dot_general(m, mid, NN, preferred_element_type=F32)
    return out + lax.dot_general(m, lo, NN, preferred_element_type=F32)


def _dot_exact01_r(v, m01):
    m = m01.astype(BF16)
    hi = v.astype(BF16)
    r1 = v - hi.astype(F32)
    mid = r1.astype(BF16)
    lo = (r1 - mid.astype(F32)).astype(BF16)
    out = lax.dot_general(hi, m, NN, preferred_element_type=F32)
    out = out + lax.dot_general(mid, m, NN, preferred_element_type=F32)
    return out + lax.dot_general(lo, m, NN, preferred_element_type=F32)


def _lockstep(stages):
    live = list(stages)
    while live:
        still = []
        for g in live:
            try:
                next(g)
                still.append(g)
            except StopIteration:
                pass
        live = still


def _row_sums_on_lanes(v):
    ones = jnp.ones((8, v.shape[1]), BF16)
    hi = v.astype(BF16)
    r1 = v - hi.astype(F32)
    mid = r1.astype(BF16)
    lo = (r1 - mid.astype(F32)).astype(BF16)
    out = lax.dot_general(ones, hi, NT, preferred_element_type=F32)
    out = out + lax.dot_general(ones, mid, NT, preferred_element_type=F32)
    return (out + lax.dot_general(ones, lo, NT, preferred_element_type=F32))[0:1, :]


def _tn(a, b):
    ap = jnp.concatenate([a, jnp.zeros_like(a)], axis=0)
    bp = jnp.concatenate([b, jnp.zeros_like(b)], axis=0)
    return _dot(ap.T, bp)


def _row_tile(t):
    for tm in (512, 256, 128, 64, 32, 16, 8):
        if t % tm == 0:
            return tm
    raise ValueError(t)


class _Side:
    def __init__(self, ins, out_shape, sems, start, finish, middle=None):
        self.ins, self.out_shape, self.sems = ins, out_shape, sems
        self.start, self.middle, self.finish = start, middle, finish


_ACTIVE = [None]


def _pallas(body, *, name, grid, in_specs, out_specs, out_shape, scratch_shapes=(), sem, args):
    sched = _ACTIVE[0]
    side = sched.side_for(name) if sched is not None else None
    if side is None:
        return pl.pallas_call(
            body, name=name, grid=grid, in_specs=list(in_specs), out_specs=list(out_specs),
            out_shape=list(out_shape), scratch_shapes=list(scratch_shapes), compiler_params=_params(*sem))(*args)
    cuts = [len(in_specs), len(side.ins), len(out_shape), len(side.out_shape), len(scratch_shapes)]

    def with_side(*refs):
        groups, at = [], 0
        for n in cuts:
            groups.append(refs[at:at + n])
            at += n
        ins, side_ins, outs, side_outs, scratch = groups
        side_sems = refs[at:]
        step, total = pl.program_id(0), grid[0]
        for a in range(1, len(grid)):
            step, total = step * grid[a] + pl.program_id(a), total * grid[a]
        has_middle = side.middle is not None and total >= 3

        @pl.when(step == 0)
        def _():
            side.start(side_ins, side_outs, side_sems)

        if has_middle:
            @pl.when(step == total // 2)
            def _():
                side.middle(side_ins, side_outs, side_sems)

        body(*ins, *outs, *scratch)

        @pl.when(step == total - 1)
        def _():
            if side.middle is not None and not has_middle:
                side.middle(side_ins, side_outs, side_sems)
            side.finish(side_ins, side_outs, side_sems)

    hbm = pl.BlockSpec(memory_space=pl.ANY)
    res = pl.pallas_call(
        with_side, name=name, grid=grid, in_specs=list(in_specs) + [hbm] * len(side.ins),
        out_specs=list(out_specs) + [hbm] * len(side.out_shape), out_shape=list(out_shape) + list(side.out_shape),
        scratch_shapes=list(scratch_shapes) + list(side.sems),
        compiler_params=_params(*(["arbitrary"] * len(grid))))(*args, *side.ins)
    sched.done(name, res[len(out_shape):])
    return res[:len(out_shape)]


def _rms_fwd(x, g, name):
    t, d = x.shape
    tm = _row_tile(t)

    def body(x_ref, g_ref, h_ref):
        xv = x_ref[...]
        r = lax.rsqrt(jnp.mean(xv * xv, axis=-1, keepdims=True) + RMS_EPS)
        h_ref[...] = (xv * r * g_ref[...]).astype(BF16)

    return _pallas(
        body, name=name, grid=(t // tm,),
        in_specs=[pl.BlockSpec((tm, d), lambda i: (i, 0)), pl.BlockSpec((1, d), lambda i: (0, 0))],
        out_specs=[pl.BlockSpec((tm, d), lambda i: (i, 0))], out_shape=[jax.ShapeDtypeStruct((t, d), BF16)],
        sem=("parallel",), args=(x, g))[0]


def _accumulate(ref, part, step):
    @pl.when(step == 0)
    def _():
        ref[...] = part

    @pl.when(step > 0)
    def _():
        ref[...] += part


def _mm(a, b, *, ta=False, tb=False, tm, tn, out_dtype=F32, add=None, scale=1.0, norm_g=None, norm_bwd=None, name):
    m, k = (a.shape[1], a.shape[0]) if ta else a.shape
    n = b.shape[0] if tb else b.shape[1]
    tm, tn = min(tm, m), min(tn, n)
    assert m % tm == 0 and n % tn == 0, (m, n, tm, tn)
    assert (norm_g is None and norm_bwd is None) or tn == n
    dims = (((0 if ta else 1,), (1 if tb else 0,)), ((), ()))
    n_in = 2 + (add is not None) + (norm_g is not None) + (3 if norm_bwd is not None else 0)

    def body(*refs):
        ins, outs = list(refs[2:n_in]), refs[n_in:]
        r = lax.dot_general(refs[0][...].astype(BF16), refs[1][...].astype(BF16), dims, preferred_element_type=F32)
        if scale != 1.0:
            r = r * scale
        if add is not None:
            r = r + ins.pop(0)[...]
        if norm_bwd is not None:
            xv, gv, dres = (ref[...] for ref in ins)
            rs = lax.rsqrt(jnp.mean(xv * xv, axis=-1, keepdims=True) + RMS_EPS)
            xhat = xv * rs
            gd = r * gv
            dx = dres + rs * (gd - xhat * jnp.mean(gd * xhat, axis=-1, keepdims=True))
            outs[0][...] = dx
            outs[1][...] = dx.astype(BF16)
            _accumulate(outs[2], jnp.sum(r * xhat, axis=0, keepdims=True), pl.program_id(0))
            return
        outs[0][...] = r.astype(out_dtype)
        if norm_g is not None:
            rs = lax.rsqrt(jnp.mean(r * r, axis=-1, keepdims=True) + RMS_EPS)
            outs[1][...] = (r * rs * ins.pop(0)[...]).astype(BF16)

    a_spec = pl.BlockSpec((k, tm), lambda i, j: (0, i)) if ta else pl.BlockSpec((tm, k), lambda i, j: (i, 0))
    b_spec = pl.BlockSpec((tn, k), lambda i, j: (j, 0)) if tb else pl.BlockSpec((k, tn), lambda i, j: (0, j))
    o_spec = pl.BlockSpec((tm, tn), lambda i, j: (i, j))
    vec = pl.BlockSpec((1, tn), lambda i, j: (0, j))
    args, specs = [a, b], [a_spec, b_spec]
    if add is not None:
        args.append(add)
        specs.append(o_spec)
    out_specs, out_shape = [o_spec], [jax.ShapeDtypeStruct((m, n), out_dtype)]
    if norm_g is not None:
        args.append(norm_g)
        specs.append(vec)
        out_specs.append(o_spec)
        out_shape.append(jax.ShapeDtypeStruct((m, n), BF16))
    if norm_bwd is not None:
        args += list(norm_bwd)
        specs += [o_spec, vec, o_spec]
        out_specs = [o_spec, o_spec, vec]
        out_shape = [jax.ShapeDtypeStruct((m, n), F32), jax.ShapeDtypeStruct((m, n), BF16),
                     jax.ShapeDtypeStruct((1, n), F32)]
    res = _pallas(body, name=name, grid=(m // tm, n // tn), in_specs=specs, out_specs=out_specs, out_shape=out_shape,
                  sem=("arbitrary", "arbitrary") if norm_bwd is not None else ("parallel", "parallel"), args=args)
    return res[0] if len(res) == 1 else res


def _ffn_tile(f):
    for tf in (1408, 512, 256, 128):
        if f % tf == 0:
            return tf
    raise ValueError(f)


def _ffn_fwd(h, x, wg, wu, wd, name, next_g=None, tgt=None):
    t, d = x.shape
    f = wg.shape[0]
    tm, tf = _row_tile(t), _ffn_tile(f)
    nf = f // tf
    assert (next_g is None) != (tgt is None)

    def body(h_ref, x_ref, wg_ref, wu_ref, wd_ref, tail_ref, g_ref, u_ref, o0_ref, o1_ref, *rest):
        acc_ref = rest[-1]
        j = pl.program_id(1)
        hv = h_ref[...]
        gv = lax.dot_general(hv, wg_ref[...], NT, preferred_element_type=F32)
        uv = lax.dot_general(hv, wu_ref[...], NT, preferred_element_type=F32)
        av = gv * _sigmoid(gv) * uv
        g_ref[...] = gv.astype(BF16)
        u_ref[...] = uv.astype(BF16)
        _accumulate(acc_ref, lax.dot_general(av.astype(BF16), wd_ref[...], NN, preferred_element_type=F32), j)

        @pl.when(j == nf - 1)
        def _():
            y = x_ref[...] + 0.5 * acc_ref[...]
            if tgt is None:
                o0_ref[...] = y
                rs = lax.rsqrt(jnp.mean(y * y, axis=-1, keepdims=True) + RMS_EPS)
                o1_ref[...] = (y * rs * tail_ref[...]).astype(BF16)
            else:
                e = y - tail_ref[...]
                dy = e * (1.0 / d)
                o0_ref[...] = dy
                o1_ref[...] = dy.astype(BF16)
                _accumulate(rest[0], jnp.sum(e * e, axis=0, keepdims=True), pl.program_id(0))

    row = pl.BlockSpec((tm, d), lambda i, j: (i, 0))
    hid = pl.BlockSpec((tm, tf), lambda i, j: (i, j))
    vec = pl.BlockSpec((1, d), lambda i, j: (0, 0))
    out_specs = [hid, hid, row, row] + ([vec] if tgt is not None else [])
    out_shape = [jax.ShapeDtypeStruct((t, f), BF16)] * 2 + [jax.ShapeDtypeStruct((t, d), F32),
                                                            jax.ShapeDtypeStruct((t, d), BF16)]
    if tgt is not None:
        out_shape.append(jax.ShapeDtypeStruct((1, d), F32))
    return _pallas(
        body, name=name, grid=(t // tm, nf),
        in_specs=[row, row] + [pl.BlockSpec((tf, d), lambda i, j: (j, 0))] * 3 + [vec if tgt is None else row],
        out_specs=out_specs, out_shape=out_shape, scratch_shapes=[pltpu.VMEM((tm, d), F32)],
        sem=("parallel" if tgt is None else "arbitrary", "arbitrary"),
        args=(h, x, wg, wu, wd, next_g if tgt is None else tgt))


def _ffn_up(h, wg, wu, name):
    t, d = h.shape
    f = wg.shape[0]
    tm, tf = _row_tile(t), _ffn_tile(f)

    def body(h_ref, wg_ref, wu_ref, g_ref, u_ref, a_ref):
        hv = h_ref[...]
        gv = lax.dot_general(hv, wg_ref[...], NT, preferred_element_type=F32)
        uv = lax.dot_general(hv, wu_ref[...], NT, preferred_element_type=F32)
        g_ref[...] = gv.astype(BF16)
        u_ref[...] = uv.astype(BF16)
        a_ref[...] = (gv * _sigmoid(gv) * uv).astype(BF16)

    hid = pl.BlockSpec((tm, tf), lambda i, j: (i, j))
    wrow = pl.BlockSpec((tf, d), lambda i, j: (j, 0))
    return _pallas(
        body, name=name, grid=(t // tm, f // tf), in_specs=[pl.BlockSpec((tm, d), lambda i, j: (i, 0)), wrow, wrow],
        out_specs=[hid, hid, hid], out_shape=[jax.ShapeDtypeStruct((t, f), BF16)] * 3,
        sem=("parallel", "parallel"), args=(h, wg, wu))


def _ffn_bwd_mid(dy, wd, g, u, name):
    t, d = dy.shape
    f = wd.shape[0]
    tm, tf = _row_tile(t), _ffn_tile(f)

    def body(dy_ref, wd_ref, g_ref, u_ref, dg_ref, du_ref, dwd_ref):
        dy16 = dy_ref[...]
        da = 0.5 * lax.dot_general(dy16, wd_ref[...], NT, preferred_element_type=F32)
        gv = g_ref[...].astype(F32)
        uv = u_ref[...].astype(F32)
        s = _sigmoid(gv)
        silu = gv * s
        dg_ref[...] = (da * uv * (s * (1.0 + gv * (1.0 - s)))).astype(BF16)
        du_ref[...] = (da * silu).astype(BF16)
        part = 0.5 * lax.dot_general((silu * uv).astype(BF16), dy16, (((0,), (0,)), ((), ())),
                                     preferred_element_type=F32)
        _accumulate(dwd_ref, part, pl.program_id(1))

    hid = pl.BlockSpec((tm, tf), lambda j, i: (i, j))
    wrow = pl.BlockSpec((tf, d), lambda j, i: (j, 0))
    return _pallas(
        body, name=name, grid=(f // tf, t // tm),
        in_specs=[pl.BlockSpec((tm, d), lambda j, i: (i, 0)), wrow, hid, hid],
        out_specs=[hid, hid, wrow],
        out_shape=[jax.ShapeDtypeStruct((t, f), BF16)] * 2 + [jax.ShapeDtypeStruct((f, d), F32)],
        sem=("parallel", "arbitrary"), args=(dy, wd, g, u))


def _rel_index(t, s_band):
    return jnp.clip(t + KPAD - s_band, -REL_CLIP, REL_CLIP) + REL_CLIP


def _bias_expand(rel_bias_pad):
    nh = rel_bias_pad.shape[0]

    def body(rb_ref, out_ref):
        rb = rb_ref[...]
        i_io = lax.broadcasted_iota(jnp.int32, (N_REL_PAD, BAND), 0)
        s_io = lax.broadcasted_iota(jnp.int32, (N_REL_PAD, BAND), 1)

        def row(r, carry):
            onehot = (i_io == _rel_index(pl.program_id(0) * rows + r, s_io)).astype(F32)
            out_ref[r] = _dot_exact01_r(rb, onehot)
            return carry

        lax.fori_loop(0, rows, row, 0)

    rows = 8
    return _pallas(
        body, name="bias_expand", grid=(CHUNK // rows,),
        in_specs=[pl.BlockSpec(rel_bias_pad.shape, lambda i: (0, 0))],
        out_specs=[pl.BlockSpec((rows, nh, BAND), lambda i: (i, 0, 0))],
        out_shape=[jax.ShapeDtypeStruct((CHUNK, nh, BAND), F32)], sem=("arbitrary",), args=(rel_bias_pad,))[0]


def _bias_fold(dbias):
    ng, nh = dbias.shape[0], dbias.shape[2]

    def body(db_ref, out_ref):
        s_io = lax.broadcasted_iota(jnp.int32, (BAND, N_REL_PAD), 0)
        i_io = lax.broadcasted_iota(jnp.int32, (BAND, N_REL_PAD), 1)

        def row(t, acc):
            onehot = (i_io == _rel_index(t, s_io)).astype(F32)
            d = db_ref[0, t]
            for gi in range(1, ng):
                d = d + db_ref[gi, t]
            return acc + _dot_exact01_r(d, onehot)

        out_ref[...] = lax.fori_loop(0, CHUNK, row, jnp.zeros((nh, N_REL_PAD), F32))

    return _pallas(
        body, name="bias_fold", grid=(1,), in_specs=[pl.BlockSpec(dbias.shape, lambda i: (0, 0, 0, 0))],
        out_specs=[pl.BlockSpec((nh, N_REL_PAD), lambda i: (0, 0))],
        out_shape=[jax.ShapeDtypeStruct((nh, N_REL_PAD), F32)], sem=("arbitrary",), args=(dbias,))[0]


def _left_half(shape):
    return lax.broadcasted_iota(jnp.int32, shape, len(shape) - 1) < ATTN_HEAD_DIM


def _stack_heads(v):
    left = _left_half(v.shape)
    zero = jnp.zeros_like(v)
    return jnp.concatenate([jnp.where(left, v, zero), jnp.where(left, zero, v)], axis=0)


def _unstack_heads(v):
    return jnp.where(_left_half((CHUNK, 128)), v[0:CHUNK, :], v[CHUNK:2 * CHUNK, :])


def _half_mean(v):
    r = lax.broadcasted_iota(jnp.int32, (128, 128), 0) < ATTN_HEAD_DIM
    c = lax.broadcasted_iota(jnp.int32, (128, 128), 1) < ATTN_HEAD_DIM
    return _dot_exact01_r(v, r == c) * (1.0 / ATTN_HEAD_DIM)


def _attn_prepare(q_ref, k_ref, v_ref, gq_ref, gk_ref, qs_scr, k_scr, v_scr):
    q, k = q_ref[...], k_ref[...]
    rq = lax.rsqrt(_half_mean(q * q) + RMS_EPS)
    rk = lax.rsqrt(_half_mean(k * k) + RMS_EPS)
    qhat, khat = q * rq, k * rk
    qs_scr[...] = (qhat * gq_ref[...] * ATTN_HEAD_DIM ** -0.5).astype(BF16)
    k_scr[0:KPAD, :] = jnp.zeros((KPAD, 128), BF16)
    v_scr[0:KPAD, :] = jnp.zeros((KPAD, 128), BF16)
    k_scr[KPAD:, :] = (khat * gk_ref[...]).astype(BF16)
    v_scr[KPAD:, :] = v_ref[...].astype(BF16)
    return qhat, rq, khat, rk


def _first_key(c):
    return jnp.maximum(CHUNK, (LEFT_CHUNKS + 1 - c) * CHUNK)


def _attn_fwd_chunk(c, qs_scr, k_scr, v_scr, bias_ref, o_ref):
    r0 = pl.multiple_of(c * CHUNK, CHUNK)
    s = lax.dot_general(_stack_heads(qs_scr[pl.ds(r0, CHUNK), :]), k_scr[pl.ds(r0, BAND), :], NT,
                        preferred_element_type=F32)
    yield
    col = lax.broadcasted_iota(jnp.int32, (2 * CHUNK, BAND), 1)
    s = jnp.where(col >= _first_key(c), s + bias_ref[...], -jnp.inf)
    m = jnp.max(s, axis=-1, keepdims=True)
    yield
    e = jnp.exp(s - m)
    yield
    inv = 1.0 / jnp.sum(e, axis=-1, keepdims=True)
    o = lax.dot_general(e.astype(BF16), v_scr[pl.ds(r0, BAND), :], NN, preferred_element_type=F32)
    yield
    o_ref[pl.ds(r0, CHUNK), :] = _unstack_heads(o * inv)


def _attn_bwd(proj, out, dout, bias, gq, gk, nb, seq):
    nc = seq // CHUNK
    lock = min(ATTN_LOCKSTEP, nc)
    assert nc % lock == 0
    scale = ATTN_HEAD_DIM ** -0.5

    def body(q_ref, k_ref, v_ref, o_ref, do_ref, bias_ref, gq_ref, gk_ref,
             dq_ref, dk_ref, dv_ref, dbias_ref, dgq_ref, dgk_ref,
             qs_scr, k_scr, v_scr, dqn_scr, dk_scr, dv_scr, db_scr):
        qhat, rq, khat, rk = _attn_prepare(q_ref, k_ref, v_ref, gq_ref, gk_ref, qs_scr, k_scr, v_scr)
        dk_scr[...] = jnp.zeros_like(dk_scr)
        dv_scr[...] = jnp.zeros_like(dv_scr)
        db_scr[...] = jnp.zeros_like(db_scr)

        def one_chunk(c):
            r0 = pl.multiple_of(c * CHUNK, CHUNK)
            qst = _stack_heads(qs_scr[pl.ds(r0, CHUNK), :])
            kb = k_scr[pl.ds(r0, BAND), :]
            vb = v_scr[pl.ds(r0, BAND), :]
            st = lax.dot_general(kb, qst, NT, preferred_element_type=F32) + bias_ref[...]
            dost = _stack_heads(do_ref[pl.ds(r0, CHUNK), :])
            dost16 = dost.astype(BF16)
            dpt = lax.dot_general(vb, dost16, NT, preferred_element_type=F32)
            yield
            key = lax.broadcasted_iota(jnp.int32, (BAND, 2 * CHUNK), 0)
            st = jnp.where(key >= _first_key(c), st, -jnp.inf)
            mx = jnp.max(st, axis=0, keepdims=True)
            drow = _row_sums_on_lanes(dost * _stack_heads(o_ref[pl.ds(r0, CHUNK), :]))
            yield
            et = jnp.exp(st - mx)
            yield
            pt = et * (1.0 / jnp.sum(et, axis=0, keepdims=True))
            yield
            dst = pt * (dpt - drow)
            dst16 = dst.astype(BF16)
            yield
            db_scr[...] += dst
            dqn_scr[pl.ds(r0, CHUNK), :] = scale * _unstack_heads(_dot(dst.T, kb))
            yield
            dk_scr[pl.ds(r0, BAND), :] += lax.dot_general(dst16, qst, NN, preferred_element_type=F32)
            yield
            dv_scr[pl.ds(r0, BAND), :] += lax.dot_general(pt.astype(BF16), dost16, NN, preferred_element_type=F32)

        def chunk(i, carry):
            _lockstep([one_chunk(i * lock + a) for a in range(lock)])
            return carry

        lax.fori_loop(0, nc // lock, chunk, 0, unroll=max(1, min(ATTN_UNROLL, nc) // lock))

        def norm_bwd(dn, hat, r, g_ref):
            gd = dn * g_ref[...]
            return r * (gd - hat * _half_mean(gd * hat)), jnp.sum(dn * hat, axis=0, keepdims=True)

        dq, dgq = norm_bwd(dqn_scr[...], qhat, rq, gq_ref)
        dk, dgk = norm_bwd(dk_scr[KPAD:, :], khat, rk, gk_ref)
        dq_ref[...] = dq.astype(BF16)
        dk_ref[...] = dk.astype(BF16)
        dv_ref[...] = dv_scr[KPAD:, :].astype(BF16)
        dbias_ref[0] = db_scr[...]
        dgq_ref[0] = dgq
        dgk_ref[0] = dgk

    def col(off):
        return pl.BlockSpec((seq, 128), lambda b, hp: (b, off + hp))

    vec = pl.BlockSpec((1, 128), lambda b, hp: (0, 0))
    gvec = pl.BlockSpec((1, 1, 128), lambda b, hp: (b * (ATTN_HEADS // 2) + hp, 0, 0))
    t = nb * seq
    return _pallas(
        body, name="attn_bwd", grid=(nb, ATTN_HEADS // 2),
        in_specs=[col(0), col(4), col(8), col(0), col(0),
                  pl.BlockSpec((BAND, 2 * CHUNK), lambda b, hp: (hp, 0)), vec, vec],
        out_specs=[col(0), col(0), col(0), pl.BlockSpec((1, BAND, 2 * CHUNK), lambda b, hp: (b, hp, 0)),
                   gvec, gvec],
        out_shape=[jax.ShapeDtypeStruct((t, ATTN_WIDTH), BF16)] * 3
        + [jax.ShapeDtypeStruct((nb, ATTN_HEADS // 2 * BAND, 2 * CHUNK), F32)]
        + [jax.ShapeDtypeStruct((nb * ATTN_HEADS // 2, 1, 128), F32)] * 2,
        scratch_shapes=[pltpu.VMEM((seq, 128), BF16), pltpu.VMEM((seq + KPAD, 128), BF16),
                        pltpu.VMEM((seq + KPAD, 128), BF16), pltpu.VMEM((seq, 128), F32),
                        pltpu.VMEM((seq + KPAD, 128), F32), pltpu.VMEM((seq + KPAD, 128), F32),
                        pltpu.VMEM((BAND, 2 * CHUNK), F32)],
        sem=("parallel", "parallel"), args=(proj, proj, proj, out, dout, bias, gq, gk))


def _tri(lower):
    r = lax.broadcasted_iota(jnp.int32, (CHUNK, CHUNK), 0)
    c = lax.broadcasted_iota(jnp.int32, (CHUNK, CHUNK), 1)
    return (r >= c) if lower else (r <= c)


def _hgrn_gates(hq, hf, lb):
    sq = _sigmoid(hq)
    sf = _sigmoid(hf)
    return hq * sq, sq, sf, lb + (1.0 - lb) * sf


def _hgrn_offdiag(q_s, k_s, b_s):
    row = lax.broadcasted_iota(jnp.int32, (CHUNK, HGRN_HEAD_DIM), 0)
    bv, qv, kv = b_s[...], q_s[...], k_s[...]
    eqs, eks = [], []
    for i in range(1, N_SUB):
        r = b_s[pl.ds(SUB * i - 1, 1), :]
        in_i = (row >= SUB * i) & (row < SUB * (i + 1))
        eqs.append(jnp.exp(jnp.where(in_i, bv - r, -jnp.inf)))
        eks.append(jnp.exp(jnp.where(row < SUB * i, r - bv, -jnp.inf)))
    eq = jnp.concatenate(eqs, axis=1)
    ek = jnp.concatenate(eks, axis=1)
    qt = jnp.concatenate([qv] * (N_SUB - 1), axis=1) * eq
    kt = jnp.concatenate([kv] * (N_SUB - 1), axis=1) * ek
    return qt, kt, eq, ek


def _hgrn_diag_e(b_s, i, s):
    t_io = lax.broadcasted_iota(jnp.int32, (SUB, HGRN_HEAD_DIM), 0)
    bi = b_s[pl.ds(SUB * i, SUB), :]
    return jnp.exp(jnp.where(t_io >= s, bi - b_s[pl.ds(SUB * i + s, 1), :], -jnp.inf)), t_io


def _hgrn_intra(q_s, k_s, b_s, a_s, qt, kt):
    ktp = jnp.concatenate([kt, jnp.zeros_like(kt)], axis=0)
    a_s[...] = _dot(qt, ktp, NT)
    yield
    col = lax.broadcasted_iota(jnp.int32, (SUB, HGRN_HEAD_DIM), 1)
    for i in range(N_SUB):
        qi = q_s[pl.ds(SUB * i, SUB), :]
        ai = jnp.zeros((SUB, HGRN_HEAD_DIM), F32)
        for s in range(SUB):
            e, _ = _hgrn_diag_e(b_s, i, s)
            a_col = jnp.sum(qi * k_s[pl.ds(SUB * i + s, 1), :] * e, axis=-1, keepdims=True)
            ai = ai + jnp.where(col == SUB * i + s, a_col, 0.0)
            if s % DIAG_STAGE == DIAG_STAGE - 1:
                yield
        a_s[pl.ds(SUB * i, SUB), :] += ai


def _mixer_fwd(proj, bias, gq, gk, lb, go, nb, seq):
    nc = seq // CHUNK
    hd = HGRN_HEAD_DIM
    nblk = ATTN_HEADS // 2
    rows_blk = seq // nblk
    nck = rows_blk // CHUNK
    per = nc // nck
    assert rows_blk % CHUNK == 0

    def body(aq_ref, ak_ref, av_ref, bias_ref, gq_ref, gk_ref, hq_ref, hf_ref, hi_ref, hg_ref, lb_ref, go_ref,
             ao_ref, y_ref, o_ref, st_ref, a_ref, qs_scr, k_scr, v_scr, st_all, q_all, k_all, b_all, a_all):
        _attn_prepare(aq_ref, ak_ref, av_ref, gq_ref, gk_ref, qs_scr, k_scr, v_scr)

        @pl.when(pl.program_id(1) == 0)
        def _():
            st_all[...] = jnp.zeros_like(st_all)

        lower = _tri(True)

        def head_chunk(hh, c, rows):
            ln = slice(hd * hh, hd * (hh + 1))
            st, q_s, k_s, b_s, a_s = st_all.at[hh], q_all.at[hh], k_all.at[hh], b_all.at[hh], a_all.at[hh]
            q, _, _, f = _hgrn_gates(hq_ref[rows, ln], hf_ref[rows, ln], lb_ref[:, ln])
            v = hi_ref[rows, ln]
            yield
            b = _dot_exact01(lower, jnp.log(f))
            q_s[...] = q
            k_s[...] = 1.0 - f
            b_s[...] = b
            st_ref[hh, c] = st[...]
            yield
            qt, kt, _, _ = _hgrn_offdiag(q_s, k_s, b_s)
            yield
            yield from _hgrn_intra(q_s, k_s, b_s, a_s, qt, kt)
            a16 = a_s[...].astype(BF16)
            a_ref[hh, c] = a16
            vp = jnp.concatenate([v, jnp.zeros_like(v)], axis=0)
            o = _dot(a16, vp) + _dot(q * jnp.exp(b), st[...], NT)
            yield
            bl = b_s[pl.ds(CHUNK - 1, 1), :]
            st[...] = st[...] * jnp.exp(bl) + _tn(v, (1.0 - f) * jnp.exp(bl - b))
            o_ref[rows, ln] = o
            yield
            n = o * lax.rsqrt(jnp.mean(o * o, axis=-1, keepdims=True) + RMS_EPS) * go_ref[...]
            hg = hg_ref[rows, ln]
            y_ref[rows, ln] = n * hg * _sigmoid(hg)

        def chunk(c, carry):
            rows = pl.ds(pl.multiple_of(c * CHUNK, CHUNK), CHUNK)
            _lockstep([_attn_fwd_chunk(c * per + a, qs_scr, k_scr, v_scr, bias_ref, ao_ref) for a in range(per)]
                      + [head_chunk(hh, c, rows) for hh in range(HGRN_HEADS)])
            return carry

        lax.fori_loop(0, nck, chunk, 0, unroll=min(4, nck))

    hp, wide = HGRN_HEADS, HGRN_HEADS * hd

    def acol(off):
        return pl.BlockSpec((seq, 128), lambda b, s: (b, off + s))

    def col(off):
        return pl.BlockSpec((rows_blk, wide), lambda b, s: (b * nblk + s, off // hp))

    out = pl.BlockSpec((rows_blk, wide), lambda b, s: (b * nblk + s, 0))
    vec = pl.BlockSpec((1, 128), lambda b, s: (0, 0))
    t = nb * seq
    return _pallas(
        body, name="mixer_fwd", grid=(nb, nblk),
        in_specs=[acol(0), acol(4), acol(8), pl.BlockSpec((2 * CHUNK, BAND), lambda b, s: (s, 0)), vec, vec,
                  col(12), col(16), col(20), col(24), pl.BlockSpec((1, wide), lambda b, s: (0, 0)), vec],
        out_specs=[pl.BlockSpec((seq, 128), lambda b, s: (b, s)), out, out,
                   pl.BlockSpec((hp, nck, hd, hd), lambda b, s: (b, s, 0, 0)),
                   pl.BlockSpec((hp, nck, CHUNK, hd), lambda b, s: (b, s, 0, 0))],
        out_shape=[jax.ShapeDtypeStruct((t, ATTN_WIDTH), F32)] + [jax.ShapeDtypeStruct((t, wide), F32)] * 2
        + [jax.ShapeDtypeStruct((nb * hp, nc, hd, hd), F32), jax.ShapeDtypeStruct((nb * hp, nc, CHUNK, hd), BF16)],
        scratch_shapes=[pltpu.VMEM((seq, 128), BF16), pltpu.VMEM((seq + KPAD, 128), BF16),
                        pltpu.VMEM((seq + KPAD, 128), BF16), pltpu.VMEM((hp, hd, hd), F32)]
        + [pltpu.VMEM((hp, CHUNK, hd), F32)] * 4,
        sem=("parallel", "arbitrary"), args=(proj,) * 3 + (bias, gq, gk) + (proj,) * 4 + (lb, go))


def _hgrn_bwd(proj, lb, go, o_pre, states, scores, dout, nb, seq):
    nc = seq // CHUNK
    hd = HGRN_HEAD_DIM
    rows_blk = min(HGRN_ROWS, seq)
    nblk, nck = seq // rows_blk, rows_blk // CHUNK

    def body(hq_ref, hf_ref, hi_ref, hg_ref, lb_ref, go_ref, o_ref, st_ref, a_ref, dy_ref,
             dhq_ref, dhf_ref, dhi_ref, dhg_ref, dlb_ref, dgo_ref,
             dst_all, q_all, k_all, b_all, da_all, dqi_all, dki_all, dlb_all, dgo_all):
        @pl.when(pl.program_id(1) == 0)
        def _():
            dst_all[...] = jnp.zeros_like(dst_all)
            dlb_all[...] = jnp.zeros_like(dlb_all)
            dgo_all[...] = jnp.zeros_like(dgo_all)

        lower, upper = _tri(True), _tri(False)
        gov = go_ref[...]
        row = lax.broadcasted_iota(jnp.int32, (CHUNK, hd), 0)

        def head_chunk(hh, c, rows):
            ln = slice(hd * hh, hd * (hh + 1))
            dst, q_s, k_s, b_s = dst_all.at[hh], q_all.at[hh], k_all.at[hh], b_all.at[hh]
            da_s, dqi_s, dki_s = da_all.at[hh], dqi_all.at[hh], dki_all.at[hh]
            dlb_acc, dgo_acc = dlb_all.at[hh], dgo_all.at[hh]
            lbv = lb_ref[:, ln]
            hq, hf, v, hg = hq_ref[rows, ln], hf_ref[rows, ln], hi_ref[rows, ln], hg_ref[rows, ln]
            q, sq, sf, f = _hgrn_gates(hq, hf, lbv)
            kk = 1.0 - f
            yield
            b = _dot_exact01(lower, jnp.log(f))
            q_s[...] = q
            k_s[...] = kk
            b_s[...] = b
            yield
            bl = b_s[pl.ds(CHUNK - 1, 1), :]
            ebl = jnp.exp(bl)
            ekd = jnp.exp(bl - b)
            kd = kk * ekd
            eb = jnp.exp(b)
            qb = q * eb
            st0 = st_ref[hh, c]
            dst1 = dst[...]
            yield

            o = o_ref[rows, ln]
            dy = dy_ref[rows, ln]
            sg = _sigmoid(hg)
            rstd = lax.rsqrt(jnp.mean(o * o, axis=-1, keepdims=True) + RMS_EPS)
            ohat = o * rstd
            dn = dy * hg * sg
            dhg_ref[rows, ln] = (dy * ohat * gov * (sg * (1.0 + hg * (1.0 - sg)))).astype(BF16)
            dgo_acc[...] += jnp.sum(dn * ohat, axis=0, keepdims=True)
            gdn = dn * gov
            do = rstd * (gdn - ohat * jnp.mean(gdn * ohat, axis=-1, keepdims=True))
            yield

            qt, kt, eq, ek = _hgrn_offdiag(q_s, k_s, b_s)
            da = _dot(do, v, NT)
            dat = _dot(v, do, NT)
            da_s[...] = da
            yield
            dqo = _dot(da, kt) * eq
            dko = _dot(dat, qt) * ek
            dqi_s[...] = sum(dqo[:, j * hd:(j + 1) * hd] for j in range(N_SUB - 1))
            dki_s[...] = sum(dko[:, j * hd:(j + 1) * hd] for j in range(N_SUB - 1))
            yield
            col = lax.broadcasted_iota(jnp.int32, (SUB, CHUNK), 1)
            for i in range(N_SUB):
                qi = q_s[pl.ds(SUB * i, SUB), :]
                dai = da_s[pl.ds(SUB * i, SUB), :]
                dqd = jnp.zeros((SUB, hd), F32)
                for s in range(SUB):
                    e, _ = _hgrn_diag_e(b_s, i, s)
                    dacol = jnp.sum(jnp.where(col == SUB * i + s, dai, 0.0), axis=-1, keepdims=True)
                    w = dacol * e
                    dqd = dqd + w * k_s[pl.ds(SUB * i + s, 1), :]
                    dki_s[pl.ds(SUB * i + s, 1), :] += jnp.sum(w * qi, axis=0, keepdims=True)
                    if s % DIAG_STAGE == DIAG_STAGE - 1:
                        yield
                dqi_s[pl.ds(SUB * i, SUB), :] += dqd
            dqi, dki = dqi_s[...], dki_s[...]

            dv = _tn(a_ref[hh, c].astype(F32), do)[0:CHUNK, :] + _dot(kd, dst1, NT)
            dqb = _dot(do, st0)
            dkd = _dot(v, dst1)
            yield
            t2 = dkd * kd
            dq = dqb * eb + dqi
            dk = dkd * ekd + dki
            dbl = jnp.sum(t2, axis=0, keepdims=True) + ebl * jnp.sum(st0 * dst1, axis=0, keepdims=True)
            db = dqb * qb - t2 + q * dqi - kk * dki + jnp.where(row == CHUNK - 1, dbl, 0.0)
            yield
            dg = _dot_exact01(upper, db)
            dst[...] = dst1 * ebl + _tn(do, qb)
            yield

            df = dg / f - dk
            dhf_ref[rows, ln] = (df * (1.0 - lbv) * sf * (1.0 - sf)).astype(BF16)
            dlb_acc[...] += jnp.sum(df * (1.0 - sf), axis=0, keepdims=True)
            dhq_ref[rows, ln] = (dq * (sq * (1.0 + hq * (1.0 - sq)))).astype(BF16)
            dhi_ref[rows, ln] = dv.astype(BF16)

        def chunk(it, carry):
            c = nck - 1 - it
            rows = pl.ds(pl.multiple_of(c * CHUNK, CHUNK), CHUNK)
            _lockstep([head_chunk(hh, c, rows) for hh in range(HGRN_HEADS)])
            return carry

        lax.fori_loop(0, nck, chunk, 0, unroll=min(2, nck))

        @pl.when(pl.program_id(1) == nblk - 1)
        def _():
            dlb_ref[...] = dlb_all[...]
            dgo_ref[...] = dgo_all[...]

    hp, wide = HGRN_HEADS, HGRN_HEADS * hd

    def col(off):
        return pl.BlockSpec((rows_blk, wide), lambda b, s: (b * nblk + nblk - 1 - s, off // hp))

    out = pl.BlockSpec((rows_blk, wide), lambda b, s: (b * nblk + nblk - 1 - s, 0))
    part = pl.BlockSpec((hp, 1, hd), lambda b, s: (b, 0, 0))
    t = nb * seq
    return pl.pallas_call(
        body, name="hgrn_bwd", grid=(nb, nblk),
        in_specs=[col(12), col(16), col(20), col(24), pl.BlockSpec((1, wide), lambda b, s: (0, 0)),
                  pl.BlockSpec((1, hd), lambda b, s: (0, 0)), out,
                  pl.BlockSpec((hp, nck, hd, hd), lambda b, s: (b, nblk - 1 - s, 0, 0)),
                  pl.BlockSpec((hp, nck, CHUNK, hd), lambda b, s: (b, nblk - 1 - s, 0, 0)), col(4)],
        out_specs=[out, out, out, out, part, part],
        out_shape=[jax.ShapeDtypeStruct((t, wide), BF16)] * 4 + [jax.ShapeDtypeStruct((nb * hp, 1, hd), F32)] * 2,
        scratch_shapes=[pltpu.VMEM((hp, hd, hd), F32)] + [pltpu.VMEM((hp, CHUNK, hd), F32)] * 3
        + [pltpu.VMEM((hp, CHUNK, CHUNK), F32)] + [pltpu.VMEM((hp, CHUNK, hd), F32)] * 2
        + [pltpu.VMEM((hp, 1, hd), F32)] * 2,
        compiler_params=_params("parallel", "arbitrary"),
    )(proj, proj, proj, proj, lb, go, o_pre, states, scores, dout)


def _lb_fwd(lower_bounds):
    def body(x_ref, o_ref):
        xv = x_ref[...]
        e = jnp.exp(xv - jnp.max(xv, axis=0, keepdims=True))
        o_ref[...] = e[0:1, :] / jnp.sum(e, axis=0, keepdims=True)

    return pl.pallas_call(body, name="lb_fwd",
                          out_shape=jax.ShapeDtypeStruct((1, lower_bounds.shape[1]), F32))(lower_bounds)


def _lb_bwd(lower_bounds, dlb_parts):
    ng = dlb_parts.shape[0]

    def body(x_ref, d_ref, o_ref):
        xv = x_ref[...]
        e = jnp.exp(xv - jnp.max(xv, axis=0, keepdims=True))
        p = e / jnp.sum(e, axis=0, keepdims=True)
        dlb = d_ref[0]
        for gi in range(1, ng):
            dlb = dlb + d_ref[gi]
        first = lax.broadcasted_iota(jnp.int32, xv.shape, 0) == 0
        o_ref[...] = p * (jnp.where(first, dlb, 0.0) - p[0:1, :] * dlb)

    return pl.pallas_call(body, name="lb_bwd",
                          out_shape=jax.ShapeDtypeStruct(lower_bounds.shape, F32))(lower_bounds, dlb_parts)


def _ffn_bwd(x, g, h, gate, up, dy, dy16, w, put, tag):
    wg, wu, wd = w[tag + "_w_gate"], w[tag + "_w_up"], w[tag + "_w_down"]
    dgate, dup, dwd = _ffn_bwd_mid(dy16, wd, gate, up, tag + "_bwd_mid")
    put(tag + "_w_down", dwd)
    put(tag + "_w_gate", _mm(dgate, h, ta=True, tm=1408, tn=512, name=tag + "_dwg"))
    put(tag + "_w_up", _mm(dup, h, ta=True, tm=1408, tn=512, name=tag + "_dwu"))
    dh = _mm(dgate, wg, tm=512, tn=1024, name=tag + "_dh_gate")
    return _mm(dup, wu, tm=512, tn=1024, add=dh, norm_bwd=(x, g, dy), name=tag + "_dh_up")


def _local_step(x, tgt, sp, w, put, nb, seq):
    d = x.shape[1]
    h1 = _rms_fwd(x, sp["ffn1_norm_g"], "ffn1_norm")
    rb_pad = jnp.pad(sp["attn_rel_bias"], ((0, 0), (0, N_REL_PAD - N_REL)))
    bias = jnp.transpose(_bias_expand(rb_pad), (1, 0, 2)).reshape(ATTN_HEADS * CHUNK, BAND)
    gq2 = jnp.concatenate([sp["attn_q_norm_g"]] * 2, axis=1)
    gk2 = jnp.concatenate([sp["attn_k_norm_g"]] * 2, axis=1)
    lb = _lb_fwd(sp["hgrn_lower_bounds"])
    gate1, up1, act1 = _ffn_up(h1, w["ffn1_w_gate"], w["ffn1_w_up"], "ffn1_up")
    x1, h2 = _mm(act1, w["ffn1_w_down"], tm=512, tn=d, add=x, scale=0.5, norm_g=sp["mix_norm_g"],
                 name="ffn1_down")
    proj = _mm(h2, w["w_in"], tb=True, tm=256, tn=w["w_in"].shape[0], name="in_proj")
    attn, hy, ho, hstate, hscore = _mixer_fwd(proj, bias, gq2, gk2, lb, sp["hgrn_out_norm_g"], nb, seq)
    mix = jnp.concatenate([attn, hy], axis=1)
    x2, h3 = _mm(mix, w["w_out"], tm=512, tn=1024, add=x1, norm_g=sp["ffn2_norm_g"], name="out_proj")
    gate2, up2, dx3, dx3_16, sq = _ffn_fwd(h3, x2, w["ffn2_w_gate"], w["ffn2_w_up"], w["ffn2_w_down"], "ffn2_fwd",
                                           tgt=tgt)
    loss = 0.5 * jnp.sum(sq) / d

    dx2, dx2_16, dg3 = _ffn_bwd(x2, sp["ffn2_norm_g"], h3, gate2, up2, dx3, dx3_16, w, put, "ffn2")
    dmix = _mm(dx2_16, w["w_out"], tb=True, tm=512, tn=1024, name="out_proj_dx")
    put("w_out", _mm(mix, dx2_16, ta=True, tm=512, tn=1024, name="out_proj_dw"))
    bias_t = jnp.transpose(bias.reshape(ATTN_HEADS // 2, 2 * CHUNK, BAND), (0, 2, 1)).reshape(-1, 2 * CHUNK)
    dq, dk, dv, dbias, dgq, dgk = _attn_bwd(proj, attn, dmix, bias_t, gq2, gk2, nb, seq)
    dbias = jnp.transpose(dbias.reshape(nb, ATTN_HEADS // 2, BAND, 2, CHUNK), (0, 4, 1, 3, 2))
    dbias = dbias.reshape(nb, CHUNK, ATTN_HEADS, BAND)
    dgq = jnp.sum(dgq, axis=(0, 1)).reshape(2, ATTN_HEAD_DIM).sum(axis=0, keepdims=True)
    dgk = jnp.sum(dgk, axis=(0, 1)).reshape(2, ATTN_HEAD_DIM).sum(axis=0, keepdims=True)
    dhq, dhf, dhi, dhg, dlb, dgo = _hgrn_bwd(proj, lb, sp["hgrn_out_norm_g"], ho, hstate, hscore, dmix, nb, seq)
    dproj = jnp.concatenate([dq, dk, dv, dhq, dhf, dhi, dhg], axis=1)
    put("w_in", _mm(dproj, h2, ta=True, tm=512, tn=1024, name="in_proj_dw"))
    dx1, dx1_16, dgm = _mm(dproj, w["w_in"], tm=512, tn=1024, norm_bwd=(x1, sp["mix_norm_g"], dx2),
                           name="in_proj_dx")
    dx0, _, dg1 = _ffn_bwd(x, sp["ffn1_norm_g"], h1, gate1, up1, dx1, dx1_16, w, put, "ffn1")

    small = {
        "ffn1_norm_g": dg1, "mix_norm_g": dgm, "ffn2_norm_g": dg3,
        "attn_q_norm_g": dgq, "attn_k_norm_g": dgk,
        "attn_rel_bias": _bias_fold(dbias)[:, :N_REL],
        "hgrn_lower_bounds": _lb_bwd(sp["hgrn_lower_bounds"], dlb.reshape(nb, 1, HGRN_HEADS * HGRN_HEAD_DIM)),
        "hgrn_out_norm_g": jnp.sum(dgo, axis=(0, 1))[None, :],
    }
    return loss, dx0, small


MESH = pl.DeviceIdType.MESH
ANY = pl.BlockSpec(memory_space=pl.ANY)


def _coords():
    return lax.axis_index("x"), lax.axis_index("y"), lax.axis_index("c")


def _other_chips(x, y):
    return [(1 - x, y), (x, 1 - y), (1 - x, 1 - y)]


def _gather_side(shards):
    n = len(shards)

    def copies(ins, outs, sems):
        send_sems, recv_sems, local_sems = sems
        x, y, c = _coords()
        xn, yn, dg = (1 - x, y), (x, 1 - y), (1 - x, 1 - y)

        def copy(i, k, block, to, half=None, src=None):
            bx, by, bc = block
            dst = outs[i].at[4 * bx + 2 * by + bc]
            if half is not None:
                rows = shards[i].shape[0] // 2
                dst = dst.at[pl.ds(half * rows, rows)]
            return pltpu.make_async_remote_copy(
                src_ref=dst if src is None else src, dst_ref=dst, send_sem=send_sems.at[i, k],
                recv_sem=recv_sems.at[i, k], device_id=to, device_id_type=MESH)

        mine = [pltpu.make_async_copy(ins[i], outs[i].at[4 * x + 2 * y + c], local_sems.at[i]) for i in range(n)]
        return copy, mine, (x, y, c), (x, y, 1 - c), xn, yn, dg, c

    def own(copy, i, ins, me, sibling, xn, yn, c):
        return [copy(i, 0, me, sibling, src=ins[i]), copy(i, 1, me, (*xn, c), src=ins[i]),
                copy(i, 2, me, (*yn, c), src=ins[i])]

    def passed_on(copy, i, sibling, xn, yn, c):
        return [copy(i, 3, (*xn, c), sibling), copy(i, 5, (*xn, c), (*yn, c), half=0),
                copy(i, 4, (*yn, c), sibling), copy(i, 6, (*yn, c), (*xn, c), half=1)]

    def diagonal(copy, i, sibling, dg, c):
        return [copy(i, 7, (*dg, c), sibling, half=0), copy(i, 8, (*dg, c), sibling, half=1)]

    def start(ins, outs, sems):
        copy, mine, me, sibling, xn, yn, dg, c = copies(ins, outs, sems)
        for cp in mine + [cp for i in range(n) for cp in own(copy, i, ins, me, sibling, xn, yn, c)]:
            cp.start()

    def middle(ins, outs, sems):
        copy, mine, me, sibling, xn, yn, dg, c = copies(ins, outs, sems)
        for i in range(n):
            fwd_x, relay_x, fwd_y, relay_y = passed_on(copy, i, sibling, xn, yn, c)
            copy(i, 1, (*xn, c), me).wait_recv()
            fwd_x.start()
            relay_x.start()
            copy(i, 2, (*yn, c), me).wait_recv()
            fwd_y.start()
            relay_y.start()

    def finish(ins, outs, sems):
        copy, mine, me, sibling, xn, yn, dg, c = copies(ins, outs, sems)
        for i in range(n):
            top, bottom = diagonal(copy, i, sibling, dg, c)
            copy(i, 5, (*dg, c), me, half=0).wait_recv()
            top.start()
            copy(i, 6, (*dg, c), me, half=1).wait_recv()
            bottom.start()
        for i in range(n):
            copy(i, 0, sibling, me).wait_recv()
            copy(i, 3, (*xn, 1 - c), me).wait_recv()
            copy(i, 4, (*yn, 1 - c), me).wait_recv()
            copy(i, 7, (*dg, 1 - c), me, half=0).wait_recv()
            copy(i, 8, (*dg, 1 - c), me, half=1).wait_recv()
        for i in range(n):
            for cp in (own(copy, i, ins, me, sibling, xn, yn, c) + passed_on(copy, i, sibling, xn, yn, c)
                       + diagonal(copy, i, sibling, dg, c)):
                cp.wait_send()
        for cp in mine:
            cp.wait()

    return _Side(list(shards), [jax.ShapeDtypeStruct((N_DEV,) + s.shape, s.dtype) for s in shards],
                 [pltpu.SemaphoreType.DMA((n, 9)), pltpu.SemaphoreType.DMA((n, 9)), pltpu.SemaphoreType.DMA((n,))],
                 start, finish, middle)


def _pair_side(grads):
    n = len(grads)

    def copies(ins, outs, sems):
        send_sems, recv_sems = sems
        x, y, c = _coords()
        return [pltpu.make_async_remote_copy(
            src_ref=ins[i].at[2 * k + 1 - c], dst_ref=outs[i].at[k], send_sem=send_sems.at[i, k],
            recv_sem=recv_sems.at[i, k], device_id=(x, y, 1 - c), device_id_type=MESH)
            for i in range(n) for k in range(4)]

    def start(ins, outs, sems):
        for cp in copies(ins, outs, sems):
            cp.start()

    def finish(ins, outs, sems):
        for cp in copies(ins, outs, sems):
            cp.wait()

    return _Side(list(grads), [jax.ShapeDtypeStruct((4,) + g.shape[1:], g.dtype) for g in grads],
                 [pltpu.SemaphoreType.DMA((n, 4)), pltpu.SemaphoreType.DMA((n, 4))], start, finish)


def _pair_add(grads, recvs, core, name):
    count = len(grads)

    def body(c_ref, *refs):
        for n in range(count):
            refs[2 * count + n][...] = (refs[2 * n][...] + refs[2 * n + 1][...]).astype(BF16)

    in_specs, out_specs = [], []
    for g in grads:
        blk = (1,) + g.shape[1:]
        in_specs += [pl.BlockSpec(blk, lambda k, c_ref: (2 * k + c_ref[0], 0, 0)),
                     pl.BlockSpec(blk, lambda k, c_ref: (k, 0, 0))]
        out_specs.append(pl.BlockSpec(blk, lambda k, c_ref: (k, 0, 0)))
    out = pl.pallas_call(
        body, name=name,
        grid_spec=pltpu.PrefetchScalarGridSpec(num_scalar_prefetch=1, grid=(4,), in_specs=in_specs,
                                               out_specs=out_specs),
        out_shape=[jax.ShapeDtypeStruct((4,) + g.shape[1:], BF16) for g in grads],
        compiler_params=_params("arbitrary"),
    )(core, *[a for pair in zip(grads, recvs) for a in pair])
    return list(out)


def _chip_side(parts):
    n = len(parts)

    def copies(ins, outs, sems):
        send_sems, recv_sems, local_sems = sems
        x, y, c = _coords()
        chips = _other_chips(x, y)
        mine = [pltpu.make_async_copy(ins[i].at[2 * x + y], outs[i].at[2 * x + y], local_sems.at[i])
                for i in range(n)]
        sent = [pltpu.make_async_remote_copy(
            src_ref=ins[i].at[2 * px + py], dst_ref=outs[i].at[2 * x + y], send_sem=send_sems.at[i, j],
            recv_sem=recv_sems.at[i, j], device_id=(px, py, c), device_id_type=MESH)
            for i in range(n) for j, (px, py) in enumerate(chips)]
        return mine, sent, chips, c

    def start(ins, outs, sems):
        mine, sent, _, _ = copies(ins, outs, sems)
        for cp in mine + sent:
            cp.start()

    def finish(ins, outs, sems):
        mine, sent, chips, c = copies(ins, outs, sems)
        send_sems, recv_sems, _ = sems
        for i in range(n):
            for j, (px, py) in enumerate(chips):
                landed = outs[i].at[2 * px + py]
                pltpu.make_async_remote_copy(
                    src_ref=landed, dst_ref=landed, send_sem=send_sems.at[i, j], recv_sem=recv_sems.at[i, j],
                    device_id=(px, py, c), device_id_type=MESH).wait_recv()
        for cp in sent:
            cp.wait_send()
        for cp in mine:
            cp.wait()

    return _Side(list(parts), [jax.ShapeDtypeStruct(p.shape, p.dtype) for p in parts],
                 [pltpu.SemaphoreType.DMA((n, 3)), pltpu.SemaphoreType.DMA((n, 3)), pltpu.SemaphoreType.DMA((n,))],
                 start, finish)


def _all_reduce_small(vals):
    n = len(vals)

    def body(*refs):
        ins, outs, bufs = refs[:n], refs[n:2 * n], refs[2 * n:3 * n]
        send_sems, recv_sems = refs[3 * n:]
        x, y, c = _coords()
        me = 4 * x + 2 * y + c
        for i in range(n):
            bufs[i][me] = ins[i][...]
        sent, landed = [], []
        for k in range(1, N_DEV):
            px = 1 - x if k & 4 else x
            py = 1 - y if k & 2 else y
            pc = 1 - c if k & 1 else c
            for i in range(n):
                sent.append(pltpu.make_async_remote_copy(
                    src_ref=ins[i], dst_ref=bufs[i].at[me], send_sem=send_sems.at[i, k - 1],
                    recv_sem=recv_sems.at[i, k - 1], device_id=(px, py, pc), device_id_type=MESH))
                landed.append(pltpu.make_async_remote_copy(
                    src_ref=ins[i], dst_ref=bufs[i].at[4 * px + 2 * py + pc], send_sem=send_sems.at[i, k - 1],
                    recv_sem=recv_sems.at[i, k - 1], device_id=(x, y, c), device_id_type=MESH))
        for cp in sent:
            cp.start()
        for cp in landed:
            cp.wait_recv()
        for cp in sent:
            cp.wait_send()
        for i in range(n):
            acc = bufs[i][0]
            for j in range(1, N_DEV):
                acc = acc + bufs[i][j]
            outs[i][...] = acc

    vmem = pl.BlockSpec(memory_space=pltpu.VMEM)
    return pl.pallas_call(
        body, name="small_all_reduce", out_shape=[jax.ShapeDtypeStruct(v.shape, F32) for v in vals],
        in_specs=[vmem] * n, out_specs=[vmem] * n,
        scratch_shapes=[pltpu.VMEM((N_DEV,) + v.shape, F32) for v in vals]
        + [pltpu.SemaphoreType.DMA((n, N_DEV - 1)), pltpu.SemaphoreType.DMA((n, N_DEV - 1))],
    )(*vals)


def _adamw(ws, ms, vs, gs, name):
    count = len(ws)
    parts = ws[0].ndim == 3
    steps = 4 if all(w.shape[-2] % 32 == 0 for w in ws) else 1

    def body(*refs):
        for n in range(count):
            w_ref, m_ref, v_ref, g_ref = refs[4 * n:4 * n + 4]
            go_ref, d_ref, mo_ref, vo_ref = refs[4 * count + 4 * n:4 * count + 4 * n + 4]
            if parts:
                gv = g_ref[0].astype(F32)
                for k in range(1, 4):
                    gv = gv + g_ref[k].astype(F32)
                gv = gv[None]
            else:
                gv = g_ref[...]
            m2 = ADAM_B1 * m_ref[...] + (1.0 - ADAM_B1) * gv
            v2 = ADAM_B2 * v_ref[...] + (1.0 - ADAM_B2) * (gv * gv)
            m_hat = m2 / (1.0 - ADAM_B1 ** ADAM_STEP)
            v_hat = v2 / (1.0 - ADAM_B2 ** ADAM_STEP)
            go_ref[...] = gv
            d_ref[...] = -ADAM_LR * (m_hat / (jnp.sqrt(v_hat) + ADAM_EPS) + ADAM_WD * w_ref[...])
            mo_ref[...] = m2
            vo_ref[...] = v2

    in_specs, out_specs, out_shape = [], [], []
    for w in ws:
        r, cdim = w.shape[-2:]
        if parts:
            row = pl.BlockSpec((1, r // steps, cdim), lambda i: (0, i, 0))
            g_spec = pl.BlockSpec((4, r // steps, cdim), lambda i: (0, i, 0))
        else:
            row = g_spec = pl.BlockSpec((r // steps, cdim), lambda i: (i, 0))
        in_specs += [row, row, row, g_spec]
        out_specs += [row] * 4
        out_shape += [jax.ShapeDtypeStruct(w.shape, F32)] * 4
    args = [a for group in zip(ws, ms, vs, gs) for a in group]
    out = pl.pallas_call(
        body, name=name, grid=(steps,), in_specs=in_specs, out_specs=out_specs, out_shape=out_shape,
        compiler_params=_params("parallel"),
    )(*args)
    return [out[4 * n:4 * n + 4] for n in range(count)]


def _adamw_small(ws, ms, vs, gs):
    count = len(ws)

    def body(*refs):
        for n in range(count):
            w_ref, m_ref, v_ref, g_ref = refs[4 * n:4 * n + 4]
            d_ref, mo_ref, vo_ref = refs[4 * count + 3 * n:4 * count + 3 * n + 3]
            gv = g_ref[...]
            m2 = ADAM_B1 * m_ref[...] + (1.0 - ADAM_B1) * gv
            v2 = ADAM_B2 * v_ref[...] + (1.0 - ADAM_B2) * (gv * gv)
            m_hat = m2 / (1.0 - ADAM_B1 ** ADAM_STEP)
            v_hat = v2 / (1.0 - ADAM_B2 ** ADAM_STEP)
            d_ref[...] = -ADAM_LR * (m_hat / (jnp.sqrt(v_hat) + ADAM_EPS) + ADAM_WD * w_ref[...])
            mo_ref[...] = m2
            vo_ref[...] = v2

    out = pl.pallas_call(
        body, name="small_adamw",
        out_shape=[jax.ShapeDtypeStruct(w.shape, F32) for w in ws for _ in range(3)],
    )(*[a for group in zip(ws, ms, vs, gs) for a in group])
    return [out[3 * n:3 * n + 3] for n in range(count)]


WEIGHTS = ["ffn1_norm_g", "ffn1_w_gate", "ffn1_w_up", "ffn1_w_down", "mix_norm_g", "w_in", "attn_q_norm_g",
           "attn_k_norm_g", "attn_rel_bias", "hgrn_lower_bounds", "hgrn_out_norm_g", "w_out", "ffn2_norm_g",
           "ffn2_w_gate", "ffn2_w_up", "ffn2_w_down"]
COL_SHARDED = ("ffn1_w_gate", "ffn1_w_up", "w_in", "ffn2_w_gate", "ffn2_w_up")
ROW_SHARDED = ("ffn1_w_down", "w_out", "ffn2_w_down")
BIG = [n for n in WEIGHTS if n in COL_SHARDED or n in ROW_SHARDED]
SMALL = [n for n in WEIGHTS if n not in BIG]
FFN2 = ["ffn2_w_down", "ffn2_w_gate", "ffn2_w_up"]
MIXER = ["w_out", "w_in"]

PLAN = {
    "ffn1_norm": [("gather", ["ffn1_w_gate"])],
    "bias_expand": [("gather", ["ffn1_w_up"])],
    "ffn1_up": [("gather", ["ffn1_w_down", "w_out"])],
    "ffn1_down": [("gather", ["w_in"])],
    "mixer_fwd": [("gather", FFN2)],
    "ffn2_dh_gate": [("pair", FFN2)],
    "attn_bwd": [("chip", FFN2)],
    "in_proj_dx": [("pair", MIXER)],
    "ffn1_bwd_mid": [("chip", MIXER)],
    "ffn1_dwg": [("pair", ["ffn1_w_down"])],
    "ffn1_dwu": [("chip", ["ffn1_w_down"]), ("pair", ["ffn1_w_gate"])],
    "ffn1_dh_gate": [("chip", ["ffn1_w_gate"]), ("pair", ["ffn1_w_up"])],
    "bias_fold": [("chip", ["ffn1_w_up"])],
}


def _join_sides(sides):
    def split(refs, counts):
        out, at = [], 0
        for n in counts:
            out.append(refs[at:at + n])
            at += n
        return out

    n_in, n_out, n_sem = ([len(getattr(s, f)) for s in sides] for f in ("ins", "out_shape", "sems"))

    def run(which):
        def go(ins, outs, sems):
            for s, i, o, m in zip(sides, split(ins, n_in), split(outs, n_out), split(sems, n_sem)):
                if getattr(s, which) is not None:
                    getattr(s, which)(i, o, m)
        return go

    return _Side([a for s in sides for a in s.ins], [a for s in sides for a in s.out_shape],
                 [a for s in sides for a in s.sems], run("start"), run("finish"),
                 run("middle") if any(s.middle is not None for s in sides) else None)


class _Schedule:
    def __init__(self, shards):
        self.shards = shards
        self.weights = {}
        self.sliced = {}
        self.partials = {}
        self.reduced = {}

    def put(self, name, grad):
        self.sliced[name] = grad.reshape((N_DEV,) + self.shards[name].shape)

    def side_for(self, call):
        if call not in PLAN:
            return None
        sides = []
        for kind, names in PLAN[call]:
            if kind == "gather":
                sides.append(_gather_side([self.shards[n] for n in names]))
            elif kind == "pair":
                sides.append(_pair_side([self.sliced[n] for n in names]))
            else:
                sides.append(_chip_side([self.partials[n] for n in names]))
        return _join_sides(sides)

    def done(self, call, outs):
        at = 0
        for kind, names in PLAN[call]:
            self.file(kind, names, outs[at:at + len(names)])
            at += len(names)

    def file(self, kind, names, outs):
        if kind == "pair":
            core = lax.axis_index("c").astype(jnp.int32).reshape(1)
            sums = _pair_add([self.sliced[n] for n in names], list(outs), core, names[0] + "_pair_add")
            self.partials.update(dict(zip(names, sums)))
            return
        for n, o in zip(names, outs):
            if kind == "gather":
                self.weights[n] = o.reshape(N_DEV * o.shape[1], o.shape[2])
            else:
                self.reduced[n] = o


def kernel(x, ffn1_norm_g, ffn1_w_gate, ffn1_w_up, ffn1_w_down, mix_norm_g, w_in, attn_q_norm_g, attn_k_norm_g, attn_rel_bias, hgrn_lower_bounds, hgrn_out_norm_g, w_out, ffn2_norm_g, ffn2_w_gate, ffn2_w_up, ffn2_w_down, loss_target, m_ffn1_norm_g, m_ffn1_w_gate, m_ffn1_w_up, m_ffn1_w_down, m_mix_norm_g, m_w_in, m_attn_q_norm_g, m_attn_k_norm_g, m_attn_rel_bias, m_hgrn_lower_bounds, m_hgrn_out_norm_g, m_w_out, m_ffn2_norm_g, m_ffn2_w_gate, m_ffn2_w_up, m_ffn2_w_down, v_ffn1_norm_g, v_ffn1_w_gate, v_ffn1_w_up, v_ffn1_w_down, v_mix_norm_g, v_w_in, v_attn_q_norm_g, v_attn_k_norm_g, v_attn_rel_bias, v_hgrn_lower_bounds, v_hgrn_out_norm_g, v_w_out, v_ffn2_norm_g, v_ffn2_w_gate, v_ffn2_w_up, v_ffn2_w_down):
    wts = dict(zip(WEIGHTS, (ffn1_norm_g, ffn1_w_gate, ffn1_w_up, ffn1_w_down, mix_norm_g, w_in, attn_q_norm_g,
                             attn_k_norm_g, attn_rel_bias, hgrn_lower_bounds, hgrn_out_norm_g, w_out, ffn2_norm_g,
                             ffn2_w_gate, ffn2_w_up, ffn2_w_down)))
    mom = dict(zip(WEIGHTS, (m_ffn1_norm_g, m_ffn1_w_gate, m_ffn1_w_up, m_ffn1_w_down, m_mix_norm_g, m_w_in,
                             m_attn_q_norm_g, m_attn_k_norm_g, m_attn_rel_bias, m_hgrn_lower_bounds,
                             m_hgrn_out_norm_g, m_w_out, m_ffn2_norm_g, m_ffn2_w_gate, m_ffn2_w_up, m_ffn2_w_down)))
    var = dict(zip(WEIGHTS, (v_ffn1_norm_g, v_ffn1_w_gate, v_ffn1_w_up, v_ffn1_w_down, v_mix_norm_g, v_w_in,
                             v_attn_q_norm_g, v_attn_k_norm_g, v_attn_rel_bias, v_hgrn_lower_bounds,
                             v_hgrn_out_norm_g, v_w_out, v_ffn2_norm_g, v_ffn2_w_gate, v_ffn2_w_up, v_ffn2_w_down)))
    nb, seq, d = x.shape
    shapes = {n: wts[n].shape for n in WEIGHTS}

    def rows_first(a, n):
        return jnp.swapaxes(a, 1, 2) if n in COL_SHARDED else a

    sched = _Schedule({n: rows_first(wts[n], n)[0].astype(BF16) for n in BIG})
    sp = {n: wts[n] for n in SMALL}
    sp["attn_rel_bias"] = wts["attn_rel_bias"][0]
    _ACTIVE[0] = sched
    try:
        loss, dx, dsmall = _local_step(x.reshape(nb * seq, d), loss_target.reshape(nb * seq, d), sp,
                                       sched.weights, sched.put, nb, seq)
    finally:
        _ACTIVE[0] = None
    reduced = sched.reduced

    sums = _all_reduce_small([dsmall[n] for n in SMALL] + [jnp.full((1, 128), loss, F32)])
    gsmall = {n: s.reshape(shapes[n]) for n, s in zip(SMALL, sums)}
    loss_total = sums[-1][0, 0]

    grads, deltas, new_m, new_v = {}, {}, {}, {}
    for group, tag in (([n for n in BIG if n not in MIXER], "ffn_adamw"), (MIXER, "mixer_adamw")):
        outs = _adamw([rows_first(wts[n], n) for n in group], [rows_first(mom[n], n) for n in group],
                      [rows_first(var[n], n) for n in group], [reduced[n] for n in group], tag)
        for n, out in zip(group, outs):
            grads[n], deltas[n], new_m[n], new_v[n] = (rows_first(o, n) for o in out)
    outs = _adamw_small([wts[n] for n in SMALL], [mom[n] for n in SMALL], [var[n] for n in SMALL],
                        [gsmall[n] for n in SMALL])
    for n, (delta, m2, v2) in zip(SMALL, outs):
        deltas[n], new_m[n], new_v[n] = delta, m2, v2
    grads.update(gsmall)

    return (loss_total, dx.reshape(nb, seq, d), *[grads[n] for n in WEIGHTS], *[deltas[n] for n in WEIGHTS],
            *[new_m[n] for n in WEIGHTS], *[new_v[n] for n in WEIGHTS])
```

```python
import functools

import jax
import jax.numpy as jnp
from jax import lax
from jax.experimental import pallas as pl
from jax.experimental.pallas import tpu as pltpu

F32 = jnp.float32
BF16 = jnp.bfloat16

RMS_EPS = 1e-6
CHUNK = 64
LEFT_CHUNKS = 8
BAND = (LEFT_CHUNKS + 2) * CHUNK
KPAD = BAND - CHUNK
REL_CLIP = 128
N_REL = 2 * REL_CLIP + 1
N_REL_PAD = 384
ATTN_HEADS = 8
ATTN_HEAD_DIM = 64
ATTN_WIDTH = ATTN_HEADS * ATTN_HEAD_DIM
ATTN_LOCKSTEP = 4
ATTN_UNROLL = 16
HGRN_HEADS = 4
HGRN_HEAD_DIM = 128
HGRN_ROWS = 512
SUB = 16
N_SUB = CHUNK // SUB
DIAG_STAGE = 4
N_DEV = 8

ADAM_LR = 0.001
ADAM_B1 = 0.9
ADAM_B2 = 0.999
ADAM_EPS = 1e-08
ADAM_WD = 0.01
ADAM_STEP = 10

VMEM_LIMIT = 56 * 1024 * 1024

NT = (((1,), (1,)), ((), ()))
NN = (((1,), (0,)), ((), ()))


def _params(*sem):
    return pltpu.CompilerParams(dimension_semantics=sem, vmem_limit_bytes=VMEM_LIMIT)


def _sigmoid(v):
    return 0.5 * jnp.tanh(0.5 * v) + 0.5


def _dot(a, b, dims=NN):
    return lax.dot_general(a.astype(BF16), b.astype(BF16), dims, preferred_element_type=F32)


def _dot_exact01(m01, v):
    m = m01.astype(BF16)
    hi = v.astype(BF16)
    r1 = v - hi.astype(F32)
    mid = r1.astype(BF16)
    lo = (r1 - mid.astype(F32)).astype(BF16)
    out = lax.dot_general(m, hi, NN, preferred_element_type=F32)
    out = out + lax.dot_general(m, mid, NN, preferred_element_type=F32)
    return out + lax.dot_general(m, lo, NN, preferred_element_type=F32)


def _dot_exact01_r(v, m01):
    m = m01.astype(BF16)
    hi = v.astype(BF16)
    r1 = v - hi.astype(F32)
    mid = r1.astype(BF16)
    lo = (r1 - mid.astype(F32)).astype(BF16)
    out = lax.dot_general(hi, m, NN, preferred_element_type=F32)
    out = out + lax.dot_general(mid, m, NN, preferred_element_type=F32)
    return out + lax.dot_general(lo, m, NN, preferred_element_type=F32)


def _lockstep(stages):
    live = list(stages)
    while live:
        still = []
        for g in live:
            try:
                next(g)
                still.append(g)
            except StopIteration:
                pass
        live = still


def _row_sums_on_lanes(v):
    ones = jnp.ones((8, v.shape[1]), BF16)
    hi = v.astype(BF16)
    r1 = v - hi.astype(F32)
    mid = r1.astype(BF16)
    lo = (r1 - mid.astype(F32)).astype(BF16)
    out = lax.dot_general(ones, hi, NT, preferred_element_type=F32)
    out = out + lax.dot_general(ones, mid, NT, preferred_element_type=F32)
    return (out + lax.dot_general(ones, lo, NT, preferred_element_type=F32))[0:1, :]


def _tn(a, b):
    ap = jnp.concatenate([a, jnp.zeros_like(a)], axis=0)
    bp = jnp.concatenate([b, jnp.zeros_like(b)], axis=0)
    return _dot(ap.T, bp)


def _row_tile(t):
    for tm in (512, 256, 128, 64, 32, 16, 8):
        if t % tm == 0:
            return tm
    raise ValueError(t)


class _Side:
    def __init__(self, ins, out_shape, sems, start, finish, middle=None):
        self.ins, self.out_shape, self.sems = ins, out_shape, sems
        self.start, self.middle, self.finish = start, middle, finish


_ACTIVE = [None]


def _pallas(body, *, name, grid, in_specs, out_specs, out_shape, scratch_shapes=(), sem, args):
    sched = _ACTIVE[0]
    side = sched.side_for(name) if sched is not None else None
    if side is None:
        return pl.pallas_call(
            body, name=name, grid=grid, in_specs=list(in_specs), out_specs=list(out_specs),
            out_shape=list(out_shape), scratch_shapes=list(scratch_shapes), compiler_params=_params(*sem))(*args)
    cuts = [len(in_specs), len(side.ins), len(out_shape), len(side.out_shape), len(scratch_shapes)]

    def with_side(*refs):
        groups, at = [], 0
        for n in cuts:
            groups.append(refs[at:at + n])
            at += n
        ins, side_ins, outs, side_outs, scratch = groups
        side_sems = refs[at:]
        step, total = pl.program_id(0), grid[0]
        for a in range(1, len(grid)):
            step, total = step * grid[a] + pl.program_id(a), total * grid[a]
        has_middle = side.middle is not None and total >= 3

        @pl.when(step == 0)
        def _():
            side.start(side_ins, side_outs, side_sems)

        if has_middle:
            @pl.when(step == total // 2)
            def _():
                side.middle(side_ins, side_outs, side_sems)

        body(*ins, *outs, *scratch)

        @pl.when(step == total - 1)
        def _():
            if side.middle is not None and not has_middle:
                side.middle(side_ins, side_outs, side_sems)
            side.finish(side_ins, side_outs, side_sems)

    hbm = pl.BlockSpec(memory_space=pl.ANY)
    res = pl.pallas_call(
        with_side, name=name, grid=grid, in_specs=list(in_specs) + [hbm] * len(side.ins),
        out_specs=list(out_specs) + [hbm] * len(side.out_shape), out_shape=list(out_shape) + list(side.out_shape),
        scratch_shapes=list(scratch_shapes) + list(side.sems),
        compiler_params=_params(*(["arbitrary"] * len(grid))))(*args, *side.ins)
    sched.done(name, res[len(out_shape):])
    return res[:len(out_shape)]


def _rms_fwd(x, g, name):
    t, d = x.shape
    tm = _row_tile(t)

    def body(x_ref, g_ref, h_ref):
        xv = x_ref[...]
        r = lax.rsqrt(jnp.mean(xv * xv, axis=-1, keepdims=True) + RMS_EPS)
        h_ref[...] = (xv * r * g_ref[...]).astype(BF16)

    return _pallas(
        body, name=name, grid=(t // tm,),
        in_specs=[pl.BlockSpec((tm, d), lambda i: (i, 0)), pl.BlockSpec((1, d), lambda i: (0, 0))],
        out_specs=[pl.BlockSpec((tm, d), lambda i: (i, 0))], out_shape=[jax.ShapeDtypeStruct((t, d), BF16)],
        sem=("parallel",), args=(x, g))[0]


def _accumulate(ref, part, step):
    @pl.when(step == 0)
    def _():
        ref[...] = part

    @pl.when(step > 0)
    def _():
        ref[...] += part


def _mm(a, b, *, ta=False, tb=False, tm, tn, out_dtype=F32, add=None, scale=1.0, norm_g=None, norm_bwd=None, name):
    m, k = (a.shape[1], a.shape[0]) if ta else a.shape
    n = b.shape[0] if tb else b.shape[1]
    tm, tn = min(tm, m), min(tn, n)
    assert m % tm == 0 and n % tn == 0, (m, n, tm, tn)
    assert (norm_g is None and norm_bwd is None) or tn == n
    dims = (((0 if ta else 1,), (1 if tb else 0,)), ((), ()))
    n_in = 2 + (add is not None) + (norm_g is not None) + (3 if norm_bwd is not None else 0)

    def body(*refs):
        ins, outs = list(refs[2:n_in]), refs[n_in:]
        r = lax.dot_general(refs[0][...].astype(BF16), refs[1][...].astype(BF16), dims, preferred_element_type=F32)
        if scale != 1.0:
            r = r * scale
        if add is not None:
            r = r + ins.pop(0)[...]
        if norm_bwd is not None:
            xv, gv, dres = (ref[...] for ref in ins)
            rs = lax.rsqrt(jnp.mean(xv * xv, axis=-1, keepdims=True) + RMS_EPS)
            xhat = xv * rs
            gd = r * gv
            dx = dres + rs * (gd - xhat * jnp.mean(gd * xhat, axis=-1, keepdims=True))
            outs[0][...] = dx
            outs[1][...] = dx.astype(BF16)
            _accumulate(outs[2], jnp.sum(r * xhat, axis=0, keepdims=True), pl.program_id(0))
            return
        outs[0][...] = r.astype(out_dtype)
        if norm_g is not None:
            rs = lax.rsqrt(jnp.mean(r * r, axis=-1, keepdims=True) + RMS_EPS)
            outs[1][...] = (r * rs * ins.pop(0)[...]).astype(BF16)

    a_spec = pl.BlockSpec((k, tm), lambda i, j: (0, i)) if ta else pl.BlockSpec((tm, k), lambda i, j: (i, 0))
    b_spec = pl.BlockSpec((tn, k), lambda i, j: (j, 0)) if tb else pl.BlockSpec((k, tn), lambda i, j: (0, j))
    o_spec = pl.BlockSpec((tm, tn), lambda i, j: (i, j))
    vec = pl.BlockSpec((1, tn), lambda i, j: (0, j))
    args, specs = [a, b], [a_spec, b_spec]
    if add is not None:
        args.append(add)
        specs.append(o_spec)
    out_specs, out_shape = [o_spec], [jax.ShapeDtypeStruct((m, n), out_dtype)]
    if norm_g is not None:
        args.append(norm_g)
        specs.append(vec)
        out_specs.append(o_spec)
        out_shape.append(jax.ShapeDtypeStruct((m, n), BF16))
    if norm_bwd is not None:
        args += list(norm_bwd)
        specs += [o_spec, vec, o_spec]
        out_specs = [o_spec, o_spec, vec]
        out_shape = [jax.ShapeDtypeStruct((m, n), F32), jax.ShapeDtypeStruct((m, n), BF16),
                     jax.ShapeDtypeStruct((1, n), F32)]
    res = _pallas(body, name=name, grid=(m // tm, n // tn), in_specs=specs, out_specs=out_specs, out_shape=out_shape,
                  sem=("arbitrary", "arbitrary") if norm_bwd is not None else ("parallel", "parallel"), args=args)
    return res[0] if len(res) == 1 else res


def _ffn_tile(f):
    for tf in (1408, 512, 256, 128):
        if f % tf == 0:
            return tf
    raise ValueError(f)


def _ffn_fwd(h, x, wg, wu, wd, name, next_g=None, tgt=None):
    t, d = x.shape
    f = wg.shape[0]
    tm, tf = _row_tile(t), _ffn_tile(f)
    nf = f // tf
    assert (next_g is None) != (tgt is None)

    def body(h_ref, x_ref, wg_ref, wu_ref, wd_ref, tail_ref, g_ref, u_ref, o0_ref, o1_ref, *rest):
        acc_ref = rest[-1]
        j = pl.program_id(1)
        hv = h_ref[...]
        gv = lax.dot_general(hv, wg_ref[...], NT, preferred_element_type=F32)
        uv = lax.dot_general(hv, wu_ref[...], NT, preferred_element_type=F32)
        av = gv * _sigmoid(gv) * uv
        g_ref[...] = gv.astype(BF16)
        u_ref[...] = uv.astype(BF16)
        _accumulate(acc_ref, lax.dot_general(av.astype(BF16), wd_ref[...], NN, preferred_element_type=F32), j)

        @pl.when(j == nf - 1)
        def _():
            y = x_ref[...] + 0.5 * acc_ref[...]
            if tgt is None:
                o0_ref[...] = y
                rs = lax.rsqrt(jnp.mean(y * y, axis=-1, keepdims=True) + RMS_EPS)
                o1_ref[...] = (y * rs * tail_ref[...]).astype(BF16)
            else:
                e = y - tail_ref[...]
                dy = e * (1.0 / d)
                o0_ref[...] = dy
                o1_ref[...] = dy.astype(BF16)
                _accumulate(rest[0], jnp.sum(e * e, axis=0, keepdims=True), pl.program_id(0))

    row = pl.BlockSpec((tm, d), lambda i, j: (i, 0))
    hid = pl.BlockSpec((tm, tf), lambda i, j: (i, j))
    vec = pl.BlockSpec((1, d), lambda i, j: (0, 0))
    out_specs = [hid, hid, row, row] + ([vec] if tgt is not None else [])
    out_shape = [jax.ShapeDtypeStruct((t, f), BF16)] * 2 + [jax.ShapeDtypeStruct((t, d), F32),
                                                            jax.ShapeDtypeStruct((t, d), BF16)]
    if tgt is not None:
        out_shape.append(jax.ShapeDtypeStruct((1, d), F32))
    return _pallas(
        body, name=name, grid=(t // tm, nf),
        in_specs=[row, row] + [pl.BlockSpec((tf, d), lambda i, j: (j, 0))] * 3 + [vec if tgt is None else row],
        out_specs=out_specs, out_shape=out_shape, scratch_shapes=[pltpu.VMEM((tm, d), F32)],
        sem=("parallel" if tgt is None else "arbitrary", "arbitrary"),
        args=(h, x, wg, wu, wd, next_g if tgt is None else tgt))


def _ffn_up(h, wg, wu, name):
    t, d = h.shape
    f = wg.shape[0]
    tm, tf = _row_tile(t), _ffn_tile(f)

    def body(h_ref, wg_ref, wu_ref, g_ref, u_ref, a_ref):
        hv = h_ref[...]
        gv = lax.dot_general(hv, wg_ref[...], NT, preferred_element_type=F32)
        uv = lax.dot_general(hv, wu_ref[...], NT, preferred_element_type=F32)
        g_ref[...] = gv.astype(BF16)
        u_ref[...] = uv.astype(BF16)
        a_ref[...] = (gv * _sigmoid(gv) * uv).astype(BF16)

    hid = pl.BlockSpec((tm, tf), lambda i, j: (i, j))
    wrow = pl.BlockSpec((tf, d), lambda i, j: (j, 0))
    return _pallas(
        body, name=name, grid=(t // tm, f // tf), in_specs=[pl.BlockSpec((tm, d), lambda i, j: (i, 0)), wrow, wrow],
        out_specs=[hid, hid, hid], out_shape=[jax.ShapeDtypeStruct((t, f), BF16)] * 3,
        sem=("parallel", "parallel"), args=(h, wg, wu))


def _ffn_bwd_mid(dy, wd, g, u, name):
    t, d = dy.shape
    f = wd.shape[0]
    tm, tf = _row_tile(t), _ffn_tile(f)

    def body(dy_ref, wd_ref, g_ref, u_ref, dg_ref, du_ref, dwd_ref):
        dy16 = dy_ref[...]
        da = 0.5 * lax.dot_general(dy16, wd_ref[...], NT, preferred_element_type=F32)
        gv = g_ref[...].astype(F32)
        uv = u_ref[...].astype(F32)
        s = _sigmoid(gv)
        silu = gv * s
        dg_ref[...] = (da * uv * (s * (1.0 + gv * (1.0 - s)))).astype(BF16)
        du_ref[...] = (da * silu).astype(BF16)
        part = 0.5 * lax.dot_general((silu * uv).astype(BF16), dy16, (((0,), (0,)), ((), ())),
                                     preferred_element_type=F32)
        _accumulate(dwd_ref, part, pl.program_id(1))

    hid = pl.BlockSpec((tm, tf), lambda j, i: (i, j))
    wrow = pl.BlockSpec((tf, d), lambda j, i: (j, 0))
    return _pallas(
        body, name=name, grid=(f // tf, t // tm),
        in_specs=[pl.BlockSpec((tm, d), lambda j, i: (i, 0)), wrow, hid, hid],
        out_specs=[hid, hid, wrow],
        out_shape=[jax.ShapeDtypeStruct((t, f), BF16)] * 2 + [jax.ShapeDtypeStruct((f, d), F32)],
        sem=("parallel", "arbitrary"), args=(dy, wd, g, u))


def _rel_index(t, s_band):
    return jnp.clip(t + KPAD - s_band, -REL_CLIP, REL_CLIP) + REL_CLIP


def _bias_expand(rel_bias_pad):
    nh = rel_bias_pad.shape[0]

    def body(rb_ref, out_ref):
        rb = rb_ref[...]
        i_io = lax.broadcasted_iota(jnp.int32, (N_REL_PAD, BAND), 0)
        s_io = lax.broadcasted_iota(jnp.int32, (N_REL_PAD, BAND), 1)

        def row(r, carry):
            onehot = (i_io == _rel_index(pl.program_id(0) * rows + r, s_io)).astype(F32)
            out_ref[r] = _dot_exact01_r(rb, onehot)
            return carry

        lax.fori_loop(0, rows, row, 0)

    rows = 8
    return _pallas(
        body, name="bias_expand", grid=(CHUNK // rows,),
        in_specs=[pl.BlockSpec(rel_bias_pad.shape, lambda i: (0, 0))],
        out_specs=[pl.BlockSpec((rows, nh, BAND), lambda i: (i, 0, 0))],
        out_shape=[jax.ShapeDtypeStruct((CHUNK, nh, BAND), F32)], sem=("arbitrary",), args=(rel_bias_pad,))[0]


def _bias_fold(dbias):
    ng, nh = dbias.shape[0], dbias.shape[2]

    def body(db_ref, out_ref):
        s_io = lax.broadcasted_iota(jnp.int32, (BAND, N_REL_PAD), 0)
        i_io = lax.broadcasted_iota(jnp.int32, (BAND, N_REL_PAD), 1)

        def row(t, acc):
            onehot = (i_io == _rel_index(t, s_io)).astype(F32)
            d = db_ref[0, t]
            for gi in range(1, ng):
                d = d + db_ref[gi, t]
            return acc + _dot_exact01_r(d, onehot)

        out_ref[...] = lax.fori_loop(0, CHUNK, row, jnp.zeros((nh, N_REL_PAD), F32))

    return _pallas(
        body, name="bias_fold", grid=(1,), in_specs=[pl.BlockSpec(dbias.shape, lambda i: (0, 0, 0, 0))],
        out_specs=[pl.BlockSpec((nh, N_REL_PAD), lambda i: (0, 0))],
        out_shape=[jax.ShapeDtypeStruct((nh, N_REL_PAD), F32)], sem=("arbitrary",), args=(dbias,))[0]


def _left_half(shape):
    return lax.broadcasted_iota(jnp.int32, shape, len(shape) - 1) < ATTN_HEAD_DIM


def _stack_heads(v):
    left = _left_half(v.shape)
    zero = jnp.zeros_like(v)
    return jnp.concatenate([jnp.where(left, v, zero), jnp.where(left, zero, v)], axis=0)


def _unstack_heads(v):
    return jnp.where(_left_half((CHUNK, 128)), v[0:CHUNK, :], v[CHUNK:2 * CHUNK, :])


def _half_mean(v):
    r = lax.broadcasted_iota(jnp.int32, (128, 128), 0) < ATTN_HEAD_DIM
    c = lax.broadcasted_iota(jnp.int32, (128, 128), 1) < ATTN_HEAD_DIM
    return _dot_exact01_r(v, r == c) * (1.0 / ATTN_HEAD_DIM)


def _attn_prepare(q_ref, k_ref, v_ref, gq_ref, gk_ref, qs_scr, k_scr, v_scr):
    q, k = q_ref[...], k_ref[...]
    rq = lax.rsqrt(_half_mean(q * q) + RMS_EPS)
    rk = lax.rsqrt(_half_mean(k * k) + RMS_EPS)
    qhat, khat = q * rq, k * rk
    qs_scr[...] = (qhat * gq_ref[...] * ATTN_HEAD_DIM ** -0.5).astype(BF16)
    k_scr[0:KPAD, :] = jnp.zeros((KPAD, 128), BF16)
    v_scr[0:KPAD, :] = jnp.zeros((KPAD, 128), BF16)
    k_scr[KPAD:, :] = (khat * gk_ref[...]).astype(BF16)
    v_scr[KPAD:, :] = v_ref[...].astype(BF16)
    return qhat, rq, khat, rk


def _first_key(c):
    return jnp.maximum(CHUNK, (LEFT_CHUNKS + 1 - c) * CHUNK)


def _attn_fwd_chunk(c, qs_scr, k_scr, v_scr, bias_ref, o_ref):
    r0 = pl.multiple_of(c * CHUNK, CHUNK)
    s = lax.dot_general(_stack_heads(qs_scr[pl.ds(r0, CHUNK), :]), k_scr[pl.ds(r0, BAND), :], NT,
                        preferred_element_type=F32)
    yield
    col = lax.broadcasted_iota(jnp.int32, (2 * CHUNK, BAND), 1)
    s = jnp.where(col >= _first_key(c), s + bias_ref[...], -jnp.inf)
    m = jnp.max(s, axis=-1, keepdims=True)
    yield
    e = jnp.exp(s - m)
    yield
    inv = 1.0 / jnp.sum(e, axis=-1, keepdims=True)
    o = lax.dot_general(e.astype(BF16), v_scr[pl.ds(r0, BAND), :], NN, preferred_element_type=F32)
    yield
    o_ref[pl.ds(r0, CHUNK), :] = _unstack_heads(o * inv)


def _attn_bwd(proj, out, dout, bias, gq, gk, nb, seq):
    nc = seq // CHUNK
    lock = min(ATTN_LOCKSTEP, nc)
    assert nc % lock == 0
    scale = ATTN_HEAD_DIM ** -0.5

    def body(q_ref, k_ref, v_ref, o_ref, do_ref, bias_ref, gq_ref, gk_ref,
             dq_ref, dk_ref, dv_ref, dbias_ref, dgq_ref, dgk_ref,
             qs_scr, k_scr, v_scr, dqn_scr, dk_scr, dv_scr, db_scr):
        qhat, rq, khat, rk = _attn_prepare(q_ref, k_ref, v_ref, gq_ref, gk_ref, qs_scr, k_scr, v_scr)
        dk_scr[...] = jnp.zeros_like(dk_scr)
        dv_scr[...] = jnp.zeros_like(dv_scr)
        db_scr[...] = jnp.zeros_like(db_scr)

        def one_chunk(c):
            r0 = pl.multiple_of(c * CHUNK, CHUNK)
            qst = _stack_heads(qs_scr[pl.ds(r0, CHUNK), :])
            kb = k_scr[pl.ds(r0, BAND), :]
            vb = v_scr[pl.ds(r0, BAND), :]
            st = lax.dot_general(kb, qst, NT, preferred_element_type=F32) + bias_ref[...]
            dost = _stack_heads(do_ref[pl.ds(r0, CHUNK), :])
            dost16 = dost.astype(BF16)
            dpt = lax.dot_general(vb, dost16, NT, preferred_element_type=F32)
            yield
            key = lax.broadcasted_iota(jnp.int32, (BAND, 2 * CHUNK), 0)
            st = jnp.where(key >= _first_key(c), st, -jnp.inf)
            mx = jnp.max(st, axis=0, keepdims=True)
            drow = _row_sums_on_lanes(dost * _stack_heads(o_ref[pl.ds(r0, CHUNK), :]))
            yield
            et = jnp.exp(st - mx)
            yield
            pt = et * (1.0 / jnp.sum(et, axis=0, keepdims=True))
            yield
            dst = pt * (dpt - drow)
            dst16 = dst.astype(BF16)
            yield
            db_scr[...] += dst
            dqn_scr[pl.ds(r0, CHUNK), :] = scale * _unstack_heads(_dot(dst.T, kb))
            yield
            dk_scr[pl.ds(r0, BAND), :] += lax.dot_general(dst16, qst, NN, preferred_element_type=F32)
            yield
            dv_scr[pl.ds(r0, BAND), :] += lax.dot_general(pt.astype(BF16), dost16, NN, preferred_element_type=F32)

        def chunk(i, carry):
            _lockstep([one_chunk(i * lock + a) for a in range(lock)])
            return carry

        lax.fori_loop(0, nc // lock, chunk, 0, unroll=max(1, min(ATTN_UNROLL, nc) // lock))

        def norm_bwd(dn, hat, r, g_ref):
            gd = dn * g_ref[...]
            return r * (gd - hat * _half_mean(gd * hat)), jnp.sum(dn * hat, axis=0, keepdims=True)

        dq, dgq = norm_bwd(dqn_scr[...], qhat, rq, gq_ref)
        dk, dgk = norm_bwd(dk_scr[KPAD:, :], khat, rk, gk_ref)
        dq_ref[...] = dq.astype(BF16)
        dk_ref[...] = dk.astype(BF16)
        dv_ref[...] = dv_scr[KPAD:, :].astype(BF16)
        dbias_ref[0] = db_scr[...]
        dgq_ref[0] = dgq
        dgk_ref[0] = dgk

    def col(off):
        return pl.BlockSpec((seq, 128), lambda b, hp: (b, off + hp))

    vec = pl.BlockSpec((1, 128), lambda b, hp: (0, 0))
    gvec = pl.BlockSpec((1, 1, 128), lambda b, hp: (b * (ATTN_HEADS // 2) + hp, 0, 0))
    t = nb * seq
    return _pallas(
        body, name="attn_bwd", grid=(nb, ATTN_HEADS // 2),
        in_specs=[col(0), col(4), col(8), col(0), col(0),
                  pl.BlockSpec((BAND, 2 * CHUNK), lambda b, hp: (hp, 0)), vec, vec],
        out_specs=[col(0), col(0), col(0), pl.BlockSpec((1, BAND, 2 * CHUNK), lambda b, hp: (b, hp, 0)),
                   gvec, gvec],
        out_shape=[jax.ShapeDtypeStruct((t, ATTN_WIDTH), BF16)] * 3
        + [jax.ShapeDtypeStruct((nb, ATTN_HEADS // 2 * BAND, 2 * CHUNK), F32)]
        + [jax.ShapeDtypeStruct((nb * ATTN_HEADS // 2, 1, 128), F32)] * 2,
        scratch_shapes=[pltpu.VMEM((seq, 128), BF16), pltpu.VMEM((seq + KPAD, 128), BF16),
                        pltpu.VMEM((seq + KPAD, 128), BF16), pltpu.VMEM((seq, 128), F32),
                        pltpu.VMEM((seq + KPAD, 128), F32), pltpu.VMEM((seq + KPAD, 128), F32),
                        pltpu.VMEM((BAND, 2 * CHUNK), F32)],
        sem=("parallel", "parallel"), args=(proj, proj, proj, out, dout, bias, gq, gk))


def _tri(lower):
    r = lax.broadcasted_iota(jnp.int32, (CHUNK, CHUNK), 0)
    c = lax.broadcasted_iota(jnp.int32, (CHUNK, CHUNK), 1)
    return (r >= c) if lower else (r <= c)


def _hgrn_gates(hq, hf, lb):
    sq = _sigmoid(hq)
    sf = _sigmoid(hf)
    return hq * sq, sq, sf, lb + (1.0 - lb) * sf


def _hgrn_offdiag(q_s, k_s, b_s):
    row = lax.broadcasted_iota(jnp.int32, (CHUNK, HGRN_HEAD_DIM), 0)
    bv, qv, kv = b_s[...], q_s[...], k_s[...]
    eqs, eks = [], []
    for i in range(1, N_SUB):
        r = b_s[pl.ds(SUB * i - 1, 1), :]
        in_i = (row >= SUB * i) & (row < SUB * (i + 1))
        eqs.append(jnp.exp(jnp.where(in_i, bv - r, -jnp.inf)))
        eks.append(jnp.exp(jnp.where(row < SUB * i, r - bv, -jnp.inf)))
    eq = jnp.concatenate(eqs, axis=1)
    ek = jnp.concatenate(eks, axis=1)
    qt = jnp.concatenate([qv] * (N_SUB - 1), axis=1) * eq
    kt = jnp.concatenate([kv] * (N_SUB - 1), axis=1) * ek
    return qt, kt, eq, ek


def _hgrn_diag_e(b_s, i, s):
    t_io = lax.broadcasted_iota(jnp.int32, (SUB, HGRN_HEAD_DIM), 0)
    bi = b_s[pl.ds(SUB * i, SUB), :]
    return jnp.exp(jnp.where(t_io >= s, bi - b_s[pl.ds(SUB * i + s, 1), :], -jnp.inf)), t_io


def _hgrn_intra(q_s, k_s, b_s, a_s, qt, kt):
    ktp = jnp.concatenate([kt, jnp.zeros_like(kt)], axis=0)
    a_s[...] = _dot(qt, ktp, NT)
    yield
    col = lax.broadcasted_iota(jnp.int32, (SUB, HGRN_HEAD_DIM), 1)
    for i in range(N_SUB):
        qi = q_s[pl.ds(SUB * i, SUB), :]
        ai = jnp.zeros((SUB, HGRN_HEAD_DIM), F32)
        for s in range(SUB):
            e, _ = _hgrn_diag_e(b_s, i, s)
            a_col = jnp.sum(qi * k_s[pl.ds(SUB * i + s, 1), :] * e, axis=-1, keepdims=True)
            ai = ai + jnp.where(col == SUB * i + s, a_col, 0.0)
            if s % DIAG_STAGE == DIAG_STAGE - 1:
                yield
        a_s[pl.ds(SUB * i, SUB), :] += ai


def _mixer_fwd(proj, bias, gq, gk, lb, go, nb, seq):
    nc = seq // CHUNK
    hd = HGRN_HEAD_DIM
    nblk = ATTN_HEADS // 2
    rows_blk = seq // nblk
    nck = rows_blk // CHUNK
    per = nc // nck
    assert rows_blk % CHUNK == 0

    def body(aq_ref, ak_ref, av_ref, bias_ref, gq_ref, gk_ref, hq_ref, hf_ref, hi_ref, hg_ref, lb_ref, go_ref,
             ao_ref, y_ref, o_ref, st_ref, a_ref, qs_scr, k_scr, v_scr, st_all, q_all, k_all, b_all, a_all):
        _attn_prepare(aq_ref, ak_ref, av_ref, gq_ref, gk_ref, qs_scr, k_scr, v_scr)

        @pl.when(pl.program_id(1) == 0)
        def _():
            st_all[...] = jnp.zeros_like(st_all)

        lower = _tri(True)

        def head_chunk(hh, c, rows):
            ln = slice(hd * hh, hd * (hh + 1))
            st, q_s, k_s, b_s, a_s = st_all.at[hh], q_all.at[hh], k_all.at[hh], b_all.at[hh], a_all.at[hh]
            q, _, _, f = _hgrn_gates(hq_ref[rows, ln], hf_ref[rows, ln], lb_ref[:, ln])
            v = hi_ref[rows, ln]
            yield
            b = _dot_exact01(lower, jnp.log(f))
            q_s[...] = q
            k_s[...] = 1.0 - f
            b_s[...] = b
            st_ref[hh, c] = st[...]
            yield
            qt, kt, _, _ = _hgrn_offdiag(q_s, k_s, b_s)
            yield
            yield from _hgrn_intra(q_s, k_s, b_s, a_s, qt, kt)
            a16 = a_s[...].astype(BF16)
            a_ref[hh, c] = a16
            vp = jnp.concatenate([v, jnp.zeros_like(v)], axis=0)
            o = _dot(a16, vp) + _dot(q * jnp.exp(b), st[...], NT)
            yield
            bl = b_s[pl.ds(CHUNK - 1, 1), :]
            st[...] = st[...] * jnp.exp(bl) + _tn(v, (1.0 - f) * jnp.exp(bl - b))
            o_ref[rows, ln] = o
            yield
            n = o * lax.rsqrt(jnp.mean(o * o, axis=-1, keepdims=True) + RMS_EPS) * go_ref[...]
            hg = hg_ref[rows, ln]
            y_ref[rows, ln] = n * hg * _sigmoid(hg)

        def chunk(c, carry):
            rows = pl.ds(pl.multiple_of(c * CHUNK, CHUNK), CHUNK)
            _lockstep([_attn_fwd_chunk(c * per + a, qs_scr, k_scr, v_scr, bias_ref, ao_ref) for a in range(per)]
                      + [head_chunk(hh, c, rows) for hh in range(HGRN_HEADS)])
            return carry

        lax.fori_loop(0, nck, chunk, 0, unroll=min(8, nck))

    hp, wide = HGRN_HEADS, HGRN_HEADS * hd

    def acol(off):
        return pl.BlockSpec((seq, 128), lambda b, s: (b, off + s))

    def col(off):
        return pl.BlockSpec((rows_blk, wide), lambda b, s: (b * nblk + s, off // hp))

    out = pl.BlockSpec((rows_blk, wide), lambda b, s: (b * nblk + s, 0))
    vec = pl.BlockSpec((1, 128), lambda b, s: (0, 0))
    t = nb * seq
    return _pallas(
        body, name="mixer_fwd", grid=(nb, nblk),
        in_specs=[acol(0), acol(4), acol(8), pl.BlockSpec((2 * CHUNK, BAND), lambda b, s: (s, 0)), vec, vec,
                  col(12), col(16), col(20), col(24), pl.BlockSpec((1, wide), lambda b, s: (0, 0)), vec],
        out_specs=[pl.BlockSpec((seq, 128), lambda b, s: (b, s)), out, out,
                   pl.BlockSpec((hp, nck, hd, hd), lambda b, s: (b, s, 0, 0)),
                   pl.BlockSpec((hp, nck, CHUNK, hd), lambda b, s: (b, s, 0, 0))],
        out_shape=[jax.ShapeDtypeStruct((t, ATTN_WIDTH), F32)] + [jax.ShapeDtypeStruct((t, wide), F32)] * 2
        + [jax.ShapeDtypeStruct((nb * hp, nc, hd, hd), F32), jax.ShapeDtypeStruct((nb * hp, nc, CHUNK, hd), BF16)],
        scratch_shapes=[pltpu.VMEM((seq, 128), BF16), pltpu.VMEM((seq + KPAD, 128), BF16),
                        pltpu.VMEM((seq + KPAD, 128), BF16), pltpu.VMEM((hp, hd, hd), F32)]
        + [pltpu.VMEM((hp, CHUNK, hd), F32)] * 4,
        sem=("parallel", "arbitrary"), args=(proj,) * 3 + (bias, gq, gk) + (proj,) * 4 + (lb, go))


def _hgrn_bwd(proj, lb, go, o_pre, states, scores, dout, nb, seq):
    nc = seq // CHUNK
    hd = HGRN_HEAD_DIM
    rows_blk = min(HGRN_ROWS, seq)
    nblk, nck = seq // rows_blk, rows_blk // CHUNK

    def body(hq_ref, hf_ref, hi_ref, hg_ref, lb_ref, go_ref, o_ref, st_ref, a_ref, dy_ref,
             dhq_ref, dhf_ref, dhi_ref, dhg_ref, dlb_ref, dgo_ref,
             dst_all, q_all, k_all, b_all, da_all, dqi_all, dki_all, dlb_all, dgo_all):
        @pl.when(pl.program_id(1) == 0)
        def _():
            dst_all[...] = jnp.zeros_like(dst_all)
            dlb_all[...] = jnp.zeros_like(dlb_all)
            dgo_all[...] = jnp.zeros_like(dgo_all)

        lower, upper = _tri(True), _tri(False)
        gov = go_ref[...]
        row = lax.broadcasted_iota(jnp.int32, (CHUNK, hd), 0)

        def head_chunk(hh, c, rows):
            ln = slice(hd * hh, hd * (hh + 1))
            dst, q_s, k_s, b_s = dst_all.at[hh], q_all.at[hh], k_all.at[hh], b_all.at[hh]
            da_s, dqi_s, dki_s = da_all.at[hh], dqi_all.at[hh], dki_all.at[hh]
            dlb_acc, dgo_acc = dlb_all.at[hh], dgo_all.at[hh]
            lbv = lb_ref[:, ln]
            hq, hf, v, hg = hq_ref[rows, ln], hf_ref[rows, ln], hi_ref[rows, ln], hg_ref[rows, ln]
            q, sq, sf, f = _hgrn_gates(hq, hf, lbv)
            kk = 1.0 - f
            yield
            b = _dot_exact01(lower, jnp.log(f))
            q_s[...] = q
            k_s[...] = kk
            b_s[...] = b
            yield
            bl = b_s[pl.ds(CHUNK - 1, 1), :]
            ebl = jnp.exp(bl)
            ekd = jnp.exp(bl - b)
            kd = kk * ekd
            eb = jnp.exp(b)
            qb = q * eb
            st0 = st_ref[hh, c]
            dst1 = dst[...]
            yield

            o = o_ref[rows, ln]
            dy = dy_ref[rows, ln]
            sg = _sigmoid(hg)
            rstd = lax.rsqrt(jnp.mean(o * o, axis=-1, keepdims=True) + RMS_EPS)
            ohat = o * rstd
            dn = dy * hg * sg
            dhg_ref[rows, ln] = (dy * ohat * gov * (sg * (1.0 + hg * (1.0 - sg)))).astype(BF16)
            dgo_acc[...] += jnp.sum(dn * ohat, axis=0, keepdims=True)
            gdn = dn * gov
            do = rstd * (gdn - ohat * jnp.mean(gdn * ohat, axis=-1, keepdims=True))
            yield

            qt, kt, eq, ek = _hgrn_offdiag(q_s, k_s, b_s)
            da = _dot(do, v, NT)
            dat = _dot(v, do, NT)
            da_s[...] = da
            yield
            dqo = _dot(da, kt) * eq
            dko = _dot(dat, qt) * ek
            dqi_s[...] = sum(dqo[:, j * hd:(j + 1) * hd] for j in range(N_SUB - 1))
            dki_s[...] = sum(dko[:, j * hd:(j + 1) * hd] for j in range(N_SUB - 1))
            yield
            col = lax.broadcasted_iota(jnp.int32, (SUB, CHUNK), 1)
            for i in range(N_SUB):
                qi = q_s[pl.ds(SUB * i, SUB), :]
                dai = da_s[pl.ds(SUB * i, SUB), :]
                dqd = jnp.zeros((SUB, hd), F32)
                for s in range(SUB):
                    e, _ = _hgrn_diag_e(b_s, i, s)
                    dacol = jnp.sum(jnp.where(col == SUB * i + s, dai, 0.0), axis=-1, keepdims=True)
                    w = dacol * e
                    dqd = dqd + w * k_s[pl.ds(SUB * i + s, 1), :]
                    dki_s[pl.ds(SUB * i + s, 1), :] += jnp.sum(w * qi, axis=0, keepdims=True)
                    if s % DIAG_STAGE == DIAG_STAGE - 1:
                        yield
                dqi_s[pl.ds(SUB * i, SUB), :] += dqd
            dqi, dki = dqi_s[...], dki_s[...]

            dv = _tn(a_ref[hh, c].astype(F32), do)[0:CHUNK, :] + _dot(kd, dst1, NT)
            dqb = _dot(do, st0)
            dkd = _dot(v, dst1)
            yield
            t2 = dkd * kd
            dq = dqb * eb + dqi
            dk = dkd * ekd + dki
            dbl = jnp.sum(t2, axis=0, keepdims=True) + ebl * jnp.sum(st0 * dst1, axis=0, keepdims=True)
            db = dqb * qb - t2 + q * dqi - kk * dki + jnp.where(row == CHUNK - 1, dbl, 0.0)
            yield
            dg = _dot_exact01(upper, db)
            dst[...] = dst1 * ebl + _tn(do, qb)
            yield

            df = dg / f - dk
            dhf_ref[rows, ln] = (df * (1.0 - lbv) * sf * (1.0 - sf)).astype(BF16)
            dlb_acc[...] += jnp.sum(df * (1.0 - sf), axis=0, keepdims=True)
            dhq_ref[rows, ln] = (dq * (sq * (1.0 + hq * (1.0 - sq)))).astype(BF16)
            dhi_ref[rows, ln] = dv.astype(BF16)

        def chunk(it, carry):
            c = nck - 1 - it
            rows = pl.ds(pl.multiple_of(c * CHUNK, CHUNK), CHUNK)
            _lockstep([head_chunk(hh, c, rows) for hh in range(HGRN_HEADS)])
            return carry

        lax.fori_loop(0, nck, chunk, 0, unroll=min(2, nck))

        @pl.when(pl.program_id(1) == nblk - 1)
        def _():
            dlb_ref[...] = dlb_all[...]
            dgo_ref[...] = dgo_all[...]

    hp, wide = HGRN_HEADS, HGRN_HEADS * hd

    def col(off):
        return pl.BlockSpec((rows_blk, wide), lambda b, s: (b * nblk + nblk - 1 - s, off // hp))

    out = pl.BlockSpec((rows_blk, wide), lambda b, s: (b * nblk + nblk - 1 - s, 0))
    part = pl.BlockSpec((hp, 1, hd), lambda b, s: (b, 0, 0))
    t = nb * seq
    return pl.pallas_call(
        body, name="hgrn_bwd", grid=(nb, nblk),
        in_specs=[col(12), col(16), col(20), col(24), pl.BlockSpec((1, wide), lambda b, s: (0, 0)),
                  pl.BlockSpec((1, hd), lambda b, s: (0, 0)), out,
                  pl.BlockSpec((hp, nck, hd, hd), lambda b, s: (b, nblk - 1 - s, 0, 0)),
                  pl.BlockSpec((hp, nck, CHUNK, hd), lambda b, s: (b, nblk - 1 - s, 0, 0)), col(4)],
        out_specs=[out, out, out, out, part, part],
        out_shape=[jax.ShapeDtypeStruct((t, wide), BF16)] * 4 + [jax.ShapeDtypeStruct((nb * hp, 1, hd), F32)] * 2,
        scratch_shapes=[pltpu.VMEM((hp, hd, hd), F32)] + [pltpu.VMEM((hp, CHUNK, hd), F32)] * 3
        + [pltpu.VMEM((hp, CHUNK, CHUNK), F32)] + [pltpu.VMEM((hp, CHUNK, hd), F32)] * 2
        + [pltpu.VMEM((hp, 1, hd), F32)] * 2,
        compiler_params=_params("parallel", "arbitrary"),
    )(proj, proj, proj, proj, lb, go, o_pre, states, scores, dout)


def _lb_fwd(lower_bounds):
    def body(x_ref, o_ref):
        xv = x_ref[...]
        e = jnp.exp(xv - jnp.max(xv, axis=0, keepdims=True))
        o_ref[...] = e[0:1, :] / jnp.sum(e, axis=0, keepdims=True)

    return pl.pallas_call(body, name="lb_fwd",
                          out_shape=jax.ShapeDtypeStruct((1, lower_bounds.shape[1]), F32))(lower_bounds)


def _lb_bwd(lower_bounds, dlb_parts):
    ng = dlb_parts.shape[0]

    def body(x_ref, d_ref, o_ref):
        xv = x_ref[...]
        e = jnp.exp(xv - jnp.max(xv, axis=0, keepdims=True))
        p = e / jnp.sum(e, axis=0, keepdims=True)
        dlb = d_ref[0]
        for gi in range(1, ng):
            dlb = dlb + d_ref[gi]
        first = lax.broadcasted_iota(jnp.int32, xv.shape, 0) == 0
        o_ref[...] = p * (jnp.where(first, dlb, 0.0) - p[0:1, :] * dlb)

    return pl.pallas_call(body, name="lb_bwd",
                          out_shape=jax.ShapeDtypeStruct(lower_bounds.shape, F32))(lower_bounds, dlb_parts)


def _ffn_bwd(x, g, h, gate, up, dy, dy16, w, put, tag):
    wg, wu, wd = w[tag + "_w_gate"], w[tag + "_w_up"], w[tag + "_w_down"]
    dgate, dup, dwd = _ffn_bwd_mid(dy16, wd, gate, up, tag + "_bwd_mid")
    put(tag + "_w_down", dwd)
    put(tag + "_w_gate", _mm(dgate, h, ta=True, tm=1408, tn=512, name=tag + "_dwg"))
    put(tag + "_w_up", _mm(dup, h, ta=True, tm=1408, tn=512, name=tag + "_dwu"))
    dh = _mm(dgate, wg, tm=512, tn=1024, name=tag + "_dh_gate")
    return _mm(dup, wu, tm=512, tn=1024, add=dh, norm_bwd=(x, g, dy), name=tag + "_dh_up")


def _local_step(x, tgt, sp, w, put, nb, seq):
    d = x.shape[1]
    h1 = _rms_fwd(x, sp["ffn1_norm_g"], "ffn1_norm")
    rb_pad = jnp.pad(sp["attn_rel_bias"], ((0, 0), (0, N_REL_PAD - N_REL)))
    bias = jnp.transpose(_bias_expand(rb_pad), (1, 0, 2)).reshape(ATTN_HEADS * CHUNK, BAND)
    gq2 = jnp.concatenate([sp["attn_q_norm_g"]] * 2, axis=1)
    gk2 = jnp.concatenate([sp["attn_k_norm_g"]] * 2, axis=1)
    lb = _lb_fwd(sp["hgrn_lower_bounds"])
    gate1, up1, act1 = _ffn_up(h1, w["ffn1_w_gate"], w["ffn1_w_up"], "ffn1_up")
    x1, h2 = _mm(act1, w["ffn1_w_down"], tm=512, tn=d, add=x, scale=0.5, norm_g=sp["mix_norm_g"],
                 name="ffn1_down")
    proj = _mm(h2, w["w_in"], tb=True, tm=256, tn=w["w_in"].shape[0], name="in_proj")
    attn, hy, ho, hstate, hscore = _mixer_fwd(proj, bias, gq2, gk2, lb, sp["hgrn_out_norm_g"], nb, seq)
    mix = jnp.concatenate([attn, hy], axis=1)
    x2, h3 = _mm(mix, w["w_out"], tm=512, tn=1024, add=x1, norm_g=sp["ffn2_norm_g"], name="out_proj")
    gate2, up2, dx3, dx3_16, sq = _ffn_fwd(h3, x2, w["ffn2_w_gate"], w["ffn2_w_up"], w["ffn2_w_down"], "ffn2_fwd",
                                           tgt=tgt)
    loss = 0.5 * jnp.sum(sq) / d

    dx2, dx2_16, dg3 = _ffn_bwd(x2, sp["ffn2_norm_g"], h3, gate2, up2, dx3, dx3_16, w, put, "ffn2")
    dmix = _mm(dx2_16, w["w_out"], tb=True, tm=512, tn=1024, name="out_proj_dx")
    put("w_out", _mm(mix, dx2_16, ta=True, tm=512, tn=1024, name="out_proj_dw"))
    bias_t = jnp.transpose(bias.reshape(ATTN_HEADS // 2, 2 * CHUNK, BAND), (0, 2, 1)).reshape(-1, 2 * CHUNK)
    dq, dk, dv, dbias, dgq, dgk = _attn_bwd(proj, attn, dmix, bias_t, gq2, gk2, nb, seq)
    dbias = jnp.transpose(dbias.reshape(nb, ATTN_HEADS // 2, BAND, 2, CHUNK), (0, 4, 1, 3, 2))
    dbias = dbias.reshape(nb, CHUNK, ATTN_HEADS, BAND)
    dgq = jnp.sum(dgq, axis=(0, 1)).reshape(2, ATTN_HEAD_DIM).sum(axis=0, keepdims=True)
    dgk = jnp.sum(dgk, axis=(0, 1)).reshape(2, ATTN_HEAD_DIM).sum(axis=0, keepdims=True)
    dhq, dhf, dhi, dhg, dlb, dgo = _hgrn_bwd(proj, lb, sp["hgrn_out_norm_g"], ho, hstate, hscore, dmix, nb, seq)
    dproj = jnp.concatenate([dq, dk, dv, dhq, dhf, dhi, dhg], axis=1)
    put("w_in", _mm(dproj, h2, ta=True, tm=512, tn=1024, name="in_proj_dw"))
    dx1, dx1_16, dgm = _mm(dproj, w["w_in"], tm=512, tn=1024, norm_bwd=(x1, sp["mix_norm_g"], dx2),
                           name="in_proj_dx")
    dx0, _, dg1 = _ffn_bwd(x, sp["ffn1_norm_g"], h1, gate1, up1, dx1, dx1_16, w, put, "ffn1")

    small = {
        "ffn1_norm_g": dg1, "mix_norm_g": dgm, "ffn2_norm_g": dg3,
        "attn_q_norm_g": dgq, "attn_k_norm_g": dgk,
        "attn_rel_bias": _bias_fold(dbias)[:, :N_REL],
        "hgrn_lower_bounds": _lb_bwd(sp["hgrn_lower_bounds"], dlb.reshape(nb, 1, HGRN_HEADS * HGRN_HEAD_DIM)),
        "hgrn_out_norm_g": jnp.sum(dgo, axis=(0, 1))[None, :],
    }
    return loss, dx0, small


MESH = pl.DeviceIdType.MESH
ANY = pl.BlockSpec(memory_space=pl.ANY)


def _coords():
    return lax.axis_index("x"), lax.axis_index("y"), lax.axis_index("c")


def _other_chips(x, y):
    return [(1 - x, y), (x, 1 - y), (1 - x, 1 - y)]


def _gather_side(shards):
    n = len(shards)

    def copies(ins, outs, sems):
        send_sems, recv_sems, local_sems = sems
        x, y, c = _coords()
        xn, yn, dg = (1 - x, y), (x, 1 - y), (1 - x, 1 - y)

        def copy(i, k, block, to, half=None, src=None):
            bx, by, bc = block
            dst = outs[i].at[4 * bx + 2 * by + bc]
            if half is not None:
                rows = shards[i].shape[0] // 2
                dst = dst.at[pl.ds(half * rows, rows)]
            return pltpu.make_async_remote_copy(
                src_ref=dst if src is None else src, dst_ref=dst, send_sem=send_sems.at[i, k],
                recv_sem=recv_sems.at[i, k], device_id=to, device_id_type=MESH)

        mine = [pltpu.make_async_copy(ins[i], outs[i].at[4 * x + 2 * y + c], local_sems.at[i]) for i in range(n)]
        return copy, mine, (x, y, c), (x, y, 1 - c), xn, yn, dg, c

    def own(copy, i, ins, me, sibling, xn, yn, c):
        return [copy(i, 0, me, sibling, src=ins[i]), copy(i, 1, me, (*xn, c), src=ins[i]),
                copy(i, 2, me, (*yn, c), src=ins[i])]

    def passed_on(copy, i, sibling, xn, yn, c):
        return [copy(i, 3, (*xn, c), sibling), copy(i, 5, (*xn, c), (*yn, c), half=0),
                copy(i, 4, (*yn, c), sibling), copy(i, 6, (*yn, c), (*xn, c), half=1)]

    def diagonal(copy, i, sibling, dg, c):
        return [copy(i, 7, (*dg, c), sibling, half=0), copy(i, 8, (*dg, c), sibling, half=1)]

    def start(ins, outs, sems):
        copy, mine, me, sibling, xn, yn, dg, c = copies(ins, outs, sems)
        for cp in mine + [cp for i in range(n) for cp in own(copy, i, ins, me, sibling, xn, yn, c)]:
            cp.start()

    def middle(ins, outs, sems):
        copy, mine, me, sibling, xn, yn, dg, c = copies(ins, outs, sems)
        for i in range(n):
            fwd_x, relay_x, fwd_y, relay_y = passed_on(copy, i, sibling, xn, yn, c)
            copy(i, 1, (*xn, c), me).wait_recv()
            fwd_x.start()
            relay_x.start()
            copy(i, 2, (*yn, c), me).wait_recv()
            fwd_y.start()
            relay_y.start()

    def finish(ins, outs, sems):
        copy, mine, me, sibling, xn, yn, dg, c = copies(ins, outs, sems)
        for i in range(n):
            top, bottom = diagonal(copy, i, sibling, dg, c)
            copy(i, 5, (*dg, c), me, half=0).wait_recv()
            top.start()
            copy(i, 6, (*dg, c), me, half=1).wait_recv()
            bottom.start()
        for i in range(n):
            copy(i, 0, sibling, me).wait_recv()
            copy(i, 3, (*xn, 1 - c), me).wait_recv()
            copy(i, 4, (*yn, 1 - c), me).wait_recv()
            copy(i, 7, (*dg, 1 - c), me, half=0).wait_recv()
            copy(i, 8, (*dg, 1 - c), me, half=1).wait_recv()
        for i in range(n):
            for cp in (own(copy, i, ins, me, sibling, xn, yn, c) + passed_on(copy, i, sibling, xn, yn, c)
                       + diagonal(copy, i, sibling, dg, c)):
                cp.wait_send()
        for cp in mine:
            cp.wait()

    return _Side(list(shards), [jax.ShapeDtypeStruct((N_DEV,) + s.shape, s.dtype) for s in shards],
                 [pltpu.SemaphoreType.DMA((n, 9)), pltpu.SemaphoreType.DMA((n, 9)), pltpu.SemaphoreType.DMA((n,))],
                 start, finish, middle)


def _pair_side(grads):
    n = len(grads)

    def copies(ins, outs, sems):
        send_sems, recv_sems = sems
        x, y, c = _coords()
        return [pltpu.make_async_remote_copy(
            src_ref=ins[i].at[2 * k + 1 - c], dst_ref=outs[i].at[k], send_sem=send_sems.at[i, k],
            recv_sem=recv_sems.at[i, k], device_id=(x, y, 1 - c), device_id_type=MESH)
            for i in range(n) for k in range(4)]

    def start(ins, outs, sems):
        for cp in copies(ins, outs, sems):
            cp.start()

    def finish(ins, outs, sems):
        for cp in copies(ins, outs, sems):
            cp.wait()

    return _Side(list(grads), [jax.ShapeDtypeStruct((4,) + g.shape[1:], g.dtype) for g in grads],
                 [pltpu.SemaphoreType.DMA((n, 4)), pltpu.SemaphoreType.DMA((n, 4))], start, finish)


def _pair_add(grads, recvs, core, name):
    count = len(grads)

    def body(c_ref, *refs):
        for n in range(count):
            refs[2 * count + n][...] = (refs[2 * n][...] + refs[2 * n + 1][...]).astype(BF16)

    in_specs, out_specs = [], []
    for g in grads:
        blk = (1,) + g.shape[1:]
        in_specs += [pl.BlockSpec(blk, lambda k, c_ref: (2 * k + c_ref[0], 0, 0)),
                     pl.BlockSpec(blk, lambda k, c_ref: (k, 0, 0))]
        out_specs.append(pl.BlockSpec(blk, lambda k, c_ref: (k, 0, 0)))
    out = pl.pallas_call(
        body, name=name,
        grid_spec=pltpu.PrefetchScalarGridSpec(num_scalar_prefetch=1, grid=(4,), in_specs=in_specs,
                                               out_specs=out_specs),
        out_shape=[jax.ShapeDtypeStruct((4,) + g.shape[1:], BF16) for g in grads],
        compiler_params=_params("arbitrary"),
    )(core, *[a for pair in zip(grads, recvs) for a in pair])
    return list(out)


def _chip_side(parts):
    n = len(parts)

    def copies(ins, outs, sems):
        send_sems, recv_sems, local_sems = sems
        x, y, c = _coords()
        chips = _other_chips(x, y)
        mine = [pltpu.make_async_copy(ins[i].at[2 * x + y], outs[i].at[2 * x + y], local_sems.at[i])
                for i in range(n)]
        sent = [pltpu.make_async_remote_copy(
            src_ref=ins[i].at[2 * px + py], dst_ref=outs[i].at[2 * x + y], send_sem=send_sems.at[i, j],
            recv_sem=recv_sems.at[i, j], device_id=(px, py, c), device_id_type=MESH)
            for i in range(n) for j, (px, py) in enumerate(chips)]
        return mine, sent, chips, c

    def start(ins, outs, sems):
        mine, sent, _, _ = copies(ins, outs, sems)
        for cp in mine + sent:
            cp.start()

    def finish(ins, outs, sems):
        mine, sent, chips, c = copies(ins, outs, sems)
        send_sems, recv_sems, _ = sems
        for i in range(n):
            for j, (px, py) in enumerate(chips):
                landed = outs[i].at[2 * px + py]
                pltpu.make_async_remote_copy(
                    src_ref=landed, dst_ref=landed, send_sem=send_sems.at[i, j], recv_sem=recv_sems.at[i, j],
                    device_id=(px, py, c), device_id_type=MESH).wait_recv()
        for cp in sent:
            cp.wait_send()
        for cp in mine:
            cp.wait()

    return _Side(list(parts), [jax.ShapeDtypeStruct(p.shape, p.dtype) for p in parts],
                 [pltpu.SemaphoreType.DMA((n, 3)), pltpu.SemaphoreType.DMA((n, 3)), pltpu.SemaphoreType.DMA((n,))],
                 start, finish)


def _all_reduce_small(vals):
    n = len(vals)

    def body(*refs):
        ins, outs, bufs = refs[:n], refs[n:2 * n], refs[2 * n:3 * n]
        send_sems, recv_sems = refs[3 * n:]
        x, y, c = _coords()
        me = 4 * x + 2 * y + c
        for i in range(n):
            bufs[i][me] = ins[i][...]
        sent, landed = [], []
        for k in range(1, N_DEV):
            px = 1 - x if k & 4 else x
            py = 1 - y if k & 2 else y
            pc = 1 - c if k & 1 else c
            for i in range(n):
                sent.append(pltpu.make_async_remote_copy(
                    src_ref=ins[i], dst_ref=bufs[i].at[me], send_sem=send_sems.at[i, k - 1],
                    recv_sem=recv_sems.at[i, k - 1], device_id=(px, py, pc), device_id_type=MESH))
                landed.append(pltpu.make_async_remote_copy(
                    src_ref=ins[i], dst_ref=bufs[i].at[4 * px + 2 * py + pc], send_sem=send_sems.at[i, k - 1],
                    recv_sem=recv_sems.at[i, k - 1], device_id=(x, y, c), device_id_type=MESH))
        for cp in sent:
            cp.start()
        for cp in landed:
            cp.wait_recv()
        for cp in sent:
            cp.wait_send()
        for i in range(n):
            acc = bufs[i][0]
            for j in range(1, N_DEV):
                acc = acc + bufs[i][j]
            outs[i][...] = acc

    vmem = pl.BlockSpec(memory_space=pltpu.VMEM)
    return pl.pallas_call(
        body, name="small_all_reduce", out_shape=[jax.ShapeDtypeStruct(v.shape, F32) for v in vals],
        in_specs=[vmem] * n, out_specs=[vmem] * n,
        scratch_shapes=[pltpu.VMEM((N_DEV,) + v.shape, F32) for v in vals]
        + [pltpu.SemaphoreType.DMA((n, N_DEV - 1)), pltpu.SemaphoreType.DMA((n, N_DEV - 1))],
    )(*vals)


def _adamw(ws, ms, vs, gs, name):
    count = len(ws)
    parts = ws[0].ndim == 3
    steps = 4 if all(w.shape[-2] % 32 == 0 for w in ws) else 1

    def body(*refs):
        for n in range(count):
            w_ref, m_ref, v_ref, g_ref = refs[4 * n:4 * n + 4]
            go_ref, d_ref, mo_ref, vo_ref = refs[4 * count + 4 * n:4 * count + 4 * n + 4]
            if parts:
                gv = g_ref[0].astype(F32)
                for k in range(1, 4):
                    gv = gv + g_ref[k].astype(F32)
                gv = gv[None]
            else:
                gv = g_ref[...]
            m2 = ADAM_B1 * m_ref[...] + (1.0 - ADAM_B1) * gv
            v2 = ADAM_B2 * v_ref[...] + (1.0 - ADAM_B2) * (gv * gv)
            m_hat = m2 / (1.0 - ADAM_B1 ** ADAM_STEP)
            v_hat = v2 / (1.0 - ADAM_B2 ** ADAM_STEP)
            go_ref[...] = gv
            d_ref[...] = -ADAM_LR * (m_hat / (jnp.sqrt(v_hat) + ADAM_EPS) + ADAM_WD * w_ref[...])
            mo_ref[...] = m2
            vo_ref[...] = v2

    in_specs, out_specs, out_shape = [], [], []
    for w in ws:
        r, cdim = w.shape[-2:]
        if parts:
            row = pl.BlockSpec((1, r // steps, cdim), lambda i: (0, i, 0))
            g_spec = pl.BlockSpec((4, r // steps, cdim), lambda i: (0, i, 0))
        else:
            row = g_spec = pl.BlockSpec((r // steps, cdim), lambda i: (i, 0))
        in_specs += [row, row, row, g_spec]
        out_specs += [row] * 4
        out_shape += [jax.ShapeDtypeStruct(w.shape, F32)] * 4
    args = [a for group in zip(ws, ms, vs, gs) for a in group]
    out = pl.pallas_call(
        body, name=name, grid=(steps,), in_specs=in_specs, out_specs=out_specs, out_shape=out_shape,
        compiler_params=_params("parallel"),
    )(*args)
    return [out[4 * n:4 * n + 4] for n in range(count)]


def _adamw_small(ws, ms, vs, gs):
    count = len(ws)

    def body(*refs):
        for n in range(count):
            w_ref, m_ref, v_ref, g_ref = refs[4 * n:4 * n + 4]
            d_ref, mo_ref, vo_ref = refs[4 * count + 3 * n:4 * count + 3 * n + 3]
            gv = g_ref[...]
            m2 = ADAM_B1 * m_ref[...] + (1.0 - ADAM_B1) * gv
            v2 = ADAM_B2 * v_ref[...] + (1.0 - ADAM_B2) * (gv * gv)
            m_hat = m2 / (1.0 - ADAM_B1 ** ADAM_STEP)
            v_hat = v2 / (1.0 - ADAM_B2 ** ADAM_STEP)
            d_ref[...] = -ADAM_LR * (m_hat / (jnp.sqrt(v_hat) + ADAM_EPS) + ADAM_WD * w_ref[...])
            mo_ref[...] = m2
            vo_ref[...] = v2

    out = pl.pallas_call(
        body, name="small_adamw",
        out_shape=[jax.ShapeDtypeStruct(w.shape, F32) for w in ws for _ in range(3)],
    )(*[a for group in zip(ws, ms, vs, gs) for a in group])
    return [out[3 * n:3 * n + 3] for n in range(count)]


WEIGHTS = ["ffn1_norm_g", "ffn1_w_gate", "ffn1_w_up", "ffn1_w_down", "mix_norm_g", "w_in", "attn_q_norm_g",
           "attn_k_norm_g", "attn_rel_bias", "hgrn_lower_bounds", "hgrn_out_norm_g", "w_out", "ffn2_norm_g",
           "ffn2_w_gate", "ffn2_w_up", "ffn2_w_down"]
COL_SHARDED = ("ffn1_w_gate", "ffn1_w_up", "w_in", "ffn2_w_gate", "ffn2_w_up")
ROW_SHARDED = ("ffn1_w_down", "w_out", "ffn2_w_down")
BIG = [n for n in WEIGHTS if n in COL_SHARDED or n in ROW_SHARDED]
SMALL = [n for n in WEIGHTS if n not in BIG]
FFN2 = ["ffn2_w_down", "ffn2_w_gate", "ffn2_w_up"]
MIXER = ["w_out", "w_in"]

PLAN = {
    "ffn1_norm": [("gather", ["ffn1_w_gate"])],
    "bias_expand": [("gather", ["ffn1_w_up"])],
    "ffn1_up": [("gather", ["ffn1_w_down", "w_out"])],
    "ffn1_down": [("gather", ["w_in"])],
    "mixer_fwd": [("gather", FFN2)],
    "ffn2_dh_gate": [("pair", FFN2)],
    "attn_bwd": [("chip", FFN2)],
    "in_proj_dx": [("pair", MIXER)],
    "ffn1_bwd_mid": [("chip", MIXER)],
    "ffn1_dwg": [("pair", ["ffn1_w_down"])],
    "ffn1_dwu": [("chip", ["ffn1_w_down"]), ("pair", ["ffn1_w_gate"])],
    "ffn1_dh_gate": [("chip", ["ffn1_w_gate"]), ("pair", ["ffn1_w_up"])],
    "bias_fold": [("chip", ["ffn1_w_up"])],
}


def _join_sides(sides):
    def split(refs, counts):
        out, at = [], 0
        for n in counts:
            out.append(refs[at:at + n])
            at += n
        return out

    n_in, n_out, n_sem = ([len(getattr(s, f)) for s in sides] for f in ("ins", "out_shape", "sems"))

    def run(which):
        def go(ins, outs, sems):
            for s, i, o, m in zip(sides, split(ins, n_in), split(outs, n_out), split(sems, n_sem)):
                if getattr(s, which) is not None:
                    getattr(s, which)(i, o, m)
        return go

    return _Side([a for s in sides for a in s.ins], [a for s in sides for a in s.out_shape],
                 [a for s in sides for a in s.sems], run("start"), run("finish"),
                 run("middle") if any(s.middle is not None for s in sides) else None)


class _Schedule:
    def __init__(self, shards):
        self.shards = shards
        self.weights = {}
        self.sliced = {}
        self.partials = {}
        self.reduced = {}

    def put(self, name, grad):
        self.sliced[name] = grad.reshape((N_DEV,) + self.shards[name].shape)

    def side_for(self, call):
        if call not in PLAN:
            return None
        sides = []
        for kind, names in PLAN[call]:
            if kind == "gather":
                sides.append(_gather_side([self.shards[n] for n in names]))
            elif kind == "pair":
                sides.append(_pair_side([self.sliced[n] for n in names]))
            else:
                sides.append(_chip_side([self.partials[n] for n in names]))
        return _join_sides(sides)

    def done(self, call, outs):
        at = 0
        for kind, names in PLAN[call]:
            self.file(kind, names, outs[at:at + len(names)])
            at += len(names)

    def file(self, kind, names, outs):
        if kind == "pair":
            core = lax.axis_index("c").astype(jnp.int32).reshape(1)
            sums = _pair_add([self.sliced[n] for n in names], list(outs), core, names[0] + "_pair_add")
            self.partials.update(dict(zip(names, sums)))
            return
        for n, o in zip(names, outs):
            if kind == "gather":
                self.weights[n] = o.reshape(N_DEV * o.shape[1], o.shape[2])
            else:
                self.reduced[n] = o


def kernel(x, ffn1_norm_g, ffn1_w_gate, ffn1_w_up, ffn1_w_down, mix_norm_g, w_in, attn_q_norm_g, attn_k_norm_g, attn_rel_bias, hgrn_lower_bounds, hgrn_out_norm_g, w_out, ffn2_norm_g, ffn2_w_gate, ffn2_w_up, ffn2_w_down, loss_target, m_ffn1_norm_g, m_ffn1_w_gate, m_ffn1_w_up, m_ffn1_w_down, m_mix_norm_g, m_w_in, m_attn_q_norm_g, m_attn_k_norm_g, m_attn_rel_bias, m_hgrn_lower_bounds, m_hgrn_out_norm_g, m_w_out, m_ffn2_norm_g, m_ffn2_w_gate, m_ffn2_w_up, m_ffn2_w_down, v_ffn1_norm_g, v_ffn1_w_gate, v_ffn1_w_up, v_ffn1_w_down, v_mix_norm_g, v_w_in, v_attn_q_norm_g, v_attn_k_norm_g, v_attn_rel_bias, v_hgrn_lower_bounds, v_hgrn_out_norm_g, v_w_out, v_ffn2_norm_g, v_ffn2_w_gate, v_ffn2_w_up, v_ffn2_w_down):
    wts = dict(zip(WEIGHTS, (ffn1_norm_g, ffn1_w_gate, ffn1_w_up, ffn1_w_down, mix_norm_g, w_in, attn_q_norm_g,
                             attn_k_norm_g, attn_rel_bias, hgrn_lower_bounds, hgrn_out_norm_g, w_out, ffn2_norm_g,
                             ffn2_w_gate, ffn2_w_up, ffn2_w_down)))
    mom = dict(zip(WEIGHTS, (m_ffn1_norm_g, m_ffn1_w_gate, m_ffn1_w_up, m_ffn1_w_down, m_mix_norm_g, m_w_in,
                             m_attn_q_norm_g, m_attn_k_norm_g, m_attn_rel_bias, m_hgrn_lower_bounds,
                             m_hgrn_out_norm_g, m_w_out, m_ffn2_norm_g, m_ffn2_w_gate, m_ffn2_w_up, m_ffn2_w_down)))
    var = dict(zip(WEIGHTS, (v_ffn1_norm_g, v_ffn1_w_gate, v_ffn1_w_up, v_ffn1_w_down, v_mix_norm_g, v_w_in,
                             v_attn_q_norm_g, v_attn_k_norm_g, v_attn_rel_bias, v_hgrn_lower_bounds,
                             v_hgrn_out_norm_g, v_w_out, v_ffn2_norm_g, v_ffn2_w_gate, v_ffn2_w_up, v_ffn2_w_down)))
    nb, seq, d = x.shape
    shapes = {n: wts[n].shape for n in WEIGHTS}

    def rows_first(a, n):
        return jnp.swapaxes(a, 1, 2) if n in COL_SHARDED else a

    sched = _Schedule({n: rows_first(wts[n], n)[0].astype(BF16) for n in BIG})
    sp = {n: wts[n] for n in SMALL}
    sp["attn_rel_bias"] = wts["attn_rel_bias"][0]
    _ACTIVE[0] = sched
    try:
        loss, dx, dsmall = _local_step(x.reshape(nb * seq, d), loss_target.reshape(nb * seq, d), sp,
                                       sched.weights, sched.put, nb, seq)
    finally:
        _ACTIVE[0] = None
    reduced = sched.reduced

    sums = _all_reduce_small([dsmall[n] for n in SMALL] + [jnp.full((1, 128), loss, F32)])
    gsmall = {n: s.reshape(shapes[n]) for n, s in zip(SMALL, sums)}
    loss_total = sums[-1][0, 0]

    grads, deltas, new_m, new_v = {}, {}, {}, {}
    for group, tag in (([n for n in BIG if n not in MIXER], "ffn_adamw"), (MIXER, "mixer_adamw")):
        outs = _adamw([rows_first(wts[n], n) for n in group], [rows_first(mom[n], n) for n in group],
                      [rows_first(var[n], n) for n in group], [reduced[n] for n in group], tag)
        for n, out in zip(group, outs):
            grads[n], deltas[n], new_m[n], new_v[n] = (rows_first(o, n) for o in out)
    outs = _adamw_small([wts[n] for n in SMALL], [mom[n] for n in SMALL], [var[n] for n in SMALL],
                        [gsmall[n] for n in SMALL])
    for n, (delta, m2, v2) in zip(SMALL, outs):
        deltas[n], new_m[n], new_v[n] = delta, m2, v2
    grads.update(gsmall)

    return (loss_total, dx.reshape(nb, seq, d), *[grads[n] for n in WEIGHTS], *[deltas[n] for n in WEIGHTS],
            *[new_m[n] for n in WEIGHTS], *[new_v[n] for n in WEIGHTS])
```

```python
import functools

import jax
import jax.numpy as jnp
from jax import lax
from jax.experimental import pallas as pl
from jax.experimental.pallas import tpu as pltpu

F32 = jnp.float32
BF16 = jnp.bfloat16

RMS_EPS = 1e-6
CHUNK = 64
LEFT_CHUNKS = 8
BAND = (LEFT_CHUNKS + 2) * CHUNK
KPAD = BAND - CHUNK
REL_CLIP = 128
N_REL = 2 * REL_CLIP + 1
N_REL_PAD = 384
ATTN_HEADS = 8
ATTN_HEAD_DIM = 64
ATTN_WIDTH = ATTN_HEADS * ATTN_HEAD_DIM
ATTN_LOCKSTEP = 4
ATTN_UNROLL = 32
HGRN_HEADS = 4
HGRN_HEAD_DIM = 128
HGRN_ROWS = 512
SUB = 16
N_SUB = CHUNK // SUB
DIAG_STAGE = 4
N_DEV = 8

ADAM_LR = 0.001
ADAM_B1 = 0.9
ADAM_B2 = 0.999
ADAM_EPS = 1e-08
ADAM_WD = 0.01
ADAM_STEP = 10

VMEM_LIMIT = 56 * 1024 * 1024

NT = (((1,), (1,)), ((), ()))
NN = (((1,), (0,)), ((), ()))


def _params(*sem):
    return pltpu.CompilerParams(dimension_semantics=sem, vmem_limit_bytes=VMEM_LIMIT)


def _sigmoid(v):
    return 0.5 * jnp.tanh(0.5 * v) + 0.5


def _dot(a, b, dims=NN):
    return lax.dot_general(a.astype(BF16), b.astype(BF16), dims, preferred_element_type=F32)


def _dot_exact01(m01, v):
    m = m01.astype(BF16)
    hi = v.astype(BF16)
    r1 = v - hi.astype(F32)
    mid = r1.astype(BF16)
    lo = (r1 - mid.astype(F32)).astype(BF16)
    out = lax.dot_general(m, hi, NN, preferred_element_type=F32)
    out = out + lax.dot_general(m, mid, NN, preferred_element_type=F32)
    return out + lax.dot_general(m, lo, NN, preferred_element_type=F32)


def _dot_exact01_r(v, m01):
    m = m01.astype(BF16)
    hi = v.astype(BF16)
    r1 = v - hi.astype(F32)
    mid = r1.astype(BF16)
    lo = (r1 - mid.astype(F32)).astype(BF16)
    out = lax.dot_general(hi, m, NN, preferred_element_type=F32)
    out = out + lax.dot_general(mid, m, NN, preferred_element_type=F32)
    return out + lax.dot_general(lo, m, NN, preferred_element_type=F32)


def _lockstep(stages):
    live = list(stages)
    while live:
        still = []
        for g in live:
            try:
                next(g)
                still.append(g)
            except StopIteration:
                pass
        live = still


def _row_sums_on_lanes(v):
    ones = jnp.ones((8, v.shape[1]), BF16)
    hi = v.astype(BF16)
    r1 = v - hi.astype(F32)
    mid = r1.astype(BF16)
    lo = (r1 - mid.astype(F32)).astype(BF16)
    out = lax.dot_general(ones, hi, NT, preferred_element_type=F32)
    out = out + lax.dot_general(ones, mid, NT, preferred_element_type=F32)
    return (out + lax.dot_general(ones, lo, NT, preferred_element_type=F32))[0:1, :]


def _tn(a, b):
    ap = jnp.concatenate([a, jnp.zeros_like(a)], axis=0)
    bp = jnp.concatenate([b, jnp.zeros_like(b)], axis=0)
    return _dot(ap.T, bp)


def _row_tile(t):
    for tm in (512, 256, 128, 64, 32, 16, 8):
        if t % tm == 0:
            return tm
    raise ValueError(t)


class _Side:
    def __init__(self, ins, out_shape, sems, start, finish, middle=None):
        self.ins, self.out_shape, self.sems = ins, out_shape, sems
        self.start, self.middle, self.finish = start, middle, finish


_ACTIVE = [None]


def _pallas(body, *, name, grid, in_specs, out_specs, out_shape, scratch_shapes=(), sem, args):
    sched = _ACTIVE[0]
    side = sched.side_for(name) if sched is not None else None
    if side is None:
        return pl.pallas_call(
            body, name=name, grid=grid, in_specs=list(in_specs), out_specs=list(out_specs),
            out_shape=list(out_shape), scratch_shapes=list(scratch_shapes), compiler_params=_params(*sem))(*args)
    cuts = [len(in_specs), len(side.ins), len(out_shape), len(side.out_shape), len(scratch_shapes)]

    def with_side(*refs):
        groups, at = [], 0
        for n in cuts:
            groups.append(refs[at:at + n])
            at += n
        ins, side_ins, outs, side_outs, scratch = groups
        side_sems = refs[at:]
        step, total = pl.program_id(0), grid[0]
        for a in range(1, len(grid)):
            step, total = step * grid[a] + pl.program_id(a), total * grid[a]
        has_middle = side.middle is not None and total >= 3

        @pl.when(step == 0)
        def _():
            side.start(side_ins, side_outs, side_sems)

        if has_middle:
            @pl.when(step == total // 2)
            def _():
                side.middle(side_ins, side_outs, side_sems)

        body(*ins, *outs, *scratch)

        @pl.when(step == total - 1)
        def _():
            if side.middle is not None and not has_middle:
                side.middle(side_ins, side_outs, side_sems)
            side.finish(side_ins, side_outs, side_sems)

    hbm = pl.BlockSpec(memory_space=pl.ANY)
    res = pl.pallas_call(
        with_side, name=name, grid=grid, in_specs=list(in_specs) + [hbm] * len(side.ins),
        out_specs=list(out_specs) + [hbm] * len(side.out_shape), out_shape=list(out_shape) + list(side.out_shape),
        scratch_shapes=list(scratch_shapes) + list(side.sems),
        compiler_params=_params(*(["arbitrary"] * len(grid))))(*args, *side.ins)
    sched.done(name, res[len(out_shape):])
    return res[:len(out_shape)]


def _rms_fwd(x, g, name):
    t, d = x.shape
    tm = _row_tile(t)

    def body(x_ref, g_ref, h_ref):
        xv = x_ref[...]
        r = lax.rsqrt(jnp.mean(xv * xv, axis=-1, keepdims=True) + RMS_EPS)
        h_ref[...] = (xv * r * g_ref[...]).astype(BF16)

    return _pallas(
        body, name=name, grid=(t // tm,),
        in_specs=[pl.BlockSpec((tm, d), lambda i: (i, 0)), pl.BlockSpec((1, d), lambda i: (0, 0))],
        out_specs=[pl.BlockSpec((tm, d), lambda i: (i, 0))], out_shape=[jax.ShapeDtypeStruct((t, d), BF16)],
        sem=("parallel",), args=(x, g))[0]


def _accumulate(ref, part, step):
    @pl.when(step == 0)
    def _():
        ref[...] = part

    @pl.when(step > 0)
    def _():
        ref[...] += part


def _mm(a, b, *, ta=False, tb=False, tm, tn, out_dtype=F32, add=None, scale=1.0, norm_g=None, norm_bwd=None, name):
    m, k = (a.shape[1], a.shape[0]) if ta else a.shape
    n = b.shape[0] if tb else b.shape[1]
    tm, tn = min(tm, m), min(tn, n)
    assert m % tm == 0 and n % tn == 0, (m, n, tm, tn)
    assert (norm_g is None and norm_bwd is None) or tn == n
    dims = (((0 if ta else 1,), (1 if tb else 0,)), ((), ()))
    n_in = 2 + (add is not None) + (norm_g is not None) + (3 if norm_bwd is not None else 0)

    def body(*refs):
        ins, outs = list(refs[2:n_in]), refs[n_in:]
        r = lax.dot_general(refs[0][...].astype(BF16), refs[1][...].astype(BF16), dims, preferred_element_type=F32)
        if scale != 1.0:
            r = r * scale
        if add is not None:
            r = r + ins.pop(0)[...]
        if norm_bwd is not None:
            xv, gv, dres = (ref[...] for ref in ins)
            rs = lax.rsqrt(jnp.mean(xv * xv, axis=-1, keepdims=True) + RMS_EPS)
            xhat = xv * rs
            gd = r * gv
            dx = dres + rs * (gd - xhat * jnp.mean(gd * xhat, axis=-1, keepdims=True))
            outs[0][...] = dx
            outs[1][...] = dx.astype(BF16)
            _accumulate(outs[2], jnp.sum(r * xhat, axis=0, keepdims=True), pl.program_id(0))
            return
        outs[0][...] = r.astype(out_dtype)
        if norm_g is not None:
            rs = lax.rsqrt(jnp.mean(r * r, axis=-1, keepdims=True) + RMS_EPS)
            outs[1][...] = (r * rs * ins.pop(0)[...]).astype(BF16)

    a_spec = pl.BlockSpec((k, tm), lambda i, j: (0, i)) if ta else pl.BlockSpec((tm, k), lambda i, j: (i, 0))
    b_spec = pl.BlockSpec((tn, k), lambda i, j: (j, 0)) if tb else pl.BlockSpec((k, tn), lambda i, j: (0, j))
    o_spec = pl.BlockSpec((tm, tn), lambda i, j: (i, j))
    vec = pl.BlockSpec((1, tn), lambda i, j: (0, j))
    args, specs = [a, b], [a_spec, b_spec]
    if add is not None:
        args.append(add)
        specs.append(o_spec)
    out_specs, out_shape = [o_spec], [jax.ShapeDtypeStruct((m, n), out_dtype)]
    if norm_g is not None:
        args.append(norm_g)
        specs.append(vec)
        out_specs.append(o_spec)
        out_shape.append(jax.ShapeDtypeStruct((m, n), BF16))
    if norm_bwd is not None:
        args += list(norm_bwd)
        specs += [o_spec, vec, o_spec]
        out_specs = [o_spec, o_spec, vec]
        out_shape = [jax.ShapeDtypeStruct((m, n), F32), jax.ShapeDtypeStruct((m, n), BF16),
                     jax.ShapeDtypeStruct((1, n), F32)]
    res = _pallas(body, name=name, grid=(m // tm, n // tn), in_specs=specs, out_specs=out_specs, out_shape=out_shape,
                  sem=("arbitrary", "arbitrary") if norm_bwd is not None else ("parallel", "parallel"), args=args)
    return res[0] if len(res) == 1 else res


def _ffn_tile(f):
    for tf in (1408, 512, 256, 128):
        if f % tf == 0:
            return tf
    raise ValueError(f)


def _ffn_fwd(h, x, wg, wu, wd, name, next_g=None, tgt=None):
    t, d = x.shape
    f = wg.shape[0]
    tm, tf = _row_tile(t), _ffn_tile(f)
    nf = f // tf
    assert (next_g is None) != (tgt is None)

    def body(h_ref, x_ref, wg_ref, wu_ref, wd_ref, tail_ref, g_ref, u_ref, o0_ref, o1_ref, *rest):
        acc_ref = rest[-1]
        j = pl.program_id(1)
        hv = h_ref[...]
        gv = lax.dot_general(hv, wg_ref[...], NT, preferred_element_type=F32)
        uv = lax.dot_general(hv, wu_ref[...], NT, preferred_element_type=F32)
        av = gv * _sigmoid(gv) * uv
        g_ref[...] = gv.astype(BF16)
        u_ref[...] = uv.astype(BF16)
        _accumulate(acc_ref, lax.dot_general(av.astype(BF16), wd_ref[...], NN, preferred_element_type=F32), j)

        @pl.when(j == nf - 1)
        def _():
            y = x_ref[...] + 0.5 * acc_ref[...]
            if tgt is None:
                o0_ref[...] = y
                rs = lax.rsqrt(jnp.mean(y * y, axis=-1, keepdims=True) + RMS_EPS)
                o1_ref[...] = (y * rs * tail_ref[...]).astype(BF16)
            else:
                e = y - tail_ref[...]
                dy = e * (1.0 / d)
                o0_ref[...] = dy
                o1_ref[...] = dy.astype(BF16)
                _accumulate(rest[0], jnp.sum(e * e, axis=0, keepdims=True), pl.program_id(0))

    row = pl.BlockSpec((tm, d), lambda i, j: (i, 0))
    hid = pl.BlockSpec((tm, tf), lambda i, j: (i, j))
    vec = pl.BlockSpec((1, d), lambda i, j: (0, 0))
    out_specs = [hid, hid, row, row] + ([vec] if tgt is not None else [])
    out_shape = [jax.ShapeDtypeStruct((t, f), BF16)] * 2 + [jax.ShapeDtypeStruct((t, d), F32),
                                                            jax.ShapeDtypeStruct((t, d), BF16)]
    if tgt is not None:
        out_shape.append(jax.ShapeDtypeStruct((1, d), F32))
    return _pallas(
        body, name=name, grid=(t // tm, nf),
        in_specs=[row, row] + [pl.BlockSpec((tf, d), lambda i, j: (j, 0))] * 3 + [vec if tgt is None else row],
        out_specs=out_specs, out_shape=out_shape, scratch_shapes=[pltpu.VMEM((tm, d), F32)],
        sem=("parallel" if tgt is None else "arbitrary", "arbitrary"),
        args=(h, x, wg, wu, wd, next_g if tgt is None else tgt))


def _ffn_up(h, wg, wu, name):
    t, d = h.shape
    f = wg.shape[0]
    tm, tf = _row_tile(t), _ffn_tile(f)

    def body(h_ref, wg_ref, wu_ref, g_ref, u_ref, a_ref):
        hv = h_ref[...]
        gv = lax.dot_general(hv, wg_ref[...], NT, preferred_element_type=F32)
        uv = lax.dot_general(hv, wu_ref[...], NT, preferred_element_type=F32)
        g_ref[...] = gv.astype(BF16)
        u_ref[...] = uv.astype(BF16)
        a_ref[...] = (gv * _sigmoid(gv) * uv).astype(BF16)

    hid = pl.BlockSpec((tm, tf), lambda i, j: (i, j))
    wrow = pl.BlockSpec((tf, d), lambda i, j: (j, 0))
    return _pallas(
        body, name=name, grid=(t // tm, f // tf), in_specs=[pl.BlockSpec((tm, d), lambda i, j: (i, 0)), wrow, wrow],
        out_specs=[hid, hid, hid], out_shape=[jax.ShapeDtypeStruct((t, f), BF16)] * 3,
        sem=("parallel", "parallel"), args=(h, wg, wu))


def _ffn_bwd_mid(dy, wd, g, u, name):
    t, d = dy.shape
    f = wd.shape[0]
    tm, tf = _row_tile(t), _ffn_tile(f)

    def body(dy_ref, wd_ref, g_ref, u_ref, dg_ref, du_ref, dwd_ref):
        dy16 = dy_ref[...]
        da = 0.5 * lax.dot_general(dy16, wd_ref[...], NT, preferred_element_type=F32)
        gv = g_ref[...].astype(F32)
        uv = u_ref[...].astype(F32)
        s = _sigmoid(gv)
        silu = gv * s
        dg_ref[...] = (da * uv * (s * (1.0 + gv * (1.0 - s)))).astype(BF16)
        du_ref[...] = (da * silu).astype(BF16)
        part = 0.5 * lax.dot_general((silu * uv).astype(BF16), dy16, (((0,), (0,)), ((), ())),
                                     preferred_element_type=F32)
        _accumulate(dwd_ref, part, pl.program_id(1))

    hid = pl.BlockSpec((tm, tf), lambda j, i: (i, j))
    wrow = pl.BlockSpec((tf, d), lambda j, i: (j, 0))
    return _pallas(
        body, name=name, grid=(f // tf, t // tm),
        in_specs=[pl.BlockSpec((tm, d), lambda j, i: (i, 0)), wrow, hid, hid],
        out_specs=[hid, hid, wrow],
        out_shape=[jax.ShapeDtypeStruct((t, f), BF16)] * 2 + [jax.ShapeDtypeStruct((f, d), F32)],
        sem=("parallel", "arbitrary"), args=(dy, wd, g, u))


def _rel_index(t, s_band):
    return jnp.clip(t + KPAD - s_band, -REL_CLIP, REL_CLIP) + REL_CLIP


def _bias_expand(rel_bias_pad):
    nh = rel_bias_pad.shape[0]

    def body(rb_ref, out_ref):
        rb = rb_ref[...]
        i_io = lax.broadcasted_iota(jnp.int32, (N_REL_PAD, BAND), 0)
        s_io = lax.broadcasted_iota(jnp.int32, (N_REL_PAD, BAND), 1)

        def row(r, carry):
            onehot = (i_io == _rel_index(pl.program_id(0) * rows + r, s_io)).astype(F32)
            out_ref[r] = _dot_exact01_r(rb, onehot)
            return carry

        lax.fori_loop(0, rows, row, 0)

    rows = 8
    return _pallas(
        body, name="bias_expand", grid=(CHUNK // rows,),
        in_specs=[pl.BlockSpec(rel_bias_pad.shape, lambda i: (0, 0))],
        out_specs=[pl.BlockSpec((rows, nh, BAND), lambda i: (i, 0, 0))],
        out_shape=[jax.ShapeDtypeStruct((CHUNK, nh, BAND), F32)], sem=("arbitrary",), args=(rel_bias_pad,))[0]


def _bias_fold(dbias):
    ng, nh = dbias.shape[0], dbias.shape[2]

    def body(db_ref, out_ref):
        s_io = lax.broadcasted_iota(jnp.int32, (BAND, N_REL_PAD), 0)
        i_io = lax.broadcasted_iota(jnp.int32, (BAND, N_REL_PAD), 1)

        def row(t, acc):
            onehot = (i_io == _rel_index(t, s_io)).astype(F32)
            d = db_ref[0, t]
            for gi in range(1, ng):
                d = d + db_ref[gi, t]
            return acc + _dot_exact01_r(d, onehot)

        out_ref[...] = lax.fori_loop(0, CHUNK, row, jnp.zeros((nh, N_REL_PAD), F32))

    return _pallas(
        body, name="bias_fold", grid=(1,), in_specs=[pl.BlockSpec(dbias.shape, lambda i: (0, 0, 0, 0))],
        out_specs=[pl.BlockSpec((nh, N_REL_PAD), lambda i: (0, 0))],
        out_shape=[jax.ShapeDtypeStruct((nh, N_REL_PAD), F32)], sem=("arbitrary",), args=(dbias,))[0]


def _left_half(shape):
    return lax.broadcasted_iota(jnp.int32, shape, len(shape) - 1) < ATTN_HEAD_DIM


def _stack_heads(v):
    left = _left_half(v.shape)
    zero = jnp.zeros_like(v)
    return jnp.concatenate([jnp.where(left, v, zero), jnp.where(left, zero, v)], axis=0)


def _unstack_heads(v):
    return jnp.where(_left_half((CHUNK, 128)), v[0:CHUNK, :], v[CHUNK:2 * CHUNK, :])


def _half_mean(v):
    r = lax.broadcasted_iota(jnp.int32, (128, 128), 0) < ATTN_HEAD_DIM
    c = lax.broadcasted_iota(jnp.int32, (128, 128), 1) < ATTN_HEAD_DIM
    return _dot_exact01_r(v, r == c) * (1.0 / ATTN_HEAD_DIM)


def _attn_prepare(q_ref, k_ref, v_ref, gq_ref, gk_ref, qs_scr, k_scr, v_scr):
    q, k = q_ref[...], k_ref[...]
    rq = lax.rsqrt(_half_mean(q * q) + RMS_EPS)
    rk = lax.rsqrt(_half_mean(k * k) + RMS_EPS)
    qhat, khat = q * rq, k * rk
    qs_scr[...] = (qhat * gq_ref[...] * ATTN_HEAD_DIM ** -0.5).astype(BF16)
    k_scr[0:KPAD, :] = jnp.zeros((KPAD, 128), BF16)
    v_scr[0:KPAD, :] = jnp.zeros((KPAD, 128), BF16)
    k_scr[KPAD:, :] = (khat * gk_ref[...]).astype(BF16)
    v_scr[KPAD:, :] = v_ref[...].astype(BF16)
    return qhat, rq, khat, rk


def _first_key(c):
    return jnp.maximum(CHUNK, (LEFT_CHUNKS + 1 - c) * CHUNK)


def _attn_fwd_chunk(c, qs_scr, k_scr, v_scr, bias_ref, o_ref):
    r0 = pl.multiple_of(c * CHUNK, CHUNK)
    s = lax.dot_general(_stack_heads(qs_scr[pl.ds(r0, CHUNK), :]), k_scr[pl.ds(r0, BAND), :], NT,
                        preferred_element_type=F32)
    yield
    col = lax.broadcasted_iota(jnp.int32, (2 * CHUNK, BAND), 1)
    s = jnp.where(col >= _first_key(c), s + bias_ref[...], -jnp.inf)
    m = jnp.max(s, axis=-1, keepdims=True)
    yield
    e = jnp.exp(s - m)
    yield
    inv = 1.0 / jnp.sum(e, axis=-1, keepdims=True)
    o = lax.dot_general(e.astype(BF16), v_scr[pl.ds(r0, BAND), :], NN, preferred_element_type=F32)
    yield
    o_ref[pl.ds(r0, CHUNK), :] = _unstack_heads(o * inv)


def _attn_bwd(proj, out, dout, bias, gq, gk, nb, seq):
    nc = seq // CHUNK
    lock = min(ATTN_LOCKSTEP, nc)
    assert nc % lock == 0
    scale = ATTN_HEAD_DIM ** -0.5

    def body(q_ref, k_ref, v_ref, o_ref, do_ref, bias_ref, gq_ref, gk_ref,
             dq_ref, dk_ref, dv_ref, dbias_ref, dgq_ref, dgk_ref,
             qs_scr, k_scr, v_scr, dqn_scr, dk_scr, dv_scr, db_scr):
        qhat, rq, khat, rk = _attn_prepare(q_ref, k_ref, v_ref, gq_ref, gk_ref, qs_scr, k_scr, v_scr)
        dk_scr[...] = jnp.zeros_like(dk_scr)
        dv_scr[...] = jnp.zeros_like(dv_scr)
        db_scr[...] = jnp.zeros_like(db_scr)

        def one_chunk(c):
            r0 = pl.multiple_of(c * CHUNK, CHUNK)
            qst = _stack_heads(qs_scr[pl.ds(r0, CHUNK), :])
            kb = k_scr[pl.ds(r0, BAND), :]
            vb = v_scr[pl.ds(r0, BAND), :]
            st = lax.dot_general(kb, qst, NT, preferred_element_type=F32) + bias_ref[...]
            dost = _stack_heads(do_ref[pl.ds(r0, CHUNK), :])
            dost16 = dost.astype(BF16)
            dpt = lax.dot_general(vb, dost16, NT, preferred_element_type=F32)
            yield
            key = lax.broadcasted_iota(jnp.int32, (BAND, 2 * CHUNK), 0)
            st = jnp.where(key >= _first_key(c), st, -jnp.inf)
            mx = jnp.max(st, axis=0, keepdims=True)
            drow = _row_sums_on_lanes(dost * _stack_heads(o_ref[pl.ds(r0, CHUNK), :]))
            yield
            et = jnp.exp(st - mx)
            yield
            pt = et * (1.0 / jnp.sum(et, axis=0, keepdims=True))
            yield
            dst = pt * (dpt - drow)
            dst16 = dst.astype(BF16)
            yield
            db_scr[...] += dst
            dqn_scr[pl.ds(r0, CHUNK), :] = scale * _unstack_heads(_dot(dst.T, kb))
            yield
            dk_scr[pl.ds(r0, BAND), :] += lax.dot_general(dst16, qst, NN, preferred_element_type=F32)
            yield
            dv_scr[pl.ds(r0, BAND), :] += lax.dot_general(pt.astype(BF16), dost16, NN, preferred_element_type=F32)

        def chunk(i, carry):
            _lockstep([one_chunk(i * lock + a) for a in range(lock)])
            return carry

        lax.fori_loop(0, nc // lock, chunk, 0, unroll=max(1, min(ATTN_UNROLL, nc) // lock))

        def norm_bwd(dn, hat, r, g_ref):
            gd = dn * g_ref[...]
            return r * (gd - hat * _half_mean(gd * hat)), jnp.sum(dn * hat, axis=0, keepdims=True)

        dq, dgq = norm_bwd(dqn_scr[...], qhat, rq, gq_ref)
        dk, dgk = norm_bwd(dk_scr[KPAD:, :], khat, rk, gk_ref)
        dq_ref[...] = dq.astype(BF16)
        dk_ref[...] = dk.astype(BF16)
        dv_ref[...] = dv_scr[KPAD:, :].astype(BF16)
        dbias_ref[0] = db_scr[...]
        dgq_ref[0] = dgq
        dgk_ref[0] = dgk

    def col(off):
        return pl.BlockSpec((seq, 128), lambda b, hp: (b, off + hp))

    vec = pl.BlockSpec((1, 128), lambda b, hp: (0, 0))
    gvec = pl.BlockSpec((1, 1, 128), lambda b, hp: (b * (ATTN_HEADS // 2) + hp, 0, 0))
    t = nb * seq
    return _pallas(
        body, name="attn_bwd", grid=(nb, ATTN_HEADS // 2),
        in_specs=[col(0), col(4), col(8), col(0), col(0),
                  pl.BlockSpec((BAND, 2 * CHUNK), lambda b, hp: (hp, 0)), vec, vec],
        out_specs=[col(0), col(0), col(0), pl.BlockSpec((1, BAND, 2 * CHUNK), lambda b, hp: (b, hp, 0)),
                   gvec, gvec],
        out_shape=[jax.ShapeDtypeStruct((t, ATTN_WIDTH), BF16)] * 3
        + [jax.ShapeDtypeStruct((nb, ATTN_HEADS // 2 * BAND, 2 * CHUNK), F32)]
        + [jax.ShapeDtypeStruct((nb * ATTN_HEADS // 2, 1, 128), F32)] * 2,
        scratch_shapes=[pltpu.VMEM((seq, 128), BF16), pltpu.VMEM((seq + KPAD, 128), BF16),
                        pltpu.VMEM((seq + KPAD, 128), BF16), pltpu.VMEM((seq, 128), F32),
                        pltpu.VMEM((seq + KPAD, 128), F32), pltpu.VMEM((seq + KPAD, 128), F32),
                        pltpu.VMEM((BAND, 2 * CHUNK), F32)],
        sem=("parallel", "parallel"), args=(proj, proj, proj, out, dout, bias, gq, gk))


def _tri(lower):
    r = lax.broadcasted_iota(jnp.int32, (CHUNK, CHUNK), 0)
    c = lax.broadcasted_iota(jnp.int32, (CHUNK, CHUNK), 1)
    return (r >= c) if lower else (r <= c)


def _hgrn_gates(hq, hf, lb):
    sq = _sigmoid(hq)
    sf = _sigmoid(hf)
    return hq * sq, sq, sf, lb + (1.0 - lb) * sf


def _hgrn_offdiag(q_s, k_s, b_s):
    row = lax.broadcasted_iota(jnp.int32, (CHUNK, HGRN_HEAD_DIM), 0)
    bv, qv, kv = b_s[...], q_s[...], k_s[...]
    eqs, eks = [], []
    for i in range(1, N_SUB):
        r = b_s[pl.ds(SUB * i - 1, 1), :]
        in_i = (row >= SUB * i) & (row < SUB * (i + 1))
        eqs.append(jnp.exp(jnp.where(in_i, bv - r, -jnp.inf)))
        eks.append(jnp.exp(jnp.where(row < SUB * i, r - bv, -jnp.inf)))
    eq = jnp.concatenate(eqs, axis=1)
    ek = jnp.concatenate(eks, axis=1)
    qt = jnp.concatenate([qv] * (N_SUB - 1), axis=1) * eq
    kt = jnp.concatenate([kv] * (N_SUB - 1), axis=1) * ek
    return qt, kt, eq, ek


def _hgrn_diag_e(b_s, i, s):
    t_io = lax.broadcasted_iota(jnp.int32, (SUB, HGRN_HEAD_DIM), 0)
    bi = b_s[pl.ds(SUB * i, SUB), :]
    return jnp.exp(jnp.where(t_io >= s, bi - b_s[pl.ds(SUB * i + s, 1), :], -jnp.inf)), t_io


def _hgrn_intra(q_s, k_s, b_s, a_s, qt, kt):
    ktp = jnp.concatenate([kt, jnp.zeros_like(kt)], axis=0)
    a_s[...] = _dot(qt, ktp, NT)
    yield
    col = lax.broadcasted_iota(jnp.int32, (SUB, HGRN_HEAD_DIM), 1)
    for i in range(N_SUB):
        qi = q_s[pl.ds(SUB * i, SUB), :]
        ai = jnp.zeros((SUB, HGRN_HEAD_DIM), F32)
        for s in range(SUB):
            e, _ = _hgrn_diag_e(b_s, i, s)
            a_col = jnp.sum(qi * k_s[pl.ds(SUB * i + s, 1), :] * e, axis=-1, keepdims=True)
            ai = ai + jnp.where(col == SUB * i + s, a_col, 0.0)
            if s % DIAG_STAGE == DIAG_STAGE - 1:
                yield
        a_s[pl.ds(SUB * i, SUB), :] += ai


def _mixer_fwd(proj, bias, gq, gk, lb, go, nb, seq):
    nc = seq // CHUNK
    hd = HGRN_HEAD_DIM
    nblk = ATTN_HEADS // 2
    rows_blk = seq // nblk
    nck = rows_blk // CHUNK
    per = nc // nck
    assert rows_blk % CHUNK == 0

    def body(aq_ref, ak_ref, av_ref, bias_ref, gq_ref, gk_ref, hq_ref, hf_ref, hi_ref, hg_ref, lb_ref, go_ref,
             ao_ref, y_ref, o_ref, st_ref, a_ref, qs_scr, k_scr, v_scr, st_all, q_all, k_all, b_all, a_all):
        _attn_prepare(aq_ref, ak_ref, av_ref, gq_ref, gk_ref, qs_scr, k_scr, v_scr)

        @pl.when(pl.program_id(1) == 0)
        def _():
            st_all[...] = jnp.zeros_like(st_all)

        lower = _tri(True)

        def head_chunk(hh, c, rows):
            ln = slice(hd * hh, hd * (hh + 1))
            st, q_s, k_s, b_s, a_s = st_all.at[hh], q_all.at[hh], k_all.at[hh], b_all.at[hh], a_all.at[hh]
            q, _, _, f = _hgrn_gates(hq_ref[rows, ln], hf_ref[rows, ln], lb_ref[:, ln])
            v = hi_ref[rows, ln]
            yield
            b = _dot_exact01(lower, jnp.log(f))
            q_s[...] = q
            k_s[...] = 1.0 - f
            b_s[...] = b
            st_ref[hh, c] = st[...]
            yield
            qt, kt, _, _ = _hgrn_offdiag(q_s, k_s, b_s)
            yield
            yield from _hgrn_intra(q_s, k_s, b_s, a_s, qt, kt)
            a16 = a_s[...].astype(BF16)
            a_ref[hh, c] = a16
            vp = jnp.concatenate([v, jnp.zeros_like(v)], axis=0)
            o = _dot(a16, vp) + _dot(q * jnp.exp(b), st[...], NT)
            yield
            bl = b_s[pl.ds(CHUNK - 1, 1), :]
            st[...] = st[...] * jnp.exp(bl) + _tn(v, (1.0 - f) * jnp.exp(bl - b))
            o_ref[rows, ln] = o
            yield
            n = o * lax.rsqrt(jnp.mean(o * o, axis=-1, keepdims=True) + RMS_EPS) * go_ref[...]
            hg = hg_ref[rows, ln]
            y_ref[rows, ln] = n * hg * _sigmoid(hg)

        def chunk(c, carry):
            rows = pl.ds(pl.multiple_of(c * CHUNK, CHUNK), CHUNK)
            _lockstep([_attn_fwd_chunk(c * per + a, qs_scr, k_scr, v_scr, bias_ref, ao_ref) for a in range(per)]
                      + [head_chunk(hh, c, rows) for hh in range(HGRN_HEADS)])
            return carry

        lax.fori_loop(0, nck, chunk, 0, unroll=min(8, nck))

    hp, wide = HGRN_HEADS, HGRN_HEADS * hd

    def acol(off):
        return pl.BlockSpec((seq, 128), lambda b, s: (b, off + s))

    def col(off):
        return pl.BlockSpec((rows_blk, wide), lambda b, s: (b * nblk + s, off // hp))

    out = pl.BlockSpec((rows_blk, wide), lambda b, s: (b * nblk + s, 0))
    vec = pl.BlockSpec((1, 128), lambda b, s: (0, 0))
    t = nb * seq
    return _pallas(
        body, name="mixer_fwd", grid=(nb, nblk),
        in_specs=[acol(0), acol(4), acol(8), pl.BlockSpec((2 * CHUNK, BAND), lambda b, s: (s, 0)), vec, vec,
                  col(12), col(16), col(20), col(24), pl.BlockSpec((1, wide), lambda b, s: (0, 0)), vec],
        out_specs=[pl.BlockSpec((seq, 128), lambda b, s: (b, s)), out, out,
                   pl.BlockSpec((hp, nck, hd, hd), lambda b, s: (b, s, 0, 0)),
                   pl.BlockSpec((hp, nck, CHUNK, hd), lambda b, s: (b, s, 0, 0))],
        out_shape=[jax.ShapeDtypeStruct((t, ATTN_WIDTH), F32)] + [jax.ShapeDtypeStruct((t, wide), F32)] * 2
        + [jax.ShapeDtypeStruct((nb * hp, nc, hd, hd), F32), jax.ShapeDtypeStruct((nb * hp, nc, CHUNK, hd), BF16)],
        scratch_shapes=[pltpu.VMEM((seq, 128), BF16), pltpu.VMEM((seq + KPAD, 128), BF16),
                        pltpu.VMEM((seq + KPAD, 128), BF16), pltpu.VMEM((hp, hd, hd), F32)]
        + [pltpu.VMEM((hp, CHUNK, hd), F32)] * 4,
        sem=("parallel", "arbitrary"), args=(proj,) * 3 + (bias, gq, gk) + (proj,) * 4 + (lb, go))


def _hgrn_bwd(proj, lb, go, o_pre, states, scores, dout, nb, seq):
    nc = seq // CHUNK
    hd = HGRN_HEAD_DIM
    rows_blk = min(HGRN_ROWS, seq)
    nblk, nck = seq // rows_blk, rows_blk // CHUNK

    def body(hq_ref, hf_ref, hi_ref, hg_ref, lb_ref, go_ref, o_ref, st_ref, a_ref, dy_ref,
             dhq_ref, dhf_ref, dhi_ref, dhg_ref, dlb_ref, dgo_ref,
             dst_all, q_all, k_all, b_all, da_all, dqi_all, dki_all, dlb_all, dgo_all):
        @pl.when(pl.program_id(1) == 0)
        def _():
            dst_all[...] = jnp.zeros_like(dst_all)
            dlb_all[...] = jnp.zeros_like(dlb_all)
            dgo_all[...] = jnp.zeros_like(dgo_all)

        lower, upper = _tri(True), _tri(False)
        gov = go_ref[...]
        row = lax.broadcasted_iota(jnp.int32, (CHUNK, hd), 0)

        def head_chunk(hh, c, rows):
            ln = slice(hd * hh, hd * (hh + 1))
            dst, q_s, k_s, b_s = dst_all.at[hh], q_all.at[hh], k_all.at[hh], b_all.at[hh]
            da_s, dqi_s, dki_s = da_all.at[hh], dqi_all.at[hh], dki_all.at[hh]
            dlb_acc, dgo_acc = dlb_all.at[hh], dgo_all.at[hh]
            lbv = lb_ref[:, ln]
            hq, hf, v, hg = hq_ref[rows, ln], hf_ref[rows, ln], hi_ref[rows, ln], hg_ref[rows, ln]
            q, sq, sf, f = _hgrn_gates(hq, hf, lbv)
            kk = 1.0 - f
            yield
            b = _dot_exact01(lower, jnp.log(f))
            q_s[...] = q
            k_s[...] = kk
            b_s[...] = b
            yield
            bl = b_s[pl.ds(CHUNK - 1, 1), :]
            ebl = jnp.exp(bl)
            ekd = jnp.exp(bl - b)
            kd = kk * ekd
            eb = jnp.exp(b)
            qb = q * eb
            st0 = st_ref[hh, c]
            dst1 = dst[...]
            yield

            o = o_ref[rows, ln]
            dy = dy_ref[rows, ln]
            sg = _sigmoid(hg)
            rstd = lax.rsqrt(jnp.mean(o * o, axis=-1, keepdims=True) + RMS_EPS)
            ohat = o * rstd
            dn = dy * hg * sg
            dhg_ref[rows, ln] = (dy * ohat * gov * (sg * (1.0 + hg * (1.0 - sg)))).astype(BF16)
            dgo_acc[...] += jnp.sum(dn * ohat, axis=0, keepdims=True)
            gdn = dn * gov
            do = rstd * (gdn - ohat * jnp.mean(gdn * ohat, axis=-1, keepdims=True))
            yield

            qt, kt, eq, ek = _hgrn_offdiag(q_s, k_s, b_s)
            da = _dot(do, v, NT)
            dat = _dot(v, do, NT)
            da_s[...] = da
            yield
            dqo = _dot(da, kt) * eq
            dko = _dot(dat, qt) * ek
            dqi_s[...] = sum(dqo[:, j * hd:(j + 1) * hd] for j in range(N_SUB - 1))
            dki_s[...] = sum(dko[:, j * hd:(j + 1) * hd] for j in range(N_SUB - 1))
            yield
            col = lax.broadcasted_iota(jnp.int32, (SUB, CHUNK), 1)
            for i in range(N_SUB):
                qi = q_s[pl.ds(SUB * i, SUB), :]
                dai = da_s[pl.ds(SUB * i, SUB), :]
                dqd = jnp.zeros((SUB, hd), F32)
                for s in range(SUB):
                    e, _ = _hgrn_diag_e(b_s, i, s)
                    dacol = jnp.sum(jnp.where(col == SUB * i + s, dai, 0.0), axis=-1, keepdims=True)
                    w = dacol * e
                    dqd = dqd + w * k_s[pl.ds(SUB * i + s, 1), :]
                    dki_s[pl.ds(SUB * i + s, 1), :] += jnp.sum(w * qi, axis=0, keepdims=True)
                    if s % DIAG_STAGE == DIAG_STAGE - 1:
                        yield
                dqi_s[pl.ds(SUB * i, SUB), :] += dqd
            dqi, dki = dqi_s[...], dki_s[...]

            dv = _tn(a_ref[hh, c].astype(F32), do)[0:CHUNK, :] + _dot(kd, dst1, NT)
            dqb = _dot(do, st0)
            dkd = _dot(v, dst1)
            yield
            t2 = dkd * kd
            dq = dqb * eb + dqi
            dk = dkd * ekd + dki
            dbl = jnp.sum(t2, axis=0, keepdims=True) + ebl * jnp.sum(st0 * dst1, axis=0, keepdims=True)
            db = dqb * qb - t2 + q * dqi - kk * dki + jnp.where(row == CHUNK - 1, dbl, 0.0)
            yield
            dg = _dot_exact01(upper, db)
            dst[...] = dst1 * ebl + _tn(do, qb)
            yield

            df = dg / f - dk
            dhf_ref[rows, ln] = (df * (1.0 - lbv) * sf * (1.0 - sf)).astype(BF16)
            dlb_acc[...] += jnp.sum(df * (1.0 - sf), axis=0, keepdims=True)
            dhq_ref[rows, ln] = (dq * (sq * (1.0 + hq * (1.0 - sq)))).astype(BF16)
            dhi_ref[rows, ln] = dv.astype(BF16)

        def chunk(it, carry):
            c = nck - 1 - it
            rows = pl.ds(pl.multiple_of(c * CHUNK, CHUNK), CHUNK)
            _lockstep([head_chunk(hh, c, rows) for hh in range(HGRN_HEADS)])
            return carry

        lax.fori_loop(0, nck, chunk, 0, unroll=min(2, nck))

        @pl.when(pl.program_id(1) == nblk - 1)
        def _():
            dlb_ref[...] = dlb_all[...]
            dgo_ref[...] = dgo_all[...]

    hp, wide = HGRN_HEADS, HGRN_HEADS * hd

    def col(off):
        return pl.BlockSpec((rows_blk, wide), lambda b, s: (b * nblk + nblk - 1 - s, off // hp))

    out = pl.BlockSpec((rows_blk, wide), lambda b, s: (b * nblk + nblk - 1 - s, 0))
    part = pl.BlockSpec((hp, 1, hd), lambda b, s: (b, 0, 0))
    t = nb * seq
    return pl.pallas_call(
        body, name="hgrn_bwd", grid=(nb, nblk),
        in_specs=[col(12), col(16), col(20), col(24), pl.BlockSpec((1, wide), lambda b, s: (0, 0)),
                  pl.BlockSpec((1, hd), lambda b, s: (0, 0)), out,
                  pl.BlockSpec((hp, nck, hd, hd), lambda b, s: (b, nblk - 1 - s, 0, 0)),
                  pl.BlockSpec((hp, nck, CHUNK, hd), lambda b, s: (b, nblk - 1 - s, 0, 0)), col(4)],
        out_specs=[out, out, out, out, part, part],
        out_shape=[jax.ShapeDtypeStruct((t, wide), BF16)] * 4 + [jax.ShapeDtypeStruct((nb * hp, 1, hd), F32)] * 2,
        scratch_shapes=[pltpu.VMEM((hp, hd, hd), F32)] + [pltpu.VMEM((hp, CHUNK, hd), F32)] * 3
        + [pltpu.VMEM((hp, CHUNK, CHUNK), F32)] + [pltpu.VMEM((hp, CHUNK, hd), F32)] * 2
        + [pltpu.VMEM((hp, 1, hd), F32)] * 2,
        compiler_params=_params("parallel", "arbitrary"),
    )(proj, proj, proj, proj, lb, go, o_pre, states, scores, dout)


def _lb_fwd(lower_bounds):
    def body(x_ref, o_ref):
        xv = x_ref[...]
        e = jnp.exp(xv - jnp.max(xv, axis=0, keepdims=True))
        o_ref[...] = e[0:1, :] / jnp.sum(e, axis=0, keepdims=True)

    return pl.pallas_call(body, name="lb_fwd",
                          out_shape=jax.ShapeDtypeStruct((1, lower_bounds.shape[1]), F32))(lower_bounds)


def _lb_bwd(lower_bounds, dlb_parts):
    ng = dlb_parts.shape[0]

    def body(x_ref, d_ref, o_ref):
        xv = x_ref[...]
        e = jnp.exp(xv - jnp.max(xv, axis=0, keepdims=True))
        p = e / jnp.sum(e, axis=0, keepdims=True)
        dlb = d_ref[0]
        for gi in range(1, ng):
            dlb = dlb + d_ref[gi]
        first = lax.broadcasted_iota(jnp.int32, xv.shape, 0) == 0
        o_ref[...] = p * (jnp.where(first, dlb, 0.0) - p[0:1, :] * dlb)

    return pl.pallas_call(body, name="lb_bwd",
                          out_shape=jax.ShapeDtypeStruct(lower_bounds.shape, F32))(lower_bounds, dlb_parts)


def _ffn_bwd(x, g, h, gate, up, dy, dy16, w, put, tag):
    wg, wu, wd = w[tag + "_w_gate"], w[tag + "_w_up"], w[tag + "_w_down"]
    dgate, dup, dwd = _ffn_bwd_mid(dy16, wd, gate, up, tag + "_bwd_mid")
    put(tag + "_w_down", dwd)
    put(tag + "_w_gate", _mm(dgate, h, ta=True, tm=1408, tn=512, name=tag + "_dwg"))
    put(tag + "_w_up", _mm(dup, h, ta=True, tm=1408, tn=512, name=tag + "_dwu"))
    dh = _mm(dgate, wg, tm=512, tn=1024, name=tag + "_dh_gate")
    return _mm(dup, wu, tm=512, tn=1024, add=dh, norm_bwd=(x, g, dy), name=tag + "_dh_up")


def _local_step(x, tgt, sp, w, put, nb, seq):
    d = x.shape[1]
    h1 = _rms_fwd(x, sp["ffn1_norm_g"], "ffn1_norm")
    rb_pad = jnp.pad(sp["attn_rel_bias"], ((0, 0), (0, N_REL_PAD - N_REL)))
    bias = jnp.transpose(_bias_expand(rb_pad), (1, 0, 2)).reshape(ATTN_HEADS * CHUNK, BAND)
    gq2 = jnp.concatenate([sp["attn_q_norm_g"]] * 2, axis=1)
    gk2 = jnp.concatenate([sp["attn_k_norm_g"]] * 2, axis=1)
    lb = _lb_fwd(sp["hgrn_lower_bounds"])
    gate1, up1, act1 = _ffn_up(h1, w["ffn1_w_gate"], w["ffn1_w_up"], "ffn1_up")
    x1, h2 = _mm(act1, w["ffn1_w_down"], tm=512, tn=d, add=x, scale=0.5, norm_g=sp["mix_norm_g"],
                 name="ffn1_down")
    proj = _mm(h2, w["w_in"], tb=True, tm=256, tn=w["w_in"].shape[0], name="in_proj")
    attn, hy, ho, hstate, hscore = _mixer_fwd(proj, bias, gq2, gk2, lb, sp["hgrn_out_norm_g"], nb, seq)
    mix = jnp.concatenate([attn, hy], axis=1)
    x2, h3 = _mm(mix, w["w_out"], tm=512, tn=1024, add=x1, norm_g=sp["ffn2_norm_g"], name="out_proj")
    gate2, up2, dx3, dx3_16, sq = _ffn_fwd(h3, x2, w["ffn2_w_gate"], w["ffn2_w_up"], w["ffn2_w_down"], "ffn2_fwd",
                                           tgt=tgt)
    loss = 0.5 * jnp.sum(sq) / d

    dx2, dx2_16, dg3 = _ffn_bwd(x2, sp["ffn2_norm_g"], h3, gate2, up2, dx3, dx3_16, w, put, "ffn2")
    dmix = _mm(dx2_16, w["w_out"], tb=True, tm=512, tn=1024, name="out_proj_dx")
    put("w_out", _mm(mix, dx2_16, ta=True, tm=512, tn=1024, name="out_proj_dw"))
    bias_t = jnp.transpose(bias.reshape(ATTN_HEADS // 2, 2 * CHUNK, BAND), (0, 2, 1)).reshape(-1, 2 * CHUNK)
    dq, dk, dv, dbias, dgq, dgk = _attn_bwd(proj, attn, dmix, bias_t, gq2, gk2, nb, seq)
    dbias = jnp.transpose(dbias.reshape(nb, ATTN_HEADS // 2, BAND, 2, CHUNK), (0, 4, 1, 3, 2))
    dbias = dbias.reshape(nb, CHUNK, ATTN_HEADS, BAND)
    dgq = jnp.sum(dgq, axis=(0, 1)).reshape(2, ATTN_HEAD_DIM).sum(axis=0, keepdims=True)
    dgk = jnp.sum(dgk, axis=(0, 1)).reshape(2, ATTN_HEAD_DIM).sum(axis=0, keepdims=True)
    dhq, dhf, dhi, dhg, dlb, dgo = _hgrn_bwd(proj, lb, sp["hgrn_out_norm_g"], ho, hstate, hscore, dmix, nb, seq)
    dproj = jnp.concatenate([dq, dk, dv, dhq, dhf, dhi, dhg], axis=1)
    put("w_in", _mm(dproj, h2, ta=True, tm=512, tn=1024, name="in_proj_dw"))
    dx1, dx1_16, dgm = _mm(dproj, w["w_in"], tm=512, tn=1024, norm_bwd=(x1, sp["mix_norm_g"], dx2),
                           name="in_proj_dx")
    dx0, _, dg1 = _ffn_bwd(x, sp["ffn1_norm_g"], h1, gate1, up1, dx1, dx1_16, w, put, "ffn1")

    small = {
        "ffn1_norm_g": dg1, "mix_norm_g": dgm, "ffn2_norm_g": dg3,
        "attn_q_norm_g": dgq, "attn_k_norm_g": dgk,
        "attn_rel_bias": _bias_fold(dbias)[:, :N_REL],
        "hgrn_lower_bounds": _lb_bwd(sp["hgrn_lower_bounds"], dlb.reshape(nb, 1, HGRN_HEADS * HGRN_HEAD_DIM)),
        "hgrn_out_norm_g": jnp.sum(dgo, axis=(0, 1))[None, :],
    }
    return loss, dx0, small


MESH = pl.DeviceIdType.MESH
ANY = pl.BlockSpec(memory_space=pl.ANY)


def _coords():
    return lax.axis_index("x"), lax.axis_index("y"), lax.axis_index("c")


def _other_chips(x, y):
    return [(1 - x, y), (x, 1 - y), (1 - x, 1 - y)]


def _gather_side(shards):
    n = len(shards)

    def copies(ins, outs, sems):
        send_sems, recv_sems, local_sems = sems
        x, y, c = _coords()
        xn, yn, dg = (1 - x, y), (x, 1 - y), (1 - x, 1 - y)

        def copy(i, k, block, to, half=None, src=None):
            bx, by, bc = block
            dst = outs[i].at[4 * bx + 2 * by + bc]
            if half is not None:
                rows = shards[i].shape[0] // 2
                dst = dst.at[pl.ds(half * rows, rows)]
            return pltpu.make_async_remote_copy(
                src_ref=dst if src is None else src, dst_ref=dst, send_sem=send_sems.at[i, k],
                recv_sem=recv_sems.at[i, k], device_id=to, device_id_type=MESH)

        mine = [pltpu.make_async_copy(ins[i], outs[i].at[4 * x + 2 * y + c], local_sems.at[i]) for i in range(n)]
        return copy, mine, (x, y, c), (x, y, 1 - c), xn, yn, dg, c

    def own(copy, i, ins, me, sibling, xn, yn, c):
        return [copy(i, 0, me, sibling, src=ins[i]), copy(i, 1, me, (*xn, c), src=ins[i]),
                copy(i, 2, me, (*yn, c), src=ins[i])]

    def passed_on(copy, i, sibling, xn, yn, c):
        return [copy(i, 3, (*xn, c), sibling), copy(i, 5, (*xn, c), (*yn, c), half=0),
                copy(i, 4, (*yn, c), sibling), copy(i, 6, (*yn, c), (*xn, c), half=1)]

    def diagonal(copy, i, sibling, dg, c):
        return [copy(i, 7, (*dg, c), sibling, half=0), copy(i, 8, (*dg, c), sibling, half=1)]

    def start(ins, outs, sems):
        copy, mine, me, sibling, xn, yn, dg, c = copies(ins, outs, sems)
        for cp in mine + [cp for i in range(n) for cp in own(copy, i, ins, me, sibling, xn, yn, c)]:
            cp.start()

    def middle(ins, outs, sems):
        copy, mine, me, sibling, xn, yn, dg, c = copies(ins, outs, sems)
        for i in range(n):
            fwd_x, relay_x, fwd_y, relay_y = passed_on(copy, i, sibling, xn, yn, c)
            copy(i, 1, (*xn, c), me).wait_recv()
            fwd_x.start()
            relay_x.start()
            copy(i, 2, (*yn, c), me).wait_recv()
            fwd_y.start()
            relay_y.start()

    def finish(ins, outs, sems):
        copy, mine, me, sibling, xn, yn, dg, c = copies(ins, outs, sems)
        for i in range(n):
            top, bottom = diagonal(copy, i, sibling, dg, c)
            copy(i, 5, (*dg, c), me, half=0).wait_recv()
            top.start()
            copy(i, 6, (*dg, c), me, half=1).wait_recv()
            bottom.start()
        for i in range(n):
            copy(i, 0, sibling, me).wait_recv()
            copy(i, 3, (*xn, 1 - c), me).wait_recv()
            copy(i, 4, (*yn, 1 - c), me).wait_recv()
            copy(i, 7, (*dg, 1 - c), me, half=0).wait_recv()
            copy(i, 8, (*dg, 1 - c), me, half=1).wait_recv()
        for i in range(n):
            for cp in (own(copy, i, ins, me, sibling, xn, yn, c) + passed_on(copy, i, sibling, xn, yn, c)
                       + diagonal(copy, i, sibling, dg, c)):
                cp.wait_send()
        for cp in mine:
            cp.wait()

    return _Side(list(shards), [jax.ShapeDtypeStruct((N_DEV,) + s.shape, s.dtype) for s in shards],
                 [pltpu.SemaphoreType.DMA((n, 9)), pltpu.SemaphoreType.DMA((n, 9)), pltpu.SemaphoreType.DMA((n,))],
                 start, finish, middle)


def _pair_side(grads):
    n = len(grads)

    def copies(ins, outs, sems):
        send_sems, recv_sems = sems
        x, y, c = _coords()
        return [pltpu.make_async_remote_copy(
            src_ref=ins[i].at[2 * k + 1 - c], dst_ref=outs[i].at[k], send_sem=send_sems.at[i, k],
            recv_sem=recv_sems.at[i, k], device_id=(x, y, 1 - c), device_id_type=MESH)
            for i in range(n) for k in range(4)]

    def start(ins, outs, sems):
        for cp in copies(ins, outs, sems):
            cp.start()

    def finish(ins, outs, sems):
        for cp in copies(ins, outs, sems):
            cp.wait()

    return _Side(list(grads), [jax.ShapeDtypeStruct((4,) + g.shape[1:], g.dtype) for g in grads],
                 [pltpu.SemaphoreType.DMA((n, 4)), pltpu.SemaphoreType.DMA((n, 4))], start, finish)


def _pair_add(grads, recvs, core, name):
    count = len(grads)

    def body(c_ref, *refs):
        for n in range(count):
            refs[2 * count + n][...] = (refs[2 * n][...] + refs[2 * n + 1][...]).astype(BF16)

    in_specs, out_specs = [], []
    for g in grads:
        blk = (1,) + g.shape[1:]
        in_specs += [pl.BlockSpec(blk, lambda k, c_ref: (2 * k + c_ref[0], 0, 0)),
                     pl.BlockSpec(blk, lambda k, c_ref: (k, 0, 0))]
        out_specs.append(pl.BlockSpec(blk, lambda k, c_ref: (k, 0, 0)))
    out = pl.pallas_call(
        body, name=name,
        grid_spec=pltpu.PrefetchScalarGridSpec(num_scalar_prefetch=1, grid=(4,), in_specs=in_specs,
                                               out_specs=out_specs),
        out_shape=[jax.ShapeDtypeStruct((4,) + g.shape[1:], BF16) for g in grads],
        compiler_params=_params("arbitrary"),
    )(core, *[a for pair in zip(grads, recvs) for a in pair])
    return list(out)


def _chip_side(parts):
    n = len(parts)

    def copies(ins, outs, sems):
        send_sems, recv_sems, local_sems = sems
        x, y, c = _coords()
        chips = _other_chips(x, y)
        mine = [pltpu.make_async_copy(ins[i].at[2 * x + y], outs[i].at[2 * x + y], local_sems.at[i])
                for i in range(n)]
        sent = [pltpu.make_async_remote_copy(
            src_ref=ins[i].at[2 * px + py], dst_ref=outs[i].at[2 * x + y], send_sem=send_sems.at[i, j],
            recv_sem=recv_sems.at[i, j], device_id=(px, py, c), device_id_type=MESH)
            for i in range(n) for j, (px, py) in enumerate(chips)]
        return mine, sent, chips, c

    def start(ins, outs, sems):
        mine, sent, _, _ = copies(ins, outs, sems)
        for cp in mine + sent:
            cp.start()

    def finish(ins, outs, sems):
        mine, sent, chips, c = copies(ins, outs, sems)
        send_sems, recv_sems, _ = sems
        for i in range(n):
            for j, (px, py) in enumerate(chips):
                landed = outs[i].at[2 * px + py]
                pltpu.make_async_remote_copy(
                    src_ref=landed, dst_ref=landed, send_sem=send_sems.at[i, j], recv_sem=recv_sems.at[i, j],
                    device_id=(px, py, c), device_id_type=MESH).wait_recv()
        for cp in sent:
            cp.wait_send()
        for cp in mine:
            cp.wait()

    return _Side(list(parts), [jax.ShapeDtypeStruct(p.shape, p.dtype) for p in parts],
                 [pltpu.SemaphoreType.DMA((n, 3)), pltpu.SemaphoreType.DMA((n, 3)), pltpu.SemaphoreType.DMA((n,))],
                 start, finish)


def _all_reduce_small(vals):
    n = len(vals)

    def body(*refs):
        ins, outs, bufs = refs[:n], refs[n:2 * n], refs[2 * n:3 * n]
        send_sems, recv_sems = refs[3 * n:]
        x, y, c = _coords()
        me = 4 * x + 2 * y + c
        for i in range(n):
            bufs[i][me] = ins[i][...]
        sent, landed = [], []
        for k in range(1, N_DEV):
            px = 1 - x if k & 4 else x
            py = 1 - y if k & 2 else y
            pc = 1 - c if k & 1 else c
            for i in range(n):
                sent.append(pltpu.make_async_remote_copy(
                    src_ref=ins[i], dst_ref=bufs[i].at[me], send_sem=send_sems.at[i, k - 1],
                    recv_sem=recv_sems.at[i, k - 1], device_id=(px, py, pc), device_id_type=MESH))
                landed.append(pltpu.make_async_remote_copy(
                    src_ref=ins[i], dst_ref=bufs[i].at[4 * px + 2 * py + pc], send_sem=send_sems.at[i, k - 1],
                    recv_sem=recv_sems.at[i, k - 1], device_id=(x, y, c), device_id_type=MESH))
        for cp in sent:
            cp.start()
        for cp in landed:
            cp.wait_recv()
        for cp in sent:
            cp.wait_send()
        for i in range(n):
            acc = bufs[i][0]
            for j in range(1, N_DEV):
                acc = acc + bufs[i][j]
            outs[i][...] = acc

    vmem = pl.BlockSpec(memory_space=pltpu.VMEM)
    return pl.pallas_call(
        body, name="small_all_reduce", out_shape=[jax.ShapeDtypeStruct(v.shape, F32) for v in vals],
        in_specs=[vmem] * n, out_specs=[vmem] * n,
        scratch_shapes=[pltpu.VMEM((N_DEV,) + v.shape, F32) for v in vals]
        + [pltpu.SemaphoreType.DMA((n, N_DEV - 1)), pltpu.SemaphoreType.DMA((n, N_DEV - 1))],
    )(*vals)


def _adamw(ws, ms, vs, gs, name):
    count = len(ws)
    parts = ws[0].ndim == 3
    steps = 4 if all(w.shape[-2] % 32 == 0 for w in ws) else 1

    def body(*refs):
        for n in range(count):
            w_ref, m_ref, v_ref, g_ref = refs[4 * n:4 * n + 4]
            go_ref, d_ref, mo_ref, vo_ref = refs[4 * count + 4 * n:4 * count + 4 * n + 4]
            if parts:
                gv = g_ref[0].astype(F32)
                for k in range(1, 4):
                    gv = gv + g_ref[k].astype(F32)
                gv = gv[None]
            else:
                gv = g_ref[...]
            m2 = ADAM_B1 * m_ref[...] + (1.0 - ADAM_B1) * gv
            v2 = ADAM_B2 * v_ref[...] + (1.0 - ADAM_B2) * (gv * gv)
            m_hat = m2 / (1.0 - ADAM_B1 ** ADAM_STEP)
            v_hat = v2 / (1.0 - ADAM_B2 ** ADAM_STEP)
            go_ref[...] = gv
            d_ref[...] = -ADAM_LR * (m_hat / (jnp.sqrt(v_hat) + ADAM_EPS) + ADAM_WD * w_ref[...])
            mo_ref[...] = m2
            vo_ref[...] = v2

    in_specs, out_specs, out_shape = [], [], []
    for w in ws:
        r, cdim = w.shape[-2:]
        if parts:
            row = pl.BlockSpec((1, r // steps, cdim), lambda i: (0, i, 0))
            g_spec = pl.BlockSpec((4, r // steps, cdim), lambda i: (0, i, 0))
        else:
            row = g_spec = pl.BlockSpec((r // steps, cdim), lambda i: (i, 0))
        in_specs += [row, row, row, g_spec]
        out_specs += [row] * 4
        out_shape += [jax.ShapeDtypeStruct(w.shape, F32)] * 4
    args = [a for group in zip(ws, ms, vs, gs) for a in group]
    out = pl.pallas_call(
        body, name=name, grid=(steps,), in_specs=in_specs, out_specs=out_specs, out_shape=out_shape,
        compiler_params=_params("parallel"),
    )(*args)
    return [out[4 * n:4 * n + 4] for n in range(count)]


def _adamw_small(ws, ms, vs, gs):
    count = len(ws)

    def body(*refs):
        for n in range(count):
            w_ref, m_ref, v_ref, g_ref = refs[4 * n:4 * n + 4]
            d_ref, mo_ref, vo_ref = refs[4 * count + 3 * n:4 * count + 3 * n + 3]
            gv = g_ref[...]
            m2 = ADAM_B1 * m_ref[...] + (1.0 - ADAM_B1) * gv
            v2 = ADAM_B2 * v_ref[...] + (1.0 - ADAM_B2) * (gv * gv)
            m_hat = m2 / (1.0 - ADAM_B1 ** ADAM_STEP)
            v_hat = v2 / (1.0 - ADAM_B2 ** ADAM_STEP)
            d_ref[...] = -ADAM_LR * (m_hat / (jnp.sqrt(v_hat) + ADAM_EPS) + ADAM_WD * w_ref[...])
            mo_ref[...] = m2
            vo_ref[...] = v2

    out = pl.pallas_call(
        body, name="small_adamw",
        out_shape=[jax.ShapeDtypeStruct(w.shape, F32) for w in ws for _ in range(3)],
    )(*[a for group in zip(ws, ms, vs, gs) for a in group])
    return [out[3 * n:3 * n + 3] for n in range(count)]


WEIGHTS = ["ffn1_norm_g", "ffn1_w_gate", "ffn1_w_up", "ffn1_w_down", "mix_norm_g", "w_in", "attn_q_norm_g",
           "attn_k_norm_g", "attn_rel_bias", "hgrn_lower_bounds", "hgrn_out_norm_g", "w_out", "ffn2_norm_g",
           "ffn2_w_gate", "ffn2_w_up", "ffn2_w_down"]
COL_SHARDED = ("ffn1_w_gate", "ffn1_w_up", "w_in", "ffn2_w_gate", "ffn2_w_up")
ROW_SHARDED = ("ffn1_w_down", "w_out", "ffn2_w_down")
BIG = [n for n in WEIGHTS if n in COL_SHARDED or n in ROW_SHARDED]
SMALL = [n for n in WEIGHTS if n not in BIG]
FFN2 = ["ffn2_w_down", "ffn2_w_gate", "ffn2_w_up"]
MIXER = ["w_out", "w_in"]

PLAN = {
    "ffn1_norm": [("gather", ["ffn1_w_gate"])],
    "bias_expand": [("gather", ["ffn1_w_up"])],
    "ffn1_up": [("gather", ["ffn1_w_down", "w_out"])],
    "ffn1_down": [("gather", ["w_in"])],
    "mixer_fwd": [("gather", FFN2)],
    "ffn2_dh_gate": [("pair", FFN2)],
    "attn_bwd": [("chip", FFN2)],
    "in_proj_dx": [("pair", MIXER)],
    "ffn1_bwd_mid": [("chip", MIXER)],
    "ffn1_dwg": [("pair", ["ffn1_w_down"])],
    "ffn1_dwu": [("chip", ["ffn1_w_down"]), ("pair", ["ffn1_w_gate"])],
    "ffn1_dh_gate": [("chip", ["ffn1_w_gate"]), ("pair", ["ffn1_w_up"])],
    "bias_fold": [("chip", ["ffn1_w_up"])],
}


def _join_sides(sides):
    def split(refs, counts):
        out, at = [], 0
        for n in counts:
            out.append(refs[at:at + n])
            at += n
        return out

    n_in, n_out, n_sem = ([len(getattr(s, f)) for s in sides] for f in ("ins", "out_shape", "sems"))

    def run(which):
        def go(ins, outs, sems):
            for s, i, o, m in zip(sides, split(ins, n_in), split(outs, n_out), split(sems, n_sem)):
                if getattr(s, which) is not None:
                    getattr(s, which)(i, o, m)
        return go

    return _Side([a for s in sides for a in s.ins], [a for s in sides for a in s.out_shape],
                 [a for s in sides for a in s.sems], run("start"), run("finish"),
                 run("middle") if any(s.middle is not None for s in sides) else None)


class _Schedule:
    def __init__(self, shards):
        self.shards = shards
        self.weights = {}
        self.sliced = {}
        self.partials = {}
        self.reduced = {}

    def put(self, name, grad):
        self.sliced[name] = grad.reshape((N_DEV,) + self.shards[name].shape)

    def side_for(self, call):
        if call not in PLAN:
            return None
        sides = []
        for kind, names in PLAN[call]:
            if kind == "gather":
                sides.append(_gather_side([self.shards[n] for n in names]))
            elif kind == "pair":
                sides.append(_pair_side([self.sliced[n] for n in names]))
            else:
                sides.append(_chip_side([self.partials[n] for n in names]))
        return _join_sides(sides)

    def done(self, call, outs):
        at = 0
        for kind, names in PLAN[call]:
            self.file(kind, names, outs[at:at + len(names)])
            at += len(names)

    def file(self, kind, names, outs):
        if kind == "pair":
            core = lax.axis_index("c").astype(jnp.int32).reshape(1)
            sums = _pair_add([self.sliced[n] for n in names], list(outs), core, names[0] + "_pair_add")
            self.partials.update(dict(zip(names, sums)))
            return
        for n, o in zip(names, outs):
            if kind == "gather":
                self.weights[n] = o.reshape(N_DEV * o.shape[1], o.shape[2])
            else:
                self.reduced[n] = o


def kernel(x, ffn1_norm_g, ffn1_w_gate, ffn1_w_up, ffn1_w_down, mix_norm_g, w_in, attn_q_norm_g, attn_k_norm_g, attn_rel_bias, hgrn_lower_bounds, hgrn_out_norm_g, w_out, ffn2_norm_g, ffn2_w_gate, ffn2_w_up, ffn2_w_down, loss_target, m_ffn1_norm_g, m_ffn1_w_gate, m_ffn1_w_up, m_ffn1_w_down, m_mix_norm_g, m_w_in, m_attn_q_norm_g, m_attn_k_norm_g, m_attn_rel_bias, m_hgrn_lower_bounds, m_hgrn_out_norm_g, m_w_out, m_ffn2_norm_g, m_ffn2_w_gate, m_ffn2_w_up, m_ffn2_w_down, v_ffn1_norm_g, v_ffn1_w_gate, v_ffn1_w_up, v_ffn1_w_down, v_mix_norm_g, v_w_in, v_attn_q_norm_g, v_attn_k_norm_g, v_attn_rel_bias, v_hgrn_lower_bounds, v_hgrn_out_norm_g, v_w_out, v_ffn2_norm_g, v_ffn2_w_gate, v_ffn2_w_up, v_ffn2_w_down):
    wts = dict(zip(WEIGHTS, (ffn1_norm_g, ffn1_w_gate, ffn1_w_up, ffn1_w_down, mix_norm_g, w_in, attn_q_norm_g,
                             attn_k_norm_g, attn_rel_bias, hgrn_lower_bounds, hgrn_out_norm_g, w_out, ffn2_norm_g,
                             ffn2_w_gate, ffn2_w_up, ffn2_w_down)))
    mom = dict(zip(WEIGHTS, (m_ffn1_norm_g, m_ffn1_w_gate, m_ffn1_w_up, m_ffn1_w_down, m_mix_norm_g, m_w_in,
                             m_attn_q_norm_g, m_attn_k_norm_g, m_attn_rel_bias, m_hgrn_lower_bounds,
                             m_hgrn_out_norm_g, m_w_out, m_ffn2_norm_g, m_ffn2_w_gate, m_ffn2_w_up, m_ffn2_w_down)))
    var = dict(zip(WEIGHTS, (v_ffn1_norm_g, v_ffn1_w_gate, v_ffn1_w_up, v_ffn1_w_down, v_mix_norm_g, v_w_in,
                             v_attn_q_norm_g, v_attn_k_norm_g, v_attn_rel_bias, v_hgrn_lower_bounds,
                             v_hgrn_out_norm_g, v_w_out, v_ffn2_norm_g, v_ffn2_w_gate, v_ffn2_w_up, v_ffn2_w_down)))
    nb, seq, d = x.shape
    shapes = {n: wts[n].shape for n in WEIGHTS}

    def rows_first(a, n):
        return jnp.swapaxes(a, 1, 2) if n in COL_SHARDED else a

    sched = _Schedule({n: rows_first(wts[n], n)[0].astype(BF16) for n in BIG})
    sp = {n: wts[n] for n in SMALL}
    sp["attn_rel_bias"] = wts["attn_rel_bias"][0]
    _ACTIVE[0] = sched
    try:
        loss, dx, dsmall = _local_step(x.reshape(nb * seq, d), loss_target.reshape(nb * seq, d), sp,
                                       sched.weights, sched.put, nb, seq)
    finally:
        _ACTIVE[0] = None
    reduced = sched.reduced

    sums = _all_reduce_small([dsmall[n] for n in SMALL] + [jnp.full((1, 128), loss, F32)])
    gsmall = {n: s.reshape(shapes[n]) for n, s in zip(SMALL, sums)}
    loss_total = sums[-1][0, 0]

    grads, deltas, new_m, new_v = {}, {}, {}, {}
    for group, tag in (([n for n in BIG if n not in MIXER], "ffn_adamw"), (MIXER, "mixer_adamw")):
        outs = _adamw([rows_first(wts[n], n) for n in group], [rows_first(mom[n], n) for n in group],
                      [rows_first(var[n], n) for n in group], [reduced[n] for n in group], tag)
        for n, out in zip(group, outs):
            grads[n], deltas[n], new_m[n], new_v[n] = (rows_first(o, n) for o in out)
    outs = _adamw_small([wts[n] for n in SMALL], [mom[n] for n in SMALL], [var[n] for n in SMALL],
                        [gsmall[n] for n in SMALL])
    for n, (delta, m2, v2) in zip(SMALL, outs):
        deltas[n], new_m[n], new_v[n] = delta, m2, v2
    grads.update(gsmall)

    return (loss_total, dx.reshape(nb, seq, d), *[grads[n] for n in WEIGHTS], *[deltas[n] for n in WEIGHTS],
            *[new_m[n] for n in WEIGHTS], *[new_v[n] for n in WEIGHTS])
```

```python
import functools

import jax
import jax.numpy as jnp
from jax import lax
from jax.experimental import pallas as pl
from jax.experimental.pallas import tpu as pltpu

F32 = jnp.float32
BF16 = jnp.bfloat16

RMS_EPS = 1e-6
CHUNK = 64
LEFT_CHUNKS = 8
BAND = (LEFT_CHUNKS + 2) * CHUNK
KPAD = BAND - CHUNK
REL_CLIP = 128
N_REL = 2 * REL_CLIP + 1
N_REL_PAD = 384
ATTN_HEADS = 8
ATTN_HEAD_DIM = 64
ATTN_WIDTH = ATTN_HEADS * ATTN_HEAD_DIM
ATTN_LOCKSTEP = 4
ATTN_PREP_ROWS = 256
ATTN_UNROLL = 32
HGRN_HEADS = 4
HGRN_HEAD_DIM = 128
HGRN_ROWS = 512
SUB = 16
N_SUB = CHUNK // SUB
DIAG_STAGE = 4
N_DEV = 8

ADAM_LR = 0.001
ADAM_B1 = 0.9
ADAM_B2 = 0.999
ADAM_EPS = 1e-08
ADAM_WD = 0.01
ADAM_STEP = 10

VMEM_LIMIT = 56 * 1024 * 1024

NT = (((1,), (1,)), ((), ()))
NN = (((1,), (0,)), ((), ()))


def _params(*sem):
    return pltpu.CompilerParams(dimension_semantics=sem, vmem_limit_bytes=VMEM_LIMIT)


def _sigmoid(v):
    return 0.5 * jnp.tanh(0.5 * v) + 0.5


def _dot(a, b, dims=NN):
    return lax.dot_general(a.astype(BF16), b.astype(BF16), dims, preferred_element_type=F32)


def _dot_exact01(m01, v):
    m = m01.astype(BF16)
    hi = v.astype(BF16)
    r1 = v - hi.astype(F32)
    mid = r1.astype(BF16)
    lo = (r1 - mid.astype(F32)).astype(BF16)
    out = lax.dot_general(m, hi, NN, preferred_element_type=F32)
    out = out + lax.dot_general(m, mid, NN, preferred_element_type=F32)
    return out + lax.dot_general(m, lo, NN, preferred_element_type=F32)


def _dot_exact01_r(v, m01):
    m = m01.astype(BF16)
    hi = v.astype(BF16)
    r1 = v - hi.astype(F32)
    mid = r1.astype(BF16)
    lo = (r1 - mid.astype(F32)).astype(BF16)
    out = lax.dot_general(hi, m, NN, preferred_element_type=F32)
    out = out + lax.dot_general(mid, m, NN, preferred_element_type=F32)
    return out + lax.dot_general(lo, m, NN, preferred_element_type=F32)


def _lockstep(stages):
    live = list(stages)
    while live:
        still = []
        for g in live:
            try:
                next(g)
                still.append(g)
            except StopIteration:
                pass
        live = still


def _row_sums_on_lanes(v):
    ones = jnp.ones((8, v.shape[1]), BF16)
    hi = v.astype(BF16)
    r1 = v - hi.astype(F32)
    mid = r1.astype(BF16)
    lo = (r1 - mid.astype(F32)).astype(BF16)
    out = lax.dot_general(ones, hi, NT, preferred_element_type=F32)
    out = out + lax.dot_general(ones, mid, NT, preferred_element_type=F32)
    return (out + lax.dot_general(ones, lo, NT, preferred_element_type=F32))[0:1, :]


def _tn(a, b):
    ap = jnp.concatenate([a, jnp.zeros_like(a)], axis=0)
    bp = jnp.concatenate([b, jnp.zeros_like(b)], axis=0)
    return _dot(ap.T, bp)


def _row_tile(t):
    for tm in (512, 256, 128, 64, 32, 16, 8):
        if t % tm == 0:
            return tm
    raise ValueError(t)


class _Side:
    def __init__(self, ins, out_shape, sems, start, finish, middle=None):
        self.ins, self.out_shape, self.sems = ins, out_shape, sems
        self.start, self.middle, self.finish = start, middle, finish


_ACTIVE = [None]


def _pallas(body, *, name, grid, in_specs, out_specs, out_shape, scratch_shapes=(), sem, args):
    sched = _ACTIVE[0]
    side = sched.side_for(name) if sched is not None else None
    if side is None:
        return pl.pallas_call(
            body, name=name, grid=grid, in_specs=list(in_specs), out_specs=list(out_specs),
            out_shape=list(out_shape), scratch_shapes=list(scratch_shapes), compiler_params=_params(*sem))(*args)
    cuts = [len(in_specs), len(side.ins), len(out_shape), len(side.out_shape), len(scratch_shapes)]

    def with_side(*refs):
        groups, at = [], 0
        for n in cuts:
            groups.append(refs[at:at + n])
            at += n
        ins, side_ins, outs, side_outs, scratch = groups
        side_sems = refs[at:]
        step, total = pl.program_id(0), grid[0]
        for a in range(1, len(grid)):
            step, total = step * grid[a] + pl.program_id(a), total * grid[a]
        has_middle = side.middle is not None and total >= 3

        @pl.when(step == 0)
        def _():
            side.start(side_ins, side_outs, side_sems)

        if has_middle:
            @pl.when(step == total // 2)
            def _():
                side.middle(side_ins, side_outs, side_sems)

        body(*ins, *outs, *scratch)

        @pl.when(step == total - 1)
        def _():
            if side.middle is not None and not has_middle:
                side.middle(side_ins, side_outs, side_sems)
            side.finish(side_ins, side_outs, side_sems)

    hbm = pl.BlockSpec(memory_space=pl.ANY)
    res = pl.pallas_call(
        with_side, name=name, grid=grid, in_specs=list(in_specs) + [hbm] * len(side.ins),
        out_specs=list(out_specs) + [hbm] * len(side.out_shape), out_shape=list(out_shape) + list(side.out_shape),
        scratch_shapes=list(scratch_shapes) + list(side.sems),
        compiler_params=_params(*(["arbitrary"] * len(grid))))(*args, *side.ins)
    sched.done(name, res[len(out_shape):])
    return res[:len(out_shape)]


def _rms_fwd(x, g, name):
    t, d = x.shape
    tm = _row_tile(t)

    def body(x_ref, g_ref, h_ref):
        xv = x_ref[...]
        r = lax.rsqrt(jnp.mean(xv * xv, axis=-1, keepdims=True) + RMS_EPS)
        h_ref[...] = (xv * r * g_ref[...]).astype(BF16)

    return _pallas(
        body, name=name, grid=(t // tm,),
        in_specs=[pl.BlockSpec((tm, d), lambda i: (i, 0)), pl.BlockSpec((1, d), lambda i: (0, 0))],
        out_specs=[pl.BlockSpec((tm, d), lambda i: (i, 0))], out_shape=[jax.ShapeDtypeStruct((t, d), BF16)],
        sem=("parallel",), args=(x, g))[0]


def _accumulate(ref, part, step):
    @pl.when(step == 0)
    def _():
        ref[...] = part

    @pl.when(step > 0)
    def _():
        ref[...] += part


def _mm(a, b, *, ta=False, tb=False, tm, tn, out_dtype=F32, add=None, scale=1.0, norm_g=None, norm_bwd=None, name):
    m, k = (a.shape[1], a.shape[0]) if ta else a.shape
    n = b.shape[0] if tb else b.shape[1]
    tm, tn = min(tm, m), min(tn, n)
    assert m % tm == 0 and n % tn == 0, (m, n, tm, tn)
    assert (norm_g is None and norm_bwd is None) or tn == n
    dims = (((0 if ta else 1,), (1 if tb else 0,)), ((), ()))
    n_in = 2 + (add is not None) + (norm_g is not None) + (3 if norm_bwd is not None else 0)

    def body(*refs):
        ins, outs = list(refs[2:n_in]), refs[n_in:]
        r = lax.dot_general(refs[0][...].astype(BF16), refs[1][...].astype(BF16), dims, preferred_element_type=F32)
        if scale != 1.0:
            r = r * scale
        if add is not None:
            r = r + ins.pop(0)[...]
        if norm_bwd is not None:
            xv, gv, dres = (ref[...] for ref in ins)
            rs = lax.rsqrt(jnp.mean(xv * xv, axis=-1, keepdims=True) + RMS_EPS)
            xhat = xv * rs
            gd = r * gv
            dx = dres + rs * (gd - xhat * jnp.mean(gd * xhat, axis=-1, keepdims=True))
            outs[0][...] = dx
            outs[1][...] = dx.astype(BF16)
            _accumulate(outs[2], jnp.sum(r * xhat, axis=0, keepdims=True), pl.program_id(0))
            return
        outs[0][...] = r.astype(out_dtype)
        if norm_g is not None:
            rs = lax.rsqrt(jnp.mean(r * r, axis=-1, keepdims=True) + RMS_EPS)
            outs[1][...] = (r * rs * ins.pop(0)[...]).astype(BF16)

    a_spec = pl.BlockSpec((k, tm), lambda i, j: (0, i)) if ta else pl.BlockSpec((tm, k), lambda i, j: (i, 0))
    b_spec = pl.BlockSpec((tn, k), lambda i, j: (j, 0)) if tb else pl.BlockSpec((k, tn), lambda i, j: (0, j))
    o_spec = pl.BlockSpec((tm, tn), lambda i, j: (i, j))
    vec = pl.BlockSpec((1, tn), lambda i, j: (0, j))
    args, specs = [a, b], [a_spec, b_spec]
    if add is not None:
        args.append(add)
        specs.append(o_spec)
    out_specs, out_shape = [o_spec], [jax.ShapeDtypeStruct((m, n), out_dtype)]
    if norm_g is not None:
        args.append(norm_g)
        specs.append(vec)
        out_specs.append(o_spec)
        out_shape.append(jax.ShapeDtypeStruct((m, n), BF16))
    if norm_bwd is not None:
        args += list(norm_bwd)
        specs += [o_spec, vec, o_spec]
        out_specs = [o_spec, o_spec, vec]
        out_shape = [jax.ShapeDtypeStruct((m, n), F32), jax.ShapeDtypeStruct((m, n), BF16),
                     jax.ShapeDtypeStruct((1, n), F32)]
    res = _pallas(body, name=name, grid=(m // tm, n // tn), in_specs=specs, out_specs=out_specs, out_shape=out_shape,
                  sem=("arbitrary", "arbitrary") if norm_bwd is not None else ("parallel", "parallel"), args=args)
    return res[0] if len(res) == 1 else res


def _ffn_tile(f):
    for tf in (1408, 512, 256, 128):
        if f % tf == 0:
            return tf
    raise ValueError(f)


def _ffn_fwd(h, x, wg, wu, wd, name, next_g=None, tgt=None):
    t, d = x.shape
    f = wg.shape[0]
    tm, tf = _row_tile(t), _ffn_tile(f)
    nf = f // tf
    assert (next_g is None) != (tgt is None)

    def body(h_ref, x_ref, wg_ref, wu_ref, wd_ref, tail_ref, g_ref, u_ref, o0_ref, o1_ref, *rest):
        acc_ref = rest[-1]
        j = pl.program_id(1)
        hv = h_ref[...]
        gv = lax.dot_general(hv, wg_ref[...], NT, preferred_element_type=F32)
        uv = lax.dot_general(hv, wu_ref[...], NT, preferred_element_type=F32)
        av = gv * _sigmoid(gv) * uv
        g_ref[...] = gv.astype(BF16)
        u_ref[...] = uv.astype(BF16)
        _accumulate(acc_ref, lax.dot_general(av.astype(BF16), wd_ref[...], NN, preferred_element_type=F32), j)

        @pl.when(j == nf - 1)
        def _():
            y = x_ref[...] + 0.5 * acc_ref[...]
            if tgt is None:
                o0_ref[...] = y
                rs = lax.rsqrt(jnp.mean(y * y, axis=-1, keepdims=True) + RMS_EPS)
                o1_ref[...] = (y * rs * tail_ref[...]).astype(BF16)
            else:
                e = y - tail_ref[...]
                dy = e * (1.0 / d)
                o0_ref[...] = dy
                o1_ref[...] = dy.astype(BF16)
                _accumulate(rest[0], jnp.sum(e * e, axis=0, keepdims=True), pl.program_id(0))

    row = pl.BlockSpec((tm, d), lambda i, j: (i, 0))
    hid = pl.BlockSpec((tm, tf), lambda i, j: (i, j))
    vec = pl.BlockSpec((1, d), lambda i, j: (0, 0))
    out_specs = [hid, hid, row, row] + ([vec] if tgt is not None else [])
    out_shape = [jax.ShapeDtypeStruct((t, f), BF16)] * 2 + [jax.ShapeDtypeStruct((t, d), F32),
                                                            jax.ShapeDtypeStruct((t, d), BF16)]
    if tgt is not None:
        out_shape.append(jax.ShapeDtypeStruct((1, d), F32))
    return _pallas(
        body, name=name, grid=(t // tm, nf),
        in_specs=[row, row] + [pl.BlockSpec((tf, d), lambda i, j: (j, 0))] * 3 + [vec if tgt is None else row],
        out_specs=out_specs, out_shape=out_shape, scratch_shapes=[pltpu.VMEM((tm, d), F32)],
        sem=("parallel" if tgt is None else "arbitrary", "arbitrary"),
        args=(h, x, wg, wu, wd, next_g if tgt is None else tgt))


def _ffn_up(h, wg, wu, name):
    t, d = h.shape
    f = wg.shape[0]
    tm, tf = _row_tile(t), _ffn_tile(f)

    def body(h_ref, wg_ref, wu_ref, g_ref, u_ref, a_ref):
        hv = h_ref[...]
        gv = lax.dot_general(hv, wg_ref[...], NT, preferred_element_type=F32)
        uv = lax.dot_general(hv, wu_ref[...], NT, preferred_element_type=F32)
        g_ref[...] = gv.astype(BF16)
        u_ref[...] = uv.astype(BF16)
        a_ref[...] = (gv * _sigmoid(gv) * uv).astype(BF16)

    hid = pl.BlockSpec((tm, tf), lambda i, j: (i, j))
    wrow = pl.BlockSpec((tf, d), lambda i, j: (j, 0))
    return _pallas(
        body, name=name, grid=(t // tm, f // tf), in_specs=[pl.BlockSpec((tm, d), lambda i, j: (i, 0)), wrow, wrow],
        out_specs=[hid, hid, hid], out_shape=[jax.ShapeDtypeStruct((t, f), BF16)] * 3,
        sem=("parallel", "parallel"), args=(h, wg, wu))


def _ffn_bwd_mid(dy, wd, g, u, name):
    t, d = dy.shape
    f = wd.shape[0]
    tm, tf = _row_tile(t), _ffn_tile(f)

    def body(dy_ref, wd_ref, g_ref, u_ref, dg_ref, du_ref, dwd_ref):
        dy16 = dy_ref[...]
        da = 0.5 * lax.dot_general(dy16, wd_ref[...], NT, preferred_element_type=F32)
        gv = g_ref[...].astype(F32)
        uv = u_ref[...].astype(F32)
        s = _sigmoid(gv)
        silu = gv * s
        dg_ref[...] = (da * uv * (s * (1.0 + gv * (1.0 - s)))).astype(BF16)
        du_ref[...] = (da * silu).astype(BF16)
        part = 0.5 * lax.dot_general((silu * uv).astype(BF16), dy16, (((0,), (0,)), ((), ())),
                                     preferred_element_type=F32)
        _accumulate(dwd_ref, part, pl.program_id(1))

    hid = pl.BlockSpec((tm, tf), lambda j, i: (i, j))
    wrow = pl.BlockSpec((tf, d), lambda j, i: (j, 0))
    return _pallas(
        body, name=name, grid=(f // tf, t // tm),
        in_specs=[pl.BlockSpec((tm, d), lambda j, i: (i, 0)), wrow, hid, hid],
        out_specs=[hid, hid, wrow],
        out_shape=[jax.ShapeDtypeStruct((t, f), BF16)] * 2 + [jax.ShapeDtypeStruct((f, d), F32)],
        sem=("parallel", "arbitrary"), args=(dy, wd, g, u))


def _rel_index(t, s_band):
    return jnp.clip(t + KPAD - s_band, -REL_CLIP, REL_CLIP) + REL_CLIP


def _bias_expand(rel_bias_pad):
    nh = rel_bias_pad.shape[0]

    def body(rb_ref, out_ref):
        rb = rb_ref[...]
        i_io = lax.broadcasted_iota(jnp.int32, (N_REL_PAD, BAND), 0)
        s_io = lax.broadcasted_iota(jnp.int32, (N_REL_PAD, BAND), 1)

        def row(r, carry):
            onehot = (i_io == _rel_index(pl.program_id(0) * rows + r, s_io)).astype(F32)
            out_ref[r] = _dot_exact01_r(rb, onehot)
            return carry

        lax.fori_loop(0, rows, row, 0)

    rows = 8
    return _pallas(
        body, name="bias_expand", grid=(CHUNK // rows,),
        in_specs=[pl.BlockSpec(rel_bias_pad.shape, lambda i: (0, 0))],
        out_specs=[pl.BlockSpec((rows, nh, BAND), lambda i: (i, 0, 0))],
        out_shape=[jax.ShapeDtypeStruct((CHUNK, nh, BAND), F32)], sem=("arbitrary",), args=(rel_bias_pad,))[0]


def _bias_fold(dbias):
    ng, nh = dbias.shape[0], dbias.shape[2]

    def body(db_ref, out_ref):
        s_io = lax.broadcasted_iota(jnp.int32, (BAND, N_REL_PAD), 0)
        i_io = lax.broadcasted_iota(jnp.int32, (BAND, N_REL_PAD), 1)

        def row(t, acc):
            onehot = (i_io == _rel_index(t, s_io)).astype(F32)
            d = db_ref[0, t]
            for gi in range(1, ng):
                d = d + db_ref[gi, t]
            return acc + _dot_exact01_r(d, onehot)

        out_ref[...] = lax.fori_loop(0, CHUNK, row, jnp.zeros((nh, N_REL_PAD), F32))

    return _pallas(
        body, name="bias_fold", grid=(1,), in_specs=[pl.BlockSpec(dbias.shape, lambda i: (0, 0, 0, 0))],
        out_specs=[pl.BlockSpec((nh, N_REL_PAD), lambda i: (0, 0))],
        out_shape=[jax.ShapeDtypeStruct((nh, N_REL_PAD), F32)], sem=("arbitrary",), args=(dbias,))[0]


def _left_half(shape):
    return lax.broadcasted_iota(jnp.int32, shape, len(shape) - 1) < ATTN_HEAD_DIM


def _stack_heads(v):
    left = _left_half(v.shape)
    zero = jnp.zeros_like(v)
    return jnp.concatenate([jnp.where(left, v, zero), jnp.where(left, zero, v)], axis=0)


def _unstack_heads(v):
    return jnp.where(_left_half((CHUNK, 128)), v[0:CHUNK, :], v[CHUNK:2 * CHUNK, :])


def _half_mean(v):
    r = lax.broadcasted_iota(jnp.int32, (128, 128), 0) < ATTN_HEAD_DIM
    c = lax.broadcasted_iota(jnp.int32, (128, 128), 1) < ATTN_HEAD_DIM
    return _dot_exact01_r(v, r == c) * (1.0 / ATTN_HEAD_DIM)


def _attn_prepare(q_ref, k_ref, v_ref, gq_ref, gk_ref, qs_scr, k_scr, v_scr):
    seq = q_ref.shape[0]
    rb = min(ATTN_PREP_ROWS, seq)
    k_scr[0:KPAD, :] = jnp.zeros((KPAD, 128), BF16)
    v_scr[0:KPAD, :] = jnp.zeros((KPAD, 128), BF16)

    def block(i, carry):
        rows = pl.ds(pl.multiple_of(i * rb, rb), rb)
        behind = pl.ds(pl.multiple_of(KPAD + i * rb, CHUNK), rb)
        q, k = q_ref[rows, :], k_ref[rows, :]
        rq = lax.rsqrt(_half_mean(q * q) + RMS_EPS)
        rk = lax.rsqrt(_half_mean(k * k) + RMS_EPS)
        qs_scr[rows, :] = (q * rq * gq_ref[...] * ATTN_HEAD_DIM ** -0.5).astype(BF16)
        k_scr[behind, :] = (k * rk * gk_ref[...]).astype(BF16)
        v_scr[behind, :] = v_ref[rows, :].astype(BF16)
        return carry

    lax.fori_loop(0, seq // rb, block, 0)


def _first_key(c):
    return jnp.maximum(CHUNK, (LEFT_CHUNKS + 1 - c) * CHUNK)


def _attn_fwd_chunk(c, qs_scr, k_scr, v_scr, bias_ref, o_ref):
    r0 = pl.multiple_of(c * CHUNK, CHUNK)
    s = lax.dot_general(_stack_heads(qs_scr[pl.ds(r0, CHUNK), :]), k_scr[pl.ds(r0, BAND), :], NT,
                        preferred_element_type=F32)
    yield
    col = lax.broadcasted_iota(jnp.int32, (2 * CHUNK, BAND), 1)
    s = jnp.where(col >= _first_key(c), s + bias_ref[...], -jnp.inf)
    m = jnp.max(s, axis=-1, keepdims=True)
    yield
    e = jnp.exp(s - m)
    yield
    inv = 1.0 / jnp.sum(e, axis=-1, keepdims=True)
    o = lax.dot_general(e.astype(BF16), v_scr[pl.ds(r0, BAND), :], NN, preferred_element_type=F32)
    yield
    o_ref[pl.ds(r0, CHUNK), :] = _unstack_heads(o * inv)


def _attn_bwd(proj, out, dout, bias, gq, gk, nb, seq):
    nc = seq // CHUNK
    lock = min(ATTN_LOCKSTEP, nc)
    assert nc % lock == 0
    scale = ATTN_HEAD_DIM ** -0.5

    def body(q_ref, k_ref, v_ref, o_ref, do_ref, bias_ref, gq_ref, gk_ref,
             dq_ref, dk_ref, dv_ref, dbias_ref, dgq_ref, dgk_ref,
             qs_scr, k_scr, v_scr, dqn_scr, dk_scr, dv_scr, db_scr):
        _attn_prepare(q_ref, k_ref, v_ref, gq_ref, gk_ref, qs_scr, k_scr, v_scr)
        dk_scr[...] = jnp.zeros_like(dk_scr)
        dv_scr[...] = jnp.zeros_like(dv_scr)
        db_scr[...] = jnp.zeros_like(db_scr)

        def one_chunk(c):
            r0 = pl.multiple_of(c * CHUNK, CHUNK)
            qst = _stack_heads(qs_scr[pl.ds(r0, CHUNK), :])
            kb = k_scr[pl.ds(r0, BAND), :]
            vb = v_scr[pl.ds(r0, BAND), :]
            st = lax.dot_general(kb, qst, NT, preferred_element_type=F32) + bias_ref[...]
            dost = _stack_heads(do_ref[pl.ds(r0, CHUNK), :])
            dost16 = dost.astype(BF16)
            dpt = lax.dot_general(vb, dost16, NT, preferred_element_type=F32)
            yield
            key = lax.broadcasted_iota(jnp.int32, (BAND, 2 * CHUNK), 0)
            st = jnp.where(key >= _first_key(c), st, -jnp.inf)
            mx = jnp.max(st, axis=0, keepdims=True)
            drow = _row_sums_on_lanes(dost * _stack_heads(o_ref[pl.ds(r0, CHUNK), :]))
            yield
            et = jnp.exp(st - mx)
            yield
            pt = et * (1.0 / jnp.sum(et, axis=0, keepdims=True))
            yield
            dst = pt * (dpt - drow)
            dst16 = dst.astype(BF16)
            yield
            db_scr[...] += dst
            dqn_scr[pl.ds(r0, CHUNK), :] = scale * _unstack_heads(_dot(dst.T, kb))
            yield
            dk_scr[pl.ds(r0, BAND), :] += lax.dot_general(dst16, qst, NN, preferred_element_type=F32)
            yield
            dv_scr[pl.ds(r0, BAND), :] += lax.dot_general(pt.astype(BF16), dost16, NN, preferred_element_type=F32)

        def chunk(i, carry):
            _lockstep([one_chunk(i * lock + a) for a in range(lock)])
            return carry

        lax.fori_loop(0, nc // lock, chunk, 0, unroll=max(1, min(ATTN_UNROLL, nc) // lock))

        def norm_bwd(dn, x, g_ref):
            r = lax.rsqrt(_half_mean(x * x) + RMS_EPS)
            hat = x * r
            gd = dn * g_ref[...]
            return r * (gd - hat * _half_mean(gd * hat)), jnp.sum(dn * hat, axis=0, keepdims=True)

        rb = min(ATTN_PREP_ROWS, seq)

        def block(i, carry):
            rows = pl.ds(pl.multiple_of(i * rb, rb), rb)
            behind = pl.ds(pl.multiple_of(KPAD + i * rb, CHUNK), rb)
            dq, dgq = norm_bwd(dqn_scr[rows, :], q_ref[rows, :], gq_ref)
            dk, dgk = norm_bwd(dk_scr[behind, :], k_ref[rows, :], gk_ref)
            dq_ref[rows, :] = dq.astype(BF16)
            dk_ref[rows, :] = dk.astype(BF16)
            dv_ref[rows, :] = dv_scr[behind, :].astype(BF16)
            return carry[0] + dgq, carry[1] + dgk

        zero = jnp.zeros((1, 128), F32)
        dgq, dgk = lax.fori_loop(0, seq // rb, block, (zero, zero))
        dbias_ref[0] = db_scr[...]
        dgq_ref[0] = dgq
        dgk_ref[0] = dgk

    def col(off):
        return pl.BlockSpec((seq, 128), lambda b, hp: (b, off + hp))

    vec = pl.BlockSpec((1, 128), lambda b, hp: (0, 0))
    gvec = pl.BlockSpec((1, 1, 128), lambda b, hp: (b * (ATTN_HEADS // 2) + hp, 0, 0))
    t = nb * seq
    return _pallas(
        body, name="attn_bwd", grid=(nb, ATTN_HEADS // 2),
        in_specs=[col(0), col(4), col(8), col(0), col(0),
                  pl.BlockSpec((BAND, 2 * CHUNK), lambda b, hp: (hp, 0)), vec, vec],
        out_specs=[col(0), col(0), col(0), pl.BlockSpec((1, BAND, 2 * CHUNK), lambda b, hp: (b, hp, 0)),
                   gvec, gvec],
        out_shape=[jax.ShapeDtypeStruct((t, ATTN_WIDTH), BF16)] * 3
        + [jax.ShapeDtypeStruct((nb, ATTN_HEADS // 2 * BAND, 2 * CHUNK), F32)]
        + [jax.ShapeDtypeStruct((nb * ATTN_HEADS // 2, 1, 128), F32)] * 2,
        scratch_shapes=[pltpu.VMEM((seq, 128), BF16), pltpu.VMEM((seq + KPAD, 128), BF16),
                        pltpu.VMEM((seq + KPAD, 128), BF16), pltpu.VMEM((seq, 128), F32),
                        pltpu.VMEM((seq + KPAD, 128), F32), pltpu.VMEM((seq + KPAD, 128), F32),
                        pltpu.VMEM((BAND, 2 * CHUNK), F32)],
        sem=("parallel", "parallel"), args=(proj, proj, proj, out, dout, bias, gq, gk))


def _tri(lower):
    r = lax.broadcasted_iota(jnp.int32, (CHUNK, CHUNK), 0)
    c = lax.broadcasted_iota(jnp.int32, (CHUNK, CHUNK), 1)
    return (r >= c) if lower else (r <= c)


def _hgrn_gates(hq, hf, lb):
    sq = _sigmoid(hq)
    sf = _sigmoid(hf)
    return hq * sq, sq, sf, lb + (1.0 - lb) * sf


def _hgrn_offdiag(q_s, k_s, b_s):
    row = lax.broadcasted_iota(jnp.int32, (CHUNK, HGRN_HEAD_DIM), 0)
    bv, qv, kv = b_s[...], q_s[...], k_s[...]
    eqs, eks = [], []
    for i in range(1, N_SUB):
        r = b_s[pl.ds(SUB * i - 1, 1), :]
        in_i = (row >= SUB * i) & (row < SUB * (i + 1))
        eqs.append(jnp.exp(jnp.where(in_i, bv - r, -jnp.inf)))
        eks.append(jnp.exp(jnp.where(row < SUB * i, r - bv, -jnp.inf)))
    eq = jnp.concatenate(eqs, axis=1)
    ek = jnp.concatenate(eks, axis=1)
    qt = jnp.concatenate([qv] * (N_SUB - 1), axis=1) * eq
    kt = jnp.concatenate([kv] * (N_SUB - 1), axis=1) * ek
    return qt, kt, eq, ek


def _hgrn_diag_e(b_s, i, s):
    t_io = lax.broadcasted_iota(jnp.int32, (SUB, HGRN_HEAD_DIM), 0)
    bi = b_s[pl.ds(SUB * i, SUB), :]
    return jnp.exp(jnp.where(t_io >= s, bi - b_s[pl.ds(SUB * i + s, 1), :], -jnp.inf)), t_io


def _hgrn_intra(q_s, k_s, b_s, a_s, qt, kt):
    ktp = jnp.concatenate([kt, jnp.zeros_like(kt)], axis=0)
    a_s[...] = _dot(qt, ktp, NT)
    yield
    col = lax.broadcasted_iota(jnp.int32, (SUB, HGRN_HEAD_DIM), 1)
    for i in range(N_SUB):
        qi = q_s[pl.ds(SUB * i, SUB), :]
        ai = jnp.zeros((SUB, HGRN_HEAD_DIM), F32)
        for s in range(SUB):
            e, _ = _hgrn_diag_e(b_s, i, s)
            a_col = jnp.sum(qi * k_s[pl.ds(SUB * i + s, 1), :] * e, axis=-1, keepdims=True)
            ai = ai + jnp.where(col == SUB * i + s, a_col, 0.0)
            if s % DIAG_STAGE == DIAG_STAGE - 1:
                yield
        a_s[pl.ds(SUB * i, SUB), :] += ai


def _mixer_fwd(proj, bias, gq, gk, lb, go, nb, seq):
    nc = seq // CHUNK
    hd = HGRN_HEAD_DIM
    nblk = ATTN_HEADS // 2
    rows_blk = seq // nblk
    nck = rows_blk // CHUNK
    per = nc // nck
    assert rows_blk % CHUNK == 0

    def body(aq_ref, ak_ref, av_ref, bias_ref, gq_ref, gk_ref, hq_ref, hf_ref, hi_ref, hg_ref, lb_ref, go_ref,
             ao_ref, y_ref, o_ref, st_ref, a_ref, qs_scr, k_scr, v_scr, st_all, q_all, k_all, b_all, a_all):
        _attn_prepare(aq_ref, ak_ref, av_ref, gq_ref, gk_ref, qs_scr, k_scr, v_scr)

        @pl.when(pl.program_id(1) == 0)
        def _():
            st_all[...] = jnp.zeros_like(st_all)

        lower = _tri(True)

        def head_chunk(hh, c, rows):
            ln = slice(hd * hh, hd * (hh + 1))
            st, q_s, k_s, b_s, a_s = st_all.at[hh], q_all.at[hh], k_all.at[hh], b_all.at[hh], a_all.at[hh]
            q, _, _, f = _hgrn_gates(hq_ref[rows, ln], hf_ref[rows, ln], lb_ref[:, ln])
            v = hi_ref[rows, ln]
            yield
            b = _dot_exact01(lower, jnp.log(f))
            q_s[...] = q
            k_s[...] = 1.0 - f
            b_s[...] = b
            st_ref[hh, c] = st[...]
            yield
            qt, kt, _, _ = _hgrn_offdiag(q_s, k_s, b_s)
            yield
            yield from _hgrn_intra(q_s, k_s, b_s, a_s, qt, kt)
            a16 = a_s[...].astype(BF16)
            a_ref[hh, c] = a16
            vp = jnp.concatenate([v, jnp.zeros_like(v)], axis=0)
            o = _dot(a16, vp) + _dot(q * jnp.exp(b), st[...], NT)
            yield
            bl = b_s[pl.ds(CHUNK - 1, 1), :]
            st[...] = st[...] * jnp.exp(bl) + _tn(v, (1.0 - f) * jnp.exp(bl - b))
            o_ref[rows, ln] = o
            yield
            n = o * lax.rsqrt(jnp.mean(o * o, axis=-1, keepdims=True) + RMS_EPS) * go_ref[...]
            hg = hg_ref[rows, ln]
            y_ref[rows, ln] = n * hg * _sigmoid(hg)

        def chunk(c, carry):
            rows = pl.ds(pl.multiple_of(c * CHUNK, CHUNK), CHUNK)
            _lockstep([_attn_fwd_chunk(c * per + a, qs_scr, k_scr, v_scr, bias_ref, ao_ref) for a in range(per)]
                      + [head_chunk(hh, c, rows) for hh in range(HGRN_HEADS)])
            return carry

        lax.fori_loop(0, nck, chunk, 0, unroll=min(8, nck))

    hp, wide = HGRN_HEADS, HGRN_HEADS * hd

    def acol(off):
        return pl.BlockSpec((seq, 128), lambda b, s: (b, off + s))

    def col(off):
        return pl.BlockSpec((rows_blk, wide), lambda b, s: (b * nblk + s, off // hp))

    out = pl.BlockSpec((rows_blk, wide), lambda b, s: (b * nblk + s, 0))
    vec = pl.BlockSpec((1, 128), lambda b, s: (0, 0))
    t = nb * seq
    return _pallas(
        body, name="mixer_fwd", grid=(nb, nblk),
        in_specs=[acol(0), acol(4), acol(8), pl.BlockSpec((2 * CHUNK, BAND), lambda b, s: (s, 0)), vec, vec,
                  col(12), col(16), col(20), col(24), pl.BlockSpec((1, wide), lambda b, s: (0, 0)), vec],
        out_specs=[pl.BlockSpec((seq, 128), lambda b, s: (b, s)), out, out,
                   pl.BlockSpec((hp, nck, hd, hd), lambda b, s: (b, s, 0, 0)),
                   pl.BlockSpec((hp, nck, CHUNK, hd), lambda b, s: (b, s, 0, 0))],
        out_shape=[jax.ShapeDtypeStruct((t, ATTN_WIDTH), F32)] + [jax.ShapeDtypeStruct((t, wide), F32)] * 2
        + [jax.ShapeDtypeStruct((nb * hp, nc, hd, hd), F32), jax.ShapeDtypeStruct((nb * hp, nc, CHUNK, hd), BF16)],
        scratch_shapes=[pltpu.VMEM((seq, 128), BF16), pltpu.VMEM((seq + KPAD, 128), BF16),
                        pltpu.VMEM((seq + KPAD, 128), BF16), pltpu.VMEM((hp, hd, hd), F32)]
        + [pltpu.VMEM((hp, CHUNK, hd), F32)] * 4,
        sem=("parallel", "arbitrary"), args=(proj,) * 3 + (bias, gq, gk) + (proj,) * 4 + (lb, go))


def _hgrn_bwd(proj, lb, go, o_pre, states, scores, dout, nb, seq):
    nc = seq // CHUNK
    hd = HGRN_HEAD_DIM
    rows_blk = min(HGRN_ROWS, seq)
    nblk, nck = seq // rows_blk, rows_blk // CHUNK

    def body(hq_ref, hf_ref, hi_ref, hg_ref, lb_ref, go_ref, o_ref, st_ref, a_ref, dy_ref,
             dhq_ref, dhf_ref, dhi_ref, dhg_ref, dlb_ref, dgo_ref,
             dst_all, q_all, k_all, b_all, da_all, dqi_all, dki_all, dlb_all, dgo_all):
        @pl.when(pl.program_id(1) == 0)
        def _():
            dst_all[...] = jnp.zeros_like(dst_all)
            dlb_all[...] = jnp.zeros_like(dlb_all)
            dgo_all[...] = jnp.zeros_like(dgo_all)

        lower, upper = _tri(True), _tri(False)
        gov = go_ref[...]
        row = lax.broadcasted_iota(jnp.int32, (CHUNK, hd), 0)

        def head_chunk(hh, c, rows):
            ln = slice(hd * hh, hd * (hh + 1))
            dst, q_s, k_s, b_s = dst_all.at[hh], q_all.at[hh], k_all.at[hh], b_all.at[hh]
            da_s, dqi_s, dki_s = da_all.at[hh], dqi_all.at[hh], dki_all.at[hh]
            dlb_acc, dgo_acc = dlb_all.at[hh], dgo_all.at[hh]
            lbv = lb_ref[:, ln]
            hq, hf, v, hg = hq_ref[rows, ln], hf_ref[rows, ln], hi_ref[rows, ln], hg_ref[rows, ln]
            q, sq, sf, f = _hgrn_gates(hq, hf, lbv)
            kk = 1.0 - f
            yield
            b = _dot_exact01(lower, jnp.log(f))
            q_s[...] = q
            k_s[...] = kk
            b_s[...] = b
            yield
            bl = b_s[pl.ds(CHUNK - 1, 1), :]
            ebl = jnp.exp(bl)
            ekd = jnp.exp(bl - b)
            kd = kk * ekd
            eb = jnp.exp(b)
            qb = q * eb
            st0 = st_ref[hh, c]
            dst1 = dst[...]
            yield

            o = o_ref[rows, ln]
            dy = dy_ref[rows, ln]
            sg = _sigmoid(hg)
            rstd = lax.rsqrt(jnp.mean(o * o, axis=-1, keepdims=True) + RMS_EPS)
            ohat = o * rstd
            dn = dy * hg * sg
            dhg_ref[rows, ln] = (dy * ohat * gov * (sg * (1.0 + hg * (1.0 - sg)))).astype(BF16)
            dgo_acc[...] += jnp.sum(dn * ohat, axis=0, keepdims=True)
            gdn = dn * gov
            do = rstd * (gdn - ohat * jnp.mean(gdn * ohat, axis=-1, keepdims=True))
            yield

            qt, kt, eq, ek = _hgrn_offdiag(q_s, k_s, b_s)
            da = _dot(do, v, NT)
            dat = _dot(v, do, NT)
            da_s[...] = da
            yield
            dqo = _dot(da, kt) * eq
            dko = _dot(dat, qt) * ek
            dqi_s[...] = sum(dqo[:, j * hd:(j + 1) * hd] for j in range(N_SUB - 1))
            dki_s[...] = sum(dko[:, j * hd:(j + 1) * hd] for j in range(N_SUB - 1))
            yield
            col = lax.broadcasted_iota(jnp.int32, (SUB, CHUNK), 1)
            for i in range(N_SUB):
                qi = q_s[pl.ds(SUB * i, SUB), :]
                dai = da_s[pl.ds(SUB * i, SUB), :]
                dqd = jnp.zeros((SUB, hd), F32)
                for s in range(SUB):
                    e, _ = _hgrn_diag_e(b_s, i, s)
                    dacol = jnp.sum(jnp.where(col == SUB * i + s, dai, 0.0), axis=-1, keepdims=True)
                    w = dacol * e
                    dqd = dqd + w * k_s[pl.ds(SUB * i + s, 1), :]
                    dki_s[pl.ds(SUB * i + s, 1), :] += jnp.sum(w * qi, axis=0, keepdims=True)
                    if s % DIAG_STAGE == DIAG_STAGE - 1:
                        yield
                dqi_s[pl.ds(SUB * i, SUB), :] += dqd
            dqi, dki = dqi_s[...], dki_s[...]

            dv = _tn(a_ref[hh, c].astype(F32), do)[0:CHUNK, :] + _dot(kd, dst1, NT)
            dqb = _dot(do, st0)
            dkd = _dot(v, dst1)
            yield
            t2 = dkd * kd
            dq = dqb * eb + dqi
            dk = dkd * ekd + dki
            dbl = jnp.sum(t2, axis=0, keepdims=True) + ebl * jnp.sum(st0 * dst1, axis=0, keepdims=True)
            db = dqb * qb - t2 + q * dqi - kk * dki + jnp.where(row == CHUNK - 1, dbl, 0.0)
            yield
            dg = _dot_exact01(upper, db)
            dst[...] = dst1 * ebl + _tn(do, qb)
            yield

            df = dg / f - dk
            dhf_ref[rows, ln] = (df * (1.0 - lbv) * sf * (1.0 - sf)).astype(BF16)
            dlb_acc[...] += jnp.sum(df * (1.0 - sf), axis=0, keepdims=True)
            dhq_ref[rows, ln] = (dq * (sq * (1.0 + hq * (1.0 - sq)))).astype(BF16)
            dhi_ref[rows, ln] = dv.astype(BF16)

        def chunk(it, carry):
            c = nck - 1 - it
            rows = pl.ds(pl.multiple_of(c * CHUNK, CHUNK), CHUNK)
            _lockstep([head_chunk(hh, c, rows) for hh in range(HGRN_HEADS)])
            return carry

        lax.fori_loop(0, nck, chunk, 0, unroll=min(2, nck))

        @pl.when(pl.program_id(1) == nblk - 1)
        def _():
            dlb_ref[...] = dlb_all[...]
            dgo_ref[...] = dgo_all[...]

    hp, wide = HGRN_HEADS, HGRN_HEADS * hd

    def col(off):
        return pl.BlockSpec((rows_blk, wide), lambda b, s: (b * nblk + nblk - 1 - s, off // hp))

    out = pl.BlockSpec((rows_blk, wide), lambda b, s: (b * nblk + nblk - 1 - s, 0))
    part = pl.BlockSpec((hp, 1, hd), lambda b, s: (b, 0, 0))
    t = nb * seq
    return pl.pallas_call(
        body, name="hgrn_bwd", grid=(nb, nblk),
        in_specs=[col(12), col(16), col(20), col(24), pl.BlockSpec((1, wide), lambda b, s: (0, 0)),
                  pl.BlockSpec((1, hd), lambda b, s: (0, 0)), out,
                  pl.BlockSpec((hp, nck, hd, hd), lambda b, s: (b, nblk - 1 - s, 0, 0)),
                  pl.BlockSpec((hp, nck, CHUNK, hd), lambda b, s: (b, nblk - 1 - s, 0, 0)), col(4)],
        out_specs=[out, out, out, out, part, part],
        out_shape=[jax.ShapeDtypeStruct((t, wide), BF16)] * 4 + [jax.ShapeDtypeStruct((nb * hp, 1, hd), F32)] * 2,
        scratch_shapes=[pltpu.VMEM((hp, hd, hd), F32)] + [pltpu.VMEM((hp, CHUNK, hd), F32)] * 3
        + [pltpu.VMEM((hp, CHUNK, CHUNK), F32)] + [pltpu.VMEM((hp, CHUNK, hd), F32)] * 2
        + [pltpu.VMEM((hp, 1, hd), F32)] * 2,
        compiler_params=_params("parallel", "arbitrary"),
    )(proj, proj, proj, proj, lb, go, o_pre, states, scores, dout)


def _lb_fwd(lower_bounds):
    def body(x_ref, o_ref):
        xv = x_ref[...]
        e = jnp.exp(xv - jnp.max(xv, axis=0, keepdims=True))
        o_ref[...] = e[0:1, :] / jnp.sum(e, axis=0, keepdims=True)

    return pl.pallas_call(body, name="lb_fwd",
                          out_shape=jax.ShapeDtypeStruct((1, lower_bounds.shape[1]), F32))(lower_bounds)


def _lb_bwd(lower_bounds, dlb_parts):
    ng = dlb_parts.shape[0]

    def body(x_ref, d_ref, o_ref):
        xv = x_ref[...]
        e = jnp.exp(xv - jnp.max(xv, axis=0, keepdims=True))
        p = e / jnp.sum(e, axis=0, keepdims=True)
        dlb = d_ref[0]
        for gi in range(1, ng):
            dlb = dlb + d_ref[gi]
        first = lax.broadcasted_iota(jnp.int32, xv.shape, 0) == 0
        o_ref[...] = p * (jnp.where(first, dlb, 0.0) - p[0:1, :] * dlb)

    return pl.pallas_call(body, name="lb_bwd",
                          out_shape=jax.ShapeDtypeStruct(lower_bounds.shape, F32))(lower_bounds, dlb_parts)


def _ffn_bwd(x, g, h, gate, up, dy, dy16, w, put, tag):
    wg, wu, wd = w[tag + "_w_gate"], w[tag + "_w_up"], w[tag + "_w_down"]
    dgate, dup, dwd = _ffn_bwd_mid(dy16, wd, gate, up, tag + "_bwd_mid")
    put(tag + "_w_down", dwd)
    put(tag + "_w_gate", _mm(dgate, h, ta=True, tm=1408, tn=512, name=tag + "_dwg"))
    put(tag + "_w_up", _mm(dup, h, ta=True, tm=1408, tn=512, name=tag + "_dwu"))
    dh = _mm(dgate, wg, tm=512, tn=1024, name=tag + "_dh_gate")
    return _mm(dup, wu, tm=512, tn=1024, add=dh, norm_bwd=(x, g, dy), name=tag + "_dh_up")


def _local_step(x, tgt, sp, w, put, nb, seq):
    d = x.shape[1]
    h1 = _rms_fwd(x, sp["ffn1_norm_g"], "ffn1_norm")
    rb_pad = jnp.pad(sp["attn_rel_bias"], ((0, 0), (0, N_REL_PAD - N_REL)))
    bias = jnp.transpose(_bias_expand(rb_pad), (1, 0, 2)).reshape(ATTN_HEADS * CHUNK, BAND)
    gq2 = jnp.concatenate([sp["attn_q_norm_g"]] * 2, axis=1)
    gk2 = jnp.concatenate([sp["attn_k_norm_g"]] * 2, axis=1)
    lb = _lb_fwd(sp["hgrn_lower_bounds"])
    gate1, up1, act1 = _ffn_up(h1, w["ffn1_w_gate"], w["ffn1_w_up"], "ffn1_up")
    x1, h2 = _mm(act1, w["ffn1_w_down"], tm=512, tn=d, add=x, scale=0.5, norm_g=sp["mix_norm_g"],
                 name="ffn1_down")
    proj = _mm(h2, w["w_in"], tb=True, tm=256, tn=w["w_in"].shape[0], name="in_proj")
    attn, hy, ho, hstate, hscore = _mixer_fwd(proj, bias, gq2, gk2, lb, sp["hgrn_out_norm_g"], nb, seq)
    mix = jnp.concatenate([attn, hy], axis=1)
    x2, h3 = _mm(mix, w["w_out"], tm=512, tn=1024, add=x1, norm_g=sp["ffn2_norm_g"], name="out_proj")
    gate2, up2, dx3, dx3_16, sq = _ffn_fwd(h3, x2, w["ffn2_w_gate"], w["ffn2_w_up"], w["ffn2_w_down"], "ffn2_fwd",
                                           tgt=tgt)
    loss = 0.5 * jnp.sum(sq) / d

    dx2, dx2_16, dg3 = _ffn_bwd(x2, sp["ffn2_norm_g"], h3, gate2, up2, dx3, dx3_16, w, put, "ffn2")
    dmix = _mm(dx2_16, w["w_out"], tb=True, tm=512, tn=1024, name="out_proj_dx")
    put("w_out", _mm(mix, dx2_16, ta=True, tm=512, tn=1024, name="out_proj_dw"))
    bias_t = jnp.transpose(bias.reshape(ATTN_HEADS // 2, 2 * CHUNK, BAND), (0, 2, 1)).reshape(-1, 2 * CHUNK)
    dq, dk, dv, dbias, dgq, dgk = _attn_bwd(proj, attn, dmix, bias_t, gq2, gk2, nb, seq)
    dbias = jnp.transpose(dbias.reshape(nb, ATTN_HEADS // 2, BAND, 2, CHUNK), (0, 4, 1, 3, 2))
    dbias = dbias.reshape(nb, CHUNK, ATTN_HEADS, BAND)
    dgq = jnp.sum(dgq, axis=(0, 1)).reshape(2, ATTN_HEAD_DIM).sum(axis=0, keepdims=True)
    dgk = jnp.sum(dgk, axis=(0, 1)).reshape(2, ATTN_HEAD_DIM).sum(axis=0, keepdims=True)
    dhq, dhf, dhi, dhg, dlb, dgo = _hgrn_bwd(proj, lb, sp["hgrn_out_norm_g"], ho, hstate, hscore, dmix, nb, seq)
    dproj = jnp.concatenate([dq, dk, dv, dhq, dhf, dhi, dhg], axis=1)
    put("w_in", _mm(dproj, h2, ta=True, tm=512, tn=1024, name="in_proj_dw"))
    dx1, dx1_16, dgm = _mm(dproj, w["w_in"], tm=512, tn=1024, norm_bwd=(x1, sp["mix_norm_g"], dx2),
                           name="in_proj_dx")
    dx0, _, dg1 = _ffn_bwd(x, sp["ffn1_norm_g"], h1, gate1, up1, dx1, dx1_16, w, put, "ffn1")

    small = {
        "ffn1_norm_g": dg1, "mix_norm_g": dgm, "ffn2_norm_g": dg3,
        "attn_q_norm_g": dgq, "attn_k_norm_g": dgk,
        "attn_rel_bias": _bias_fold(dbias)[:, :N_REL],
        "hgrn_lower_bounds": _lb_bwd(sp["hgrn_lower_bounds"], dlb.reshape(nb, 1, HGRN_HEADS * HGRN_HEAD_DIM)),
        "hgrn_out_norm_g": jnp.sum(dgo, axis=(0, 1))[None, :],
    }
    return loss, dx0, small


MESH = pl.DeviceIdType.MESH
ANY = pl.BlockSpec(memory_space=pl.ANY)


def _coords():
    return lax.axis_index("x"), lax.axis_index("y"), lax.axis_index("c")


def _other_chips(x, y):
    return [(1 - x, y), (x, 1 - y), (1 - x, 1 - y)]


def _gather_side(shards):
    n = len(shards)

    def copies(ins, outs, sems):
        send_sems, recv_sems, local_sems = sems
        x, y, c = _coords()
        xn, yn, dg = (1 - x, y), (x, 1 - y), (1 - x, 1 - y)

        def copy(i, k, block, to, half=None, src=None):
            bx, by, bc = block
            dst = outs[i].at[4 * bx + 2 * by + bc]
            if half is not None:
                rows = shards[i].shape[0] // 2
                dst = dst.at[pl.ds(half * rows, rows)]
            return pltpu.make_async_remote_copy(
                src_ref=dst if src is None else src, dst_ref=dst, send_sem=send_sems.at[i, k],
                recv_sem=recv_sems.at[i, k], device_id=to, device_id_type=MESH)

        mine = [pltpu.make_async_copy(ins[i], outs[i].at[4 * x + 2 * y + c], local_sems.at[i]) for i in range(n)]
        return copy, mine, (x, y, c), (x, y, 1 - c), xn, yn, dg, c

    def own(copy, i, ins, me, sibling, xn, yn, c):
        return [copy(i, 0, me, sibling, src=ins[i]), copy(i, 1, me, (*xn, c), src=ins[i]),
                copy(i, 2, me, (*yn, c), src=ins[i])]

    def passed_on(copy, i, sibling, xn, yn, c):
        return [copy(i, 3, (*xn, c), sibling), copy(i, 5, (*xn, c), (*yn, c), half=0),
                copy(i, 4, (*yn, c), sibling), copy(i, 6, (*yn, c), (*xn, c), half=1)]

    def diagonal(copy, i, sibling, dg, c):
        return [copy(i, 7, (*dg, c), sibling, half=0), copy(i, 8, (*dg, c), sibling, half=1)]

    def start(ins, outs, sems):
        copy, mine, me, sibling, xn, yn, dg, c = copies(ins, outs, sems)
        for cp in mine + [cp for i in range(n) for cp in own(copy, i, ins, me, sibling, xn, yn, c)]:
            cp.start()

    def middle(ins, outs, sems):
        copy, mine, me, sibling, xn, yn, dg, c = copies(ins, outs, sems)
        for i in range(n):
            fwd_x, relay_x, fwd_y, relay_y = passed_on(copy, i, sibling, xn, yn, c)
            copy(i, 1, (*xn, c), me).wait_recv()
            fwd_x.start()
            relay_x.start()
            copy(i, 2, (*yn, c), me).wait_recv()
            fwd_y.start()
            relay_y.start()

    def finish(ins, outs, sems):
        copy, mine, me, sibling, xn, yn, dg, c = copies(ins, outs, sems)
        for i in range(n):
            top, bottom = diagonal(copy, i, sibling, dg, c)
            copy(i, 5, (*dg, c), me, half=0).wait_recv()
            top.start()
            copy(i, 6, (*dg, c), me, half=1).wait_recv()
            bottom.start()
        for i in range(n):
            copy(i, 0, sibling, me).wait_recv()
            copy(i, 3, (*xn, 1 - c), me).wait_recv()
            copy(i, 4, (*yn, 1 - c), me).wait_recv()
            copy(i, 7, (*dg, 1 - c), me, half=0).wait_recv()
            copy(i, 8, (*dg, 1 - c), me, half=1).wait_recv()
        for i in range(n):
            for cp in (own(copy, i, ins, me, sibling, xn, yn, c) + passed_on(copy, i, sibling, xn, yn, c)
                       + diagonal(copy, i, sibling, dg, c)):
                cp.wait_send()
        for cp in mine:
            cp.wait()

    return _Side(list(shards), [jax.ShapeDtypeStruct((N_DEV,) + s.shape, s.dtype) for s in shards],
                 [pltpu.SemaphoreType.DMA((n, 9)), pltpu.SemaphoreType.DMA((n, 9)), pltpu.SemaphoreType.DMA((n,))],
                 start, finish, middle)


def _pair_side(grads):
    n = len(grads)

    def copies(ins, outs, sems):
        send_sems, recv_sems = sems
        x, y, c = _coords()
        return [pltpu.make_async_remote_copy(
            src_ref=ins[i].at[2 * k + 1 - c], dst_ref=outs[i].at[k], send_sem=send_sems.at[i, k],
            recv_sem=recv_sems.at[i, k], device_id=(x, y, 1 - c), device_id_type=MESH)
            for i in range(n) for k in range(4)]

    def start(ins, outs, sems):
        for cp in copies(ins, outs, sems):
            cp.start()

    def finish(ins, outs, sems):
        for cp in copies(ins, outs, sems):
            cp.wait()

    return _Side(list(grads), [jax.ShapeDtypeStruct((4,) + g.shape[1:], g.dtype) for g in grads],
                 [pltpu.SemaphoreType.DMA((n, 4)), pltpu.SemaphoreType.DMA((n, 4))], start, finish)


def _pair_add(grads, recvs, core, name):
    count = len(grads)

    def body(c_ref, *refs):
        for n in range(count):
            refs[2 * count + n][...] = (refs[2 * n][...] + refs[2 * n + 1][...]).astype(BF16)

    in_specs, out_specs = [], []
    for g in grads:
        blk = (1,) + g.shape[1:]
        in_specs += [pl.BlockSpec(blk, lambda k, c_ref: (2 * k + c_ref[0], 0, 0)),
                     pl.BlockSpec(blk, lambda k, c_ref: (k, 0, 0))]
        out_specs.append(pl.BlockSpec(blk, lambda k, c_ref: (k, 0, 0)))
    out = pl.pallas_call(
        body, name=name,
        grid_spec=pltpu.PrefetchScalarGridSpec(num_scalar_prefetch=1, grid=(4,), in_specs=in_specs,
                                               out_specs=out_specs),
        out_shape=[jax.ShapeDtypeStruct((4,) + g.shape[1:], BF16) for g in grads],
        compiler_params=_params("arbitrary"),
    )(core, *[a for pair in zip(grads, recvs) for a in pair])
    return list(out)


def _chip_side(parts):
    n = len(parts)

    def copies(ins, outs, sems):
        send_sems, recv_sems, local_sems = sems
        x, y, c = _coords()
        chips = _other_chips(x, y)
        mine = [pltpu.make_async_copy(ins[i].at[2 * x + y], outs[i].at[2 * x + y], local_sems.at[i])
                for i in range(n)]
        sent = [pltpu.make_async_remote_copy(
            src_ref=ins[i].at[2 * px + py], dst_ref=outs[i].at[2 * x + y], send_sem=send_sems.at[i, j],
            recv_sem=recv_sems.at[i, j], device_id=(px, py, c), device_id_type=MESH)
            for i in range(n) for j, (px, py) in enumerate(chips)]
        return mine, sent, chips, c

    def start(ins, outs, sems):
        mine, sent, _, _ = copies(ins, outs, sems)
        for cp in mine + sent:
            cp.start()

    def finish(ins, outs, sems):
        mine, sent, chips, c = copies(ins, outs, sems)
        send_sems, recv_sems, _ = sems
        for i in range(n):
            for j, (px, py) in enumerate(chips):
                landed = outs[i].at[2 * px + py]
                pltpu.make_async_remote_copy(
                    src_ref=landed, dst_ref=landed, send_sem=send_sems.at[i, j], recv_sem=recv_sems.at[i, j],
                    device_id=(px, py, c), device_id_type=MESH).wait_recv()
        for cp in sent:
            cp.wait_send()
        for cp in mine:
            cp.wait()

    return _Side(list(parts), [jax.ShapeDtypeStruct(p.shape, p.dtype) for p in parts],
                 [pltpu.SemaphoreType.DMA((n, 3)), pltpu.SemaphoreType.DMA((n, 3)), pltpu.SemaphoreType.DMA((n,))],
                 start, finish)


def _all_reduce_small(vals):
    n = len(vals)

    def body(*refs):
        ins, outs, bufs = refs[:n], refs[n:2 * n], refs[2 * n:3 * n]
        send_sems, recv_sems = refs[3 * n:]
        x, y, c = _coords()
        me = 4 * x + 2 * y + c
        for i in range(n):
            bufs[i][me] = ins[i][...]
        sent, landed = [], []
        for k in range(1, N_DEV):
            px = 1 - x if k & 4 else x
            py = 1 - y if k & 2 else y
            pc = 1 - c if k & 1 else c
            for i in range(n):
                sent.append(pltpu.make_async_remote_copy(
                    src_ref=ins[i], dst_ref=bufs[i].at[me], send_sem=send_sems.at[i, k - 1],
                    recv_sem=recv_sems.at[i, k - 1], device_id=(px, py, pc), device_id_type=MESH))
                landed.append(pltpu.make_async_remote_copy(
                    src_ref=ins[i], dst_ref=bufs[i].at[4 * px + 2 * py + pc], send_sem=send_sems.at[i, k - 1],
                    recv_sem=recv_sems.at[i, k - 1], device_id=(x, y, c), device_id_type=MESH))
        for cp in sent:
            cp.start()
        for cp in landed:
            cp.wait_recv()
        for cp in sent:
            cp.wait_send()
        for i in range(n):
            acc = bufs[i][0]
            for j in range(1, N_DEV):
                acc = acc + bufs[i][j]
            outs[i][...] = acc

    vmem = pl.BlockSpec(memory_space=pltpu.VMEM)
    return pl.pallas_call(
        body, name="small_all_reduce", out_shape=[jax.ShapeDtypeStruct(v.shape, F32) for v in vals],
        in_specs=[vmem] * n, out_specs=[vmem] * n,
        scratch_shapes=[pltpu.VMEM((N_DEV,) + v.shape, F32) for v in vals]
        + [pltpu.SemaphoreType.DMA((n, N_DEV - 1)), pltpu.SemaphoreType.DMA((n, N_DEV - 1))],
    )(*vals)


def _adamw(ws, ms, vs, gs, name):
    count = len(ws)
    parts = ws[0].ndim == 3
    steps = 4 if all(w.shape[-2] % 32 == 0 for w in ws) else 1

    def body(*refs):
        for n in range(count):
            w_ref, m_ref, v_ref, g_ref = refs[4 * n:4 * n + 4]
            go_ref, d_ref, mo_ref, vo_ref = refs[4 * count + 4 * n:4 * count + 4 * n + 4]
            if parts:
                gv = g_ref[0].astype(F32)
                for k in range(1, 4):
                    gv = gv + g_ref[k].astype(F32)
                gv = gv[None]
            else:
                gv = g_ref[...]
            m2 = ADAM_B1 * m_ref[...] + (1.0 - ADAM_B1) * gv
            v2 = ADAM_B2 * v_ref[...] + (1.0 - ADAM_B2) * (gv * gv)
            m_hat = m2 / (1.0 - ADAM_B1 ** ADAM_STEP)
            v_hat = v2 / (1.0 - ADAM_B2 ** ADAM_STEP)
            go_ref[...] = gv
            d_ref[...] = -ADAM_LR * (m_hat / (jnp.sqrt(v_hat) + ADAM_EPS) + ADAM_WD * w_ref[...])
            mo_ref[...] = m2
            vo_ref[...] = v2

    in_specs, out_specs, out_shape = [], [], []
    for w in ws:
        r, cdim = w.shape[-2:]
        if parts:
            row = pl.BlockSpec((1, r // steps, cdim), lambda i: (0, i, 0))
            g_spec = pl.BlockSpec((4, r // steps, cdim), lambda i: (0, i, 0))
        else:
            row = g_spec = pl.BlockSpec((r // steps, cdim), lambda i: (i, 0))
        in_specs += [row, row, row, g_spec]
        out_specs += [row] * 4
        out_shape += [jax.ShapeDtypeStruct(w.shape, F32)] * 4
    args = [a for group in zip(ws, ms, vs, gs) for a in group]
    out = pl.pallas_call(
        body, name=name, grid=(steps,), in_specs=in_specs, out_specs=out_specs, out_shape=out_shape,
        compiler_params=_params("parallel"),
    )(*args)
    return [out[4 * n:4 * n + 4] for n in range(count)]


def _adamw_small(ws, ms, vs, gs):
    count = len(ws)

    def body(*refs):
        for n in range(count):
            w_ref, m_ref, v_ref, g_ref = refs[4 * n:4 * n + 4]
            d_ref, mo_ref, vo_ref = refs[4 * count + 3 * n:4 * count + 3 * n + 3]
            gv = g_ref[...]
            m2 = ADAM_B1 * m_ref[...] + (1.0 - ADAM_B1) * gv
            v2 = ADAM_B2 * v_ref[...] + (1.0 - ADAM_B2) * (gv * gv)
            m_hat = m2 / (1.0 - ADAM_B1 ** ADAM_STEP)
            v_hat = v2 / (1.0 - ADAM_B2 ** ADAM_STEP)
            d_ref[...] = -ADAM_LR * (m_hat / (jnp.sqrt(v_hat) + ADAM_EPS) + ADAM_WD * w_ref[...])
            mo_ref[...] = m2
            vo_ref[...] = v2

    out = pl.pallas_call(
        body, name="small_adamw",
        out_shape=[jax.ShapeDtypeStruct(w.shape, F32) for w in ws for _ in range(3)],
    )(*[a for group in zip(ws, ms, vs, gs) for a in group])
    return [out[3 * n:3 * n + 3] for n in range(count)]


WEIGHTS = ["ffn1_norm_g", "ffn1_w_gate", "ffn1_w_up", "ffn1_w_down", "mix_norm_g", "w_in", "attn_q_norm_g",
           "attn_k_norm_g", "attn_rel_bias", "hgrn_lower_bounds", "hgrn_out_norm_g", "w_out", "ffn2_norm_g",
           "ffn2_w_gate", "ffn2_w_up", "ffn2_w_down"]
COL_SHARDED = ("ffn1_w_gate", "ffn1_w_up", "w_in", "ffn2_w_gate", "ffn2_w_up")
ROW_SHARDED = ("ffn1_w_down", "w_out", "ffn2_w_down")
BIG = [n for n in WEIGHTS if n in COL_SHARDED or n in ROW_SHARDED]
SMALL = [n for n in WEIGHTS if n not in BIG]
FFN2 = ["ffn2_w_down", "ffn2_w_gate", "ffn2_w_up"]
MIXER = ["w_out", "w_in"]

PLAN = {
    "ffn1_norm": [("gather", ["ffn1_w_gate"])],
    "bias_expand": [("gather", ["ffn1_w_up"])],
    "ffn1_up": [("gather", ["ffn1_w_down", "w_out"])],
    "ffn1_down": [("gather", ["w_in"])],
    "mixer_fwd": [("gather", FFN2)],
    "ffn2_dh_gate": [("pair", FFN2)],
    "attn_bwd": [("chip", FFN2)],
    "in_proj_dx": [("pair", MIXER)],
    "ffn1_bwd_mid": [("chip", MIXER)],
    "ffn1_dwg": [("pair", ["ffn1_w_down"])],
    "ffn1_dwu": [("chip", ["ffn1_w_down"]), ("pair", ["ffn1_w_gate"])],
    "ffn1_dh_gate": [("chip", ["ffn1_w_gate"]), ("pair", ["ffn1_w_up"])],
    "bias_fold": [("chip", ["ffn1_w_up"])],
}


def _join_sides(sides):
    def split(refs, counts):
        out, at = [], 0
        for n in counts:
            out.append(refs[at:at + n])
            at += n
        return out

    n_in, n_out, n_sem = ([len(getattr(s, f)) for s in sides] for f in ("ins", "out_shape", "sems"))

    def run(which):
        def go(ins, outs, sems):
            for s, i, o, m in zip(sides, split(ins, n_in), split(outs, n_out), split(sems, n_sem)):
                if getattr(s, which) is not None:
                    getattr(s, which)(i, o, m)
        return go

    return _Side([a for s in sides for a in s.ins], [a for s in sides for a in s.out_shape],
                 [a for s in sides for a in s.sems], run("start"), run("finish"),
                 run("middle") if any(s.middle is not None for s in sides) else None)


class _Schedule:
    def __init__(self, shards):
        self.shards = shards
        self.weights = {}
        self.sliced = {}
        self.partials = {}
        self.reduced = {}

    def put(self, name, grad):
        self.sliced[name] = grad.reshape((N_DEV,) + self.shards[name].shape)

    def side_for(self, call):
        if call not in PLAN:
            return None
        sides = []
        for kind, names in PLAN[call]:
            if kind == "gather":
                sides.append(_gather_side([self.shards[n] for n in names]))
            elif kind == "pair":
                sides.append(_pair_side([self.sliced[n] for n in names]))
            else:
                sides.append(_chip_side([self.partials[n] for n in names]))
        return _join_sides(sides)

    def done(self, call, outs):
        at = 0
        for kind, names in PLAN[call]:
            self.file(kind, names, outs[at:at + len(names)])
            at += len(names)

    def file(self, kind, names, outs):
        if kind == "pair":
            core = lax.axis_index("c").astype(jnp.int32).reshape(1)
            sums = _pair_add([self.sliced[n] for n in names], list(outs), core, names[0] + "_pair_add")
            self.partials.update(dict(zip(names, sums)))
            return
        for n, o in zip(names, outs):
            if kind == "gather":
                self.weights[n] = o.reshape(N_DEV * o.shape[1], o.shape[2])
            else:
                self.reduced[n] = o


def kernel(x, ffn1_norm_g, ffn1_w_gate, ffn1_w_up, ffn1_w_down, mix_norm_g, w_in, attn_q_norm_g, attn_k_norm_g, attn_rel_bias, hgrn_lower_bounds, hgrn_out_norm_g, w_out, ffn2_norm_g, ffn2_w_gate, ffn2_w_up, ffn2_w_down, loss_target, m_ffn1_norm_g, m_ffn1_w_gate, m_ffn1_w_up, m_ffn1_w_down, m_mix_norm_g, m_w_in, m_attn_q_norm_g, m_attn_k_norm_g, m_attn_rel_bias, m_hgrn_lower_bounds, m_hgrn_out_norm_g, m_w_out, m_ffn2_norm_g, m_ffn2_w_gate, m_ffn2_w_up, m_ffn2_w_down, v_ffn1_norm_g, v_ffn1_w_gate, v_ffn1_w_up, v_ffn1_w_down, v_mix_norm_g, v_w_in, v_attn_q_norm_g, v_attn_k_norm_g, v_attn_rel_bias, v_hgrn_lower_bounds, v_hgrn_out_norm_g, v_w_out, v_ffn2_norm_g, v_ffn2_w_gate, v_ffn2_w_up, v_ffn2_w_down):
    wts = dict(zip(WEIGHTS, (ffn1_norm_g, ffn1_w_gate, ffn1_w_up, ffn1_w_down, mix_norm_g, w_in, attn_q_norm_g,
                             attn_k_norm_g, attn_rel_bias, hgrn_lower_bounds, hgrn_out_norm_g, w_out, ffn2_norm_g,
                             ffn2_w_gate, ffn2_w_up, ffn2_w_down)))
    mom = dict(zip(WEIGHTS, (m_ffn1_norm_g, m_ffn1_w_gate, m_ffn1_w_up, m_ffn1_w_down, m_mix_norm_g, m_w_in,
                             m_attn_q_norm_g, m_attn_k_norm_g, m_attn_rel_bias, m_hgrn_lower_bounds,
                             m_hgrn_out_norm_g, m_w_out, m_ffn2_norm_g, m_ffn2_w_gate, m_ffn2_w_up, m_ffn2_w_down)))
    var = dict(zip(WEIGHTS, (v_ffn1_norm_g, v_ffn1_w_gate, v_ffn1_w_up, v_ffn1_w_down, v_mix_norm_g, v_w_in,
                             v_attn_q_norm_g, v_attn_k_norm_g, v_attn_rel_bias, v_hgrn_lower_bounds,
                             v_hgrn_out_norm_g, v_w_out, v_ffn2_norm_g, v_ffn2_w_gate, v_ffn2_w_up, v_ffn2_w_down)))
    nb, seq, d = x.shape
    shapes = {n: wts[n].shape for n in WEIGHTS}

    def rows_first(a, n):
        return jnp.swapaxes(a, 1, 2) if n in COL_SHARDED else a

    sched = _Schedule({n: rows_first(wts[n], n)[0].astype(BF16) for n in BIG})
    sp = {n: wts[n] for n in SMALL}
    sp["attn_rel_bias"] = wts["attn_rel_bias"][0]
    _ACTIVE[0] = sched
    try:
        loss, dx, dsmall = _local_step(x.reshape(nb * seq, d), loss_target.reshape(nb * seq, d), sp,
                                       sched.weights, sched.put, nb, seq)
    finally:
        _ACTIVE[0] = None
    reduced = sched.reduced

    sums = _all_reduce_small([dsmall[n] for n in SMALL] + [jnp.full((1, 128), loss, F32)])
    gsmall = {n: s.reshape(shapes[n]) for n, s in zip(SMALL, sums)}
    loss_total = sums[-1][0, 0]

    grads, deltas, new_m, new_v = {}, {}, {}, {}
    for group, tag in (([n for n in BIG if n not in MIXER], "ffn_adamw"), (MIXER, "mixer_adamw")):
        outs = _adamw([rows_first(wts[n], n) for n in group], [rows_first(mom[n], n) for n in group],
                      [rows_first(var[n], n) for n in group], [reduced[n] for n in group], tag)
        for n, out in zip(group, outs):
            grads[n], deltas[n], new_m[n], new_v[n] = (rows_first(o, n) for o in out)
    outs = _adamw_small([wts[n] for n in SMALL], [mom[n] for n in SMALL], [var[n] for n in SMALL],
                        [gsmall[n] for n in SMALL])
    for n, (delta, m2, v2) in zip(SMALL, outs):
        deltas[n], new_m[n], new_v[n] = delta, m2, v2
    grads.update(gsmall)

    return (loss_total, dx.reshape(nb, seq, d), *[grads[n] for n in WEIGHTS], *[deltas[n] for n in WEIGHTS],
            *[new_m[n] for n in WEIGHTS], *[new_v[n] for n in WEIGHTS])
```
